```python
import jax, jax.numpy as jnp
from jax import lax
import numpy as np

D_MODEL = 1024
BATCH = 8
SEQ = 8192
DEPTH = 4

N_MIXERS = 2
EPS = 1e-6

CHUNK = 128
A_WIDTH = 2 * D_MODEL
A_GROUPS = 8
A_GROUP_DIM = A_WIDTH // A_GROUPS

B_WINDOWS = (2, 4, 8, 16)
B_GROUPS = len(B_WINDOWS)
B_WIDTH = D_MODEL
B_GROUP_DIM = B_WIDTH // B_GROUPS

D_FF = ((8 * D_MODEL + 3 * 256 - 1) // (3 * 256)) * 256

N_A_LAYERS = (DEPTH + 1) // 2
N_B_LAYERS = DEPTH // 2

kernel_name = 'hybrid_gmlp_pool_swiglu_trunk'


def rmsnorm(x, g):
    xf = x.astype(jnp.float32)
    y = xf * lax.rsqrt(jnp.mean(xf * xf, axis=-1, keepdims=True) + EPS)
    return (y * g.astype(jnp.float32)).astype(x.dtype)


def layernorm(x, g, b):
    xf = x.astype(jnp.float32)
    mu = jnp.mean(xf, axis=-1, keepdims=True)
    xc = xf - mu
    var = jnp.mean(xc * xc, axis=-1, keepdims=True)
    y = xc * lax.rsqrt(var + EPS) * g.astype(jnp.float32) + b.astype(jnp.float32)
    return y.astype(x.dtype)


def mixer_a(h, w_in, ln_g, ln_b, w_s, b_s, w_out):
    bsz, s, _ = h.shape
    z = jax.nn.gelu(h @ w_in, approximate=False)
    u, v = jnp.split(z, 2, axis=-1)
    v = layernorm(v, ln_g, ln_b)
    n_chunks = s // CHUNK
    v = v.reshape(bsz, n_chunks, CHUNK, A_GROUPS, A_GROUP_DIM)
    u = u.reshape(bsz, n_chunks, CHUNK, A_GROUPS, A_GROUP_DIM)
    causal = jnp.tril(jnp.ones((CHUNK, CHUNK), dtype=bool))
    w = jnp.where(causal[None], w_s, jnp.zeros_like(w_s))
    sv = jnp.einsum('gts,bnsgd->bntgd', w, v)
    sv = sv + jnp.transpose(b_s)[None, None, :, :, None]
    gated = (u * sv).reshape(bsz, s, A_WIDTH)
    return gated @ w_out


def mixer_b(h, w_in, w_grp, scale, w_out):
    bsz, s, _ = h.shape
    p = h @ w_in
    pf = p.astype(jnp.float32)
    cs = jnp.cumsum(pf, axis=1)
    cs0 = jnp.concatenate([jnp.zeros((bsz, 1, B_WIDTH), jnp.float32), cs], axis=1)
    t = jnp.arange(s)
    pooled = []
    for g, win in enumerate(B_WINDOWS):
        lo, hi = g * B_GROUP_DIM, (g + 1) * B_GROUP_DIM
        c = cs0[..., lo:hi]
        c_pad = jnp.concatenate([jnp.zeros((bsz, win - 1, B_GROUP_DIM), jnp.float32), c], axis=1)
        total = c[:, 1:] - c_pad[:, :s]
        count = jnp.minimum(t + 1, win).astype(jnp.float32)
        pooled.append(total / count[None, :, None] - pf[..., lo:hi])
    pooled = jnp.stack(pooled, axis=2)
    mixed = jnp.einsum('bsgd,gde->bsge', pooled, w_grp.astype(jnp.float32))
    mixed = mixed.reshape(bsz, s, B_WIDTH) * scale.astype(jnp.float32)
    return mixed.astype(h.dtype) @ w_out


def swiglu(h, w_gate, w_up, w_down):
    return (jax.nn.silu(h @ w_gate) * (h @ w_up)) @ w_down


def _fwd_setup_inputs(seed: int = 0) -> dict:
    key = jax.random.key(seed)
    ks = jax.random.split(key, 20)
    f32 = jnp.float32
    d = D_MODEL
    x = jax.random.normal(ks[0], (BATCH, SEQ, d), f32)
    a_w_in = jax.random.normal(ks[1], (N_A_LAYERS, d, 2 * A_WIDTH), f32) * d ** -0.5
    a_ln_g = 1.0 + 0.02 * jax.random.normal(ks[2], (N_A_LAYERS, A_WIDTH), f32)
    a_ln_b = 0.02 * jax.random.normal(ks[3], (N_A_LAYERS, A_WIDTH), f32)
    a_w_s = jnp.tril(jax.random.normal(ks[4], (N_A_LAYERS, A_GROUPS, CHUNK, CHUNK), f32) * CHUNK ** -0.5)
    a_b_s = 1.0 + 0.1 * jax.random.normal(ks[5], (N_A_LAYERS, A_GROUPS, CHUNK), f32)
    a_w_out = jax.random.normal(ks[6], (N_A_LAYERS, A_WIDTH, d), f32) * A_WIDTH ** -0.5
    b_w_in = jax.random.normal(ks[7], (N_B_LAYERS, d, B_WIDTH), f32) * d ** -0.5
    b_w_grp = jax.random.normal(ks[8], (N_B_LAYERS, B_GROUPS, B_GROUP_DIM, B_GROUP_DIM), f32) * B_GROUP_DIM ** -0.5
    b_scale = 1.0 + 0.1 * jax.random.normal(ks[9], (N_B_LAYERS, B_WIDTH), f32)
    b_w_out = jax.random.normal(ks[10], (N_B_LAYERS, B_WIDTH, d), f32) * B_WIDTH ** -0.5
    mix_pre_g = 1.0 + 0.02 * jax.random.normal(ks[11], (DEPTH, d), f32)
    mix_post_g = 1.0 + 0.02 * jax.random.normal(ks[12], (DEPTH, d), f32)
    ffn_pre_g = 1.0 + 0.02 * jax.random.normal(ks[13], (DEPTH, d), f32)
    ffn_post_g = 1.0 + 0.02 * jax.random.normal(ks[14], (DEPTH, d), f32)
    ffn_w_gate = jax.random.normal(ks[15], (DEPTH, d, D_FF), f32) * d ** -0.5
    ffn_w_up = jax.random.normal(ks[16], (DEPTH, d, D_FF), f32) * d ** -0.5
    ffn_w_down = jax.random.normal(ks[17], (DEPTH, D_FF, d), f32) * D_FF ** -0.5
    return {'x': x, 'a_w_in': a_w_in, 'a_ln_g': a_ln_g, 'a_ln_b': a_ln_b,
            'a_w_s': a_w_s, 'a_b_s': a_b_s, 'a_w_out': a_w_out,
            'b_w_in': b_w_in, 'b_w_grp': b_w_grp, 'b_scale': b_scale, 'b_w_out': b_w_out,
            'mix_pre_g': mix_pre_g, 'mix_post_g': mix_post_g,
            'ffn_pre_g': ffn_pre_g, 'ffn_post_g': ffn_post_g,
            'ffn_w_gate': ffn_w_gate, 'ffn_w_up': ffn_w_up, 'ffn_w_down': ffn_w_down}


def _fwd_reference(x, a_w_in, a_ln_g, a_ln_b, a_w_s, a_b_s, a_w_out,
              b_w_in, b_w_grp, b_scale, b_w_out,
              mix_pre_g, mix_post_g, ffn_pre_g, ffn_post_g,
              ffn_w_gate, ffn_w_up, ffn_w_down):
    for i in range(DEPTH):
        j = i // N_MIXERS
        h = rmsnorm(x, mix_pre_g[i])
        if i % N_MIXERS == 0:
            m = mixer_a(h, a_w_in[j], a_ln_g[j], a_ln_b[j], a_w_s[j], a_b_s[j], a_w_out[j])
        else:
            m = mixer_b(h, b_w_in[j], b_w_grp[j], b_scale[j], b_w_out[j])
        x = x + rmsnorm(m, mix_post_g[i])
        h = rmsnorm(x, ffn_pre_g[i])
        f = swiglu(h, ffn_w_gate[i], ffn_w_up[i], ffn_w_down[i])
        x = x + rmsnorm(f, ffn_post_g[i])
    return x


import jax as _jax
import jax.numpy as _jnp

TWIN_FORMAT = 'train_step'
FWD_PARAMS = ['x', 'a_w_in', 'a_ln_g', 'a_ln_b', 'a_w_s', 'a_b_s', 'a_w_out', 'b_w_in', 'b_w_grp', 'b_scale', 'b_w_out', 'mix_pre_g', 'mix_post_g', 'ffn_pre_g', 'ffn_post_g', 'ffn_w_gate', 'ffn_w_up', 'ffn_w_down']
TWIN_WEIGHTS = ['a_w_in', 'a_ln_g', 'a_ln_b', 'a_w_s', 'a_b_s', 'a_w_out', 'b_w_in', 'b_w_grp', 'b_scale', 'b_w_out', 'mix_pre_g', 'mix_post_g', 'ffn_pre_g', 'ffn_post_g', 'ffn_w_gate', 'ffn_w_up', 'ffn_w_down']
TWIN_DIFF_INPUT = 'x'
TWIN_INPUTS = ['x', 'a_w_in', 'a_ln_g', 'a_ln_b', 'a_w_s', 'a_b_s', 'a_w_out', 'b_w_in', 'b_w_grp', 'b_scale', 'b_w_out', 'mix_pre_g', 'mix_post_g', 'ffn_pre_g', 'ffn_post_g', 'ffn_w_gate', 'ffn_w_up', 'ffn_w_down', 'loss_target', 'm_a_w_in', 'm_a_ln_g', 'm_a_ln_b', 'm_a_w_s', 'm_a_b_s', 'm_a_w_out', 'm_b_w_in', 'm_b_w_grp', 'm_b_scale', 'm_b_w_out', 'm_mix_pre_g', 'm_mix_post_g', 'm_ffn_pre_g', 'm_ffn_post_g', 'm_ffn_w_gate', 'm_ffn_w_up', 'm_ffn_w_down', 'v_a_w_in', 'v_a_ln_g', 'v_a_ln_b', 'v_a_w_s', 'v_a_b_s', 'v_a_w_out', 'v_b_w_in', 'v_b_w_grp', 'v_b_scale', 'v_b_w_out', 'v_mix_pre_g', 'v_mix_post_g', 'v_ffn_pre_g', 'v_ffn_post_g', 'v_ffn_w_gate', 'v_ffn_w_up', 'v_ffn_w_down']
TWIN_OUTPUTS = ['loss', 'grad_x', 'grad_a_w_in', 'grad_a_ln_g', 'grad_a_ln_b', 'grad_a_w_s', 'grad_a_b_s', 'grad_a_w_out', 'grad_b_w_in', 'grad_b_w_grp', 'grad_b_scale', 'grad_b_w_out', 'grad_mix_pre_g', 'grad_mix_post_g', 'grad_ffn_pre_g', 'grad_ffn_post_g', 'grad_ffn_w_gate', 'grad_ffn_w_up', 'grad_ffn_w_down', 'delta_a_w_in', 'delta_a_ln_g', 'delta_a_ln_b', 'delta_a_w_s', 'delta_a_b_s', 'delta_a_w_out', 'delta_b_w_in', 'delta_b_w_grp', 'delta_b_scale', 'delta_b_w_out', 'delta_mix_pre_g', 'delta_mix_post_g', 'delta_ffn_pre_g', 'delta_ffn_post_g', 'delta_ffn_w_gate', 'delta_ffn_w_up', 'delta_ffn_w_down', 'new_m_a_w_in', 'new_m_a_ln_g', 'new_m_a_ln_b', 'new_m_a_w_s', 'new_m_a_b_s', 'new_m_a_w_out', 'new_m_b_w_in', 'new_m_b_w_grp', 'new_m_b_scale', 'new_m_b_w_out', 'new_m_mix_pre_g', 'new_m_mix_post_g', 'new_m_ffn_pre_g', 'new_m_ffn_post_g', 'new_m_ffn_w_gate', 'new_m_ffn_w_up', 'new_m_ffn_w_down', 'new_v_a_w_in', 'new_v_a_ln_g', 'new_v_a_ln_b', 'new_v_a_w_s', 'new_v_a_b_s', 'new_v_a_w_out', 'new_v_b_w_in', 'new_v_b_w_grp', 'new_v_b_scale', 'new_v_b_w_out', 'new_v_mix_pre_g', 'new_v_mix_post_g', 'new_v_ffn_pre_g', 'new_v_ffn_post_g', 'new_v_ffn_w_gate', 'new_v_ffn_w_up', 'new_v_ffn_w_down']
TWIN_LEAF_KINDS = {'loss': 'loss', 'grad_x': 'grad_x', 'grad_a_w_in': 'grad_w', 'grad_a_ln_g': 'grad_w', 'grad_a_ln_b': 'grad_w', 'grad_a_w_s': 'grad_w', 'grad_a_b_s': 'grad_w', 'grad_a_w_out': 'grad_w', 'grad_b_w_in': 'grad_w', 'grad_b_w_grp': 'grad_w', 'grad_b_scale': 'grad_w', 'grad_b_w_out': 'grad_w', 'grad_mix_pre_g': 'grad_w', 'grad_mix_post_g': 'grad_w', 'grad_ffn_pre_g': 'grad_w', 'grad_ffn_post_g': 'grad_w', 'grad_ffn_w_gate': 'grad_w', 'grad_ffn_w_up': 'grad_w', 'grad_ffn_w_down': 'grad_w', 'delta_a_w_in': 'delta_w', 'delta_a_ln_g': 'delta_w', 'delta_a_ln_b': 'delta_w', 'delta_a_w_s': 'delta_w', 'delta_a_b_s': 'delta_w', 'delta_a_w_out': 'delta_w', 'delta_b_w_in': 'delta_w', 'delta_b_w_grp': 'delta_w', 'delta_b_scale': 'delta_w', 'delta_b_w_out': 'delta_w', 'delta_mix_pre_g': 'delta_w', 'delta_mix_post_g': 'delta_w', 'delta_ffn_pre_g': 'delta_w', 'delta_ffn_post_g': 'delta_w', 'delta_ffn_w_gate': 'delta_w', 'delta_ffn_w_up': 'delta_w', 'delta_ffn_w_down': 'delta_w', 'new_m_a_w_in': 'new_m', 'new_m_a_ln_g': 'new_m', 'new_m_a_ln_b': 'new_m', 'new_m_a_w_s': 'new_m', 'new_m_a_b_s': 'new_m', 'new_m_a_w_out': 'new_m', 'new_m_b_w_in': 'new_m', 'new_m_b_w_grp': 'new_m', 'new_m_b_scale': 'new_m', 'new_m_b_w_out': 'new_m', 'new_m_mix_pre_g': 'new_m', 'new_m_mix_post_g': 'new_m', 'new_m_ffn_pre_g': 'new_m', 'new_m_ffn_post_g': 'new_m', 'new_m_ffn_w_gate': 'new_m', 'new_m_ffn_w_up': 'new_m', 'new_m_ffn_w_down': 'new_m', 'new_v_a_w_in': 'new_v', 'new_v_a_ln_g': 'new_v', 'new_v_a_ln_b': 'new_v', 'new_v_a_w_s': 'new_v', 'new_v_a_b_s': 'new_v', 'new_v_a_w_out': 'new_v', 'new_v_b_w_in': 'new_v', 'new_v_b_w_grp': 'new_v', 'new_v_b_scale': 'new_v', 'new_v_b_w_out': 'new_v', 'new_v_mix_pre_g': 'new_v', 'new_v_mix_post_g': 'new_v', 'new_v_ffn_pre_g': 'new_v', 'new_v_ffn_post_g': 'new_v', 'new_v_ffn_w_gate': 'new_v', 'new_v_ffn_w_up': 'new_v', 'new_v_ffn_w_down': 'new_v'}


def _forward(args):
    return _fwd_reference(*[args[k] for k in FWD_PARAMS])


def _output_shape():
    def fwd():
        inp = _fwd_setup_inputs(0)
        return _fwd_reference(*[inp[k] for k in FWD_PARAMS])
    out = _jax.eval_shape(fwd)
    return out.shape, out.dtype

N_MICROBATCH = 1
ADAM_LR = 0.001
ADAM_B1 = 0.9
ADAM_B2 = 0.999
ADAM_EPS = 1e-08
ADAM_WD = 0.01
ADAM_STEP = 10
PER_EXAMPLE_BATCH_AXIS = {'x': 0, 'loss_target': 0}
SHARED_INPUTS = []
_WEIGHT_DTYPES = {'a_w_in': _jnp.float32, 'a_ln_g': _jnp.float32, 'a_ln_b': _jnp.float32, 'a_w_s': _jnp.float32, 'a_b_s': _jnp.float32, 'a_w_out': _jnp.float32, 'b_w_in': _jnp.float32, 'b_w_grp': _jnp.float32, 'b_scale': _jnp.float32, 'b_w_out': _jnp.float32, 'mix_pre_g': _jnp.float32, 'mix_post_g': _jnp.float32, 'ffn_pre_g': _jnp.float32, 'ffn_post_g': _jnp.float32, 'ffn_w_gate': _jnp.float32, 'ffn_w_up': _jnp.float32, 'ffn_w_down': _jnp.float32}
MOMENT_SCALE = {'a_w_in': 1.677000e+00, 'a_ln_g': 7.399564e-01, 'a_ln_b': 7.878988e-01, 'a_w_s': 1.030985e+00, 'a_b_s': 1.564064e+00, 'a_w_out': 9.823950e+00, 'b_w_in': 4.260535e+00, 'b_w_grp': 4.408197e+00, 'b_scale': 4.693024e+00, 'b_w_out': 4.550808e+00, 'mix_pre_g': 3.552997e+00, 'mix_post_g': 6.451676e+01, 'ffn_pre_g': 3.710792e+00, 'ffn_post_g': 6.376990e+01, 'ffn_w_gate': 1.178576e+00, 'ffn_w_up': 1.906724e+00, 'ffn_w_down': 3.272167e+00}


def _to_microbatches(a, axis):
    t = _jnp.moveaxis(a, axis, 0)
    t = t.reshape((N_MICROBATCH, t.shape[0] // N_MICROBATCH) + t.shape[1:])
    return _jnp.moveaxis(t, 1, axis + 1)


def setup_inputs(seed: int = 0) -> dict:
    inp = _fwd_setup_inputs(seed)
    key = _jax.random.fold_in(_jax.random.key(seed), 7919)
    shape, _ = _output_shape()
    out = dict(inp)
    out["loss_target"] = _jax.random.normal(_jax.random.fold_in(key, 0), shape, _jnp.float32)
    for i, name in enumerate(TWIN_WEIGHTS):
        w = inp[name].astype(_jnp.float32)
        if MOMENT_SCALE is None:
            s = _jnp.sqrt(_jnp.mean(_jnp.square(w)) + 1e-30)
        else:
            s = MOMENT_SCALE[name]
        km, kv = _jax.random.split(_jax.random.fold_in(key, i + 1))
        out[name] = w
        out["m_" + name] = s * _jax.random.normal(km, w.shape, _jnp.float32)
        out["v_" + name] = (s * s) * _jax.random.uniform(kv, w.shape, _jnp.float32, 0.5, 1.5)
    if N_MICROBATCH > 1:
        for name, axis in PER_EXAMPLE_BATCH_AXIS.items():
            out[name] = _to_microbatches(out[name], axis)
    return {'x': out['x'], 'a_w_in': out['a_w_in'], 'a_ln_g': out['a_ln_g'], 'a_ln_b': out['a_ln_b'], 'a_w_s': out['a_w_s'], 'a_b_s': out['a_b_s'], 'a_w_out': out['a_w_out'], 'b_w_in': out['b_w_in'], 'b_w_grp': out['b_w_grp'], 'b_scale': out['b_scale'], 'b_w_out': out['b_w_out'], 'mix_pre_g': out['mix_pre_g'], 'mix_post_g': out['mix_post_g'], 'ffn_pre_g': out['ffn_pre_g'], 'ffn_post_g': out['ffn_post_g'], 'ffn_w_gate': out['ffn_w_gate'], 'ffn_w_up': out['ffn_w_up'], 'ffn_w_down': out['ffn_w_down'], 'loss_target': out['loss_target'], 'm_a_w_in': out['m_a_w_in'], 'm_a_ln_g': out['m_a_ln_g'], 'm_a_ln_b': out['m_a_ln_b'], 'm_a_w_s': out['m_a_w_s'], 'm_a_b_s': out['m_a_b_s'], 'm_a_w_out': out['m_a_w_out'], 'm_b_w_in': out['m_b_w_in'], 'm_b_w_grp': out['m_b_w_grp'], 'm_b_scale': out['m_b_scale'], 'm_b_w_out': out['m_b_w_out'], 'm_mix_pre_g': out['m_mix_pre_g'], 'm_mix_post_g': out['m_mix_post_g'], 'm_ffn_pre_g': out['m_ffn_pre_g'], 'm_ffn_post_g': out['m_ffn_post_g'], 'm_ffn_w_gate': out['m_ffn_w_gate'], 'm_ffn_w_up': out['m_ffn_w_up'], 'm_ffn_w_down': out['m_ffn_w_down'], 'v_a_w_in': out['v_a_w_in'], 'v_a_ln_g': out['v_a_ln_g'], 'v_a_ln_b': out['v_a_ln_b'], 'v_a_w_s': out['v_a_w_s'], 'v_a_b_s': out['v_a_b_s'], 'v_a_w_out': out['v_a_w_out'], 'v_b_w_in': out['v_b_w_in'], 'v_b_w_grp': out['v_b_w_grp'], 'v_b_scale': out['v_b_scale'], 'v_b_w_out': out['v_b_w_out'], 'v_mix_pre_g': out['v_mix_pre_g'], 'v_mix_post_g': out['v_mix_post_g'], 'v_ffn_pre_g': out['v_ffn_pre_g'], 'v_ffn_post_g': out['v_ffn_post_g'], 'v_ffn_w_gate': out['v_ffn_w_gate'], 'v_ffn_w_up': out['v_ffn_w_up'], 'v_ffn_w_down': out['v_ffn_w_down']}


def _loss(weights, diff, rest, loss_target):
    with _jax.named_scope("forward"):
        args = {**rest, TWIN_DIFF_INPUT: diff, **{k: w.astype(_WEIGHT_DTYPES[k]) for k, w in weights.items()}}
        y = _forward(args)
    with _jax.named_scope("loss_head"):
        err = _jnp.square(y.astype(_jnp.float32) - loss_target)
        return 0.5 * _jnp.sum(_jnp.mean(err, axis=-1)) if err.ndim else 0.5 * err


def _adamw(w, g, m, v):
    m = ADAM_B1 * m + (1.0 - ADAM_B1) * g
    v = ADAM_B2 * v + (1.0 - ADAM_B2) * _jnp.square(g)
    m_hat = m / (1.0 - ADAM_B1 ** ADAM_STEP)
    v_hat = v / (1.0 - ADAM_B2 ** ADAM_STEP)
    delta = -ADAM_LR * (m_hat / (_jnp.sqrt(v_hat) + ADAM_EPS) + ADAM_WD * w)
    return delta, m, v


def reference(x, a_w_in, a_ln_g, a_ln_b, a_w_s, a_b_s, a_w_out, b_w_in, b_w_grp, b_scale, b_w_out, mix_pre_g, mix_post_g, ffn_pre_g, ffn_post_g, ffn_w_gate, ffn_w_up, ffn_w_down, loss_target, m_a_w_in, m_a_ln_g, m_a_ln_b, m_a_w_s, m_a_b_s, m_a_w_out, m_b_w_in, m_b_w_grp, m_b_scale, m_b_w_out, m_mix_pre_g, m_mix_post_g, m_ffn_pre_g, m_ffn_post_g, m_ffn_w_gate, m_ffn_w_up, m_ffn_w_down, v_a_w_in, v_a_ln_g, v_a_ln_b, v_a_w_s, v_a_b_s, v_a_w_out, v_b_w_in, v_b_w_grp, v_b_scale, v_b_w_out, v_mix_pre_g, v_mix_post_g, v_ffn_pre_g, v_ffn_post_g, v_ffn_w_gate, v_ffn_w_up, v_ffn_w_down):
    given = dict(x=x, a_w_in=a_w_in, a_ln_g=a_ln_g, a_ln_b=a_ln_b, a_w_s=a_w_s, a_b_s=a_b_s, a_w_out=a_w_out, b_w_in=b_w_in, b_w_grp=b_w_grp, b_scale=b_scale, b_w_out=b_w_out, mix_pre_g=mix_pre_g, mix_post_g=mix_post_g, ffn_pre_g=ffn_pre_g, ffn_post_g=ffn_post_g, ffn_w_gate=ffn_w_gate, ffn_w_up=ffn_w_up, ffn_w_down=ffn_w_down, loss_target=loss_target, m_a_w_in=m_a_w_in, m_a_ln_g=m_a_ln_g, m_a_ln_b=m_a_ln_b, m_a_w_s=m_a_w_s, m_a_b_s=m_a_b_s, m_a_w_out=m_a_w_out, m_b_w_in=m_b_w_in, m_b_w_grp=m_b_w_grp, m_b_scale=m_b_scale, m_b_w_out=m_b_w_out, m_mix_pre_g=m_mix_pre_g, m_mix_post_g=m_mix_post_g, m_ffn_pre_g=m_ffn_pre_g, m_ffn_post_g=m_ffn_post_g, m_ffn_w_gate=m_ffn_w_gate, m_ffn_w_up=m_ffn_w_up, m_ffn_w_down=m_ffn_w_down, v_a_w_in=v_a_w_in, v_a_ln_g=v_a_ln_g, v_a_ln_b=v_a_ln_b, v_a_w_s=v_a_w_s, v_a_b_s=v_a_b_s, v_a_w_out=v_a_w_out, v_b_w_in=v_b_w_in, v_b_w_grp=v_b_w_grp, v_b_scale=v_b_scale, v_b_w_out=v_b_w_out, v_mix_pre_g=v_mix_pre_g, v_mix_post_g=v_mix_post_g, v_ffn_pre_g=v_ffn_pre_g, v_ffn_post_g=v_ffn_post_g, v_ffn_w_gate=v_ffn_w_gate, v_ffn_w_up=v_ffn_w_up, v_ffn_w_down=v_ffn_w_down)
    weights = {n: given[n] for n in TWIN_WEIGHTS}
    shared = {n: given[n] for n in SHARED_INPUTS}
    per_example = {n: given[n] for n in ['x']}
    grad_fn = _jax.value_and_grad(_loss, argnums=(0, 1))

    def one_microbatch(ex, loss_target):
        ex = dict(ex)
        diff = ex.pop(TWIN_DIFF_INPUT)
        return grad_fn(weights, diff, {**shared, **ex}, loss_target)

    if N_MICROBATCH == 1:
        loss, (grad_w, grad_x) = one_microbatch(per_example, given["loss_target"])
    else:
        def body(carry, xs):
            loss_sum, grad_sum = carry
            l_k, (gw_k, gx_k) = one_microbatch(xs[0], xs[1])
            with _jax.named_scope("update"):
                return (loss_sum + l_k, _jax.tree.map(_jnp.add, grad_sum, gw_k)), gx_k

        init = (_jnp.zeros((), _jnp.float32), _jax.tree.map(_jnp.zeros_like, weights))
        (loss, grad_w), grad_x = _jax.lax.scan(body, init, (per_example, given["loss_target"]))
    with _jax.named_scope("update"):
        delta_w, new_m, new_v = {}, {}, {}
        for n in TWIN_WEIGHTS:
            delta_w[n], new_m[n], new_v[n] = _adamw(weights[n], grad_w[n], given["m_" + n], given["v_" + n])
    return (loss, grad_x, *[grad_w[n] for n in TWIN_WEIGHTS], *[delta_w[n] for n in TWIN_WEIGHTS],
            *[new_m[n] for n in TWIN_WEIGHTS], *[new_v[n] for n in TWIN_WEIGHTS])
```

```python
import functools
import math

import jax
import jax.numpy as jnp
from jax import lax
from jax.experimental import pallas as pl
from jax.experimental.pallas import tpu as pltpu

F32 = jnp.float32
BF16 = jnp.bfloat16
MESH = pl.DeviceIdType.MESH

EPS = 1e-6
CHUNK = 128
A_GROUPS = 8
B_WINDOWS = (2, 4, 8, 16)
HALO = 16
N_CHIPS = 4
N_DEV = 8

ADAM_LR = 0.001
ADAM_B1 = 0.9
ADAM_B2 = 0.999
ADAM_EPS = 1e-08
ADAM_WD = 0.01
ADAM_STEP = 10

VMEM_LIMIT_BYTES = 60 * 1024 * 1024
INV_SQRT2 = 1.0 / math.sqrt(2.0)
INV_SQRT_2PI = 1.0 / math.sqrt(2.0 * math.pi)

_ANY = pl.BlockSpec(memory_space=pl.ANY)
_VM = pl.BlockSpec(memory_space=pltpu.VMEM)


def _params():
    return pltpu.CompilerParams(dimension_semantics=("arbitrary",), vmem_limit_bytes=VMEM_LIMIT_BYTES)


def _token_block(t):
    return 256 if t >= 1024 else 128


def _rows(tm, d):
    return pl.BlockSpec((tm, d), lambda i: (i, 0))


def _blocks(nb, tm, bw):
    return pl.BlockSpec((nb, tm, bw), lambda i: (0, i, 0))


def _dot(a, b):
    return lax.dot_general(a, b, (((1,), (0,)), ((), ())), preferred_element_type=F32)


def _dot_nt(a, b):
    return lax.dot_general(a, b, (((1,), (1,)), ((), ())), preferred_element_type=F32)


def _dot_tn(a, b):
    return lax.dot_general(a, b, (((0,), (0,)), ((), ())), preferred_element_type=F32)


def _rms(x, g):
    return x * lax.rsqrt(jnp.mean(x * x, axis=-1, keepdims=True) + EPS) * g


def _rms_bwd(dy, x, g):
    r = lax.rsqrt(jnp.mean(x * x, axis=-1, keepdims=True) + EPS)
    n = x * r
    dn = dy * g
    dx = r * (dn - n * jnp.mean(dn * n, axis=-1, keepdims=True))
    return dx, jnp.sum(dy * n, axis=0, keepdims=True)


def _gelu_and_grad(x):
    cdf = 0.5 * (1.0 + lax.erf(x * INV_SQRT2))
    return x * cdf, cdf + x * (jnp.exp(-0.5 * x * x) * INV_SQRT_2PI)


def _gelu(x):
    return x * (0.5 * (1.0 + lax.erf(x * INV_SQRT2)))


def _ffn_fwd(name, x, gpre, gpost, wg, wu, wd):
    t, d = x.shape
    nb, _, fs = wg.shape
    tm = _token_block(t)

    def body(x_ref, gpre_ref, gpost_ref, wg_ref, wu_ref, wd_ref, o_ref, g_ref, u_ref, f_ref):
        xv = x_ref[...]
        hb = _rms(xv, gpre_ref[...]).astype(BF16)
        f = jnp.zeros((tm, d), F32)
        for k in range(nb):
            g = _dot(hb, wg_ref[k])
            u = _dot(hb, wu_ref[k])
            g_ref[k] = g
            u_ref[k] = u
            a = (g * jax.nn.sigmoid(g) * u).astype(BF16)
            f = f + _dot(a, wd_ref[k])
        f_ref[...] = f
        o_ref[...] = xv + _rms(f, gpost_ref[...])

    return pl.pallas_call(
        body,
        name=name,
        grid=(t // tm,),
        in_specs=[_rows(tm, d), _VM, _VM, _VM, _VM, _VM],
        out_specs=[_rows(tm, d), _blocks(nb, tm, fs), _blocks(nb, tm, fs), _rows(tm, d)],
        out_shape=[
            jax.ShapeDtypeStruct((t, d), F32),
            jax.ShapeDtypeStruct((nb, t, fs), F32),
            jax.ShapeDtypeStruct((nb, t, fs), F32),
            jax.ShapeDtypeStruct((t, d), F32),
        ],
        compiler_params=_params(),
    )(x, gpre, gpost, wg, wu, wd)


def _ffn_bwd_hidden(name, dy, f, gpost, g_pre, u_pre, wd):
    t, d = dy.shape
    nb, fs, _ = wd.shape
    tm = _token_block(t)

    def body(dy_ref, f_ref, gpost_ref, g_ref, u_ref, wd_ref, dg_ref, du_ref, dwd_ref, dgain_ref):
        @pl.when(pl.program_id(0) == 0)
        def _():
            dwd_ref[...] = jnp.zeros_like(dwd_ref)
            dgain_ref[...] = jnp.zeros_like(dgain_ref)

        df, dgain = _rms_bwd(dy_ref[...], f_ref[...], gpost_ref[...])
        dgain_ref[...] += dgain
        dfb = df.astype(BF16)
        for k in range(nb):
            g = g_ref[k]
            u = u_ref[k]
            s = jax.nn.sigmoid(g)
            sg = g * s
            a = (sg * u).astype(BF16)
            da = _dot_nt(dfb, wd_ref[k])
            dwd_ref[k] += _dot_tn(a, dfb)
            du_ref[k] = (da * sg).astype(BF16)
            dg_ref[k] = (da * u * (s * (1.0 + g * (1.0 - s)))).astype(BF16)

    return pl.pallas_call(
        body,
        name=name,
        grid=(t // tm,),
        in_specs=[_rows(tm, d), _rows(tm, d), _VM, _blocks(nb, tm, fs), _blocks(nb, tm, fs), _VM],
        out_specs=[_blocks(nb, tm, fs), _blocks(nb, tm, fs), _VM, _VM],
        out_shape=[
            jax.ShapeDtypeStruct((nb, t, fs), BF16),
            jax.ShapeDtypeStruct((nb, t, fs), BF16),
            jax.ShapeDtypeStruct((nb, fs, d), F32),
            jax.ShapeDtypeStruct((1, d), F32),
        ],
        compiler_params=_params(),
    )(dy, f, gpost, g_pre, u_pre, wd)


def _bwd_in(name, dres, x, gpre, dzs, ws):
    t, d = x.shape
    n = len(ws)
    tm = _token_block(t)

    def body(*refs):
        dres_ref, x_ref, gpre_ref = refs[:3]
        dz_refs = refs[3 : 3 + n]
        w_refs = refs[3 + n : 3 + 2 * n]
        dx_ref = refs[3 + 2 * n]
        dw_refs = refs[4 + 2 * n : 4 + 3 * n]
        dgain_ref = refs[4 + 3 * n]

        @pl.when(pl.program_id(0) == 0)
        def _():
            for dw_ref in dw_refs:
                dw_ref[...] = jnp.zeros_like(dw_ref)
            dgain_ref[...] = jnp.zeros_like(dgain_ref)

        xv = x_ref[...]
        gain = gpre_ref[...]
        hb = _rms(xv, gain).astype(BF16)
        dh = jnp.zeros((tm, d), F32)
        for dz_ref, w_ref, dw_ref in zip(dz_refs, w_refs, dw_refs):
            for k in range(w_ref.shape[0]):
                dz = dz_ref[k]
                dh = dh + _dot_nt(dz, w_ref[k])
                dw_ref[k] += _dot_tn(hb, dz)
        dx, dgain = _rms_bwd(dh, xv, gain)
        dx_ref[...] = dres_ref[...] + dx
        dgain_ref[...] += dgain

    return pl.pallas_call(
        body,
        name=name,
        grid=(t // tm,),
        in_specs=[_rows(tm, d), _rows(tm, d), _VM]
        + [_blocks(w.shape[0], tm, w.shape[2]) for w in ws]
        + [_VM] * n,
        out_specs=[_rows(tm, d)] + [_VM] * n + [_VM],
        out_shape=[jax.ShapeDtypeStruct((t, d), F32)]
        + [jax.ShapeDtypeStruct(w.shape, F32) for w in ws]
        + [jax.ShapeDtypeStruct((1, d), F32)],
        compiler_params=_params(),
    )(dres, x, gpre, *dzs, *ws)


def _causal_weights(ws_ref):
    row = lax.broadcasted_iota(jnp.int32, (CHUNK, CHUNK), 0)
    col = lax.broadcasted_iota(jnp.int32, (CHUNK, CHUNK), 1)
    return [jnp.where(row >= col, ws_ref[g], 0.0).astype(BF16) for g in range(A_GROUPS)]


def _layernorm_halves(v0, v1):
    width = v0.shape[-1] + v1.shape[-1]
    mu = (jnp.sum(v0, axis=-1, keepdims=True) + jnp.sum(v1, axis=-1, keepdims=True)) / width
    c0 = v0 - mu
    c1 = v1 - mu
    var = (jnp.sum(c0 * c0, axis=-1, keepdims=True) + jnp.sum(c1 * c1, axis=-1, keepdims=True)) / width
    rstd = lax.rsqrt(var + EPS)
    return c0 * rstd, c1 * rstd, rstd


def _spatial_gate(sv_ref, wtril, vl, bt_ref, half, tm, gd):
    for gg in range(A_GROUPS // 2):
        g = half * (A_GROUPS // 2) + gg
        bias = bt_ref[:, g : g + 1]
        for n in range(tm // CHUNK):
            blk = vl[n * CHUNK : (n + 1) * CHUNK, gg * gd : (gg + 1) * gd]
            sv_ref[n * CHUNK : (n + 1) * CHUNK, gg * gd : (gg + 1) * gd] = _dot(wtril[g], blk) + bias


def _mix_a_fwd(name, x, gpre, gpost, w_in, ln_g, ln_b, w_s, b_t, w_out):
    t, d = x.shape
    _, _, q = w_in.shape
    gd = 2 * q // A_GROUPS
    tm = _token_block(t)

    def body(x_ref, gpre_ref, gpost_ref, win_ref, lng_ref, lnb_ref, ws_ref, bt_ref, wout_ref,
             o_ref, z_ref, m_ref, sv_ref):
        xv = x_ref[...]
        hb = _rms(xv, gpre_ref[...]).astype(BF16)
        z = []
        for k in range(4):
            zp = _dot(hb, win_ref[k])
            z_ref[k] = zp
            z.append(_gelu(zp))
        vh0, vh1, _ = _layernorm_halves(z[2], z[3])
        vls = [(vh * lng_ref[b : b + 1, :] + lnb_ref[b : b + 1, :]).astype(BF16) for b, vh in enumerate((vh0, vh1))]
        wtril = _causal_weights(ws_ref)
        m = jnp.zeros((tm, d), F32)
        for b in range(2):
            _spatial_gate(sv_ref, wtril, vls[b], bt_ref, b, tm, gd)
            gated = (z[b] * sv_ref[...]).astype(BF16)
            m = m + _dot(gated, wout_ref[b])
        m_ref[...] = m
        o_ref[...] = xv + _rms(m, gpost_ref[...])

    return pl.pallas_call(
        body,
        name=name,
        grid=(t // tm,),
        in_specs=[_rows(tm, d)] + [_VM] * 8,
        out_specs=[_rows(tm, d), _blocks(4, tm, q), _rows(tm, d)],
        out_shape=[
            jax.ShapeDtypeStruct((t, d), F32),
            jax.ShapeDtypeStruct((4, t, q), F32),
            jax.ShapeDtypeStruct((t, d), F32),
        ],
        scratch_shapes=[pltpu.VMEM((tm, q), F32)],
        compiler_params=_params(),
    )(x, gpre, gpost, w_in, ln_g, ln_b, w_s, b_t, w_out)


def _mix_a_bwd_hidden(name, dy, m, gpost, zpre, ln_g, ln_b, w_s, b_t, w_out):
    t, d = dy.shape
    _, _, q = zpre.shape
    gd = 2 * q // A_GROUPS
    tm = _token_block(t)
    n_chunks = tm // CHUNK

    def body(dy_ref, m_ref, gpost_ref, z_ref, lng_ref, lnb_ref, ws_ref, bt_ref, wout_ref,
             dz_ref, dwout_ref, dws_ref, dbacc_ref, dlng_ref, dlnb_ref, dgain_ref, sv_ref, dvl_ref):
        first = pl.program_id(0) == 0

        @pl.when(first)
        def _():
            for ref in (dwout_ref, dws_ref, dbacc_ref, dlng_ref, dlnb_ref, dgain_ref):
                ref[...] = jnp.zeros_like(ref)

        dm, dgain = _rms_bwd(dy_ref[...], m_ref[...], gpost_ref[...])
        dgain_ref[...] += dgain
        dmb = dm.astype(BF16)
        v0 = _gelu(z_ref[2])
        v1 = _gelu(z_ref[3])
        vhs = list(_layernorm_halves(v0, v1))
        rstd = vhs.pop()
        vls = [(vh * lng_ref[b : b + 1, :] + lnb_ref[b : b + 1, :]).astype(BF16) for b, vh in enumerate(vhs)]
        wtril = _causal_weights(ws_ref)
        dvhs = []
        for b in range(2):
            zp = z_ref[b]
            u, du_dz = _gelu_and_grad(zp)
            _spatial_gate(sv_ref, wtril, vls[b], bt_ref, b, tm, gd)
            sv = sv_ref[...]
            gated = (u * sv).astype(BF16)
            dgated = _dot_nt(dmb, wout_ref[b])
            dwout_ref[b] += _dot_tn(gated, dmb)
            dz_ref[b] = (dgated * sv * du_dz).astype(BF16)
            dsv = dgated * u
            folded = dsv[0:CHUNK, :]
            for c in range(1, n_chunks):
                folded = folded + dsv[c * CHUNK : (c + 1) * CHUNK, :]
            for gg in range(A_GROUPS // 2):
                g = b * (A_GROUPS // 2) + gg
                dbacc_ref[:, g : g + 1] += jnp.sum(folded[:, gg * gd : (gg + 1) * gd], axis=1, keepdims=True)
            dsvb = dsv.astype(BF16)
            for gg in range(A_GROUPS // 2):
                g = b * (A_GROUPS // 2) + gg
                for c in range(n_chunks):
                    rows = slice(c * CHUNK, (c + 1) * CHUNK)
                    cols = slice(gg * gd, (gg + 1) * gd)
                    blk = dsvb[rows, cols]
                    dvl_ref[rows, cols] = _dot_tn(wtril[g], blk)
                    dws_ref[g] += _dot_nt(blk, vls[b][rows, cols])
            dvl = dvl_ref[...]
            dlng_ref[b : b + 1, :] += jnp.sum(dvl * vhs[b], axis=0, keepdims=True)
            dlnb_ref[b : b + 1, :] += jnp.sum(dvl, axis=0, keepdims=True)
            dvhs.append(dvl * lng_ref[b : b + 1, :])
        width = 2.0 * q
        m1 = (jnp.sum(dvhs[0], axis=-1, keepdims=True) + jnp.sum(dvhs[1], axis=-1, keepdims=True)) / width
        m2 = (jnp.sum(dvhs[0] * vhs[0], axis=-1, keepdims=True)
              + jnp.sum(dvhs[1] * vhs[1], axis=-1, keepdims=True)) / width
        for b in range(2):
            dv = rstd * (dvhs[b] - m1 - vhs[b] * m2)
            _, dv_dz = _gelu_and_grad(z_ref[2 + b])
            dz_ref[2 + b] = (dv * dv_dz).astype(BF16)

        @pl.when(pl.program_id(0) == t // tm - 1)
        def _():
            row = lax.broadcasted_iota(jnp.int32, (CHUNK, CHUNK), 0)
            col = lax.broadcasted_iota(jnp.int32, (CHUNK, CHUNK), 1)
            for g in range(A_GROUPS):
                dws_ref[g] = jnp.where(row >= col, dws_ref[g], 0.0)

    return pl.pallas_call(
        body,
        name=name,
        grid=(t // tm,),
        in_specs=[_rows(tm, d), _rows(tm, d), _VM, _blocks(4, tm, q)] + [_VM] * 5,
        out_specs=[_blocks(4, tm, q)] + [_VM] * 6,
        out_shape=[
            jax.ShapeDtypeStruct((4, t, q), BF16),
            jax.ShapeDtypeStruct((2, q, d), F32),
            jax.ShapeDtypeStruct((A_GROUPS, CHUNK, CHUNK), F32),
            jax.ShapeDtypeStruct((CHUNK, A_GROUPS), F32),
            jax.ShapeDtypeStruct((2, q), F32),
            jax.ShapeDtypeStruct((2, q), F32),
            jax.ShapeDtypeStruct((1, d), F32),
        ],
        scratch_shapes=[pltpu.VMEM((tm, q), F32), pltpu.VMEM((tm, q), F32)],
        compiler_params=_params(),
    )(dy, m, gpost, zpre, ln_g, ln_b, w_s, b_t, w_out)


def _norm_matmul(name, x, gpre, w):
    t, d = x.shape
    n = w.shape[1]
    tm = _token_block(t)

    def body(x_ref, gpre_ref, w_ref, o_ref):
        o_ref[...] = _dot(_rms(x_ref[...], gpre_ref[...]).astype(BF16), w_ref[...])

    return pl.pallas_call(
        body,
        name=name,
        grid=(t // tm,),
        in_specs=[_rows(tm, d), _VM, _VM],
        out_specs=_rows(tm, n),
        out_shape=jax.ShapeDtypeStruct((t, n), F32),
        compiler_params=_params(),
    )(x, gpre, w)


def _window_counts(tm, win):
    pos = pl.program_id(0) * tm + lax.broadcasted_iota(jnp.int32, (tm, 1), 0)
    return jnp.minimum(pos + 1, win).astype(F32)


def _pooled(p, halo, tm, gd):
    prev = jnp.where(pl.program_id(0) == 0, 0.0, halo)
    ext = jnp.concatenate([prev, p], axis=0)
    out = []
    for g, win in enumerate(B_WINDOWS):
        s = ext[:, g * gd : (g + 1) * gd]
        step = 1
        while step < win:
            s = s + pltpu.roll(s, step, 0)
            step *= 2
        total = s[HALO:, :]
        out.append(total / _window_counts(tm, win) - p[:, g * gd : (g + 1) * gd])
    return out


def _halo_spec(t, tm, d, ahead):
    per = tm // HALO
    if ahead:
        return pl.BlockSpec((HALO, d), lambda i: (jnp.minimum((i + 1) * per, t // HALO - 1), 0))
    return pl.BlockSpec((HALO, d), lambda i: (jnp.maximum(i * per - 1, 0), 0))


def _mix_b_fwd(name, x, p, gpost, w_grp, scale, w_out):
    t, d = x.shape
    gd = d // len(B_WINDOWS)
    tm = _token_block(t)

    def body(x_ref, p_ref, halo_ref, gpost_ref, wgrp_ref, scale_ref, wout_ref, o_ref, m_ref):
        pooled = _pooled(p_ref[...], halo_ref[...], tm, gd)
        mixed = jnp.concatenate([_dot(pg.astype(BF16), wgrp_ref[g]) for g, pg in enumerate(pooled)], axis=1)
        m = _dot((mixed * scale_ref[...]).astype(BF16), wout_ref[...])
        m_ref[...] = m
        o_ref[...] = x_ref[...] + _rms(m, gpost_ref[...])

    return pl.pallas_call(
        body,
        name=name,
        grid=(t // tm,),
        in_specs=[_rows(tm, d), _rows(tm, d), _halo_spec(t, tm, d, False), _VM, _VM, _VM, _VM],
        out_specs=[_rows(tm, d), _rows(tm, d)],
        out_shape=[jax.ShapeDtypeStruct((t, d), F32), jax.ShapeDtypeStruct((t, d), F32)],
        compiler_params=_params(),
    )(x, p, p, gpost, w_grp, scale, w_out)


def _mix_b_bwd_hidden(name, dy, m, gpost, p, w_grp, scale, w_out):
    t, d = dy.shape
    gd = d // len(B_WINDOWS)
    tm = _token_block(t)

    def body(dy_ref, m_ref, gpost_ref, p_ref, halo_ref, wgrp_ref, scale_ref, wout_ref,
             dq_ref, dwout_ref, dwgrp_ref, dscale_ref, dgain_ref):
        @pl.when(pl.program_id(0) == 0)
        def _():
            for ref in (dwout_ref, dwgrp_ref, dscale_ref, dgain_ref):
                ref[...] = jnp.zeros_like(ref)

        dm, dgain = _rms_bwd(dy_ref[...], m_ref[...], gpost_ref[...])
        dgain_ref[...] += dgain
        dmb = dm.astype(BF16)
        pooled = [pg.astype(BF16) for pg in _pooled(p_ref[...], halo_ref[...], tm, gd)]
        mixed = jnp.concatenate([_dot(pg, wgrp_ref[g]) for g, pg in enumerate(pooled)], axis=1)
        scale = scale_ref[...]
        ms = (mixed * scale).astype(BF16)
        dms = _dot_nt(dmb, wout_ref[...])
        dwout_ref[...] += _dot_tn(ms, dmb)
        dscale_ref[...] += jnp.sum(dms * mixed, axis=0, keepdims=True)
        dmixed = (dms * scale).astype(BF16)
        for g, win in enumerate(B_WINDOWS):
            cols = slice(g * gd, (g + 1) * gd)
            dmg = dmixed[:, cols]
            dwgrp_ref[g] += _dot_tn(pooled[g], dmg)
            dq_ref[:, cols] = _dot_nt(dmg, wgrp_ref[g]) / _window_counts(tm, win)

    return pl.pallas_call(
        body,
        name=name,
        grid=(t // tm,),
        in_specs=[_rows(tm, d), _rows(tm, d), _VM, _rows(tm, d), _halo_spec(t, tm, d, False), _VM, _VM, _VM],
        out_specs=[_rows(tm, d), _VM, _VM, _VM, _VM],
        out_shape=[
            jax.ShapeDtypeStruct((t, d), F32),
            jax.ShapeDtypeStruct((d, d), F32),
            jax.ShapeDtypeStruct((len(B_WINDOWS), gd, gd), F32),
            jax.ShapeDtypeStruct((1, d), F32),
            jax.ShapeDtypeStruct((1, d), F32),
        ],
        compiler_params=_params(),
    )(dy, m, gpost, p, p, w_grp, scale, w_out)


def _pool_bwd(name, dq):
    t, d = dq.shape
    gd = d // len(B_WINDOWS)
    tm = _token_block(t)
    n_steps = t // tm

    def body(dq_ref, halo_ref, dp_ref):
        dq_blk = dq_ref[...]
        nxt = jnp.where(pl.program_id(0) == n_steps - 1, 0.0, halo_ref[...])
        ext = jnp.concatenate([dq_blk, nxt], axis=0)
        for g, win in enumerate(B_WINDOWS):
            cols = slice(g * gd, (g + 1) * gd)
            s = ext[:, cols]
            step = 1
            while step < win:
                s = s + pltpu.roll(s, tm + HALO - step, 0)
                step *= 2
            dp_ref[0, :, cols] = (s[:tm, :] - dq_blk[:, cols] * _window_counts(tm, win)).astype(BF16)

    return pl.pallas_call(
        body,
        name=name,
        grid=(n_steps,),
        in_specs=[_rows(tm, d), _halo_spec(t, tm, d, True)],
        out_specs=_blocks(1, tm, d),
        out_shape=jax.ShapeDtypeStruct((1, t, d), BF16),
        compiler_params=_params(),
    )(dq, dq)


def _loss_head(y, target):
    t, d = y.shape
    tm = _token_block(t)

    def body(y_ref, t_ref, sq_ref, dy_ref):
        @pl.when(pl.program_id(0) == 0)
        def _():
            sq_ref[...] = jnp.zeros_like(sq_ref)

        err = y_ref[...] - t_ref[...]
        dy_ref[...] = err / d
        sq_ref[...] += jnp.sum(err * err)

    return pl.pallas_call(
        body,
        name="loss_head",
        grid=(t // tm,),
        in_specs=[_rows(tm, d), _rows(tm, d)],
        out_specs=[_VM, _rows(tm, d)],
        out_shape=[jax.ShapeDtypeStruct((8, 128), F32), jax.ShapeDtypeStruct((t, d), F32)],
        compiler_params=_params(),
    )(y, target)


def _cast_bf16(name, w):
    n_layers, r, c = w.shape

    def body(w_ref, o_ref):
        o_ref[...] = w_ref[...].astype(BF16)

    spec = pl.BlockSpec((1, r, c), lambda i: (i, 0, 0))
    return pl.pallas_call(
        body,
        name=name,
        grid=(n_layers,),
        in_specs=[spec],
        out_specs=spec,
        out_shape=jax.ShapeDtypeStruct(w.shape, BF16),
        compiler_params=_params(),
    )(w)


def _row_tile(r):
    return 256 if r % 256 == 0 else r


def _pair_sum(name, core, dw, recv):
    _, n_layers, r, c = dw.shape
    half = n_layers // 2
    tr = _row_tile(r)

    def body(core_ref, a_ref, b_ref, o_ref):
        del core_ref
        o_ref[...] = a_ref[...] + b_ref[...]

    return pl.pallas_call(
        body,
        name=name,
        grid_spec=pltpu.PrefetchScalarGridSpec(
            num_scalar_prefetch=1,
            grid=(N_CHIPS, half, r // tr),
            in_specs=[
                pl.BlockSpec((1, 1, tr, c), lambda k, l, i, core_ref: (k, core_ref[0] * half + l, i, 0)),
                pl.BlockSpec((1, 1, tr, c), lambda k, l, i, core_ref: (k, l, i, 0)),
            ],
            out_specs=pl.BlockSpec((1, 1, tr, c), lambda k, l, i, core_ref: (k, l, i, 0)),
        ),
        out_shape=jax.ShapeDtypeStruct(recv.shape, F32),
        compiler_params=pltpu.CompilerParams(
            dimension_semantics=("arbitrary",) * 3, vmem_limit_bytes=VMEM_LIMIT_BYTES),
    )(core, dw, recv)


def _chip_sum(name, parts):
    _, half, r, c = parts.shape
    tr = _row_tile(r)

    def body(p_ref, o_ref):
        acc = p_ref[0]
        for k in range(1, N_CHIPS):
            acc = acc + p_ref[k]
        o_ref[...] = acc

    return pl.pallas_call(
        body,
        name=name,
        grid=(half, r // tr),
        in_specs=[pl.BlockSpec((N_CHIPS, 1, tr, c), lambda l, i: (0, l, i, 0))],
        out_specs=pl.BlockSpec((1, tr, c), lambda l, i: (l, i, 0)),
        out_shape=jax.ShapeDtypeStruct((half, r, c), F32),
        compiler_params=pltpu.CompilerParams(
            dimension_semantics=("arbitrary",) * 2, vmem_limit_bytes=VMEM_LIMIT_BYTES),
    )(parts)


def _adamw(name, w, g, m, v):
    n_layers, r, c = w.shape
    tr = _row_tile(r)

    def body(w_ref, g_ref, m_ref, v_ref, d_ref, nm_ref, nv_ref):
        gv = g_ref[...]
        nm = ADAM_B1 * m_ref[...] + (1.0 - ADAM_B1) * gv
        nv = ADAM_B2 * v_ref[...] + (1.0 - ADAM_B2) * jnp.square(gv)
        m_hat = nm / (1.0 - ADAM_B1 ** ADAM_STEP)
        v_hat = nv / (1.0 - ADAM_B2 ** ADAM_STEP)
        d_ref[...] = -ADAM_LR * (m_hat / (jnp.sqrt(v_hat) + ADAM_EPS) + ADAM_WD * w_ref[...])
        nm_ref[...] = nm
        nv_ref[...] = nv

    spec = pl.BlockSpec((1, tr, c), lambda l, i: (l, i, 0))
    return pl.pallas_call(
        body,
        name=name,
        grid=(n_layers, r // tr),
        in_specs=[spec] * 4,
        out_specs=[spec] * 3,
        out_shape=[jax.ShapeDtypeStruct(w.shape, F32)] * 3,
        compiler_params=pltpu.CompilerParams(
            dimension_semantics=("arbitrary",) * 2, vmem_limit_bytes=VMEM_LIMIT_BYTES),
    )(w, g, m, v)


def _position():
    return lax.axis_index("x"), lax.axis_index("y"), lax.axis_index("c")


def _other_chips(x, y):
    return [(1 - x, y), (x, 1 - y), (1 - x, 1 - y)]


def _gather_chips(name, shards):
    n = len(shards)

    def body(*refs):
        srcs = refs[:n]
        outs = refs[n : 2 * n]
        send_sems, recv_sems, local_sems = refs[2 * n :]
        x, y, c = _position()
        me = 2 * x + y
        chips = _other_chips(x, y)
        local = [pltpu.make_async_copy(srcs[w], outs[w].at[me], local_sems.at[w]) for w in range(n)]
        for cp in local:
            cp.start()

        def copy(w, j, slot):
            px, py = chips[j]
            return pltpu.make_async_remote_copy(
                src_ref=srcs[w], dst_ref=outs[w].at[slot],
                send_sem=send_sems.at[3 * w + j], recv_sem=recv_sems.at[3 * w + j],
                device_id=(px, py, c), device_id_type=MESH)

        sends = [copy(w, j, me) for w in range(n) for j in range(3)]
        for cp in sends:
            cp.start()
        for w in range(n):
            for j, (px, py) in enumerate(chips):
                copy(w, j, 2 * px + py).wait_recv()
        for cp in sends:
            cp.wait_send()
        for cp in local:
            cp.wait()

    return pl.pallas_call(
        body,
        name=name,
        in_specs=[_ANY] * n,
        out_specs=[_ANY] * n,
        out_shape=[jax.ShapeDtypeStruct((N_CHIPS,) + s.shape, s.dtype) for s in shards],
        scratch_shapes=[
            pltpu.SemaphoreType.DMA((3 * n,)),
            pltpu.SemaphoreType.DMA((3 * n,)),
            pltpu.SemaphoreType.DMA((n,)),
        ],
    )(*shards)


def _send_other_half(name, grads):
    n = len(grads)

    def body(*refs):
        srcs = refs[:n]
        outs = refs[n : 2 * n]
        send_sems, recv_sems = refs[2 * n :]
        x, y, c = _position()
        copies = []
        for w in range(n):
            half = srcs[w].shape[1] // 2
            copies.append(pltpu.make_async_remote_copy(
                src_ref=srcs[w].at[:, pl.ds((1 - c) * half, half)], dst_ref=outs[w],
                send_sem=send_sems.at[w], recv_sem=recv_sems.at[w],
                device_id=(x, y, 1 - c), device_id_type=MESH))
        for cp in copies:
            cp.start()
        for cp in copies:
            cp.wait_recv()
        for cp in copies:
            cp.wait_send()

    return pl.pallas_call(
        body,
        name=name,
        in_specs=[_ANY] * n,
        out_specs=[_ANY] * n,
        out_shape=[jax.ShapeDtypeStruct((g.shape[0], g.shape[1] // 2) + g.shape[2:], g.dtype) for g in grads],
        scratch_shapes=[pltpu.SemaphoreType.DMA((n,)), pltpu.SemaphoreType.DMA((n,))],
    )(*grads)


def _scatter_chips(name, parts):
    n = len(parts)

    def body(*refs):
        srcs = refs[:n]
        outs = refs[n : 2 * n]
        send_sems, recv_sems, local_sems = refs[2 * n :]
        x, y, c = _position()
        me = 2 * x + y
        chips = _other_chips(x, y)
        local = [pltpu.make_async_copy(srcs[w].at[me], outs[w].at[me], local_sems.at[w]) for w in range(n)]
        for cp in local:
            cp.start()

        def copy(w, j, src_slot, dst_slot):
            px, py = chips[j]
            return pltpu.make_async_remote_copy(
                src_ref=srcs[w].at[src_slot], dst_ref=outs[w].at[dst_slot],
                send_sem=send_sems.at[3 * w + j], recv_sem=recv_sems.at[3 * w + j],
                device_id=(px, py, c), device_id_type=MESH)

        sends = [copy(w, j, 2 * chips[j][0] + chips[j][1], me) for w in range(n) for j in range(3)]
        for cp in sends:
            cp.start()
        for w in range(n):
            for j, (px, py) in enumerate(chips):
                copy(w, j, me, 2 * px + py).wait_recv()
        for cp in sends:
            cp.wait_send()
        for cp in local:
            cp.wait()

    return pl.pallas_call(
        body,
        name=name,
        in_specs=[_ANY] * n,
        out_specs=[_ANY] * n,
        out_shape=[jax.ShapeDtypeStruct(p.shape, p.dtype) for p in parts],
        scratch_shapes=[
            pltpu.SemaphoreType.DMA((3 * n,)),
            pltpu.SemaphoreType.DMA((3 * n,)),
            pltpu.SemaphoreType.DMA((n,)),
        ],
    )(*parts)


def _swap_halves(name, halves):
    n = len(halves)

    def body(*refs):
        srcs = refs[:n]
        outs = refs[n : 2 * n]
        send_sems, recv_sems, local_sems = refs[2 * n :]
        x, y, c = _position()
        local, remote = [], []
        for w in range(n):
            half = srcs[w].shape[0]
            mine = outs[w].at[pl.ds(c * half, half)]
            local.append(pltpu.make_async_copy(srcs[w], mine, local_sems.at[w]))
            remote.append(pltpu.make_async_remote_copy(
                src_ref=srcs[w], dst_ref=mine, send_sem=send_sems.at[w], recv_sem=recv_sems.at[w],
                device_id=(x, y, 1 - c), device_id_type=MESH))
        for cp in local + remote:
            cp.start()
        for w in range(n):
            half = srcs[w].shape[0]
            pltpu.make_async_remote_copy(
                src_ref=srcs[w], dst_ref=outs[w].at[pl.ds((1 - c) * half, half)],
                send_sem=send_sems.at[w], recv_sem=recv_sems.at[w],
                device_id=(x, y, 1 - c), device_id_type=MESH).wait_recv()
        for cp in remote:
            cp.wait_send()
        for cp in local:
            cp.wait()

    return pl.pallas_call(
        body,
        name=name,
        in_specs=[_ANY] * n,
        out_specs=[_ANY] * n,
        out_shape=[jax.ShapeDtypeStruct((2 * h.shape[0],) + h.shape[1:], h.dtype) for h in halves],
        scratch_shapes=[
            pltpu.SemaphoreType.DMA((n,)),
            pltpu.SemaphoreType.DMA((n,)),
            pltpu.SemaphoreType.DMA((n,)),
        ],
    )(*halves)


def _all_sum_small(packed):
    m_per, n = packed.shape

    def body(x_ref, sum_ref, all_ref, send_sems, recv_sems, local_sem):
        x, y, c = _position()
        me, sibling = (x, y, c), (x, y, 1 - c)
        chips = _other_chips(x, y)

        def rows(px, py, pc):
            return all_ref.at[pl.ds((4 * px + 2 * py + pc) * m_per, m_per), :]

        def copy(k, block, to, src=None):
            return pltpu.make_async_remote_copy(
                src_ref=rows(*block) if src is None else src, dst_ref=rows(*block),
                send_sem=send_sems.at[k], recv_sem=recv_sems.at[k], device_id=to, device_id_type=MESH)

        mine = pltpu.make_async_copy(x_ref, rows(*me), local_sem)
        mine.start()
        first = [copy(0, me, sibling, src=x_ref)]
        first += [copy(1 + j, me, (*chip, c), src=x_ref) for j, chip in enumerate(chips)]
        for cp in first:
            cp.start()
        passed = [copy(4 + j, (*chip, c), sibling) for j, chip in enumerate(chips)]
        for j, chip in enumerate(chips):
            copy(1 + j, (*chip, c), me).wait_recv()
            passed[j].start()
        copy(0, sibling, me).wait_recv()
        for j, chip in enumerate(chips):
            copy(4 + j, (*chip, 1 - c), me).wait_recv()
        for cp in first + passed:
            cp.wait_send()
        mine.wait()
        acc = all_ref[0:m_per, :]
        for k in range(1, N_DEV):
            acc = acc + all_ref[k * m_per : (k + 1) * m_per, :]
        sum_ref[...] = acc

    return pl.pallas_call(
        body,
        name="all_sum_small",
        in_specs=[_VM],
        out_specs=_VM,
        out_shape=jax.ShapeDtypeStruct((m_per, n), F32),
        scratch_shapes=[
            pltpu.VMEM((N_DEV * m_per, n), F32),
            pltpu.SemaphoreType.DMA((7,)),
            pltpu.SemaphoreType.DMA((7,)),
            pltpu.SemaphoreType.DMA,
        ],
        compiler_params=pltpu.CompilerParams(vmem_limit_bytes=VMEM_LIMIT_BYTES),
    )(packed)


SHARDED = ("a_w_in", "a_w_out", "b_w_in", "b_w_grp", "b_scale", "b_w_out", "ffn_w_gate", "ffn_w_up", "ffn_w_down")
SMALL = ("a_ln_g", "a_ln_b", "a_w_s", "a_b_s", "mix_pre_g", "mix_post_g", "ffn_pre_g", "ffn_post_g")
WEIGHTS = ("a_w_in", "a_ln_g", "a_ln_b", "a_w_s", "a_b_s", "a_w_out", "b_w_in", "b_w_grp", "b_scale", "b_w_out",
           "mix_pre_g", "mix_post_g", "ffn_pre_g", "ffn_post_g", "ffn_w_gate", "ffn_w_up", "ffn_w_down")


def _as_layers(a):
    if a.ndim == 2:
        return a.reshape(a.shape[0], 1, a.shape[1])
    return a.reshape(a.shape[0], -1, a.shape[-1])


def _pack_small(parts):
    return jnp.concatenate([p.reshape(-1, 128) for p in parts], axis=0)


def _unpack_small(packed, like):
    out, row = [], 0
    for ref in like:
        rows = ref.size // 128
        out.append(packed[row : row + rows].reshape(ref.shape))
        row += rows
    return out


def kernel(x, a_w_in, a_ln_g, a_ln_b, a_w_s, a_b_s, a_w_out, b_w_in, b_w_grp, b_scale, b_w_out, mix_pre_g, mix_post_g, ffn_pre_g, ffn_post_g, ffn_w_gate, ffn_w_up, ffn_w_down, loss_target, m_a_w_in, m_a_ln_g, m_a_ln_b, m_a_w_s, m_a_b_s, m_a_w_out, m_b_w_in, m_b_w_grp, m_b_scale, m_b_w_out, m_mix_pre_g, m_mix_post_g, m_ffn_pre_g, m_ffn_post_g, m_ffn_w_gate, m_ffn_w_up, m_ffn_w_down, v_a_w_in, v_a_ln_g, v_a_ln_b, v_a_w_s, v_a_b_s, v_a_w_out, v_b_w_in, v_b_w_grp, v_b_scale, v_b_w_out, v_mix_pre_g, v_mix_post_g, v_ffn_pre_g, v_ffn_post_g, v_ffn_w_gate, v_ffn_w_up, v_ffn_w_down):
    weights = dict(a_w_in=a_w_in, a_ln_g=a_ln_g, a_ln_b=a_ln_b, a_w_s=a_w_s, a_b_s=a_b_s, a_w_out=a_w_out,
                   b_w_in=b_w_in, b_w_grp=b_w_grp, b_scale=b_scale, b_w_out=b_w_out, mix_pre_g=mix_pre_g,
                   mix_post_g=mix_post_g, ffn_pre_g=ffn_pre_g, ffn_post_g=ffn_post_g, ffn_w_gate=ffn_w_gate,
                   ffn_w_up=ffn_w_up, ffn_w_down=ffn_w_down)
    mom1 = dict(a_w_in=m_a_w_in, a_ln_g=m_a_ln_g, a_ln_b=m_a_ln_b, a_w_s=m_a_w_s, a_b_s=m_a_b_s, a_w_out=m_a_w_out,
                b_w_in=m_b_w_in, b_w_grp=m_b_w_grp, b_scale=m_b_scale, b_w_out=m_b_w_out, mix_pre_g=m_mix_pre_g,
                mix_post_g=m_mix_post_g, ffn_pre_g=m_ffn_pre_g, ffn_post_g=m_ffn_post_g, ffn_w_gate=m_ffn_w_gate,
                ffn_w_up=m_ffn_w_up, ffn_w_down=m_ffn_w_down)
    mom2 = dict(a_w_in=v_a_w_in, a_ln_g=v_a_ln_g, a_ln_b=v_a_ln_b, a_w_s=v_a_w_s, a_b_s=v_a_b_s, a_w_out=v_a_w_out,
                b_w_in=v_b_w_in, b_w_grp=v_b_w_grp, b_scale=v_b_scale, b_w_out=v_b_w_out, mix_pre_g=v_mix_pre_g,
                mix_post_g=v_mix_post_g, ffn_pre_g=v_ffn_pre_g, ffn_post_g=v_ffn_post_g, ffn_w_gate=v_ffn_w_gate,
                ffn_w_up=v_ffn_w_up, ffn_w_down=v_ffn_w_down)

    t, d = x.shape[1], x.shape[2]
    depth = mix_pre_g.shape[0]
    gd_b = d // len(B_WINDOWS)
    xs = x.reshape(t, d)
    target = loss_target.reshape(t, d)

    shards = []
    for name in SHARDED:
        w3 = _as_layers(weights[name])
        shards.append(w3 if name == "b_scale" else _cast_bf16("cast_" + name, w3))
    full = dict(zip(SHARDED, _gather_chips("gather_weights", shards)))

    def gain(name, i):
        return weights[name][i].reshape(1, d)

    saved = []
    cur = xs
    for i in range(depth):
        j = i // 2
        if i % 2 == 0:
            w_in = full["a_w_in"][:, j]
            q = w_in.shape[2]
            w_out = full["a_w_out"][:, j].reshape(2, q, d)
            ln_g = a_ln_g[j].reshape(2, q)
            ln_b = a_ln_b[j].reshape(2, q)
            b_t = jnp.transpose(a_b_s[j])
            nxt, zpre, m = _mix_a_fwd(f"mix_a_fwd{j}", cur, gain("mix_pre_g", i), gain("mix_post_g", i),
                                      w_in, ln_g, ln_b, a_w_s[j], b_t, w_out)
            mix_saved = dict(x=cur, zpre=zpre, m=m, w_in=w_in, w_out=w_out, ln_g=ln_g, ln_b=ln_b, b_t=b_t)
        else:
            w_in = full["b_w_in"][:, j].reshape(d, d)
            w_out = full["b_w_out"][:, j].reshape(d, d)
            w_grp = jnp.transpose(full["b_w_grp"][:, j].reshape(N_CHIPS, len(B_WINDOWS), gd_b // N_CHIPS, gd_b),
                                  (1, 0, 2, 3)).reshape(len(B_WINDOWS), gd_b, gd_b)
            scale = full["b_scale"][:, j].reshape(1, d)
            p = _norm_matmul(f"mix_b_in{j}", cur, gain("mix_pre_g", i), w_in)
            nxt, m = _mix_b_fwd(f"mix_b_fwd{j}", cur, p, gain("mix_post_g", i), w_grp, scale, w_out)
            mix_saved = dict(x=cur, p=p, m=m, w_in=w_in, w_out=w_out, w_grp=w_grp, scale=scale)
        cur = nxt
        wg, wu, wd = full["ffn_w_gate"][:, i], full["ffn_w_up"][:, i], full["ffn_w_down"][:, i]
        nxt, g_pre, u_pre, f = _ffn_fwd(f"ffn_fwd{i}", cur, gain("ffn_pre_g", i), gain("ffn_post_g", i), wg, wu, wd)
        saved.append((mix_saved, dict(x=cur, g=g_pre, u=u_pre, f=f, wg=wg, wu=wu, wd=wd)))
        cur = nxt

    sq, dcur = _loss_head(cur, target)
    loss = lax.psum(0.5 * sq[0, 0] / d, ("x", "y", "c"))

    grads = {name: [None] * weights[name].shape[0] for name in WEIGHTS}
    for i in reversed(range(depth)):
        j = i // 2
        mix_saved, ffn_saved = saved[i]
        s = ffn_saved
        dg, du, dwd, dgain = _ffn_bwd_hidden(f"ffn_bwd_hidden{i}", dcur, s["f"], gain("ffn_post_g", i),
                                             s["g"], s["u"], s["wd"])
        grads["ffn_post_g"][i] = dgain
        grads["ffn_w_down"][i] = dwd
        dcur, dwg, dwu, dgain = _bwd_in(f"ffn_bwd_in{i}", dcur, s["x"], gain("ffn_pre_g", i),
                                        [dg, du], [s["wg"], s["wu"]])
        grads["ffn_pre_g"][i] = dgain
        grads["ffn_w_gate"][i] = dwg
        grads["ffn_w_up"][i] = dwu
        s = mix_saved
        if i % 2 == 0:
            dz, dwout, dws, dbacc, dlng, dlnb, dgain = _mix_a_bwd_hidden(
                f"mix_a_bwd_hidden{j}", dcur, s["m"], gain("mix_post_g", i), s["zpre"], s["ln_g"], s["ln_b"],
                a_w_s[j], s["b_t"], s["w_out"])
            grads["a_w_s"][j] = dws
            grads["a_b_s"][j] = jnp.transpose(dbacc)
            grads["a_ln_g"][j] = dlng.reshape(-1)
            grads["a_ln_b"][j] = dlnb.reshape(-1)
            grads["a_w_out"][j] = dwout.reshape(N_CHIPS, -1, d)
            grads["mix_post_g"][i] = dgain
            dcur, dwin, dgain = _bwd_in(f"mix_a_bwd_in{j}", dcur, s["x"], gain("mix_pre_g", i), [dz], [s["w_in"]])
            grads["a_w_in"][j] = dwin
            grads["mix_pre_g"][i] = dgain
        else:
            dq, dwout, dwgrp, dscale, dgain = _mix_b_bwd_hidden(
                f"mix_b_bwd_hidden{j}", dcur, s["m"], gain("mix_post_g", i), s["p"], s["w_grp"], s["scale"],
                s["w_out"])
            grads["b_w_out"][j] = dwout.reshape(N_CHIPS, -1, d)
            grads["b_w_grp"][j] = jnp.transpose(
                dwgrp.reshape(len(B_WINDOWS), N_CHIPS, gd_b // N_CHIPS, gd_b), (1, 0, 2, 3)).reshape(N_CHIPS, -1, gd_b)
            grads["b_scale"][j] = dscale.reshape(N_CHIPS, 1, -1)
            grads["mix_post_g"][i] = dgain
            dp = _pool_bwd(f"pool_bwd{j}", dq)
            dcur, dwin, dgain = _bwd_in(f"mix_b_bwd_in{j}", dcur, s["x"], gain("mix_pre_g", i), [dp],
                                        [s["w_in"].reshape(1, d, d)])
            grads["b_w_in"][j] = dwin.reshape(N_CHIPS, -1, d)
            grads["mix_pre_g"][i] = dgain
    grad_x = dcur.reshape(x.shape)

    core = lax.axis_index("c").astype(jnp.int32).reshape(1)
    stacked = [jnp.stack(grads[name], axis=1) for name in SHARDED]
    from_sibling = _send_other_half("grads_to_sibling", stacked)
    pair = [_pair_sum("pair_sum_" + name, core, g, r) for name, g, r in zip(SHARDED, stacked, from_sibling)]
    from_chips = _scatter_chips("grads_to_chips", pair)
    halves = [_chip_sum("chip_sum_" + name, p) for name, p in zip(SHARDED, from_chips)]
    reduced = dict(zip(SHARDED, _swap_halves("swap_halves", halves)))

    small_grads = [jnp.stack([g.reshape(weights[name].shape[1:]) for g in grads[name]], axis=0) for name in SMALL]
    summed = _all_sum_small(_pack_small(small_grads))
    reduced["small"] = summed.reshape(1, -1, 128)

    out_g, out_d, out_m, out_v = {}, {}, {}, {}
    for name in SHARDED:
        shape = weights[name].shape
        dlt, nm, nv = _adamw("adamw_" + name, _as_layers(weights[name]), reduced[name],
                             _as_layers(mom1[name]), _as_layers(mom2[name]))
        out_g[name] = reduced[name].reshape(shape)
        out_d[name], out_m[name], out_v[name] = dlt.reshape(shape), nm.reshape(shape), nv.reshape(shape)
    small_like = [weights[name] for name in SMALL]
    packs = [_pack_small([src[name] for name in SMALL]).reshape(1, -1, 128) for src in (weights, mom1, mom2)]
    dlt, nm, nv = _adamw("adamw_small", packs[0], reduced["small"], packs[1], packs[2])
    for dst, packed in ((out_g, summed), (out_d, dlt[0]), (out_m, nm[0]), (out_v, nv[0])):
        for name, val in zip(SMALL, _unpack_small(packed, small_like)):
            dst[name] = val

    return (loss, grad_x, *[out_g[n] for n in WEIGHTS], *[out_d[n] for n in WEIGHTS],
            *[out_m[n] for n in WEIGHTS], *[out_v[n] for n in WEIGHTS])
```

```python
import functools
import math

import jax
import jax.numpy as jnp
from jax import lax
from jax.experimental import pallas as pl
from jax.experimental.pallas import tpu as pltpu

F32 = jnp.float32
BF16 = jnp.bfloat16
MESH = pl.DeviceIdType.MESH

EPS = 1e-6
CHUNK = 128
A_GROUPS = 8
B_WINDOWS = (2, 4, 8, 16)
HALO = 16
N_CHIPS = 4
N_DEV = 8

ADAM_LR = 0.001
ADAM_B1 = 0.9
ADAM_B2 = 0.999
ADAM_EPS = 1e-08
ADAM_WD = 0.01
ADAM_STEP = 10

VMEM_LIMIT_BYTES = 60 * 1024 * 1024
INV_SQRT2 = 1.0 / math.sqrt(2.0)
INV_SQRT_2PI = 1.0 / math.sqrt(2.0 * math.pi)

_ANY = pl.BlockSpec(memory_space=pl.ANY)
_VM = pl.BlockSpec(memory_space=pltpu.VMEM)


def _params():
    return pltpu.CompilerParams(dimension_semantics=("arbitrary",), vmem_limit_bytes=VMEM_LIMIT_BYTES)


def _token_block(t):
    return 256 if t >= 1024 else 128


def _rows(tm, d):
    return pl.BlockSpec((tm, d), lambda i: (i, 0))


def _blocks(nb, tm, bw):
    return pl.BlockSpec((nb, tm, bw), lambda i: (0, i, 0))


def _dot(a, b):
    return lax.dot_general(a, b, (((1,), (0,)), ((), ())), preferred_element_type=F32)


def _dot_nt(a, b):
    return lax.dot_general(a, b, (((1,), (1,)), ((), ())), preferred_element_type=F32)


def _dot_tn(a, b):
    return lax.dot_general(a, b, (((0,), (0,)), ((), ())), preferred_element_type=F32)


def _rms(x, g):
    return x * lax.rsqrt(jnp.mean(x * x, axis=-1, keepdims=True) + EPS) * g


def _rms_bwd(dy, x, g):
    r = lax.rsqrt(jnp.mean(x * x, axis=-1, keepdims=True) + EPS)
    n = x * r
    dn = dy * g
    dx = r * (dn - n * jnp.mean(dn * n, axis=-1, keepdims=True))
    return dx, jnp.sum(dy * n, axis=0, keepdims=True)


def _gelu_and_grad(x):
    cdf = 0.5 * (1.0 + lax.erf(x * INV_SQRT2))
    return x * cdf, cdf + x * (jnp.exp(-0.5 * x * x) * INV_SQRT_2PI)


def _gelu(x):
    return x * (0.5 * (1.0 + lax.erf(x * INV_SQRT2)))


def _ffn_fwd(name, x, gpre, gpost, wg, wu, wd):
    t, d = x.shape
    nb, fs, _ = wg.shape
    tm = _token_block(t)

    def body(x_ref, gpre_ref, gpost_ref, wg_ref, wu_ref, wd_ref, o_ref, g_ref, u_ref, f_ref):
        xv = x_ref[...]
        hb = _rms(xv, gpre_ref[...]).astype(BF16)
        f = jnp.zeros((tm, d), F32)
        for k in range(nb):
            g = _dot_nt(hb, wg_ref[k])
            u = _dot_nt(hb, wu_ref[k])
            g_ref[k] = g
            u_ref[k] = u
            a = (g * jax.nn.sigmoid(g) * u).astype(BF16)
            f = f + _dot(a, wd_ref[k])
        f_ref[...] = f
        o_ref[...] = xv + _rms(f, gpost_ref[...])

    return pl.pallas_call(
        body,
        name=name,
        grid=(t // tm,),
        in_specs=[_rows(tm, d), _VM, _VM, _VM, _VM, _VM],
        out_specs=[_rows(tm, d), _blocks(nb, tm, fs), _blocks(nb, tm, fs), _rows(tm, d)],
        out_shape=[
            jax.ShapeDtypeStruct((t, d), F32),
            jax.ShapeDtypeStruct((nb, t, fs), F32),
            jax.ShapeDtypeStruct((nb, t, fs), F32),
            jax.ShapeDtypeStruct((t, d), F32),
        ],
        compiler_params=_params(),
    )(x, gpre, gpost, wg, wu, wd)


def _ffn_bwd_hidden(name, dy, f, gpost, g_pre, u_pre, wd):
    t, d = dy.shape
    nb, fs, _ = wd.shape
    tm = _token_block(t)

    def body(dy_ref, f_ref, gpost_ref, g_ref, u_ref, wd_ref, dg_ref, du_ref, dwd_ref, dgain_ref):
        @pl.when(pl.program_id(0) == 0)
        def _():
            dwd_ref[...] = jnp.zeros_like(dwd_ref)
            dgain_ref[...] = jnp.zeros_like(dgain_ref)

        df, dgain = _rms_bwd(dy_ref[...], f_ref[...], gpost_ref[...])
        dgain_ref[...] += dgain
        dfb = df.astype(BF16)
        for k in range(nb):
            g = g_ref[k]
            u = u_ref[k]
            s = jax.nn.sigmoid(g)
            sg = g * s
            a = (sg * u).astype(BF16)
            da = _dot_nt(dfb, wd_ref[k])
            dwd_ref[k] += _dot_tn(a, dfb)
            du_ref[k] = (da * sg).astype(BF16)
            dg_ref[k] = (da * u * (s * (1.0 + g * (1.0 - s)))).astype(BF16)

    return pl.pallas_call(
        body,
        name=name,
        grid=(t // tm,),
        in_specs=[_rows(tm, d), _rows(tm, d), _VM, _blocks(nb, tm, fs), _blocks(nb, tm, fs), _VM],
        out_specs=[_blocks(nb, tm, fs), _blocks(nb, tm, fs), _VM, _VM],
        out_shape=[
            jax.ShapeDtypeStruct((nb, t, fs), BF16),
            jax.ShapeDtypeStruct((nb, t, fs), BF16),
            jax.ShapeDtypeStruct((nb, fs, d), F32),
            jax.ShapeDtypeStruct((1, d), F32),
        ],
        compiler_params=_params(),
    )(dy, f, gpost, g_pre, u_pre, wd)


def _bwd_in(name, dres, x, gpre, dzs, ws, transposed=False):
    t, d = x.shape
    n = len(ws)
    tm = _token_block(t)
    widths = [w.shape[1] if transposed else w.shape[2] for w in ws]

    def body(*refs):
        dres_ref, x_ref, gpre_ref = refs[:3]
        dz_refs = refs[3 : 3 + n]
        w_refs = refs[3 + n : 3 + 2 * n]
        dx_ref = refs[3 + 2 * n]
        dw_refs = refs[4 + 2 * n : 4 + 3 * n]
        dgain_ref = refs[4 + 3 * n]

        @pl.when(pl.program_id(0) == 0)
        def _():
            for dw_ref in dw_refs:
                dw_ref[...] = jnp.zeros_like(dw_ref)
            dgain_ref[...] = jnp.zeros_like(dgain_ref)

        xv = x_ref[...]
        gain = gpre_ref[...]
        hb = _rms(xv, gain).astype(BF16)
        dh = jnp.zeros((tm, d), F32)
        for dz_ref, w_ref, dw_ref in zip(dz_refs, w_refs, dw_refs):
            for k in range(w_ref.shape[0]):
                dz = dz_ref[k]
                if transposed:
                    dh = dh + _dot(dz, w_ref[k])
                    dw_ref[k] += _dot_tn(dz, hb)
                else:
                    dh = dh + _dot_nt(dz, w_ref[k])
                    dw_ref[k] += _dot_tn(hb, dz)
        dx, dgain = _rms_bwd(dh, xv, gain)
        dx_ref[...] = dres_ref[...] + dx
        dgain_ref[...] += dgain

    return pl.pallas_call(
        body,
        name=name,
        grid=(t // tm,),
        in_specs=[_rows(tm, d), _rows(tm, d), _VM]
        + [_blocks(w.shape[0], tm, bw) for w, bw in zip(ws, widths)]
        + [_VM] * n,
        out_specs=[_rows(tm, d)] + [_VM] * n + [_VM],
        out_shape=[jax.ShapeDtypeStruct((t, d), F32)]
        + [jax.ShapeDtypeStruct(w.shape, F32) for w in ws]
        + [jax.ShapeDtypeStruct((1, d), F32)],
        compiler_params=_params(),
    )(dres, x, gpre, *dzs, *ws)


def _causal_weights(ws_ref):
    row = lax.broadcasted_iota(jnp.int32, (CHUNK, CHUNK), 0)
    col = lax.broadcasted_iota(jnp.int32, (CHUNK, CHUNK), 1)
    return [jnp.where(row >= col, ws_ref[g], 0.0).astype(BF16) for g in range(A_GROUPS)]


def _layernorm_halves(v0, v1):
    width = v0.shape[-1] + v1.shape[-1]
    mu = (jnp.sum(v0, axis=-1, keepdims=True) + jnp.sum(v1, axis=-1, keepdims=True)) / width
    c0 = v0 - mu
    c1 = v1 - mu
    var = (jnp.sum(c0 * c0, axis=-1, keepdims=True) + jnp.sum(c1 * c1, axis=-1, keepdims=True)) / width
    rstd = lax.rsqrt(var + EPS)
    return c0 * rstd, c1 * rstd, rstd


def _spatial_gate(sv_ref, wtril, vl, bt_ref, half, tm, gd):
    for gg in range(A_GROUPS // 2):
        g = half * (A_GROUPS // 2) + gg
        bias = bt_ref[:, g : g + 1]
        for n in range(tm // CHUNK):
            blk = vl[n * CHUNK : (n + 1) * CHUNK, gg * gd : (gg + 1) * gd]
            sv_ref[n * CHUNK : (n + 1) * CHUNK, gg * gd : (gg + 1) * gd] = _dot(wtril[g], blk) + bias


def _mix_a_fwd(name, x, gpre, gpost, w_in, ln_g, ln_b, w_s, b_t, w_out):
    t, d = x.shape
    _, _, q = w_in.shape
    gd = 2 * q // A_GROUPS
    tm = _token_block(t)

    def body(x_ref, gpre_ref, gpost_ref, win_ref, lng_ref, lnb_ref, ws_ref, bt_ref, wout_ref,
             o_ref, z_ref, m_ref, sv_ref):
        xv = x_ref[...]
        hb = _rms(xv, gpre_ref[...]).astype(BF16)
        z = []
        for k in range(4):
            zp = _dot(hb, win_ref[k])
            z_ref[k] = zp
            z.append(_gelu(zp))
        vh0, vh1, _ = _layernorm_halves(z[2], z[3])
        vls = [(vh * lng_ref[b : b + 1, :] + lnb_ref[b : b + 1, :]).astype(BF16) for b, vh in enumerate((vh0, vh1))]
        wtril = _causal_weights(ws_ref)
        m = jnp.zeros((tm, d), F32)
        for b in range(2):
            _spatial_gate(sv_ref, wtril, vls[b], bt_ref, b, tm, gd)
            gated = (z[b] * sv_ref[...]).astype(BF16)
            m = m + _dot(gated, wout_ref[b])
        m_ref[...] = m
        o_ref[...] = xv + _rms(m, gpost_ref[...])

    return pl.pallas_call(
        body,
        name=name,
        grid=(t // tm,),
        in_specs=[_rows(tm, d)] + [_VM] * 8,
        out_specs=[_rows(tm, d), _blocks(4, tm, q), _rows(tm, d)],
        out_shape=[
            jax.ShapeDtypeStruct((t, d), F32),
            jax.ShapeDtypeStruct((4, t, q), F32),
            jax.ShapeDtypeStruct((t, d), F32),
        ],
        scratch_shapes=[pltpu.VMEM((tm, q), F32)],
        compiler_params=_params(),
    )(x, gpre, gpost, w_in, ln_g, ln_b, w_s, b_t, w_out)


def _mix_a_bwd_hidden(name, dy, m, gpost, zpre, ln_g, ln_b, w_s, b_t, w_out):
    t, d = dy.shape
    _, _, q = zpre.shape
    gd = 2 * q // A_GROUPS
    tm = _token_block(t)
    n_chunks = tm // CHUNK

    def body(dy_ref, m_ref, gpost_ref, z_ref, lng_ref, lnb_ref, ws_ref, bt_ref, wout_ref,
             dz_ref, dwout_ref, dws_ref, dbacc_ref, dlng_ref, dlnb_ref, dgain_ref, sv_ref, dvl_ref):
        first = pl.program_id(0) == 0

        @pl.when(first)
        def _():
            for ref in (dwout_ref, dws_ref, dbacc_ref, dlng_ref, dlnb_ref, dgain_ref):
                ref[...] = jnp.zeros_like(ref)

        dm, dgain = _rms_bwd(dy_ref[...], m_ref[...], gpost_ref[...])
        dgain_ref[...] += dgain
        dmb = dm.astype(BF16)
        v0 = _gelu(z_ref[2])
        v1 = _gelu(z_ref[3])
        vhs = list(_layernorm_halves(v0, v1))
        rstd = vhs.pop()
        vls = [(vh * lng_ref[b : b + 1, :] + lnb_ref[b : b + 1, :]).astype(BF16) for b, vh in enumerate(vhs)]
        wtril = _causal_weights(ws_ref)
        dvhs = []
        for b in range(2):
            zp = z_ref[b]
            u, du_dz = _gelu_and_grad(zp)
            _spatial_gate(sv_ref, wtril, vls[b], bt_ref, b, tm, gd)
            sv = sv_ref[...]
            gated = (u * sv).astype(BF16)
            dgated = _dot_nt(dmb, wout_ref[b])
            dwout_ref[b] += _dot_tn(gated, dmb)
            dz_ref[b] = (dgated * sv * du_dz).astype(BF16)
            dsv = dgated * u
            folded = dsv[0:CHUNK, :]
            for c in range(1, n_chunks):
                folded = folded + dsv[c * CHUNK : (c + 1) * CHUNK, :]
            for gg in range(A_GROUPS // 2):
                g = b * (A_GROUPS // 2) + gg
                dbacc_ref[:, g : g + 1] += jnp.sum(folded[:, gg * gd : (gg + 1) * gd], axis=1, keepdims=True)
            dsvb = dsv.astype(BF16)
            for gg in range(A_GROUPS // 2):
                g = b * (A_GROUPS // 2) + gg
                for c in range(n_chunks):
                    rows = slice(c * CHUNK, (c + 1) * CHUNK)
                    cols = slice(gg * gd, (gg + 1) * gd)
                    blk = dsvb[rows, cols]
                    dvl_ref[rows, cols] = _dot_tn(wtril[g], blk)
                    dws_ref[g] += _dot_nt(blk, vls[b][rows, cols])
            dvl = dvl_ref[...]
            dlng_ref[b : b + 1, :] += jnp.sum(dvl * vhs[b], axis=0, keepdims=True)
            dlnb_ref[b : b + 1, :] += jnp.sum(dvl, axis=0, keepdims=True)
            dvhs.append(dvl * lng_ref[b : b + 1, :])
        width = 2.0 * q
        m1 = (jnp.sum(dvhs[0], axis=-1, keepdims=True) + jnp.sum(dvhs[1], axis=-1, keepdims=True)) / width
        m2 = (jnp.sum(dvhs[0] * vhs[0], axis=-1, keepdims=True)
              + jnp.sum(dvhs[1] * vhs[1], axis=-1, keepdims=True)) / width
        for b in range(2):
            dv = rstd * (dvhs[b] - m1 - vhs[b] * m2)
            _, dv_dz = _gelu_and_grad(z_ref[2 + b])
            dz_ref[2 + b] = (dv * dv_dz).astype(BF16)

        @pl.when(pl.program_id(0) == t // tm - 1)
        def _():
            row = lax.broadcasted_iota(jnp.int32, (CHUNK, CHUNK), 0)
            col = lax.broadcasted_iota(jnp.int32, (CHUNK, CHUNK), 1)
            for g in range(A_GROUPS):
                dws_ref[g] = jnp.where(row >= col, dws_ref[g], 0.0)

    return pl.pallas_call(
        body,
        name=name,
        grid=(t // tm,),
        in_specs=[_rows(tm, d), _rows(tm, d), _VM, _blocks(4, tm, q)] + [_VM] * 5,
        out_specs=[_blocks(4, tm, q)] + [_VM] * 6,
        out_shape=[
            jax.ShapeDtypeStruct((4, t, q), BF16),
            jax.ShapeDtypeStruct((2, q, d), F32),
            jax.ShapeDtypeStruct((A_GROUPS, CHUNK, CHUNK), F32),
            jax.ShapeDtypeStruct((CHUNK, A_GROUPS), F32),
            jax.ShapeDtypeStruct((2, q), F32),
            jax.ShapeDtypeStruct((2, q), F32),
            jax.ShapeDtypeStruct((1, d), F32),
        ],
        scratch_shapes=[pltpu.VMEM((tm, q), F32), pltpu.VMEM((tm, q), F32)],
        compiler_params=_params(),
    )(dy, m, gpost, zpre, ln_g, ln_b, w_s, b_t, w_out)


def _norm_matmul(name, x, gpre, w):
    t, d = x.shape
    n = w.shape[1]
    tm = _token_block(t)

    def body(x_ref, gpre_ref, w_ref, o_ref):
        o_ref[...] = _dot(_rms(x_ref[...], gpre_ref[...]).astype(BF16), w_ref[...])

    return pl.pallas_call(
        body,
        name=name,
        grid=(t // tm,),
        in_specs=[_rows(tm, d), _VM, _VM],
        out_specs=_rows(tm, n),
        out_shape=jax.ShapeDtypeStruct((t, n), F32),
        compiler_params=_params(),
    )(x, gpre, w)


def _window_counts(tm, win):
    pos = pl.program_id(0) * tm + lax.broadcasted_iota(jnp.int32, (tm, 1), 0)
    return jnp.minimum(pos + 1, win).astype(F32)


def _pooled(p, halo, tm, gd):
    prev = jnp.where(pl.program_id(0) == 0, 0.0, halo)
    ext = jnp.concatenate([prev, p], axis=0)
    out = []
    for g, win in enumerate(B_WINDOWS):
        s = ext[:, g * gd : (g + 1) * gd]
        step = 1
        while step < win:
            s = s + pltpu.roll(s, step, 0)
            step *= 2
        total = s[HALO:, :]
        out.append(total / _window_counts(tm, win) - p[:, g * gd : (g + 1) * gd])
    return out


def _halo_spec(t, tm, d, ahead):
    per = tm // HALO
    if ahead:
        return pl.BlockSpec((HALO, d), lambda i: (jnp.minimum((i + 1) * per, t // HALO - 1), 0))
    return pl.BlockSpec((HALO, d), lambda i: (jnp.maximum(i * per - 1, 0), 0))


def _mix_b_fwd(name, x, p, gpost, w_grp, scale, w_out):
    t, d = x.shape
    gd = d // len(B_WINDOWS)
    tm = _token_block(t)

    def body(x_ref, p_ref, halo_ref, gpost_ref, wgrp_ref, scale_ref, wout_ref, o_ref, m_ref):
        pooled = _pooled(p_ref[...], halo_ref[...], tm, gd)
        mixed = jnp.concatenate([_dot(pg.astype(BF16), wgrp_ref[g]) for g, pg in enumerate(pooled)], axis=1)
        m = _dot((mixed * scale_ref[...]).astype(BF16), wout_ref[...])
        m_ref[...] = m
        o_ref[...] = x_ref[...] + _rms(m, gpost_ref[...])

    return pl.pallas_call(
        body,
        name=name,
        grid=(t // tm,),
        in_specs=[_rows(tm, d), _rows(tm, d), _halo_spec(t, tm, d, False), _VM, _VM, _VM, _VM],
        out_specs=[_rows(tm, d), _rows(tm, d)],
        out_shape=[jax.ShapeDtypeStruct((t, d), F32), jax.ShapeDtypeStruct((t, d), F32)],
        compiler_params=_params(),
    )(x, p, p, gpost, w_grp, scale, w_out)


def _mix_b_bwd_hidden(name, dy, m, gpost, p, w_grp, scale, w_out):
    t, d = dy.shape
    gd = d // len(B_WINDOWS)
    tm = _token_block(t)

    def body(dy_ref, m_ref, gpost_ref, p_ref, halo_ref, wgrp_ref, scale_ref, wout_ref,
             dq_ref, dwout_ref, dwgrp_ref, dscale_ref, dgain_ref):
        @pl.when(pl.program_id(0) == 0)
        def _():
            for ref in (dwout_ref, dwgrp_ref, dscale_ref, dgain_ref):
                ref[...] = jnp.zeros_like(ref)

        dm, dgain = _rms_bwd(dy_ref[...], m_ref[...], gpost_ref[...])
        dgain_ref[...] += dgain
        dmb = dm.astype(BF16)
        pooled = [pg.astype(BF16) for pg in _pooled(p_ref[...], halo_ref[...], tm, gd)]
        mixed = jnp.concatenate([_dot(pg, wgrp_ref[g]) for g, pg in enumerate(pooled)], axis=1)
        scale = scale_ref[...]
        ms = (mixed * scale).astype(BF16)
        dms = _dot_nt(dmb, wout_ref[...])
        dwout_ref[...] += _dot_tn(ms, dmb)
        dscale_ref[...] += jnp.sum(dms * mixed, axis=0, keepdims=True)
        dmixed = (dms * scale).astype(BF16)
        for g, win in enumerate(B_WINDOWS):
            cols = slice(g * gd, (g + 1) * gd)
            dmg = dmixed[:, cols]
            dwgrp_ref[g] += _dot_tn(pooled[g], dmg)
            dq_ref[:, cols] = _dot_nt(dmg, wgrp_ref[g]) / _window_counts(tm, win)

    return pl.pallas_call(
        body,
        name=name,
        grid=(t // tm,),
        in_specs=[_rows(tm, d), _rows(tm, d), _VM, _rows(tm, d), _halo_spec(t, tm, d, False), _VM, _VM, _VM],
        out_specs=[_rows(tm, d), _VM, _VM, _VM, _VM],
        out_shape=[
            jax.ShapeDtypeStruct((t, d), F32),
            jax.ShapeDtypeStruct((d, d), F32),
            jax.ShapeDtypeStruct((len(B_WINDOWS), gd, gd), F32),
            jax.ShapeDtypeStruct((1, d), F32),
            jax.ShapeDtypeStruct((1, d), F32),
        ],
        compiler_params=_params(),
    )(dy, m, gpost, p, p, w_grp, scale, w_out)


def _pool_bwd(name, dq):
    t, d = dq.shape
    gd = d // len(B_WINDOWS)
    tm = _token_block(t)
    n_steps = t // tm

    def body(dq_ref, halo_ref, dp_ref):
        dq_blk = dq_ref[...]
        nxt = jnp.where(pl.program_id(0) == n_steps - 1, 0.0, halo_ref[...])
        ext = jnp.concatenate([dq_blk, nxt], axis=0)
        for g, win in enumerate(B_WINDOWS):
            cols = slice(g * gd, (g + 1) * gd)
            s = ext[:, cols]
            step = 1
            while step < win:
                s = s + pltpu.roll(s, tm + HALO - step, 0)
                step *= 2
            dp_ref[0, :, cols] = (s[:tm, :] - dq_blk[:, cols] * _window_counts(tm, win)).astype(BF16)

    return pl.pallas_call(
        body,
        name=name,
        grid=(n_steps,),
        in_specs=[_rows(tm, d), _halo_spec(t, tm, d, True)],
        out_specs=_blocks(1, tm, d),
        out_shape=jax.ShapeDtypeStruct((1, t, d), BF16),
        compiler_params=_params(),
    )(dq, dq)


def _loss_head(y, target):
    t, d = y.shape
    tm = _token_block(t)

    def body(y_ref, t_ref, sq_ref, dy_ref):
        @pl.when(pl.program_id(0) == 0)
        def _():
            sq_ref[...] = jnp.zeros_like(sq_ref)

        err = y_ref[...] - t_ref[...]
        dy_ref[...] = err / d
        sq_ref[...] += jnp.sum(err * err)

    return pl.pallas_call(
        body,
        name="loss_head",
        grid=(t // tm,),
        in_specs=[_rows(tm, d), _rows(tm, d)],
        out_specs=[_VM, _rows(tm, d)],
        out_shape=[jax.ShapeDtypeStruct((8, 128), F32), jax.ShapeDtypeStruct((t, d), F32)],
        compiler_params=_params(),
    )(y, target)


def _cast_into_slot(name, place, w, dtype):
    n_layers, r, c = w.shape

    def body(place_ref, w_ref, o_ref):
        del place_ref
        o_ref[0] = w_ref[...].astype(dtype)

    return pl.pallas_call(
        body,
        name=name,
        grid_spec=pltpu.PrefetchScalarGridSpec(
            num_scalar_prefetch=1,
            grid=(n_layers,),
            in_specs=[pl.BlockSpec((1, r, c), lambda i, place_ref: (i, 0, 0))],
            out_specs=pl.BlockSpec((1, 1, r, c), lambda i, place_ref: (place_ref[0], i, 0, 0)),
        ),
        out_shape=jax.ShapeDtypeStruct((N_CHIPS,) + w.shape, dtype),
        compiler_params=_params(),
    )(place, w)


def _row_tile(r):
    return 256 if r % 256 == 0 else r


def _pair_sum(name, place, dw, recv):
    _, n_layers, r, c = dw.shape
    half = n_layers // 2
    tr = _row_tile(r)

    def body(place_ref, a_ref, b_ref, o_ref):
        del place_ref
        o_ref[...] = (a_ref[...] + b_ref[...]).astype(BF16)

    return pl.pallas_call(
        body,
        name=name,
        grid_spec=pltpu.PrefetchScalarGridSpec(
            num_scalar_prefetch=1,
            grid=(N_CHIPS, half, r // tr),
            in_specs=[
                pl.BlockSpec((1, 1, tr, c), lambda k, l, i, place_ref: (k, place_ref[1] * half + l, i, 0)),
                pl.BlockSpec((1, 1, tr, c), lambda k, l, i, place_ref: (k, l, i, 0)),
            ],
            out_specs=pl.BlockSpec((1, 1, tr, c), lambda k, l, i, place_ref: (k, l, i, 0)),
        ),
        out_shape=jax.ShapeDtypeStruct(recv.shape, BF16),
        compiler_params=pltpu.CompilerParams(
            dimension_semantics=("arbitrary",) * 3, vmem_limit_bytes=VMEM_LIMIT_BYTES),
    )(place, dw, recv)


def _chip_sum(name, place, mine, others):
    _, half, r, c = mine.shape
    tr = _row_tile(r)

    def body(place_ref, a_ref, b_ref, c_ref, d_ref, o_ref):
        del place_ref
        acc = a_ref[...].astype(F32) + b_ref[...].astype(F32)
        acc = acc + c_ref[...].astype(F32)
        o_ref[...] = (acc + d_ref[...].astype(F32))[0]

    def part(flip):
        return pl.BlockSpec((1, 1, tr, c), lambda l, i, place_ref: (jnp.bitwise_xor(place_ref[0], flip), l, i, 0))

    return pl.pallas_call(
        body,
        name=name,
        grid_spec=pltpu.PrefetchScalarGridSpec(
            num_scalar_prefetch=1,
            grid=(half, r // tr),
            in_specs=[part(0), part(1), part(2), part(3)],
            out_specs=pl.BlockSpec((1, tr, c), lambda l, i, place_ref: (place_ref[1] * half + l, i, 0)),
        ),
        out_shape=jax.ShapeDtypeStruct((2 * half, r, c), F32),
        compiler_params=pltpu.CompilerParams(
            dimension_semantics=("arbitrary",) * 2, vmem_limit_bytes=VMEM_LIMIT_BYTES),
    )(place, mine, others, others, others)


def _adamw(name, w, g, m, v):
    n_layers, r, c = w.shape
    tr = _row_tile(r)

    def body(w_ref, g_ref, m_ref, v_ref, d_ref, nm_ref, nv_ref):
        gv = g_ref[...]
        nm = ADAM_B1 * m_ref[...] + (1.0 - ADAM_B1) * gv
        nv = ADAM_B2 * v_ref[...] + (1.0 - ADAM_B2) * jnp.square(gv)
        m_hat = nm / (1.0 - ADAM_B1 ** ADAM_STEP)
        v_hat = nv / (1.0 - ADAM_B2 ** ADAM_STEP)
        d_ref[...] = -ADAM_LR * (m_hat / (jnp.sqrt(v_hat) + ADAM_EPS) + ADAM_WD * w_ref[...])
        nm_ref[...] = nm
        nv_ref[...] = nv

    spec = pl.BlockSpec((1, tr, c), lambda l, i: (l, i, 0))
    return pl.pallas_call(
        body,
        name=name,
        grid=(n_layers, r // tr),
        in_specs=[spec] * 4,
        out_specs=[spec] * 3,
        out_shape=[jax.ShapeDtypeStruct(w.shape, F32)] * 3,
        compiler_params=pltpu.CompilerParams(
            dimension_semantics=("arbitrary",) * 2, vmem_limit_bytes=VMEM_LIMIT_BYTES),
    )(w, g, m, v)


def _position():
    return lax.axis_index("x"), lax.axis_index("y"), lax.axis_index("c")


def _other_chips(x, y):
    return [(1 - x, y), (x, 1 - y), (1 - x, 1 - y)]


def _gather_chips(name, bufs):
    n = len(bufs)

    def body(*refs):
        outs = refs[n : 2 * n]
        ici_send, ici_recv, d2d_send, d2d_recv = refs[2 * n :]
        x, y, c = _position()
        me = 2 * x + y
        chips = _other_chips(x, y)

        def part(w, slot, core):
            half = outs[w].shape[1] // 2
            return outs[w].at[slot, pl.ds(core * half, half)]

        def over_ici(w, j, slot):
            px, py = chips[j]
            return pltpu.make_async_remote_copy(
                src_ref=part(w, slot, c), dst_ref=part(w, slot, c),
                send_sem=ici_send.at[3 * w + j], recv_sem=ici_recv.at[3 * w + j],
                device_id=(px, py, c), device_id_type=MESH)

        def over_d2d(w, j, slot, core):
            return pltpu.make_async_remote_copy(
                src_ref=part(w, slot, core), dst_ref=part(w, slot, core),
                send_sem=d2d_send.at[3 * w + j], recv_sem=d2d_recv.at[3 * w + j],
                device_id=(x, y, 1 - c), device_id_type=MESH)

        sends = [over_ici(w, j, me) for w in range(n) for j in range(3)]
        for cp in sends:
            cp.start()
        passed = []
        for w in range(n):
            for j, (px, py) in enumerate(chips):
                over_ici(w, j, 2 * px + py).wait_recv()
                passed.append(over_d2d(w, j, 2 * px + py, c))
                passed[-1].start()
        for w in range(n):
            for j, (px, py) in enumerate(chips):
                over_d2d(w, j, 2 * px + py, 1 - c).wait_recv()
        for cp in sends + passed:
            cp.wait_send()

    return pl.pallas_call(
        body,
        name=name,
        in_specs=[_ANY] * n,
        out_specs=[_ANY] * n,
        out_shape=[jax.ShapeDtypeStruct(b.shape, b.dtype) for b in bufs],
        input_output_aliases={w: w for w in range(n)},
        scratch_shapes=[pltpu.SemaphoreType.DMA((3 * n,))] * 4,
    )(*bufs)


def _send_other_half(name, grads):
    n = len(grads)

    def body(*refs):
        srcs = refs[:n]
        outs = refs[n : 2 * n]
        send_sems, recv_sems = refs[2 * n :]
        x, y, c = _position()
        copies = []
        for w in range(n):
            half = srcs[w].shape[1] // 2
            copies.append(pltpu.make_async_remote_copy(
                src_ref=srcs[w].at[:, pl.ds((1 - c) * half, half)], dst_ref=outs[w],
                send_sem=send_sems.at[w], recv_sem=recv_sems.at[w],
                device_id=(x, y, 1 - c), device_id_type=MESH))
        for cp in copies:
            cp.start()
        for cp in copies:
            cp.wait_recv()
        for cp in copies:
            cp.wait_send()

    return pl.pallas_call(
        body,
        name=name,
        in_specs=[_ANY] * n,
        out_specs=[_ANY] * n,
        out_shape=[jax.ShapeDtypeStruct((g.shape[0], g.shape[1] // 2) + g.shape[2:], g.dtype) for g in grads],
        scratch_shapes=[pltpu.SemaphoreType.DMA((n,)), pltpu.SemaphoreType.DMA((n,))],
    )(*grads)


def _scatter_chips(name, parts):
    n = len(parts)

    def body(*refs):
        srcs = refs[:n]
        outs = refs[n : 2 * n]
        send_sems, recv_sems = refs[2 * n :]
        x, y, c = _position()
        me = 2 * x + y
        chips = _other_chips(x, y)

        def copy(w, j, src_slot, dst_slot):
            px, py = chips[j]
            return pltpu.make_async_remote_copy(
                src_ref=srcs[w].at[src_slot], dst_ref=outs[w].at[dst_slot],
                send_sem=send_sems.at[3 * w + j], recv_sem=recv_sems.at[3 * w + j],
                device_id=(px, py, c), device_id_type=MESH)

        sends = [copy(w, j, 2 * chips[j][0] + chips[j][1], me) for w in range(n) for j in range(3)]
        for cp in sends:
            cp.start()
        for w in range(n):
            for j, (px, py) in enumerate(chips):
                copy(w, j, me, 2 * px + py).wait_recv()
        for cp in sends:
            cp.wait_send()

    return pl.pallas_call(
        body,
        name=name,
        in_specs=[_ANY] * n,
        out_specs=[_ANY] * n,
        out_shape=[jax.ShapeDtypeStruct(p.shape, p.dtype) for p in parts],
        scratch_shapes=[pltpu.SemaphoreType.DMA((3 * n,)), pltpu.SemaphoreType.DMA((3 * n,))],
    )(*parts)


def _swap_halves(name, bufs):
    n = len(bufs)

    def body(*refs):
        outs = refs[n : 2 * n]
        send_sems, recv_sems = refs[2 * n :]
        x, y, c = _position()

        def copy(w, core):
            half = outs[w].shape[0] // 2
            rows = outs[w].at[pl.ds(core * half, half)]
            return pltpu.make_async_remote_copy(
                src_ref=rows, dst_ref=rows, send_sem=send_sems.at[w], recv_sem=recv_sems.at[w],
                device_id=(x, y, 1 - c), device_id_type=MESH)

        sends = [copy(w, c) for w in range(n)]
        for cp in sends:
            cp.start()
        for w in range(n):
            copy(w, 1 - c).wait_recv()
        for cp in sends:
            cp.wait_send()

    return pl.pallas_call(
        body,
        name=name,
        in_specs=[_ANY] * n,
        out_specs=[_ANY] * n,
        out_shape=[jax.ShapeDtypeStruct(b.shape, b.dtype) for b in bufs],
        input_output_aliases={w: w for w in range(n)},
        scratch_shapes=[pltpu.SemaphoreType.DMA((n,)), pltpu.SemaphoreType.DMA((n,))],
    )(*bufs)


def _all_sum_small(packed):
    m_per, n = packed.shape

    def body(x_ref, sum_ref, all_ref, send_sems, recv_sems, local_sem):
        x, y, c = _position()
        me, sibling = (x, y, c), (x, y, 1 - c)
        chips = _other_chips(x, y)

        def rows(px, py, pc):
            return all_ref.at[pl.ds((4 * px + 2 * py + pc) * m_per, m_per), :]

        def copy(k, block, to, src=None):
            return pltpu.make_async_remote_copy(
                src_ref=rows(*block) if src is None else src, dst_ref=rows(*block),
                send_sem=send_sems.at[k], recv_sem=recv_sems.at[k], device_id=to, device_id_type=MESH)

        mine = pltpu.make_async_copy(x_ref, rows(*me), local_sem)
        mine.start()
        first = [copy(0, me, sibling, src=x_ref)]
        first += [copy(1 + j, me, (*chip, c), src=x_ref) for j, chip in enumerate(chips)]
        for cp in first:
            cp.start()
        passed = [copy(4 + j, (*chip, c), sibling) for j, chip in enumerate(chips)]
        for j, chip in enumerate(chips):
            copy(1 + j, (*chip, c), me).wait_recv()
            passed[j].start()
        copy(0, sibling, me).wait_recv()
        for j, chip in enumerate(chips):
            copy(4 + j, (*chip, 1 - c), me).wait_recv()
        for cp in first + passed:
            cp.wait_send()
        mine.wait()
        acc = all_ref[0:m_per, :]
        for k in range(1, N_DEV):
            acc = acc + all_ref[k * m_per : (k + 1) * m_per, :]
        sum_ref[...] = acc

    return pl.pallas_call(
        body,
        name="all_sum_small",
        in_specs=[_VM],
        out_specs=_VM,
        out_shape=jax.ShapeDtypeStruct((m_per, n), F32),
        scratch_shapes=[
            pltpu.VMEM((N_DEV * m_per, n), F32),
            pltpu.SemaphoreType.DMA((7,)),
            pltpu.SemaphoreType.DMA((7,)),
            pltpu.SemaphoreType.DMA,
        ],
        compiler_params=pltpu.CompilerParams(vmem_limit_bytes=VMEM_LIMIT_BYTES),
    )(packed)


SHARDED = ("a_w_in", "a_w_out", "b_w_in", "b_w_grp", "b_scale", "b_w_out", "ffn_w_gate", "ffn_w_up", "ffn_w_down")
SMALL = ("a_ln_g", "a_ln_b", "a_w_s", "a_b_s", "mix_pre_g", "mix_post_g", "ffn_pre_g", "ffn_post_g")
WEIGHTS = ("a_w_in", "a_ln_g", "a_ln_b", "a_w_s", "a_b_s", "a_w_out", "b_w_in", "b_w_grp", "b_scale", "b_w_out",
           "mix_pre_g", "mix_post_g", "ffn_pre_g", "ffn_post_g", "ffn_w_gate", "ffn_w_up", "ffn_w_down")


TRANSPOSED = ("ffn_w_gate", "ffn_w_up")


def _as_layers(name, a):
    if name in TRANSPOSED:
        return jnp.swapaxes(a, 1, 2)
    if a.ndim == 2:
        return a.reshape(a.shape[0], 1, a.shape[1])
    return a.reshape(a.shape[0], -1, a.shape[-1])


def _from_layers(name, a, shape):
    if name in TRANSPOSED:
        return jnp.swapaxes(a, 1, 2)
    return a.reshape(shape)


def _pack_small(parts):
    return jnp.concatenate([p.reshape(-1, 128) for p in parts], axis=0)


def _unpack_small(packed, like):
    out, row = [], 0
    for ref in like:
        rows = ref.size // 128
        out.append(packed[row : row + rows].reshape(ref.shape))
        row += rows
    return out


def kernel(x, a_w_in, a_ln_g, a_ln_b, a_w_s, a_b_s, a_w_out, b_w_in, b_w_grp, b_scale, b_w_out, mix_pre_g, mix_post_g, ffn_pre_g, ffn_post_g, ffn_w_gate, ffn_w_up, ffn_w_down, loss_target, m_a_w_in, m_a_ln_g, m_a_ln_b, m_a_w_s, m_a_b_s, m_a_w_out, m_b_w_in, m_b_w_grp, m_b_scale, m_b_w_out, m_mix_pre_g, m_mix_post_g, m_ffn_pre_g, m_ffn_post_g, m_ffn_w_gate, m_ffn_w_up, m_ffn_w_down, v_a_w_in, v_a_ln_g, v_a_ln_b, v_a_w_s, v_a_b_s, v_a_w_out, v_b_w_in, v_b_w_grp, v_b_scale, v_b_w_out, v_mix_pre_g, v_mix_post_g, v_ffn_pre_g, v_ffn_post_g, v_ffn_w_gate, v_ffn_w_up, v_ffn_w_down):
    weights = dict(a_w_in=a_w_in, a_ln_g=a_ln_g, a_ln_b=a_ln_b, a_w_s=a_w_s, a_b_s=a_b_s, a_w_out=a_w_out,
                   b_w_in=b_w_in, b_w_grp=b_w_grp, b_scale=b_scale, b_w_out=b_w_out, mix_pre_g=mix_pre_g,
                   mix_post_g=mix_post_g, ffn_pre_g=ffn_pre_g, ffn_post_g=ffn_post_g, ffn_w_gate=ffn_w_gate,
                   ffn_w_up=ffn_w_up, ffn_w_down=ffn_w_down)
    mom1 = dict(a_w_in=m_a_w_in, a_ln_g=m_a_ln_g, a_ln_b=m_a_ln_b, a_w_s=m_a_w_s, a_b_s=m_a_b_s, a_w_out=m_a_w_out,
                b_w_in=m_b_w_in, b_w_grp=m_b_w_grp, b_scale=m_b_scale, b_w_out=m_b_w_out, mix_pre_g=m_mix_pre_g,
                mix_post_g=m_mix_post_g, ffn_pre_g=m_ffn_pre_g, ffn_post_g=m_ffn_post_g, ffn_w_gate=m_ffn_w_gate,
                ffn_w_up=m_ffn_w_up, ffn_w_down=m_ffn_w_down)
    mom2 = dict(a_w_in=v_a_w_in, a_ln_g=v_a_ln_g, a_ln_b=v_a_ln_b, a_w_s=v_a_w_s, a_b_s=v_a_b_s, a_w_out=v_a_w_out,
                b_w_in=v_b_w_in, b_w_grp=v_b_w_grp, b_scale=v_b_scale, b_w_out=v_b_w_out, mix_pre_g=v_mix_pre_g,
                mix_post_g=v_mix_post_g, ffn_pre_g=v_ffn_pre_g, ffn_post_g=v_ffn_post_g, ffn_w_gate=v_ffn_w_gate,
                ffn_w_up=v_ffn_w_up, ffn_w_down=v_ffn_w_down)

    t, d = x.shape[1], x.shape[2]
    depth = mix_pre_g.shape[0]
    gd_b = d // len(B_WINDOWS)
    xs = x.reshape(t, d)
    target = loss_target.reshape(t, d)

    chip = 2 * lax.axis_index("x") + lax.axis_index("y")
    place = jnp.stack([chip, lax.axis_index("c")]).astype(jnp.int32)
    slots = [_cast_into_slot("cast_" + name, place, _as_layers(name, weights[name]),
                             F32 if name == "b_scale" else BF16) for name in SHARDED]
    full = dict(zip(SHARDED, _gather_chips("gather_weights", slots)))

    def gain(name, i):
        return weights[name][i].reshape(1, d)

    saved = []
    cur = xs
    for i in range(depth):
        j = i // 2
        if i % 2 == 0:
            w_in = full["a_w_in"][:, j]
            q = w_in.shape[2]
            w_out = full["a_w_out"][:, j].reshape(2, q, d)
            ln_g = a_ln_g[j].reshape(2, q)
            ln_b = a_ln_b[j].reshape(2, q)
            b_t = jnp.transpose(a_b_s[j])
            nxt, zpre, m = _mix_a_fwd(f"mix_a_fwd{j}", cur, gain("mix_pre_g", i), gain("mix_post_g", i),
                                      w_in, ln_g, ln_b, a_w_s[j], b_t, w_out)
            mix_saved = dict(x=cur, zpre=zpre, m=m, w_in=w_in, w_out=w_out, ln_g=ln_g, ln_b=ln_b, b_t=b_t)
        else:
            w_in = full["b_w_in"][:, j].reshape(d, d)
            w_out = full["b_w_out"][:, j].reshape(d, d)
            w_grp = jnp.transpose(full["b_w_grp"][:, j].reshape(N_CHIPS, len(B_WINDOWS), gd_b // N_CHIPS, gd_b),
                                  (1, 0, 2, 3)).reshape(len(B_WINDOWS), gd_b, gd_b)
            scale = full["b_scale"][:, j].reshape(1, d)
            p = _norm_matmul(f"mix_b_in{j}", cur, gain("mix_pre_g", i), w_in)
            nxt, m = _mix_b_fwd(f"mix_b_fwd{j}", cur, p, gain("mix_post_g", i), w_grp, scale, w_out)
            mix_saved = dict(x=cur, p=p, m=m, w_in=w_in, w_out=w_out, w_grp=w_grp, scale=scale)
        cur = nxt
        wg, wu, wd = full["ffn_w_gate"][:, i], full["ffn_w_up"][:, i], full["ffn_w_down"][:, i]
        nxt, g_pre, u_pre, f = _ffn_fwd(f"ffn_fwd{i}", cur, gain("ffn_pre_g", i), gain("ffn_post_g", i), wg, wu, wd)
        saved.append((mix_saved, dict(x=cur, g=g_pre, u=u_pre, f=f, wg=wg, wu=wu, wd=wd)))
        cur = nxt

    sq, dcur = _loss_head(cur, target)
    loss = lax.psum(0.5 * sq[0, 0] / d, ("x", "y", "c"))

    grads = {name: [None] * weights[name].shape[0] for name in WEIGHTS}
    for i in reversed(range(depth)):
        j = i // 2
        mix_saved, ffn_saved = saved[i]
        s = ffn_saved
        dg, du, dwd, dgain = _ffn_bwd_hidden(f"ffn_bwd_hidden{i}", dcur, s["f"], gain("ffn_post_g", i),
                                             s["g"], s["u"], s["wd"])
        grads["ffn_post_g"][i] = dgain
        grads["ffn_w_down"][i] = dwd
        dcur, dwg, dwu, dgain = _bwd_in(f"ffn_bwd_in{i}", dcur, s["x"], gain("ffn_pre_g", i),
                                        [dg, du], [s["wg"], s["wu"]], transposed=True)
        grads["ffn_pre_g"][i] = dgain
        grads["ffn_w_gate"][i] = dwg
        grads["ffn_w_up"][i] = dwu
        s = mix_saved
        if i % 2 == 0:
            dz, dwout, dws, dbacc, dlng, dlnb, dgain = _mix_a_bwd_hidden(
                f"mix_a_bwd_hidden{j}", dcur, s["m"], gain("mix_post_g", i), s["zpre"], s["ln_g"], s["ln_b"],
                a_w_s[j], s["b_t"], s["w_out"])
            grads["a_w_s"][j] = dws
            grads["a_b_s"][j] = jnp.transpose(dbacc)
            grads["a_ln_g"][j] = dlng.reshape(-1)
            grads["a_ln_b"][j] = dlnb.reshape(-1)
            grads["a_w_out"][j] = dwout.reshape(N_CHIPS, -1, d)
            grads["mix_post_g"][i] = dgain
            dcur, dwin, dgain = _bwd_in(f"mix_a_bwd_in{j}", dcur, s["x"], gain("mix_pre_g", i), [dz], [s["w_in"]])
            grads["a_w_in"][j] = dwin
            grads["mix_pre_g"][i] = dgain
        else:
            dq, dwout, dwgrp, dscale, dgain = _mix_b_bwd_hidden(
                f"mix_b_bwd_hidden{j}", dcur, s["m"], gain("mix_post_g", i), s["p"], s["w_grp"], s["scale"],
                s["w_out"])
            grads["b_w_out"][j] = dwout.reshape(N_CHIPS, -1, d)
            grads["b_w_grp"][j] = jnp.transpose(
                dwgrp.reshape(len(B_WINDOWS), N_CHIPS, gd_b // N_CHIPS, gd_b), (1, 0, 2, 3)).reshape(N_CHIPS, -1, gd_b)
            grads["b_scale"][j] = dscale.reshape(N_CHIPS, 1, -1)
            grads["mix_post_g"][i] = dgain
            dp = _pool_bwd(f"pool_bwd{j}", dq)
            dcur, dwin, dgain = _bwd_in(f"mix_b_bwd_in{j}", dcur, s["x"], gain("mix_pre_g", i), [dp],
                                        [s["w_in"].reshape(1, d, d)])
            grads["b_w_in"][j] = dwin.reshape(N_CHIPS, -1, d)
            grads["mix_pre_g"][i] = dgain
    grad_x = dcur.reshape(x.shape)

    stacked = [jnp.stack(grads[name], axis=1) for name in SHARDED]
    from_sibling = _send_other_half("grads_to_sibling", stacked)
    pair = [_pair_sum("pair_sum_" + name, place, g, r) for name, g, r in zip(SHARDED, stacked, from_sibling)]
    from_chips = _scatter_chips("grads_to_chips", pair)
    halves = [_chip_sum("chip_sum_" + name, place, p, o) for name, p, o in zip(SHARDED, pair, from_chips)]
    reduced = dict(zip(SHARDED, _swap_halves("swap_halves", halves)))

    small_grads = [jnp.stack([g.reshape(weights[name].shape[1:]) for g in grads[name]], axis=0) for name in SMALL]
    summed = _all_sum_small(_pack_small(small_grads))
    reduced["small"] = summed.reshape(1, -1, 128)

    out_g, out_d, out_m, out_v = {}, {}, {}, {}
    for name in SHARDED:
        shape = weights[name].shape
        dlt, nm, nv = _adamw("adamw_" + name, _as_layers(name, weights[name]), reduced[name],
                             _as_layers(name, mom1[name]), _as_layers(name, mom2[name]))
        out_g[name] = _from_layers(name, reduced[name], shape)
        out_d[name], out_m[name], out_v[name] = (_from_layers(name, a, shape) for a in (dlt, nm, nv))
    small_like = [weights[name] for name in SMALL]
    packs = [_pack_small([src[name] for name in SMALL]).reshape(1, -1, 128) for src in (weights, mom1, mom2)]
    dlt, nm, nv = _adamw("adamw_small", packs[0], reduced["small"], packs[1], packs[2])
    for dst, packed in ((out_g, summed), (out_d, dlt[0]), (out_m, nm[0]), (out_v, nv[0])):
        for name, val in zip(SMALL, _unpack_small(packed, small_like)):
            dst[name] = val

    return (loss, grad_x, *[out_g[n] for n in WEIGHTS], *[out_d[n] for n in WEIGHTS],
            *[out_m[n] for n in WEIGHTS], *[out_v[n] for n in WEIGHTS])
```

```python
import functools
import math

import jax
import jax.numpy as jnp
from jax import lax
from jax.experimental import pallas as pl
from jax.experimental.pallas import tpu as pltpu

F32 = jnp.float32
BF16 = jnp.bfloat16
MESH = pl.DeviceIdType.MESH

EPS = 1e-6
CHUNK = 128
A_GROUPS = 8
B_WINDOWS = (2, 4, 8, 16)
HALO = 16
N_CHIPS = 4
N_DEV = 8

ADAM_LR = 0.001
ADAM_B1 = 0.9
ADAM_B2 = 0.999
ADAM_EPS = 1e-08
ADAM_WD = 0.01
ADAM_STEP = 10

VMEM_LIMIT_BYTES = 60 * 1024 * 1024
INV_SQRT2 = 1.0 / math.sqrt(2.0)
INV_SQRT_2PI = 1.0 / math.sqrt(2.0 * math.pi)

_ANY = pl.BlockSpec(memory_space=pl.ANY)
_VM = pl.BlockSpec(memory_space=pltpu.VMEM)


def _params():
    return pltpu.CompilerParams(dimension_semantics=("arbitrary",), vmem_limit_bytes=VMEM_LIMIT_BYTES)


def _token_block(t):
    return 256 if t >= 1024 else 128


def _rows(tm, d):
    return pl.BlockSpec((tm, d), lambda i: (i, 0))


def _blocks(nb, tm, bw):
    return pl.BlockSpec((nb, tm, bw), lambda i: (0, i, 0))


def _dot(a, b):
    return lax.dot_general(a, b, (((1,), (0,)), ((), ())), preferred_element_type=F32)


def _dot_nt(a, b):
    return lax.dot_general(a, b, (((1,), (1,)), ((), ())), preferred_element_type=F32)


def _dot_tn(a, b):
    return lax.dot_general(a, b, (((0,), (0,)), ((), ())), preferred_element_type=F32)


def _rms(x, g):
    return x * lax.rsqrt(jnp.mean(x * x, axis=-1, keepdims=True) + EPS) * g


def _rms_bwd(dy, x, g):
    r = lax.rsqrt(jnp.mean(x * x, axis=-1, keepdims=True) + EPS)
    n = x * r
    dn = dy * g
    dx = r * (dn - n * jnp.mean(dn * n, axis=-1, keepdims=True))
    return dx, jnp.sum(dy * n, axis=0, keepdims=True)


def _gelu_and_grad(x):
    cdf = 0.5 * (1.0 + lax.erf(x * INV_SQRT2))
    return x * cdf, cdf + x * (jnp.exp(-0.5 * x * x) * INV_SQRT_2PI)


def _gelu(x):
    return x * (0.5 * (1.0 + lax.erf(x * INV_SQRT2)))


def _position():
    return lax.axis_index("x"), lax.axis_index("y"), lax.axis_index("c")


def _other_chips(x, y):
    return [(1 - x, y), (x, 1 - y), (1 - x, 1 - y)]


class _GatherWeights:
    def __init__(self, bufs, whole=()):
        self.inputs = list(bufs)
        self.out_shapes = [jax.ShapeDtypeStruct(b.shape, b.dtype) for b in bufs]
        self.aliases = {w: w for w in range(len(bufs))}
        self.n_sems = 6 * len(bufs)
        self.whole = frozenset(whole)

    def _part(self, outs, w, slot, core):
        if w in self.whole:
            return outs[w].at[slot]
        half = outs[w].shape[1] // 2
        return outs[w].at[slot, pl.ds(core * half, half)]

    def _ici(self, outs, send, recv, w, j, slot):
        x, y, c = _position()
        px, py = _other_chips(x, y)[j]
        part = self._part(outs, w, slot, c)
        return pltpu.make_async_remote_copy(
            src_ref=part, dst_ref=part, send_sem=send.at[6 * w + j], recv_sem=recv.at[6 * w + j],
            device_id=(px, py, c), device_id_type=MESH)

    def _d2d(self, outs, send, recv, w, j, slot, core):
        x, y, c = _position()
        part = self._part(outs, w, slot, core)
        return pltpu.make_async_remote_copy(
            src_ref=part, dst_ref=part, send_sem=send.at[6 * w + 3 + j], recv_sem=recv.at[6 * w + 3 + j],
            device_id=(x, y, 1 - c), device_id_type=MESH)

    def start(self, ins, outs, send, recv):
        x, y, _ = _position()
        for w in range(len(outs)):
            for j in range(3):
                self._ici(outs, send, recv, w, j, 2 * x + y).start()

    def finish(self, ins, outs, send, recv):
        x, y, c = _position()
        slots = [2 * px + py for px, py in _other_chips(x, y)]
        for w in range(len(outs)):
            for j, slot in enumerate(slots):
                self._ici(outs, send, recv, w, j, slot).wait_recv()
                if w not in self.whole:
                    self._d2d(outs, send, recv, w, j, slot, c).start()
        for w in range(len(outs)):
            for j, slot in enumerate(slots):
                if w not in self.whole:
                    self._d2d(outs, send, recv, w, j, slot, 1 - c).wait_recv()
        for w in range(len(outs)):
            for j, slot in enumerate(slots):
                self._ici(outs, send, recv, w, j, 2 * x + y).wait_send()
                if w not in self.whole:
                    self._d2d(outs, send, recv, w, j, slot, c).wait_send()


class _ToSibling:
    def __init__(self, grads):
        self.inputs = list(grads)
        self.out_shapes = [jax.ShapeDtypeStruct((g.shape[0], g.shape[1] // 2, g.shape[2]), g.dtype) for g in grads]
        self.aliases = {}
        self.n_sems = len(grads)

    def _copy(self, ins, outs, send, recv, w):
        x, y, c = _position()
        half = ins[w].shape[1] // 2
        return pltpu.make_async_remote_copy(
            src_ref=ins[w].at[:, pl.ds((1 - c) * half, half)], dst_ref=outs[w],
            send_sem=send.at[w], recv_sem=recv.at[w], device_id=(x, y, 1 - c), device_id_type=MESH)

    def start(self, ins, outs, send, recv):
        for w in range(len(ins)):
            self._copy(ins, outs, send, recv, w).start()

    def finish(self, ins, outs, send, recv):
        for w in range(len(ins)):
            self._copy(ins, outs, send, recv, w).wait_recv()
        for w in range(len(ins)):
            self._copy(ins, outs, send, recv, w).wait_send()


class _ToChips:
    def __init__(self, parts):
        self.inputs = list(parts)
        self.out_shapes = [jax.ShapeDtypeStruct(p.shape, p.dtype) for p in parts]
        self.aliases = {}
        self.n_sems = 3 * len(parts)

    def _copy(self, ins, outs, send, recv, w, j, outbound):
        x, y, c = _position()
        px, py = _other_chips(x, y)[j]
        me, peer = 2 * x + y, 2 * px + py
        src_slot, dst_slot = (peer, me) if outbound else (me, peer)
        return pltpu.make_async_remote_copy(
            src_ref=ins[w].at[src_slot], dst_ref=outs[w].at[dst_slot],
            send_sem=send.at[3 * w + j], recv_sem=recv.at[3 * w + j], device_id=(px, py, c), device_id_type=MESH)

    def start(self, ins, outs, send, recv):
        for w in range(len(ins)):
            for j in range(3):
                self._copy(ins, outs, send, recv, w, j, True).start()

    def finish(self, ins, outs, send, recv):
        for w in range(len(ins)):
            for j in range(3):
                self._copy(ins, outs, send, recv, w, j, False).wait_recv()
        for w in range(len(ins)):
            for j in range(3):
                self._copy(ins, outs, send, recv, w, j, True).wait_send()


class _SwapHalves:
    def __init__(self, bufs):
        self.inputs = list(bufs)
        self.out_shapes = [jax.ShapeDtypeStruct(b.shape, b.dtype) for b in bufs]
        self.aliases = {w: w for w in range(len(bufs))}
        self.n_sems = len(bufs)

    def _copy(self, outs, send, recv, w, core):
        x, y, c = _position()
        half = outs[w].shape[1] // 2
        rows = outs[w].at[:, pl.ds(core * half, half)]
        return pltpu.make_async_remote_copy(
            src_ref=rows, dst_ref=rows, send_sem=send.at[w], recv_sem=recv.at[w],
            device_id=(x, y, 1 - c), device_id_type=MESH)

    def start(self, ins, outs, send, recv):
        c = lax.axis_index("c")
        for w in range(len(outs)):
            self._copy(outs, send, recv, w, c).start()

    def finish(self, ins, outs, send, recv):
        c = lax.axis_index("c")
        for w in range(len(outs)):
            self._copy(outs, send, recv, w, 1 - c).wait_recv()
        for w in range(len(outs)):
            self._copy(outs, send, recv, w, c).wait_send()


def _call(name, body, *, grid, in_specs, out_specs, out_shape, args, scratch_shapes=(), exchanges=()):
    given = list(exchanges)
    exchanges = [e for e in given if e.inputs]
    n_in, n_out, n_scr = len(args), len(out_shape), len(scratch_shapes)
    ex_in = [a for e in exchanges for a in e.inputs]
    ex_out = [s for e in exchanges for s in e.out_shapes]
    aliases = {}
    at_in, at_out = n_in, n_out
    for e in exchanges:
        for i, o in e.aliases.items():
            aliases[at_in + i] = at_out + o
        at_in += len(e.inputs)
        at_out += len(e.out_shapes)
    n_steps = grid[0]

    def fused(*refs):
        body_in, refs = refs[:n_in], refs[n_in:]
        ex_in_refs, refs = refs[: len(ex_in)], refs[len(ex_in) :]
        body_out, refs = refs[:n_out], refs[n_out:]
        ex_out_refs, refs = refs[: len(ex_out)], refs[len(ex_out) :]
        body_scr, sems = refs[:n_scr], refs[n_scr:]

        def each(stage):
            a = b = 0
            for n, e in enumerate(exchanges):
                ins, outs = ex_in_refs[a : a + len(e.inputs)], ex_out_refs[b : b + len(e.out_shapes)]
                getattr(e, stage)(ins, outs, sems[2 * n], sems[2 * n + 1])
                a += len(e.inputs)
                b += len(e.out_shapes)

        if exchanges:
            @pl.when(pl.program_id(0) == 0)
            def _():
                each("start")

        if body is not None:
            body(*body_in, *body_out, *body_scr)

        if exchanges:
            @pl.when(pl.program_id(0) == n_steps - 1)
            def _():
                each("finish")

    outs = pl.pallas_call(
        fused,
        name=name,
        grid=grid,
        in_specs=list(in_specs) + [_ANY] * len(ex_in),
        out_specs=list(out_specs) + [_ANY] * len(ex_out),
        out_shape=list(out_shape) + ex_out,
        input_output_aliases=aliases,
        scratch_shapes=list(scratch_shapes)
        + [pltpu.SemaphoreType.DMA((e.n_sems,)) for e in exchanges for _ in range(2)],
        compiler_params=_params(),
    )(*args, *ex_in)
    body_outs, rest = list(outs[:n_out]), list(outs[n_out:])
    ex_outs = []
    for e in given:
        n_e = len(e.out_shapes) if e.inputs else 0
        ex_outs.append(rest[:n_e])
        rest = rest[n_e:]
    return body_outs, ex_outs


def _exchange(name, exchanges):
    return _call(name, None, grid=(1,), in_specs=[], out_specs=[], out_shape=[], args=[], exchanges=exchanges)[1]


def _ffn_fwd(name, x, gpre, gpost, wg, wu, wd, exchanges=()):
    t, d = x.shape
    nb, fs, _ = wg.shape
    tm = _token_block(t)

    def body(x_ref, gpre_ref, gpost_ref, wg_ref, wu_ref, wd_ref, o_ref, g_ref, u_ref, f_ref):
        xv = x_ref[...]
        hb = _rms(xv, gpre_ref[...]).astype(BF16)
        f = jnp.zeros((tm, d), F32)
        for k in range(nb):
            g = _dot_nt(hb, wg_ref[k])
            u = _dot_nt(hb, wu_ref[k])
            g_ref[k] = g
            u_ref[k] = u
            a = (g * jax.nn.sigmoid(g) * u).astype(BF16)
            f = f + _dot(a, wd_ref[k])
        f_ref[...] = f
        o_ref[...] = xv + _rms(f, gpost_ref[...])

    return _call(
        name,
        body,
        grid=(t // tm,),
        in_specs=[_rows(tm, d), _VM, _VM, _VM, _VM, _VM],
        out_specs=[_rows(tm, d), _blocks(nb, tm, fs), _blocks(nb, tm, fs), _rows(tm, d)],
        out_shape=[
            jax.ShapeDtypeStruct((t, d), F32),
            jax.ShapeDtypeStruct((nb, t, fs), F32),
            jax.ShapeDtypeStruct((nb, t, fs), F32),
            jax.ShapeDtypeStruct((t, d), F32),
        ],
        args=[x, gpre, gpost, wg, wu, wd],
        exchanges=exchanges,
    )


def _ffn_bwd_hidden(name, dy, f, gpost, g_pre, u_pre, wd, exchanges=()):
    t, d = dy.shape
    nb, fs, _ = wd.shape
    tm = _token_block(t)

    def body(dy_ref, f_ref, gpost_ref, g_ref, u_ref, wd_ref, dg_ref, du_ref, dwd_ref, dgain_ref):
        @pl.when(pl.program_id(0) == 0)
        def _():
            dwd_ref[...] = jnp.zeros_like(dwd_ref)
            dgain_ref[...] = jnp.zeros_like(dgain_ref)

        df, dgain = _rms_bwd(dy_ref[...], f_ref[...], gpost_ref[...])
        dgain_ref[...] += dgain
        dfb = df.astype(BF16)
        for k in range(nb):
            g = g_ref[k]
            u = u_ref[k]
            s = jax.nn.sigmoid(g)
            sg = g * s
            a = (sg * u).astype(BF16)
            da = _dot_nt(dfb, wd_ref[k])
            dwd_ref[k] += _dot_tn(a, dfb)
            du_ref[k] = (da * sg).astype(BF16)
            dg_ref[k] = (da * u * (s * (1.0 + g * (1.0 - s)))).astype(BF16)

    return _call(
        name,
        body,
        grid=(t // tm,),
        in_specs=[_rows(tm, d), _rows(tm, d), _VM, _blocks(nb, tm, fs), _blocks(nb, tm, fs), _VM],
        out_specs=[_blocks(nb, tm, fs), _blocks(nb, tm, fs), _VM, _VM],
        out_shape=[
            jax.ShapeDtypeStruct((nb, t, fs), BF16),
            jax.ShapeDtypeStruct((nb, t, fs), BF16),
            jax.ShapeDtypeStruct((nb, fs, d), F32),
            jax.ShapeDtypeStruct((1, d), F32),
        ],
        args=[dy, f, gpost, g_pre, u_pre, wd],
        exchanges=exchanges,
    )


def _bwd_in(name, dres, x, gpre, dzs, ws, transposed=False, exchanges=()):
    t, d = x.shape
    n = len(ws)
    tm = _token_block(t)
    widths = [w.shape[1] if transposed else w.shape[2] for w in ws]

    def body(*refs):
        dres_ref, x_ref, gpre_ref = refs[:3]
        dz_refs = refs[3 : 3 + n]
        w_refs = refs[3 + n : 3 + 2 * n]
        dx_ref = refs[3 + 2 * n]
        dw_refs = refs[4 + 2 * n : 4 + 3 * n]
        dgain_ref = refs[4 + 3 * n]

        @pl.when(pl.program_id(0) == 0)
        def _():
            for dw_ref in dw_refs:
                dw_ref[...] = jnp.zeros_like(dw_ref)
            dgain_ref[...] = jnp.zeros_like(dgain_ref)

        xv = x_ref[...]
        gain = gpre_ref[...]
        hb = _rms(xv, gain).astype(BF16)
        dh = jnp.zeros((tm, d), F32)
        for dz_ref, w_ref, dw_ref in zip(dz_refs, w_refs, dw_refs):
            for k in range(w_ref.shape[0]):
                dz = dz_ref[k]
                if transposed:
                    dh = dh + _dot(dz, w_ref[k])
                    dw_ref[k] += _dot_tn(dz, hb)
                else:
                    dh = dh + _dot_nt(dz, w_ref[k])
                    dw_ref[k] += _dot_tn(hb, dz)
        dx, dgain = _rms_bwd(dh, xv, gain)
        dx_ref[...] = dres_ref[...] + dx
        dgain_ref[...] += dgain

    return _call(
        name,
        body,
        grid=(t // tm,),
        in_specs=[_rows(tm, d), _rows(tm, d), _VM]
        + [_blocks(w.shape[0], tm, bw) for w, bw in zip(ws, widths)]
        + [_VM] * n,
        out_specs=[_rows(tm, d)] + [_VM] * n + [_VM],
        out_shape=[jax.ShapeDtypeStruct((t, d), F32)]
        + [jax.ShapeDtypeStruct(w.shape, F32) for w in ws]
        + [jax.ShapeDtypeStruct((1, d), F32)],
        args=[dres, x, gpre, *dzs, *ws],
        exchanges=exchanges,
    )


def _causal_weights(ws_ref):
    row = lax.broadcasted_iota(jnp.int32, (CHUNK, CHUNK), 0)
    col = lax.broadcasted_iota(jnp.int32, (CHUNK, CHUNK), 1)
    return [jnp.where(row >= col, ws_ref[g], 0.0).astype(BF16) for g in range(A_GROUPS)]


def _layernorm_halves(v0, v1):
    width = v0.shape[-1] + v1.shape[-1]
    mu = (jnp.sum(v0, axis=-1, keepdims=True) + jnp.sum(v1, axis=-1, keepdims=True)) / width
    c0 = v0 - mu
    c1 = v1 - mu
    var = (jnp.sum(c0 * c0, axis=-1, keepdims=True) + jnp.sum(c1 * c1, axis=-1, keepdims=True)) / width
    rstd = lax.rsqrt(var + EPS)
    return c0 * rstd, c1 * rstd, rstd


def _spatial_gate(sv_ref, wtril, vl, bt_ref, half, tm, gd):
    for gg in range(A_GROUPS // 2):
        g = half * (A_GROUPS // 2) + gg
        bias = bt_ref[:, g : g + 1]
        for n in range(tm // CHUNK):
            blk = vl[n * CHUNK : (n + 1) * CHUNK, gg * gd : (gg + 1) * gd]
            sv_ref[n * CHUNK : (n + 1) * CHUNK, gg * gd : (gg + 1) * gd] = _dot(wtril[g], blk) + bias


def _mix_a_fwd(name, x, gpre, gpost, w_in, ln_g, ln_b, w_s, b_t, w_out, exchanges=()):
    t, d = x.shape
    _, _, q = w_in.shape
    gd = 2 * q // A_GROUPS
    tm = _token_block(t)

    def body(x_ref, gpre_ref, gpost_ref, win_ref, lng_ref, lnb_ref, ws_ref, bt_ref, wout_ref,
             o_ref, z_ref, m_ref, sv_ref):
        xv = x_ref[...]
        hb = _rms(xv, gpre_ref[...]).astype(BF16)
        z = []
        for k in range(4):
            zp = _dot(hb, win_ref[k])
            z_ref[k] = zp
            z.append(_gelu(zp))
        vh0, vh1, _ = _layernorm_halves(z[2], z[3])
        vls = [(vh * lng_ref[b : b + 1, :] + lnb_ref[b : b + 1, :]).astype(BF16) for b, vh in enumerate((vh0, vh1))]
        wtril = _causal_weights(ws_ref)
        m = jnp.zeros((tm, d), F32)
        for b in range(2):
            _spatial_gate(sv_ref, wtril, vls[b], bt_ref, b, tm, gd)
            gated = (z[b] * sv_ref[...]).astype(BF16)
            m = m + _dot(gated, wout_ref[b])
        m_ref[...] = m
        o_ref[...] = xv + _rms(m, gpost_ref[...])

    return _call(
        name,
        body,
        grid=(t // tm,),
        in_specs=[_rows(tm, d)] + [_VM] * 8,
        out_specs=[_rows(tm, d), _blocks(4, tm, q), _rows(tm, d)],
        out_shape=[
            jax.ShapeDtypeStruct((t, d), F32),
            jax.ShapeDtypeStruct((4, t, q), F32),
            jax.ShapeDtypeStruct((t, d), F32),
        ],
        scratch_shapes=[pltpu.VMEM((tm, q), F32)],
        args=[x, gpre, gpost, w_in, ln_g, ln_b, w_s, b_t, w_out],
        exchanges=exchanges,
    )


def _mix_a_bwd_hidden(name, dy, m, gpost, zpre, ln_g, ln_b, w_s, b_t, w_out, exchanges=()):
    t, d = dy.shape
    _, _, q = zpre.shape
    gd = 2 * q // A_GROUPS
    tm = _token_block(t)
    n_chunks = tm // CHUNK

    def body(dy_ref, m_ref, gpost_ref, z_ref, lng_ref, lnb_ref, ws_ref, bt_ref, wout_ref,
             dz_ref, dwout_ref, dws_ref, dbacc_ref, dlng_ref, dlnb_ref, dgain_ref, sv_ref, dvl_ref):
        first = pl.program_id(0) == 0

        @pl.when(first)
        def _():
            for ref in (dwout_ref, dws_ref, dbacc_ref, dlng_ref, dlnb_ref, dgain_ref):
                ref[...] = jnp.zeros_like(ref)

        dm, dgain = _rms_bwd(dy_ref[...], m_ref[...], gpost_ref[...])
        dgain_ref[...] += dgain
        dmb = dm.astype(BF16)
        v0 = _gelu(z_ref[2])
        v1 = _gelu(z_ref[3])
        vhs = list(_layernorm_halves(v0, v1))
        rstd = vhs.pop()
        vls = [(vh * lng_ref[b : b + 1, :] + lnb_ref[b : b + 1, :]).astype(BF16) for b, vh in enumerate(vhs)]
        wtril = _causal_weights(ws_ref)
        dvhs = []
        for b in range(2):
            zp = z_ref[b]
            u, du_dz = _gelu_and_grad(zp)
            _spatial_gate(sv_ref, wtril, vls[b], bt_ref, b, tm, gd)
            sv = sv_ref[...]
            gated = (u * sv).astype(BF16)
            dgated = _dot_nt(dmb, wout_ref[b])
            dwout_ref[b] += _dot_tn(gated, dmb)
            dz_ref[b] = (dgated * sv * du_dz).astype(BF16)
            dsv = dgated * u
            folded = dsv[0:CHUNK, :]
            for c in range(1, n_chunks):
                folded = folded + dsv[c * CHUNK : (c + 1) * CHUNK, :]
            for gg in range(A_GROUPS // 2):
                g = b * (A_GROUPS // 2) + gg
                dbacc_ref[:, g : g + 1] += jnp.sum(folded[:, gg * gd : (gg + 1) * gd], axis=1, keepdims=True)
            dsvb = dsv.astype(BF16)
            for gg in range(A_GROUPS // 2):
                g = b * (A_GROUPS // 2) + gg
                for c in range(n_chunks):
                    rows = slice(c * CHUNK, (c + 1) * CHUNK)
                    cols = slice(gg * gd, (gg + 1) * gd)
                    blk = dsvb[rows, cols]
                    dvl_ref[rows, cols] = _dot_tn(wtril[g], blk)
                    dws_ref[g] += _dot_nt(blk, vls[b][rows, cols])
            dvl = dvl_ref[...]
            dlng_ref[b : b + 1, :] += jnp.sum(dvl * vhs[b], axis=0, keepdims=True)
            dlnb_ref[b : b + 1, :] += jnp.sum(dvl, axis=0, keepdims=True)
            dvhs.append(dvl * lng_ref[b : b + 1, :])
        width = 2.0 * q
        m1 = (jnp.sum(dvhs[0], axis=-1, keepdims=True) + jnp.sum(dvhs[1], axis=-1, keepdims=True)) / width
        m2 = (jnp.sum(dvhs[0] * vhs[0], axis=-1, keepdims=True)
              + jnp.sum(dvhs[1] * vhs[1], axis=-1, keepdims=True)) / width
        for b in range(2):
            dv = rstd * (dvhs[b] - m1 - vhs[b] * m2)
            _, dv_dz = _gelu_and_grad(z_ref[2 + b])
            dz_ref[2 + b] = (dv * dv_dz).astype(BF16)

        @pl.when(pl.program_id(0) == t // tm - 1)
        def _():
            row = lax.broadcasted_iota(jnp.int32, (CHUNK, CHUNK), 0)
            col = lax.broadcasted_iota(jnp.int32, (CHUNK, CHUNK), 1)
            for g in range(A_GROUPS):
                dws_ref[g] = jnp.where(row >= col, dws_ref[g], 0.0)

    return _call(
        name,
        body,
        grid=(t // tm,),
        in_specs=[_rows(tm, d), _rows(tm, d), _VM, _blocks(4, tm, q)] + [_VM] * 5,
        out_specs=[_blocks(4, tm, q)] + [_VM] * 6,
        out_shape=[
            jax.ShapeDtypeStruct((4, t, q), BF16),
            jax.ShapeDtypeStruct((2, q, d), F32),
            jax.ShapeDtypeStruct((A_GROUPS, CHUNK, CHUNK), F32),
            jax.ShapeDtypeStruct((CHUNK, A_GROUPS), F32),
            jax.ShapeDtypeStruct((2, q), F32),
            jax.ShapeDtypeStruct((2, q), F32),
            jax.ShapeDtypeStruct((1, d), F32),
        ],
        scratch_shapes=[pltpu.VMEM((tm, q), F32), pltpu.VMEM((tm, q), F32)],
        args=[dy, m, gpost, zpre, ln_g, ln_b, w_s, b_t, w_out],
        exchanges=exchanges,
    )


def _norm_matmul(name, x, gpre, w, exchanges=()):
    t, d = x.shape
    n = w.shape[1]
    tm = _token_block(t)

    def body(x_ref, gpre_ref, w_ref, o_ref):
        o_ref[...] = _dot(_rms(x_ref[...], gpre_ref[...]).astype(BF16), w_ref[...])

    return _call(
        name,
        body,
        grid=(t // tm,),
        in_specs=[_rows(tm, d), _VM, _VM],
        out_specs=[_rows(tm, n)],
        out_shape=[jax.ShapeDtypeStruct((t, n), F32)],
        args=[x, gpre, w],
        exchanges=exchanges,
    )


def _window_counts(tm, win):
    pos = pl.program_id(0) * tm + lax.broadcasted_iota(jnp.int32, (tm, 1), 0)
    return jnp.minimum(pos + 1, win).astype(F32)


def _pooled(p, halo, tm, gd):
    prev = jnp.where(pl.program_id(0) == 0, 0.0, halo)
    ext = jnp.concatenate([prev, p], axis=0)
    out = []
    for g, win in enumerate(B_WINDOWS):
        s = ext[:, g * gd : (g + 1) * gd]
        step = 1
        while step < win:
            s = s + pltpu.roll(s, step, 0)
            step *= 2
        total = s[HALO:, :]
        out.append(total / _window_counts(tm, win) - p[:, g * gd : (g + 1) * gd])
    return out


def _halo_spec(t, tm, d, ahead):
    per = tm // HALO
    if ahead:
        return pl.BlockSpec((HALO, d), lambda i: (jnp.minimum((i + 1) * per, t // HALO - 1), 0))
    return pl.BlockSpec((HALO, d), lambda i: (jnp.maximum(i * per - 1, 0), 0))


def _mix_b_fwd(name, x, p, gpost, w_grp, scale, w_out, exchanges=()):
    t, d = x.shape
    gd = d // len(B_WINDOWS)
    tm = _token_block(t)

    def body(x_ref, p_ref, halo_ref, gpost_ref, wgrp_ref, scale_ref, wout_ref, o_ref, m_ref):
        pooled = _pooled(p_ref[...], halo_ref[...], tm, gd)
        mixed = jnp.concatenate([_dot(pg.astype(BF16), wgrp_ref[g]) for g, pg in enumerate(pooled)], axis=1)
        m = _dot((mixed * scale_ref[...]).astype(BF16), wout_ref[...])
        m_ref[...] = m
        o_ref[...] = x_ref[...] + _rms(m, gpost_ref[...])

    return _call(
        name,
        body,
        grid=(t // tm,),
        in_specs=[_rows(tm, d), _rows(tm, d), _halo_spec(t, tm, d, False), _VM, _VM, _VM, _VM],
        out_specs=[_rows(tm, d), _rows(tm, d)],
        out_shape=[jax.ShapeDtypeStruct((t, d), F32), jax.ShapeDtypeStruct((t, d), F32)],
        args=[x, p, p, gpost, w_grp, scale, w_out],
        exchanges=exchanges,
    )


def _mix_b_bwd_hidden(name, dy, m, gpost, p, w_grp, scale, w_out, exchanges=()):
    t, d = dy.shape
    gd = d // len(B_WINDOWS)
    tm = _token_block(t)

    def body(dy_ref, m_ref, gpost_ref, p_ref, halo_ref, wgrp_ref, scale_ref, wout_ref,
             dq_ref, dwout_ref, dwgrp_ref, dscale_ref, dgain_ref):
        @pl.when(pl.program_id(0) == 0)
        def _():
            for ref in (dwout_ref, dwgrp_ref, dscale_ref, dgain_ref):
                ref[...] = jnp.zeros_like(ref)

        dm, dgain = _rms_bwd(dy_ref[...], m_ref[...], gpost_ref[...])
        dgain_ref[...] += dgain
        dmb = dm.astype(BF16)
        pooled = [pg.astype(BF16) for pg in _pooled(p_ref[...], halo_ref[...], tm, gd)]
        mixed = jnp.concatenate([_dot(pg, wgrp_ref[g]) for g, pg in enumerate(pooled)], axis=1)
        scale = scale_ref[...]
        ms = (mixed * scale).astype(BF16)
        dms = _dot_nt(dmb, wout_ref[...])
        dwout_ref[...] += _dot_tn(ms, dmb)
        dscale_ref[...] += jnp.sum(dms * mixed, axis=0, keepdims=True)
        dmixed = (dms * scale).astype(BF16)
        for g, win in enumerate(B_WINDOWS):
            cols = slice(g * gd, (g + 1) * gd)
            dmg = dmixed[:, cols]
            dwgrp_ref[g] += _dot_tn(pooled[g], dmg)
            dq_ref[:, cols] = _dot_nt(dmg, wgrp_ref[g]) / _window_counts(tm, win)

    return _call(
        name,
        body,
        grid=(t // tm,),
        in_specs=[_rows(tm, d), _rows(tm, d), _VM, _rows(tm, d), _halo_spec(t, tm, d, False), _VM, _VM, _VM],
        out_specs=[_rows(tm, d), _VM, _VM, _VM, _VM],
        out_shape=[
            jax.ShapeDtypeStruct((t, d), F32),
            jax.ShapeDtypeStruct((d, d), F32),
            jax.ShapeDtypeStruct((len(B_WINDOWS), gd, gd), F32),
            jax.ShapeDtypeStruct((1, d), F32),
            jax.ShapeDtypeStruct((1, d), F32),
        ],
        args=[dy, m, gpost, p, p, w_grp, scale, w_out],
        exchanges=exchanges,
    )


def _pool_bwd(name, dq, exchanges=()):
    t, d = dq.shape
    gd = d // len(B_WINDOWS)
    tm = _token_block(t)
    n_steps = t // tm

    def body(dq_ref, halo_ref, dp_ref):
        dq_blk = dq_ref[...]
        nxt = jnp.where(pl.program_id(0) == n_steps - 1, 0.0, halo_ref[...])
        ext = jnp.concatenate([dq_blk, nxt], axis=0)
        for g, win in enumerate(B_WINDOWS):
            cols = slice(g * gd, (g + 1) * gd)
            s = ext[:, cols]
            step = 1
            while step < win:
                s = s + pltpu.roll(s, tm + HALO - step, 0)
                step *= 2
            dp_ref[0, :, cols] = (s[:tm, :] - dq_blk[:, cols] * _window_counts(tm, win)).astype(BF16)

    return _call(
        name,
        body,
        grid=(n_steps,),
        in_specs=[_rows(tm, d), _halo_spec(t, tm, d, True)],
        out_specs=[_blocks(1, tm, d)],
        out_shape=[jax.ShapeDtypeStruct((1, t, d), BF16)],
        args=[dq, dq],
        exchanges=exchanges,
    )


def _loss_head(y, target):
    t, d = y.shape
    tm = _token_block(t)

    def body(y_ref, t_ref, sq_ref, dy_ref):
        @pl.when(pl.program_id(0) == 0)
        def _():
            sq_ref[...] = jnp.zeros_like(sq_ref)

        err = y_ref[...] - t_ref[...]
        dy_ref[...] = err / d
        sq_ref[...] += jnp.sum(err * err)

    return pl.pallas_call(
        body,
        name="loss_head",
        grid=(t // tm,),
        in_specs=[_rows(tm, d), _rows(tm, d)],
        out_specs=[_VM, _rows(tm, d)],
        out_shape=[jax.ShapeDtypeStruct((8, 128), F32), jax.ShapeDtypeStruct((t, d), F32)],
        compiler_params=_params(),
    )(y, target)


def _cast_into_slots(name, place, w, dtype):
    n_layers, r, c = w.shape

    def body(place_ref, w_ref, *o_refs):
        del place_ref
        for j, o_ref in enumerate(o_refs):
            @pl.when(pl.program_id(0) == j)
            def _():
                o_ref[...] = w_ref[...].astype(dtype)

    return pl.pallas_call(
        body,
        name=name,
        grid_spec=pltpu.PrefetchScalarGridSpec(
            num_scalar_prefetch=1,
            grid=(n_layers,),
            in_specs=[pl.BlockSpec((1, r, c), lambda i, place_ref: (i, 0, 0))],
            out_specs=[pl.BlockSpec((1, r, c), lambda i, place_ref: (place_ref[0], 0, 0))] * n_layers,
        ),
        out_shape=[jax.ShapeDtypeStruct((N_CHIPS, r, c), dtype)] * n_layers,
        compiler_params=_params(),
    )(place, w)


def _row_tile(r):
    return 256 if r % 256 == 0 else r


def _pair_sum(name, place, dw, recv):
    _, r, c = dw.shape
    half = r // 2
    tr = _row_tile(half)
    per = half // tr

    def body(place_ref, a_ref, b_ref, o_ref):
        del place_ref
        o_ref[...] = (a_ref[...] + b_ref[...]).astype(BF16)

    return pl.pallas_call(
        body,
        name=name,
        grid_spec=pltpu.PrefetchScalarGridSpec(
            num_scalar_prefetch=1,
            grid=(N_CHIPS, per),
            in_specs=[
                pl.BlockSpec((1, tr, c), lambda k, i, place_ref: (k, place_ref[1] * per + i, 0)),
                pl.BlockSpec((1, tr, c), lambda k, i, place_ref: (k, i, 0)),
            ],
            out_specs=pl.BlockSpec((1, tr, c), lambda k, i, place_ref: (k, i, 0)),
        ),
        out_shape=jax.ShapeDtypeStruct(recv.shape, BF16),
        compiler_params=pltpu.CompilerParams(
            dimension_semantics=("arbitrary",) * 2, vmem_limit_bytes=VMEM_LIMIT_BYTES),
    )(place, dw, recv)


def _chip_sum(name, place, mine, others):
    n_layers = len(mine)
    _, half, c = mine[0].shape
    tr = _row_tile(half)
    per = half // tr

    def body(place_ref, *refs):
        del place_ref
        o_ref = refs[-1]
        for j in range(n_layers):
            @pl.when(pl.program_id(0) == j)
            def _():
                parts = refs[4 * j : 4 * j + 4]
                acc = parts[0][...].astype(F32) + parts[1][...].astype(F32)
                acc = acc + parts[2][...].astype(F32)
                o_ref[...] = acc + parts[3][...].astype(F32)

    def part(j, flip):
        return pl.BlockSpec((1, tr, c), lambda l, i, place_ref: (
            jnp.bitwise_xor(place_ref[0], flip), jnp.where(l == j, i, 0), 0))

    args = []
    for j in range(n_layers):
        args += [mine[j], others[j], others[j], others[j]]
    return pl.pallas_call(
        body,
        name=name,
        grid_spec=pltpu.PrefetchScalarGridSpec(
            num_scalar_prefetch=1,
            grid=(n_layers, per),
            in_specs=[part(j, flip) for j in range(n_layers) for flip in range(N_CHIPS)],
            out_specs=pl.BlockSpec((1, tr, c), lambda l, i, place_ref: (l, place_ref[1] * per + i, 0)),
        ),
        out_shape=jax.ShapeDtypeStruct((n_layers, 2 * half, c), F32),
        compiler_params=pltpu.CompilerParams(
            dimension_semantics=("arbitrary",) * 2, vmem_limit_bytes=VMEM_LIMIT_BYTES),
    )(place, *args)


def _adamw(name, w, g, m, v):
    n_layers, r, c = w.shape
    tr = _row_tile(r)

    def body(w_ref, g_ref, m_ref, v_ref, d_ref, nm_ref, nv_ref):
        gv = g_ref[...]
        nm = ADAM_B1 * m_ref[...] + (1.0 - ADAM_B1) * gv
        nv = ADAM_B2 * v_ref[...] + (1.0 - ADAM_B2) * jnp.square(gv)
        m_hat = nm / (1.0 - ADAM_B1 ** ADAM_STEP)
        v_hat = nv / (1.0 - ADAM_B2 ** ADAM_STEP)
        d_ref[...] = -ADAM_LR * (m_hat / (jnp.sqrt(v_hat) + ADAM_EPS) + ADAM_WD * w_ref[...])
        nm_ref[...] = nm
        nv_ref[...] = nv

    spec = pl.BlockSpec((1, tr, c), lambda l, i: (l, i, 0))
    return pl.pallas_call(
        body,
        name=name,
        grid=(n_layers, r // tr),
        in_specs=[spec] * 4,
        out_specs=[spec] * 3,
        out_shape=[jax.ShapeDtypeStruct(w.shape, F32)] * 3,
        compiler_params=pltpu.CompilerParams(
            dimension_semantics=("arbitrary",) * 2, vmem_limit_bytes=VMEM_LIMIT_BYTES),
    )(w, g, m, v)


def _all_sum_small(packed):
    m_per, n = packed.shape

    def body(x_ref, sum_ref, all_ref, send_sems, recv_sems, local_sem):
        x, y, c = _position()
        me, sibling = (x, y, c), (x, y, 1 - c)
        chips = _other_chips(x, y)

        def rows(px, py, pc):
            return all_ref.at[pl.ds((4 * px + 2 * py + pc) * m_per, m_per), :]

        def copy(k, block, to, src=None):
            return pltpu.make_async_remote_copy(
                src_ref=rows(*block) if src is None else src, dst_ref=rows(*block),
                send_sem=send_sems.at[k], recv_sem=recv_sems.at[k], device_id=to, device_id_type=MESH)

        mine = pltpu.make_async_copy(x_ref, rows(*me), local_sem)
        mine.start()
        first = [copy(0, me, sibling, src=x_ref)]
        first += [copy(1 + j, me, (*chip, c), src=x_ref) for j, chip in enumerate(chips)]
        for cp in first:
            cp.start()
        passed = [copy(4 + j, (*chip, c), sibling) for j, chip in enumerate(chips)]
        for j, chip in enumerate(chips):
            copy(1 + j, (*chip, c), me).wait_recv()
            passed[j].start()
        copy(0, sibling, me).wait_recv()
        for j, chip in enumerate(chips):
            copy(4 + j, (*chip, 1 - c), me).wait_recv()
        for cp in first + passed:
            cp.wait_send()
        mine.wait()
        acc = all_ref[0:m_per, :]
        for k in range(1, N_DEV):
            acc = acc + all_ref[k * m_per : (k + 1) * m_per, :]
        sum_ref[...] = acc

    return pl.pallas_call(
        body,
        name="all_sum_small",
        in_specs=[_VM],
        out_specs=_VM,
        out_shape=jax.ShapeDtypeStruct((m_per, n), F32),
        scratch_shapes=[
            pltpu.VMEM((N_DEV * m_per, n), F32),
            pltpu.SemaphoreType.DMA((7,)),
            pltpu.SemaphoreType.DMA((7,)),
            pltpu.SemaphoreType.DMA,
        ],
        compiler_params=pltpu.CompilerParams(vmem_limit_bytes=VMEM_LIMIT_BYTES),
    )(packed)


SHARDED = ("a_w_in", "a_w_out", "b_w_in", "b_w_grp", "b_scale", "b_w_out", "ffn_w_gate", "ffn_w_up", "ffn_w_down")
SMALL = ("a_ln_g", "a_ln_b", "a_w_s", "a_b_s", "mix_pre_g", "mix_post_g", "ffn_pre_g", "ffn_post_g")
WEIGHTS = ("a_w_in", "a_ln_g", "a_ln_b", "a_w_s", "a_b_s", "a_w_out", "b_w_in", "b_w_grp", "b_scale", "b_w_out",
           "mix_pre_g", "mix_post_g", "ffn_pre_g", "ffn_post_g", "ffn_w_gate", "ffn_w_up", "ffn_w_down")


TRANSPOSED = ("ffn_w_gate", "ffn_w_up")


def _as_layers(name, a):
    if name in TRANSPOSED:
        return jnp.swapaxes(a, 1, 2)
    if a.ndim == 2:
        return a.reshape(a.shape[0], 1, a.shape[1])
    return a.reshape(a.shape[0], -1, a.shape[-1])


def _from_layers(name, a, shape):
    if name in TRANSPOSED:
        return jnp.swapaxes(a, 1, 2)
    return a.reshape(shape)


def _pack_small(parts):
    return jnp.concatenate([p.reshape(-1, 128) for p in parts], axis=0)


def _unpack_small(packed, like):
    out, row = [], 0
    for ref in like:
        rows = ref.size // 128
        out.append(packed[row : row + rows].reshape(ref.shape))
        row += rows
    return out


def kernel(x, a_w_in, a_ln_g, a_ln_b, a_w_s, a_b_s, a_w_out, b_w_in, b_w_grp, b_scale, b_w_out, mix_pre_g, mix_post_g, ffn_pre_g, ffn_post_g, ffn_w_gate, ffn_w_up, ffn_w_down, loss_target, m_a_w_in, m_a_ln_g, m_a_ln_b, m_a_w_s, m_a_b_s, m_a_w_out, m_b_w_in, m_b_w_grp, m_b_scale, m_b_w_out, m_mix_pre_g, m_mix_post_g, m_ffn_pre_g, m_ffn_post_g, m_ffn_w_gate, m_ffn_w_up, m_ffn_w_down, v_a_w_in, v_a_ln_g, v_a_ln_b, v_a_w_s, v_a_b_s, v_a_w_out, v_b_w_in, v_b_w_grp, v_b_scale, v_b_w_out, v_mix_pre_g, v_mix_post_g, v_ffn_pre_g, v_ffn_post_g, v_ffn_w_gate, v_ffn_w_up, v_ffn_w_down):
    weights = dict(a_w_in=a_w_in, a_ln_g=a_ln_g, a_ln_b=a_ln_b, a_w_s=a_w_s, a_b_s=a_b_s, a_w_out=a_w_out,
                   b_w_in=b_w_in, b_w_grp=b_w_grp, b_scale=b_scale, b_w_out=b_w_out, mix_pre_g=mix_pre_g,
                   mix_post_g=mix_post_g, ffn_pre_g=ffn_pre_g, ffn_post_g=ffn_post_g, ffn_w_gate=ffn_w_gate,
                   ffn_w_up=ffn_w_up, ffn_w_down=ffn_w_down)
    mom1 = dict(a_w_in=m_a_w_in, a_ln_g=m_a_ln_g, a_ln_b=m_a_ln_b, a_w_s=m_a_w_s, a_b_s=m_a_b_s, a_w_out=m_a_w_out,
                b_w_in=m_b_w_in, b_w_grp=m_b_w_grp, b_scale=m_b_scale, b_w_out=m_b_w_out, mix_pre_g=m_mix_pre_g,
                mix_post_g=m_mix_post_g, ffn_pre_g=m_ffn_pre_g, ffn_post_g=m_ffn_post_g, ffn_w_gate=m_ffn_w_gate,
                ffn_w_up=m_ffn_w_up, ffn_w_down=m_ffn_w_down)
    mom2 = dict(a_w_in=v_a_w_in, a_ln_g=v_a_ln_g, a_ln_b=v_a_ln_b, a_w_s=v_a_w_s, a_b_s=v_a_b_s, a_w_out=v_a_w_out,
                b_w_in=v_b_w_in, b_w_grp=v_b_w_grp, b_scale=v_b_scale, b_w_out=v_b_w_out, mix_pre_g=v_mix_pre_g,
                mix_post_g=v_mix_post_g, ffn_pre_g=v_ffn_pre_g, ffn_post_g=v_ffn_post_g, ffn_w_gate=v_ffn_w_gate,
                ffn_w_up=v_ffn_w_up, ffn_w_down=v_ffn_w_down)

    t, d = x.shape[1], x.shape[2]
    depth = mix_pre_g.shape[0]
    gd_b = d // len(B_WINDOWS)
    xs = x.reshape(t, d)
    target = loss_target.reshape(t, d)

    chip = 2 * lax.axis_index("x") + lax.axis_index("y")
    place = jnp.stack([chip, lax.axis_index("c")]).astype(jnp.int32)
    bufs = {name: list(_cast_into_slots("cast_" + name, place, _as_layers(name, weights[name]),
                                        F32 if name == "b_scale" else BF16)) for name in SHARDED}

    def gain(name, i):
        return weights[name][i].reshape(1, d)

    def weight_keys(i):
        j = i // 2
        mixer = [("a_w_in", j), ("a_w_out", j)] if i % 2 == 0 else [("b_w_in", j), ("b_w_grp", j), ("b_w_out", j)]
        return mixer, [("ffn_w_gate", i), ("ffn_w_up", i), ("ffn_w_down", i)]

    def gather(keys):
        return _GatherWeights([bufs[n][j] for n, j in keys],
                              whole=[k for k, (n, _) in enumerate(keys) if n == "b_scale"])

    def gathered(keys, outs):
        for (n, j), buf in zip(keys, outs):
            bufs[n][j] = buf

    first = weight_keys(0)[0] + [("b_scale", j) for j in range(b_scale.shape[0])]
    gathered(first, _exchange("gather_first", [gather(first)])[0])
    saved = []
    cur = xs
    for i in range(depth):
        j = i // 2
        mixer_next, ffn_next = weight_keys(i + 1) if i + 1 < depth else ([], [])
        ffn_keys = weight_keys(i)[1]
        after_ffn = mixer_next if (i + 1) % 2 == 0 else mixer_next + ffn_next
        if i % 2 == 0:
            w_in = bufs["a_w_in"][j]
            q = w_in.shape[2]
            w_out = bufs["a_w_out"][j].reshape(2, q, d)
            ln_g = a_ln_g[j].reshape(2, q)
            ln_b = a_ln_b[j].reshape(2, q)
            b_t = jnp.transpose(a_b_s[j])
            (nxt, zpre, m), (got,) = _mix_a_fwd(
                f"mix_a_fwd{j}", cur, gain("mix_pre_g", i), gain("mix_post_g", i), w_in, ln_g, ln_b, a_w_s[j], b_t,
                w_out, exchanges=[gather(ffn_keys)])
            gathered(ffn_keys, got)
            mix_saved = dict(x=cur, zpre=zpre, m=m, w_in=w_in, w_out=w_out, ln_g=ln_g, ln_b=ln_b, b_t=b_t)
        else:
            w_in = bufs["b_w_in"][j].reshape(d, d)
            w_out = bufs["b_w_out"][j].reshape(d, d)
            w_grp = jnp.transpose(bufs["b_w_grp"][j].reshape(N_CHIPS, len(B_WINDOWS), gd_b // N_CHIPS, gd_b),
                                  (1, 0, 2, 3)).reshape(len(B_WINDOWS), gd_b, gd_b)
            scale = bufs["b_scale"][j].reshape(1, d)
            (p,), _ = _norm_matmul(f"mix_b_in{j}", cur, gain("mix_pre_g", i), w_in)
            (nxt, m), _ = _mix_b_fwd(f"mix_b_fwd{j}", cur, p, gain("mix_post_g", i), w_grp, scale, w_out)
            mix_saved = dict(x=cur, p=p, m=m, w_in=w_in, w_out=w_out, w_grp=w_grp, scale=scale)
        cur = nxt
        wg, wu, wd = (bufs[n][k] for n, k in ffn_keys)
        (nxt, g_pre, u_pre, f), (got,) = _ffn_fwd(
            f"ffn_fwd{i}", cur, gain("ffn_pre_g", i), gain("ffn_post_g", i), wg, wu, wd, exchanges=[gather(after_ffn)])
        gathered(after_ffn, got)
        saved.append((mix_saved, dict(x=cur, g=g_pre, u=u_pre, f=f, wg=wg, wu=wu, wd=wd)))
        cur = nxt

    sq, dcur = _loss_head(cur, target)
    loss = lax.psum(0.5 * sq[0, 0] / d, ("x", "y", "c"))

    grads = {name: [None] * weights[name].shape[0] for name in WEIGHTS}
    state = dict(to_sibling=[], to_chips=[])
    pair, from_chips = {}, {}

    def exchanges_due():
        return [_ToSibling([a for _, _, a in state["to_sibling"]]), _ToChips([a for _, _, a in state["to_chips"]])]

    def exchanged(outs):
        from_sibling, arrived = outs
        for (n, k, _), got in zip(state["to_chips"], arrived):
            from_chips[n, k] = got
        state["to_chips"] = []
        for (n, k, dw), got in zip(state["to_sibling"], from_sibling):
            pair[n, k] = _pair_sum(f"pair_sum_{n}{k}", place, dw, got)
            state["to_chips"].append((n, k, pair[n, k]))
        state["to_sibling"] = []

    def made(name, k, dw):
        grads[name][k] = dw
        state["to_sibling"].append((name, k, dw))

    for i in reversed(range(depth)):
        j = i // 2
        mix_saved, ffn_saved = saved[i]
        s = ffn_saved
        (dg, du, dwd, dgain), outs = _ffn_bwd_hidden(
            f"ffn_bwd_hidden{i}", dcur, s["f"], gain("ffn_post_g", i), s["g"], s["u"], s["wd"],
            exchanges=exchanges_due())
        exchanged(outs)
        grads["ffn_post_g"][i] = dgain
        made("ffn_w_down", i, dwd)
        (dcur, dwg, dwu, dgain), outs = _bwd_in(
            f"ffn_bwd_in{i}", dcur, s["x"], gain("ffn_pre_g", i), [dg, du], [s["wg"], s["wu"]], transposed=True,
            exchanges=exchanges_due())
        exchanged(outs)
        grads["ffn_pre_g"][i] = dgain
        made("ffn_w_gate", i, dwg)
        made("ffn_w_up", i, dwu)
        s = mix_saved
        if i % 2 == 0:
            (dz, dwout, dws, dbacc, dlng, dlnb, dgain), outs = _mix_a_bwd_hidden(
                f"mix_a_bwd_hidden{j}", dcur, s["m"], gain("mix_post_g", i), s["zpre"], s["ln_g"], s["ln_b"],
                a_w_s[j], s["b_t"], s["w_out"], exchanges=exchanges_due())
            exchanged(outs)
            grads["a_w_s"][j] = dws
            grads["a_b_s"][j] = jnp.transpose(dbacc)
            grads["a_ln_g"][j] = dlng.reshape(-1)
            grads["a_ln_b"][j] = dlnb.reshape(-1)
            grads["mix_post_g"][i] = dgain
            made("a_w_out", j, dwout.reshape(N_CHIPS, -1, d))
            (dcur, dwin, dgain), outs = _bwd_in(
                f"mix_a_bwd_in{j}", dcur, s["x"], gain("mix_pre_g", i), [dz], [s["w_in"]], exchanges=exchanges_due())
            exchanged(outs)
            grads["mix_pre_g"][i] = dgain
            made("a_w_in", j, dwin)
        else:
            (dq, dwout, dwgrp, dscale, dgain), outs = _mix_b_bwd_hidden(
                f"mix_b_bwd_hidden{j}", dcur, s["m"], gain("mix_post_g", i), s["p"], s["w_grp"], s["scale"],
                s["w_out"], exchanges=exchanges_due())
            exchanged(outs)
            grads["b_scale"][j] = dscale
            grads["mix_post_g"][i] = dgain
            made("b_w_out", j, dwout.reshape(N_CHIPS, -1, d))
            made("b_w_grp", j, jnp.transpose(
                dwgrp.reshape(len(B_WINDOWS), N_CHIPS, gd_b // N_CHIPS, gd_b), (1, 0, 2, 3)).reshape(N_CHIPS, -1, gd_b))
            (dp,), _ = _pool_bwd(f"pool_bwd{j}", dq)
            (dcur, dwin, dgain), outs = _bwd_in(
                f"mix_b_bwd_in{j}", dcur, s["x"], gain("mix_pre_g", i), [dp], [s["w_in"].reshape(1, d, d)],
                exchanges=exchanges_due())
            exchanged(outs)
            grads["mix_pre_g"][i] = dgain
            made("b_w_in", j, dwin.reshape(N_CHIPS, -1, d))
    grad_x = dcur.reshape(x.shape)
    exchanged(_exchange("grads_last", exchanges_due()))
    exchanged(_exchange("grads_last_to_chips", exchanges_due()))

    reduced_names = [name for name in SHARDED if name != "b_scale"]
    sums = []
    for name in reduced_names:
        layers = range(weights[name].shape[0])
        sums.append(_chip_sum("chip_sum_" + name, place, [pair[name, k] for k in layers],
                              [from_chips[name, k] for k in layers]))
    reduced = dict(zip(reduced_names, _exchange("swap_halves", [_SwapHalves(sums)])[0]))

    small_grads = [jnp.stack([g.reshape(weights[name].shape[1:]) for g in grads[name]], axis=0) for name in SMALL]
    scale_grad = jnp.concatenate(grads["b_scale"], axis=0)
    summed = _all_sum_small(_pack_small(small_grads + [scale_grad]))
    small_rows = summed.shape[0] - scale_grad.size // 128
    scale_sum = summed[small_rows:].reshape(scale_grad.shape)
    reduced["b_scale"] = lax.dynamic_slice_in_dim(scale_sum, chip * b_scale.shape[1], b_scale.shape[1], axis=1)[:, None, :]
    summed = summed[:small_rows]

    out_g, out_d, out_m, out_v = {}, {}, {}, {}
    for name in SHARDED:
        shape = weights[name].shape
        dlt, nm, nv = _adamw("adamw_" + name, _as_layers(name, weights[name]), reduced[name],
                             _as_layers(name, mom1[name]), _as_layers(name, mom2[name]))
        out_g[name] = _from_layers(name, reduced[name], shape)
        out_d[name], out_m[name], out_v[name] = (_from_layers(name, a, shape) for a in (dlt, nm, nv))
    small_like = [weights[name] for name in SMALL]
    packs = [_pack_small([src[name] for name in SMALL]).reshape(1, -1, 128) for src in (weights, mom1, mom2)]
    dlt, nm, nv = _adamw("adamw_small", packs[0], summed.reshape(1, -1, 128), packs[1], packs[2])
    for dst, packed in ((out_g, summed), (out_d, dlt[0]), (out_m, nm[0]), (out_v, nv[0])):
        for name, val in zip(SMALL, _unpack_small(packed, small_like)):
            dst[name] = val

    return (loss, grad_x, *[out_g[n] for n in WEIGHTS], *[out_d[n] for n in WEIGHTS],
            *[out_m[n] for n in WEIGHTS], *[out_v[n] for n in WEIGHTS])
```

```python
import functools
import math

import jax
import jax.numpy as jnp
from jax import lax
from jax.experimental import pallas as pl
from jax.experimental.pallas import tpu as pltpu

F32 = jnp.float32
BF16 = jnp.bfloat16
MESH = pl.DeviceIdType.MESH

EPS = 1e-6
CHUNK = 128
A_GROUPS = 8
B_WINDOWS = (2, 4, 8, 16)
HALO = 16
N_CHIPS = 4
N_DEV = 8

ADAM_LR = 0.001
ADAM_B1 = 0.9
ADAM_B2 = 0.999
ADAM_EPS = 1e-08
ADAM_WD = 0.01
ADAM_STEP = 10

VMEM_LIMIT_BYTES = 60 * 1024 * 1024
INV_SQRT2 = 1.0 / math.sqrt(2.0)
INV_SQRT_2PI = 1.0 / math.sqrt(2.0 * math.pi)

_ANY = pl.BlockSpec(memory_space=pl.ANY)
_VM = pl.BlockSpec(memory_space=pltpu.VMEM)


def _params():
    return pltpu.CompilerParams(dimension_semantics=("arbitrary",), vmem_limit_bytes=VMEM_LIMIT_BYTES)


def _token_block(t):
    return 256 if t >= 1024 else 128


def _rows(tm, d):
    return pl.BlockSpec((tm, d), lambda i: (i, 0))


def _blocks(nb, tm, bw):
    return pl.BlockSpec((nb, tm, bw), lambda i: (0, i, 0))


def _dot(a, b):
    return lax.dot_general(a, b, (((1,), (0,)), ((), ())), preferred_element_type=F32)


def _dot_nt(a, b):
    return lax.dot_general(a, b, (((1,), (1,)), ((), ())), preferred_element_type=F32)


def _dot_tn(a, b):
    return lax.dot_general(a, b, (((0,), (0,)), ((), ())), preferred_element_type=F32)


def _rms(x, g):
    return x * lax.rsqrt(jnp.mean(x * x, axis=-1, keepdims=True) + EPS) * g


def _rms_bwd(dy, x, g):
    r = lax.rsqrt(jnp.mean(x * x, axis=-1, keepdims=True) + EPS)
    n = x * r
    dn = dy * g
    dx = r * (dn - n * jnp.mean(dn * n, axis=-1, keepdims=True))
    return dx, jnp.sum(dy * n, axis=0, keepdims=True)


def _gelu_and_grad(x):
    cdf = 0.5 * (1.0 + lax.erf(x * INV_SQRT2))
    return x * cdf, cdf + x * (jnp.exp(-0.5 * x * x) * INV_SQRT_2PI)


def _gelu(x):
    return x * (0.5 * (1.0 + lax.erf(x * INV_SQRT2)))


def _position():
    return lax.axis_index("x"), lax.axis_index("y"), lax.axis_index("c")


def _other_chips(x, y):
    return [(1 - x, y), (x, 1 - y), (1 - x, 1 - y)]


class _GatherWeights:
    def __init__(self, bufs, whole=()):
        self.inputs = list(bufs)
        self.out_shapes = [jax.ShapeDtypeStruct(b.shape, b.dtype) for b in bufs]
        self.aliases = {w: w for w in range(len(bufs))}
        self.n_sems = 6 * len(bufs)
        self.whole = frozenset(whole)

    def _part(self, outs, w, slot, core):
        if w in self.whole:
            return outs[w].at[slot]
        half = outs[w].shape[1] // 2
        return outs[w].at[slot, pl.ds(core * half, half)]

    def _ici(self, outs, send, recv, w, j, slot):
        x, y, c = _position()
        px, py = _other_chips(x, y)[j]
        part = self._part(outs, w, slot, c)
        return pltpu.make_async_remote_copy(
            src_ref=part, dst_ref=part, send_sem=send.at[6 * w + j], recv_sem=recv.at[6 * w + j],
            device_id=(px, py, c), device_id_type=MESH)

    def _d2d(self, outs, send, recv, w, j, slot, core):
        x, y, c = _position()
        part = self._part(outs, w, slot, core)
        return pltpu.make_async_remote_copy(
            src_ref=part, dst_ref=part, send_sem=send.at[6 * w + 3 + j], recv_sem=recv.at[6 * w + 3 + j],
            device_id=(x, y, 1 - c), device_id_type=MESH)

    def start(self, ins, outs, send, recv):
        x, y, _ = _position()
        for w in range(len(outs)):
            for j in range(3):
                self._ici(outs, send, recv, w, j, 2 * x + y).start()

    def finish(self, ins, outs, send, recv):
        x, y, c = _position()
        slots = [2 * px + py for px, py in _other_chips(x, y)]
        for w in range(len(outs)):
            for j, slot in enumerate(slots):
                self._ici(outs, send, recv, w, j, slot).wait_recv()
                if w not in self.whole:
                    self._d2d(outs, send, recv, w, j, slot, c).start()
        for w in range(len(outs)):
            for j, slot in enumerate(slots):
                if w not in self.whole:
                    self._d2d(outs, send, recv, w, j, slot, 1 - c).wait_recv()
        for w in range(len(outs)):
            for j, slot in enumerate(slots):
                self._ici(outs, send, recv, w, j, 2 * x + y).wait_send()
                if w not in self.whole:
                    self._d2d(outs, send, recv, w, j, slot, c).wait_send()


class _ToSibling:
    def __init__(self, grads):
        self.inputs = list(grads)
        self.out_shapes = [jax.ShapeDtypeStruct((g.shape[0], g.shape[1] // 2, g.shape[2]), g.dtype) for g in grads]
        self.aliases = {}
        self.n_sems = len(grads)

    def _copy(self, ins, outs, send, recv, w):
        x, y, c = _position()
        half = ins[w].shape[1] // 2
        return pltpu.make_async_remote_copy(
            src_ref=ins[w].at[:, pl.ds((1 - c) * half, half)], dst_ref=outs[w],
            send_sem=send.at[w], recv_sem=recv.at[w], device_id=(x, y, 1 - c), device_id_type=MESH)

    def start(self, ins, outs, send, recv):
        for w in range(len(ins)):
            self._copy(ins, outs, send, recv, w).start()

    def finish(self, ins, outs, send, recv):
        for w in range(len(ins)):
            self._copy(ins, outs, send, recv, w).wait_recv()
        for w in range(len(ins)):
            self._copy(ins, outs, send, recv, w).wait_send()


class _ToChips:
    def __init__(self, parts):
        self.inputs = list(parts)
        self.out_shapes = [jax.ShapeDtypeStruct(p.shape, p.dtype) for p in parts]
        self.aliases = {}
        self.n_sems = 3 * len(parts)

    def _copy(self, ins, outs, send, recv, w, j, outbound):
        x, y, c = _position()
        px, py = _other_chips(x, y)[j]
        me, peer = 2 * x + y, 2 * px + py
        src_slot, dst_slot = (peer, me) if outbound else (me, peer)
        return pltpu.make_async_remote_copy(
            src_ref=ins[w].at[src_slot], dst_ref=outs[w].at[dst_slot],
            send_sem=send.at[3 * w + j], recv_sem=recv.at[3 * w + j], device_id=(px, py, c), device_id_type=MESH)

    def start(self, ins, outs, send, recv):
        for w in range(len(ins)):
            for j in range(3):
                self._copy(ins, outs, send, recv, w, j, True).start()

    def finish(self, ins, outs, send, recv):
        for w in range(len(ins)):
            for j in range(3):
                self._copy(ins, outs, send, recv, w, j, False).wait_recv()
        for w in range(len(ins)):
            for j in range(3):
                self._copy(ins, outs, send, recv, w, j, True).wait_send()


class _SwapHalves:
    def __init__(self, bufs):
        self.inputs = list(bufs)
        self.out_shapes = [jax.ShapeDtypeStruct(b.shape, b.dtype) for b in bufs]
        self.aliases = {w: w for w in range(len(bufs))}
        self.n_sems = len(bufs)

    def _copy(self, outs, send, recv, w, core):
        x, y, c = _position()
        half = outs[w].shape[1] // 2
        rows = outs[w].at[:, pl.ds(core * half, half)]
        return pltpu.make_async_remote_copy(
            src_ref=rows, dst_ref=rows, send_sem=send.at[w], recv_sem=recv.at[w],
            device_id=(x, y, 1 - c), device_id_type=MESH)

    def start(self, ins, outs, send, recv):
        c = lax.axis_index("c")
        for w in range(len(outs)):
            self._copy(outs, send, recv, w, c).start()

    def finish(self, ins, outs, send, recv):
        c = lax.axis_index("c")
        for w in range(len(outs)):
            self._copy(outs, send, recv, w, 1 - c).wait_recv()
        for w in range(len(outs)):
            self._copy(outs, send, recv, w, c).wait_send()


def _call(name, body, *, grid, in_specs, out_specs, out_shape, args, scratch_shapes=(), exchanges=()):
    given = list(exchanges)
    exchanges = [e for e in given if e.inputs]
    n_in, n_out, n_scr = len(args), len(out_shape), len(scratch_shapes)
    ex_in = [a for e in exchanges for a in e.inputs]
    ex_out = [s for e in exchanges for s in e.out_shapes]
    aliases = {}
    at_in, at_out = n_in, n_out
    for e in exchanges:
        for i, o in e.aliases.items():
            aliases[at_in + i] = at_out + o
        at_in += len(e.inputs)
        at_out += len(e.out_shapes)
    n_steps = grid[0]

    def fused(*refs):
        body_in, refs = refs[:n_in], refs[n_in:]
        ex_in_refs, refs = refs[: len(ex_in)], refs[len(ex_in) :]
        body_out, refs = refs[:n_out], refs[n_out:]
        ex_out_refs, refs = refs[: len(ex_out)], refs[len(ex_out) :]
        body_scr, sems = refs[:n_scr], refs[n_scr:]

        def each(stage):
            a = b = 0
            for n, e in enumerate(exchanges):
                ins, outs = ex_in_refs[a : a + len(e.inputs)], ex_out_refs[b : b + len(e.out_shapes)]
                getattr(e, stage)(ins, outs, sems[2 * n], sems[2 * n + 1])
                a += len(e.inputs)
                b += len(e.out_shapes)

        if exchanges:
            @pl.when(pl.program_id(0) == 0)
            def _():
                each("start")

        if body is not None:
            body(*body_in, *body_out, *body_scr)

        if exchanges:
            @pl.when(pl.program_id(0) == n_steps - 1)
            def _():
                each("finish")

    outs = pl.pallas_call(
        fused,
        name=name,
        grid=grid,
        in_specs=list(in_specs) + [_ANY] * len(ex_in),
        out_specs=list(out_specs) + [_ANY] * len(ex_out),
        out_shape=list(out_shape) + ex_out,
        input_output_aliases=aliases,
        scratch_shapes=list(scratch_shapes)
        + [pltpu.SemaphoreType.DMA((e.n_sems,)) for e in exchanges for _ in range(2)],
        compiler_params=_params(),
    )(*args, *ex_in)
    body_outs, rest = list(outs[:n_out]), list(outs[n_out:])
    ex_outs = []
    for e in given:
        n_e = len(e.out_shapes) if e.inputs else 0
        ex_outs.append(rest[:n_e])
        rest = rest[n_e:]
    return body_outs, ex_outs


def _exchange(name, exchanges):
    return _call(name, None, grid=(1,), in_specs=[], out_specs=[], out_shape=[], args=[], exchanges=exchanges)[1]


def _ffn_fwd(name, x, gpre, gpost, wg, wu, wd, target=None, exchanges=()):
    t, d = x.shape
    nb, fs, _ = wg.shape
    tm = _token_block(t)
    with_loss = target is not None

    def body(x_ref, gpre_ref, gpost_ref, wg_ref, wu_ref, wd_ref, *refs):
        if with_loss:
            t_ref, o_ref, g_ref, u_ref, f_ref, sq_ref = refs
        else:
            o_ref, g_ref, u_ref, f_ref = refs
        xv = x_ref[...]
        hb = _rms(xv, gpre_ref[...]).astype(BF16)
        f = jnp.zeros((tm, d), F32)
        for k in range(nb):
            g = _dot_nt(hb, wg_ref[k])
            u = _dot_nt(hb, wu_ref[k])
            g_ref[k] = g
            u_ref[k] = u
            a = (g * jax.nn.sigmoid(g) * u).astype(BF16)
            f = f + _dot(a, wd_ref[k])
        f_ref[...] = f
        y = xv + _rms(f, gpost_ref[...])
        if with_loss:
            @pl.when(pl.program_id(0) == 0)
            def _():
                sq_ref[...] = jnp.zeros_like(sq_ref)

            err = y - t_ref[...]
            o_ref[...] = err / d
            sq_ref[...] += jnp.sum(err * err)
        else:
            o_ref[...] = y

    return _call(
        name,
        body,
        grid=(t // tm,),
        in_specs=[_rows(tm, d), _VM, _VM, _VM, _VM, _VM] + [_rows(tm, d)] * with_loss,
        out_specs=[_rows(tm, d), _blocks(nb, tm, fs), _blocks(nb, tm, fs), _rows(tm, d)] + [_VM] * with_loss,
        out_shape=[
            jax.ShapeDtypeStruct((t, d), F32),
            jax.ShapeDtypeStruct((nb, t, fs), F32),
            jax.ShapeDtypeStruct((nb, t, fs), F32),
            jax.ShapeDtypeStruct((t, d), F32),
        ] + [jax.ShapeDtypeStruct((8, 128), F32)] * with_loss,
        args=[x, gpre, gpost, wg, wu, wd] + [target] * with_loss,
        exchanges=exchanges,
    )


def _ffn_bwd_hidden(name, dy, f, gpost, g_pre, u_pre, wd, exchanges=()):
    t, d = dy.shape
    nb, fs, _ = wd.shape
    tm = _token_block(t)

    def body(dy_ref, f_ref, gpost_ref, g_ref, u_ref, wd_ref, dg_ref, du_ref, dwd_ref, dgain_ref):
        @pl.when(pl.program_id(0) == 0)
        def _():
            dwd_ref[...] = jnp.zeros_like(dwd_ref)
            dgain_ref[...] = jnp.zeros_like(dgain_ref)

        df, dgain = _rms_bwd(dy_ref[...], f_ref[...], gpost_ref[...])
        dgain_ref[...] += dgain
        dfb = df.astype(BF16)
        for k in range(nb):
            g = g_ref[k]
            u = u_ref[k]
            s = jax.nn.sigmoid(g)
            sg = g * s
            a = (sg * u).astype(BF16)
            da = _dot_nt(dfb, wd_ref[k])
            dwd_ref[k] += _dot_tn(a, dfb)
            du_ref[k] = (da * sg).astype(BF16)
            dg_ref[k] = (da * u * (s * (1.0 + g * (1.0 - s)))).astype(BF16)

    return _call(
        name,
        body,
        grid=(t // tm,),
        in_specs=[_rows(tm, d), _rows(tm, d), _VM, _blocks(nb, tm, fs), _blocks(nb, tm, fs), _VM],
        out_specs=[_blocks(nb, tm, fs), _blocks(nb, tm, fs), _VM, _VM],
        out_shape=[
            jax.ShapeDtypeStruct((nb, t, fs), BF16),
            jax.ShapeDtypeStruct((nb, t, fs), BF16),
            jax.ShapeDtypeStruct((nb, fs, d), F32),
            jax.ShapeDtypeStruct((1, d), F32),
        ],
        args=[dy, f, gpost, g_pre, u_pre, wd],
        exchanges=exchanges,
    )


def _bwd_in(name, dres, x, gpre, dzs, ws, transposed=False, exchanges=()):
    t, d = x.shape
    n = len(ws)
    tm = _token_block(t)
    widths = [w.shape[1] if transposed else w.shape[2] for w in ws]

    def body(*refs):
        dres_ref, x_ref, gpre_ref = refs[:3]
        dz_refs = refs[3 : 3 + n]
        w_refs = refs[3 + n : 3 + 2 * n]
        dx_ref = refs[3 + 2 * n]
        dw_refs = refs[4 + 2 * n : 4 + 3 * n]
        dgain_ref = refs[4 + 3 * n]

        @pl.when(pl.program_id(0) == 0)
        def _():
            for dw_ref in dw_refs:
                dw_ref[...] = jnp.zeros_like(dw_ref)
            dgain_ref[...] = jnp.zeros_like(dgain_ref)

        xv = x_ref[...]
        gain = gpre_ref[...]
        hb = _rms(xv, gain).astype(BF16)
        dh = jnp.zeros((tm, d), F32)
        for dz_ref, w_ref, dw_ref in zip(dz_refs, w_refs, dw_refs):
            for k in range(w_ref.shape[0]):
                dz = dz_ref[k]
                if transposed:
                    dh = dh + _dot(dz, w_ref[k])
                    dw_ref[k] += _dot_tn(dz, hb)
                else:
                    dh = dh + _dot_nt(dz, w_ref[k])
                    dw_ref[k] += _dot_tn(hb, dz)
        dx, dgain = _rms_bwd(dh, xv, gain)
        dx_ref[...] = dres_ref[...] + dx
        dgain_ref[...] += dgain

    return _call(
        name,
        body,
        grid=(t // tm,),
        in_specs=[_rows(tm, d), _rows(tm, d), _VM]
        + [_blocks(w.shape[0], tm, bw) for w, bw in zip(ws, widths)]
        + [_VM] * n,
        out_specs=[_rows(tm, d)] + [_VM] * n + [_VM],
        out_shape=[jax.ShapeDtypeStruct((t, d), F32)]
        + [jax.ShapeDtypeStruct(w.shape, F32) for w in ws]
        + [jax.ShapeDtypeStruct((1, d), F32)],
        args=[dres, x, gpre, *dzs, *ws],
        exchanges=exchanges,
    )


def _causal_weights(ws_ref):
    row = lax.broadcasted_iota(jnp.int32, (CHUNK, CHUNK), 0)
    col = lax.broadcasted_iota(jnp.int32, (CHUNK, CHUNK), 1)
    return [jnp.where(row >= col, ws_ref[g], 0.0).astype(BF16) for g in range(A_GROUPS)]


def _layernorm_halves(v0, v1):
    width = v0.shape[-1] + v1.shape[-1]
    mu = (jnp.sum(v0, axis=-1, keepdims=True) + jnp.sum(v1, axis=-1, keepdims=True)) / width
    c0 = v0 - mu
    c1 = v1 - mu
    var = (jnp.sum(c0 * c0, axis=-1, keepdims=True) + jnp.sum(c1 * c1, axis=-1, keepdims=True)) / width
    rstd = lax.rsqrt(var + EPS)
    return c0 * rstd, c1 * rstd, rstd


def _spatial_gate(sv_ref, wtril, vl, bt_ref, half, tm, gd):
    for gg in range(A_GROUPS // 2):
        g = half * (A_GROUPS // 2) + gg
        bias = bt_ref[:, g : g + 1]
        for n in range(tm // CHUNK):
            blk = vl[n * CHUNK : (n + 1) * CHUNK, gg * gd : (gg + 1) * gd]
            sv_ref[n * CHUNK : (n + 1) * CHUNK, gg * gd : (gg + 1) * gd] = _dot(wtril[g], blk) + bias


def _mix_a_fwd(name, x, gpre, gpost, w_in, ln_g, ln_b, w_s, b_t, w_out, exchanges=()):
    t, d = x.shape
    _, _, q = w_in.shape
    gd = 2 * q // A_GROUPS
    tm = _token_block(t)

    def body(x_ref, gpre_ref, gpost_ref, win_ref, lng_ref, lnb_ref, ws_ref, bt_ref, wout_ref,
             o_ref, z_ref, m_ref, sv_ref):
        xv = x_ref[...]
        hb = _rms(xv, gpre_ref[...]).astype(BF16)
        z = []
        for k in range(4):
            zp = _dot(hb, win_ref[k])
            z_ref[k] = zp
            z.append(_gelu(zp))
        vh0, vh1, _ = _layernorm_halves(z[2], z[3])
        vls = [(vh * lng_ref[b : b + 1, :] + lnb_ref[b : b + 1, :]).astype(BF16) for b, vh in enumerate((vh0, vh1))]
        wtril = _causal_weights(ws_ref)
        m = jnp.zeros((tm, d), F32)
        for b in range(2):
            _spatial_gate(sv_ref, wtril, vls[b], bt_ref, b, tm, gd)
            gated = (z[b] * sv_ref[...]).astype(BF16)
            m = m + _dot(gated, wout_ref[b])
        m_ref[...] = m
        o_ref[...] = xv + _rms(m, gpost_ref[...])

    return _call(
        name,
        body,
        grid=(t // tm,),
        in_specs=[_rows(tm, d)] + [_VM] * 8,
        out_specs=[_rows(tm, d), _blocks(4, tm, q), _rows(tm, d)],
        out_shape=[
            jax.ShapeDtypeStruct((t, d), F32),
            jax.ShapeDtypeStruct((4, t, q), F32),
            jax.ShapeDtypeStruct((t, d), F32),
        ],
        scratch_shapes=[pltpu.VMEM((tm, q), F32)],
        args=[x, gpre, gpost, w_in, ln_g, ln_b, w_s, b_t, w_out],
        exchanges=exchanges,
    )


def _mix_a_bwd_hidden(name, dy, m, gpost, zpre, ln_g, ln_b, w_s, b_t, w_out, exchanges=()):
    t, d = dy.shape
    _, _, q = zpre.shape
    gd = 2 * q // A_GROUPS
    tm = _token_block(t)
    n_chunks = tm // CHUNK

    def body(dy_ref, m_ref, gpost_ref, z_ref, lng_ref, lnb_ref, ws_ref, bt_ref, wout_ref,
             dz_ref, dwout_ref, dws_ref, dbacc_ref, dlng_ref, dlnb_ref, dgain_ref, sv_ref, dvl_ref):
        first = pl.program_id(0) == 0

        @pl.when(first)
        def _():
            for ref in (dwout_ref, dws_ref, dbacc_ref, dlng_ref, dlnb_ref, dgain_ref):
                ref[...] = jnp.zeros_like(ref)

        dm, dgain = _rms_bwd(dy_ref[...], m_ref[...], gpost_ref[...])
        dgain_ref[...] += dgain
        dmb = dm.astype(BF16)
        v0 = _gelu(z_ref[2])
        v1 = _gelu(z_ref[3])
        vhs = list(_layernorm_halves(v0, v1))
        rstd = vhs.pop()
        vls = [(vh * lng_ref[b : b + 1, :] + lnb_ref[b : b + 1, :]).astype(BF16) for b, vh in enumerate(vhs)]
        wtril = _causal_weights(ws_ref)
        dvhs = []
        for b in range(2):
            zp = z_ref[b]
            u, du_dz = _gelu_and_grad(zp)
            _spatial_gate(sv_ref, wtril, vls[b], bt_ref, b, tm, gd)
            sv = sv_ref[...]
            gated = (u * sv).astype(BF16)
            dgated = _dot_nt(dmb, wout_ref[b])
            dwout_ref[b] += _dot_tn(gated, dmb)
            dz_ref[b] = (dgated * sv * du_dz).astype(BF16)
            dsv = dgated * u
            folded = dsv[0:CHUNK, :]
            for c in range(1, n_chunks):
                folded = folded + dsv[c * CHUNK : (c + 1) * CHUNK, :]
            for gg in range(A_GROUPS // 2):
                g = b * (A_GROUPS // 2) + gg
                dbacc_ref[:, g : g + 1] += jnp.sum(folded[:, gg * gd : (gg + 1) * gd], axis=1, keepdims=True)
            dsvb = dsv.astype(BF16)
            for gg in range(A_GROUPS // 2):
                g = b * (A_GROUPS // 2) + gg
                for c in range(n_chunks):
                    rows = slice(c * CHUNK, (c + 1) * CHUNK)
                    cols = slice(gg * gd, (gg + 1) * gd)
                    blk = dsvb[rows, cols]
                    dvl_ref[rows, cols] = _dot_tn(wtril[g], blk)
                    dws_ref[g] += _dot_nt(blk, vls[b][rows, cols])
            dvl = dvl_ref[...]
            dlng_ref[b : b + 1, :] += jnp.sum(dvl * vhs[b], axis=0, keepdims=True)
            dlnb_ref[b : b + 1, :] += jnp.sum(dvl, axis=0, keepdims=True)
            dvhs.append(dvl * lng_ref[b : b + 1, :])
        width = 2.0 * q
        m1 = (jnp.sum(dvhs[0], axis=-1, keepdims=True) + jnp.sum(dvhs[1], axis=-1, keepdims=True)) / width
        m2 = (jnp.sum(dvhs[0] * vhs[0], axis=-1, keepdims=True)
              + jnp.sum(dvhs[1] * vhs[1], axis=-1, keepdims=True)) / width
        for b in range(2):
            dv = rstd * (dvhs[b] - m1 - vhs[b] * m2)
            _, dv_dz = _gelu_and_grad(z_ref[2 + b])
            dz_ref[2 + b] = (dv * dv_dz).astype(BF16)

        @pl.when(pl.program_id(0) == t // tm - 1)
        def _():
            row = lax.broadcasted_iota(jnp.int32, (CHUNK, CHUNK), 0)
            col = lax.broadcasted_iota(jnp.int32, (CHUNK, CHUNK), 1)
            for g in range(A_GROUPS):
                dws_ref[g] = jnp.where(row >= col, dws_ref[g], 0.0)

    return _call(
        name,
        body,
        grid=(t // tm,),
        in_specs=[_rows(tm, d), _rows(tm, d), _VM, _blocks(4, tm, q)] + [_VM] * 5,
        out_specs=[_blocks(4, tm, q)] + [_VM] * 6,
        out_shape=[
            jax.ShapeDtypeStruct((4, t, q), BF16),
            jax.ShapeDtypeStruct((2, q, d), F32),
            jax.ShapeDtypeStruct((A_GROUPS, CHUNK, CHUNK), F32),
            jax.ShapeDtypeStruct((CHUNK, A_GROUPS), F32),
            jax.ShapeDtypeStruct((2, q), F32),
            jax.ShapeDtypeStruct((2, q), F32),
            jax.ShapeDtypeStruct((1, d), F32),
        ],
        scratch_shapes=[pltpu.VMEM((tm, q), F32), pltpu.VMEM((tm, q), F32)],
        args=[dy, m, gpost, zpre, ln_g, ln_b, w_s, b_t, w_out],
        exchanges=exchanges,
    )


def _norm_matmul(name, x, gpre, w, exchanges=()):
    t, d = x.shape
    n = w.shape[1]
    tm = _token_block(t)

    def body(x_ref, gpre_ref, w_ref, o_ref):
        o_ref[...] = _dot(_rms(x_ref[...], gpre_ref[...]).astype(BF16), w_ref[...])

    return _call(
        name,
        body,
        grid=(t // tm,),
        in_specs=[_rows(tm, d), _VM, _VM],
        out_specs=[_rows(tm, n)],
        out_shape=[jax.ShapeDtypeStruct((t, n), F32)],
        args=[x, gpre, w],
        exchanges=exchanges,
    )


def _window_counts(tm, win):
    pos = pl.program_id(0) * tm + lax.broadcasted_iota(jnp.int32, (tm, 1), 0)
    return jnp.minimum(pos + 1, win).astype(F32)


def _pooled(p, halo, tm, gd):
    prev = jnp.where(pl.program_id(0) == 0, 0.0, halo)
    ext = jnp.concatenate([prev, p], axis=0)
    out = []
    for g, win in enumerate(B_WINDOWS):
        s = ext[:, g * gd : (g + 1) * gd]
        step = 1
        while step < win:
            s = s + pltpu.roll(s, step, 0)
            step *= 2
        total = s[HALO:, :]
        out.append(total / _window_counts(tm, win) - p[:, g * gd : (g + 1) * gd])
    return out


def _halo_spec(t, tm, d, ahead):
    per = tm // HALO
    if ahead:
        return pl.BlockSpec((HALO, d), lambda i: (jnp.minimum((i + 1) * per, t // HALO - 1), 0))
    return pl.BlockSpec((HALO, d), lambda i: (jnp.maximum(i * per - 1, 0), 0))


def _mix_b_fwd(name, x, p, gpost, w_grp, scale, w_out, exchanges=()):
    t, d = x.shape
    gd = d // len(B_WINDOWS)
    tm = _token_block(t)

    def body(x_ref, p_ref, halo_ref, gpost_ref, wgrp_ref, scale_ref, wout_ref, o_ref, m_ref):
        pooled = _pooled(p_ref[...], halo_ref[...], tm, gd)
        mixed = jnp.concatenate([_dot(pg.astype(BF16), wgrp_ref[g]) for g, pg in enumerate(pooled)], axis=1)
        m = _dot((mixed * scale_ref[...]).astype(BF16), wout_ref[...])
        m_ref[...] = m
        o_ref[...] = x_ref[...] + _rms(m, gpost_ref[...])

    return _call(
        name,
        body,
        grid=(t // tm,),
        in_specs=[_rows(tm, d), _rows(tm, d), _halo_spec(t, tm, d, False), _VM, _VM, _VM, _VM],
        out_specs=[_rows(tm, d), _rows(tm, d)],
        out_shape=[jax.ShapeDtypeStruct((t, d), F32), jax.ShapeDtypeStruct((t, d), F32)],
        args=[x, p, p, gpost, w_grp, scale, w_out],
        exchanges=exchanges,
    )


def _mix_b_bwd_hidden(name, dy, m, gpost, p, w_grp, scale, w_out, exchanges=()):
    t, d = dy.shape
    gd = d // len(B_WINDOWS)
    tm = _token_block(t)

    def body(dy_ref, m_ref, gpost_ref, p_ref, halo_ref, wgrp_ref, scale_ref, wout_ref,
             dq_ref, dwout_ref, dwgrp_ref, dscale_ref, dgain_ref):
        @pl.when(pl.program_id(0) == 0)
        def _():
            for ref in (dwout_ref, dwgrp_ref, dscale_ref, dgain_ref):
                ref[...] = jnp.zeros_like(ref)

        dm, dgain = _rms_bwd(dy_ref[...], m_ref[...], gpost_ref[...])
        dgain_ref[...] += dgain
        dmb = dm.astype(BF16)
        pooled = [pg.astype(BF16) for pg in _pooled(p_ref[...], halo_ref[...], tm, gd)]
        mixed = jnp.concatenate([_dot(pg, wgrp_ref[g]) for g, pg in enumerate(pooled)], axis=1)
        scale = scale_ref[...]
        ms = (mixed * scale).astype(BF16)
        dms = _dot_nt(dmb, wout_ref[...])
        dwout_ref[...] += _dot_tn(ms, dmb)
        dscale_ref[...] += jnp.sum(dms * mixed, axis=0, keepdims=True)
        dmixed = (dms * scale).astype(BF16)
        for g, win in enumerate(B_WINDOWS):
            cols = slice(g * gd, (g + 1) * gd)
            dmg = dmixed[:, cols]
            dwgrp_ref[g] += _dot_tn(pooled[g], dmg)
            dq_ref[:, cols] = _dot_nt(dmg, wgrp_ref[g]) / _window_counts(tm, win)

    return _call(
        name,
        body,
        grid=(t // tm,),
        in_specs=[_rows(tm, d), _rows(tm, d), _VM, _rows(tm, d), _halo_spec(t, tm, d, False), _VM, _VM, _VM],
        out_specs=[_rows(tm, d), _VM, _VM, _VM, _VM],
        out_shape=[
            jax.ShapeDtypeStruct((t, d), F32),
            jax.ShapeDtypeStruct((d, d), F32),
            jax.ShapeDtypeStruct((len(B_WINDOWS), gd, gd), F32),
            jax.ShapeDtypeStruct((1, d), F32),
            jax.ShapeDtypeStruct((1, d), F32),
        ],
        args=[dy, m, gpost, p, p, w_grp, scale, w_out],
        exchanges=exchanges,
    )


def _pool_bwd(name, dq, exchanges=()):
    t, d = dq.shape
    gd = d // len(B_WINDOWS)
    tm = _token_block(t)
    n_steps = t // tm

    def body(dq_ref, halo_ref, dp_ref):
        dq_blk = dq_ref[...]
        nxt = jnp.where(pl.program_id(0) == n_steps - 1, 0.0, halo_ref[...])
        ext = jnp.concatenate([dq_blk, nxt], axis=0)
        for g, win in enumerate(B_WINDOWS):
            cols = slice(g * gd, (g + 1) * gd)
            s = ext[:, cols]
            step = 1
            while step < win:
                s = s + pltpu.roll(s, tm + HALO - step, 0)
                step *= 2
            dp_ref[0, :, cols] = (s[:tm, :] - dq_blk[:, cols] * _window_counts(tm, win)).astype(BF16)

    return _call(
        name,
        body,
        grid=(n_steps,),
        in_specs=[_rows(tm, d), _halo_spec(t, tm, d, True)],
        out_specs=[_blocks(1, tm, d)],
        out_shape=[jax.ShapeDtypeStruct((1, t, d), BF16)],
        args=[dq, dq],
        exchanges=exchanges,
    )


def _cast_into_slots(name, place, w, dtype):
    n_layers, r, c = w.shape

    def body(place_ref, w_ref, *o_refs):
        del place_ref
        for j, o_ref in enumerate(o_refs):
            @pl.when(pl.program_id(0) == j)
            def _():
                o_ref[...] = w_ref[...].astype(dtype)

    return pl.pallas_call(
        body,
        name=name,
        grid_spec=pltpu.PrefetchScalarGridSpec(
            num_scalar_prefetch=1,
            grid=(n_layers,),
            in_specs=[pl.BlockSpec((1, r, c), lambda i, place_ref: (i, 0, 0))],
            out_specs=[pl.BlockSpec((1, r, c), lambda i, place_ref: (place_ref[0], 0, 0))] * n_layers,
        ),
        out_shape=[jax.ShapeDtypeStruct((N_CHIPS, r, c), dtype)] * n_layers,
        compiler_params=_params(),
    )(place, w)


def _row_tile(r):
    return 256 if r % 256 == 0 else r


def _pair_sum(name, place, dw, recv):
    _, r, c = dw.shape
    half = r // 2
    tr = _row_tile(half)
    per = half // tr

    def body(place_ref, a_ref, b_ref, o_ref):
        del place_ref
        o_ref[...] = (a_ref[...] + b_ref[...]).astype(BF16)

    return pl.pallas_call(
        body,
        name=name,
        grid_spec=pltpu.PrefetchScalarGridSpec(
            num_scalar_prefetch=1,
            grid=(N_CHIPS, per),
            in_specs=[
                pl.BlockSpec((1, tr, c), lambda k, i, place_ref: (k, place_ref[1] * per + i, 0)),
                pl.BlockSpec((1, tr, c), lambda k, i, place_ref: (k, i, 0)),
            ],
            out_specs=pl.BlockSpec((1, tr, c), lambda k, i, place_ref: (k, i, 0)),
        ),
        out_shape=jax.ShapeDtypeStruct(recv.shape, BF16),
        compiler_params=pltpu.CompilerParams(
            dimension_semantics=("arbitrary",) * 2, vmem_limit_bytes=VMEM_LIMIT_BYTES),
    )(place, dw, recv)


def _chip_sum(name, place, mine, others):
    n_layers = len(mine)
    _, half, c = mine[0].shape
    tr = _row_tile(half)
    per = half // tr

    def body(place_ref, *refs):
        del place_ref
        o_ref = refs[-1]
        for j in range(n_layers):
            @pl.when(pl.program_id(0) == j)
            def _():
                parts = refs[4 * j : 4 * j + 4]
                acc = parts[0][...].astype(F32) + parts[1][...].astype(F32)
                acc = acc + parts[2][...].astype(F32)
                o_ref[...] = acc + parts[3][...].astype(F32)

    def part(j, flip):
        return pl.BlockSpec((1, tr, c), lambda l, i, place_ref: (
            jnp.bitwise_xor(place_ref[0], flip), jnp.where(l == j, i, 0), 0))

    args = []
    for j in range(n_layers):
        args += [mine[j], others[j], others[j], others[j]]
    return pl.pallas_call(
        body,
        name=name,
        grid_spec=pltpu.PrefetchScalarGridSpec(
            num_scalar_prefetch=1,
            grid=(n_layers, per),
            in_specs=[part(j, flip) for j in range(n_layers) for flip in range(N_CHIPS)],
            out_specs=pl.BlockSpec((1, tr, c), lambda l, i, place_ref: (l, place_ref[1] * per + i, 0)),
        ),
        out_shape=jax.ShapeDtypeStruct((n_layers, 2 * half, c), F32),
        compiler_params=pltpu.CompilerParams(
            dimension_semantics=("arbitrary",) * 2, vmem_limit_bytes=VMEM_LIMIT_BYTES),
    )(place, *args)


def _adamw(name, w, g, m, v):
    n_layers, r, c = w.shape
    tr = _row_tile(r)

    def body(w_ref, g_ref, m_ref, v_ref, d_ref, nm_ref, nv_ref):
        gv = g_ref[...]
        nm = ADAM_B1 * m_ref[...] + (1.0 - ADAM_B1) * gv
        nv = ADAM_B2 * v_ref[...] + (1.0 - ADAM_B2) * jnp.square(gv)
        m_hat = nm / (1.0 - ADAM_B1 ** ADAM_STEP)
        v_hat = nv / (1.0 - ADAM_B2 ** ADAM_STEP)
        d_ref[...] = -ADAM_LR * (m_hat / (jnp.sqrt(v_hat) + ADAM_EPS) + ADAM_WD * w_ref[...])
        nm_ref[...] = nm
        nv_ref[...] = nv

    spec = pl.BlockSpec((1, tr, c), lambda l, i: (l, i, 0))
    return pl.pallas_call(
        body,
        name=name,
        grid=(n_layers, r // tr),
        in_specs=[spec] * 4,
        out_specs=[spec] * 3,
        out_shape=[jax.ShapeDtypeStruct(w.shape, F32)] * 3,
        compiler_params=pltpu.CompilerParams(
            dimension_semantics=("arbitrary",) * 2, vmem_limit_bytes=VMEM_LIMIT_BYTES),
    )(w, g, m, v)


def _all_sum_small(packed):
    m_per, n = packed.shape

    def body(x_ref, sum_ref, all_ref, send_sems, recv_sems, local_sem):
        x, y, c = _position()
        me, sibling = (x, y, c), (x, y, 1 - c)
        chips = _other_chips(x, y)

        def rows(px, py, pc):
            return all_ref.at[pl.ds((4 * px + 2 * py + pc) * m_per, m_per), :]

        def copy(k, block, to, src=None):
            return pltpu.make_async_remote_copy(
                src_ref=rows(*block) if src is None else src, dst_ref=rows(*block),
                send_sem=send_sems.at[k], recv_sem=recv_sems.at[k], device_id=to, device_id_type=MESH)

        mine = pltpu.make_async_copy(x_ref, rows(*me), local_sem)
        mine.start()
        first = [copy(0, me, sibling, src=x_ref)]
        first += [copy(1 + j, me, (*chip, c), src=x_ref) for j, chip in enumerate(chips)]
        for cp in first:
            cp.start()
        passed = [copy(4 + j, (*chip, c), sibling) for j, chip in enumerate(chips)]
        for j, chip in enumerate(chips):
            copy(1 + j, (*chip, c), me).wait_recv()
            passed[j].start()
        copy(0, sibling, me).wait_recv()
        for j, chip in enumerate(chips):
            copy(4 + j, (*chip, 1 - c), me).wait_recv()
        for cp in first + passed:
            cp.wait_send()
        mine.wait()
        acc = all_ref[0:m_per, :]
        for k in range(1, N_DEV):
            acc = acc + all_ref[k * m_per : (k + 1) * m_per, :]
        sum_ref[...] = acc

    return pl.pallas_call(
        body,
        name="all_sum_small",
        in_specs=[_VM],
        out_specs=_VM,
        out_shape=jax.ShapeDtypeStruct((m_per, n), F32),
        scratch_shapes=[
            pltpu.VMEM((N_DEV * m_per, n), F32),
            pltpu.SemaphoreType.DMA((7,)),
            pltpu.SemaphoreType.DMA((7,)),
            pltpu.SemaphoreType.DMA,
        ],
        compiler_params=pltpu.CompilerParams(vmem_limit_bytes=VMEM_LIMIT_BYTES),
    )(packed)


SHARDED = ("a_w_in", "a_w_out", "b_w_in", "b_w_grp", "b_scale", "b_w_out", "ffn_w_gate", "ffn_w_up", "ffn_w_down")
SMALL = ("a_ln_g", "a_ln_b", "a_w_s", "a_b_s", "mix_pre_g", "mix_post_g", "ffn_pre_g", "ffn_post_g")
WEIGHTS = ("a_w_in", "a_ln_g", "a_ln_b", "a_w_s", "a_b_s", "a_w_out", "b_w_in", "b_w_grp", "b_scale", "b_w_out",
           "mix_pre_g", "mix_post_g", "ffn_pre_g", "ffn_post_g", "ffn_w_gate", "ffn_w_up", "ffn_w_down")


TRANSPOSED = ("ffn_w_gate", "ffn_w_up")


def _as_layers(name, a):
    if name in TRANSPOSED:
        return jnp.swapaxes(a, 1, 2)
    if a.ndim == 2:
        return a.reshape(a.shape[0], 1, a.shape[1])
    return a.reshape(a.shape[0], -1, a.shape[-1])


def _from_layers(name, a, shape):
    if name in TRANSPOSED:
        return jnp.swapaxes(a, 1, 2)
    return a.reshape(shape)


def _pack_small(parts):
    return jnp.concatenate([p.reshape(-1, 128) for p in parts], axis=0)


def _unpack_small(packed, like):
    out, row = [], 0
    for ref in like:
        rows = ref.size // 128
        out.append(packed[row : row + rows].reshape(ref.shape))
        row += rows
    return out


def kernel(x, a_w_in, a_ln_g, a_ln_b, a_w_s, a_b_s, a_w_out, b_w_in, b_w_grp, b_scale, b_w_out, mix_pre_g, mix_post_g, ffn_pre_g, ffn_post_g, ffn_w_gate, ffn_w_up, ffn_w_down, loss_target, m_a_w_in, m_a_ln_g, m_a_ln_b, m_a_w_s, m_a_b_s, m_a_w_out, m_b_w_in, m_b_w_grp, m_b_scale, m_b_w_out, m_mix_pre_g, m_mix_post_g, m_ffn_pre_g, m_ffn_post_g, m_ffn_w_gate, m_ffn_w_up, m_ffn_w_down, v_a_w_in, v_a_ln_g, v_a_ln_b, v_a_w_s, v_a_b_s, v_a_w_out, v_b_w_in, v_b_w_grp, v_b_scale, v_b_w_out, v_mix_pre_g, v_mix_post_g, v_ffn_pre_g, v_ffn_post_g, v_ffn_w_gate, v_ffn_w_up, v_ffn_w_down):
    weights = dict(a_w_in=a_w_in, a_ln_g=a_ln_g, a_ln_b=a_ln_b, a_w_s=a_w_s, a_b_s=a_b_s, a_w_out=a_w_out,
                   b_w_in=b_w_in, b_w_grp=b_w_grp, b_scale=b_scale, b_w_out=b_w_out, mix_pre_g=mix_pre_g,
                   mix_post_g=mix_post_g, ffn_pre_g=ffn_pre_g, ffn_post_g=ffn_post_g, ffn_w_gate=ffn_w_gate,
                   ffn_w_up=ffn_w_up, ffn_w_down=ffn_w_down)
    mom1 = dict(a_w_in=m_a_w_in, a_ln_g=m_a_ln_g, a_ln_b=m_a_ln_b, a_w_s=m_a_w_s, a_b_s=m_a_b_s, a_w_out=m_a_w_out,
                b_w_in=m_b_w_in, b_w_grp=m_b_w_grp, b_scale=m_b_scale, b_w_out=m_b_w_out, mix_pre_g=m_mix_pre_g,
                mix_post_g=m_mix_post_g, ffn_pre_g=m_ffn_pre_g, ffn_post_g=m_ffn_post_g, ffn_w_gate=m_ffn_w_gate,
                ffn_w_up=m_ffn_w_up, ffn_w_down=m_ffn_w_down)
    mom2 = dict(a_w_in=v_a_w_in, a_ln_g=v_a_ln_g, a_ln_b=v_a_ln_b, a_w_s=v_a_w_s, a_b_s=v_a_b_s, a_w_out=v_a_w_out,
                b_w_in=v_b_w_in, b_w_grp=v_b_w_grp, b_scale=v_b_scale, b_w_out=v_b_w_out, mix_pre_g=v_mix_pre_g,
                mix_post_g=v_mix_post_g, ffn_pre_g=v_ffn_pre_g, ffn_post_g=v_ffn_post_g, ffn_w_gate=v_ffn_w_gate,
                ffn_w_up=v_ffn_w_up, ffn_w_down=v_ffn_w_down)

    t, d = x.shape[1], x.shape[2]
    depth = mix_pre_g.shape[0]
    gd_b = d // len(B_WINDOWS)
    xs = x.reshape(t, d)
    target = loss_target.reshape(t, d)

    chip = 2 * lax.axis_index("x") + lax.axis_index("y")
    place = jnp.stack([chip, lax.axis_index("c")]).astype(jnp.int32)
    bufs = {name: list(_cast_into_slots("cast_" + name, place, _as_layers(name, weights[name]),
                                        F32 if name == "b_scale" else BF16)) for name in SHARDED}

    def gain(name, i):
        return weights[name][i].reshape(1, d)

    def weight_keys(i):
        j = i // 2
        mixer = [("a_w_in", j), ("a_w_out", j)] if i % 2 == 0 else [("b_w_in", j), ("b_w_grp", j), ("b_w_out", j)]
        return mixer, [("ffn_w_gate", i), ("ffn_w_up", i), ("ffn_w_down", i)]

    def gather(keys):
        return _GatherWeights([bufs[n][j] for n, j in keys],
                              whole=[k for k, (n, _) in enumerate(keys) if n == "b_scale"])

    def gathered(keys, outs):
        for (n, j), buf in zip(keys, outs):
            bufs[n][j] = buf

    first = weight_keys(0)[0] + [("b_scale", j) for j in range(b_scale.shape[0])]
    gathered(first, _exchange("gather_first", [gather(first)])[0])
    saved = []
    cur = xs
    for i in range(depth):
        j = i // 2
        mixer_next, ffn_next = weight_keys(i + 1) if i + 1 < depth else ([], [])
        ffn_keys = weight_keys(i)[1]
        after_ffn = mixer_next if (i + 1) % 2 == 0 else mixer_next + ffn_next
        if i % 2 == 0:
            w_in = bufs["a_w_in"][j]
            q = w_in.shape[2]
            w_out = bufs["a_w_out"][j].reshape(2, q, d)
            ln_g = a_ln_g[j].reshape(2, q)
            ln_b = a_ln_b[j].reshape(2, q)
            b_t = jnp.transpose(a_b_s[j])
            (nxt, zpre, m), (got,) = _mix_a_fwd(
                f"mix_a_fwd{j}", cur, gain("mix_pre_g", i), gain("mix_post_g", i), w_in, ln_g, ln_b, a_w_s[j], b_t,
                w_out, exchanges=[gather(ffn_keys)])
            gathered(ffn_keys, got)
            mix_saved = dict(x=cur, zpre=zpre, m=m, w_in=w_in, w_out=w_out, ln_g=ln_g, ln_b=ln_b, b_t=b_t)
        else:
            w_in = bufs["b_w_in"][j].reshape(d, d)
            w_out = bufs["b_w_out"][j].reshape(d, d)
            w_grp = jnp.transpose(bufs["b_w_grp"][j].reshape(N_CHIPS, len(B_WINDOWS), gd_b // N_CHIPS, gd_b),
                                  (1, 0, 2, 3)).reshape(len(B_WINDOWS), gd_b, gd_b)
            scale = bufs["b_scale"][j].reshape(1, d)
            (p,), _ = _norm_matmul(f"mix_b_in{j}", cur, gain("mix_pre_g", i), w_in)
            (nxt, m), _ = _mix_b_fwd(f"mix_b_fwd{j}", cur, p, gain("mix_post_g", i), w_grp, scale, w_out)
            mix_saved = dict(x=cur, p=p, m=m, w_in=w_in, w_out=w_out, w_grp=w_grp, scale=scale)
        cur = nxt
        wg, wu, wd = (bufs[n][k].reshape(1, -1, d) for n, k in ffn_keys)
        (nxt, g_pre, u_pre, f, *sq), (got,) = _ffn_fwd(
            f"ffn_fwd{i}", cur, gain("ffn_pre_g", i), gain("ffn_post_g", i), wg, wu, wd,
            target=target if i == depth - 1 else None, exchanges=[gather(after_ffn)])
        gathered(after_ffn, got)
        saved.append((mix_saved, dict(x=cur, g=g_pre, u=u_pre, f=f, wg=wg, wu=wu, wd=wd)))
        cur = nxt

    dcur = cur
    loss = lax.psum(0.5 * sq[0][0, 0] / d, ("x", "y", "c"))

    grads = {name: [None] * weights[name].shape[0] for name in WEIGHTS}
    state = dict(to_sibling=[], to_chips=[])
    pair, from_chips = {}, {}

    def exchanges_due():
        return [_ToSibling([a for _, _, a in state["to_sibling"]]), _ToChips([a for _, _, a in state["to_chips"]])]

    def exchanged(outs):
        from_sibling, arrived = outs
        for (n, k, _), got in zip(state["to_chips"], arrived):
            from_chips[n, k] = got
        state["to_chips"] = []
        for (n, k, dw), got in zip(state["to_sibling"], from_sibling):
            pair[n, k] = _pair_sum(f"pair_sum_{n}{k}", place, dw, got)
            state["to_chips"].append((n, k, pair[n, k]))
        state["to_sibling"] = []

    def made(name, k, dw):
        grads[name][k] = dw
        state["to_sibling"].append((name, k, dw))

    for i in reversed(range(depth)):
        j = i // 2
        mix_saved, ffn_saved = saved[i]
        s = ffn_saved
        (dg, du, dwd, dgain), outs = _ffn_bwd_hidden(
            f"ffn_bwd_hidden{i}", dcur, s["f"], gain("ffn_post_g", i), s["g"], s["u"], s["wd"],
            exchanges=exchanges_due())
        exchanged(outs)
        grads["ffn_post_g"][i] = dgain
        made("ffn_w_down", i, dwd.reshape(N_CHIPS, -1, d))
        (dcur, dwg, dwu, dgain), outs = _bwd_in(
            f"ffn_bwd_in{i}", dcur, s["x"], gain("ffn_pre_g", i), [dg, du], [s["wg"], s["wu"]], transposed=True,
            exchanges=exchanges_due())
        exchanged(outs)
        grads["ffn_pre_g"][i] = dgain
        made("ffn_w_gate", i, dwg.reshape(N_CHIPS, -1, d))
        made("ffn_w_up", i, dwu.reshape(N_CHIPS, -1, d))
        s = mix_saved
        if i % 2 == 0:
            (dz, dwout, dws, dbacc, dlng, dlnb, dgain), outs = _mix_a_bwd_hidden(
                f"mix_a_bwd_hidden{j}", dcur, s["m"], gain("mix_post_g", i), s["zpre"], s["ln_g"], s["ln_b"],
                a_w_s[j], s["b_t"], s["w_out"], exchanges=exchanges_due())
            exchanged(outs)
            grads["a_w_s"][j] = dws
            grads["a_b_s"][j] = jnp.transpose(dbacc)
            grads["a_ln_g"][j] = dlng.reshape(-1)
            grads["a_ln_b"][j] = dlnb.reshape(-1)
            grads["mix_post_g"][i] = dgain
            made("a_w_out", j, dwout.reshape(N_CHIPS, -1, d))
            (dcur, dwin, dgain), outs = _bwd_in(
                f"mix_a_bwd_in{j}", dcur, s["x"], gain("mix_pre_g", i), [dz], [s["w_in"]], exchanges=exchanges_due())
            exchanged(outs)
            grads["mix_pre_g"][i] = dgain
            made("a_w_in", j, dwin)
        else:
            (dq, dwout, dwgrp, dscale, dgain), outs = _mix_b_bwd_hidden(
                f"mix_b_bwd_hidden{j}", dcur, s["m"], gain("mix_post_g", i), s["p"], s["w_grp"], s["scale"],
                s["w_out"], exchanges=exchanges_due())
            exchanged(outs)
            grads["b_scale"][j] = dscale
            grads["mix_post_g"][i] = dgain
            made("b_w_out", j, dwout.reshape(N_CHIPS, -1, d))
            made("b_w_grp", j, jnp.transpose(
                dwgrp.reshape(len(B_WINDOWS), N_CHIPS, gd_b // N_CHIPS, gd_b), (1, 0, 2, 3)).reshape(N_CHIPS, -1, gd_b))
            (dp,), _ = _pool_bwd(f"pool_bwd{j}", dq)
            (dcur, dwin, dgain), outs = _bwd_in(
                f"mix_b_bwd_in{j}", dcur, s["x"], gain("mix_pre_g", i), [dp], [s["w_in"].reshape(1, d, d)],
                exchanges=exchanges_due())
            exchanged(outs)
            grads["mix_pre_g"][i] = dgain
            made("b_w_in", j, dwin.reshape(N_CHIPS, -1, d))
    grad_x = dcur.reshape(x.shape)
    exchanged(_exchange("grads_last", exchanges_due()))
    exchanged(_exchange("grads_last_to_chips", exchanges_due()))

    reduced_names = [name for name in SHARDED if name != "b_scale"]
    sums = []
    for name in reduced_names:
        layers = range(weights[name].shape[0])
        sums.append(_chip_sum("chip_sum_" + name, place, [pair[name, k] for k in layers],
                              [from_chips[name, k] for k in layers]))
    reduced = dict(zip(reduced_names, _exchange("swap_halves", [_SwapHalves(sums)])[0]))

    small_grads = [jnp.stack([g.reshape(weights[name].shape[1:]) for g in grads[name]], axis=0) for name in SMALL]
    scale_grad = jnp.concatenate(grads["b_scale"], axis=0)
    summed = _all_sum_small(_pack_small(small_grads + [scale_grad]))
    small_rows = summed.shape[0] - scale_grad.size // 128
    scale_sum = summed[small_rows:].reshape(scale_grad.shape)
    reduced["b_scale"] = lax.dynamic_slice_in_dim(scale_sum, chip * b_scale.shape[1], b_scale.shape[1], axis=1)[:, None, :]
    summed = summed[:small_rows]

    out_g, out_d, out_m, out_v = {}, {}, {}, {}
    for name in SHARDED:
        shape = weights[name].shape
        dlt, nm, nv = _adamw("adamw_" + name, _as_layers(name, weights[name]), reduced[name],
                             _as_layers(name, mom1[name]), _as_layers(name, mom2[name]))
        out_g[name] = _from_layers(name, reduced[name], shape)
        out_d[name], out_m[name], out_v[name] = (_from_layers(name, a, shape) for a in (dlt, nm, nv))
    small_like = [weights[name] for name in SMALL]
    packs = [_pack_small([src[name] for name in SMALL]).reshape(1, -1, 128) for src in (weights, mom1, mom2)]
    dlt, nm, nv = _adamw("adamw_small", packs[0], summed.reshape(1, -1, 128), packs[1], packs[2])
    for dst, packed in ((out_g, summed), (out_d, dlt[0]), (out_m, nm[0]), (out_v, nv[0])):
        for name, val in zip(SMALL, _unpack_small(packed, small_like)):
            dst[name] = val

    return (loss, grad_x, *[out_g[n] for n in WEIGHTS], *[out_d[n] for n in WEIGHTS],
            *[out_m[n] for n in WEIGHTS], *[out_v[n] for n in WEIGHTS])
```

```python
import functools
import math

import jax
import jax.numpy as jnp
from jax import lax
from jax.experimental import pallas as pl
from jax.experimental.pallas import tpu as pltpu

F32 = jnp.float32
BF16 = jnp.bfloat16
MESH = pl.DeviceIdType.MESH

EPS = 1e-6
CHUNK = 128
A_GROUPS = 8
B_WINDOWS = (2, 4, 8, 16)
HALO = 16
N_CHIPS = 4
N_DEV = 8

ADAM_LR = 0.001
ADAM_B1 = 0.9
ADAM_B2 = 0.999
ADAM_EPS = 1e-08
ADAM_WD = 0.01
ADAM_STEP = 10

VMEM_LIMIT_BYTES = 60 * 1024 * 1024
INV_SQRT2 = 1.0 / math.sqrt(2.0)
INV_SQRT_2PI = 1.0 / math.sqrt(2.0 * math.pi)

_ANY = pl.BlockSpec(memory_space=pl.ANY)
_VM = pl.BlockSpec(memory_space=pltpu.VMEM)


def _params():
    return pltpu.CompilerParams(dimension_semantics=("arbitrary",), vmem_limit_bytes=VMEM_LIMIT_BYTES)


def _token_block(t):
    return 256 if t >= 1024 else 128


def _rows(tm, d):
    return pl.BlockSpec((tm, d), lambda i: (i, 0))


def _blocks(nb, tm, bw):
    return pl.BlockSpec((nb, tm, bw), lambda i: (0, i, 0))


def _dot(a, b):
    return lax.dot_general(a, b, (((1,), (0,)), ((), ())), preferred_element_type=F32)


def _dot_nt(a, b):
    return lax.dot_general(a, b, (((1,), (1,)), ((), ())), preferred_element_type=F32)


def _dot_tn(a, b):
    return lax.dot_general(a, b, (((0,), (0,)), ((), ())), preferred_element_type=F32)


def _rms(x, g):
    return x * lax.rsqrt(jnp.mean(x * x, axis=-1, keepdims=True) + EPS) * g


def _rms_bwd(dy, x, g):
    r = lax.rsqrt(jnp.mean(x * x, axis=-1, keepdims=True) + EPS)
    n = x * r
    dn = dy * g
    dx = r * (dn - n * jnp.mean(dn * n, axis=-1, keepdims=True))
    return dx, jnp.sum(dy * n, axis=0, keepdims=True)


def _gelu_and_grad(x):
    cdf = 0.5 * (1.0 + lax.erf(x * INV_SQRT2))
    return x * cdf, cdf + x * (jnp.exp(-0.5 * x * x) * INV_SQRT_2PI)


def _position():
    return lax.axis_index("x"), lax.axis_index("y"), lax.axis_index("c")


def _other_chips(x, y):
    return [(1 - x, y), (x, 1 - y), (1 - x, 1 - y)]


class _GatherWeights:
    def __init__(self, bufs, whole=()):
        self.inputs = list(bufs)
        self.out_shapes = [jax.ShapeDtypeStruct(b.shape, b.dtype) for b in bufs]
        self.aliases = {w: w for w in range(len(bufs))}
        self.n_sems = 6 * len(bufs)
        self.whole = frozenset(whole)

    def _part(self, outs, w, slot, core):
        if w in self.whole:
            return outs[w].at[slot]
        half = outs[w].shape[1] // 2
        return outs[w].at[slot, pl.ds(core * half, half)]

    def _ici(self, outs, send, recv, w, j, slot):
        x, y, c = _position()
        px, py = _other_chips(x, y)[j]
        part = self._part(outs, w, slot, c)
        return pltpu.make_async_remote_copy(
            src_ref=part, dst_ref=part, send_sem=send.at[6 * w + j], recv_sem=recv.at[6 * w + j],
            device_id=(px, py, c), device_id_type=MESH)

    def _d2d(self, outs, send, recv, w, j, slot, core):
        x, y, c = _position()
        part = self._part(outs, w, slot, core)
        return pltpu.make_async_remote_copy(
            src_ref=part, dst_ref=part, send_sem=send.at[6 * w + 3 + j], recv_sem=recv.at[6 * w + 3 + j],
            device_id=(x, y, 1 - c), device_id_type=MESH)

    def start(self, ins, outs, send, recv):
        x, y, _ = _position()
        for w in range(len(outs)):
            for j in range(3):
                self._ici(outs, send, recv, w, j, 2 * x + y).start()

    def finish(self, ins, outs, send, recv):
        x, y, c = _position()
        slots = [2 * px + py for px, py in _other_chips(x, y)]
        for w in range(len(outs)):
            for j, slot in enumerate(slots):
                self._ici(outs, send, recv, w, j, slot).wait_recv()
                if w not in self.whole:
                    self._d2d(outs, send, recv, w, j, slot, c).start()
        for w in range(len(outs)):
            for j, slot in enumerate(slots):
                if w not in self.whole:
                    self._d2d(outs, send, recv, w, j, slot, 1 - c).wait_recv()
        for w in range(len(outs)):
            for j, slot in enumerate(slots):
                self._ici(outs, send, recv, w, j, 2 * x + y).wait_send()
                if w not in self.whole:
                    self._d2d(outs, send, recv, w, j, slot, c).wait_send()


class _ToSibling:
    def __init__(self, grads):
        self.inputs = list(grads)
        self.out_shapes = [jax.ShapeDtypeStruct((g.shape[0], g.shape[1] // 2, g.shape[2]), g.dtype) for g in grads]
        self.aliases = {}
        self.n_sems = len(grads)

    def _copy(self, ins, outs, send, recv, w):
        x, y, c = _position()
        half = ins[w].shape[1] // 2
        return pltpu.make_async_remote_copy(
            src_ref=ins[w].at[:, pl.ds((1 - c) * half, half)], dst_ref=outs[w],
            send_sem=send.at[w], recv_sem=recv.at[w], device_id=(x, y, 1 - c), device_id_type=MESH)

    def start(self, ins, outs, send, recv):
        for w in range(len(ins)):
            self._copy(ins, outs, send, recv, w).start()

    def finish(self, ins, outs, send, recv):
        for w in range(len(ins)):
            self._copy(ins, outs, send, recv, w).wait_recv()
        for w in range(len(ins)):
            self._copy(ins, outs, send, recv, w).wait_send()


class _ToChips:
    def __init__(self, parts):
        self.inputs = list(parts)
        self.out_shapes = [jax.ShapeDtypeStruct(p.shape, p.dtype) for p in parts]
        self.aliases = {}
        self.n_sems = 3 * len(parts)

    def _copy(self, ins, outs, send, recv, w, j, outbound):
        x, y, c = _position()
        px, py = _other_chips(x, y)[j]
        me, peer = 2 * x + y, 2 * px + py
        src_slot, dst_slot = (peer, me) if outbound else (me, peer)
        return pltpu.make_async_remote_copy(
            src_ref=ins[w].at[src_slot], dst_ref=outs[w].at[dst_slot],
            send_sem=send.at[3 * w + j], recv_sem=recv.at[3 * w + j], device_id=(px, py, c), device_id_type=MESH)

    def start(self, ins, outs, send, recv):
        for w in range(len(ins)):
            for j in range(3):
                self._copy(ins, outs, send, recv, w, j, True).start()

    def finish(self, ins, outs, send, recv):
        for w in range(len(ins)):
            for j in range(3):
                self._copy(ins, outs, send, recv, w, j, False).wait_recv()
        for w in range(len(ins)):
            for j in range(3):
                self._copy(ins, outs, send, recv, w, j, True).wait_send()


class _SwapHalves:
    def __init__(self, bufs):
        self.inputs = list(bufs)
        self.out_shapes = [jax.ShapeDtypeStruct(b.shape, b.dtype) for b in bufs]
        self.aliases = {w: w for w in range(len(bufs))}
        self.n_sems = len(bufs)

    def _copy(self, outs, send, recv, w, core):
        x, y, c = _position()
        half = outs[w].shape[1] // 2
        rows = outs[w].at[:, pl.ds(core * half, half)]
        return pltpu.make_async_remote_copy(
            src_ref=rows, dst_ref=rows, send_sem=send.at[w], recv_sem=recv.at[w],
            device_id=(x, y, 1 - c), device_id_type=MESH)

    def start(self, ins, outs, send, recv):
        c = lax.axis_index("c")
        for w in range(len(outs)):
            self._copy(outs, send, recv, w, c).start()

    def finish(self, ins, outs, send, recv):
        c = lax.axis_index("c")
        for w in range(len(outs)):
            self._copy(outs, send, recv, w, 1 - c).wait_recv()
        for w in range(len(outs)):
            self._copy(outs, send, recv, w, c).wait_send()


def _call(name, body, *, grid, in_specs, out_specs, out_shape, args, scratch_shapes=(), exchanges=()):
    given = list(exchanges)
    exchanges = [e for e in given if e.inputs]
    n_in, n_out, n_scr = len(args), len(out_shape), len(scratch_shapes)
    ex_in = [a for e in exchanges for a in e.inputs]
    ex_out = [s for e in exchanges for s in e.out_shapes]
    aliases = {}
    at_in, at_out = n_in, n_out
    for e in exchanges:
        for i, o in e.aliases.items():
            aliases[at_in + i] = at_out + o
        at_in += len(e.inputs)
        at_out += len(e.out_shapes)
    n_steps = grid[0]

    def fused(*refs):
        body_in, refs = refs[:n_in], refs[n_in:]
        ex_in_refs, refs = refs[: len(ex_in)], refs[len(ex_in) :]
        body_out, refs = refs[:n_out], refs[n_out:]
        ex_out_refs, refs = refs[: len(ex_out)], refs[len(ex_out) :]
        body_scr, sems = refs[:n_scr], refs[n_scr:]

        def each(stage):
            a = b = 0
            for n, e in enumerate(exchanges):
                ins, outs = ex_in_refs[a : a + len(e.inputs)], ex_out_refs[b : b + len(e.out_shapes)]
                getattr(e, stage)(ins, outs, sems[2 * n], sems[2 * n + 1])
                a += len(e.inputs)
                b += len(e.out_shapes)

        if exchanges:
            @pl.when(pl.program_id(0) == 0)
            def _():
                each("start")

        if body is not None:
            body(*body_in, *body_out, *body_scr)

        if exchanges:
            @pl.when(pl.program_id(0) == n_steps - 1)
            def _():
                each("finish")

    outs = pl.pallas_call(
        fused,
        name=name,
        grid=grid,
        in_specs=list(in_specs) + [_ANY] * len(ex_in),
        out_specs=list(out_specs) + [_ANY] * len(ex_out),
        out_shape=list(out_shape) + ex_out,
        input_output_aliases=aliases,
        scratch_shapes=list(scratch_shapes)
        + [pltpu.SemaphoreType.DMA((e.n_sems,)) for e in exchanges for _ in range(2)],
        compiler_params=_params(),
    )(*args, *ex_in)
    body_outs, rest = list(outs[:n_out]), list(outs[n_out:])
    ex_outs = []
    for e in given:
        n_e = len(e.out_shapes) if e.inputs else 0
        ex_outs.append(rest[:n_e])
        rest = rest[n_e:]
    return body_outs, ex_outs


def _exchange(name, exchanges):
    return _call(name, None, grid=(1,), in_specs=[], out_specs=[], out_shape=[], args=[], exchanges=exchanges)[1]


def _ffn_fwd(name, x, gpre, gpost, wg, wu, wd, target=None, exchanges=()):
    t, d = x.shape
    nb, fs, _ = wg.shape
    tm = _token_block(t)
    with_loss = target is not None

    def body(x_ref, gpre_ref, gpost_ref, wg_ref, wu_ref, wd_ref, *refs):
        if with_loss:
            t_ref, o_ref, a_ref, dup_ref, dgate_ref, f_ref, sq_ref = refs
        else:
            o_ref, a_ref, dup_ref, dgate_ref, f_ref = refs
        xv = x_ref[...]
        hb = _rms(xv, gpre_ref[...]).astype(BF16)
        f = jnp.zeros((tm, d), F32)
        for k in range(nb):
            g = _dot_nt(hb, wg_ref[k])
            u = _dot_nt(hb, wu_ref[k])
            s = jax.nn.sigmoid(g)
            sg = g * s
            a = (sg * u).astype(BF16)
            a_ref[k] = a
            dup_ref[k] = sg.astype(BF16)
            dgate_ref[k] = (u * (s * (1.0 + g * (1.0 - s)))).astype(BF16)
            f = f + _dot(a, wd_ref[k])
        f_ref[...] = f
        y = xv + _rms(f, gpost_ref[...])
        if with_loss:
            @pl.when(pl.program_id(0) == 0)
            def _():
                sq_ref[...] = jnp.zeros_like(sq_ref)

            err = y - t_ref[...]
            o_ref[...] = err / d
            sq_ref[...] += jnp.sum(err * err)
        else:
            o_ref[...] = y

    return _call(
        name,
        body,
        grid=(t // tm,),
        in_specs=[_rows(tm, d), _VM, _VM, _VM, _VM, _VM] + [_rows(tm, d)] * with_loss,
        out_specs=[_rows(tm, d)] + [_blocks(nb, tm, fs)] * 3 + [_rows(tm, d)] + [_VM] * with_loss,
        out_shape=[jax.ShapeDtypeStruct((t, d), F32)]
        + [jax.ShapeDtypeStruct((nb, t, fs), BF16)] * 3
        + [jax.ShapeDtypeStruct((t, d), F32)]
        + [jax.ShapeDtypeStruct((8, 128), F32)] * with_loss,
        args=[x, gpre, gpost, wg, wu, wd] + [target] * with_loss,
        exchanges=exchanges,
    )


def _ffn_bwd_hidden(name, dy, f, gpost, a, dup, dgate, wd, exchanges=()):
    t, d = dy.shape
    nb, fs, _ = wd.shape
    tm = _token_block(t)

    def body(dy_ref, f_ref, gpost_ref, a_ref, dup_ref, dgate_ref, wd_ref, dg_ref, du_ref, dwd_ref, dgain_ref):
        @pl.when(pl.program_id(0) == 0)
        def _():
            dwd_ref[...] = jnp.zeros_like(dwd_ref)
            dgain_ref[...] = jnp.zeros_like(dgain_ref)

        df, dgain = _rms_bwd(dy_ref[...], f_ref[...], gpost_ref[...])
        dgain_ref[...] += dgain
        dfb = df.astype(BF16)
        for k in range(nb):
            da = _dot_nt(dfb, wd_ref[k])
            dwd_ref[k] += _dot_tn(a_ref[k], dfb)
            du_ref[k] = (da * dup_ref[k].astype(F32)).astype(BF16)
            dg_ref[k] = (da * dgate_ref[k].astype(F32)).astype(BF16)

    return _call(
        name,
        body,
        grid=(t // tm,),
        in_specs=[_rows(tm, d), _rows(tm, d), _VM] + [_blocks(nb, tm, fs)] * 3 + [_VM],
        out_specs=[_blocks(nb, tm, fs), _blocks(nb, tm, fs), _VM, _VM],
        out_shape=[
            jax.ShapeDtypeStruct((nb, t, fs), BF16),
            jax.ShapeDtypeStruct((nb, t, fs), BF16),
            jax.ShapeDtypeStruct((nb, fs, d), F32),
            jax.ShapeDtypeStruct((1, d), F32),
        ],
        args=[dy, f, gpost, a, dup, dgate, wd],
        exchanges=exchanges,
    )


def _bwd_in(name, dres, x, gpre, dzs, ws, transposed=False, exchanges=()):
    t, d = x.shape
    n = len(ws)
    tm = _token_block(t)
    widths = [w.shape[1] if transposed else w.shape[2] for w in ws]

    def body(*refs):
        dres_ref, x_ref, gpre_ref = refs[:3]
        dz_refs = refs[3 : 3 + n]
        w_refs = refs[3 + n : 3 + 2 * n]
        dx_ref = refs[3 + 2 * n]
        dw_refs = refs[4 + 2 * n : 4 + 3 * n]
        dgain_ref = refs[4 + 3 * n]

        @pl.when(pl.program_id(0) == 0)
        def _():
            for dw_ref in dw_refs:
                dw_ref[...] = jnp.zeros_like(dw_ref)
            dgain_ref[...] = jnp.zeros_like(dgain_ref)

        xv = x_ref[...]
        gain = gpre_ref[...]
        hb = _rms(xv, gain).astype(BF16)
        dh = jnp.zeros((tm, d), F32)
        for dz_ref, w_ref, dw_ref in zip(dz_refs, w_refs, dw_refs):
            for k in range(w_ref.shape[0]):
                dz = dz_ref[k]
                if transposed:
                    dh = dh + _dot(dz, w_ref[k])
                    dw_ref[k] += _dot_tn(dz, hb)
                else:
                    dh = dh + _dot_nt(dz, w_ref[k])
                    dw_ref[k] += _dot_tn(hb, dz)
        dx, dgain = _rms_bwd(dh, xv, gain)
        dx_ref[...] = dres_ref[...] + dx
        dgain_ref[...] += dgain

    return _call(
        name,
        body,
        grid=(t // tm,),
        in_specs=[_rows(tm, d), _rows(tm, d), _VM]
        + [_blocks(w.shape[0], tm, bw) for w, bw in zip(ws, widths)]
        + [_VM] * n,
        out_specs=[_rows(tm, d)] + [_VM] * n + [_VM],
        out_shape=[jax.ShapeDtypeStruct((t, d), F32)]
        + [jax.ShapeDtypeStruct(w.shape, F32) for w in ws]
        + [jax.ShapeDtypeStruct((1, d), F32)],
        args=[dres, x, gpre, *dzs, *ws],
        exchanges=exchanges,
    )


def _causal_weights(ws_ref):
    row = lax.broadcasted_iota(jnp.int32, (CHUNK, CHUNK), 0)
    col = lax.broadcasted_iota(jnp.int32, (CHUNK, CHUNK), 1)
    return [jnp.where(row >= col, ws_ref[g], 0.0).astype(BF16) for g in range(A_GROUPS)]


def _layernorm_halves(v0, v1):
    width = v0.shape[-1] + v1.shape[-1]
    mu = (jnp.sum(v0, axis=-1, keepdims=True) + jnp.sum(v1, axis=-1, keepdims=True)) / width
    c0 = v0 - mu
    c1 = v1 - mu
    var = (jnp.sum(c0 * c0, axis=-1, keepdims=True) + jnp.sum(c1 * c1, axis=-1, keepdims=True)) / width
    rstd = lax.rsqrt(var + EPS)
    return c0 * rstd, c1 * rstd, rstd


def _spatial_gate(sv_ref, wtril, vl, bt_ref, half, tm, gd):
    for gg in range(A_GROUPS // 2):
        g = half * (A_GROUPS // 2) + gg
        bias = bt_ref[:, g : g + 1]
        for n in range(tm // CHUNK):
            blk = vl[n * CHUNK : (n + 1) * CHUNK, gg * gd : (gg + 1) * gd]
            sv_ref[n * CHUNK : (n + 1) * CHUNK, gg * gd : (gg + 1) * gd] = _dot(wtril[g], blk) + bias


def _mix_a_fwd(name, x, gpre, gpost, w_in, ln_g, ln_b, w_s, b_t, w_out, exchanges=()):
    t, d = x.shape
    _, _, q = w_in.shape
    gd = 2 * q // A_GROUPS
    tm = _token_block(t)

    def body(x_ref, gpre_ref, gpost_ref, win_ref, lng_ref, lnb_ref, ws_ref, bt_ref, wout_ref,
             o_ref, z_ref, dz_ref, m_ref, sv_ref):
        xv = x_ref[...]
        hb = _rms(xv, gpre_ref[...]).astype(BF16)
        z = []
        for k in range(4):
            act, slope = _gelu_and_grad(_dot(hb, win_ref[k]))
            z_ref[k] = act.astype(BF16)
            dz_ref[k] = slope.astype(BF16)
            z.append(act)
        vh0, vh1, _ = _layernorm_halves(z[2], z[3])
        vls = [(vh * lng_ref[b : b + 1, :] + lnb_ref[b : b + 1, :]).astype(BF16) for b, vh in enumerate((vh0, vh1))]
        wtril = _causal_weights(ws_ref)
        m = jnp.zeros((tm, d), F32)
        for b in range(2):
            _spatial_gate(sv_ref, wtril, vls[b], bt_ref, b, tm, gd)
            gated = (z[b] * sv_ref[...]).astype(BF16)
            m = m + _dot(gated, wout_ref[b])
        m_ref[...] = m
        o_ref[...] = xv + _rms(m, gpost_ref[...])

    return _call(
        name,
        body,
        grid=(t // tm,),
        in_specs=[_rows(tm, d)] + [_VM] * 8,
        out_specs=[_rows(tm, d), _blocks(4, tm, q), _blocks(4, tm, q), _rows(tm, d)],
        out_shape=[
            jax.ShapeDtypeStruct((t, d), F32),
            jax.ShapeDtypeStruct((4, t, q), BF16),
            jax.ShapeDtypeStruct((4, t, q), BF16),
            jax.ShapeDtypeStruct((t, d), F32),
        ],
        scratch_shapes=[pltpu.VMEM((tm, q), F32)],
        args=[x, gpre, gpost, w_in, ln_g, ln_b, w_s, b_t, w_out],
        exchanges=exchanges,
    )


def _mix_a_bwd_hidden(name, dy, m, gpost, act, slope, ln_g, ln_b, w_s, b_t, w_out, exchanges=()):
    t, d = dy.shape
    _, _, q = act.shape
    gd = 2 * q // A_GROUPS
    tm = _token_block(t)
    n_chunks = tm // CHUNK

    def body(dy_ref, m_ref, gpost_ref, z_ref, slope_ref, lng_ref, lnb_ref, ws_ref, bt_ref, wout_ref,
             dz_ref, dwout_ref, dws_ref, dbacc_ref, dlng_ref, dlnb_ref, dgain_ref, sv_ref, dvl_ref):
        first = pl.program_id(0) == 0

        @pl.when(first)
        def _():
            for ref in (dwout_ref, dws_ref, dbacc_ref, dlng_ref, dlnb_ref, dgain_ref):
                ref[...] = jnp.zeros_like(ref)

        dm, dgain = _rms_bwd(dy_ref[...], m_ref[...], gpost_ref[...])
        dgain_ref[...] += dgain
        dmb = dm.astype(BF16)
        vhs = list(_layernorm_halves(z_ref[2].astype(F32), z_ref[3].astype(F32)))
        rstd = vhs.pop()
        vls = [(vh * lng_ref[b : b + 1, :] + lnb_ref[b : b + 1, :]).astype(BF16) for b, vh in enumerate(vhs)]
        wtril = _causal_weights(ws_ref)
        dvhs = []
        for b in range(2):
            u = z_ref[b].astype(F32)
            _spatial_gate(sv_ref, wtril, vls[b], bt_ref, b, tm, gd)
            sv = sv_ref[...]
            gated = (u * sv).astype(BF16)
            dgated = _dot_nt(dmb, wout_ref[b])
            dwout_ref[b] += _dot_tn(gated, dmb)
            dz_ref[b] = (dgated * sv * slope_ref[b].astype(F32)).astype(BF16)
            dsv = dgated * u
            folded = dsv[0:CHUNK, :]
            for c in range(1, n_chunks):
                folded = folded + dsv[c * CHUNK : (c + 1) * CHUNK, :]
            for gg in range(A_GROUPS // 2):
                g = b * (A_GROUPS // 2) + gg
                dbacc_ref[:, g : g + 1] += jnp.sum(folded[:, gg * gd : (gg + 1) * gd], axis=1, keepdims=True)
            dsvb = dsv.astype(BF16)
            for gg in range(A_GROUPS // 2):
                g = b * (A_GROUPS // 2) + gg
                for c in range(n_chunks):
                    rows = slice(c * CHUNK, (c + 1) * CHUNK)
                    cols = slice(gg * gd, (gg + 1) * gd)
                    blk = dsvb[rows, cols]
                    dvl_ref[rows, cols] = _dot_tn(wtril[g], blk)
                    dws_ref[g] += _dot_nt(blk, vls[b][rows, cols])
            dvl = dvl_ref[...]
            dlng_ref[b : b + 1, :] += jnp.sum(dvl * vhs[b], axis=0, keepdims=True)
            dlnb_ref[b : b + 1, :] += jnp.sum(dvl, axis=0, keepdims=True)
            dvhs.append(dvl * lng_ref[b : b + 1, :])
        width = 2.0 * q
        m1 = (jnp.sum(dvhs[0], axis=-1, keepdims=True) + jnp.sum(dvhs[1], axis=-1, keepdims=True)) / width
        m2 = (jnp.sum(dvhs[0] * vhs[0], axis=-1, keepdims=True)
              + jnp.sum(dvhs[1] * vhs[1], axis=-1, keepdims=True)) / width
        for b in range(2):
            dv = rstd * (dvhs[b] - m1 - vhs[b] * m2)
            dz_ref[2 + b] = (dv * slope_ref[2 + b].astype(F32)).astype(BF16)

        @pl.when(pl.program_id(0) == t // tm - 1)
        def _():
            row = lax.broadcasted_iota(jnp.int32, (CHUNK, CHUNK), 0)
            col = lax.broadcasted_iota(jnp.int32, (CHUNK, CHUNK), 1)
            for g in range(A_GROUPS):
                dws_ref[g] = jnp.where(row >= col, dws_ref[g], 0.0)

    return _call(
        name,
        body,
        grid=(t // tm,),
        in_specs=[_rows(tm, d), _rows(tm, d), _VM, _blocks(4, tm, q), _blocks(4, tm, q)] + [_VM] * 5,
        out_specs=[_blocks(4, tm, q)] + [_VM] * 6,
        out_shape=[
            jax.ShapeDtypeStruct((4, t, q), BF16),
            jax.ShapeDtypeStruct((2, q, d), F32),
            jax.ShapeDtypeStruct((A_GROUPS, CHUNK, CHUNK), F32),
            jax.ShapeDtypeStruct((CHUNK, A_GROUPS), F32),
            jax.ShapeDtypeStruct((2, q), F32),
            jax.ShapeDtypeStruct((2, q), F32),
            jax.ShapeDtypeStruct((1, d), F32),
        ],
        scratch_shapes=[pltpu.VMEM((tm, q), F32), pltpu.VMEM((tm, q), F32)],
        args=[dy, m, gpost, act, slope, ln_g, ln_b, w_s, b_t, w_out],
        exchanges=exchanges,
    )


def _norm_matmul(name, x, gpre, w, exchanges=()):
    t, d = x.shape
    n = w.shape[1]
    tm = _token_block(t)

    def body(x_ref, gpre_ref, w_ref, o_ref):
        o_ref[...] = _dot(_rms(x_ref[...], gpre_ref[...]).astype(BF16), w_ref[...])

    return _call(
        name,
        body,
        grid=(t // tm,),
        in_specs=[_rows(tm, d), _VM, _VM],
        out_specs=[_rows(tm, n)],
        out_shape=[jax.ShapeDtypeStruct((t, n), F32)],
        args=[x, gpre, w],
        exchanges=exchanges,
    )


def _window_counts(tm, win):
    pos = pl.program_id(0) * tm + lax.broadcasted_iota(jnp.int32, (tm, 1), 0)
    return jnp.minimum(pos + 1, win).astype(F32)


def _pooled(p, halo, tm, gd):
    prev = jnp.where(pl.program_id(0) == 0, 0.0, halo)
    ext = jnp.concatenate([prev, p], axis=0)
    out = []
    for g, win in enumerate(B_WINDOWS):
        s = ext[:, g * gd : (g + 1) * gd]
        step = 1
        while step < win:
            s = s + pltpu.roll(s, step, 0)
            step *= 2
        total = s[HALO:, :]
        out.append(total / _window_counts(tm, win) - p[:, g * gd : (g + 1) * gd])
    return out


def _halo_spec(t, tm, d, ahead):
    per = tm // HALO
    if ahead:
        return pl.BlockSpec((HALO, d), lambda i: (jnp.minimum((i + 1) * per, t // HALO - 1), 0))
    return pl.BlockSpec((HALO, d), lambda i: (jnp.maximum(i * per - 1, 0), 0))


def _mix_b_fwd(name, x, p, gpost, w_grp, scale, w_out, exchanges=()):
    t, d = x.shape
    gd = d // len(B_WINDOWS)
    tm = _token_block(t)

    def body(x_ref, p_ref, halo_ref, gpost_ref, wgrp_ref, scale_ref, wout_ref, o_ref, m_ref):
        pooled = _pooled(p_ref[...], halo_ref[...], tm, gd)
        mixed = jnp.concatenate([_dot(pg.astype(BF16), wgrp_ref[g]) for g, pg in enumerate(pooled)], axis=1)
        m = _dot((mixed * scale_ref[...]).astype(BF16), wout_ref[...])
        m_ref[...] = m
        o_ref[...] = x_ref[...] + _rms(m, gpost_ref[...])

    return _call(
        name,
        body,
        grid=(t // tm,),
        in_specs=[_rows(tm, d), _rows(tm, d), _halo_spec(t, tm, d, False), _VM, _VM, _VM, _VM],
        out_specs=[_rows(tm, d), _rows(tm, d)],
        out_shape=[jax.ShapeDtypeStruct((t, d), F32), jax.ShapeDtypeStruct((t, d), F32)],
        args=[x, p, p, gpost, w_grp, scale, w_out],
        exchanges=exchanges,
    )


def _mix_b_bwd_hidden(name, dy, m, gpost, p, w_grp, scale, w_out, exchanges=()):
    t, d = dy.shape
    gd = d // len(B_WINDOWS)
    tm = _token_block(t)

    def body(dy_ref, m_ref, gpost_ref, p_ref, halo_ref, wgrp_ref, scale_ref, wout_ref,
             dq_ref, dwout_ref, dwgrp_ref, dscale_ref, dgain_ref):
        @pl.when(pl.program_id(0) == 0)
        def _():
            for ref in (dwout_ref, dwgrp_ref, dscale_ref, dgain_ref):
                ref[...] = jnp.zeros_like(ref)

        dm, dgain = _rms_bwd(dy_ref[...], m_ref[...], gpost_ref[...])
        dgain_ref[...] += dgain
        dmb = dm.astype(BF16)
        pooled = [pg.astype(BF16) for pg in _pooled(p_ref[...], halo_ref[...], tm, gd)]
        mixed = jnp.concatenate([_dot(pg, wgrp_ref[g]) for g, pg in enumerate(pooled)], axis=1)
        scale = scale_ref[...]
        ms = (mixed * scale).astype(BF16)
        dms = _dot_nt(dmb, wout_ref[...])
        dwout_ref[...] += _dot_tn(ms, dmb)
        dscale_ref[...] += jnp.sum(dms * mixed, axis=0, keepdims=True)
        dmixed = (dms * scale).astype(BF16)
        for g, win in enumerate(B_WINDOWS):
            cols = slice(g * gd, (g + 1) * gd)
            dmg = dmixed[:, cols]
            dwgrp_ref[g] += _dot_tn(pooled[g], dmg)
            dq_ref[:, cols] = _dot_nt(dmg, wgrp_ref[g]) / _window_counts(tm, win)

    return _call(
        name,
        body,
        grid=(t // tm,),
        in_specs=[_rows(tm, d), _rows(tm, d), _VM, _rows(tm, d), _halo_spec(t, tm, d, False), _VM, _VM, _VM],
        out_specs=[_rows(tm, d), _VM, _VM, _VM, _VM],
        out_shape=[
            jax.ShapeDtypeStruct((t, d), F32),
            jax.ShapeDtypeStruct((d, d), F32),
            jax.ShapeDtypeStruct((len(B_WINDOWS), gd, gd), F32),
            jax.ShapeDtypeStruct((1, d), F32),
            jax.ShapeDtypeStruct((1, d), F32),
        ],
        args=[dy, m, gpost, p, p, w_grp, scale, w_out],
        exchanges=exchanges,
    )


def _pool_bwd(name, dq, exchanges=()):
    t, d = dq.shape
    gd = d // len(B_WINDOWS)
    tm = _token_block(t)
    n_steps = t // tm

    def body(dq_ref, halo_ref, dp_ref):
        dq_blk = dq_ref[...]
        nxt = jnp.where(pl.program_id(0) == n_steps - 1, 0.0, halo_ref[...])
        ext = jnp.concatenate([dq_blk, nxt], axis=0)
        for g, win in enumerate(B_WINDOWS):
            cols = slice(g * gd, (g + 1) * gd)
            s = ext[:, cols]
            step = 1
            while step < win:
                s = s + pltpu.roll(s, tm + HALO - step, 0)
                step *= 2
            dp_ref[0, :, cols] = (s[:tm, :] - dq_blk[:, cols] * _window_counts(tm, win)).astype(BF16)

    return _call(
        name,
        body,
        grid=(n_steps,),
        in_specs=[_rows(tm, d), _halo_spec(t, tm, d, True)],
        out_specs=[_blocks(1, tm, d)],
        out_shape=[jax.ShapeDtypeStruct((1, t, d), BF16)],
        args=[dq, dq],
        exchanges=exchanges,
    )


def _cast_into_slots(name, place, w, dtype):
    n_layers, r, c = w.shape

    def body(place_ref, w_ref, *o_refs):
        del place_ref
        for j, o_ref in enumerate(o_refs):
            @pl.when(pl.program_id(0) == j)
            def _():
                o_ref[...] = w_ref[...].astype(dtype)

    return pl.pallas_call(
        body,
        name=name,
        grid_spec=pltpu.PrefetchScalarGridSpec(
            num_scalar_prefetch=1,
            grid=(n_layers,),
            in_specs=[pl.BlockSpec((1, r, c), lambda i, place_ref: (i, 0, 0))],
            out_specs=[pl.BlockSpec((1, r, c), lambda i, place_ref: (place_ref[0], 0, 0))] * n_layers,
        ),
        out_shape=[jax.ShapeDtypeStruct((N_CHIPS, r, c), dtype)] * n_layers,
        compiler_params=_params(),
    )(place, w)


def _row_tile(r):
    return 256 if r % 256 == 0 else r


def _pair_sum(name, place, dw, recv):
    _, r, c = dw.shape
    half = r // 2
    tr = _row_tile(half)
    per = half // tr

    def body(place_ref, a_ref, b_ref, o_ref):
        del place_ref
        o_ref[...] = (a_ref[...] + b_ref[...]).astype(BF16)

    return pl.pallas_call(
        body,
        name=name,
        grid_spec=pltpu.PrefetchScalarGridSpec(
            num_scalar_prefetch=1,
            grid=(N_CHIPS, per),
            in_specs=[
                pl.BlockSpec((1, tr, c), lambda k, i, place_ref: (k, place_ref[1] * per + i, 0)),
                pl.BlockSpec((1, tr, c), lambda k, i, place_ref: (k, i, 0)),
            ],
            out_specs=pl.BlockSpec((1, tr, c), lambda k, i, place_ref: (k, i, 0)),
        ),
        out_shape=jax.ShapeDtypeStruct(recv.shape, BF16),
        compiler_params=pltpu.CompilerParams(
            dimension_semantics=("arbitrary",) * 2, vmem_limit_bytes=VMEM_LIMIT_BYTES),
    )(place, dw, recv)


def _chip_sum(name, place, mine, others):
    n_layers = len(mine)
    _, half, c = mine[0].shape
    tr = _row_tile(half)
    per = half // tr

    def body(place_ref, *refs):
        del place_ref
        o_ref = refs[-1]
        for j in range(n_layers):
            @pl.when(pl.program_id(0) == j)
            def _():
                parts = refs[4 * j : 4 * j + 4]
                acc = parts[0][...].astype(F32) + parts[1][...].astype(F32)
                acc = acc + parts[2][...].astype(F32)
                o_ref[...] = acc + parts[3][...].astype(F32)

    def part(j, flip):
        return pl.BlockSpec((1, tr, c), lambda l, i, place_ref: (
            jnp.bitwise_xor(place_ref[0], flip), jnp.where(l == j, i, 0), 0))

    args = []
    for j in range(n_layers):
        args += [mine[j], others[j], others[j], others[j]]
    return pl.pallas_call(
        body,
        name=name,
        grid_spec=pltpu.PrefetchScalarGridSpec(
            num_scalar_prefetch=1,
            grid=(n_layers, per),
            in_specs=[part(j, flip) for j in range(n_layers) for flip in range(N_CHIPS)],
            out_specs=pl.BlockSpec((1, tr, c), lambda l, i, place_ref: (l, place_ref[1] * per + i, 0)),
        ),
        out_shape=jax.ShapeDtypeStruct((n_layers, 2 * half, c), F32),
        compiler_params=pltpu.CompilerParams(
            dimension_semantics=("arbitrary",) * 2, vmem_limit_bytes=VMEM_LIMIT_BYTES),
    )(place, *args)


def _adamw(name, w, g, m, v):
    n_layers, r, c = w.shape
    tr = _row_tile(r)

    def body(w_ref, g_ref, m_ref, v_ref, d_ref, nm_ref, nv_ref):
        gv = g_ref[...]
        nm = ADAM_B1 * m_ref[...] + (1.0 - ADAM_B1) * gv
        nv = ADAM_B2 * v_ref[...] + (1.0 - ADAM_B2) * jnp.square(gv)
        m_hat = nm / (1.0 - ADAM_B1 ** ADAM_STEP)
        v_hat = nv / (1.0 - ADAM_B2 ** ADAM_STEP)
        d_ref[...] = -ADAM_LR * (m_hat / (jnp.sqrt(v_hat) + ADAM_EPS) + ADAM_WD * w_ref[...])
        nm_ref[...] = nm
        nv_ref[...] = nv

    spec = pl.BlockSpec((1, tr, c), lambda l, i: (l, i, 0))
    return pl.pallas_call(
        body,
        name=name,
        grid=(n_layers, r // tr),
        in_specs=[spec] * 4,
        out_specs=[spec] * 3,
        out_shape=[jax.ShapeDtypeStruct(w.shape, F32)] * 3,
        compiler_params=pltpu.CompilerParams(
            dimension_semantics=("arbitrary",) * 2, vmem_limit_bytes=VMEM_LIMIT_BYTES),
    )(w, g, m, v)


def _all_sum_small(packed):
    m_per, n = packed.shape

    def body(x_ref, sum_ref, all_ref, send_sems, recv_sems, local_sem):
        x, y, c = _position()
        me, sibling = (x, y, c), (x, y, 1 - c)
        chips = _other_chips(x, y)

        def rows(px, py, pc):
            return all_ref.at[pl.ds((4 * px + 2 * py + pc) * m_per, m_per), :]

        def copy(k, block, to, src=None):
            return pltpu.make_async_remote_copy(
                src_ref=rows(*block) if src is None else src, dst_ref=rows(*block),
                send_sem=send_sems.at[k], recv_sem=recv_sems.at[k], device_id=to, device_id_type=MESH)

        mine = pltpu.make_async_copy(x_ref, rows(*me), local_sem)
        mine.start()
        first = [copy(0, me, sibling, src=x_ref)]
        first += [copy(1 + j, me, (*chip, c), src=x_ref) for j, chip in enumerate(chips)]
        for cp in first:
            cp.start()
        passed = [copy(4 + j, (*chip, c), sibling) for j, chip in enumerate(chips)]
        for j, chip in enumerate(chips):
            copy(1 + j, (*chip, c), me).wait_recv()
            passed[j].start()
        copy(0, sibling, me).wait_recv()
        for j, chip in enumerate(chips):
            copy(4 + j, (*chip, 1 - c), me).wait_recv()
        for cp in first + passed:
            cp.wait_send()
        mine.wait()
        acc = all_ref[0:m_per, :]
        for k in range(1, N_DEV):
            acc = acc + all_ref[k * m_per : (k + 1) * m_per, :]
        sum_ref[...] = acc

    return pl.pallas_call(
        body,
        name="all_sum_small",
        in_specs=[_VM],
        out_specs=_VM,
        out_shape=jax.ShapeDtypeStruct((m_per, n), F32),
        scratch_shapes=[
            pltpu.VMEM((N_DEV * m_per, n), F32),
            pltpu.SemaphoreType.DMA((7,)),
            pltpu.SemaphoreType.DMA((7,)),
            pltpu.SemaphoreType.DMA,
        ],
        compiler_params=pltpu.CompilerParams(vmem_limit_bytes=VMEM_LIMIT_BYTES),
    )(packed)


SHARDED = ("a_w_in", "a_w_out", "b_w_in", "b_w_grp", "b_scale", "b_w_out", "ffn_w_gate", "ffn_w_up", "ffn_w_down")
SMALL = ("a_ln_g", "a_ln_b", "a_w_s", "a_b_s", "mix_pre_g", "mix_post_g", "ffn_pre_g", "ffn_post_g")
WEIGHTS = ("a_w_in", "a_ln_g", "a_ln_b", "a_w_s", "a_b_s", "a_w_out", "b_w_in", "b_w_grp", "b_scale", "b_w_out",
           "mix_pre_g", "mix_post_g", "ffn_pre_g", "ffn_post_g", "ffn_w_gate", "ffn_w_up", "ffn_w_down")


TRANSPOSED = ("ffn_w_gate", "ffn_w_up")


def _as_layers(name, a):
    if name in TRANSPOSED:
        return jnp.swapaxes(a, 1, 2)
    if a.ndim == 2:
        return a.reshape(a.shape[0], 1, a.shape[1])
    return a.reshape(a.shape[0], -1, a.shape[-1])


def _from_layers(name, a, shape):
    if name in TRANSPOSED:
        return jnp.swapaxes(a, 1, 2)
    return a.reshape(shape)


def _pack_small(parts):
    return jnp.concatenate([p.reshape(-1, 128) for p in parts], axis=0)


def _unpack_small(packed, like):
    out, row = [], 0
    for ref in like:
        rows = ref.size // 128
        out.append(packed[row : row + rows].reshape(ref.shape))
        row += rows
    return out


def kernel(x, a_w_in, a_ln_g, a_ln_b, a_w_s, a_b_s, a_w_out, b_w_in, b_w_grp, b_scale, b_w_out, mix_pre_g, mix_post_g, ffn_pre_g, ffn_post_g, ffn_w_gate, ffn_w_up, ffn_w_down, loss_target, m_a_w_in, m_a_ln_g, m_a_ln_b, m_a_w_s, m_a_b_s, m_a_w_out, m_b_w_in, m_b_w_grp, m_b_scale, m_b_w_out, m_mix_pre_g, m_mix_post_g, m_ffn_pre_g, m_ffn_post_g, m_ffn_w_gate, m_ffn_w_up, m_ffn_w_down, v_a_w_in, v_a_ln_g, v_a_ln_b, v_a_w_s, v_a_b_s, v_a_w_out, v_b_w_in, v_b_w_grp, v_b_scale, v_b_w_out, v_mix_pre_g, v_mix_post_g, v_ffn_pre_g, v_ffn_post_g, v_ffn_w_gate, v_ffn_w_up, v_ffn_w_down):
    weights = dict(a_w_in=a_w_in, a_ln_g=a_ln_g, a_ln_b=a_ln_b, a_w_s=a_w_s, a_b_s=a_b_s, a_w_out=a_w_out,
                   b_w_in=b_w_in, b_w_grp=b_w_grp, b_scale=b_scale, b_w_out=b_w_out, mix_pre_g=mix_pre_g,
                   mix_post_g=mix_post_g, ffn_pre_g=ffn_pre_g, ffn_post_g=ffn_post_g, ffn_w_gate=ffn_w_gate,
                   ffn_w_up=ffn_w_up, ffn_w_down=ffn_w_down)
    mom1 = dict(a_w_in=m_a_w_in, a_ln_g=m_a_ln_g, a_ln_b=m_a_ln_b, a_w_s=m_a_w_s, a_b_s=m_a_b_s, a_w_out=m_a_w_out,
                b_w_in=m_b_w_in, b_w_grp=m_b_w_grp, b_scale=m_b_scale, b_w_out=m_b_w_out, mix_pre_g=m_mix_pre_g,
                mix_post_g=m_mix_post_g, ffn_pre_g=m_ffn_pre_g, ffn_post_g=m_ffn_post_g, ffn_w_gate=m_ffn_w_gate,
                ffn_w_up=m_ffn_w_up, ffn_w_down=m_ffn_w_down)
    mom2 = dict(a_w_in=v_a_w_in, a_ln_g=v_a_ln_g, a_ln_b=v_a_ln_b, a_w_s=v_a_w_s, a_b_s=v_a_b_s, a_w_out=v_a_w_out,
                b_w_in=v_b_w_in, b_w_grp=v_b_w_grp, b_scale=v_b_scale, b_w_out=v_b_w_out, mix_pre_g=v_mix_pre_g,
                mix_post_g=v_mix_post_g, ffn_pre_g=v_ffn_pre_g, ffn_post_g=v_ffn_post_g, ffn_w_gate=v_ffn_w_gate,
                ffn_w_up=v_ffn_w_up, ffn_w_down=v_ffn_w_down)

    t, d = x.shape[1], x.shape[2]
    depth = mix_pre_g.shape[0]
    gd_b = d // len(B_WINDOWS)
    xs = x.reshape(t, d)
    target = loss_target.reshape(t, d)

    chip = 2 * lax.axis_index("x") + lax.axis_index("y")
    place = jnp.stack([chip, lax.axis_index("c")]).astype(jnp.int32)
    bufs = {name: list(_cast_into_slots("cast_" + name, place, _as_layers(name, weights[name]),
                                        F32 if name == "b_scale" else BF16)) for name in SHARDED}

    def gain(name, i):
        return weights[name][i].reshape(1, d)

    def weight_keys(i):
        j = i // 2
        mixer = [("a_w_in", j), ("a_w_out", j)] if i % 2 == 0 else [("b_w_in", j), ("b_w_grp", j), ("b_w_out", j)]
        return mixer, [("ffn_w_gate", i), ("ffn_w_up", i), ("ffn_w_down", i)]

    def gather(keys):
        return _GatherWeights([bufs[n][j] for n, j in keys],
                              whole=[k for k, (n, _) in enumerate(keys) if n == "b_scale"])

    def gathered(keys, outs):
        for (n, j), buf in zip(keys, outs):
            bufs[n][j] = buf

    first = weight_keys(0)[0] + [("b_scale", j) for j in range(b_scale.shape[0])]
    gathered(first, _exchange("gather_first", [gather(first)])[0])
    saved = []
    cur = xs
    for i in range(depth):
        j = i // 2
        mixer_next, ffn_next = weight_keys(i + 1) if i + 1 < depth else ([], [])
        ffn_keys = weight_keys(i)[1]
        after_ffn = mixer_next if (i + 1) % 2 == 0 else mixer_next + ffn_next
        if i % 2 == 0:
            w_in = bufs["a_w_in"][j]
            q = w_in.shape[2]
            w_out = bufs["a_w_out"][j].reshape(2, q, d)
            ln_g = a_ln_g[j].reshape(2, q)
            ln_b = a_ln_b[j].reshape(2, q)
            b_t = jnp.transpose(a_b_s[j])
            (nxt, act, slope, m), (got,) = _mix_a_fwd(
                f"mix_a_fwd{j}", cur, gain("mix_pre_g", i), gain("mix_post_g", i), w_in, ln_g, ln_b, a_w_s[j], b_t,
                w_out, exchanges=[gather(ffn_keys)])
            gathered(ffn_keys, got)
            mix_saved = dict(x=cur, act=act, slope=slope, m=m, w_in=w_in, w_out=w_out, ln_g=ln_g, ln_b=ln_b, b_t=b_t)
        else:
            w_in = bufs["b_w_in"][j].reshape(d, d)
            w_out = bufs["b_w_out"][j].reshape(d, d)
            w_grp = jnp.transpose(bufs["b_w_grp"][j].reshape(N_CHIPS, len(B_WINDOWS), gd_b // N_CHIPS, gd_b),
                                  (1, 0, 2, 3)).reshape(len(B_WINDOWS), gd_b, gd_b)
            scale = bufs["b_scale"][j].reshape(1, d)
            (p,), _ = _norm_matmul(f"mix_b_in{j}", cur, gain("mix_pre_g", i), w_in)
            (nxt, m), _ = _mix_b_fwd(f"mix_b_fwd{j}", cur, p, gain("mix_post_g", i), w_grp, scale, w_out)
            mix_saved = dict(x=cur, p=p, m=m, w_in=w_in, w_out=w_out, w_grp=w_grp, scale=scale)
        cur = nxt
        wg, wu, wd = (bufs[n][k].reshape(1, -1, d) for n, k in ffn_keys)
        (nxt, a, dup, dgate, f, *sq), (got,) = _ffn_fwd(
            f"ffn_fwd{i}", cur, gain("ffn_pre_g", i), gain("ffn_post_g", i), wg, wu, wd,
            target=target if i == depth - 1 else None, exchanges=[gather(after_ffn)])
        gathered(after_ffn, got)
        saved.append((mix_saved, dict(x=cur, a=a, dup=dup, dgate=dgate, f=f, wg=wg, wu=wu, wd=wd)))
        cur = nxt

    dcur = cur
    loss = lax.psum(0.5 * sq[0][0, 0] / d, ("x", "y", "c"))

    grads = {name: [None] * weights[name].shape[0] for name in WEIGHTS}
    state = dict(to_sibling=[], to_chips=[])
    pair, from_chips = {}, {}

    def exchanges_due():
        return [_ToSibling([a for _, _, a in state["to_sibling"]]), _ToChips([a for _, _, a in state["to_chips"]])]

    def exchanged(outs):
        from_sibling, arrived = outs
        for (n, k, _), got in zip(state["to_chips"], arrived):
            from_chips[n, k] = got
        state["to_chips"] = []
        for (n, k, dw), got in zip(state["to_sibling"], from_sibling):
            pair[n, k] = _pair_sum(f"pair_sum_{n}{k}", place, dw, got)
            state["to_chips"].append((n, k, pair[n, k]))
        state["to_sibling"] = []

    def made(name, k, dw):
        grads[name][k] = dw
        state["to_sibling"].append((name, k, dw))

    for i in reversed(range(depth)):
        j = i // 2
        mix_saved, ffn_saved = saved[i]
        s = ffn_saved
        (dg, du, dwd, dgain), outs = _ffn_bwd_hidden(
            f"ffn_bwd_hidden{i}", dcur, s["f"], gain("ffn_post_g", i), s["a"], s["dup"], s["dgate"], s["wd"],
            exchanges=exchanges_due())
        exchanged(outs)
        grads["ffn_post_g"][i] = dgain
        made("ffn_w_down", i, dwd.reshape(N_CHIPS, -1, d))
        (dcur, dwg, dwu, dgain), outs = _bwd_in(
            f"ffn_bwd_in{i}", dcur, s["x"], gain("ffn_pre_g", i), [dg, du], [s["wg"], s["wu"]], transposed=True,
            exchanges=exchanges_due())
        exchanged(outs)
        grads["ffn_pre_g"][i] = dgain
        made("ffn_w_gate", i, dwg.reshape(N_CHIPS, -1, d))
        made("ffn_w_up", i, dwu.reshape(N_CHIPS, -1, d))
        s = mix_saved
        if i % 2 == 0:
            (dz, dwout, dws, dbacc, dlng, dlnb, dgain), outs = _mix_a_bwd_hidden(
                f"mix_a_bwd_hidden{j}", dcur, s["m"], gain("mix_post_g", i), s["act"], s["slope"], s["ln_g"], s["ln_b"],
                a_w_s[j], s["b_t"], s["w_out"], exchanges=exchanges_due())
            exchanged(outs)
            grads["a_w_s"][j] = dws
            grads["a_b_s"][j] = jnp.transpose(dbacc)
            grads["a_ln_g"][j] = dlng.reshape(-1)
            grads["a_ln_b"][j] = dlnb.reshape(-1)
            grads["mix_post_g"][i] = dgain
            made("a_w_out", j, dwout.reshape(N_CHIPS, -1, d))
            (dcur, dwin, dgain), outs = _bwd_in(
                f"mix_a_bwd_in{j}", dcur, s["x"], gain("mix_pre_g", i), [dz], [s["w_in"]], exchanges=exchanges_due())
            exchanged(outs)
            grads["mix_pre_g"][i] = dgain
            made("a_w_in", j, dwin)
        else:
            (dq, dwout, dwgrp, dscale, dgain), outs = _mix_b_bwd_hidden(
                f"mix_b_bwd_hidden{j}", dcur, s["m"], gain("mix_post_g", i), s["p"], s["w_grp"], s["scale"],
                s["w_out"], exchanges=exchanges_due())
            exchanged(outs)
            grads["b_scale"][j] = dscale
            grads["mix_post_g"][i] = dgain
            made("b_w_out", j, dwout.reshape(N_CHIPS, -1, d))
            made("b_w_grp", j, jnp.transpose(
                dwgrp.reshape(len(B_WINDOWS), N_CHIPS, gd_b // N_CHIPS, gd_b), (1, 0, 2, 3)).reshape(N_CHIPS, -1, gd_b))
            (dp,), _ = _pool_bwd(f"pool_bwd{j}", dq)
            (dcur, dwin, dgain), outs = _bwd_in(
                f"mix_b_bwd_in{j}", dcur, s["x"], gain("mix_pre_g", i), [dp], [s["w_in"].reshape(1, d, d)],
                exchanges=exchanges_due())
            exchanged(outs)
            grads["mix_pre_g"][i] = dgain
            made("b_w_in", j, dwin.reshape(N_CHIPS, -1, d))
    grad_x = dcur.reshape(x.shape)
    exchanged(_exchange("grads_last", exchanges_due()))
    exchanged(_exchange("grads_last_to_chips", exchanges_due()))

    reduced_names = [name for name in SHARDED if name != "b_scale"]
    sums = []
    for name in reduced_names:
        layers = range(weights[name].shape[0])
        sums.append(_chip_sum("chip_sum_" + name, place, [pair[name, k] for k in layers],
                              [from_chips[name, k] for k in layers]))
    reduced = dict(zip(reduced_names, _exchange("swap_halves", [_SwapHalves(sums)])[0]))

    small_grads = [jnp.stack([g.reshape(weights[name].shape[1:]) for g in grads[name]], axis=0) for name in SMALL]
    scale_grad = jnp.concatenate(grads["b_scale"], axis=0)
    summed = _all_sum_small(_pack_small(small_grads + [scale_grad]))
    small_rows = summed.shape[0] - scale_grad.size // 128
    scale_sum = summed[small_rows:].reshape(scale_grad.shape)
    reduced["b_scale"] = lax.dynamic_slice_in_dim(scale_sum, chip * b_scale.shape[1], b_scale.shape[1], axis=1)[:, None, :]
    summed = summed[:small_rows]

    out_g, out_d, out_m, out_v = {}, {}, {}, {}
    for name in SHARDED:
        shape = weights[name].shape
        dlt, nm, nv = _adamw("adamw_" + name, _as_layers(name, weights[name]), reduced[name],
                             _as_layers(name, mom1[name]), _as_layers(name, mom2[name]))
        out_g[name] = _from_layers(name, reduced[name], shape)
        out_d[name], out_m[name], out_v[name] = (_from_layers(name, a, shape) for a in (dlt, nm, nv))
    small_like = [weights[name] for name in SMALL]
    packs = [_pack_small([src[name] for name in SMALL]).reshape(1, -1, 128) for src in (weights, mom1, mom2)]
    dlt, nm, nv = _adamw("adamw_small", packs[0], summed.reshape(1, -1, 128), packs[1], packs[2])
    for dst, packed in ((out_g, summed), (out_d, dlt[0]), (out_m, nm[0]), (out_v, nv[0])):
        for name, val in zip(SMALL, _unpack_small(packed, small_like)):
            dst[name] = val

    return (loss, grad_x, *[out_g[n] for n in WEIGHTS], *[out_d[n] for n in WEIGHTS],
            *[out_m[n] for n in WEIGHTS], *[out_v[n] for n in WEIGHTS])
```

```python
import functools
import math

import jax
import jax.numpy as jnp
from jax import lax
from jax.experimental import pallas as pl
from jax.experimental.pallas import tpu as pltpu

F32 = jnp.float32
BF16 = jnp.bfloat16
MESH = pl.DeviceIdType.MESH

EPS = 1e-6
CHUNK = 128
A_GROUPS = 8
B_WINDOWS = (2, 4, 8, 16)
HALO = 16
N_CHIPS = 4
N_DEV = 8

ADAM_LR = 0.001
ADAM_B1 = 0.9
ADAM_B2 = 0.999
ADAM_EPS = 1e-08
ADAM_WD = 0.01
ADAM_STEP = 10

VMEM_LIMIT_BYTES = 60 * 1024 * 1024
INV_SQRT2 = 1.0 / math.sqrt(2.0)
INV_SQRT_2PI = 1.0 / math.sqrt(2.0 * math.pi)

_ANY = pl.BlockSpec(memory_space=pl.ANY)
_VM = pl.BlockSpec(memory_space=pltpu.VMEM)


def _params():
    return pltpu.CompilerParams(dimension_semantics=("arbitrary",), vmem_limit_bytes=VMEM_LIMIT_BYTES)


def _token_block(t):
    return 256 if t >= 1024 else 128


def _step_rows(t):
    return 2 * _token_block(t)


def _rows(tm, d):
    return pl.BlockSpec((tm, d), lambda i: (i, 0))


def _blocks(nb, tm, bw):
    return pl.BlockSpec((nb, tm, bw), lambda i: (0, i, 0))


def _dot(a, b):
    return lax.dot_general(a, b, (((1,), (0,)), ((), ())), preferred_element_type=F32)


def _dot_nt(a, b):
    return lax.dot_general(a, b, (((1,), (1,)), ((), ())), preferred_element_type=F32)


def _dot_tn(a, b):
    return lax.dot_general(a, b, (((0,), (0,)), ((), ())), preferred_element_type=F32)


def _rms(x, g):
    return x * lax.rsqrt(jnp.mean(x * x, axis=-1, keepdims=True) + EPS) * g


def _rms_bwd(dy, x, g):
    r = lax.rsqrt(jnp.mean(x * x, axis=-1, keepdims=True) + EPS)
    n = x * r
    dn = dy * g
    dx = r * (dn - n * jnp.mean(dn * n, axis=-1, keepdims=True))
    return dx, jnp.sum(dy * n, axis=0, keepdims=True)


def _gelu_and_grad(x):
    cdf = 0.5 * (1.0 + lax.erf(x * INV_SQRT2))
    return x * cdf, cdf + x * (jnp.exp(-0.5 * x * x) * INV_SQRT_2PI)


def _position():
    return lax.axis_index("x"), lax.axis_index("y"), lax.axis_index("c")


def _other_chips(x, y):
    return [(1 - x, y), (x, 1 - y), (1 - x, 1 - y)]


class _GatherWeights:
    def __init__(self, bufs, whole=()):
        self.inputs = list(bufs)
        self.out_shapes = [jax.ShapeDtypeStruct(b.shape, b.dtype) for b in bufs]
        self.aliases = {w: w for w in range(len(bufs))}
        self.n_sems = 6 * len(bufs)
        self.whole = frozenset(whole)

    def _part(self, outs, w, slot, core):
        if w in self.whole:
            return outs[w].at[slot]
        half = outs[w].shape[1] // 2
        return outs[w].at[slot, pl.ds(core * half, half)]

    def _ici(self, outs, send, recv, w, j, slot):
        x, y, c = _position()
        px, py = _other_chips(x, y)[j]
        part = self._part(outs, w, slot, c)
        return pltpu.make_async_remote_copy(
            src_ref=part, dst_ref=part, send_sem=send.at[6 * w + j], recv_sem=recv.at[6 * w + j],
            device_id=(px, py, c), device_id_type=MESH)

    def _d2d(self, outs, send, recv, w, j, slot, core):
        x, y, c = _position()
        part = self._part(outs, w, slot, core)
        return pltpu.make_async_remote_copy(
            src_ref=part, dst_ref=part, send_sem=send.at[6 * w + 3 + j], recv_sem=recv.at[6 * w + 3 + j],
            device_id=(x, y, 1 - c), device_id_type=MESH)

    def start(self, ins, outs, send, recv):
        x, y, _ = _position()
        for w in range(len(outs)):
            for j in range(3):
                self._ici(outs, send, recv, w, j, 2 * x + y).start()

    def finish(self, ins, outs, send, recv):
        x, y, c = _position()
        slots = [2 * px + py for px, py in _other_chips(x, y)]
        for w in range(len(outs)):
            for j, slot in enumerate(slots):
                self._ici(outs, send, recv, w, j, slot).wait_recv()
                if w not in self.whole:
                    self._d2d(outs, send, recv, w, j, slot, c).start()
        for w in range(len(outs)):
            for j, slot in enumerate(slots):
                if w not in self.whole:
                    self._d2d(outs, send, recv, w, j, slot, 1 - c).wait_recv()
        for w in range(len(outs)):
            for j, slot in enumerate(slots):
                self._ici(outs, send, recv, w, j, 2 * x + y).wait_send()
                if w not in self.whole:
                    self._d2d(outs, send, recv, w, j, slot, c).wait_send()


class _ToSibling:
    def __init__(self, grads):
        self.inputs = list(grads)
        self.out_shapes = [jax.ShapeDtypeStruct((g.shape[0], g.shape[1] // 2, g.shape[2]), g.dtype) for g in grads]
        self.aliases = {}
        self.n_sems = len(grads)

    def _copy(self, ins, outs, send, recv, w):
        x, y, c = _position()
        half = ins[w].shape[1] // 2
        return pltpu.make_async_remote_copy(
            src_ref=ins[w].at[:, pl.ds((1 - c) * half, half)], dst_ref=outs[w],
            send_sem=send.at[w], recv_sem=recv.at[w], device_id=(x, y, 1 - c), device_id_type=MESH)

    def start(self, ins, outs, send, recv):
        for w in range(len(ins)):
            self._copy(ins, outs, send, recv, w).start()

    def finish(self, ins, outs, send, recv):
        for w in range(len(ins)):
            self._copy(ins, outs, send, recv, w).wait_recv()
        for w in range(len(ins)):
            self._copy(ins, outs, send, recv, w).wait_send()


class _ToChips:
    def __init__(self, parts):
        self.inputs = list(parts)
        self.out_shapes = [jax.ShapeDtypeStruct(p.shape, p.dtype) for p in parts]
        self.aliases = {}
        self.n_sems = 3 * len(parts)

    def _copy(self, ins, outs, send, recv, w, j, outbound):
        x, y, c = _position()
        px, py = _other_chips(x, y)[j]
        me, peer = 2 * x + y, 2 * px + py
        src_slot, dst_slot = (peer, me) if outbound else (me, peer)
        return pltpu.make_async_remote_copy(
            src_ref=ins[w].at[src_slot], dst_ref=outs[w].at[dst_slot],
            send_sem=send.at[3 * w + j], recv_sem=recv.at[3 * w + j], device_id=(px, py, c), device_id_type=MESH)

    def start(self, ins, outs, send, recv):
        for w in range(len(ins)):
            for j in range(3):
                self._copy(ins, outs, send, recv, w, j, True).start()

    def finish(self, ins, outs, send, recv):
        for w in range(len(ins)):
            for j in range(3):
                self._copy(ins, outs, send, recv, w, j, False).wait_recv()
        for w in range(len(ins)):
            for j in range(3):
                self._copy(ins, outs, send, recv, w, j, True).wait_send()


class _SwapHalves:
    def __init__(self, bufs):
        self.inputs = list(bufs)
        self.out_shapes = [jax.ShapeDtypeStruct(b.shape, b.dtype) for b in bufs]
        self.aliases = {w: w for w in range(len(bufs))}
        self.n_sems = len(bufs)

    def _copy(self, outs, send, recv, w, core):
        x, y, c = _position()
        half = outs[w].shape[1] // 2
        rows = outs[w].at[:, pl.ds(core * half, half)]
        return pltpu.make_async_remote_copy(
            src_ref=rows, dst_ref=rows, send_sem=send.at[w], recv_sem=recv.at[w],
            device_id=(x, y, 1 - c), device_id_type=MESH)

    def start(self, ins, outs, send, recv):
        c = lax.axis_index("c")
        for w in range(len(outs)):
            self._copy(outs, send, recv, w, c).start()

    def finish(self, ins, outs, send, recv):
        c = lax.axis_index("c")
        for w in range(len(outs)):
            self._copy(outs, send, recv, w, 1 - c).wait_recv()
        for w in range(len(outs)):
            self._copy(outs, send, recv, w, c).wait_send()


def _call(name, body, *, grid, in_specs, out_specs, out_shape, args, scratch_shapes=(), exchanges=()):
    given = list(exchanges)
    exchanges = [e for e in given if e.inputs]
    n_in, n_out, n_scr = len(args), len(out_shape), len(scratch_shapes)
    ex_in = [a for e in exchanges for a in e.inputs]
    ex_out = [s for e in exchanges for s in e.out_shapes]
    aliases = {}
    at_in, at_out = n_in, n_out
    for e in exchanges:
        for i, o in e.aliases.items():
            aliases[at_in + i] = at_out + o
        at_in += len(e.inputs)
        at_out += len(e.out_shapes)
    n_steps = grid[0]

    def fused(*refs):
        body_in, refs = refs[:n_in], refs[n_in:]
        ex_in_refs, refs = refs[: len(ex_in)], refs[len(ex_in) :]
        body_out, refs = refs[:n_out], refs[n_out:]
        ex_out_refs, refs = refs[: len(ex_out)], refs[len(ex_out) :]
        body_scr, sems = refs[:n_scr], refs[n_scr:]

        def each(stage):
            a = b = 0
            for n, e in enumerate(exchanges):
                ins, outs = ex_in_refs[a : a + len(e.inputs)], ex_out_refs[b : b + len(e.out_shapes)]
                getattr(e, stage)(ins, outs, sems[2 * n], sems[2 * n + 1])
                a += len(e.inputs)
                b += len(e.out_shapes)

        if exchanges:
            @pl.when(pl.program_id(0) == 0)
            def _():
                each("start")

        if body is not None:
            body(*body_in, *body_out, *body_scr)

        if exchanges:
            @pl.when(pl.program_id(0) == n_steps - 1)
            def _():
                each("finish")

    outs = pl.pallas_call(
        fused,
        name=name,
        grid=grid,
        in_specs=list(in_specs) + [_ANY] * len(ex_in),
        out_specs=list(out_specs) + [_ANY] * len(ex_out),
        out_shape=list(out_shape) + ex_out,
        input_output_aliases=aliases,
        scratch_shapes=list(scratch_shapes)
        + [pltpu.SemaphoreType.DMA((e.n_sems,)) for e in exchanges for _ in range(2)],
        compiler_params=_params(),
    )(*args, *ex_in)
    body_outs, rest = list(outs[:n_out]), list(outs[n_out:])
    ex_outs = []
    for e in given:
        n_e = len(e.out_shapes) if e.inputs else 0
        ex_outs.append(rest[:n_e])
        rest = rest[n_e:]
    return body_outs, ex_outs


def _exchange(name, exchanges):
    return _call(name, None, grid=(1,), in_specs=[], out_specs=[], out_shape=[], args=[], exchanges=exchanges)[1]


def _ffn_fwd(name, x, gpre, gpost, wg, wu, wd, target=None, exchanges=()):
    t, d = x.shape
    nb, fs, _ = wg.shape
    sub = _token_block(t)
    tm = _step_rows(t)
    with_loss = target is not None

    def body(x_ref, gpre_ref, gpost_ref, wg_ref, wu_ref, wd_ref, *refs):
        if with_loss:
            t_ref, o_ref, a_ref, dup_ref, dgate_ref, f_ref, sq_ref = refs

            @pl.when(pl.program_id(0) == 0)
            def _():
                sq_ref[...] = jnp.zeros_like(sq_ref)
        else:
            o_ref, a_ref, dup_ref, dgate_ref, f_ref = refs
        for h in range(tm // sub):
            rows = slice(h * sub, (h + 1) * sub)
            xv = x_ref[rows, :]
            hb = _rms(xv, gpre_ref[...]).astype(BF16)
            f = jnp.zeros((sub, d), F32)
            for k in range(nb):
                g = _dot_nt(hb, wg_ref[k])
                u = _dot_nt(hb, wu_ref[k])
                s = jax.nn.sigmoid(g)
                sg = g * s
                a = (sg * u).astype(BF16)
                a_ref[k, rows, :] = a
                dup_ref[k, rows, :] = sg.astype(BF16)
                dgate_ref[k, rows, :] = (u * (s * (1.0 + g * (1.0 - s)))).astype(BF16)
                f = f + _dot(a, wd_ref[k])
            f_ref[rows, :] = f
            y = xv + _rms(f, gpost_ref[...])
            if with_loss:
                err = y - t_ref[rows, :]
                o_ref[rows, :] = err / d
                sq_ref[...] += jnp.sum(err * err)
            else:
                o_ref[rows, :] = y

    return _call(
        name,
        body,
        grid=(t // tm,),
        in_specs=[_rows(tm, d), _VM, _VM, _VM, _VM, _VM] + [_rows(tm, d)] * with_loss,
        out_specs=[_rows(tm, d)] + [_blocks(nb, tm, fs)] * 3 + [_rows(tm, d)] + [_VM] * with_loss,
        out_shape=[jax.ShapeDtypeStruct((t, d), F32)]
        + [jax.ShapeDtypeStruct((nb, t, fs), BF16)] * 3
        + [jax.ShapeDtypeStruct((t, d), F32)]
        + [jax.ShapeDtypeStruct((8, 128), F32)] * with_loss,
        args=[x, gpre, gpost, wg, wu, wd] + [target] * with_loss,
        exchanges=exchanges,
    )


def _ffn_bwd_hidden(name, dy, f, gpost, a, dup, dgate, wd, exchanges=()):
    t, d = dy.shape
    nb, fs, _ = wd.shape
    tm = _step_rows(t)

    def body(dy_ref, f_ref, gpost_ref, a_ref, dup_ref, dgate_ref, wd_ref, dg_ref, du_ref, dwd_ref, dgain_ref):
        @pl.when(pl.program_id(0) == 0)
        def _():
            dwd_ref[...] = jnp.zeros_like(dwd_ref)
            dgain_ref[...] = jnp.zeros_like(dgain_ref)

        df, dgain = _rms_bwd(dy_ref[...], f_ref[...], gpost_ref[...])
        dgain_ref[...] += dgain
        dfb = df.astype(BF16)
        for k in range(nb):
            da = _dot_nt(dfb, wd_ref[k])
            dwd_ref[k] += _dot_tn(a_ref[k], dfb)
            du_ref[k] = (da * dup_ref[k].astype(F32)).astype(BF16)
            dg_ref[k] = (da * dgate_ref[k].astype(F32)).astype(BF16)

    return _call(
        name,
        body,
        grid=(t // tm,),
        in_specs=[_rows(tm, d), _rows(tm, d), _VM] + [_blocks(nb, tm, fs)] * 3 + [_VM],
        out_specs=[_blocks(nb, tm, fs), _blocks(nb, tm, fs), _VM, _VM],
        out_shape=[
            jax.ShapeDtypeStruct((nb, t, fs), BF16),
            jax.ShapeDtypeStruct((nb, t, fs), BF16),
            jax.ShapeDtypeStruct((nb, fs, d), F32),
            jax.ShapeDtypeStruct((1, d), F32),
        ],
        args=[dy, f, gpost, a, dup, dgate, wd],
        exchanges=exchanges,
    )


def _bwd_in(name, dres, x, gpre, dzs, ws, transposed=False, exchanges=()):
    t, d = x.shape
    n = len(ws)
    resident = sum(6 * w.size for w in ws)
    tm = _step_rows(t) if resident <= VMEM_LIMIT_BYTES // 2 else _token_block(t)
    widths = [w.shape[1] if transposed else w.shape[2] for w in ws]

    def body(*refs):
        dres_ref, x_ref, gpre_ref = refs[:3]
        dz_refs = refs[3 : 3 + n]
        w_refs = refs[3 + n : 3 + 2 * n]
        dx_ref = refs[3 + 2 * n]
        dw_refs = refs[4 + 2 * n : 4 + 3 * n]
        dgain_ref = refs[4 + 3 * n]

        @pl.when(pl.program_id(0) == 0)
        def _():
            for dw_ref in dw_refs:
                dw_ref[...] = jnp.zeros_like(dw_ref)
            dgain_ref[...] = jnp.zeros_like(dgain_ref)

        xv = x_ref[...]
        gain = gpre_ref[...]
        hb = _rms(xv, gain).astype(BF16)
        dh = jnp.zeros((tm, d), F32)
        for dz_ref, w_ref, dw_ref in zip(dz_refs, w_refs, dw_refs):
            for k in range(w_ref.shape[0]):
                dz = dz_ref[k]
                if transposed:
                    dh = dh + _dot(dz, w_ref[k])
                    dw_ref[k] += _dot_tn(dz, hb)
                else:
                    dh = dh + _dot_nt(dz, w_ref[k])
                    dw_ref[k] += _dot_tn(hb, dz)
        dx, dgain = _rms_bwd(dh, xv, gain)
        dx_ref[...] = dres_ref[...] + dx
        dgain_ref[...] += dgain

    return _call(
        name,
        body,
        grid=(t // tm,),
        in_specs=[_rows(tm, d), _rows(tm, d), _VM]
        + [_blocks(w.shape[0], tm, bw) for w, bw in zip(ws, widths)]
        + [_VM] * n,
        out_specs=[_rows(tm, d)] + [_VM] * n + [_VM],
        out_shape=[jax.ShapeDtypeStruct((t, d), F32)]
        + [jax.ShapeDtypeStruct(w.shape, F32) for w in ws]
        + [jax.ShapeDtypeStruct((1, d), F32)],
        args=[dres, x, gpre, *dzs, *ws],
        exchanges=exchanges,
    )


def _causal_weights(ws_ref):
    row = lax.broadcasted_iota(jnp.int32, (CHUNK, CHUNK), 0)
    col = lax.broadcasted_iota(jnp.int32, (CHUNK, CHUNK), 1)
    return [jnp.where(row >= col, ws_ref[g], 0.0).astype(BF16) for g in range(A_GROUPS)]


def _layernorm_halves(v0, v1):
    width = v0.shape[-1] + v1.shape[-1]
    mu = (jnp.sum(v0, axis=-1, keepdims=True) + jnp.sum(v1, axis=-1, keepdims=True)) / width
    c0 = v0 - mu
    c1 = v1 - mu
    var = (jnp.sum(c0 * c0, axis=-1, keepdims=True) + jnp.sum(c1 * c1, axis=-1, keepdims=True)) / width
    rstd = lax.rsqrt(var + EPS)
    return c0 * rstd, c1 * rstd, rstd


def _spatial_gate(sv_ref, wtril, vl, bt_ref, half, tm, gd):
    for gg in range(A_GROUPS // 2):
        g = half * (A_GROUPS // 2) + gg
        bias = bt_ref[:, g : g + 1]
        for n in range(tm // CHUNK):
            blk = vl[n * CHUNK : (n + 1) * CHUNK, gg * gd : (gg + 1) * gd]
            sv_ref[n * CHUNK : (n + 1) * CHUNK, gg * gd : (gg + 1) * gd] = _dot(wtril[g], blk) + bias


def _mix_a_fwd(name, x, gpre, gpost, w_in, ln_g, ln_b, w_s, b_t, w_out, exchanges=()):
    t, d = x.shape
    _, _, q = w_in.shape
    gd = 2 * q // A_GROUPS
    tm = _step_rows(t)

    def body(x_ref, gpre_ref, gpost_ref, win_ref, lng_ref, lnb_ref, ws_ref, bt_ref, wout_ref,
             o_ref, z_ref, dz_ref, m_ref, sv_ref):
        xv = x_ref[...]
        hb = _rms(xv, gpre_ref[...]).astype(BF16)
        z = []
        for k in range(4):
            act, slope = _gelu_and_grad(_dot(hb, win_ref[k]))
            z_ref[k] = act.astype(BF16)
            dz_ref[k] = slope.astype(BF16)
            z.append(act)
        vh0, vh1, _ = _layernorm_halves(z[2], z[3])
        vls = [(vh * lng_ref[b : b + 1, :] + lnb_ref[b : b + 1, :]).astype(BF16) for b, vh in enumerate((vh0, vh1))]
        wtril = _causal_weights(ws_ref)
        m = jnp.zeros((tm, d), F32)
        for b in range(2):
            _spatial_gate(sv_ref, wtril, vls[b], bt_ref, b, tm, gd)
            gated = (z[b] * sv_ref[...]).astype(BF16)
            m = m + _dot(gated, wout_ref[b])
        m_ref[...] = m
        o_ref[...] = xv + _rms(m, gpost_ref[...])

    return _call(
        name,
        body,
        grid=(t // tm,),
        in_specs=[_rows(tm, d)] + [_VM] * 8,
        out_specs=[_rows(tm, d), _blocks(4, tm, q), _blocks(4, tm, q), _rows(tm, d)],
        out_shape=[
            jax.ShapeDtypeStruct((t, d), F32),
            jax.ShapeDtypeStruct((4, t, q), BF16),
            jax.ShapeDtypeStruct((4, t, q), BF16),
            jax.ShapeDtypeStruct((t, d), F32),
        ],
        scratch_shapes=[pltpu.VMEM((tm, q), F32)],
        args=[x, gpre, gpost, w_in, ln_g, ln_b, w_s, b_t, w_out],
        exchanges=exchanges,
    )


def _mix_a_bwd_hidden(name, dy, m, gpost, act, slope, ln_g, ln_b, w_s, b_t, w_out, exchanges=()):
    t, d = dy.shape
    _, _, q = act.shape
    gd = 2 * q // A_GROUPS
    tm = _step_rows(t)
    n_chunks = tm // CHUNK

    def body(dy_ref, m_ref, gpost_ref, z_ref, slope_ref, lng_ref, lnb_ref, ws_ref, bt_ref, wout_ref,
             dz_ref, dwout_ref, dws_ref, dbacc_ref, dlng_ref, dlnb_ref, dgain_ref, sv_ref, dvl_ref):
        first = pl.program_id(0) == 0

        @pl.when(first)
        def _():
            for ref in (dwout_ref, dws_ref, dbacc_ref, dlng_ref, dlnb_ref, dgain_ref):
                ref[...] = jnp.zeros_like(ref)

        dm, dgain = _rms_bwd(dy_ref[...], m_ref[...], gpost_ref[...])
        dgain_ref[...] += dgain
        dmb = dm.astype(BF16)
        vhs = list(_layernorm_halves(z_ref[2].astype(F32), z_ref[3].astype(F32)))
        rstd = vhs.pop()
        vls = [(vh * lng_ref[b : b + 1, :] + lnb_ref[b : b + 1, :]).astype(BF16) for b, vh in enumerate(vhs)]
        wtril = _causal_weights(ws_ref)
        dvhs = []
        for b in range(2):
            u = z_ref[b].astype(F32)
            _spatial_gate(sv_ref, wtril, vls[b], bt_ref, b, tm, gd)
            sv = sv_ref[...]
            gated = (u * sv).astype(BF16)
            dgated = _dot_nt(dmb, wout_ref[b])
            dwout_ref[b] += _dot_tn(gated, dmb)
            dz_ref[b] = (dgated * sv * slope_ref[b].astype(F32)).astype(BF16)
            dsv = dgated * u
            folded = dsv[0:CHUNK, :]
            for c in range(1, n_chunks):
                folded = folded + dsv[c * CHUNK : (c + 1) * CHUNK, :]
            for gg in range(A_GROUPS // 2):
                g = b * (A_GROUPS // 2) + gg
                dbacc_ref[:, g : g + 1] += jnp.sum(folded[:, gg * gd : (gg + 1) * gd], axis=1, keepdims=True)
            dsvb = dsv.astype(BF16)
            for gg in range(A_GROUPS // 2):
                g = b * (A_GROUPS // 2) + gg
                for c in range(n_chunks):
                    rows = slice(c * CHUNK, (c + 1) * CHUNK)
                    cols = slice(gg * gd, (gg + 1) * gd)
                    blk = dsvb[rows, cols]
                    dvl_ref[rows, cols] = _dot_tn(wtril[g], blk)
                    dws_ref[g] += _dot_nt(blk, vls[b][rows, cols])
            dvl = dvl_ref[...]
            dlng_ref[b : b + 1, :] += jnp.sum(dvl * vhs[b], axis=0, keepdims=True)
            dlnb_ref[b : b + 1, :] += jnp.sum(dvl, axis=0, keepdims=True)
            dvhs.append(dvl * lng_ref[b : b + 1, :])
        width = 2.0 * q
        m1 = (jnp.sum(dvhs[0], axis=-1, keepdims=True) + jnp.sum(dvhs[1], axis=-1, keepdims=True)) / width
        m2 = (jnp.sum(dvhs[0] * vhs[0], axis=-1, keepdims=True)
              + jnp.sum(dvhs[1] * vhs[1], axis=-1, keepdims=True)) / width
        for b in range(2):
            dv = rstd * (dvhs[b] - m1 - vhs[b] * m2)
            dz_ref[2 + b] = (dv * slope_ref[2 + b].astype(F32)).astype(BF16)

        @pl.when(pl.program_id(0) == t // tm - 1)
        def _():
            row = lax.broadcasted_iota(jnp.int32, (CHUNK, CHUNK), 0)
            col = lax.broadcasted_iota(jnp.int32, (CHUNK, CHUNK), 1)
            for g in range(A_GROUPS):
                dws_ref[g] = jnp.where(row >= col, dws_ref[g], 0.0)

    return _call(
        name,
        body,
        grid=(t // tm,),
        in_specs=[_rows(tm, d), _rows(tm, d), _VM, _blocks(4, tm, q), _blocks(4, tm, q)] + [_VM] * 5,
        out_specs=[_blocks(4, tm, q)] + [_VM] * 6,
        out_shape=[
            jax.ShapeDtypeStruct((4, t, q), BF16),
            jax.ShapeDtypeStruct((2, q, d), F32),
            jax.ShapeDtypeStruct((A_GROUPS, CHUNK, CHUNK), F32),
            jax.ShapeDtypeStruct((CHUNK, A_GROUPS), F32),
            jax.ShapeDtypeStruct((2, q), F32),
            jax.ShapeDtypeStruct((2, q), F32),
            jax.ShapeDtypeStruct((1, d), F32),
        ],
        scratch_shapes=[pltpu.VMEM((tm, q), F32), pltpu.VMEM((tm, q), F32)],
        args=[dy, m, gpost, act, slope, ln_g, ln_b, w_s, b_t, w_out],
        exchanges=exchanges,
    )


def _norm_matmul(name, x, gpre, w, exchanges=()):
    t, d = x.shape
    n = w.shape[1]
    tm = _step_rows(t)

    def body(x_ref, gpre_ref, w_ref, o_ref):
        o_ref[...] = _dot(_rms(x_ref[...], gpre_ref[...]).astype(BF16), w_ref[...])

    return _call(
        name,
        body,
        grid=(t // tm,),
        in_specs=[_rows(tm, d), _VM, _VM],
        out_specs=[_rows(tm, n)],
        out_shape=[jax.ShapeDtypeStruct((t, n), F32)],
        args=[x, gpre, w],
        exchanges=exchanges,
    )


def _window_counts(tm, win):
    pos = pl.program_id(0) * tm + lax.broadcasted_iota(jnp.int32, (tm, 1), 0)
    return jnp.minimum(pos + 1, win).astype(F32)


def _pooled(p, halo, tm, gd):
    prev = jnp.where(pl.program_id(0) == 0, 0.0, halo)
    ext = jnp.concatenate([prev, p], axis=0)
    out = []
    for g, win in enumerate(B_WINDOWS):
        s = ext[:, g * gd : (g + 1) * gd]
        step = 1
        while step < win:
            s = s + pltpu.roll(s, step, 0)
            step *= 2
        total = s[HALO:, :]
        out.append(total / _window_counts(tm, win) - p[:, g * gd : (g + 1) * gd])
    return out


def _halo_spec(t, tm, d, ahead):
    per = tm // HALO
    if ahead:
        return pl.BlockSpec((HALO, d), lambda i: (jnp.minimum((i + 1) * per, t // HALO - 1), 0))
    return pl.BlockSpec((HALO, d), lambda i: (jnp.maximum(i * per - 1, 0), 0))


def _mix_b_fwd(name, x, p, gpost, w_grp, scale, w_out, exchanges=()):
    t, d = x.shape
    gd = d // len(B_WINDOWS)
    tm = _step_rows(t)

    def body(x_ref, p_ref, halo_ref, gpost_ref, wgrp_ref, scale_ref, wout_ref, o_ref, m_ref):
        pooled = _pooled(p_ref[...], halo_ref[...], tm, gd)
        mixed = jnp.concatenate([_dot(pg.astype(BF16), wgrp_ref[g]) for g, pg in enumerate(pooled)], axis=1)
        m = _dot((mixed * scale_ref[...]).astype(BF16), wout_ref[...])
        m_ref[...] = m
        o_ref[...] = x_ref[...] + _rms(m, gpost_ref[...])

    return _call(
        name,
        body,
        grid=(t // tm,),
        in_specs=[_rows(tm, d), _rows(tm, d), _halo_spec(t, tm, d, False), _VM, _VM, _VM, _VM],
        out_specs=[_rows(tm, d), _rows(tm, d)],
        out_shape=[jax.ShapeDtypeStruct((t, d), F32), jax.ShapeDtypeStruct((t, d), F32)],
        args=[x, p, p, gpost, w_grp, scale, w_out],
        exchanges=exchanges,
    )


def _mix_b_bwd_hidden(name, dy, m, gpost, p, w_grp, scale, w_out, exchanges=()):
    t, d = dy.shape
    gd = d // len(B_WINDOWS)
    tm = _step_rows(t)

    def body(dy_ref, m_ref, gpost_ref, p_ref, halo_ref, wgrp_ref, scale_ref, wout_ref,
             dq_ref, dwout_ref, dwgrp_ref, dscale_ref, dgain_ref):
        @pl.when(pl.program_id(0) == 0)
        def _():
            for ref in (dwout_ref, dwgrp_ref, dscale_ref, dgain_ref):
                ref[...] = jnp.zeros_like(ref)

        dm, dgain = _rms_bwd(dy_ref[...], m_ref[...], gpost_ref[...])
        dgain_ref[...] += dgain
        dmb = dm.astype(BF16)
        pooled = [pg.astype(BF16) for pg in _pooled(p_ref[...], halo_ref[...], tm, gd)]
        mixed = jnp.concatenate([_dot(pg, wgrp_ref[g]) for g, pg in enumerate(pooled)], axis=1)
        scale = scale_ref[...]
        ms = (mixed * scale).astype(BF16)
        dms = _dot_nt(dmb, wout_ref[...])
        dwout_ref[...] += _dot_tn(ms, dmb)
        dscale_ref[...] += jnp.sum(dms * mixed, axis=0, keepdims=True)
        dmixed = (dms * scale).astype(BF16)
        for g, win in enumerate(B_WINDOWS):
            cols = slice(g * gd, (g + 1) * gd)
            dmg = dmixed[:, cols]
            dwgrp_ref[g] += _dot_tn(pooled[g], dmg)
            dq_ref[:, cols] = _dot_nt(dmg, wgrp_ref[g]) / _window_counts(tm, win)

    return _call(
        name,
        body,
        grid=(t // tm,),
        in_specs=[_rows(tm, d), _rows(tm, d), _VM, _rows(tm, d), _halo_spec(t, tm, d, False), _VM, _VM, _VM],
        out_specs=[_rows(tm, d), _VM, _VM, _VM, _VM],
        out_shape=[
            jax.ShapeDtypeStruct((t, d), F32),
            jax.ShapeDtypeStruct((d, d), F32),
            jax.ShapeDtypeStruct((len(B_WINDOWS), gd, gd), F32),
            jax.ShapeDtypeStruct((1, d), F32),
            jax.ShapeDtypeStruct((1, d), F32),
        ],
        args=[dy, m, gpost, p, p, w_grp, scale, w_out],
        exchanges=exchanges,
    )


def _pool_bwd(name, dq, exchanges=()):
    t, d = dq.shape
    gd = d // len(B_WINDOWS)
    tm = _step_rows(t)
    n_steps = t // tm

    def body(dq_ref, halo_ref, dp_ref):
        dq_blk = dq_ref[...]
        nxt = jnp.where(pl.program_id(0) == n_steps - 1, 0.0, halo_ref[...])
        ext = jnp.concatenate([dq_blk, nxt], axis=0)
        for g, win in enumerate(B_WINDOWS):
            cols = slice(g * gd, (g + 1) * gd)
            s = ext[:, cols]
            step = 1
            while step < win:
                s = s + pltpu.roll(s, tm + HALO - step, 0)
                step *= 2
            dp_ref[0, :, cols] = (s[:tm, :] - dq_blk[:, cols] * _window_counts(tm, win)).astype(BF16)

    return _call(
        name,
        body,
        grid=(n_steps,),
        in_specs=[_rows(tm, d), _halo_spec(t, tm, d, True)],
        out_specs=[_blocks(1, tm, d)],
        out_shape=[jax.ShapeDtypeStruct((1, t, d), BF16)],
        args=[dq, dq],
        exchanges=exchanges,
    )


def _cast_into_slots(name, place, w, dtype):
    n_layers, r, c = w.shape

    def body(place_ref, w_ref, *o_refs):
        del place_ref
        for j, o_ref in enumerate(o_refs):
            @pl.when(pl.program_id(0) == j)
            def _():
                o_ref[...] = w_ref[...].astype(dtype)

    return pl.pallas_call(
        body,
        name=name,
        grid_spec=pltpu.PrefetchScalarGridSpec(
            num_scalar_prefetch=1,
            grid=(n_layers,),
            in_specs=[pl.BlockSpec((1, r, c), lambda i, place_ref: (i, 0, 0))],
            out_specs=[pl.BlockSpec((1, r, c), lambda i, place_ref: (place_ref[0], 0, 0))] * n_layers,
        ),
        out_shape=[jax.ShapeDtypeStruct((N_CHIPS, r, c), dtype)] * n_layers,
        compiler_params=_params(),
    )(place, w)


def _row_tile(r):
    return 256 if r % 256 == 0 else r


def _pair_sum(name, place, dw, recv):
    _, r, c = dw.shape
    half = r // 2
    tr = _row_tile(half)
    per = half // tr

    def body(place_ref, a_ref, b_ref, o_ref):
        del place_ref
        o_ref[...] = (a_ref[...] + b_ref[...]).astype(BF16)

    return pl.pallas_call(
        body,
        name=name,
        grid_spec=pltpu.PrefetchScalarGridSpec(
            num_scalar_prefetch=1,
            grid=(N_CHIPS, per),
            in_specs=[
                pl.BlockSpec((1, tr, c), lambda k, i, place_ref: (k, place_ref[1] * per + i, 0)),
                pl.BlockSpec((1, tr, c), lambda k, i, place_ref: (k, i, 0)),
            ],
            out_specs=pl.BlockSpec((1, tr, c), lambda k, i, place_ref: (k, i, 0)),
        ),
        out_shape=jax.ShapeDtypeStruct(recv.shape, BF16),
        compiler_params=pltpu.CompilerParams(
            dimension_semantics=("arbitrary",) * 2, vmem_limit_bytes=VMEM_LIMIT_BYTES),
    )(place, dw, recv)


def _chip_sum(name, place, mine, others):
    n_layers = len(mine)
    _, half, c = mine[0].shape
    tr = _row_tile(half)
    per = half // tr

    def body(place_ref, *refs):
        del place_ref
        o_ref = refs[-1]
        for j in range(n_layers):
            @pl.when(pl.program_id(0) == j)
            def _():
                parts = refs[4 * j : 4 * j + 4]
                acc = parts[0][...].astype(F32) + parts[1][...].astype(F32)
                acc = acc + parts[2][...].astype(F32)
                o_ref[...] = acc + parts[3][...].astype(F32)

    def part(j, flip):
        return pl.BlockSpec((1, tr, c), lambda l, i, place_ref: (
            jnp.bitwise_xor(place_ref[0], flip), jnp.where(l == j, i, 0), 0))

    args = []
    for j in range(n_layers):
        args += [mine[j], others[j], others[j], others[j]]
    return pl.pallas_call(
        body,
        name=name,
        grid_spec=pltpu.PrefetchScalarGridSpec(
            num_scalar_prefetch=1,
            grid=(n_layers, per),
            in_specs=[part(j, flip) for j in range(n_layers) for flip in range(N_CHIPS)],
            out_specs=pl.BlockSpec((1, tr, c), lambda l, i, place_ref: (l, place_ref[1] * per + i, 0)),
        ),
        out_shape=jax.ShapeDtypeStruct((n_layers, 2 * half, c), F32),
        compiler_params=pltpu.CompilerParams(
            dimension_semantics=("arbitrary",) * 2, vmem_limit_bytes=VMEM_LIMIT_BYTES),
    )(place, *args)


def _adamw(name, w, g, m, v):
    n_layers, r, c = w.shape
    tr = _row_tile(r)

    def body(w_ref, g_ref, m_ref, v_ref, d_ref, nm_ref, nv_ref):
        gv = g_ref[...]
        nm = ADAM_B1 * m_ref[...] + (1.0 - ADAM_B1) * gv
        nv = ADAM_B2 * v_ref[...] + (1.0 - ADAM_B2) * jnp.square(gv)
        m_hat = nm / (1.0 - ADAM_B1 ** ADAM_STEP)
        v_hat = nv / (1.0 - ADAM_B2 ** ADAM_STEP)
        d_ref[...] = -ADAM_LR * (m_hat / (jnp.sqrt(v_hat) + ADAM_EPS) + ADAM_WD * w_ref[...])
        nm_ref[...] = nm
        nv_ref[...] = nv

    spec = pl.BlockSpec((1, tr, c), lambda l, i: (l, i, 0))
    return pl.pallas_call(
        body,
        name=name,
        grid=(n_layers, r // tr),
        in_specs=[spec] * 4,
        out_specs=[spec] * 3,
        out_shape=[jax.ShapeDtypeStruct(w.shape, F32)] * 3,
        compiler_params=pltpu.CompilerParams(
            dimension_semantics=("arbitrary",) * 2, vmem_limit_bytes=VMEM_LIMIT_BYTES),
    )(w, g, m, v)


def _all_sum_small(packed):
    m_per, n = packed.shape

    def body(x_ref, sum_ref, all_ref, send_sems, recv_sems, local_sem):
        x, y, c = _position()
        me, sibling = (x, y, c), (x, y, 1 - c)
        chips = _other_chips(x, y)

        def rows(px, py, pc):
            return all_ref.at[pl.ds((4 * px + 2 * py + pc) * m_per, m_per), :]

        def copy(k, block, to, src=None):
            return pltpu.make_async_remote_copy(
                src_ref=rows(*block) if src is None else src, dst_ref=rows(*block),
                send_sem=send_sems.at[k], recv_sem=recv_sems.at[k], device_id=to, device_id_type=MESH)

        mine = pltpu.make_async_copy(x_ref, rows(*me), local_sem)
        mine.start()
        first = [copy(0, me, sibling, src=x_ref)]
        first += [copy(1 + j, me, (*chip, c), src=x_ref) for j, chip in enumerate(chips)]
        for cp in first:
            cp.start()
        passed = [copy(4 + j, (*chip, c), sibling) for j, chip in enumerate(chips)]
        for j, chip in enumerate(chips):
            copy(1 + j, (*chip, c), me).wait_recv()
            passed[j].start()
        copy(0, sibling, me).wait_recv()
        for j, chip in enumerate(chips):
            copy(4 + j, (*chip, 1 - c), me).wait_recv()
        for cp in first + passed:
            cp.wait_send()
        mine.wait()
        acc = all_ref[0:m_per, :]
        for k in range(1, N_DEV):
            acc = acc + all_ref[k * m_per : (k + 1) * m_per, :]
        sum_ref[...] = acc

    return pl.pallas_call(
        body,
        name="all_sum_small",
        in_specs=[_VM],
        out_specs=_VM,
        out_shape=jax.ShapeDtypeStruct((m_per, n), F32),
        scratch_shapes=[
            pltpu.VMEM((N_DEV * m_per, n), F32),
            pltpu.SemaphoreType.DMA((7,)),
            pltpu.SemaphoreType.DMA((7,)),
            pltpu.SemaphoreType.DMA,
        ],
        compiler_params=pltpu.CompilerParams(vmem_limit_bytes=VMEM_LIMIT_BYTES),
    )(packed)


SHARDED = ("a_w_in", "a_w_out", "b_w_in", "b_w_grp", "b_scale", "b_w_out", "ffn_w_gate", "ffn_w_up", "ffn_w_down")
SMALL = ("a_ln_g", "a_ln_b", "a_w_s", "a_b_s", "mix_pre_g", "mix_post_g", "ffn_pre_g", "ffn_post_g")
WEIGHTS = ("a_w_in", "a_ln_g", "a_ln_b", "a_w_s", "a_b_s", "a_w_out", "b_w_in", "b_w_grp", "b_scale", "b_w_out",
           "mix_pre_g", "mix_post_g", "ffn_pre_g", "ffn_post_g", "ffn_w_gate", "ffn_w_up", "ffn_w_down")


TRANSPOSED = ("ffn_w_gate", "ffn_w_up")


def _as_layers(name, a):
    if name in TRANSPOSED:
        return jnp.swapaxes(a, 1, 2)
    if a.ndim == 2:
        return a.reshape(a.shape[0], 1, a.shape[1])
    return a.reshape(a.shape[0], -1, a.shape[-1])


def _from_layers(name, a, shape):
    if name in TRANSPOSED:
        return jnp.swapaxes(a, 1, 2)
    return a.reshape(shape)


def _pack_small(parts):
    return jnp.concatenate([p.reshape(-1, 128) for p in parts], axis=0)


def _unpack_small(packed, like):
    out, row = [], 0
    for ref in like:
        rows = ref.size // 128
        out.append(packed[row : row + rows].reshape(ref.shape))
        row += rows
    return out


def kernel(x, a_w_in, a_ln_g, a_ln_b, a_w_s, a_b_s, a_w_out, b_w_in, b_w_grp, b_scale, b_w_out, mix_pre_g, mix_post_g, ffn_pre_g, ffn_post_g, ffn_w_gate, ffn_w_up, ffn_w_down, loss_target, m_a_w_in, m_a_ln_g, m_a_ln_b, m_a_w_s, m_a_b_s, m_a_w_out, m_b_w_in, m_b_w_grp, m_b_scale, m_b_w_out, m_mix_pre_g, m_mix_post_g, m_ffn_pre_g, m_ffn_post_g, m_ffn_w_gate, m_ffn_w_up, m_ffn_w_down, v_a_w_in, v_a_ln_g, v_a_ln_b, v_a_w_s, v_a_b_s, v_a_w_out, v_b_w_in, v_b_w_grp, v_b_scale, v_b_w_out, v_mix_pre_g, v_mix_post_g, v_ffn_pre_g, v_ffn_post_g, v_ffn_w_gate, v_ffn_w_up, v_ffn_w_down):
    weights = dict(a_w_in=a_w_in, a_ln_g=a_ln_g, a_ln_b=a_ln_b, a_w_s=a_w_s, a_b_s=a_b_s, a_w_out=a_w_out,
                   b_w_in=b_w_in, b_w_grp=b_w_grp, b_scale=b_scale, b_w_out=b_w_out, mix_pre_g=mix_pre_g,
                   mix_post_g=mix_post_g, ffn_pre_g=ffn_pre_g, ffn_post_g=ffn_post_g, ffn_w_gate=ffn_w_gate,
                   ffn_w_up=ffn_w_up, ffn_w_down=ffn_w_down)
    mom1 = dict(a_w_in=m_a_w_in, a_ln_g=m_a_ln_g, a_ln_b=m_a_ln_b, a_w_s=m_a_w_s, a_b_s=m_a_b_s, a_w_out=m_a_w_out,
                b_w_in=m_b_w_in, b_w_grp=m_b_w_grp, b_scale=m_b_scale, b_w_out=m_b_w_out, mix_pre_g=m_mix_pre_g,
                mix_post_g=m_mix_post_g, ffn_pre_g=m_ffn_pre_g, ffn_post_g=m_ffn_post_g, ffn_w_gate=m_ffn_w_gate,
                ffn_w_up=m_ffn_w_up, ffn_w_down=m_ffn_w_down)
    mom2 = dict(a_w_in=v_a_w_in, a_ln_g=v_a_ln_g, a_ln_b=v_a_ln_b, a_w_s=v_a_w_s, a_b_s=v_a_b_s, a_w_out=v_a_w_out,
                b_w_in=v_b_w_in, b_w_grp=v_b_w_grp, b_scale=v_b_scale, b_w_out=v_b_w_out, mix_pre_g=v_mix_pre_g,
                mix_post_g=v_mix_post_g, ffn_pre_g=v_ffn_pre_g, ffn_post_g=v_ffn_post_g, ffn_w_gate=v_ffn_w_gate,
                ffn_w_up=v_ffn_w_up, ffn_w_down=v_ffn_w_down)

    t, d = x.shape[1], x.shape[2]
    depth = mix_pre_g.shape[0]
    gd_b = d // len(B_WINDOWS)
    xs = x.reshape(t, d)
    target = loss_target.reshape(t, d)

    chip = 2 * lax.axis_index("x") + lax.axis_index("y")
    place = jnp.stack([chip, lax.axis_index("c")]).astype(jnp.int32)
    bufs = {name: list(_cast_into_slots("cast_" + name, place, _as_layers(name, weights[name]),
                                        F32 if name == "b_scale" else BF16)) for name in SHARDED}

    def gain(name, i):
        return weights[name][i].reshape(1, d)

    def weight_keys(i):
        j = i // 2
        mixer = [("a_w_in", j), ("a_w_out", j)] if i % 2 == 0 else [("b_w_in", j), ("b_w_grp", j), ("b_w_out", j)]
        return mixer, [("ffn_w_gate", i), ("ffn_w_up", i), ("ffn_w_down", i)]

    def gather(keys):
        return _GatherWeights([bufs[n][j] for n, j in keys],
                              whole=[k for k, (n, _) in enumerate(keys) if n == "b_scale"])

    def gathered(keys, outs):
        for (n, j), buf in zip(keys, outs):
            bufs[n][j] = buf

    first = weight_keys(0)[0] + [("b_scale", j) for j in range(b_scale.shape[0])]
    gathered(first, _exchange("gather_first", [gather(first)])[0])
    saved = []
    cur = xs
    for i in range(depth):
        j = i // 2
        mixer_next, ffn_next = weight_keys(i + 1) if i + 1 < depth else ([], [])
        ffn_keys = weight_keys(i)[1]
        after_ffn = mixer_next if (i + 1) % 2 == 0 else mixer_next + ffn_next
        if i % 2 == 0:
            w_in = bufs["a_w_in"][j]
            q = w_in.shape[2]
            w_out = bufs["a_w_out"][j].reshape(2, q, d)
            ln_g = a_ln_g[j].reshape(2, q)
            ln_b = a_ln_b[j].reshape(2, q)
            b_t = jnp.transpose(a_b_s[j])
            (nxt, act, slope, m), (got,) = _mix_a_fwd(
                f"mix_a_fwd{j}", cur, gain("mix_pre_g", i), gain("mix_post_g", i), w_in, ln_g, ln_b, a_w_s[j], b_t,
                w_out, exchanges=[gather(ffn_keys)])
            gathered(ffn_keys, got)
            mix_saved = dict(x=cur, act=act, slope=slope, m=m, w_in=w_in, w_out=w_out, ln_g=ln_g, ln_b=ln_b, b_t=b_t)
        else:
            w_in = bufs["b_w_in"][j].reshape(d, d)
            w_out = bufs["b_w_out"][j].reshape(d, d)
            w_grp = jnp.transpose(bufs["b_w_grp"][j].reshape(N_CHIPS, len(B_WINDOWS), gd_b // N_CHIPS, gd_b),
                                  (1, 0, 2, 3)).reshape(len(B_WINDOWS), gd_b, gd_b)
            scale = bufs["b_scale"][j].reshape(1, d)
            (p,), _ = _norm_matmul(f"mix_b_in{j}", cur, gain("mix_pre_g", i), w_in)
            (nxt, m), _ = _mix_b_fwd(f"mix_b_fwd{j}", cur, p, gain("mix_post_g", i), w_grp, scale, w_out)
            mix_saved = dict(x=cur, p=p, m=m, w_in=w_in, w_out=w_out, w_grp=w_grp, scale=scale)
        cur = nxt
        wg, wu, wd = (bufs[n][k].reshape(1, -1, d) for n, k in ffn_keys)
        (nxt, a, dup, dgate, f, *sq), (got,) = _ffn_fwd(
            f"ffn_fwd{i}", cur, gain("ffn_pre_g", i), gain("ffn_post_g", i), wg, wu, wd,
            target=target if i == depth - 1 else None, exchanges=[gather(after_ffn)])
        gathered(after_ffn, got)
        saved.append((mix_saved, dict(x=cur, a=a, dup=dup, dgate=dgate, f=f, wg=wg, wu=wu, wd=wd)))
        cur = nxt

    dcur = cur
    loss = lax.psum(0.5 * sq[0][0, 0] / d, ("x", "y", "c"))

    grads = {name: [None] * weights[name].shape[0] for name in WEIGHTS}
    state = dict(to_sibling=[], to_chips=[])
    pair, from_chips = {}, {}

    def exchanges_due():
        return [_ToSibling([a for _, _, a in state["to_sibling"]]), _ToChips([a for _, _, a in state["to_chips"]])]

    def exchanged(outs):
        from_sibling, arrived = outs
        for (n, k, _), got in zip(state["to_chips"], arrived):
            from_chips[n, k] = got
        state["to_chips"] = []
        for (n, k, dw), got in zip(state["to_sibling"], from_sibling):
            pair[n, k] = _pair_sum(f"pair_sum_{n}{k}", place, dw, got)
            state["to_chips"].append((n, k, pair[n, k]))
        state["to_sibling"] = []

    def made(name, k, dw):
        grads[name][k] = dw
        state["to_sibling"].append((name, k, dw))

    for i in reversed(range(depth)):
        j = i // 2
        mix_saved, ffn_saved = saved[i]
        s = ffn_saved
        (dg, du, dwd, dgain), outs = _ffn_bwd_hidden(
            f"ffn_bwd_hidden{i}", dcur, s["f"], gain("ffn_post_g", i), s["a"], s["dup"], s["dgate"], s["wd"],
            exchanges=exchanges_due())
        exchanged(outs)
        grads["ffn_post_g"][i] = dgain
        made("ffn_w_down", i, dwd.reshape(N_CHIPS, -1, d))
        (dcur, dwg, dwu, dgain), outs = _bwd_in(
            f"ffn_bwd_in{i}", dcur, s["x"], gain("ffn_pre_g", i), [dg, du], [s["wg"], s["wu"]], transposed=True,
            exchanges=exchanges_due())
        exchanged(outs)
        grads["ffn_pre_g"][i] = dgain
        made("ffn_w_gate", i, dwg.reshape(N_CHIPS, -1, d))
        made("ffn_w_up", i, dwu.reshape(N_CHIPS, -1, d))
        s = mix_saved
        if i % 2 == 0:
            (dz, dwout, dws, dbacc, dlng, dlnb, dgain), outs = _mix_a_bwd_hidden(
                f"mix_a_bwd_hidden{j}", dcur, s["m"], gain("mix_post_g", i), s["act"], s["slope"], s["ln_g"], s["ln_b"],
                a_w_s[j], s["b_t"], s["w_out"], exchanges=exchanges_due())
            exchanged(outs)
            grads["a_w_s"][j] = dws
            grads["a_b_s"][j] = jnp.transpose(dbacc)
            grads["a_ln_g"][j] = dlng.reshape(-1)
            grads["a_ln_b"][j] = dlnb.reshape(-1)
            grads["mix_post_g"][i] = dgain
            made("a_w_out", j, dwout.reshape(N_CHIPS, -1, d))
            (dcur, dwin, dgain), outs = _bwd_in(
                f"mix_a_bwd_in{j}", dcur, s["x"], gain("mix_pre_g", i), [dz], [s["w_in"]], exchanges=exchanges_due())
            exchanged(outs)
            grads["mix_pre_g"][i] = dgain
            made("a_w_in", j, dwin)
        else:
            (dq, dwout, dwgrp, dscale, dgain), outs = _mix_b_bwd_hidden(
                f"mix_b_bwd_hidden{j}", dcur, s["m"], gain("mix_post_g", i), s["p"], s["w_grp"], s["scale"],
                s["w_out"], exchanges=exchanges_due())
            exchanged(outs)
            grads["b_scale"][j] = dscale
            grads["mix_post_g"][i] = dgain
            made("b_w_out", j, dwout.reshape(N_CHIPS, -1, d))
            made("b_w_grp", j, jnp.transpose(
                dwgrp.reshape(len(B_WINDOWS), N_CHIPS, gd_b // N_CHIPS, gd_b), (1, 0, 2, 3)).reshape(N_CHIPS, -1, gd_b))
            (dp,), _ = _pool_bwd(f"pool_bwd{j}", dq)
            (dcur, dwin, dgain), outs = _bwd_in(
                f"mix_b_bwd_in{j}", dcur, s["x"], gain("mix_pre_g", i), [dp], [s["w_in"].reshape(1, d, d)],
                exchanges=exchanges_due())
            exchanged(outs)
            grads["mix_pre_g"][i] = dgain
            made("b_w_in", j, dwin.reshape(N_CHIPS, -1, d))
    grad_x = dcur.reshape(x.shape)
    exchanged(_exchange("grads_last", exchanges_due()))
    exchanged(_exchange("grads_last_to_chips", exchanges_due()))

    reduced_names = [name for name in SHARDED if name != "b_scale"]
    sums = []
    for name in reduced_names:
        layers = range(weights[name].shape[0])
        sums.append(_chip_sum("chip_sum_" + name, place, [pair[name, k] for k in layers],
                              [from_chips[name, k] for k in layers]))
    reduced = dict(zip(reduced_names, _exchange("swap_halves", [_SwapHalves(sums)])[0]))

    small_grads = [jnp.stack([g.reshape(weights[name].shape[1:]) for g in grads[name]], axis=0) for name in SMALL]
    scale_grad = jnp.concatenate(grads["b_scale"], axis=0)
    summed = _all_sum_small(_pack_small(small_grads + [scale_grad]))
    small_rows = summed.shape[0] - scale_grad.size // 128
    scale_sum = summed[small_rows:].reshape(scale_grad.shape)
    reduced["b_scale"] = lax.dynamic_slice_in_dim(scale_sum, chip * b_scale.shape[1], b_scale.shape[1], axis=1)[:, None, :]
    summed = summed[:small_rows]

    out_g, out_d, out_m, out_v = {}, {}, {}, {}
    for name in SHARDED:
        shape = weights[name].shape
        dlt, nm, nv = _adamw("adamw_" + name, _as_layers(name, weights[name]), reduced[name],
                             _as_layers(name, mom1[name]), _as_layers(name, mom2[name]))
        out_g[name] = _from_layers(name, reduced[name], shape)
        out_d[name], out_m[name], out_v[name] = (_from_layers(name, a, shape) for a in (dlt, nm, nv))
    small_like = [weights[name] for name in SMALL]
    packs = [_pack_small([src[name] for name in SMALL]).reshape(1, -1, 128) for src in (weights, mom1, mom2)]
    dlt, nm, nv = _adamw("adamw_small", packs[0], summed.reshape(1, -1, 128), packs[1], packs[2])
    for dst, packed in ((out_g, summed), (out_d, dlt[0]), (out_m, nm[0]), (out_v, nv[0])):
        for name, val in zip(SMALL, _unpack_small(packed, small_like)):
            dst[name] = val

    return (loss, grad_x, *[out_g[n] for n in WEIGHTS], *[out_d[n] for n in WEIGHTS],
            *[out_m[n] for n in WEIGHTS], *[out_v[n] for n in WEIGHTS])
```

```python
import functools
import math

import jax
import jax.numpy as jnp
from jax import lax
from jax.experimental import pallas as pl
from jax.experimental.pallas import tpu as pltpu

F32 = jnp.float32
BF16 = jnp.bfloat16
MESH = pl.DeviceIdType.MESH

EPS = 1e-6
CHUNK = 128
A_GROUPS = 8
B_WINDOWS = (2, 4, 8, 16)
HALO = 16
N_CHIPS = 4
N_DEV = 8

ADAM_LR = 0.001
ADAM_B1 = 0.9
ADAM_B2 = 0.999
ADAM_EPS = 1e-08
ADAM_WD = 0.01
ADAM_STEP = 10

VMEM_LIMIT_BYTES = 60 * 1024 * 1024
INV_SQRT2 = 1.0 / math.sqrt(2.0)
INV_SQRT_2PI = 1.0 / math.sqrt(2.0 * math.pi)

_ANY = pl.BlockSpec(memory_space=pl.ANY)
_VM = pl.BlockSpec(memory_space=pltpu.VMEM)


def _params():
    return pltpu.CompilerParams(dimension_semantics=("arbitrary",), vmem_limit_bytes=VMEM_LIMIT_BYTES)


def _token_block(t):
    return 256 if t >= 1024 else 128


def _step_rows(t):
    return 2 * _token_block(t)


def _rows(tm, d):
    return pl.BlockSpec((tm, d), lambda i: (i, 0))


def _blocks(nb, tm, bw):
    return pl.BlockSpec((nb, tm, bw), lambda i: (0, i, 0))


def _dot(a, b):
    return lax.dot_general(a, b, (((1,), (0,)), ((), ())), preferred_element_type=F32)


def _dot_nt(a, b):
    return lax.dot_general(a, b, (((1,), (1,)), ((), ())), preferred_element_type=F32)


def _dot_tn(a, b):
    return lax.dot_general(a, b, (((0,), (0,)), ((), ())), preferred_element_type=F32)


def _rms(x, g):
    return x * lax.rsqrt(jnp.mean(x * x, axis=-1, keepdims=True) + EPS) * g


def _rms_bwd(dy, x, g):
    r = lax.rsqrt(jnp.mean(x * x, axis=-1, keepdims=True) + EPS)
    n = x * r
    dn = dy * g
    dx = r * (dn - n * jnp.mean(dn * n, axis=-1, keepdims=True))
    return dx, jnp.sum(dy * n, axis=0, keepdims=True)


def _gelu_and_grad(x):
    cdf = 0.5 * (1.0 + lax.erf(x * INV_SQRT2))
    return x * cdf, cdf + x * (jnp.exp(-0.5 * x * x) * INV_SQRT_2PI)


def _position():
    return lax.axis_index("x"), lax.axis_index("y"), lax.axis_index("c")


def _other_chips(x, y):
    return [(1 - x, y), (x, 1 - y), (1 - x, 1 - y)]


class _GatherWeights:
    def __init__(self, bufs, whole=()):
        self.inputs = list(bufs)
        self.out_shapes = [jax.ShapeDtypeStruct(b.shape, b.dtype) for b in bufs]
        self.aliases = {w: w for w in range(len(bufs))}
        self.n_sems = 6 * len(bufs)
        self.whole = frozenset(whole)

    def _part(self, outs, w, slot, core):
        if w in self.whole:
            return outs[w].at[slot]
        half = outs[w].shape[1] // 2
        return outs[w].at[slot, pl.ds(core * half, half)]

    def _ici(self, outs, send, recv, w, j, slot):
        x, y, c = _position()
        px, py = _other_chips(x, y)[j]
        part = self._part(outs, w, slot, c)
        return pltpu.make_async_remote_copy(
            src_ref=part, dst_ref=part, send_sem=send.at[6 * w + j], recv_sem=recv.at[6 * w + j],
            device_id=(px, py, c), device_id_type=MESH)

    def _d2d(self, outs, send, recv, w, j, slot, core):
        x, y, c = _position()
        part = self._part(outs, w, slot, core)
        return pltpu.make_async_remote_copy(
            src_ref=part, dst_ref=part, send_sem=send.at[6 * w + 3 + j], recv_sem=recv.at[6 * w + 3 + j],
            device_id=(x, y, 1 - c), device_id_type=MESH)

    def start(self, ins, outs, send, recv):
        x, y, _ = _position()
        for w in range(len(outs)):
            for j in range(3):
                self._ici(outs, send, recv, w, j, 2 * x + y).start()

    def finish(self, ins, outs, send, recv):
        x, y, c = _position()
        slots = [2 * px + py for px, py in _other_chips(x, y)]
        for w in range(len(outs)):
            for j, slot in enumerate(slots):
                self._ici(outs, send, recv, w, j, slot).wait_recv()
                if w not in self.whole:
                    self._d2d(outs, send, recv, w, j, slot, c).start()
        for w in range(len(outs)):
            for j, slot in enumerate(slots):
                if w not in self.whole:
                    self._d2d(outs, send, recv, w, j, slot, 1 - c).wait_recv()
        for w in range(len(outs)):
            for j, slot in enumerate(slots):
                self._ici(outs, send, recv, w, j, 2 * x + y).wait_send()
                if w not in self.whole:
                    self._d2d(outs, send, recv, w, j, slot, c).wait_send()


class _ToSibling:
    def __init__(self, grads):
        self.inputs = list(grads)
        self.out_shapes = [jax.ShapeDtypeStruct((g.shape[0], g.shape[1] // 2, g.shape[2]), g.dtype) for g in grads]
        self.aliases = {}
        self.n_sems = len(grads)

    def _copy(self, ins, outs, send, recv, w):
        x, y, c = _position()
        half = ins[w].shape[1] // 2
        return pltpu.make_async_remote_copy(
            src_ref=ins[w].at[:, pl.ds((1 - c) * half, half)], dst_ref=outs[w],
            send_sem=send.at[w], recv_sem=recv.at[w], device_id=(x, y, 1 - c), device_id_type=MESH)

    def start(self, ins, outs, send, recv):
        for w in range(len(ins)):
            self._copy(ins, outs, send, recv, w).start()

    def finish(self, ins, outs, send, recv):
        for w in range(len(ins)):
            self._copy(ins, outs, send, recv, w).wait_recv()
        for w in range(len(ins)):
            self._copy(ins, outs, send, recv, w).wait_send()


class _ToChips:
    def __init__(self, parts):
        self.inputs = list(parts)
        self.out_shapes = [jax.ShapeDtypeStruct(p.shape, p.dtype) for p in parts]
        self.aliases = {}
        self.n_sems = 3 * len(parts)

    def _copy(self, ins, outs, send, recv, w, j, outbound):
        x, y, c = _position()
        px, py = _other_chips(x, y)[j]
        me, peer = 2 * x + y, 2 * px + py
        src_slot, dst_slot = (peer, me) if outbound else (me, peer)
        return pltpu.make_async_remote_copy(
            src_ref=ins[w].at[src_slot], dst_ref=outs[w].at[dst_slot],
            send_sem=send.at[3 * w + j], recv_sem=recv.at[3 * w + j], device_id=(px, py, c), device_id_type=MESH)

    def start(self, ins, outs, send, recv):
        for w in range(len(ins)):
            for j in range(3):
                self._copy(ins, outs, send, recv, w, j, True).start()

    def finish(self, ins, outs, send, recv):
        for w in range(len(ins)):
            for j in range(3):
                self._copy(ins, outs, send, recv, w, j, False).wait_recv()
        for w in range(len(ins)):
            for j in range(3):
                self._copy(ins, outs, send, recv, w, j, True).wait_send()


class _SwapHalves:
    def __init__(self, bufs):
        self.inputs = list(bufs)
        self.out_shapes = [jax.ShapeDtypeStruct(b.shape, b.dtype) for b in bufs]
        self.aliases = {w: w for w in range(len(bufs))}
        self.n_sems = len(bufs)

    def _copy(self, outs, send, recv, w, core):
        x, y, c = _position()
        half = outs[w].shape[1] // 2
        rows = outs[w].at[:, pl.ds(core * half, half)]
        return pltpu.make_async_remote_copy(
            src_ref=rows, dst_ref=rows, send_sem=send.at[w], recv_sem=recv.at[w],
            device_id=(x, y, 1 - c), device_id_type=MESH)

    def start(self, ins, outs, send, recv):
        c = lax.axis_index("c")
        for w in range(len(outs)):
            self._copy(outs, send, recv, w, c).start()

    def finish(self, ins, outs, send, recv):
        c = lax.axis_index("c")
        for w in range(len(outs)):
            self._copy(outs, send, recv, w, 1 - c).wait_recv()
        for w in range(len(outs)):
            self._copy(outs, send, recv, w, c).wait_send()


def _call(name, body, *, grid, in_specs, out_specs, out_shape, args, scratch_shapes=(), prefetch=(), exchanges=()):
    given = list(exchanges)
    exchanges = [e for e in given if e.inputs]
    n_pre, n_in, n_out, n_scr = len(prefetch), len(args), len(out_shape), len(scratch_shapes)
    ex_in = [a for e in exchanges for a in e.inputs]
    ex_out = [s for e in exchanges for s in e.out_shapes]
    aliases = {}
    at_in, at_out = n_pre + n_in, n_out
    for e in exchanges:
        for i, o in e.aliases.items():
            aliases[at_in + i] = at_out + o
        at_in += len(e.inputs)
        at_out += len(e.out_shapes)

    def at_step(last):
        hit = None
        for axis, n in enumerate(grid):
            here = pl.program_id(axis) == (n - 1 if last else 0)
            hit = here if hit is None else jnp.logical_and(hit, here)
        return hit

    def fused(*refs):
        pre, refs = refs[:n_pre], refs[n_pre:]
        body_in, refs = refs[:n_in], refs[n_in:]
        ex_in_refs, refs = refs[: len(ex_in)], refs[len(ex_in) :]
        body_out, refs = refs[:n_out], refs[n_out:]
        ex_out_refs, refs = refs[: len(ex_out)], refs[len(ex_out) :]
        body_scr, sems = refs[:n_scr], refs[n_scr:]

        def each(stage):
            a = b = 0
            for n, e in enumerate(exchanges):
                ins, outs = ex_in_refs[a : a + len(e.inputs)], ex_out_refs[b : b + len(e.out_shapes)]
                getattr(e, stage)(ins, outs, sems[2 * n], sems[2 * n + 1])
                a += len(e.inputs)
                b += len(e.out_shapes)

        if exchanges:
            @pl.when(at_step(False))
            def _():
                each("start")

        if body is not None:
            body(*pre, *body_in, *body_out, *body_scr)

        if exchanges:
            @pl.when(at_step(True))
            def _():
                each("finish")

    outs = pl.pallas_call(
        fused,
        name=name,
        grid_spec=pltpu.PrefetchScalarGridSpec(
            num_scalar_prefetch=n_pre,
            grid=grid,
            in_specs=list(in_specs) + [_ANY] * len(ex_in),
            out_specs=list(out_specs) + [_ANY] * len(ex_out),
            scratch_shapes=list(scratch_shapes)
            + [pltpu.SemaphoreType.DMA((e.n_sems,)) for e in exchanges for _ in range(2)],
        ),
        out_shape=list(out_shape) + ex_out,
        input_output_aliases=aliases,
        compiler_params=pltpu.CompilerParams(
            dimension_semantics=("arbitrary",) * len(grid), vmem_limit_bytes=VMEM_LIMIT_BYTES),
    )(*prefetch, *args, *ex_in)
    body_outs, rest = list(outs[:n_out]), list(outs[n_out:])
    ex_outs = []
    for e in given:
        n_e = len(e.out_shapes) if e.inputs else 0
        ex_outs.append(rest[:n_e])
        rest = rest[n_e:]
    return body_outs, ex_outs


def _exchange(name, exchanges):
    return _call(name, None, grid=(1,), in_specs=[], out_specs=[], out_shape=[], args=[], exchanges=exchanges)[1]


def _ffn_fwd(name, x, gpre, gpost, wg, wu, wd, target=None, exchanges=()):
    t, d = x.shape
    nb, fs, _ = wg.shape
    sub = _token_block(t)
    tm = _step_rows(t)
    with_loss = target is not None

    def body(x_ref, gpre_ref, gpost_ref, wg_ref, wu_ref, wd_ref, *refs):
        if with_loss:
            t_ref, o_ref, a_ref, dup_ref, dgate_ref, f_ref, sq_ref = refs

            @pl.when(pl.program_id(0) == 0)
            def _():
                sq_ref[...] = jnp.zeros_like(sq_ref)
        else:
            o_ref, a_ref, dup_ref, dgate_ref, f_ref = refs
        for h in range(tm // sub):
            rows = slice(h * sub, (h + 1) * sub)
            xv = x_ref[rows, :]
            hb = _rms(xv, gpre_ref[...]).astype(BF16)
            f = jnp.zeros((sub, d), F32)
            for k in range(nb):
                g = _dot_nt(hb, wg_ref[k])
                u = _dot_nt(hb, wu_ref[k])
                s = jax.nn.sigmoid(g)
                sg = g * s
                a = (sg * u).astype(BF16)
                a_ref[k, rows, :] = a
                dup_ref[k, rows, :] = sg.astype(BF16)
                dgate_ref[k, rows, :] = (u * (s * (1.0 + g * (1.0 - s)))).astype(BF16)
                f = f + _dot(a, wd_ref[k])
            f_ref[rows, :] = f
            y = xv + _rms(f, gpost_ref[...])
            if with_loss:
                err = y - t_ref[rows, :]
                o_ref[rows, :] = err / d
                sq_ref[...] += jnp.sum(err * err)
            else:
                o_ref[rows, :] = y

    return _call(
        name,
        body,
        grid=(t // tm,),
        in_specs=[_rows(tm, d), _VM, _VM, _VM, _VM, _VM] + [_rows(tm, d)] * with_loss,
        out_specs=[_rows(tm, d)] + [_blocks(nb, tm, fs)] * 3 + [_rows(tm, d)] + [_VM] * with_loss,
        out_shape=[jax.ShapeDtypeStruct((t, d), F32)]
        + [jax.ShapeDtypeStruct((nb, t, fs), BF16)] * 3
        + [jax.ShapeDtypeStruct((t, d), F32)]
        + [jax.ShapeDtypeStruct((8, 128), F32)] * with_loss,
        args=[x, gpre, gpost, wg, wu, wd] + [target] * with_loss,
        exchanges=exchanges,
    )


def _ffn_bwd_hidden(name, dy, f, gpost, a, dup, dgate, wd, exchanges=()):
    t, d = dy.shape
    nb, fs, _ = wd.shape
    tm = _step_rows(t)

    def body(dy_ref, f_ref, gpost_ref, a_ref, dup_ref, dgate_ref, wd_ref, dg_ref, du_ref, dwd_ref, dgain_ref):
        @pl.when(pl.program_id(0) == 0)
        def _():
            dwd_ref[...] = jnp.zeros_like(dwd_ref)
            dgain_ref[...] = jnp.zeros_like(dgain_ref)

        df, dgain = _rms_bwd(dy_ref[...], f_ref[...], gpost_ref[...])
        dgain_ref[...] += dgain
        dfb = df.astype(BF16)
        for k in range(nb):
            da = _dot_nt(dfb, wd_ref[k])
            dwd_ref[k] += _dot_tn(a_ref[k], dfb)
            du_ref[k] = (da * dup_ref[k].astype(F32)).astype(BF16)
            dg_ref[k] = (da * dgate_ref[k].astype(F32)).astype(BF16)

    return _call(
        name,
        body,
        grid=(t // tm,),
        in_specs=[_rows(tm, d), _rows(tm, d), _VM] + [_blocks(nb, tm, fs)] * 3 + [_VM],
        out_specs=[_blocks(nb, tm, fs), _blocks(nb, tm, fs), _VM, _VM],
        out_shape=[
            jax.ShapeDtypeStruct((nb, t, fs), BF16),
            jax.ShapeDtypeStruct((nb, t, fs), BF16),
            jax.ShapeDtypeStruct((nb, fs, d), F32),
            jax.ShapeDtypeStruct((1, d), F32),
        ],
        args=[dy, f, gpost, a, dup, dgate, wd],
        exchanges=exchanges,
    )


def _bwd_in(name, dres, x, gpre, dzs, ws, transposed=False, exchanges=()):
    t, d = x.shape
    n = len(ws)
    resident = sum(6 * w.size for w in ws)
    tm = _step_rows(t) if resident <= VMEM_LIMIT_BYTES // 2 else _token_block(t)
    widths = [w.shape[1] if transposed else w.shape[2] for w in ws]

    def body(*refs):
        dres_ref, x_ref, gpre_ref = refs[:3]
        dz_refs = refs[3 : 3 + n]
        w_refs = refs[3 + n : 3 + 2 * n]
        dx_ref = refs[3 + 2 * n]
        dw_refs = refs[4 + 2 * n : 4 + 3 * n]
        dgain_ref = refs[4 + 3 * n]

        @pl.when(pl.program_id(0) == 0)
        def _():
            for dw_ref in dw_refs:
                dw_ref[...] = jnp.zeros_like(dw_ref)
            dgain_ref[...] = jnp.zeros_like(dgain_ref)

        xv = x_ref[...]
        gain = gpre_ref[...]
        hb = _rms(xv, gain).astype(BF16)
        dh = jnp.zeros((tm, d), F32)
        for dz_ref, w_ref, dw_ref in zip(dz_refs, w_refs, dw_refs):
            for k in range(w_ref.shape[0]):
                dz = dz_ref[k]
                if transposed:
                    dh = dh + _dot(dz, w_ref[k])
                    dw_ref[k] += _dot_tn(dz, hb)
                else:
                    dh = dh + _dot_nt(dz, w_ref[k])
                    dw_ref[k] += _dot_tn(hb, dz)
        dx, dgain = _rms_bwd(dh, xv, gain)
        dx_ref[...] = dres_ref[...] + dx
        dgain_ref[...] += dgain

    return _call(
        name,
        body,
        grid=(t // tm,),
        in_specs=[_rows(tm, d), _rows(tm, d), _VM]
        + [_blocks(w.shape[0], tm, bw) for w, bw in zip(ws, widths)]
        + [_VM] * n,
        out_specs=[_rows(tm, d)] + [_VM] * n + [_VM],
        out_shape=[jax.ShapeDtypeStruct((t, d), F32)]
        + [jax.ShapeDtypeStruct(w.shape, F32) for w in ws]
        + [jax.ShapeDtypeStruct((1, d), F32)],
        args=[dres, x, gpre, *dzs, *ws],
        exchanges=exchanges,
    )


def _causal_weights(ws_ref):
    row = lax.broadcasted_iota(jnp.int32, (CHUNK, CHUNK), 0)
    col = lax.broadcasted_iota(jnp.int32, (CHUNK, CHUNK), 1)
    return [jnp.where(row >= col, ws_ref[g], 0.0).astype(BF16) for g in range(A_GROUPS)]


def _layernorm_halves(v0, v1):
    width = v0.shape[-1] + v1.shape[-1]
    mu = (jnp.sum(v0, axis=-1, keepdims=True) + jnp.sum(v1, axis=-1, keepdims=True)) / width
    c0 = v0 - mu
    c1 = v1 - mu
    var = (jnp.sum(c0 * c0, axis=-1, keepdims=True) + jnp.sum(c1 * c1, axis=-1, keepdims=True)) / width
    rstd = lax.rsqrt(var + EPS)
    return c0 * rstd, c1 * rstd, rstd


def _spatial_gate(sv_ref, wtril, vl, bt_ref, half, tm, gd):
    for gg in range(A_GROUPS // 2):
        g = half * (A_GROUPS // 2) + gg
        bias = bt_ref[:, g : g + 1]
        for n in range(tm // CHUNK):
            blk = vl[n * CHUNK : (n + 1) * CHUNK, gg * gd : (gg + 1) * gd]
            sv_ref[n * CHUNK : (n + 1) * CHUNK, gg * gd : (gg + 1) * gd] = _dot(wtril[g], blk) + bias


def _mix_a_fwd(name, x, gpre, gpost, w_in, ln_g, ln_b, w_s, b_t, w_out, exchanges=()):
    t, d = x.shape
    _, _, q = w_in.shape
    gd = 2 * q // A_GROUPS
    sub = _token_block(t)
    tm = _step_rows(t)

    def body(x_ref, gpre_ref, gpost_ref, win_ref, lng_ref, lnb_ref, ws_ref, bt_ref, wout_ref,
             o_ref, z_ref, dz_ref, m_ref, *sv_refs):
        wtril = _causal_weights(ws_ref)
        for h, sv_ref in enumerate(sv_refs):
            rows = slice(h * sub, (h + 1) * sub)
            xv = x_ref[rows, :]
            hb = _rms(xv, gpre_ref[...]).astype(BF16)
            z = []
            for k in range(4):
                act, slope = _gelu_and_grad(_dot(hb, win_ref[k]))
                z_ref[k, rows, :] = act.astype(BF16)
                dz_ref[k, rows, :] = slope.astype(BF16)
                z.append(act)
            vh0, vh1, _ = _layernorm_halves(z[2], z[3])
            vls = [(vh * lng_ref[b : b + 1, :] + lnb_ref[b : b + 1, :]).astype(BF16)
                   for b, vh in enumerate((vh0, vh1))]
            m = jnp.zeros((sub, d), F32)
            for b in range(2):
                _spatial_gate(sv_ref, wtril, vls[b], bt_ref, b, sub, gd)
                gated = (z[b] * sv_ref[...]).astype(BF16)
                m = m + _dot(gated, wout_ref[b])
            m_ref[rows, :] = m
            o_ref[rows, :] = xv + _rms(m, gpost_ref[...])

    return _call(
        name,
        body,
        grid=(t // tm,),
        in_specs=[_rows(tm, d)] + [_VM] * 8,
        out_specs=[_rows(tm, d), _blocks(4, tm, q), _blocks(4, tm, q), _rows(tm, d)],
        out_shape=[
            jax.ShapeDtypeStruct((t, d), F32),
            jax.ShapeDtypeStruct((4, t, q), BF16),
            jax.ShapeDtypeStruct((4, t, q), BF16),
            jax.ShapeDtypeStruct((t, d), F32),
        ],
        scratch_shapes=[pltpu.VMEM((sub, q), F32)] * (tm // sub),
        args=[x, gpre, gpost, w_in, ln_g, ln_b, w_s, b_t, w_out],
        exchanges=exchanges,
    )


def _mix_a_bwd_hidden(name, dy, m, gpost, act, slope, ln_g, ln_b, w_s, b_t, w_out, exchanges=()):
    t, d = dy.shape
    _, _, q = act.shape
    gd = 2 * q // A_GROUPS
    tm = _step_rows(t)
    n_chunks = tm // CHUNK

    def body(dy_ref, m_ref, gpost_ref, z_ref, slope_ref, lng_ref, lnb_ref, ws_ref, bt_ref, wout_ref,
             dz_ref, dwout_ref, dws_ref, dbacc_ref, dlng_ref, dlnb_ref, dgain_ref, sv_ref, dvl_ref):
        first = pl.program_id(0) == 0

        @pl.when(first)
        def _():
            for ref in (dwout_ref, dws_ref, dbacc_ref, dlng_ref, dlnb_ref, dgain_ref):
                ref[...] = jnp.zeros_like(ref)

        dm, dgain = _rms_bwd(dy_ref[...], m_ref[...], gpost_ref[...])
        dgain_ref[...] += dgain
        dmb = dm.astype(BF16)
        vhs = list(_layernorm_halves(z_ref[2].astype(F32), z_ref[3].astype(F32)))
        rstd = vhs.pop()
        vls = [(vh * lng_ref[b : b + 1, :] + lnb_ref[b : b + 1, :]).astype(BF16) for b, vh in enumerate(vhs)]
        wtril = _causal_weights(ws_ref)
        dvhs = []
        for b in range(2):
            u = z_ref[b].astype(F32)
            _spatial_gate(sv_ref, wtril, vls[b], bt_ref, b, tm, gd)
            sv = sv_ref[...]
            gated = (u * sv).astype(BF16)
            dgated = _dot_nt(dmb, wout_ref[b])
            dwout_ref[b] += _dot_tn(gated, dmb)
            dz_ref[b] = (dgated * sv * slope_ref[b].astype(F32)).astype(BF16)
            dsv = dgated * u
            folded = dsv[0:CHUNK, :]
            for c in range(1, n_chunks):
                folded = folded + dsv[c * CHUNK : (c + 1) * CHUNK, :]
            for gg in range(A_GROUPS // 2):
                g = b * (A_GROUPS // 2) + gg
                dbacc_ref[:, g : g + 1] += jnp.sum(folded[:, gg * gd : (gg + 1) * gd], axis=1, keepdims=True)
            dsvb = dsv.astype(BF16)
            for gg in range(A_GROUPS // 2):
                g = b * (A_GROUPS // 2) + gg
                for c in range(n_chunks):
                    rows = slice(c * CHUNK, (c + 1) * CHUNK)
                    cols = slice(gg * gd, (gg + 1) * gd)
                    blk = dsvb[rows, cols]
                    dvl_ref[rows, cols] = _dot_tn(wtril[g], blk)
                    dws_ref[g] += _dot_nt(blk, vls[b][rows, cols])
            dvl = dvl_ref[...]
            dlng_ref[b : b + 1, :] += jnp.sum(dvl * vhs[b], axis=0, keepdims=True)
            dlnb_ref[b : b + 1, :] += jnp.sum(dvl, axis=0, keepdims=True)
            dvhs.append(dvl * lng_ref[b : b + 1, :])
        width = 2.0 * q
        m1 = (jnp.sum(dvhs[0], axis=-1, keepdims=True) + jnp.sum(dvhs[1], axis=-1, keepdims=True)) / width
        m2 = (jnp.sum(dvhs[0] * vhs[0], axis=-1, keepdims=True)
              + jnp.sum(dvhs[1] * vhs[1], axis=-1, keepdims=True)) / width
        for b in range(2):
            dv = rstd * (dvhs[b] - m1 - vhs[b] * m2)
            dz_ref[2 + b] = (dv * slope_ref[2 + b].astype(F32)).astype(BF16)

        @pl.when(pl.program_id(0) == t // tm - 1)
        def _():
            row = lax.broadcasted_iota(jnp.int32, (CHUNK, CHUNK), 0)
            col = lax.broadcasted_iota(jnp.int32, (CHUNK, CHUNK), 1)
            for g in range(A_GROUPS):
                dws_ref[g] = jnp.where(row >= col, dws_ref[g], 0.0)

    return _call(
        name,
        body,
        grid=(t // tm,),
        in_specs=[_rows(tm, d), _rows(tm, d), _VM, _blocks(4, tm, q), _blocks(4, tm, q)] + [_VM] * 5,
        out_specs=[_blocks(4, tm, q)] + [_VM] * 6,
        out_shape=[
            jax.ShapeDtypeStruct((4, t, q), BF16),
            jax.ShapeDtypeStruct((2, q, d), F32),
            jax.ShapeDtypeStruct((A_GROUPS, CHUNK, CHUNK), F32),
            jax.ShapeDtypeStruct((CHUNK, A_GROUPS), F32),
            jax.ShapeDtypeStruct((2, q), F32),
            jax.ShapeDtypeStruct((2, q), F32),
            jax.ShapeDtypeStruct((1, d), F32),
        ],
        scratch_shapes=[pltpu.VMEM((tm, q), F32), pltpu.VMEM((tm, q), F32)],
        args=[dy, m, gpost, act, slope, ln_g, ln_b, w_s, b_t, w_out],
        exchanges=exchanges,
    )


def _norm_matmul(name, x, gpre, w, exchanges=()):
    t, d = x.shape
    n = w.shape[1]
    tm = _step_rows(t)

    def body(x_ref, gpre_ref, w_ref, o_ref):
        o_ref[...] = _dot(_rms(x_ref[...], gpre_ref[...]).astype(BF16), w_ref[...])

    return _call(
        name,
        body,
        grid=(t // tm,),
        in_specs=[_rows(tm, d), _VM, _VM],
        out_specs=[_rows(tm, n)],
        out_shape=[jax.ShapeDtypeStruct((t, n), F32)],
        args=[x, gpre, w],
        exchanges=exchanges,
    )


def _window_counts(tm, win):
    pos = pl.program_id(0) * tm + lax.broadcasted_iota(jnp.int32, (tm, 1), 0)
    return jnp.minimum(pos + 1, win).astype(F32)


def _pooled(p, halo, tm, gd):
    prev = jnp.where(pl.program_id(0) == 0, 0.0, halo)
    ext = jnp.concatenate([prev, p], axis=0)
    out = []
    for g, win in enumerate(B_WINDOWS):
        s = ext[:, g * gd : (g + 1) * gd]
        step = 1
        while step < win:
            s = s + pltpu.roll(s, step, 0)
            step *= 2
        total = s[HALO:, :]
        out.append(total / _window_counts(tm, win) - p[:, g * gd : (g + 1) * gd])
    return out


def _halo_spec(t, tm, d, ahead):
    per = tm // HALO
    if ahead:
        return pl.BlockSpec((HALO, d), lambda i: (jnp.minimum((i + 1) * per, t // HALO - 1), 0))
    return pl.BlockSpec((HALO, d), lambda i: (jnp.maximum(i * per - 1, 0), 0))


def _mix_b_fwd(name, x, p, gpost, w_grp, scale, w_out, exchanges=()):
    t, d = x.shape
    gd = d // len(B_WINDOWS)
    tm = _step_rows(t)

    def body(x_ref, p_ref, halo_ref, gpost_ref, wgrp_ref, scale_ref, wout_ref, o_ref, m_ref):
        pooled = _pooled(p_ref[...], halo_ref[...], tm, gd)
        mixed = jnp.concatenate([_dot(pg.astype(BF16), wgrp_ref[g]) for g, pg in enumerate(pooled)], axis=1)
        m = _dot((mixed * scale_ref[...]).astype(BF16), wout_ref[...])
        m_ref[...] = m
        o_ref[...] = x_ref[...] + _rms(m, gpost_ref[...])

    return _call(
        name,
        body,
        grid=(t // tm,),
        in_specs=[_rows(tm, d), _rows(tm, d), _halo_spec(t, tm, d, False), _VM, _VM, _VM, _VM],
        out_specs=[_rows(tm, d), _rows(tm, d)],
        out_shape=[jax.ShapeDtypeStruct((t, d), F32), jax.ShapeDtypeStruct((t, d), F32)],
        args=[x, p, p, gpost, w_grp, scale, w_out],
        exchanges=exchanges,
    )


def _mix_b_bwd_hidden(name, dy, m, gpost, p, w_grp, scale, w_out, exchanges=()):
    t, d = dy.shape
    gd = d // len(B_WINDOWS)
    tm = _step_rows(t)

    def body(dy_ref, m_ref, gpost_ref, p_ref, halo_ref, wgrp_ref, scale_ref, wout_ref,
             dq_ref, dwout_ref, dwgrp_ref, dscale_ref, dgain_ref):
        @pl.when(pl.program_id(0) == 0)
        def _():
            for ref in (dwout_ref, dwgrp_ref, dscale_ref, dgain_ref):
                ref[...] = jnp.zeros_like(ref)

        dm, dgain = _rms_bwd(dy_ref[...], m_ref[...], gpost_ref[...])
        dgain_ref[...] += dgain
        dmb = dm.astype(BF16)
        pooled = [pg.astype(BF16) for pg in _pooled(p_ref[...], halo_ref[...], tm, gd)]
        mixed = jnp.concatenate([_dot(pg, wgrp_ref[g]) for g, pg in enumerate(pooled)], axis=1)
        scale = scale_ref[...]
        ms = (mixed * scale).astype(BF16)
        dms = _dot_nt(dmb, wout_ref[...])
        dwout_ref[...] += _dot_tn(ms, dmb)
        dscale_ref[...] += jnp.sum(dms * mixed, axis=0, keepdims=True)
        dmixed = (dms * scale).astype(BF16)
        for g, win in enumerate(B_WINDOWS):
            cols = slice(g * gd, (g + 1) * gd)
            dmg = dmixed[:, cols]
            dwgrp_ref[g] += _dot_tn(pooled[g], dmg)
            dq_ref[:, cols] = _dot_nt(dmg, wgrp_ref[g]) / _window_counts(tm, win)

    return _call(
        name,
        body,
        grid=(t // tm,),
        in_specs=[_rows(tm, d), _rows(tm, d), _VM, _rows(tm, d), _halo_spec(t, tm, d, False), _VM, _VM, _VM],
        out_specs=[_rows(tm, d), _VM, _VM, _VM, _VM],
        out_shape=[
            jax.ShapeDtypeStruct((t, d), F32),
            jax.ShapeDtypeStruct((d, d), F32),
            jax.ShapeDtypeStruct((len(B_WINDOWS), gd, gd), F32),
            jax.ShapeDtypeStruct((1, d), F32),
            jax.ShapeDtypeStruct((1, d), F32),
        ],
        args=[dy, m, gpost, p, p, w_grp, scale, w_out],
        exchanges=exchanges,
    )


def _pool_bwd(name, dq, exchanges=()):
    t, d = dq.shape
    gd = d // len(B_WINDOWS)
    tm = _step_rows(t)
    n_steps = t // tm

    def body(dq_ref, halo_ref, dp_ref):
        dq_blk = dq_ref[...]
        nxt = jnp.where(pl.program_id(0) == n_steps - 1, 0.0, halo_ref[...])
        ext = jnp.concatenate([dq_blk, nxt], axis=0)
        for g, win in enumerate(B_WINDOWS):
            cols = slice(g * gd, (g + 1) * gd)
            s = ext[:, cols]
            step = 1
            while step < win:
                s = s + pltpu.roll(s, tm + HALO - step, 0)
                step *= 2
            dp_ref[0, :, cols] = (s[:tm, :] - dq_blk[:, cols] * _window_counts(tm, win)).astype(BF16)

    return _call(
        name,
        body,
        grid=(n_steps,),
        in_specs=[_rows(tm, d), _halo_spec(t, tm, d, True)],
        out_specs=[_blocks(1, tm, d)],
        out_shape=[jax.ShapeDtypeStruct((1, t, d), BF16)],
        args=[dq, dq],
        exchanges=exchanges,
    )


def _cast_into_slots(name, place, w, dtype):
    n_layers, r, c = w.shape

    def body(place_ref, w_ref, *o_refs):
        del place_ref
        for j, o_ref in enumerate(o_refs):
            @pl.when(pl.program_id(0) == j)
            def _():
                o_ref[...] = w_ref[...].astype(dtype)

    return pl.pallas_call(
        body,
        name=name,
        grid_spec=pltpu.PrefetchScalarGridSpec(
            num_scalar_prefetch=1,
            grid=(n_layers,),
            in_specs=[pl.BlockSpec((1, r, c), lambda i, place_ref: (i, 0, 0))],
            out_specs=[pl.BlockSpec((1, r, c), lambda i, place_ref: (place_ref[0], 0, 0))] * n_layers,
        ),
        out_shape=[jax.ShapeDtypeStruct((N_CHIPS, r, c), dtype)] * n_layers,
        compiler_params=_params(),
    )(place, w)


def _row_tile(r):
    return 256 if r % 256 == 0 else r


def _pair_sum(name, place, dw, recv):
    _, r, c = dw.shape
    half = r // 2
    tr = _row_tile(half)
    per = half // tr

    def body(place_ref, a_ref, b_ref, o_ref):
        del place_ref
        o_ref[...] = (a_ref[...] + b_ref[...]).astype(BF16)

    return pl.pallas_call(
        body,
        name=name,
        grid_spec=pltpu.PrefetchScalarGridSpec(
            num_scalar_prefetch=1,
            grid=(N_CHIPS, per),
            in_specs=[
                pl.BlockSpec((1, tr, c), lambda k, i, place_ref: (k, place_ref[1] * per + i, 0)),
                pl.BlockSpec((1, tr, c), lambda k, i, place_ref: (k, i, 0)),
            ],
            out_specs=pl.BlockSpec((1, tr, c), lambda k, i, place_ref: (k, i, 0)),
        ),
        out_shape=jax.ShapeDtypeStruct(recv.shape, BF16),
        compiler_params=pltpu.CompilerParams(
            dimension_semantics=("arbitrary",) * 2, vmem_limit_bytes=VMEM_LIMIT_BYTES),
    )(place, dw, recv)


def _chip_sum(name, place, mine, others, exchanges=()):
    n_layers = len(mine)
    _, half, c = mine[0].shape
    tr = _row_tile(half)
    per = half // tr

    def body(place_ref, *refs):
        del place_ref
        o_ref = refs[-1]
        for j in range(n_layers):
            @pl.when(pl.program_id(0) == j)
            def _():
                parts = refs[4 * j : 4 * j + 4]
                acc = parts[0][...].astype(F32) + parts[1][...].astype(F32)
                acc = acc + parts[2][...].astype(F32)
                o_ref[...] = acc + parts[3][...].astype(F32)

    def part(j, flip):
        return pl.BlockSpec((1, tr, c), lambda l, i, place_ref: (
            jnp.bitwise_xor(place_ref[0], flip), jnp.where(l == j, i, 0), 0))

    args = []
    for j in range(n_layers):
        args += [mine[j], others[j], others[j], others[j]]
    (total,), ex_outs = _call(
        name,
        body,
        grid=(n_layers, per),
        in_specs=[part(j, flip) for j in range(n_layers) for flip in range(N_CHIPS)],
        out_specs=[pl.BlockSpec((1, tr, c), lambda l, i, place_ref: (l, place_ref[1] * per + i, 0))],
        out_shape=[jax.ShapeDtypeStruct((n_layers, 2 * half, c), F32)],
        args=args,
        prefetch=[place],
        exchanges=exchanges,
    )
    return total, ex_outs


def _adamw(name, w, g, m, v, exchanges=()):
    n_layers, r, c = w.shape
    tr = _row_tile(r)

    def body(w_ref, g_ref, m_ref, v_ref, go_ref, d_ref, nm_ref, nv_ref):
        gv = g_ref[...]
        go_ref[...] = gv
        nm = ADAM_B1 * m_ref[...] + (1.0 - ADAM_B1) * gv
        nv = ADAM_B2 * v_ref[...] + (1.0 - ADAM_B2) * jnp.square(gv)
        m_hat = nm / (1.0 - ADAM_B1 ** ADAM_STEP)
        v_hat = nv / (1.0 - ADAM_B2 ** ADAM_STEP)
        d_ref[...] = -ADAM_LR * (m_hat / (jnp.sqrt(v_hat) + ADAM_EPS) + ADAM_WD * w_ref[...])
        nm_ref[...] = nm
        nv_ref[...] = nv

    spec = pl.BlockSpec((1, tr, c), lambda l, i: (l, i, 0))
    return _call(
        name,
        body,
        grid=(n_layers, r // tr),
        in_specs=[spec] * 4,
        out_specs=[spec] * 4,
        out_shape=[jax.ShapeDtypeStruct(w.shape, F32)] * 4,
        args=[w, g, m, v],
        exchanges=exchanges,
    )


def _all_sum_small(packed):
    m_per, n = packed.shape

    def body(x_ref, sum_ref, all_ref, send_sems, recv_sems, local_sem):
        x, y, c = _position()
        me, sibling = (x, y, c), (x, y, 1 - c)
        chips = _other_chips(x, y)

        def rows(px, py, pc):
            return all_ref.at[pl.ds((4 * px + 2 * py + pc) * m_per, m_per), :]

        def copy(k, block, to, src=None):
            return pltpu.make_async_remote_copy(
                src_ref=rows(*block) if src is None else src, dst_ref=rows(*block),
                send_sem=send_sems.at[k], recv_sem=recv_sems.at[k], device_id=to, device_id_type=MESH)

        mine = pltpu.make_async_copy(x_ref, rows(*me), local_sem)
        mine.start()
        first = [copy(0, me, sibling, src=x_ref)]
        first += [copy(1 + j, me, (*chip, c), src=x_ref) for j, chip in enumerate(chips)]
        for cp in first:
            cp.start()
        passed = [copy(4 + j, (*chip, c), sibling) for j, chip in enumerate(chips)]
        for j, chip in enumerate(chips):
            copy(1 + j, (*chip, c), me).wait_recv()
            passed[j].start()
        copy(0, sibling, me).wait_recv()
        for j, chip in enumerate(chips):
            copy(4 + j, (*chip, 1 - c), me).wait_recv()
        for cp in first + passed:
            cp.wait_send()
        mine.wait()
        acc = all_ref[0:m_per, :]
        for k in range(1, N_DEV):
            acc = acc + all_ref[k * m_per : (k + 1) * m_per, :]
        sum_ref[...] = acc

    return pl.pallas_call(
        body,
        name="all_sum_small",
        in_specs=[_VM],
        out_specs=_VM,
        out_shape=jax.ShapeDtypeStruct((m_per, n), F32),
        scratch_shapes=[
            pltpu.VMEM((N_DEV * m_per, n), F32),
            pltpu.SemaphoreType.DMA((7,)),
            pltpu.SemaphoreType.DMA((7,)),
            pltpu.SemaphoreType.DMA,
        ],
        compiler_params=pltpu.CompilerParams(vmem_limit_bytes=VMEM_LIMIT_BYTES),
    )(packed)


SHARDED = ("a_w_in", "a_w_out", "b_w_in", "b_w_grp", "b_scale", "b_w_out", "ffn_w_gate", "ffn_w_up", "ffn_w_down")
SMALL = ("a_ln_g", "a_ln_b", "a_w_s", "a_b_s", "mix_pre_g", "mix_post_g", "ffn_pre_g", "ffn_post_g")
WEIGHTS = ("a_w_in", "a_ln_g", "a_ln_b", "a_w_s", "a_b_s", "a_w_out", "b_w_in", "b_w_grp", "b_scale", "b_w_out",
           "mix_pre_g", "mix_post_g", "ffn_pre_g", "ffn_post_g", "ffn_w_gate", "ffn_w_up", "ffn_w_down")


TRANSPOSED = ("ffn_w_gate", "ffn_w_up")


def _as_layers(name, a):
    if name in TRANSPOSED:
        return jnp.swapaxes(a, 1, 2)
    if a.ndim == 2:
        return a.reshape(a.shape[0], 1, a.shape[1])
    return a.reshape(a.shape[0], -1, a.shape[-1])


def _from_layers(name, a, shape):
    if name in TRANSPOSED:
        return jnp.swapaxes(a, 1, 2)
    return a.reshape(shape)


def _pack_small(parts):
    return jnp.concatenate([p.reshape(-1, 128) for p in parts], axis=0)


def _unpack_small(packed, like):
    out, row = [], 0
    for ref in like:
        rows = ref.size // 128
        out.append(packed[row : row + rows].reshape(ref.shape))
        row += rows
    return out


def kernel(x, a_w_in, a_ln_g, a_ln_b, a_w_s, a_b_s, a_w_out, b_w_in, b_w_grp, b_scale, b_w_out, mix_pre_g, mix_post_g, ffn_pre_g, ffn_post_g, ffn_w_gate, ffn_w_up, ffn_w_down, loss_target, m_a_w_in, m_a_ln_g, m_a_ln_b, m_a_w_s, m_a_b_s, m_a_w_out, m_b_w_in, m_b_w_grp, m_b_scale, m_b_w_out, m_mix_pre_g, m_mix_post_g, m_ffn_pre_g, m_ffn_post_g, m_ffn_w_gate, m_ffn_w_up, m_ffn_w_down, v_a_w_in, v_a_ln_g, v_a_ln_b, v_a_w_s, v_a_b_s, v_a_w_out, v_b_w_in, v_b_w_grp, v_b_scale, v_b_w_out, v_mix_pre_g, v_mix_post_g, v_ffn_pre_g, v_ffn_post_g, v_ffn_w_gate, v_ffn_w_up, v_ffn_w_down):
    weights = dict(a_w_in=a_w_in, a_ln_g=a_ln_g, a_ln_b=a_ln_b, a_w_s=a_w_s, a_b_s=a_b_s, a_w_out=a_w_out,
                   b_w_in=b_w_in, b_w_grp=b_w_grp, b_scale=b_scale, b_w_out=b_w_out, mix_pre_g=mix_pre_g,
                   mix_post_g=mix_post_g, ffn_pre_g=ffn_pre_g, ffn_post_g=ffn_post_g, ffn_w_gate=ffn_w_gate,
                   ffn_w_up=ffn_w_up, ffn_w_down=ffn_w_down)
    mom1 = dict(a_w_in=m_a_w_in, a_ln_g=m_a_ln_g, a_ln_b=m_a_ln_b, a_w_s=m_a_w_s, a_b_s=m_a_b_s, a_w_out=m_a_w_out,
                b_w_in=m_b_w_in, b_w_grp=m_b_w_grp, b_scale=m_b_scale, b_w_out=m_b_w_out, mix_pre_g=m_mix_pre_g,
                mix_post_g=m_mix_post_g, ffn_pre_g=m_ffn_pre_g, ffn_post_g=m_ffn_post_g, ffn_w_gate=m_ffn_w_gate,
                ffn_w_up=m_ffn_w_up, ffn_w_down=m_ffn_w_down)
    mom2 = dict(a_w_in=v_a_w_in, a_ln_g=v_a_ln_g, a_ln_b=v_a_ln_b, a_w_s=v_a_w_s, a_b_s=v_a_b_s, a_w_out=v_a_w_out,
                b_w_in=v_b_w_in, b_w_grp=v_b_w_grp, b_scale=v_b_scale, b_w_out=v_b_w_out, mix_pre_g=v_mix_pre_g,
                mix_post_g=v_mix_post_g, ffn_pre_g=v_ffn_pre_g, ffn_post_g=v_ffn_post_g, ffn_w_gate=v_ffn_w_gate,
                ffn_w_up=v_ffn_w_up, ffn_w_down=v_ffn_w_down)

    t, d = x.shape[1], x.shape[2]
    depth = mix_pre_g.shape[0]
    gd_b = d // len(B_WINDOWS)
    xs = x.reshape(t, d)
    target = loss_target.reshape(t, d)

    chip = 2 * lax.axis_index("x") + lax.axis_index("y")
    place = jnp.stack([chip, lax.axis_index("c")]).astype(jnp.int32)
    bufs = {name: list(_cast_into_slots("cast_" + name, place, _as_layers(name, weights[name]),
                                        F32 if name == "b_scale" else BF16)) for name in SHARDED}

    def gain(name, i):
        return weights[name][i].reshape(1, d)

    def weight_keys(i):
        j = i // 2
        mixer = [("a_w_in", j), ("a_w_out", j)] if i % 2 == 0 else [("b_w_in", j), ("b_w_grp", j), ("b_w_out", j)]
        return mixer, [("ffn_w_gate", i), ("ffn_w_up", i), ("ffn_w_down", i)]

    def gather(keys):
        return _GatherWeights([bufs[n][j] for n, j in keys],
                              whole=[k for k, (n, _) in enumerate(keys) if n == "b_scale"])

    def gathered(keys, outs):
        for (n, j), buf in zip(keys, outs):
            bufs[n][j] = buf

    first = weight_keys(0)[0] + [("b_scale", j) for j in range(b_scale.shape[0])]
    gathered(first, _exchange("gather_first", [gather(first)])[0])
    saved = []
    cur = xs
    for i in range(depth):
        j = i // 2
        mixer_next, ffn_next = weight_keys(i + 1) if i + 1 < depth else ([], [])
        ffn_keys = weight_keys(i)[1]
        after_ffn = mixer_next if (i + 1) % 2 == 0 else mixer_next + ffn_next
        if i % 2 == 0:
            w_in = bufs["a_w_in"][j]
            q = w_in.shape[2]
            w_out = bufs["a_w_out"][j].reshape(2, q, d)
            ln_g = a_ln_g[j].reshape(2, q)
            ln_b = a_ln_b[j].reshape(2, q)
            b_t = jnp.transpose(a_b_s[j])
            (nxt, act, slope, m), (got,) = _mix_a_fwd(
                f"mix_a_fwd{j}", cur, gain("mix_pre_g", i), gain("mix_post_g", i), w_in, ln_g, ln_b, a_w_s[j], b_t,
                w_out, exchanges=[gather(ffn_keys)])
            gathered(ffn_keys, got)
            mix_saved = dict(x=cur, act=act, slope=slope, m=m, w_in=w_in, w_out=w_out, ln_g=ln_g, ln_b=ln_b, b_t=b_t)
        else:
            w_in = bufs["b_w_in"][j].reshape(d, d)
            w_out = bufs["b_w_out"][j].reshape(d, d)
            w_grp = jnp.transpose(bufs["b_w_grp"][j].reshape(N_CHIPS, len(B_WINDOWS), gd_b // N_CHIPS, gd_b),
                                  (1, 0, 2, 3)).reshape(len(B_WINDOWS), gd_b, gd_b)
            scale = bufs["b_scale"][j].reshape(1, d)
            (p,), _ = _norm_matmul(f"mix_b_in{j}", cur, gain("mix_pre_g", i), w_in)
            (nxt, m), _ = _mix_b_fwd(f"mix_b_fwd{j}", cur, p, gain("mix_post_g", i), w_grp, scale, w_out)
            mix_saved = dict(x=cur, p=p, m=m, w_in=w_in, w_out=w_out, w_grp=w_grp, scale=scale)
        cur = nxt
        wg, wu, wd = (bufs[n][k].reshape(1, -1, d) for n, k in ffn_keys)
        (nxt, a, dup, dgate, f, *sq), (got,) = _ffn_fwd(
            f"ffn_fwd{i}", cur, gain("ffn_pre_g", i), gain("ffn_post_g", i), wg, wu, wd,
            target=target if i == depth - 1 else None, exchanges=[gather(after_ffn)])
        gathered(after_ffn, got)
        saved.append((mix_saved, dict(x=cur, a=a, dup=dup, dgate=dgate, f=f, wg=wg, wu=wu, wd=wd)))
        cur = nxt

    dcur = cur
    loss = lax.psum(0.5 * sq[0][0, 0] / d, ("x", "y", "c"))

    grads = {name: [None] * weights[name].shape[0] for name in WEIGHTS}
    state = dict(to_sibling=[], to_chips=[])
    pair, from_chips = {}, {}

    def exchanges_due():
        return [_ToSibling([a for _, _, a in state["to_sibling"]]), _ToChips([a for _, _, a in state["to_chips"]])]

    def exchanged(outs):
        from_sibling, arrived = outs
        for (n, k, _), got in zip(state["to_chips"], arrived):
            from_chips[n, k] = got
        state["to_chips"] = []
        for (n, k, dw), got in zip(state["to_sibling"], from_sibling):
            pair[n, k] = _pair_sum(f"pair_sum_{n}{k}", place, dw, got)
            state["to_chips"].append((n, k, pair[n, k]))
        state["to_sibling"] = []

    def made(name, k, dw):
        grads[name][k] = dw
        state["to_sibling"].append((name, k, dw))

    for i in reversed(range(depth)):
        j = i // 2
        mix_saved, ffn_saved = saved[i]
        s = ffn_saved
        (dg, du, dwd, dgain), outs = _ffn_bwd_hidden(
            f"ffn_bwd_hidden{i}", dcur, s["f"], gain("ffn_post_g", i), s["a"], s["dup"], s["dgate"], s["wd"],
            exchanges=exchanges_due())
        exchanged(outs)
        grads["ffn_post_g"][i] = dgain
        made("ffn_w_down", i, dwd.reshape(N_CHIPS, -1, d))
        (dcur, dwg, dwu, dgain), outs = _bwd_in(
            f"ffn_bwd_in{i}", dcur, s["x"], gain("ffn_pre_g", i), [dg, du], [s["wg"], s["wu"]], transposed=True,
            exchanges=exchanges_due())
        exchanged(outs)
        grads["ffn_pre_g"][i] = dgain
        made("ffn_w_gate", i, dwg.reshape(N_CHIPS, -1, d))
        made("ffn_w_up", i, dwu.reshape(N_CHIPS, -1, d))
        s = mix_saved
        if i % 2 == 0:
            (dz, dwout, dws, dbacc, dlng, dlnb, dgain), outs = _mix_a_bwd_hidden(
                f"mix_a_bwd_hidden{j}", dcur, s["m"], gain("mix_post_g", i), s["act"], s["slope"], s["ln_g"], s["ln_b"],
                a_w_s[j], s["b_t"], s["w_out"], exchanges=exchanges_due())
            exchanged(outs)
            grads["a_w_s"][j] = dws
            grads["a_b_s"][j] = jnp.transpose(dbacc)
            grads["a_ln_g"][j] = dlng.reshape(-1)
            grads["a_ln_b"][j] = dlnb.reshape(-1)
            grads["mix_post_g"][i] = dgain
            made("a_w_out", j, dwout.reshape(N_CHIPS, -1, d))
            (dcur, dwin, dgain), outs = _bwd_in(
                f"mix_a_bwd_in{j}", dcur, s["x"], gain("mix_pre_g", i), [dz], [s["w_in"]], exchanges=exchanges_due())
            exchanged(outs)
            grads["mix_pre_g"][i] = dgain
            made("a_w_in", j, dwin)
        else:
            (dq, dwout, dwgrp, dscale, dgain), outs = _mix_b_bwd_hidden(
                f"mix_b_bwd_hidden{j}", dcur, s["m"], gain("mix_post_g", i), s["p"], s["w_grp"], s["scale"],
                s["w_out"], exchanges=exchanges_due())
            exchanged(outs)
            grads["b_scale"][j] = dscale
            grads["mix_post_g"][i] = dgain
            made("b_w_out", j, dwout.reshape(N_CHIPS, -1, d))
            made("b_w_grp", j, jnp.transpose(
                dwgrp.reshape(len(B_WINDOWS), N_CHIPS, gd_b // N_CHIPS, gd_b), (1, 0, 2, 3)).reshape(N_CHIPS, -1, gd_b))
            (dp,), _ = _pool_bwd(f"pool_bwd{j}", dq)
            (dcur, dwin, dgain), outs = _bwd_in(
                f"mix_b_bwd_in{j}", dcur, s["x"], gain("mix_pre_g", i), [dp], [s["w_in"].reshape(1, d, d)],
                exchanges=exchanges_due())
            exchanged(outs)
            grads["mix_pre_g"][i] = dgain
            made("b_w_in", j, dwin.reshape(N_CHIPS, -1, d))
    grad_x = dcur.reshape(x.shape)

    order = ["ffn_w_down", "ffn_w_gate", "ffn_w_up", "b_w_in", "b_w_grp", "b_w_out", "a_w_out", "a_w_in"]
    to_swap, reduced = [], {}

    def carried():
        return exchanges_due() + [_SwapHalves([a for _, a in to_swap])]

    def landed(outs):
        exchanged(outs[:2])
        for (n, _), got in zip(to_swap, outs[2]):
            reduced[n] = got
        to_swap.clear()

    for name in order:
        layers = range(weights[name].shape[0])
        total, outs = _chip_sum("chip_sum_" + name, place, [pair[name, k] for k in layers],
                                [from_chips[name, k] for k in layers], exchanges=carried())
        landed(outs)
        to_swap.append((name, total))

    out_g, out_d, out_m, out_v = {}, {}, {}, {}
    for name in order:
        shape = weights[name].shape
        results, outs = _adamw("adamw_" + name, _as_layers(name, weights[name]), reduced[name],
                               _as_layers(name, mom1[name]), _as_layers(name, mom2[name]), exchanges=carried())
        landed(outs)
        out_g[name], out_d[name], out_m[name], out_v[name] = (_from_layers(name, a, shape) for a in results)

    small_grads = [jnp.stack([g.reshape(weights[name].shape[1:]) for g in grads[name]], axis=0) for name in SMALL]
    scale_grad = jnp.concatenate(grads["b_scale"], axis=0)
    summed = _all_sum_small(_pack_small(small_grads + [scale_grad]))
    small_rows = summed.shape[0] - scale_grad.size // 128
    scale_sum = summed[small_rows:].reshape(scale_grad.shape)
    scale_mine = lax.dynamic_slice_in_dim(scale_sum, chip * b_scale.shape[1], b_scale.shape[1], axis=1)[:, None, :]
    summed = summed[:small_rows]
    results, _ = _adamw("adamw_b_scale", _as_layers("b_scale", b_scale), scale_mine,
                        _as_layers("b_scale", m_b_scale), _as_layers("b_scale", v_b_scale))
    out_g["b_scale"], out_d["b_scale"], out_m["b_scale"], out_v["b_scale"] = (a.reshape(b_scale.shape) for a in results)
    small_like = [weights[name] for name in SMALL]
    packs = [_pack_small([src[name] for name in SMALL]).reshape(1, -1, 128) for src in (weights, mom1, mom2)]
    results, _ = _adamw("adamw_small", packs[0], summed.reshape(1, -1, 128), packs[1], packs[2])
    for dst, packed in zip((out_g, out_d, out_m, out_v), (a[0] for a in results)):
        for name, val in zip(SMALL, _unpack_small(packed, small_like)):
            dst[name] = val

    return (loss, grad_x, *[out_g[n] for n in WEIGHTS], *[out_d[n] for n in WEIGHTS],
            *[out_m[n] for n in WEIGHTS], *[out_v[n] for n in WEIGHTS])
```

```python
import functools
import math

import jax
import jax.numpy as jnp
from jax import lax
from jax.experimental import pallas as pl
from jax.experimental.pallas import tpu as pltpu

F32 = jnp.float32
BF16 = jnp.bfloat16
MESH = pl.DeviceIdType.MESH

EPS = 1e-6
CHUNK = 128
A_GROUPS = 8
B_WINDOWS = (2, 4, 8, 16)
HALO = 16
N_CHIPS = 4
N_DEV = 8

ADAM_LR = 0.001
ADAM_B1 = 0.9
ADAM_B2 = 0.999
ADAM_EPS = 1e-08
ADAM_WD = 0.01
ADAM_STEP = 10

VMEM_LIMIT_BYTES = 60 * 1024 * 1024
INV_SQRT2 = 1.0 / math.sqrt(2.0)
INV_SQRT_2PI = 1.0 / math.sqrt(2.0 * math.pi)

_ANY = pl.BlockSpec(memory_space=pl.ANY)
_VM = pl.BlockSpec(memory_space=pltpu.VMEM)


def _params():
    return pltpu.CompilerParams(dimension_semantics=("arbitrary",), vmem_limit_bytes=VMEM_LIMIT_BYTES)


def _token_block(t):
    return 256 if t >= 1024 else 128


def _step_rows(t):
    return 2 * _token_block(t)


def _rows(tm, d):
    return pl.BlockSpec((tm, d), lambda i: (i, 0))


def _blocks(nb, tm, bw):
    return pl.BlockSpec((nb, tm, bw), lambda i: (0, i, 0))


def _dot(a, b):
    return lax.dot_general(a, b, (((1,), (0,)), ((), ())), preferred_element_type=F32)


def _dot_nt(a, b):
    return lax.dot_general(a, b, (((1,), (1,)), ((), ())), preferred_element_type=F32)


def _dot_tn(a, b):
    return lax.dot_general(a, b, (((0,), (0,)), ((), ())), preferred_element_type=F32)


def _rms(x, g):
    return x * lax.rsqrt(jnp.mean(x * x, axis=-1, keepdims=True) + EPS) * g


def _rms_bwd(dy, x, g):
    r = lax.rsqrt(jnp.mean(x * x, axis=-1, keepdims=True) + EPS)
    n = x * r
    dn = dy * g
    dx = r * (dn - n * jnp.mean(dn * n, axis=-1, keepdims=True))
    return dx, jnp.sum(dy * n, axis=0, keepdims=True)


def _gelu_and_grad(x):
    cdf = 0.5 * (1.0 + lax.erf(x * INV_SQRT2))
    return x * cdf, cdf + x * (jnp.exp(-0.5 * x * x) * INV_SQRT_2PI)


def _position():
    return lax.axis_index("x"), lax.axis_index("y"), lax.axis_index("c")


def _other_chips(x, y):
    return [(1 - x, y), (x, 1 - y), (1 - x, 1 - y)]


class _GatherWeights:
    def __init__(self, bufs, whole=()):
        self.inputs = list(bufs)
        self.out_shapes = [jax.ShapeDtypeStruct(b.shape, b.dtype) for b in bufs]
        self.aliases = {w: w for w in range(len(bufs))}
        self.n_sems = 6 * len(bufs)
        self.whole = frozenset(whole)

    def _part(self, outs, w, slot, core):
        if w in self.whole:
            return outs[w].at[slot]
        half = outs[w].shape[1] // 2
        return outs[w].at[slot, pl.ds(core * half, half)]

    def _ici(self, outs, send, recv, w, j, slot):
        x, y, c = _position()
        px, py = _other_chips(x, y)[j]
        part = self._part(outs, w, slot, c)
        return pltpu.make_async_remote_copy(
            src_ref=part, dst_ref=part, send_sem=send.at[6 * w + j], recv_sem=recv.at[6 * w + j],
            device_id=(px, py, c), device_id_type=MESH)

    def _d2d(self, outs, send, recv, w, j, slot, core):
        x, y, c = _position()
        part = self._part(outs, w, slot, core)
        return pltpu.make_async_remote_copy(
            src_ref=part, dst_ref=part, send_sem=send.at[6 * w + 3 + j], recv_sem=recv.at[6 * w + 3 + j],
            device_id=(x, y, 1 - c), device_id_type=MESH)

    def start(self, ins, outs, send, recv):
        x, y, _ = _position()
        for w in range(len(outs)):
            for j in range(3):
                self._ici(outs, send, recv, w, j, 2 * x + y).start()

    def finish(self, ins, outs, send, recv):
        x, y, c = _position()
        slots = [2 * px + py for px, py in _other_chips(x, y)]
        for w in range(len(outs)):
            for j, slot in enumerate(slots):
                self._ici(outs, send, recv, w, j, slot).wait_recv()
                if w not in self.whole:
                    self._d2d(outs, send, recv, w, j, slot, c).start()
        for w in range(len(outs)):
            for j, slot in enumerate(slots):
                if w not in self.whole:
                    self._d2d(outs, send, recv, w, j, slot, 1 - c).wait_recv()
        for w in range(len(outs)):
            for j, slot in enumerate(slots):
                self._ici(outs, send, recv, w, j, 2 * x + y).wait_send()
                if w not in self.whole:
                    self._d2d(outs, send, recv, w, j, slot, c).wait_send()


class _ToSibling:
    def __init__(self, grads):
        self.inputs = list(grads)
        self.out_shapes = [jax.ShapeDtypeStruct((g.shape[0], g.shape[1] // 2, g.shape[2]), g.dtype) for g in grads]
        self.aliases = {}
        self.n_sems = len(grads)

    def _copy(self, ins, outs, send, recv, w):
        x, y, c = _position()
        half = ins[w].shape[1] // 2
        return pltpu.make_async_remote_copy(
            src_ref=ins[w].at[:, pl.ds((1 - c) * half, half)], dst_ref=outs[w],
            send_sem=send.at[w], recv_sem=recv.at[w], device_id=(x, y, 1 - c), device_id_type=MESH)

    def start(self, ins, outs, send, recv):
        for w in range(len(ins)):
            self._copy(ins, outs, send, recv, w).start()

    def finish(self, ins, outs, send, recv):
        for w in range(len(ins)):
            self._copy(ins, outs, send, recv, w).wait_recv()
        for w in range(len(ins)):
            self._copy(ins, outs, send, recv, w).wait_send()


class _ToChips:
    def __init__(self, parts):
        self.inputs = list(parts)
        self.out_shapes = [jax.ShapeDtypeStruct(p.shape, p.dtype) for p in parts]
        self.aliases = {}
        self.n_sems = 3 * len(parts)

    def _copy(self, ins, outs, send, recv, w, j, outbound):
        x, y, c = _position()
        px, py = _other_chips(x, y)[j]
        me, peer = 2 * x + y, 2 * px + py
        src_slot, dst_slot = (peer, me) if outbound else (me, peer)
        return pltpu.make_async_remote_copy(
            src_ref=ins[w].at[src_slot], dst_ref=outs[w].at[dst_slot],
            send_sem=send.at[3 * w + j], recv_sem=recv.at[3 * w + j], device_id=(px, py, c), device_id_type=MESH)

    def start(self, ins, outs, send, recv):
        for w in range(len(ins)):
            for j in range(3):
                self._copy(ins, outs, send, recv, w, j, True).start()

    def finish(self, ins, outs, send, recv):
        for w in range(len(ins)):
            for j in range(3):
                self._copy(ins, outs, send, recv, w, j, False).wait_recv()
        for w in range(len(ins)):
            for j in range(3):
                self._copy(ins, outs, send, recv, w, j, True).wait_send()


class _SwapHalves:
    def __init__(self, bufs):
        self.inputs = list(bufs)
        self.out_shapes = [jax.ShapeDtypeStruct(b.shape, b.dtype) for b in bufs]
        self.aliases = {w: w for w in range(len(bufs))}
        self.n_sems = len(bufs)

    def _copy(self, outs, send, recv, w, core):
        x, y, c = _position()
        half = outs[w].shape[1] // 2
        rows = outs[w].at[:, pl.ds(core * half, half)]
        return pltpu.make_async_remote_copy(
            src_ref=rows, dst_ref=rows, send_sem=send.at[w], recv_sem=recv.at[w],
            device_id=(x, y, 1 - c), device_id_type=MESH)

    def start(self, ins, outs, send, recv):
        c = lax.axis_index("c")
        for w in range(len(outs)):
            self._copy(outs, send, recv, w, c).start()

    def finish(self, ins, outs, send, recv):
        c = lax.axis_index("c")
        for w in range(len(outs)):
            self._copy(outs, send, recv, w, 1 - c).wait_recv()
        for w in range(len(outs)):
            self._copy(outs, send, recv, w, c).wait_send()


def _call(name, body, *, grid, in_specs, out_specs, out_shape, args, scratch_shapes=(), prefetch=(), exchanges=()):
    given = list(exchanges)
    exchanges = [e for e in given if e.inputs]
    n_pre, n_in, n_out, n_scr = len(prefetch), len(args), len(out_shape), len(scratch_shapes)
    ex_in = [a for e in exchanges for a in e.inputs]
    ex_out = [s for e in exchanges for s in e.out_shapes]
    aliases = {}
    at_in, at_out = n_pre + n_in, n_out
    for e in exchanges:
        for i, o in e.aliases.items():
            aliases[at_in + i] = at_out + o
        at_in += len(e.inputs)
        at_out += len(e.out_shapes)

    def at_step(last):
        hit = None
        for axis, n in enumerate(grid):
            here = pl.program_id(axis) == (n - 1 if last else 0)
            hit = here if hit is None else jnp.logical_and(hit, here)
        return hit

    def fused(*refs):
        pre, refs = refs[:n_pre], refs[n_pre:]
        body_in, refs = refs[:n_in], refs[n_in:]
        ex_in_refs, refs = refs[: len(ex_in)], refs[len(ex_in) :]
        body_out, refs = refs[:n_out], refs[n_out:]
        ex_out_refs, refs = refs[: len(ex_out)], refs[len(ex_out) :]
        body_scr, sems = refs[:n_scr], refs[n_scr:]

        def each(stage):
            a = b = 0
            for n, e in enumerate(exchanges):
                ins, outs = ex_in_refs[a : a + len(e.inputs)], ex_out_refs[b : b + len(e.out_shapes)]
                getattr(e, stage)(ins, outs, sems[2 * n], sems[2 * n + 1])
                a += len(e.inputs)
                b += len(e.out_shapes)

        if exchanges:
            @pl.when(at_step(False))
            def _():
                each("start")

        if body is not None:
            body(*pre, *body_in, *body_out, *body_scr)

        if exchanges:
            @pl.when(at_step(True))
            def _():
                each("finish")

    outs = pl.pallas_call(
        fused,
        name=name,
        grid_spec=pltpu.PrefetchScalarGridSpec(
            num_scalar_prefetch=n_pre,
            grid=grid,
            in_specs=list(in_specs) + [_ANY] * len(ex_in),
            out_specs=list(out_specs) + [_ANY] * len(ex_out),
            scratch_shapes=list(scratch_shapes)
            + [pltpu.SemaphoreType.DMA((e.n_sems,)) for e in exchanges for _ in range(2)],
        ),
        out_shape=list(out_shape) + ex_out,
        input_output_aliases=aliases,
        compiler_params=pltpu.CompilerParams(
            dimension_semantics=("arbitrary",) * len(grid), vmem_limit_bytes=VMEM_LIMIT_BYTES),
    )(*prefetch, *args, *ex_in)
    body_outs, rest = list(outs[:n_out]), list(outs[n_out:])
    ex_outs = []
    for e in given:
        n_e = len(e.out_shapes) if e.inputs else 0
        ex_outs.append(rest[:n_e])
        rest = rest[n_e:]
    return body_outs, ex_outs


def _exchange(name, exchanges):
    return _call(name, None, grid=(1,), in_specs=[], out_specs=[], out_shape=[], args=[], exchanges=exchanges)[1]


def _ffn_fwd(name, x, gpre, gpost, wg, wu, wd, target=None, exchanges=()):
    t, d = x.shape
    nb, fs, _ = wg.shape
    sub = _token_block(t)
    tm = _step_rows(t)
    with_loss = target is not None

    def body(x_ref, gpre_ref, gpost_ref, wg_ref, wu_ref, wd_ref, *refs):
        if with_loss:
            t_ref, o_ref, a_ref, dup_ref, dgate_ref, f_ref, sq_ref = refs

            @pl.when(pl.program_id(0) == 0)
            def _():
                sq_ref[...] = jnp.zeros_like(sq_ref)
        else:
            o_ref, a_ref, dup_ref, dgate_ref, f_ref = refs
        for h in range(tm // sub):
            rows = slice(h * sub, (h + 1) * sub)
            xv = x_ref[rows, :]
            hb = _rms(xv, gpre_ref[...]).astype(BF16)
            f = jnp.zeros((sub, d), F32)
            for k in range(nb):
                g = _dot_nt(hb, wg_ref[k])
                u = _dot_nt(hb, wu_ref[k])
                s = jax.nn.sigmoid(g)
                sg = g * s
                a = (sg * u).astype(BF16)
                a_ref[k, rows, :] = a
                dup_ref[k, rows, :] = sg.astype(BF16)
                dgate_ref[k, rows, :] = (u * (s * (1.0 + g * (1.0 - s)))).astype(BF16)
                f = f + _dot(a, wd_ref[k])
            f_ref[rows, :] = f
            y = xv + _rms(f, gpost_ref[...])
            if with_loss:
                err = y - t_ref[rows, :]
                o_ref[rows, :] = err / d
                sq_ref[...] += jnp.sum(err * err)
            else:
                o_ref[rows, :] = y

    return _call(
        name,
        body,
        grid=(t // tm,),
        in_specs=[_rows(tm, d), _VM, _VM, _VM, _VM, _VM] + [_rows(tm, d)] * with_loss,
        out_specs=[_rows(tm, d)] + [_blocks(nb, tm, fs)] * 3 + [_rows(tm, d)] + [_VM] * with_loss,
        out_shape=[jax.ShapeDtypeStruct((t, d), F32)]
        + [jax.ShapeDtypeStruct((nb, t, fs), BF16)] * 3
        + [jax.ShapeDtypeStruct((t, d), F32)]
        + [jax.ShapeDtypeStruct((8, 128), F32)] * with_loss,
        args=[x, gpre, gpost, wg, wu, wd] + [target] * with_loss,
        exchanges=exchanges,
    )


def _ffn_bwd_hidden(name, dy, f, gpost, a, dup, dgate, wd, exchanges=()):
    t, d = dy.shape
    nb, fs, _ = wd.shape
    tm = _step_rows(t)

    def body(dy_ref, f_ref, gpost_ref, a_ref, dup_ref, dgate_ref, wd_ref, dg_ref, du_ref, dwd_ref, dgain_ref):
        @pl.when(pl.program_id(0) == 0)
        def _():
            dwd_ref[...] = jnp.zeros_like(dwd_ref)
            dgain_ref[...] = jnp.zeros_like(dgain_ref)

        df, dgain = _rms_bwd(dy_ref[...], f_ref[...], gpost_ref[...])
        dgain_ref[...] += dgain
        dfb = df.astype(BF16)
        for k in range(nb):
            da = _dot_nt(dfb, wd_ref[k])
            dwd_ref[k] += _dot_tn(a_ref[k], dfb)
            du_ref[k] = (da * dup_ref[k].astype(F32)).astype(BF16)
            dg_ref[k] = (da * dgate_ref[k].astype(F32)).astype(BF16)

    return _call(
        name,
        body,
        grid=(t // tm,),
        in_specs=[_rows(tm, d), _rows(tm, d), _VM] + [_blocks(nb, tm, fs)] * 3 + [_VM],
        out_specs=[_blocks(nb, tm, fs), _blocks(nb, tm, fs), _VM, _VM],
        out_shape=[
            jax.ShapeDtypeStruct((nb, t, fs), BF16),
            jax.ShapeDtypeStruct((nb, t, fs), BF16),
            jax.ShapeDtypeStruct((nb, fs, d), F32),
            jax.ShapeDtypeStruct((1, d), F32),
        ],
        args=[dy, f, gpost, a, dup, dgate, wd],
        exchanges=exchanges,
    )


def _bwd_in(name, dres, x, gpre, dzs, ws, transposed=False, exchanges=()):
    t, d = x.shape
    n = len(ws)
    resident = sum(6 * w.size for w in ws)
    tm = _step_rows(t) if resident <= VMEM_LIMIT_BYTES // 2 else _token_block(t)
    widths = [w.shape[1] if transposed else w.shape[2] for w in ws]

    def body(*refs):
        dres_ref, x_ref, gpre_ref = refs[:3]
        dz_refs = refs[3 : 3 + n]
        w_refs = refs[3 + n : 3 + 2 * n]
        dx_ref = refs[3 + 2 * n]
        dw_refs = refs[4 + 2 * n : 4 + 3 * n]
        dgain_ref = refs[4 + 3 * n]

        @pl.when(pl.program_id(0) == 0)
        def _():
            for dw_ref in dw_refs:
                dw_ref[...] = jnp.zeros_like(dw_ref)
            dgain_ref[...] = jnp.zeros_like(dgain_ref)

        xv = x_ref[...]
        gain = gpre_ref[...]
        hb = _rms(xv, gain).astype(BF16)
        dh = jnp.zeros((tm, d), F32)
        for dz_ref, w_ref, dw_ref in zip(dz_refs, w_refs, dw_refs):
            for k in range(w_ref.shape[0]):
                dz = dz_ref[k]
                if transposed:
                    dh = dh + _dot(dz, w_ref[k])
                    dw_ref[k] += _dot_tn(dz, hb)
                else:
                    dh = dh + _dot_nt(dz, w_ref[k])
                    dw_ref[k] += _dot_tn(hb, dz)
        dx, dgain = _rms_bwd(dh, xv, gain)
        dx_ref[...] = dres_ref[...] + dx
        dgain_ref[...] += dgain

    return _call(
        name,
        body,
        grid=(t // tm,),
        in_specs=[_rows(tm, d), _rows(tm, d), _VM]
        + [_blocks(w.shape[0], tm, bw) for w, bw in zip(ws, widths)]
        + [_VM] * n,
        out_specs=[_rows(tm, d)] + [_VM] * n + [_VM],
        out_shape=[jax.ShapeDtypeStruct((t, d), F32)]
        + [jax.ShapeDtypeStruct(w.shape, F32) for w in ws]
        + [jax.ShapeDtypeStruct((1, d), F32)],
        args=[dres, x, gpre, *dzs, *ws],
        exchanges=exchanges,
    )


def _causal_weights(ws_ref):
    row = lax.broadcasted_iota(jnp.int32, (CHUNK, CHUNK), 0)
    col = lax.broadcasted_iota(jnp.int32, (CHUNK, CHUNK), 1)
    return [jnp.where(row >= col, ws_ref[g], 0.0).astype(BF16) for g in range(A_GROUPS)]


def _layernorm_halves(v0, v1):
    width = v0.shape[-1] + v1.shape[-1]
    mu = (jnp.sum(v0, axis=-1, keepdims=True) + jnp.sum(v1, axis=-1, keepdims=True)) / width
    c0 = v0 - mu
    c1 = v1 - mu
    var = (jnp.sum(c0 * c0, axis=-1, keepdims=True) + jnp.sum(c1 * c1, axis=-1, keepdims=True)) / width
    rstd = lax.rsqrt(var + EPS)
    return c0 * rstd, c1 * rstd, rstd


def _spatial_gate(sv_ref, wtril, vl, bt_ref, half, tm, gd):
    for gg in range(A_GROUPS // 2):
        g = half * (A_GROUPS // 2) + gg
        bias = bt_ref[:, g : g + 1]
        for n in range(tm // CHUNK):
            blk = vl[n * CHUNK : (n + 1) * CHUNK, gg * gd : (gg + 1) * gd]
            sv_ref[n * CHUNK : (n + 1) * CHUNK, gg * gd : (gg + 1) * gd] = _dot(wtril[g], blk) + bias


def _mix_a_fwd(name, x, gpre, gpost, w_in, ln_g, ln_b, w_s, b_t, w_out, exchanges=()):
    t, d = x.shape
    _, _, q = w_in.shape
    gd = 2 * q // A_GROUPS
    tm = _step_rows(t)

    def body(x_ref, gpre_ref, gpost_ref, win_ref, lng_ref, lnb_ref, ws_ref, bt_ref, wout_ref,
             o_ref, z_ref, dz_ref, m_ref, sv_ref):
        xv = x_ref[...]
        hb = _rms(xv, gpre_ref[...]).astype(BF16)
        z = []
        for k in range(4):
            act, slope = _gelu_and_grad(_dot(hb, win_ref[k]))
            z_ref[k] = act.astype(BF16)
            dz_ref[k] = slope.astype(BF16)
            z.append(act)
        vh0, vh1, _ = _layernorm_halves(z[2], z[3])
        vls = [(vh * lng_ref[b : b + 1, :] + lnb_ref[b : b + 1, :]).astype(BF16) for b, vh in enumerate((vh0, vh1))]
        wtril = _causal_weights(ws_ref)
        m = jnp.zeros((tm, d), F32)
        for b in range(2):
            _spatial_gate(sv_ref, wtril, vls[b], bt_ref, b, tm, gd)
            gated = (z[b] * sv_ref[...]).astype(BF16)
            m = m + _dot(gated, wout_ref[b])
        m_ref[...] = m
        o_ref[...] = xv + _rms(m, gpost_ref[...])

    return _call(
        name,
        body,
        grid=(t // tm,),
        in_specs=[_rows(tm, d)] + [_VM] * 8,
        out_specs=[_rows(tm, d), _blocks(4, tm, q), _blocks(4, tm, q), _rows(tm, d)],
        out_shape=[
            jax.ShapeDtypeStruct((t, d), F32),
            jax.ShapeDtypeStruct((4, t, q), BF16),
            jax.ShapeDtypeStruct((4, t, q), BF16),
            jax.ShapeDtypeStruct((t, d), F32),
        ],
        scratch_shapes=[pltpu.VMEM((tm, q), F32)],
        args=[x, gpre, gpost, w_in, ln_g, ln_b, w_s, b_t, w_out],
        exchanges=exchanges,
    )


def _mix_a_bwd_hidden(name, dy, m, gpost, act, slope, ln_g, ln_b, w_s, b_t, w_out, exchanges=()):
    t, d = dy.shape
    _, _, q = act.shape
    gd = 2 * q // A_GROUPS
    tm = _step_rows(t)
    n_chunks = tm // CHUNK

    def body(dy_ref, m_ref, gpost_ref, z_ref, slope_ref, lng_ref, lnb_ref, ws_ref, bt_ref, wout_ref,
             dz_ref, dwout_ref, dws_ref, dbacc_ref, dlng_ref, dlnb_ref, dgain_ref, sv_ref, dvl_ref):
        first = pl.program_id(0) == 0

        @pl.when(first)
        def _():
            for ref in (dwout_ref, dws_ref, dbacc_ref, dlng_ref, dlnb_ref, dgain_ref):
                ref[...] = jnp.zeros_like(ref)

        dm, dgain = _rms_bwd(dy_ref[...], m_ref[...], gpost_ref[...])
        dgain_ref[...] += dgain
        dmb = dm.astype(BF16)
        vhs = list(_layernorm_halves(z_ref[2].astype(F32), z_ref[3].astype(F32)))
        rstd = vhs.pop()
        vls = [(vh * lng_ref[b : b + 1, :] + lnb_ref[b : b + 1, :]).astype(BF16) for b, vh in enumerate(vhs)]
        wtril = _causal_weights(ws_ref)
        dvhs = []
        for b in range(2):
            u = z_ref[b].astype(F32)
            _spatial_gate(sv_ref, wtril, vls[b], bt_ref, b, tm, gd)
            sv = sv_ref[...]
            gated = (u * sv).astype(BF16)
            dgated = _dot_nt(dmb, wout_ref[b])
            dwout_ref[b] += _dot_tn(gated, dmb)
            dz_ref[b] = (dgated * sv * slope_ref[b].astype(F32)).astype(BF16)
            dsv = dgated * u
            folded = dsv[0:CHUNK, :]
            for c in range(1, n_chunks):
                folded = folded + dsv[c * CHUNK : (c + 1) * CHUNK, :]
            for gg in range(A_GROUPS // 2):
                g = b * (A_GROUPS // 2) + gg
                dbacc_ref[:, g : g + 1] += jnp.sum(folded[:, gg * gd : (gg + 1) * gd], axis=1, keepdims=True)
            dsvb = dsv.astype(BF16)
            for gg in range(A_GROUPS // 2):
                g = b * (A_GROUPS // 2) + gg
                for c in range(n_chunks):
                    rows = slice(c * CHUNK, (c + 1) * CHUNK)
                    cols = slice(gg * gd, (gg + 1) * gd)
                    blk = dsvb[rows, cols]
                    dvl_ref[rows, cols] = _dot_tn(wtril[g], blk)
                    dws_ref[g] += _dot_nt(blk, vls[b][rows, cols])
            dvl = dvl_ref[...]
            dlng_ref[b : b + 1, :] += jnp.sum(dvl * vhs[b], axis=0, keepdims=True)
            dlnb_ref[b : b + 1, :] += jnp.sum(dvl, axis=0, keepdims=True)
            dvhs.append(dvl * lng_ref[b : b + 1, :])
        width = 2.0 * q
        m1 = (jnp.sum(dvhs[0], axis=-1, keepdims=True) + jnp.sum(dvhs[1], axis=-1, keepdims=True)) / width
        m2 = (jnp.sum(dvhs[0] * vhs[0], axis=-1, keepdims=True)
              + jnp.sum(dvhs[1] * vhs[1], axis=-1, keepdims=True)) / width
        for b in range(2):
            dv = rstd * (dvhs[b] - m1 - vhs[b] * m2)
            dz_ref[2 + b] = (dv * slope_ref[2 + b].astype(F32)).astype(BF16)

        @pl.when(pl.program_id(0) == t // tm - 1)
        def _():
            row = lax.broadcasted_iota(jnp.int32, (CHUNK, CHUNK), 0)
            col = lax.broadcasted_iota(jnp.int32, (CHUNK, CHUNK), 1)
            for g in range(A_GROUPS):
                dws_ref[g] = jnp.where(row >= col, dws_ref[g], 0.0)

    return _call(
        name,
        body,
        grid=(t // tm,),
        in_specs=[_rows(tm, d), _rows(tm, d), _VM, _blocks(4, tm, q), _blocks(4, tm, q)] + [_VM] * 5,
        out_specs=[_blocks(4, tm, q)] + [_VM] * 6,
        out_shape=[
            jax.ShapeDtypeStruct((4, t, q), BF16),
            jax.ShapeDtypeStruct((2, q, d), F32),
            jax.ShapeDtypeStruct((A_GROUPS, CHUNK, CHUNK), F32),
            jax.ShapeDtypeStruct((CHUNK, A_GROUPS), F32),
            jax.ShapeDtypeStruct((2, q), F32),
            jax.ShapeDtypeStruct((2, q), F32),
            jax.ShapeDtypeStruct((1, d), F32),
        ],
        scratch_shapes=[pltpu.VMEM((tm, q), F32), pltpu.VMEM((tm, q), F32)],
        args=[dy, m, gpost, act, slope, ln_g, ln_b, w_s, b_t, w_out],
        exchanges=exchanges,
    )


def _window_counts(tm, win):
    pos = pl.program_id(0) * tm + lax.broadcasted_iota(jnp.int32, (tm, 1), 0)
    return jnp.minimum(pos + 1, win).astype(F32)


def _pooled(p, halo, tm, gd):
    prev = jnp.where(pl.program_id(0) == 0, 0.0, halo)
    ext = jnp.concatenate([prev, p], axis=0)
    out = []
    for g, win in enumerate(B_WINDOWS):
        s = ext[:, g * gd : (g + 1) * gd]
        step = 1
        while step < win:
            s = s + pltpu.roll(s, step, 0)
            step *= 2
        total = s[HALO:, :]
        out.append(total / _window_counts(tm, win) - p[:, g * gd : (g + 1) * gd])
    return out


def _halo_spec(t, tm, d, ahead):
    per = tm // HALO
    if ahead:
        return pl.BlockSpec((HALO, d), lambda i: (jnp.minimum((i + 1) * per, t // HALO - 1), 0))
    return pl.BlockSpec((HALO, d), lambda i: (jnp.maximum(i * per - 1, 0), 0))


def _mix_b_fwd(name, x, gpre, gpost, w_in, w_grp, scale, w_out, exchanges=()):
    t, d = x.shape
    gd = d // len(B_WINDOWS)
    tm = _step_rows(t)

    def body(x_ref, xh_ref, gpre_ref, gpost_ref, win_ref, wgrp_ref, scale_ref, wout_ref, o_ref, p_ref, m_ref):
        xv = x_ref[...]
        gain = gpre_ref[...]
        p = _dot(_rms(xv, gain).astype(BF16), win_ref[...])
        p_ref[...] = p
        halo = _dot(_rms(xh_ref[...], gain).astype(BF16), win_ref[...])
        pooled = _pooled(p, halo, tm, gd)
        mixed = jnp.concatenate([_dot(pg.astype(BF16), wgrp_ref[g]) for g, pg in enumerate(pooled)], axis=1)
        m = _dot((mixed * scale_ref[...]).astype(BF16), wout_ref[...])
        m_ref[...] = m
        o_ref[...] = xv + _rms(m, gpost_ref[...])

    return _call(
        name,
        body,
        grid=(t // tm,),
        in_specs=[_rows(tm, d), _halo_spec(t, tm, d, False), _VM, _VM, _VM, _VM, _VM, _VM],
        out_specs=[_rows(tm, d)] * 3,
        out_shape=[jax.ShapeDtypeStruct((t, d), F32)] * 3,
        args=[x, x, gpre, gpost, w_in, w_grp, scale, w_out],
        exchanges=exchanges,
    )


def _mix_b_bwd_hidden(name, dy, m, gpost, p, w_grp, scale, w_out, exchanges=()):
    t, d = dy.shape
    gd = d // len(B_WINDOWS)
    tm = _step_rows(t)

    def body(dy_ref, m_ref, gpost_ref, p_ref, halo_ref, wgrp_ref, scale_ref, wout_ref,
             dq_ref, dwout_ref, dwgrp_ref, dscale_ref, dgain_ref):
        @pl.when(pl.program_id(0) == 0)
        def _():
            for ref in (dwout_ref, dwgrp_ref, dscale_ref, dgain_ref):
                ref[...] = jnp.zeros_like(ref)

        dm, dgain = _rms_bwd(dy_ref[...], m_ref[...], gpost_ref[...])
        dgain_ref[...] += dgain
        dmb = dm.astype(BF16)
        pooled = [pg.astype(BF16) for pg in _pooled(p_ref[...], halo_ref[...], tm, gd)]
        mixed = jnp.concatenate([_dot(pg, wgrp_ref[g]) for g, pg in enumerate(pooled)], axis=1)
        scale = scale_ref[...]
        ms = (mixed * scale).astype(BF16)
        dms = _dot_nt(dmb, wout_ref[...])
        dwout_ref[...] += _dot_tn(ms, dmb)
        dscale_ref[...] += jnp.sum(dms * mixed, axis=0, keepdims=True)
        dmixed = (dms * scale).astype(BF16)
        for g, win in enumerate(B_WINDOWS):
            cols = slice(g * gd, (g + 1) * gd)
            dmg = dmixed[:, cols]
            dwgrp_ref[g] += _dot_tn(pooled[g], dmg)
            dq_ref[:, cols] = _dot_nt(dmg, wgrp_ref[g]) / _window_counts(tm, win)

    return _call(
        name,
        body,
        grid=(t // tm,),
        in_specs=[_rows(tm, d), _rows(tm, d), _VM, _rows(tm, d), _halo_spec(t, tm, d, False), _VM, _VM, _VM],
        out_specs=[_rows(tm, d), _VM, _VM, _VM, _VM],
        out_shape=[
            jax.ShapeDtypeStruct((t, d), F32),
            jax.ShapeDtypeStruct((d, d), F32),
            jax.ShapeDtypeStruct((len(B_WINDOWS), gd, gd), F32),
            jax.ShapeDtypeStruct((1, d), F32),
            jax.ShapeDtypeStruct((1, d), F32),
        ],
        args=[dy, m, gpost, p, p, w_grp, scale, w_out],
        exchanges=exchanges,
    )


def _mix_b_bwd_in(name, dres, x, gpre, dq, w_in, exchanges=()):
    t, d = dq.shape
    gd = d // len(B_WINDOWS)
    tm = _step_rows(t)
    n_steps = t // tm

    def body(dres_ref, x_ref, gpre_ref, dq_ref, halo_ref, win_ref, dx_ref, dw_ref, dgain_ref):
        @pl.when(pl.program_id(0) == 0)
        def _():
            dw_ref[...] = jnp.zeros_like(dw_ref)
            dgain_ref[...] = jnp.zeros_like(dgain_ref)

        dq_blk = dq_ref[...]
        nxt = jnp.where(pl.program_id(0) == n_steps - 1, 0.0, halo_ref[...])
        ext = jnp.concatenate([dq_blk, nxt], axis=0)
        parts = []
        for g, win in enumerate(B_WINDOWS):
            cols = slice(g * gd, (g + 1) * gd)
            s = ext[:, cols]
            step = 1
            while step < win:
                s = s + pltpu.roll(s, tm + HALO - step, 0)
                step *= 2
            parts.append((s[:tm, :] - dq_blk[:, cols] * _window_counts(tm, win)).astype(BF16))
        dp = jnp.concatenate(parts, axis=1)
        xv = x_ref[...]
        gain = gpre_ref[...]
        hb = _rms(xv, gain).astype(BF16)
        dw_ref[...] += _dot_tn(hb, dp)
        dx, dgain = _rms_bwd(_dot_nt(dp, win_ref[...]), xv, gain)
        dx_ref[...] = dres_ref[...] + dx
        dgain_ref[...] += dgain

    return _call(
        name,
        body,
        grid=(n_steps,),
        in_specs=[_rows(tm, d), _rows(tm, d), _VM, _rows(tm, d), _halo_spec(t, tm, d, True), _VM],
        out_specs=[_rows(tm, d), _VM, _VM],
        out_shape=[
            jax.ShapeDtypeStruct((t, d), F32),
            jax.ShapeDtypeStruct((d, d), F32),
            jax.ShapeDtypeStruct((1, d), F32),
        ],
        args=[dres, x, gpre, dq, dq, w_in],
        exchanges=exchanges,
    )


def _cast_into_slots(name, place, w, dtype):
    n_layers, r, c = w.shape

    def body(place_ref, w_ref, *o_refs):
        del place_ref
        for j, o_ref in enumerate(o_refs):
            @pl.when(pl.program_id(0) == j)
            def _():
                o_ref[...] = w_ref[...].astype(dtype)

    return pl.pallas_call(
        body,
        name=name,
        grid_spec=pltpu.PrefetchScalarGridSpec(
            num_scalar_prefetch=1,
            grid=(n_layers,),
            in_specs=[pl.BlockSpec((1, r, c), lambda i, place_ref: (i, 0, 0))],
            out_specs=[pl.BlockSpec((1, r, c), lambda i, place_ref: (place_ref[0], 0, 0))] * n_layers,
        ),
        out_shape=[jax.ShapeDtypeStruct((N_CHIPS, r, c), dtype)] * n_layers,
        compiler_params=_params(),
    )(place, w)


def _row_tile(r):
    return 256 if r % 256 == 0 else r


def _pair_sum(name, place, dw, recv):
    _, r, c = dw.shape
    half = r // 2
    tr = _row_tile(half)
    per = half // tr

    def body(place_ref, a_ref, b_ref, o_ref):
        del place_ref
        o_ref[...] = (a_ref[...] + b_ref[...]).astype(BF16)

    return pl.pallas_call(
        body,
        name=name,
        grid_spec=pltpu.PrefetchScalarGridSpec(
            num_scalar_prefetch=1,
            grid=(N_CHIPS, per),
            in_specs=[
                pl.BlockSpec((1, tr, c), lambda k, i, place_ref: (k, place_ref[1] * per + i, 0)),
                pl.BlockSpec((1, tr, c), lambda k, i, place_ref: (k, i, 0)),
            ],
            out_specs=pl.BlockSpec((1, tr, c), lambda k, i, place_ref: (k, i, 0)),
        ),
        out_shape=jax.ShapeDtypeStruct(recv.shape, BF16),
        compiler_params=pltpu.CompilerParams(
            dimension_semantics=("arbitrary",) * 2, vmem_limit_bytes=VMEM_LIMIT_BYTES),
    )(place, dw, recv)


def _chip_sum(name, place, mine, others, exchanges=()):
    n_layers = len(mine)
    _, half, c = mine[0].shape
    tr = _row_tile(half)
    per = half // tr

    def body(place_ref, *refs):
        del place_ref
        o_ref = refs[-1]
        for j in range(n_layers):
            @pl.when(pl.program_id(0) == j)
            def _():
                parts = refs[4 * j : 4 * j + 4]
                acc = parts[0][...].astype(F32) + parts[1][...].astype(F32)
                acc = acc + parts[2][...].astype(F32)
                o_ref[...] = acc + parts[3][...].astype(F32)

    def part(j, flip):
        return pl.BlockSpec((1, tr, c), lambda l, i, place_ref: (
            jnp.bitwise_xor(place_ref[0], flip), jnp.where(l == j, i, 0), 0))

    args = []
    for j in range(n_layers):
        args += [mine[j], others[j], others[j], others[j]]
    (total,), ex_outs = _call(
        name,
        body,
        grid=(n_layers, per),
        in_specs=[part(j, flip) for j in range(n_layers) for flip in range(N_CHIPS)],
        out_specs=[pl.BlockSpec((1, tr, c), lambda l, i, place_ref: (l, place_ref[1] * per + i, 0))],
        out_shape=[jax.ShapeDtypeStruct((n_layers, 2 * half, c), F32)],
        args=args,
        prefetch=[place],
        exchanges=exchanges,
    )
    return total, ex_outs


def _adamw(name, w, g, m, v, exchanges=()):
    n_layers, r, c = w.shape
    tr = _row_tile(r)

    def body(w_ref, g_ref, m_ref, v_ref, go_ref, d_ref, nm_ref, nv_ref):
        gv = g_ref[...]
        go_ref[...] = gv
        nm = ADAM_B1 * m_ref[...] + (1.0 - ADAM_B1) * gv
        nv = ADAM_B2 * v_ref[...] + (1.0 - ADAM_B2) * jnp.square(gv)
        m_hat = nm / (1.0 - ADAM_B1 ** ADAM_STEP)
        v_hat = nv / (1.0 - ADAM_B2 ** ADAM_STEP)
        d_ref[...] = -ADAM_LR * (m_hat / (jnp.sqrt(v_hat) + ADAM_EPS) + ADAM_WD * w_ref[...])
        nm_ref[...] = nm
        nv_ref[...] = nv

    spec = pl.BlockSpec((1, tr, c), lambda l, i: (l, i, 0))
    return _call(
        name,
        body,
        grid=(n_layers, r // tr),
        in_specs=[spec] * 4,
        out_specs=[spec] * 4,
        out_shape=[jax.ShapeDtypeStruct(w.shape, F32)] * 4,
        args=[w, g, m, v],
        exchanges=exchanges,
    )


def _all_sum_small(packed):
    m_per, n = packed.shape

    def body(x_ref, sum_ref, all_ref, send_sems, recv_sems, local_sem):
        x, y, c = _position()
        me, sibling = (x, y, c), (x, y, 1 - c)
        chips = _other_chips(x, y)

        def rows(px, py, pc):
            return all_ref.at[pl.ds((4 * px + 2 * py + pc) * m_per, m_per), :]

        def copy(k, block, to, src=None):
            return pltpu.make_async_remote_copy(
                src_ref=rows(*block) if src is None else src, dst_ref=rows(*block),
                send_sem=send_sems.at[k], recv_sem=recv_sems.at[k], device_id=to, device_id_type=MESH)

        mine = pltpu.make_async_copy(x_ref, rows(*me), local_sem)
        mine.start()
        first = [copy(0, me, sibling, src=x_ref)]
        first += [copy(1 + j, me, (*chip, c), src=x_ref) for j, chip in enumerate(chips)]
        for cp in first:
            cp.start()
        passed = [copy(4 + j, (*chip, c), sibling) for j, chip in enumerate(chips)]
        for j, chip in enumerate(chips):
            copy(1 + j, (*chip, c), me).wait_recv()
            passed[j].start()
        copy(0, sibling, me).wait_recv()
        for j, chip in enumerate(chips):
            copy(4 + j, (*chip, 1 - c), me).wait_recv()
        for cp in first + passed:
            cp.wait_send()
        mine.wait()
        acc = all_ref[0:m_per, :]
        for k in range(1, N_DEV):
            acc = acc + all_ref[k * m_per : (k + 1) * m_per, :]
        sum_ref[...] = acc

    return pl.pallas_call(
        body,
        name="all_sum_small",
        in_specs=[_VM],
        out_specs=_VM,
        out_shape=jax.ShapeDtypeStruct((m_per, n), F32),
        scratch_shapes=[
            pltpu.VMEM((N_DEV * m_per, n), F32),
            pltpu.SemaphoreType.DMA((7,)),
            pltpu.SemaphoreType.DMA((7,)),
            pltpu.SemaphoreType.DMA,
        ],
        compiler_params=pltpu.CompilerParams(vmem_limit_bytes=VMEM_LIMIT_BYTES),
    )(packed)


SHARDED = ("a_w_in", "a_w_out", "b_w_in", "b_w_grp", "b_scale", "b_w_out", "ffn_w_gate", "ffn_w_up", "ffn_w_down")
SMALL = ("a_ln_g", "a_ln_b", "a_w_s", "a_b_s", "mix_pre_g", "mix_post_g", "ffn_pre_g", "ffn_post_g")
WEIGHTS = ("a_w_in", "a_ln_g", "a_ln_b", "a_w_s", "a_b_s", "a_w_out", "b_w_in", "b_w_grp", "b_scale", "b_w_out",
           "mix_pre_g", "mix_post_g", "ffn_pre_g", "ffn_post_g", "ffn_w_gate", "ffn_w_up", "ffn_w_down")


TRANSPOSED = ("ffn_w_gate", "ffn_w_up")


def _as_layers(name, a):
    if name in TRANSPOSED:
        return jnp.swapaxes(a, 1, 2)
    if a.ndim == 2:
        return a.reshape(a.shape[0], 1, a.shape[1])
    return a.reshape(a.shape[0], -1, a.shape[-1])


def _from_layers(name, a, shape):
    if name in TRANSPOSED:
        return jnp.swapaxes(a, 1, 2)
    return a.reshape(shape)


def _pack_small(parts):
    return jnp.concatenate([p.reshape(-1, 128) for p in parts], axis=0)


def _unpack_small(packed, like):
    out, row = [], 0
    for ref in like:
        rows = ref.size // 128
        out.append(packed[row : row + rows].reshape(ref.shape))
        row += rows
    return out


def kernel(x, a_w_in, a_ln_g, a_ln_b, a_w_s, a_b_s, a_w_out, b_w_in, b_w_grp, b_scale, b_w_out, mix_pre_g, mix_post_g, ffn_pre_g, ffn_post_g, ffn_w_gate, ffn_w_up, ffn_w_down, loss_target, m_a_w_in, m_a_ln_g, m_a_ln_b, m_a_w_s, m_a_b_s, m_a_w_out, m_b_w_in, m_b_w_grp, m_b_scale, m_b_w_out, m_mix_pre_g, m_mix_post_g, m_ffn_pre_g, m_ffn_post_g, m_ffn_w_gate, m_ffn_w_up, m_ffn_w_down, v_a_w_in, v_a_ln_g, v_a_ln_b, v_a_w_s, v_a_b_s, v_a_w_out, v_b_w_in, v_b_w_grp, v_b_scale, v_b_w_out, v_mix_pre_g, v_mix_post_g, v_ffn_pre_g, v_ffn_post_g, v_ffn_w_gate, v_ffn_w_up, v_ffn_w_down):
    weights = dict(a_w_in=a_w_in, a_ln_g=a_ln_g, a_ln_b=a_ln_b, a_w_s=a_w_s, a_b_s=a_b_s, a_w_out=a_w_out,
                   b_w_in=b_w_in, b_w_grp=b_w_grp, b_scale=b_scale, b_w_out=b_w_out, mix_pre_g=mix_pre_g,
                   mix_post_g=mix_post_g, ffn_pre_g=ffn_pre_g, ffn_post_g=ffn_post_g, ffn_w_gate=ffn_w_gate,
                   ffn_w_up=ffn_w_up, ffn_w_down=ffn_w_down)
    mom1 = dict(a_w_in=m_a_w_in, a_ln_g=m_a_ln_g, a_ln_b=m_a_ln_b, a_w_s=m_a_w_s, a_b_s=m_a_b_s, a_w_out=m_a_w_out,
                b_w_in=m_b_w_in, b_w_grp=m_b_w_grp, b_scale=m_b_scale, b_w_out=m_b_w_out, mix_pre_g=m_mix_pre_g,
                mix_post_g=m_mix_post_g, ffn_pre_g=m_ffn_pre_g, ffn_post_g=m_ffn_post_g, ffn_w_gate=m_ffn_w_gate,
                ffn_w_up=m_ffn_w_up, ffn_w_down=m_ffn_w_down)
    mom2 = dict(a_w_in=v_a_w_in, a_ln_g=v_a_ln_g, a_ln_b=v_a_ln_b, a_w_s=v_a_w_s, a_b_s=v_a_b_s, a_w_out=v_a_w_out,
                b_w_in=v_b_w_in, b_w_grp=v_b_w_grp, b_scale=v_b_scale, b_w_out=v_b_w_out, mix_pre_g=v_mix_pre_g,
                mix_post_g=v_mix_post_g, ffn_pre_g=v_ffn_pre_g, ffn_post_g=v_ffn_post_g, ffn_w_gate=v_ffn_w_gate,
                ffn_w_up=v_ffn_w_up, ffn_w_down=v_ffn_w_down)

    t, d = x.shape[1], x.shape[2]
    depth = mix_pre_g.shape[0]
    gd_b = d // len(B_WINDOWS)
    xs = x.reshape(t, d)
    target = loss_target.reshape(t, d)

    chip = 2 * lax.axis_index("x") + lax.axis_index("y")
    place = jnp.stack([chip, lax.axis_index("c")]).astype(jnp.int32)
    bufs = {name: list(_cast_into_slots("cast_" + name, place, _as_layers(name, weights[name]),
                                        F32 if name == "b_scale" else BF16)) for name in SHARDED}

    def gain(name, i):
        return weights[name][i].reshape(1, d)

    def weight_keys(i):
        j = i // 2
        mixer = [("a_w_in", j), ("a_w_out", j)] if i % 2 == 0 else [("b_w_in", j), ("b_w_grp", j), ("b_w_out", j)]
        return mixer, [("ffn_w_gate", i), ("ffn_w_up", i), ("ffn_w_down", i)]

    def gather(keys):
        return _GatherWeights([bufs[n][j] for n, j in keys],
                              whole=[k for k, (n, _) in enumerate(keys) if n == "b_scale"])

    def gathered(keys, outs):
        for (n, j), buf in zip(keys, outs):
            bufs[n][j] = buf

    first = weight_keys(0)[0] + [("b_scale", j) for j in range(b_scale.shape[0])]
    gathered(first, _exchange("gather_first", [gather(first)])[0])
    saved = []
    cur = xs
    for i in range(depth):
        j = i // 2
        mixer_next, ffn_next = weight_keys(i + 1) if i + 1 < depth else ([], [])
        ffn_keys = weight_keys(i)[1]
        after_ffn = mixer_next if (i + 1) % 2 == 0 else mixer_next + ffn_next
        if i % 2 == 0:
            w_in = bufs["a_w_in"][j]
            q = w_in.shape[2]
            w_out = bufs["a_w_out"][j].reshape(2, q, d)
            ln_g = a_ln_g[j].reshape(2, q)
            ln_b = a_ln_b[j].reshape(2, q)
            b_t = jnp.transpose(a_b_s[j])
            (nxt, act, slope, m), (got,) = _mix_a_fwd(
                f"mix_a_fwd{j}", cur, gain("mix_pre_g", i), gain("mix_post_g", i), w_in, ln_g, ln_b, a_w_s[j], b_t,
                w_out, exchanges=[gather(ffn_keys)])
            gathered(ffn_keys, got)
            mix_saved = dict(x=cur, act=act, slope=slope, m=m, w_in=w_in, w_out=w_out, ln_g=ln_g, ln_b=ln_b, b_t=b_t)
        else:
            w_in = bufs["b_w_in"][j].reshape(d, d)
            w_out = bufs["b_w_out"][j].reshape(d, d)
            w_grp = jnp.transpose(bufs["b_w_grp"][j].reshape(N_CHIPS, len(B_WINDOWS), gd_b // N_CHIPS, gd_b),
                                  (1, 0, 2, 3)).reshape(len(B_WINDOWS), gd_b, gd_b)
            scale = bufs["b_scale"][j].reshape(1, d)
            (nxt, p, m), _ = _mix_b_fwd(f"mix_b_fwd{j}", cur, gain("mix_pre_g", i), gain("mix_post_g", i),
                                        w_in, w_grp, scale, w_out)
            mix_saved = dict(x=cur, p=p, m=m, w_in=w_in, w_out=w_out, w_grp=w_grp, scale=scale)
        cur = nxt
        wg, wu, wd = (bufs[n][k].reshape(1, -1, d) for n, k in ffn_keys)
        (nxt, a, dup, dgate, f, *sq), (got,) = _ffn_fwd(
            f"ffn_fwd{i}", cur, gain("ffn_pre_g", i), gain("ffn_post_g", i), wg, wu, wd,
            target=target if i == depth - 1 else None, exchanges=[gather(after_ffn)])
        gathered(after_ffn, got)
        saved.append((mix_saved, dict(x=cur, a=a, dup=dup, dgate=dgate, f=f, wg=wg, wu=wu, wd=wd)))
        cur = nxt

    dcur = cur
    loss = lax.psum(0.5 * sq[0][0, 0] / d, ("x", "y", "c"))

    grads = {name: [None] * weights[name].shape[0] for name in WEIGHTS}
    state = dict(to_sibling=[], to_chips=[])
    pair, from_chips = {}, {}

    def exchanges_due():
        return [_ToSibling([a for _, _, a in state["to_sibling"]]), _ToChips([a for _, _, a in state["to_chips"]])]

    def exchanged(outs):
        from_sibling, arrived = outs
        for (n, k, _), got in zip(state["to_chips"], arrived):
            from_chips[n, k] = got
        state["to_chips"] = []
        for (n, k, dw), got in zip(state["to_sibling"], from_sibling):
            pair[n, k] = _pair_sum(f"pair_sum_{n}{k}", place, dw, got)
            state["to_chips"].append((n, k, pair[n, k]))
        state["to_sibling"] = []

    def made(name, k, dw):
        grads[name][k] = dw
        state["to_sibling"].append((name, k, dw))

    for i in reversed(range(depth)):
        j = i // 2
        mix_saved, ffn_saved = saved[i]
        s = ffn_saved
        (dg, du, dwd, dgain), outs = _ffn_bwd_hidden(
            f"ffn_bwd_hidden{i}", dcur, s["f"], gain("ffn_post_g", i), s["a"], s["dup"], s["dgate"], s["wd"],
            exchanges=exchanges_due())
        exchanged(outs)
        grads["ffn_post_g"][i] = dgain
        made("ffn_w_down", i, dwd.reshape(N_CHIPS, -1, d))
        (dcur, dwg, dwu, dgain), outs = _bwd_in(
            f"ffn_bwd_in{i}", dcur, s["x"], gain("ffn_pre_g", i), [dg, du], [s["wg"], s["wu"]], transposed=True,
            exchanges=exchanges_due())
        exchanged(outs)
        grads["ffn_pre_g"][i] = dgain
        made("ffn_w_gate", i, dwg.reshape(N_CHIPS, -1, d))
        made("ffn_w_up", i, dwu.reshape(N_CHIPS, -1, d))
        s = mix_saved
        if i % 2 == 0:
            (dz, dwout, dws, dbacc, dlng, dlnb, dgain), outs = _mix_a_bwd_hidden(
                f"mix_a_bwd_hidden{j}", dcur, s["m"], gain("mix_post_g", i), s["act"], s["slope"], s["ln_g"], s["ln_b"],
                a_w_s[j], s["b_t"], s["w_out"], exchanges=exchanges_due())
            exchanged(outs)
            grads["a_w_s"][j] = dws
            grads["a_b_s"][j] = jnp.transpose(dbacc)
            grads["a_ln_g"][j] = dlng.reshape(-1)
            grads["a_ln_b"][j] = dlnb.reshape(-1)
            grads["mix_post_g"][i] = dgain
            made("a_w_out", j, dwout.reshape(N_CHIPS, -1, d))
            (dcur, dwin, dgain), outs = _bwd_in(
                f"mix_a_bwd_in{j}", dcur, s["x"], gain("mix_pre_g", i), [dz], [s["w_in"]], exchanges=exchanges_due())
            exchanged(outs)
            grads["mix_pre_g"][i] = dgain
            made("a_w_in", j, dwin)
        else:
            (dq, dwout, dwgrp, dscale, dgain), outs = _mix_b_bwd_hidden(
                f"mix_b_bwd_hidden{j}", dcur, s["m"], gain("mix_post_g", i), s["p"], s["w_grp"], s["scale"],
                s["w_out"], exchanges=exchanges_due())
            exchanged(outs)
            grads["b_scale"][j] = dscale
            grads["mix_post_g"][i] = dgain
            made("b_w_out", j, dwout.reshape(N_CHIPS, -1, d))
            made("b_w_grp", j, jnp.transpose(
                dwgrp.reshape(len(B_WINDOWS), N_CHIPS, gd_b // N_CHIPS, gd_b), (1, 0, 2, 3)).reshape(N_CHIPS, -1, gd_b))
            (dcur, dwin, dgain), outs = _mix_b_bwd_in(
                f"mix_b_bwd_in{j}", dcur, s["x"], gain("mix_pre_g", i), dq, s["w_in"], exchanges=exchanges_due())
            exchanged(outs)
            grads["mix_pre_g"][i] = dgain
            made("b_w_in", j, dwin.reshape(N_CHIPS, -1, d))
    grad_x = dcur.reshape(x.shape)

    exchanged(_exchange("grads_last", exchanges_due()))
    exchanged(_exchange("grads_last_to_chips", exchanges_due()))

    reduced_names = [name for name in SHARDED if name != "b_scale"]
    sums = []
    for name in reduced_names:
        layers = range(weights[name].shape[0])
        sums.append(_chip_sum("chip_sum_" + name, place, [pair[name, k] for k in layers],
                              [from_chips[name, k] for k in layers])[0])
    reduced = dict(zip(reduced_names, _exchange("swap_halves", [_SwapHalves(sums)])[0]))

    out_g, out_d, out_m, out_v = {}, {}, {}, {}
    for name in reduced_names:
        shape = weights[name].shape
        results, _ = _adamw("adamw_" + name, _as_layers(name, weights[name]), reduced[name],
                            _as_layers(name, mom1[name]), _as_layers(name, mom2[name]))
        out_g[name], out_d[name], out_m[name], out_v[name] = (_from_layers(name, a, shape) for a in results)

    small_grads = [jnp.stack([g.reshape(weights[name].shape[1:]) for g in grads[name]], axis=0) for name in SMALL]
    scale_grad = jnp.concatenate(grads["b_scale"], axis=0)
    summed = _all_sum_small(_pack_small(small_grads + [scale_grad]))
    small_rows = summed.shape[0] - scale_grad.size // 128
    scale_sum = summed[small_rows:].reshape(scale_grad.shape)
    scale_mine = lax.dynamic_slice_in_dim(scale_sum, chip * b_scale.shape[1], b_scale.shape[1], axis=1)[:, None, :]
    summed = summed[:small_rows]
    results, _ = _adamw("adamw_b_scale", _as_layers("b_scale", b_scale), scale_mine,
                        _as_layers("b_scale", m_b_scale), _as_layers("b_scale", v_b_scale))
    out_g["b_scale"], out_d["b_scale"], out_m["b_scale"], out_v["b_scale"] = (a.reshape(b_scale.shape) for a in results)
    small_like = [weights[name] for name in SMALL]
    packs = [_pack_small([src[name] for name in SMALL]).reshape(1, -1, 128) for src in (weights, mom1, mom2)]
    results, _ = _adamw("adamw_small", packs[0], summed.reshape(1, -1, 128), packs[1], packs[2])
    for dst, packed in zip((out_g, out_d, out_m, out_v), (a[0] for a in results)):
        for name, val in zip(SMALL, _unpack_small(packed, small_like)):
            dst[name] = val

    return (loss, grad_x, *[out_g[n] for n in WEIGHTS], *[out_d[n] for n in WEIGHTS],
            *[out_m[n] for n in WEIGHTS], *[out_v[n] for n in WEIGHTS])
```

```python
import functools
import math

import jax
import jax.numpy as jnp
from jax import lax
from jax.experimental import pallas as pl
from jax.experimental.pallas import tpu as pltpu

F32 = jnp.float32
BF16 = jnp.bfloat16
MESH = pl.DeviceIdType.MESH

EPS = 1e-6
CHUNK = 128
A_GROUPS = 8
B_WINDOWS = (2, 4, 8, 16)
HALO = 16
N_CHIPS = 4
N_DEV = 8

ADAM_LR = 0.001
ADAM_B1 = 0.9
ADAM_B2 = 0.999
ADAM_EPS = 1e-08
ADAM_WD = 0.01
ADAM_STEP = 10

VMEM_LIMIT_BYTES = 60 * 1024 * 1024
INV_SQRT2 = 1.0 / math.sqrt(2.0)
INV_SQRT_2PI = 1.0 / math.sqrt(2.0 * math.pi)

_ANY = pl.BlockSpec(memory_space=pl.ANY)
_VM = pl.BlockSpec(memory_space=pltpu.VMEM)


def _params():
    return pltpu.CompilerParams(dimension_semantics=("arbitrary",), vmem_limit_bytes=VMEM_LIMIT_BYTES)


def _token_block(t):
    return 256 if t >= 1024 else 128


def _step_rows(t):
    return 2 * _token_block(t)


def _rows(tm, d):
    return pl.BlockSpec((tm, d), lambda i: (i, 0))


def _blocks(nb, tm, bw):
    return pl.BlockSpec((nb, tm, bw), lambda i: (0, i, 0))


def _dot(a, b):
    return lax.dot_general(a, b, (((1,), (0,)), ((), ())), preferred_element_type=F32)


def _dot_nt(a, b):
    return lax.dot_general(a, b, (((1,), (1,)), ((), ())), preferred_element_type=F32)


def _dot_tn(a, b):
    return lax.dot_general(a, b, (((0,), (0,)), ((), ())), preferred_element_type=F32)


def _rms(x, g):
    return x * lax.rsqrt(jnp.mean(x * x, axis=-1, keepdims=True) + EPS) * g


def _rms_bwd(dy, x, g):
    r = lax.rsqrt(jnp.mean(x * x, axis=-1, keepdims=True) + EPS)
    n = x * r
    dn = dy * g
    dx = r * (dn - n * jnp.mean(dn * n, axis=-1, keepdims=True))
    return dx, jnp.sum(dy * n, axis=0, keepdims=True)


def _gelu_and_grad(x):
    cdf = 0.5 * (1.0 + lax.erf(x * INV_SQRT2))
    return x * cdf, cdf + x * (jnp.exp(-0.5 * x * x) * INV_SQRT_2PI)


def _position():
    return lax.axis_index("x"), lax.axis_index("y"), lax.axis_index("c")


def _other_chips(x, y):
    return [(1 - x, y), (x, 1 - y), (1 - x, 1 - y)]


class _GatherWeights:
    def __init__(self, bufs, whole=()):
        self.inputs = list(bufs)
        self.out_shapes = [jax.ShapeDtypeStruct(b.shape, b.dtype) for b in bufs]
        self.aliases = {w: w for w in range(len(bufs))}
        self.n_sems = 6 * len(bufs)
        self.whole = frozenset(whole)

    def _part(self, outs, w, slot, core):
        if w in self.whole:
            return outs[w].at[slot]
        half = outs[w].shape[1] // 2
        return outs[w].at[slot, pl.ds(core * half, half)]

    def _ici(self, outs, send, recv, w, j, slot):
        x, y, c = _position()
        px, py = _other_chips(x, y)[j]
        part = self._part(outs, w, slot, c)
        return pltpu.make_async_remote_copy(
            src_ref=part, dst_ref=part, send_sem=send.at[6 * w + j], recv_sem=recv.at[6 * w + j],
            device_id=(px, py, c), device_id_type=MESH)

    def _d2d(self, outs, send, recv, w, j, slot, core):
        x, y, c = _position()
        part = self._part(outs, w, slot, core)
        return pltpu.make_async_remote_copy(
            src_ref=part, dst_ref=part, send_sem=send.at[6 * w + 3 + j], recv_sem=recv.at[6 * w + 3 + j],
            device_id=(x, y, 1 - c), device_id_type=MESH)

    def start(self, ins, outs, send, recv):
        x, y, _ = _position()
        for w in range(len(outs)):
            for j in range(3):
                self._ici(outs, send, recv, w, j, 2 * x + y).start()

    def finish(self, ins, outs, send, recv):
        x, y, c = _position()
        slots = [2 * px + py for px, py in _other_chips(x, y)]
        for w in range(len(outs)):
            for j, slot in enumerate(slots):
                self._ici(outs, send, recv, w, j, slot).wait_recv()
                if w not in self.whole:
                    self._d2d(outs, send, recv, w, j, slot, c).start()
        for w in range(len(outs)):
            for j, slot in enumerate(slots):
                if w not in self.whole:
                    self._d2d(outs, send, recv, w, j, slot, 1 - c).wait_recv()
        for w in range(len(outs)):
            for j, slot in enumerate(slots):
                self._ici(outs, send, recv, w, j, 2 * x + y).wait_send()
                if w not in self.whole:
                    self._d2d(outs, send, recv, w, j, slot, c).wait_send()


class _ToSibling:
    def __init__(self, grads):
        self.inputs = list(grads)
        self.out_shapes = [jax.ShapeDtypeStruct((g.shape[0], g.shape[1] // 2, g.shape[2]), g.dtype) for g in grads]
        self.aliases = {}
        self.n_sems = len(grads)

    def _copy(self, ins, outs, send, recv, w):
        x, y, c = _position()
        half = ins[w].shape[1] // 2
        return pltpu.make_async_remote_copy(
            src_ref=ins[w].at[:, pl.ds((1 - c) * half, half)], dst_ref=outs[w],
            send_sem=send.at[w], recv_sem=recv.at[w], device_id=(x, y, 1 - c), device_id_type=MESH)

    def start(self, ins, outs, send, recv):
        for w in range(len(ins)):
            self._copy(ins, outs, send, recv, w).start()

    def finish(self, ins, outs, send, recv):
        for w in range(len(ins)):
            self._copy(ins, outs, send, recv, w).wait_recv()
        for w in range(len(ins)):
            self._copy(ins, outs, send, recv, w).wait_send()


class _ToChips:
    def __init__(self, parts):
        self.inputs = list(parts)
        self.out_shapes = [jax.ShapeDtypeStruct(p.shape, p.dtype) for p in parts]
        self.aliases = {}
        self.n_sems = 3 * len(parts)

    def _copy(self, ins, outs, send, recv, w, j, outbound):
        x, y, c = _position()
        px, py = _other_chips(x, y)[j]
        me, peer = 2 * x + y, 2 * px + py
        src_slot, dst_slot = (peer, me) if outbound else (me, peer)
        return pltpu.make_async_remote_copy(
            src_ref=ins[w].at[src_slot], dst_ref=outs[w].at[dst_slot],
            send_sem=send.at[3 * w + j], recv_sem=recv.at[3 * w + j], device_id=(px, py, c), device_id_type=MESH)

    def start(self, ins, outs, send, recv):
        for w in range(len(ins)):
            for j in range(3):
                self._copy(ins, outs, send, recv, w, j, True).start()

    def finish(self, ins, outs, send, recv):
        for w in range(len(ins)):
            for j in range(3):
                self._copy(ins, outs, send, recv, w, j, False).wait_recv()
        for w in range(len(ins)):
            for j in range(3):
                self._copy(ins, outs, send, recv, w, j, True).wait_send()


class _SwapHalves:
    def __init__(self, bufs):
        self.inputs = list(bufs)
        self.out_shapes = [jax.ShapeDtypeStruct(b.shape, b.dtype) for b in bufs]
        self.aliases = {w: w for w in range(len(bufs))}
        self.n_sems = len(bufs)

    def _copy(self, outs, send, recv, w, core):
        x, y, c = _position()
        half = outs[w].shape[1] // 2
        rows = outs[w].at[:, pl.ds(core * half, half)]
        return pltpu.make_async_remote_copy(
            src_ref=rows, dst_ref=rows, send_sem=send.at[w], recv_sem=recv.at[w],
            device_id=(x, y, 1 - c), device_id_type=MESH)

    def start(self, ins, outs, send, recv):
        c = lax.axis_index("c")
        for w in range(len(outs)):
            self._copy(outs, send, recv, w, c).start()

    def finish(self, ins, outs, send, recv):
        c = lax.axis_index("c")
        for w in range(len(outs)):
            self._copy(outs, send, recv, w, 1 - c).wait_recv()
        for w in range(len(outs)):
            self._copy(outs, send, recv, w, c).wait_send()


def _call(name, body, *, grid, in_specs, out_specs, out_shape, args, scratch_shapes=(), prefetch=(), exchanges=()):
    given = list(exchanges)
    exchanges = [e for e in given if e.inputs]
    n_pre, n_in, n_out, n_scr = len(prefetch), len(args), len(out_shape), len(scratch_shapes)
    ex_in = [a for e in exchanges for a in e.inputs]
    ex_out = [s for e in exchanges for s in e.out_shapes]
    aliases = {}
    at_in, at_out = n_pre + n_in, n_out
    for e in exchanges:
        for i, o in e.aliases.items():
            aliases[at_in + i] = at_out + o
        at_in += len(e.inputs)
        at_out += len(e.out_shapes)

    def at_step(last):
        hit = None
        for axis, n in enumerate(grid):
            here = pl.program_id(axis) == (n - 1 if last else 0)
            hit = here if hit is None else jnp.logical_and(hit, here)
        return hit

    def fused(*refs):
        pre, refs = refs[:n_pre], refs[n_pre:]
        body_in, refs = refs[:n_in], refs[n_in:]
        ex_in_refs, refs = refs[: len(ex_in)], refs[len(ex_in) :]
        body_out, refs = refs[:n_out], refs[n_out:]
        ex_out_refs, refs = refs[: len(ex_out)], refs[len(ex_out) :]
        body_scr, sems = refs[:n_scr], refs[n_scr:]

        def each(stage):
            a = b = 0
            for n, e in enumerate(exchanges):
                ins, outs = ex_in_refs[a : a + len(e.inputs)], ex_out_refs[b : b + len(e.out_shapes)]
                getattr(e, stage)(ins, outs, sems[2 * n], sems[2 * n + 1])
                a += len(e.inputs)
                b += len(e.out_shapes)

        if exchanges:
            @pl.when(at_step(False))
            def _():
                each("start")

        if body is not None:
            body(*pre, *body_in, *body_out, *body_scr)

        if exchanges:
            @pl.when(at_step(True))
            def _():
                each("finish")

    outs = pl.pallas_call(
        fused,
        name=name,
        grid_spec=pltpu.PrefetchScalarGridSpec(
            num_scalar_prefetch=n_pre,
            grid=grid,
            in_specs=list(in_specs) + [_ANY] * len(ex_in),
            out_specs=list(out_specs) + [_ANY] * len(ex_out),
            scratch_shapes=list(scratch_shapes)
            + [pltpu.SemaphoreType.DMA((e.n_sems,)) for e in exchanges for _ in range(2)],
        ),
        out_shape=list(out_shape) + ex_out,
        input_output_aliases=aliases,
        compiler_params=pltpu.CompilerParams(
            dimension_semantics=("arbitrary",) * len(grid), vmem_limit_bytes=VMEM_LIMIT_BYTES),
    )(*prefetch, *args, *ex_in)
    body_outs, rest = list(outs[:n_out]), list(outs[n_out:])
    ex_outs = []
    for e in given:
        n_e = len(e.out_shapes) if e.inputs else 0
        ex_outs.append(rest[:n_e])
        rest = rest[n_e:]
    return body_outs, ex_outs


def _exchange(name, exchanges):
    return _call(name, None, grid=(1,), in_specs=[], out_specs=[], out_shape=[], args=[], exchanges=exchanges)[1]


def _ffn_fwd(name, x, gpre, gpost, wg, wu, wd, target=None, exchanges=()):
    t, d = x.shape
    nb, fs, _ = wg.shape
    sub = _token_block(t)
    tm = _step_rows(t)
    with_loss = target is not None

    def body(x_ref, gpre_ref, gpost_ref, wg_ref, wu_ref, wd_ref, *refs):
        if with_loss:
            t_ref, o_ref, a_ref, dup_ref, dgate_ref, f_ref, sq_ref = refs

            @pl.when(pl.program_id(0) == 0)
            def _():
                sq_ref[...] = jnp.zeros_like(sq_ref)
        else:
            o_ref, a_ref, dup_ref, dgate_ref, f_ref = refs
        for h in range(tm // sub):
            rows = slice(h * sub, (h + 1) * sub)
            xv = x_ref[rows, :]
            hb = _rms(xv, gpre_ref[...]).astype(BF16)
            f = jnp.zeros((sub, d), F32)
            for k in range(nb):
                g = _dot_nt(hb, wg_ref[k])
                u = _dot_nt(hb, wu_ref[k])
                s = jax.nn.sigmoid(g)
                sg = g * s
                a = (sg * u).astype(BF16)
                a_ref[k, rows, :] = a
                dup_ref[k, rows, :] = sg.astype(BF16)
                dgate_ref[k, rows, :] = (u * (s * (1.0 + g * (1.0 - s)))).astype(BF16)
                f = f + _dot(a, wd_ref[k])
            f_ref[rows, :] = f
            y = xv + _rms(f, gpost_ref[...])
            if with_loss:
                err = y - t_ref[rows, :]
                o_ref[rows, :] = err / d
                sq_ref[...] += jnp.sum(err * err)
            else:
                o_ref[rows, :] = y

    return _call(
        name,
        body,
        grid=(t // tm,),
        in_specs=[_rows(tm, d), _VM, _VM, _VM, _VM, _VM] + [_rows(tm, d)] * with_loss,
        out_specs=[_rows(tm, d)] + [_blocks(nb, tm, fs)] * 3 + [_rows(tm, d)] + [_VM] * with_loss,
        out_shape=[jax.ShapeDtypeStruct((t, d), F32)]
        + [jax.ShapeDtypeStruct((nb, t, fs), BF16)] * 3
        + [jax.ShapeDtypeStruct((t, d), F32)]
        + [jax.ShapeDtypeStruct((8, 128), F32)] * with_loss,
        args=[x, gpre, gpost, wg, wu, wd] + [target] * with_loss,
        exchanges=exchanges,
    )


def _ffn_bwd_hidden(name, dy, f, gpost, a, dup, dgate, wd, exchanges=()):
    t, d = dy.shape
    nb, fs, _ = wd.shape
    tm = _step_rows(t)

    def body(dy_ref, f_ref, gpost_ref, a_ref, dup_ref, dgate_ref, wd_ref, dg_ref, du_ref, dwd_ref, dgain_ref):
        @pl.when(pl.program_id(0) == 0)
        def _():
            dwd_ref[...] = jnp.zeros_like(dwd_ref)
            dgain_ref[...] = jnp.zeros_like(dgain_ref)

        df, dgain = _rms_bwd(dy_ref[...], f_ref[...], gpost_ref[...])
        dgain_ref[...] += dgain
        dfb = df.astype(BF16)
        for k in range(nb):
            da = _dot_nt(dfb, wd_ref[k])
            dwd_ref[k] += _dot_tn(a_ref[k], dfb)
            du_ref[k] = (da * dup_ref[k].astype(F32)).astype(BF16)
            dg_ref[k] = (da * dgate_ref[k].astype(F32)).astype(BF16)

    return _call(
        name,
        body,
        grid=(t // tm,),
        in_specs=[_rows(tm, d), _rows(tm, d), _VM] + [_blocks(nb, tm, fs)] * 3 + [_VM],
        out_specs=[_blocks(nb, tm, fs), _blocks(nb, tm, fs), _VM, _VM],
        out_shape=[
            jax.ShapeDtypeStruct((nb, t, fs), BF16),
            jax.ShapeDtypeStruct((nb, t, fs), BF16),
            jax.ShapeDtypeStruct((nb, fs, d), F32),
            jax.ShapeDtypeStruct((1, d), F32),
        ],
        args=[dy, f, gpost, a, dup, dgate, wd],
        exchanges=exchanges,
    )


def _bwd_in(name, dres, x, gpre, dzs, ws, transposed=False, exchanges=()):
    t, d = x.shape
    n = len(ws)
    resident = sum(6 * w.size for w in ws)
    tm = _step_rows(t) if resident <= VMEM_LIMIT_BYTES // 2 else _token_block(t)
    widths = [w.shape[1] if transposed else w.shape[2] for w in ws]

    def body(*refs):
        dres_ref, x_ref, gpre_ref = refs[:3]
        dz_refs = refs[3 : 3 + n]
        w_refs = refs[3 + n : 3 + 2 * n]
        dx_ref = refs[3 + 2 * n]
        dw_refs = refs[4 + 2 * n : 4 + 3 * n]
        dgain_ref = refs[4 + 3 * n]

        @pl.when(pl.program_id(0) == 0)
        def _():
            for dw_ref in dw_refs:
                dw_ref[...] = jnp.zeros_like(dw_ref)
            dgain_ref[...] = jnp.zeros_like(dgain_ref)

        xv = x_ref[...]
        gain = gpre_ref[...]
        hb = _rms(xv, gain).astype(BF16)
        dh = jnp.zeros((tm, d), F32)
        for dz_ref, w_ref, dw_ref in zip(dz_refs, w_refs, dw_refs):
            for k in range(w_ref.shape[0]):
                dz = dz_ref[k]
                if transposed:
                    dh = dh + _dot(dz, w_ref[k])
                    dw_ref[k] += _dot_tn(dz, hb)
                else:
                    dh = dh + _dot_nt(dz, w_ref[k])
                    dw_ref[k] += _dot_tn(hb, dz)
        dx, dgain = _rms_bwd(dh, xv, gain)
        dx_ref[...] = dres_ref[...] + dx
        dgain_ref[...] += dgain

    return _call(
        name,
        body,
        grid=(t // tm,),
        in_specs=[_rows(tm, d), _rows(tm, d), _VM]
        + [_blocks(w.shape[0], tm, bw) for w, bw in zip(ws, widths)]
        + [_VM] * n,
        out_specs=[_rows(tm, d)] + [_VM] * n + [_VM],
        out_shape=[jax.ShapeDtypeStruct((t, d), F32)]
        + [jax.ShapeDtypeStruct(w.shape, F32) for w in ws]
        + [jax.ShapeDtypeStruct((1, d), F32)],
        args=[dres, x, gpre, *dzs, *ws],
        exchanges=exchanges,
    )


def _causal_weights(ws_ref):
    row = lax.broadcasted_iota(jnp.int32, (CHUNK, CHUNK), 0)
    col = lax.broadcasted_iota(jnp.int32, (CHUNK, CHUNK), 1)
    return [jnp.where(row >= col, ws_ref[g], 0.0).astype(BF16) for g in range(A_GROUPS)]


def _layernorm_halves(v0, v1):
    width = v0.shape[-1] + v1.shape[-1]
    mu = (jnp.sum(v0, axis=-1, keepdims=True) + jnp.sum(v1, axis=-1, keepdims=True)) / width
    c0 = v0 - mu
    c1 = v1 - mu
    var = (jnp.sum(c0 * c0, axis=-1, keepdims=True) + jnp.sum(c1 * c1, axis=-1, keepdims=True)) / width
    rstd = lax.rsqrt(var + EPS)
    return c0 * rstd, c1 * rstd, rstd


def _spatial_gate(sv_ref, wtril, vl, bt_ref, half, tm, gd):
    for gg in range(A_GROUPS // 2):
        g = half * (A_GROUPS // 2) + gg
        bias = bt_ref[:, g : g + 1]
        for n in range(tm // CHUNK):
            blk = vl[n * CHUNK : (n + 1) * CHUNK, gg * gd : (gg + 1) * gd]
            sv_ref[n * CHUNK : (n + 1) * CHUNK, gg * gd : (gg + 1) * gd] = _dot(wtril[g], blk) + bias


def _mix_a_fwd(name, x, gpre, gpost, w_in, ln_g, ln_b, w_s, b_t, w_out, exchanges=()):
    t, d = x.shape
    _, _, q = w_in.shape
    gd = 2 * q // A_GROUPS
    tm = _step_rows(t)

    def body(x_ref, gpre_ref, gpost_ref, win_ref, lng_ref, lnb_ref, ws_ref, bt_ref, wout_ref,
             o_ref, z_ref, dz_ref, m_ref, sv_ref):
        xv = x_ref[...]
        hb = _rms(xv, gpre_ref[...]).astype(BF16)
        z = [None] * 4
        vls = []
        for k in (2, 3, 0, 1):
            act, slope = _gelu_and_grad(_dot(hb, win_ref[k]))
            z_ref[k] = act.astype(BF16)
            dz_ref[k] = slope.astype(BF16)
            z[k] = act
            if k == 3:
                vh0, vh1, _ = _layernorm_halves(z[2], z[3])
                vls = [(vh * lng_ref[b : b + 1, :] + lnb_ref[b : b + 1, :]).astype(BF16)
                       for b, vh in enumerate((vh0, vh1))]
        wtril = _causal_weights(ws_ref)
        m = jnp.zeros((tm, d), F32)
        for b in range(2):
            _spatial_gate(sv_ref, wtril, vls[b], bt_ref, b, tm, gd)
            gated = (z[b] * sv_ref[...]).astype(BF16)
            m = m + _dot(gated, wout_ref[b])
        m_ref[...] = m
        o_ref[...] = xv + _rms(m, gpost_ref[...])

    return _call(
        name,
        body,
        grid=(t // tm,),
        in_specs=[_rows(tm, d)] + [_VM] * 8,
        out_specs=[_rows(tm, d), _blocks(4, tm, q), _blocks(4, tm, q), _rows(tm, d)],
        out_shape=[
            jax.ShapeDtypeStruct((t, d), F32),
            jax.ShapeDtypeStruct((4, t, q), BF16),
            jax.ShapeDtypeStruct((4, t, q), BF16),
            jax.ShapeDtypeStruct((t, d), F32),
        ],
        scratch_shapes=[pltpu.VMEM((tm, q), F32)],
        args=[x, gpre, gpost, w_in, ln_g, ln_b, w_s, b_t, w_out],
        exchanges=exchanges,
    )


def _mix_a_bwd_hidden(name, dy, m, gpost, act, slope, ln_g, ln_b, w_s, b_t, w_out, exchanges=()):
    t, d = dy.shape
    _, _, q = act.shape
    gd = 2 * q // A_GROUPS
    tm = _step_rows(t)
    n_chunks = tm // CHUNK

    def body(dy_ref, m_ref, gpost_ref, z_ref, slope_ref, lng_ref, lnb_ref, ws_ref, bt_ref, wout_ref,
             dz_ref, dwout_ref, dws_ref, dbacc_ref, dlng_ref, dlnb_ref, dgain_ref, sv_ref, dvl_ref):
        first = pl.program_id(0) == 0

        @pl.when(first)
        def _():
            for ref in (dwout_ref, dws_ref, dbacc_ref, dlng_ref, dlnb_ref, dgain_ref):
                ref[...] = jnp.zeros_like(ref)

        dm, dgain = _rms_bwd(dy_ref[...], m_ref[...], gpost_ref[...])
        dgain_ref[...] += dgain
        dmb = dm.astype(BF16)
        vhs = list(_layernorm_halves(z_ref[2].astype(F32), z_ref[3].astype(F32)))
        rstd = vhs.pop()
        vls = [(vh * lng_ref[b : b + 1, :] + lnb_ref[b : b + 1, :]).astype(BF16) for b, vh in enumerate(vhs)]
        wtril = _causal_weights(ws_ref)
        dvhs = []
        for b in range(2):
            u = z_ref[b].astype(F32)
            _spatial_gate(sv_ref, wtril, vls[b], bt_ref, b, tm, gd)
            sv = sv_ref[...]
            gated = (u * sv).astype(BF16)
            dgated = _dot_nt(dmb, wout_ref[b])
            dwout_ref[b] += _dot_tn(gated, dmb)
            dz_ref[b] = (dgated * sv * slope_ref[b].astype(F32)).astype(BF16)
            dsv = dgated * u
            folded = dsv[0:CHUNK, :]
            for c in range(1, n_chunks):
                folded = folded + dsv[c * CHUNK : (c + 1) * CHUNK, :]
            for gg in range(A_GROUPS // 2):
                g = b * (A_GROUPS // 2) + gg
                dbacc_ref[:, g : g + 1] += jnp.sum(folded[:, gg * gd : (gg + 1) * gd], axis=1, keepdims=True)
            dsvb = dsv.astype(BF16)
            for gg in range(A_GROUPS // 2):
                g = b * (A_GROUPS // 2) + gg
                for c in range(n_chunks):
                    rows = slice(c * CHUNK, (c + 1) * CHUNK)
                    cols = slice(gg * gd, (gg + 1) * gd)
                    blk = dsvb[rows, cols]
                    dvl_ref[rows, cols] = _dot_tn(wtril[g], blk)
                    dws_ref[g] += _dot_nt(blk, vls[b][rows, cols])
            dvl = dvl_ref[...]
            dlng_ref[b : b + 1, :] += jnp.sum(dvl * vhs[b], axis=0, keepdims=True)
            dlnb_ref[b : b + 1, :] += jnp.sum(dvl, axis=0, keepdims=True)
            dvhs.append(dvl * lng_ref[b : b + 1, :])
        width = 2.0 * q
        m1 = (jnp.sum(dvhs[0], axis=-1, keepdims=True) + jnp.sum(dvhs[1], axis=-1, keepdims=True)) / width
        m2 = (jnp.sum(dvhs[0] * vhs[0], axis=-1, keepdims=True)
              + jnp.sum(dvhs[1] * vhs[1], axis=-1, keepdims=True)) / width
        for b in range(2):
            dv = rstd * (dvhs[b] - m1 - vhs[b] * m2)
            dz_ref[2 + b] = (dv * slope_ref[2 + b].astype(F32)).astype(BF16)

        @pl.when(pl.program_id(0) == t // tm - 1)
        def _():
            row = lax.broadcasted_iota(jnp.int32, (CHUNK, CHUNK), 0)
            col = lax.broadcasted_iota(jnp.int32, (CHUNK, CHUNK), 1)
            for g in range(A_GROUPS):
                dws_ref[g] = jnp.where(row >= col, dws_ref[g], 0.0)

    return _call(
        name,
        body,
        grid=(t // tm,),
        in_specs=[_rows(tm, d), _rows(tm, d), _VM, _blocks(4, tm, q), _blocks(4, tm, q)] + [_VM] * 5,
        out_specs=[_blocks(4, tm, q)] + [_VM] * 6,
        out_shape=[
            jax.ShapeDtypeStruct((4, t, q), BF16),
            jax.ShapeDtypeStruct((2, q, d), F32),
            jax.ShapeDtypeStruct((A_GROUPS, CHUNK, CHUNK), F32),
            jax.ShapeDtypeStruct((CHUNK, A_GROUPS), F32),
            jax.ShapeDtypeStruct((2, q), F32),
            jax.ShapeDtypeStruct((2, q), F32),
            jax.ShapeDtypeStruct((1, d), F32),
        ],
        scratch_shapes=[pltpu.VMEM((tm, q), F32), pltpu.VMEM((tm, q), F32)],
        args=[dy, m, gpost, act, slope, ln_g, ln_b, w_s, b_t, w_out],
        exchanges=exchanges,
    )


def _window_counts(tm, win):
    pos = pl.program_id(0) * tm + lax.broadcasted_iota(jnp.int32, (tm, 1), 0)
    return jnp.minimum(pos + 1, win).astype(F32)


def _pooled(p, halo, tm, gd):
    prev = jnp.where(pl.program_id(0) == 0, 0.0, halo)
    ext = jnp.concatenate([prev, p], axis=0)
    out = []
    for g, win in enumerate(B_WINDOWS):
        s = ext[:, g * gd : (g + 1) * gd]
        step = 1
        while step < win:
            s = s + pltpu.roll(s, step, 0)
            step *= 2
        total = s[HALO:, :]
        out.append(total / _window_counts(tm, win) - p[:, g * gd : (g + 1) * gd])
    return out


def _halo_spec(t, tm, d, ahead):
    per = tm // HALO
    if ahead:
        return pl.BlockSpec((HALO, d), lambda i: (jnp.minimum((i + 1) * per, t // HALO - 1), 0))
    return pl.BlockSpec((HALO, d), lambda i: (jnp.maximum(i * per - 1, 0), 0))


def _mix_b_fwd(name, x, gpre, gpost, w_in, w_grp, scale, w_out, exchanges=()):
    t, d = x.shape
    gd = d // len(B_WINDOWS)
    tm = _step_rows(t)

    def body(x_ref, xh_ref, gpre_ref, gpost_ref, win_ref, wgrp_ref, scale_ref, wout_ref, o_ref, p_ref, m_ref):
        xv = x_ref[...]
        gain = gpre_ref[...]
        p = _dot(_rms(xv, gain).astype(BF16), win_ref[...])
        p_ref[...] = p
        halo = _dot(_rms(xh_ref[...], gain).astype(BF16), win_ref[...])
        pooled = _pooled(p, halo, tm, gd)
        mixed = jnp.concatenate([_dot(pg.astype(BF16), wgrp_ref[g]) for g, pg in enumerate(pooled)], axis=1)
        m = _dot((mixed * scale_ref[...]).astype(BF16), wout_ref[...])
        m_ref[...] = m
        o_ref[...] = xv + _rms(m, gpost_ref[...])

    return _call(
        name,
        body,
        grid=(t // tm,),
        in_specs=[_rows(tm, d), _halo_spec(t, tm, d, False), _VM, _VM, _VM, _VM, _VM, _VM],
        out_specs=[_rows(tm, d)] * 3,
        out_shape=[jax.ShapeDtypeStruct((t, d), F32)] * 3,
        args=[x, x, gpre, gpost, w_in, w_grp, scale, w_out],
        exchanges=exchanges,
    )


def _mix_b_bwd_hidden(name, dy, m, gpost, p, w_grp, scale, w_out, exchanges=()):
    t, d = dy.shape
    gd = d // len(B_WINDOWS)
    tm = _step_rows(t)

    def body(dy_ref, m_ref, gpost_ref, p_ref, halo_ref, wgrp_ref, scale_ref, wout_ref,
             dq_ref, dwout_ref, dwgrp_ref, dscale_ref, dgain_ref):
        @pl.when(pl.program_id(0) == 0)
        def _():
            for ref in (dwout_ref, dwgrp_ref, dscale_ref, dgain_ref):
                ref[...] = jnp.zeros_like(ref)

        dm, dgain = _rms_bwd(dy_ref[...], m_ref[...], gpost_ref[...])
        dgain_ref[...] += dgain
        dmb = dm.astype(BF16)
        pooled = [pg.astype(BF16) for pg in _pooled(p_ref[...], halo_ref[...], tm, gd)]
        mixed = jnp.concatenate([_dot(pg, wgrp_ref[g]) for g, pg in enumerate(pooled)], axis=1)
        scale = scale_ref[...]
        ms = (mixed * scale).astype(BF16)
        dms = _dot_nt(dmb, wout_ref[...])
        dwout_ref[...] += _dot_tn(ms, dmb)
        dscale_ref[...] += jnp.sum(dms * mixed, axis=0, keepdims=True)
        dmixed = (dms * scale).astype(BF16)
        for g, win in enumerate(B_WINDOWS):
            cols = slice(g * gd, (g + 1) * gd)
            dmg = dmixed[:, cols]
            dwgrp_ref[g] += _dot_tn(pooled[g], dmg)
            dq_ref[:, cols] = _dot_nt(dmg, wgrp_ref[g]) / _window_counts(tm, win)

    return _call(
        name,
        body,
        grid=(t // tm,),
        in_specs=[_rows(tm, d), _rows(tm, d), _VM, _rows(tm, d), _halo_spec(t, tm, d, False), _VM, _VM, _VM],
        out_specs=[_rows(tm, d), _VM, _VM, _VM, _VM],
        out_shape=[
            jax.ShapeDtypeStruct((t, d), F32),
            jax.ShapeDtypeStruct((d, d), F32),
            jax.ShapeDtypeStruct((len(B_WINDOWS), gd, gd), F32),
            jax.ShapeDtypeStruct((1, d), F32),
            jax.ShapeDtypeStruct((1, d), F32),
        ],
        args=[dy, m, gpost, p, p, w_grp, scale, w_out],
        exchanges=exchanges,
    )


def _mix_b_bwd_in(name, dres, x, gpre, dq, w_in, exchanges=()):
    t, d = dq.shape
    gd = d // len(B_WINDOWS)
    tm = _step_rows(t)
    n_steps = t // tm

    def body(dres_ref, x_ref, gpre_ref, dq_ref, halo_ref, win_ref, dx_ref, dw_ref, dgain_ref):
        @pl.when(pl.program_id(0) == 0)
        def _():
            dw_ref[...] = jnp.zeros_like(dw_ref)
            dgain_ref[...] = jnp.zeros_like(dgain_ref)

        dq_blk = dq_ref[...]
        nxt = jnp.where(pl.program_id(0) == n_steps - 1, 0.0, halo_ref[...])
        ext = jnp.concatenate([dq_blk, nxt], axis=0)
        parts = []
        for g, win in enumerate(B_WINDOWS):
            cols = slice(g * gd, (g + 1) * gd)
            s = ext[:, cols]
            step = 1
            while step < win:
                s = s + pltpu.roll(s, tm + HALO - step, 0)
                step *= 2
            parts.append((s[:tm, :] - dq_blk[:, cols] * _window_counts(tm, win)).astype(BF16))
        dp = jnp.concatenate(parts, axis=1)
        xv = x_ref[...]
        gain = gpre_ref[...]
        hb = _rms(xv, gain).astype(BF16)
        dw_ref[...] += _dot_tn(hb, dp)
        dx, dgain = _rms_bwd(_dot_nt(dp, win_ref[...]), xv, gain)
        dx_ref[...] = dres_ref[...] + dx
        dgain_ref[...] += dgain

    return _call(
        name,
        body,
        grid=(n_steps,),
        in_specs=[_rows(tm, d), _rows(tm, d), _VM, _rows(tm, d), _halo_spec(t, tm, d, True), _VM],
        out_specs=[_rows(tm, d), _VM, _VM],
        out_shape=[
            jax.ShapeDtypeStruct((t, d), F32),
            jax.ShapeDtypeStruct((d, d), F32),
            jax.ShapeDtypeStruct((1, d), F32),
        ],
        args=[dres, x, gpre, dq, dq, w_in],
        exchanges=exchanges,
    )


def _cast_into_slots(name, place, w, dtype):
    n_layers, r, c = w.shape

    def body(place_ref, w_ref, *o_refs):
        del place_ref
        for j, o_ref in enumerate(o_refs):
            @pl.when(pl.program_id(0) == j)
            def _():
                o_ref[...] = w_ref[...].astype(dtype)

    return pl.pallas_call(
        body,
        name=name,
        grid_spec=pltpu.PrefetchScalarGridSpec(
            num_scalar_prefetch=1,
            grid=(n_layers,),
            in_specs=[pl.BlockSpec((1, r, c), lambda i, place_ref: (i, 0, 0))],
            out_specs=[pl.BlockSpec((1, r, c), lambda i, place_ref: (place_ref[0], 0, 0))] * n_layers,
        ),
        out_shape=[jax.ShapeDtypeStruct((N_CHIPS, r, c), dtype)] * n_layers,
        compiler_params=_params(),
    )(place, w)


def _row_tile(r):
    return 256 if r % 256 == 0 else r


def _pair_sum(name, place, dw, recv):
    _, r, c = dw.shape
    half = r // 2
    tr = _row_tile(half)
    per = half // tr

    def body(place_ref, a_ref, b_ref, o_ref):
        del place_ref
        o_ref[...] = (a_ref[...] + b_ref[...]).astype(BF16)

    return pl.pallas_call(
        body,
        name=name,
        grid_spec=pltpu.PrefetchScalarGridSpec(
            num_scalar_prefetch=1,
            grid=(N_CHIPS, per),
            in_specs=[
                pl.BlockSpec((1, tr, c), lambda k, i, place_ref: (k, place_ref[1] * per + i, 0)),
                pl.BlockSpec((1, tr, c), lambda k, i, place_ref: (k, i, 0)),
            ],
            out_specs=pl.BlockSpec((1, tr, c), lambda k, i, place_ref: (k, i, 0)),
        ),
        out_shape=jax.ShapeDtypeStruct(recv.shape, BF16),
        compiler_params=pltpu.CompilerParams(
            dimension_semantics=("arbitrary",) * 2, vmem_limit_bytes=VMEM_LIMIT_BYTES),
    )(place, dw, recv)


def _chip_sum(name, place, mine, others, exchanges=()):
    n_layers = len(mine)
    _, half, c = mine[0].shape
    tr = _row_tile(half)
    per = half // tr

    def body(place_ref, *refs):
        del place_ref
        o_ref = refs[-1]
        for j in range(n_layers):
            @pl.when(pl.program_id(0) == j)
            def _():
                parts = refs[4 * j : 4 * j + 4]
                acc = parts[0][...].astype(F32) + parts[1][...].astype(F32)
                acc = acc + parts[2][...].astype(F32)
                o_ref[...] = acc + parts[3][...].astype(F32)

    def part(j, flip):
        return pl.BlockSpec((1, tr, c), lambda l, i, place_ref: (
            jnp.bitwise_xor(place_ref[0], flip), jnp.where(l == j, i, 0), 0))

    args = []
    for j in range(n_layers):
        args += [mine[j], others[j], others[j], others[j]]
    (total,), ex_outs = _call(
        name,
        body,
        grid=(n_layers, per),
        in_specs=[part(j, flip) for j in range(n_layers) for flip in range(N_CHIPS)],
        out_specs=[pl.BlockSpec((1, tr, c), lambda l, i, place_ref: (l, place_ref[1] * per + i, 0))],
        out_shape=[jax.ShapeDtypeStruct((n_layers, 2 * half, c), F32)],
        args=args,
        prefetch=[place],
        exchanges=exchanges,
    )
    return total, ex_outs


def _adamw(name, w, g, m, v, exchanges=()):
    n_layers, r, c = w.shape
    tr = _row_tile(r)

    def body(w_ref, g_ref, m_ref, v_ref, go_ref, d_ref, nm_ref, nv_ref):
        gv = g_ref[...]
        go_ref[...] = gv
        nm = ADAM_B1 * m_ref[...] + (1.0 - ADAM_B1) * gv
        nv = ADAM_B2 * v_ref[...] + (1.0 - ADAM_B2) * jnp.square(gv)
        m_hat = nm / (1.0 - ADAM_B1 ** ADAM_STEP)
        v_hat = nv / (1.0 - ADAM_B2 ** ADAM_STEP)
        d_ref[...] = -ADAM_LR * (m_hat / (jnp.sqrt(v_hat) + ADAM_EPS) + ADAM_WD * w_ref[...])
        nm_ref[...] = nm
        nv_ref[...] = nv

    spec = pl.BlockSpec((1, tr, c), lambda l, i: (l, i, 0))
    return _call(
        name,
        body,
        grid=(n_layers, r // tr),
        in_specs=[spec] * 4,
        out_specs=[spec] * 4,
        out_shape=[jax.ShapeDtypeStruct(w.shape, F32)] * 4,
        args=[w, g, m, v],
        exchanges=exchanges,
    )


def _all_sum_small(packed):
    m_per, n = packed.shape

    def body(x_ref, sum_ref, all_ref, send_sems, recv_sems, local_sem):
        x, y, c = _position()
        me, sibling = (x, y, c), (x, y, 1 - c)
        chips = _other_chips(x, y)

        def rows(px, py, pc):
            return all_ref.at[pl.ds((4 * px + 2 * py + pc) * m_per, m_per), :]

        def copy(k, block, to, src=None):
            return pltpu.make_async_remote_copy(
                src_ref=rows(*block) if src is None else src, dst_ref=rows(*block),
                send_sem=send_sems.at[k], recv_sem=recv_sems.at[k], device_id=to, device_id_type=MESH)

        mine = pltpu.make_async_copy(x_ref, rows(*me), local_sem)
        mine.start()
        first = [copy(0, me, sibling, src=x_ref)]
        first += [copy(1 + j, me, (*chip, c), src=x_ref) for j, chip in enumerate(chips)]
        for cp in first:
            cp.start()
        passed = [copy(4 + j, (*chip, c), sibling) for j, chip in enumerate(chips)]
        for j, chip in enumerate(chips):
            copy(1 + j, (*chip, c), me).wait_recv()
            passed[j].start()
        copy(0, sibling, me).wait_recv()
        for j, chip in enumerate(chips):
            copy(4 + j, (*chip, 1 - c), me).wait_recv()
        for cp in first + passed:
            cp.wait_send()
        mine.wait()
        acc = all_ref[0:m_per, :]
        for k in range(1, N_DEV):
            acc = acc + all_ref[k * m_per : (k + 1) * m_per, :]
        sum_ref[...] = acc

    return pl.pallas_call(
        body,
        name="all_sum_small",
        in_specs=[_VM],
        out_specs=_VM,
        out_shape=jax.ShapeDtypeStruct((m_per, n), F32),
        scratch_shapes=[
            pltpu.VMEM((N_DEV * m_per, n), F32),
            pltpu.SemaphoreType.DMA((7,)),
            pltpu.SemaphoreType.DMA((7,)),
            pltpu.SemaphoreType.DMA,
        ],
        compiler_params=pltpu.CompilerParams(vmem_limit_bytes=VMEM_LIMIT_BYTES),
    )(packed)


SHARDED = ("a_w_in", "a_w_out", "b_w_in", "b_w_grp", "b_scale", "b_w_out", "ffn_w_gate", "ffn_w_up", "ffn_w_down")
SMALL = ("a_ln_g", "a_ln_b", "a_w_s", "a_b_s", "mix_pre_g", "mix_post_g", "ffn_pre_g", "ffn_post_g")
WEIGHTS = ("a_w_in", "a_ln_g", "a_ln_b", "a_w_s", "a_b_s", "a_w_out", "b_w_in", "b_w_grp", "b_scale", "b_w_out",
           "mix_pre_g", "mix_post_g", "ffn_pre_g", "ffn_post_g", "ffn_w_gate", "ffn_w_up", "ffn_w_down")


TRANSPOSED = ("ffn_w_gate", "ffn_w_up")


def _as_layers(name, a):
    if name in TRANSPOSED:
        return jnp.swapaxes(a, 1, 2)
    if a.ndim == 2:
        return a.reshape(a.shape[0], 1, a.shape[1])
    return a.reshape(a.shape[0], -1, a.shape[-1])


def _from_layers(name, a, shape):
    if name in TRANSPOSED:
        return jnp.swapaxes(a, 1, 2)
    return a.reshape(shape)


def _pack_small(parts):
    return jnp.concatenate([p.reshape(-1, 128) for p in parts], axis=0)


def _unpack_small(packed, like):
    out, row = [], 0
    for ref in like:
        rows = ref.size // 128
        out.append(packed[row : row + rows].reshape(ref.shape))
        row += rows
    return out


def kernel(x, a_w_in, a_ln_g, a_ln_b, a_w_s, a_b_s, a_w_out, b_w_in, b_w_grp, b_scale, b_w_out, mix_pre_g, mix_post_g, ffn_pre_g, ffn_post_g, ffn_w_gate, ffn_w_up, ffn_w_down, loss_target, m_a_w_in, m_a_ln_g, m_a_ln_b, m_a_w_s, m_a_b_s, m_a_w_out, m_b_w_in, m_b_w_grp, m_b_scale, m_b_w_out, m_mix_pre_g, m_mix_post_g, m_ffn_pre_g, m_ffn_post_g, m_ffn_w_gate, m_ffn_w_up, m_ffn_w_down, v_a_w_in, v_a_ln_g, v_a_ln_b, v_a_w_s, v_a_b_s, v_a_w_out, v_b_w_in, v_b_w_grp, v_b_scale, v_b_w_out, v_mix_pre_g, v_mix_post_g, v_ffn_pre_g, v_ffn_post_g, v_ffn_w_gate, v_ffn_w_up, v_ffn_w_down):
    weights = dict(a_w_in=a_w_in, a_ln_g=a_ln_g, a_ln_b=a_ln_b, a_w_s=a_w_s, a_b_s=a_b_s, a_w_out=a_w_out,
                   b_w_in=b_w_in, b_w_grp=b_w_grp, b_scale=b_scale, b_w_out=b_w_out, mix_pre_g=mix_pre_g,
                   mix_post_g=mix_post_g, ffn_pre_g=ffn_pre_g, ffn_post_g=ffn_post_g, ffn_w_gate=ffn_w_gate,
                   ffn_w_up=ffn_w_up, ffn_w_down=ffn_w_down)
    mom1 = dict(a_w_in=m_a_w_in, a_ln_g=m_a_ln_g, a_ln_b=m_a_ln_b, a_w_s=m_a_w_s, a_b_s=m_a_b_s, a_w_out=m_a_w_out,
                b_w_in=m_b_w_in, b_w_grp=m_b_w_grp, b_scale=m_b_scale, b_w_out=m_b_w_out, mix_pre_g=m_mix_pre_g,
                mix_post_g=m_mix_post_g, ffn_pre_g=m_ffn_pre_g, ffn_post_g=m_ffn_post_g, ffn_w_gate=m_ffn_w_gate,
                ffn_w_up=m_ffn_w_up, ffn_w_down=m_ffn_w_down)
    mom2 = dict(a_w_in=v_a_w_in, a_ln_g=v_a_ln_g, a_ln_b=v_a_ln_b, a_w_s=v_a_w_s, a_b_s=v_a_b_s, a_w_out=v_a_w_out,
                b_w_in=v_b_w_in, b_w_grp=v_b_w_grp, b_scale=v_b_scale, b_w_out=v_b_w_out, mix_pre_g=v_mix_pre_g,
                mix_post_g=v_mix_post_g, ffn_pre_g=v_ffn_pre_g, ffn_post_g=v_ffn_post_g, ffn_w_gate=v_ffn_w_gate,
                ffn_w_up=v_ffn_w_up, ffn_w_down=v_ffn_w_down)

    t, d = x.shape[1], x.shape[2]
    depth = mix_pre_g.shape[0]
    gd_b = d // len(B_WINDOWS)
    xs = x.reshape(t, d)
    target = loss_target.reshape(t, d)

    chip = 2 * lax.axis_index("x") + lax.axis_index("y")
    place = jnp.stack([chip, lax.axis_index("c")]).astype(jnp.int32)
    bufs = {name: list(_cast_into_slots("cast_" + name, place, _as_layers(name, weights[name]),
                                        F32 if name == "b_scale" else BF16)) for name in SHARDED}

    def gain(name, i):
        return weights[name][i].reshape(1, d)

    def weight_keys(i):
        j = i // 2
        mixer = [("a_w_in", j), ("a_w_out", j)] if i % 2 == 0 else [("b_w_in", j), ("b_w_grp", j), ("b_w_out", j)]
        return mixer, [("ffn_w_gate", i), ("ffn_w_up", i), ("ffn_w_down", i)]

    def gather(keys):
        return _GatherWeights([bufs[n][j] for n, j in keys],
                              whole=[k for k, (n, _) in enumerate(keys) if n == "b_scale"])

    def gathered(keys, outs):
        for (n, j), buf in zip(keys, outs):
            bufs[n][j] = buf

    first = weight_keys(0)[0] + [("b_scale", j) for j in range(b_scale.shape[0])]
    gathered(first, _exchange("gather_first", [gather(first)])[0])
    saved = []
    cur = xs
    for i in range(depth):
        j = i // 2
        mixer_next, ffn_next = weight_keys(i + 1) if i + 1 < depth else ([], [])
        ffn_keys = weight_keys(i)[1]
        after_ffn = mixer_next if (i + 1) % 2 == 0 else mixer_next + ffn_next
        if i % 2 == 0:
            w_in = bufs["a_w_in"][j]
            q = w_in.shape[2]
            w_out = bufs["a_w_out"][j].reshape(2, q, d)
            ln_g = a_ln_g[j].reshape(2, q)
            ln_b = a_ln_b[j].reshape(2, q)
            b_t = jnp.transpose(a_b_s[j])
            (nxt, act, slope, m), (got,) = _mix_a_fwd(
                f"mix_a_fwd{j}", cur, gain("mix_pre_g", i), gain("mix_post_g", i), w_in, ln_g, ln_b, a_w_s[j], b_t,
                w_out, exchanges=[gather(ffn_keys)])
            gathered(ffn_keys, got)
            mix_saved = dict(x=cur, act=act, slope=slope, m=m, w_in=w_in, w_out=w_out, ln_g=ln_g, ln_b=ln_b, b_t=b_t)
        else:
            w_in = bufs["b_w_in"][j].reshape(d, d)
            w_out = bufs["b_w_out"][j].reshape(d, d)
            w_grp = jnp.transpose(bufs["b_w_grp"][j].reshape(N_CHIPS, len(B_WINDOWS), gd_b // N_CHIPS, gd_b),
                                  (1, 0, 2, 3)).reshape(len(B_WINDOWS), gd_b, gd_b)
            scale = bufs["b_scale"][j].reshape(1, d)
            (nxt, p, m), _ = _mix_b_fwd(f"mix_b_fwd{j}", cur, gain("mix_pre_g", i), gain("mix_post_g", i),
                                        w_in, w_grp, scale, w_out)
            mix_saved = dict(x=cur, p=p, m=m, w_in=w_in, w_out=w_out, w_grp=w_grp, scale=scale)
        cur = nxt
        wg, wu, wd = (bufs[n][k].reshape(1, -1, d) for n, k in ffn_keys)
        (nxt, a, dup, dgate, f, *sq), (got,) = _ffn_fwd(
            f"ffn_fwd{i}", cur, gain("ffn_pre_g", i), gain("ffn_post_g", i), wg, wu, wd,
            target=target if i == depth - 1 else None, exchanges=[gather(after_ffn)])
        gathered(after_ffn, got)
        saved.append((mix_saved, dict(x=cur, a=a, dup=dup, dgate=dgate, f=f, wg=wg, wu=wu, wd=wd)))
        cur = nxt

    dcur = cur
    loss = lax.psum(0.5 * sq[0][0, 0] / d, ("x", "y", "c"))

    grads = {name: [None] * weights[name].shape[0] for name in WEIGHTS}
    state = dict(to_sibling=[], to_chips=[])
    pair, from_chips = {}, {}

    def exchanges_due(carry=True):
        if not carry:
            return []
        return [_ToSibling([a for _, _, a in state["to_sibling"]]), _ToChips([a for _, _, a in state["to_chips"]])]

    def exchanged(outs):
        if not outs:
            return
        from_sibling, arrived = outs
        for (n, k, _), got in zip(state["to_chips"], arrived):
            from_chips[n, k] = got
        state["to_chips"] = []
        for (n, k, dw), got in zip(state["to_sibling"], from_sibling):
            pair[n, k] = _pair_sum(f"pair_sum_{n}{k}", place, dw, got)
            state["to_chips"].append((n, k, pair[n, k]))
        state["to_sibling"] = []

    def made(name, k, dw):
        grads[name][k] = dw
        state["to_sibling"].append((name, k, dw))

    for i in reversed(range(depth)):
        j = i // 2
        mix_saved, ffn_saved = saved[i]
        s = ffn_saved
        both = i == 0
        (dg, du, dwd, dgain), outs = _ffn_bwd_hidden(
            f"ffn_bwd_hidden{i}", dcur, s["f"], gain("ffn_post_g", i), s["a"], s["dup"], s["dgate"], s["wd"],
            exchanges=exchanges_due(both))
        exchanged(outs)
        grads["ffn_post_g"][i] = dgain
        made("ffn_w_down", i, dwd.reshape(N_CHIPS, -1, d))
        (dcur, dwg, dwu, dgain), outs = _bwd_in(
            f"ffn_bwd_in{i}", dcur, s["x"], gain("ffn_pre_g", i), [dg, du], [s["wg"], s["wu"]], transposed=True,
            exchanges=exchanges_due())
        exchanged(outs)
        grads["ffn_pre_g"][i] = dgain
        made("ffn_w_gate", i, dwg.reshape(N_CHIPS, -1, d))
        made("ffn_w_up", i, dwu.reshape(N_CHIPS, -1, d))
        s = mix_saved
        if i % 2 == 0:
            (dz, dwout, dws, dbacc, dlng, dlnb, dgain), outs = _mix_a_bwd_hidden(
                f"mix_a_bwd_hidden{j}", dcur, s["m"], gain("mix_post_g", i), s["act"], s["slope"], s["ln_g"], s["ln_b"],
                a_w_s[j], s["b_t"], s["w_out"], exchanges=exchanges_due(both))
            exchanged(outs)
            grads["a_w_s"][j] = dws
            grads["a_b_s"][j] = jnp.transpose(dbacc)
            grads["a_ln_g"][j] = dlng.reshape(-1)
            grads["a_ln_b"][j] = dlnb.reshape(-1)
            grads["mix_post_g"][i] = dgain
            made("a_w_out", j, dwout.reshape(N_CHIPS, -1, d))
            (dcur, dwin, dgain), outs = _bwd_in(
                f"mix_a_bwd_in{j}", dcur, s["x"], gain("mix_pre_g", i), [dz], [s["w_in"]], exchanges=exchanges_due())
            exchanged(outs)
            grads["mix_pre_g"][i] = dgain
            made("a_w_in", j, dwin)
        else:
            (dq, dwout, dwgrp, dscale, dgain), outs = _mix_b_bwd_hidden(
                f"mix_b_bwd_hidden{j}", dcur, s["m"], gain("mix_post_g", i), s["p"], s["w_grp"], s["scale"],
                s["w_out"], exchanges=exchanges_due(both))
            exchanged(outs)
            grads["b_scale"][j] = dscale
            grads["mix_post_g"][i] = dgain
            made("b_w_out", j, dwout.reshape(N_CHIPS, -1, d))
            made("b_w_grp", j, jnp.transpose(
                dwgrp.reshape(len(B_WINDOWS), N_CHIPS, gd_b // N_CHIPS, gd_b), (1, 0, 2, 3)).reshape(N_CHIPS, -1, gd_b))
            (dcur, dwin, dgain), outs = _mix_b_bwd_in(
                f"mix_b_bwd_in{j}", dcur, s["x"], gain("mix_pre_g", i), dq, s["w_in"], exchanges=exchanges_due())
            exchanged(outs)
            grads["mix_pre_g"][i] = dgain
            made("b_w_in", j, dwin.reshape(N_CHIPS, -1, d))
    grad_x = dcur.reshape(x.shape)

    exchanged(_exchange("grads_last", exchanges_due()))
    exchanged(_exchange("grads_last_to_chips", exchanges_due()))

    reduced_names = [name for name in SHARDED if name != "b_scale"]
    sums = []
    for name in reduced_names:
        layers = range(weights[name].shape[0])
        sums.append(_chip_sum("chip_sum_" + name, place, [pair[name, k] for k in layers],
                              [from_chips[name, k] for k in layers])[0])
    reduced = dict(zip(reduced_names, _exchange("swap_halves", [_SwapHalves(sums)])[0]))

    out_g, out_d, out_m, out_v = {}, {}, {}, {}
    for name in reduced_names:
        shape = weights[name].shape
        results, _ = _adamw("adamw_" + name, _as_layers(name, weights[name]), reduced[name],
                            _as_layers(name, mom1[name]), _as_layers(name, mom2[name]))
        out_g[name], out_d[name], out_m[name], out_v[name] = (_from_layers(name, a, shape) for a in results)

    small_grads = [jnp.stack([g.reshape(weights[name].shape[1:]) for g in grads[name]], axis=0) for name in SMALL]
    scale_grad = jnp.concatenate(grads["b_scale"], axis=0)
    summed = _all_sum_small(_pack_small(small_grads + [scale_grad]))
    small_rows = summed.shape[0] - scale_grad.size // 128
    scale_sum = summed[small_rows:].reshape(scale_grad.shape)
    scale_mine = lax.dynamic_slice_in_dim(scale_sum, chip * b_scale.shape[1], b_scale.shape[1], axis=1)[:, None, :]
    summed = summed[:small_rows]
    results, _ = _adamw("adamw_b_scale", _as_layers("b_scale", b_scale), scale_mine,
                        _as_layers("b_scale", m_b_scale), _as_layers("b_scale", v_b_scale))
    out_g["b_scale"], out_d["b_scale"], out_m["b_scale"], out_v["b_scale"] = (a.reshape(b_scale.shape) for a in results)
    small_like = [weights[name] for name in SMALL]
    packs = [_pack_small([src[name] for name in SMALL]).reshape(1, -1, 128) for src in (weights, mom1, mom2)]
    results, _ = _adamw("adamw_small", packs[0], summed.reshape(1, -1, 128), packs[1], packs[2])
    for dst, packed in zip((out_g, out_d, out_m, out_v), (a[0] for a in results)):
        for name, val in zip(SMALL, _unpack_small(packed, small_like)):
            dst[name] = val

    return (loss, grad_x, *[out_g[n] for n in WEIGHTS], *[out_d[n] for n in WEIGHTS],
            *[out_m[n] for n in WEIGHTS], *[out_v[n] for n in WEIGHTS])
```

```python
import functools
import math

import jax
import jax.numpy as jnp
from jax import lax
from jax.experimental import pallas as pl
from jax.experimental.pallas import tpu as pltpu

F32 = jnp.float32
BF16 = jnp.bfloat16
MESH = pl.DeviceIdType.MESH

EPS = 1e-6
CHUNK = 128
A_GROUPS = 8
B_WINDOWS = (2, 4, 8, 16)
HALO = 16
N_CHIPS = 4
N_DEV = 8

ADAM_LR = 0.001
ADAM_B1 = 0.9
ADAM_B2 = 0.999
ADAM_EPS = 1e-08
ADAM_WD = 0.01
ADAM_STEP = 10

VMEM_LIMIT_BYTES = 60 * 1024 * 1024
INV_SQRT2 = 1.0 / math.sqrt(2.0)
INV_SQRT_2PI = 1.0 / math.sqrt(2.0 * math.pi)

_ANY = pl.BlockSpec(memory_space=pl.ANY)
_VM = pl.BlockSpec(memory_space=pltpu.VMEM)


def _params():
    return pltpu.CompilerParams(dimension_semantics=("arbitrary",), vmem_limit_bytes=VMEM_LIMIT_BYTES)


def _token_block(t):
    return 256 if t >= 1024 else 128


def _step_rows(t):
    return 2 * _token_block(t)


def _rows(tm, d):
    return pl.BlockSpec((tm, d), lambda i: (i, 0))


def _blocks(nb, tm, bw):
    return pl.BlockSpec((nb, tm, bw), lambda i: (0, i, 0))


def _dot(a, b):
    return lax.dot_general(a, b, (((1,), (0,)), ((), ())), preferred_element_type=F32)


def _dot_nt(a, b):
    return lax.dot_general(a, b, (((1,), (1,)), ((), ())), preferred_element_type=F32)


def _dot_tn(a, b):
    return lax.dot_general(a, b, (((0,), (0,)), ((), ())), preferred_element_type=F32)


def _rms(x, g):
    return x * lax.rsqrt(jnp.mean(x * x, axis=-1, keepdims=True) + EPS) * g


def _rms_bwd(dy, x, g):
    r = lax.rsqrt(jnp.mean(x * x, axis=-1, keepdims=True) + EPS)
    n = x * r
    dn = dy * g
    dx = r * (dn - n * jnp.mean(dn * n, axis=-1, keepdims=True))
    return dx, jnp.sum(dy * n, axis=0, keepdims=True)


def _gelu_and_grad(x):
    cdf = 0.5 * (1.0 + lax.erf(x * INV_SQRT2))
    return x * cdf, cdf + x * (jnp.exp(-0.5 * x * x) * INV_SQRT_2PI)


def _position():
    return lax.axis_index("x"), lax.axis_index("y"), lax.axis_index("c")


def _other_chips(x, y):
    return [(1 - x, y), (x, 1 - y), (1 - x, 1 - y)]


class _GatherWeights:
    def __init__(self, bufs, whole=()):
        self.inputs = list(bufs)
        self.out_shapes = [jax.ShapeDtypeStruct(b.shape, b.dtype) for b in bufs]
        self.aliases = {w: w for w in range(len(bufs))}
        self.n_sems = 6 * len(bufs)
        self.whole = frozenset(whole)

    def _part(self, outs, w, slot, core):
        if w in self.whole:
            return outs[w].at[slot]
        half = outs[w].shape[1] // 2
        return outs[w].at[slot, pl.ds(core * half, half)]

    def _ici(self, outs, send, recv, w, j, slot):
        x, y, c = _position()
        px, py = _other_chips(x, y)[j]
        part = self._part(outs, w, slot, c)
        return pltpu.make_async_remote_copy(
            src_ref=part, dst_ref=part, send_sem=send.at[6 * w + j], recv_sem=recv.at[6 * w + j],
            device_id=(px, py, c), device_id_type=MESH)

    def _d2d(self, outs, send, recv, w, j, slot, core):
        x, y, c = _position()
        part = self._part(outs, w, slot, core)
        return pltpu.make_async_remote_copy(
            src_ref=part, dst_ref=part, send_sem=send.at[6 * w + 3 + j], recv_sem=recv.at[6 * w + 3 + j],
            device_id=(x, y, 1 - c), device_id_type=MESH)

    def start(self, ins, outs, send, recv):
        x, y, _ = _position()
        for w in range(len(outs)):
            for j in range(3):
                self._ici(outs, send, recv, w, j, 2 * x + y).start()

    def finish(self, ins, outs, send, recv):
        x, y, c = _position()
        slots = [2 * px + py for px, py in _other_chips(x, y)]
        for w in range(len(outs)):
            for j, slot in enumerate(slots):
                self._ici(outs, send, recv, w, j, slot).wait_recv()
                if w not in self.whole:
                    self._d2d(outs, send, recv, w, j, slot, c).start()
        for w in range(len(outs)):
            for j, slot in enumerate(slots):
                if w not in self.whole:
                    self._d2d(outs, send, recv, w, j, slot, 1 - c).wait_recv()
        for w in range(len(outs)):
            for j, slot in enumerate(slots):
                self._ici(outs, send, recv, w, j, 2 * x + y).wait_send()
                if w not in self.whole:
                    self._d2d(outs, send, recv, w, j, slot, c).wait_send()


class _ToSibling:
    def __init__(self, grads):
        self.inputs = list(grads)
        self.out_shapes = [jax.ShapeDtypeStruct((g.shape[0], g.shape[1] // 2, g.shape[2]), g.dtype) for g in grads]
        self.aliases = {}
        self.n_sems = len(grads)

    def _copy(self, ins, outs, send, recv, w):
        x, y, c = _position()
        half = ins[w].shape[1] // 2
        return pltpu.make_async_remote_copy(
            src_ref=ins[w].at[:, pl.ds((1 - c) * half, half)], dst_ref=outs[w],
            send_sem=send.at[w], recv_sem=recv.at[w], device_id=(x, y, 1 - c), device_id_type=MESH)

    def start(self, ins, outs, send, recv):
        for w in range(len(ins)):
            self._copy(ins, outs, send, recv, w).start()

    def finish(self, ins, outs, send, recv):
        for w in range(len(ins)):
            self._copy(ins, outs, send, recv, w).wait_recv()
        for w in range(len(ins)):
            self._copy(ins, outs, send, recv, w).wait_send()


class _ToChips:
    def __init__(self, parts):
        self.inputs = list(parts)
        self.out_shapes = [jax.ShapeDtypeStruct(p.shape, p.dtype) for p in parts]
        self.aliases = {}
        self.n_sems = 3 * len(parts)

    def _copy(self, ins, outs, send, recv, w, j, outbound):
        x, y, c = _position()
        px, py = _other_chips(x, y)[j]
        me, peer = 2 * x + y, 2 * px + py
        src_slot, dst_slot = (peer, me) if outbound else (me, peer)
        return pltpu.make_async_remote_copy(
            src_ref=ins[w].at[src_slot], dst_ref=outs[w].at[dst_slot],
            send_sem=send.at[3 * w + j], recv_sem=recv.at[3 * w + j], device_id=(px, py, c), device_id_type=MESH)

    def start(self, ins, outs, send, recv):
        for w in range(len(ins)):
            for j in range(3):
                self._copy(ins, outs, send, recv, w, j, True).start()

    def finish(self, ins, outs, send, recv):
        for w in range(len(ins)):
            for j in range(3):
                self._copy(ins, outs, send, recv, w, j, False).wait_recv()
        for w in range(len(ins)):
            for j in range(3):
                self._copy(ins, outs, send, recv, w, j, True).wait_send()


class _SwapHalves:
    def __init__(self, bufs):
        self.inputs = list(bufs)
        self.out_shapes = [jax.ShapeDtypeStruct(b.shape, b.dtype) for b in bufs]
        self.aliases = {w: w for w in range(len(bufs))}
        self.n_sems = len(bufs)

    def _copy(self, outs, send, recv, w, core):
        x, y, c = _position()
        half = outs[w].shape[1] // 2
        rows = outs[w].at[:, pl.ds(core * half, half)]
        return pltpu.make_async_remote_copy(
            src_ref=rows, dst_ref=rows, send_sem=send.at[w], recv_sem=recv.at[w],
            device_id=(x, y, 1 - c), device_id_type=MESH)

    def start(self, ins, outs, send, recv):
        c = lax.axis_index("c")
        for w in range(len(outs)):
            self._copy(outs, send, recv, w, c).start()

    def finish(self, ins, outs, send, recv):
        c = lax.axis_index("c")
        for w in range(len(outs)):
            self._copy(outs, send, recv, w, 1 - c).wait_recv()
        for w in range(len(outs)):
            self._copy(outs, send, recv, w, c).wait_send()


def _call(name, body, *, grid, in_specs, out_specs, out_shape, args, scratch_shapes=(), prefetch=(), exchanges=()):
    given = list(exchanges)
    exchanges = [e for e in given if e.inputs]
    n_pre, n_in, n_out, n_scr = len(prefetch), len(args), len(out_shape), len(scratch_shapes)
    ex_in = [a for e in exchanges for a in e.inputs]
    ex_out = [s for e in exchanges for s in e.out_shapes]
    aliases = {}
    at_in, at_out = n_pre + n_in, n_out
    for e in exchanges:
        for i, o in e.aliases.items():
            aliases[at_in + i] = at_out + o
        at_in += len(e.inputs)
        at_out += len(e.out_shapes)

    def at_step(last):
        hit = None
        for axis, n in enumerate(grid):
            here = pl.program_id(axis) == (n - 1 if last else 0)
            hit = here if hit is None else jnp.logical_and(hit, here)
        return hit

    def fused(*refs):
        pre, refs = refs[:n_pre], refs[n_pre:]
        body_in, refs = refs[:n_in], refs[n_in:]
        ex_in_refs, refs = refs[: len(ex_in)], refs[len(ex_in) :]
        body_out, refs = refs[:n_out], refs[n_out:]
        ex_out_refs, refs = refs[: len(ex_out)], refs[len(ex_out) :]
        body_scr, sems = refs[:n_scr], refs[n_scr:]

        def each(stage):
            a = b = 0
            for n, e in enumerate(exchanges):
                ins, outs = ex_in_refs[a : a + len(e.inputs)], ex_out_refs[b : b + len(e.out_shapes)]
                getattr(e, stage)(ins, outs, sems[2 * n], sems[2 * n + 1])
                a += len(e.inputs)
                b += len(e.out_shapes)

        if exchanges:
            @pl.when(at_step(False))
            def _():
                each("start")

        if body is not None:
            body(*pre, *body_in, *body_out, *body_scr)

        if exchanges:
            @pl.when(at_step(True))
            def _():
                each("finish")

    outs = pl.pallas_call(
        fused,
        name=name,
        grid_spec=pltpu.PrefetchScalarGridSpec(
            num_scalar_prefetch=n_pre,
            grid=grid,
            in_specs=list(in_specs) + [_ANY] * len(ex_in),
            out_specs=list(out_specs) + [_ANY] * len(ex_out),
            scratch_shapes=list(scratch_shapes)
            + [pltpu.SemaphoreType.DMA((e.n_sems,)) for e in exchanges for _ in range(2)],
        ),
        out_shape=list(out_shape) + ex_out,
        input_output_aliases=aliases,
        compiler_params=pltpu.CompilerParams(
            dimension_semantics=("arbitrary",) * len(grid), vmem_limit_bytes=VMEM_LIMIT_BYTES),
    )(*prefetch, *args, *ex_in)
    body_outs, rest = list(outs[:n_out]), list(outs[n_out:])
    ex_outs = []
    for e in given:
        n_e = len(e.out_shapes) if e.inputs else 0
        ex_outs.append(rest[:n_e])
        rest = rest[n_e:]
    return body_outs, ex_outs


def _exchange(name, exchanges):
    return _call(name, None, grid=(1,), in_specs=[], out_specs=[], out_shape=[], args=[], exchanges=exchanges)[1]


def _ffn_fwd(name, x, gpre, gpost, wg, wu, wd, target=None, exchanges=()):
    t, d = x.shape
    nb, fs, _ = wg.shape
    sub = _token_block(t)
    tm = _step_rows(t)
    with_loss = target is not None

    def body(x_ref, gpre_ref, gpost_ref, wg_ref, wu_ref, wd_ref, *refs):
        if with_loss:
            t_ref, o_ref, a_ref, dup_ref, dgate_ref, f_ref, sq_ref = refs

            @pl.when(pl.program_id(0) == 0)
            def _():
                sq_ref[...] = jnp.zeros_like(sq_ref)
        else:
            o_ref, a_ref, dup_ref, dgate_ref, f_ref = refs
        for h in range(tm // sub):
            rows = slice(h * sub, (h + 1) * sub)
            xv = x_ref[rows, :]
            hb = _rms(xv, gpre_ref[...]).astype(BF16)
            f = jnp.zeros((sub, d), F32)
            for k in range(nb):
                g = _dot_nt(hb, wg_ref[k])
                u = _dot_nt(hb, wu_ref[k])
                s = jax.nn.sigmoid(g)
                sg = g * s
                a = (sg * u).astype(BF16)
                a_ref[k, rows, :] = a
                dup_ref[k, rows, :] = sg.astype(BF16)
                dgate_ref[k, rows, :] = (u * (s * (1.0 + g * (1.0 - s)))).astype(BF16)
                f = f + _dot(a, wd_ref[k])
            f_ref[rows, :] = f
            y = xv + _rms(f, gpost_ref[...])
            if with_loss:
                err = y - t_ref[rows, :]
                o_ref[rows, :] = err / d
                sq_ref[...] += jnp.sum(err * err)
            else:
                o_ref[rows, :] = y

    return _call(
        name,
        body,
        grid=(t // tm,),
        in_specs=[_rows(tm, d), _VM, _VM, _VM, _VM, _VM] + [_rows(tm, d)] * with_loss,
        out_specs=[_rows(tm, d)] + [_blocks(nb, tm, fs)] * 3 + [_rows(tm, d)] + [_VM] * with_loss,
        out_shape=[jax.ShapeDtypeStruct((t, d), F32)]
        + [jax.ShapeDtypeStruct((nb, t, fs), BF16)] * 3
        + [jax.ShapeDtypeStruct((t, d), F32)]
        + [jax.ShapeDtypeStruct((8, 128), F32)] * with_loss,
        args=[x, gpre, gpost, wg, wu, wd] + [target] * with_loss,
        exchanges=exchanges,
    )


def _ffn_bwd_hidden(name, dy, f, gpost, a, dup, dgate, wd, exchanges=()):
    t, d = dy.shape
    nb, fs, _ = wd.shape
    tm = _step_rows(t)

    def body(dy_ref, f_ref, gpost_ref, a_ref, dup_ref, dgate_ref, wd_ref, dg_ref, du_ref, dwd_ref, dgain_ref):
        @pl.when(pl.program_id(0) == 0)
        def _():
            dwd_ref[...] = jnp.zeros_like(dwd_ref)
            dgain_ref[...] = jnp.zeros_like(dgain_ref)

        df, dgain = _rms_bwd(dy_ref[...], f_ref[...], gpost_ref[...])
        dgain_ref[...] += dgain
        dfb = df.astype(BF16)
        for k in range(nb):
            da = _dot_nt(dfb, wd_ref[k])
            dwd_ref[k] += _dot_tn(a_ref[k], dfb)
            du_ref[k] = (da * dup_ref[k].astype(F32)).astype(BF16)
            dg_ref[k] = (da * dgate_ref[k].astype(F32)).astype(BF16)

    return _call(
        name,
        body,
        grid=(t // tm,),
        in_specs=[_rows(tm, d), _rows(tm, d), _VM] + [_blocks(nb, tm, fs)] * 3 + [_VM],
        out_specs=[_blocks(nb, tm, fs), _blocks(nb, tm, fs), _VM, _VM],
        out_shape=[
            jax.ShapeDtypeStruct((nb, t, fs), BF16),
            jax.ShapeDtypeStruct((nb, t, fs), BF16),
            jax.ShapeDtypeStruct((nb, fs, d), F32),
            jax.ShapeDtypeStruct((1, d), F32),
        ],
        args=[dy, f, gpost, a, dup, dgate, wd],
        exchanges=exchanges,
    )


def _bwd_in(name, dres, x, gpre, dzs, ws, transposed=False, exchanges=()):
    t, d = x.shape
    n = len(ws)
    resident = sum(6 * w.size for w in ws)
    tm = _step_rows(t) if resident <= VMEM_LIMIT_BYTES // 2 else _token_block(t)
    widths = [w.shape[1] if transposed else w.shape[2] for w in ws]

    def body(*refs):
        dres_ref, x_ref, gpre_ref = refs[:3]
        dz_refs = refs[3 : 3 + n]
        w_refs = refs[3 + n : 3 + 2 * n]
        dx_ref = refs[3 + 2 * n]
        dw_refs = refs[4 + 2 * n : 4 + 3 * n]
        dgain_ref = refs[4 + 3 * n]

        @pl.when(pl.program_id(0) == 0)
        def _():
            for dw_ref in dw_refs:
                dw_ref[...] = jnp.zeros_like(dw_ref)
            dgain_ref[...] = jnp.zeros_like(dgain_ref)

        xv = x_ref[...]
        gain = gpre_ref[...]
        hb = _rms(xv, gain).astype(BF16)
        dh = jnp.zeros((tm, d), F32)
        for dz_ref, w_ref, dw_ref in zip(dz_refs, w_refs, dw_refs):
            for k in range(w_ref.shape[0]):
                dz = dz_ref[k]
                if transposed:
                    dh = dh + _dot(dz, w_ref[k])
                    dw_ref[k] += _dot_tn(dz, hb)
                else:
                    dh = dh + _dot_nt(dz, w_ref[k])
                    dw_ref[k] += _dot_tn(hb, dz)
        dx, dgain = _rms_bwd(dh, xv, gain)
        dx_ref[...] = dres_ref[...] + dx
        dgain_ref[...] += dgain

    return _call(
        name,
        body,
        grid=(t // tm,),
        in_specs=[_rows(tm, d), _rows(tm, d), _VM]
        + [_blocks(w.shape[0], tm, bw) for w, bw in zip(ws, widths)]
        + [_VM] * n,
        out_specs=[_rows(tm, d)] + [_VM] * n + [_VM],
        out_shape=[jax.ShapeDtypeStruct((t, d), F32)]
        + [jax.ShapeDtypeStruct(w.shape, F32) for w in ws]
        + [jax.ShapeDtypeStruct((1, d), F32)],
        args=[dres, x, gpre, *dzs, *ws],
        exchanges=exchanges,
    )


def _causal_weights(ws_ref):
    row = lax.broadcasted_iota(jnp.int32, (CHUNK, CHUNK), 0)
    col = lax.broadcasted_iota(jnp.int32, (CHUNK, CHUNK), 1)
    return [jnp.where(row >= col, ws_ref[g], 0.0).astype(BF16) for g in range(A_GROUPS)]


def _layernorm_halves(v0, v1):
    width = v0.shape[-1] + v1.shape[-1]
    mu = (jnp.sum(v0, axis=-1, keepdims=True) + jnp.sum(v1, axis=-1, keepdims=True)) / width
    c0 = v0 - mu
    c1 = v1 - mu
    var = (jnp.sum(c0 * c0, axis=-1, keepdims=True) + jnp.sum(c1 * c1, axis=-1, keepdims=True)) / width
    rstd = lax.rsqrt(var + EPS)
    return c0 * rstd, c1 * rstd, rstd


def _spatial_gate(sv_ref, wtril, vl, bt_ref, half, tm, gd):
    for gg in range(A_GROUPS // 2):
        g = half * (A_GROUPS // 2) + gg
        bias = bt_ref[:, g : g + 1]
        for n in range(tm // CHUNK):
            blk = vl[n * CHUNK : (n + 1) * CHUNK, gg * gd : (gg + 1) * gd]
            sv_ref[n * CHUNK : (n + 1) * CHUNK, gg * gd : (gg + 1) * gd] = _dot(wtril[g], blk) + bias


def _mix_a_fwd(name, x, gpre, gpost, w_in, ln_g, ln_b, w_s, b_t, w_out, exchanges=()):
    t, d = x.shape
    _, _, q = w_in.shape
    gd = 2 * q // A_GROUPS
    tm = _step_rows(t)

    def body(x_ref, gpre_ref, gpost_ref, win_ref, lng_ref, lnb_ref, ws_ref, bt_ref, wout_ref,
             o_ref, z_ref, dz_ref, m_ref, sv_ref):
        xv = x_ref[...]
        hb = _rms(xv, gpre_ref[...]).astype(BF16)
        z = [None] * 4
        vls = []
        for k in (2, 3, 0, 1):
            act, slope = _gelu_and_grad(_dot(hb, win_ref[k]))
            z_ref[k] = act.astype(BF16)
            dz_ref[k] = slope.astype(BF16)
            z[k] = act
            if k == 3:
                vh0, vh1, _ = _layernorm_halves(z[2], z[3])
                vls = [(vh * lng_ref[b : b + 1, :] + lnb_ref[b : b + 1, :]).astype(BF16)
                       for b, vh in enumerate((vh0, vh1))]
        wtril = _causal_weights(ws_ref)
        m = jnp.zeros((tm, d), F32)
        for b in range(2):
            _spatial_gate(sv_ref, wtril, vls[b], bt_ref, b, tm, gd)
            gated = (z[b] * sv_ref[...]).astype(BF16)
            m = m + _dot(gated, wout_ref[b])
        m_ref[...] = m
        o_ref[...] = xv + _rms(m, gpost_ref[...])

    return _call(
        name,
        body,
        grid=(t // tm,),
        in_specs=[_rows(tm, d)] + [_VM] * 8,
        out_specs=[_rows(tm, d), _blocks(4, tm, q), _blocks(4, tm, q), _rows(tm, d)],
        out_shape=[
            jax.ShapeDtypeStruct((t, d), F32),
            jax.ShapeDtypeStruct((4, t, q), BF16),
            jax.ShapeDtypeStruct((4, t, q), BF16),
            jax.ShapeDtypeStruct((t, d), F32),
        ],
        scratch_shapes=[pltpu.VMEM((tm, q), F32)],
        args=[x, gpre, gpost, w_in, ln_g, ln_b, w_s, b_t, w_out],
        exchanges=exchanges,
    )


def _mix_a_bwd_hidden(name, dy, m, gpost, act, slope, ln_g, ln_b, w_s, b_t, w_out, exchanges=()):
    t, d = dy.shape
    _, _, q = act.shape
    gd = 2 * q // A_GROUPS
    tm = _step_rows(t)
    n_chunks = tm // CHUNK

    def body(dy_ref, m_ref, gpost_ref, z_ref, slope_ref, lng_ref, lnb_ref, ws_ref, bt_ref, wout_ref,
             dz_ref, dwout_ref, dws_ref, dbacc_ref, dlng_ref, dlnb_ref, dgain_ref, sv_ref, dvl_ref):
        first = pl.program_id(0) == 0

        @pl.when(first)
        def _():
            for ref in (dwout_ref, dws_ref, dbacc_ref, dlng_ref, dlnb_ref, dgain_ref):
                ref[...] = jnp.zeros_like(ref)

        dm, dgain = _rms_bwd(dy_ref[...], m_ref[...], gpost_ref[...])
        dgain_ref[...] += dgain
        dmb = dm.astype(BF16)
        vhs = list(_layernorm_halves(z_ref[2].astype(F32), z_ref[3].astype(F32)))
        rstd = vhs.pop()
        vls = [(vh * lng_ref[b : b + 1, :] + lnb_ref[b : b + 1, :]).astype(BF16) for b, vh in enumerate(vhs)]
        wtril = _causal_weights(ws_ref)
        dvhs = []
        for b in range(2):
            u = z_ref[b].astype(F32)
            _spatial_gate(sv_ref, wtril, vls[b], bt_ref, b, tm, gd)
            sv = sv_ref[...]
            gated = (u * sv).astype(BF16)
            dgated = _dot_nt(dmb, wout_ref[b])
            dwout_ref[b] += _dot_tn(gated, dmb)
            dz_ref[b] = (dgated * sv * slope_ref[b].astype(F32)).astype(BF16)
            dsv = dgated * u
            folded = dsv[0:CHUNK, :]
            for c in range(1, n_chunks):
                folded = folded + dsv[c * CHUNK : (c + 1) * CHUNK, :]
            for gg in range(A_GROUPS // 2):
                g = b * (A_GROUPS // 2) + gg
                dbacc_ref[:, g : g + 1] += jnp.sum(folded[:, gg * gd : (gg + 1) * gd], axis=1, keepdims=True)
            dsvb = dsv.astype(BF16)
            for gg in range(A_GROUPS // 2):
                g = b * (A_GROUPS // 2) + gg
                for c in range(n_chunks):
                    rows = slice(c * CHUNK, (c + 1) * CHUNK)
                    cols = slice(gg * gd, (gg + 1) * gd)
                    blk = dsvb[rows, cols]
                    dvl_ref[rows, cols] = _dot_tn(wtril[g], blk)
                    dws_ref[g] += _dot_nt(blk, vls[b][rows, cols])
            dvl = dvl_ref[...]
            dlng_ref[b : b + 1, :] += jnp.sum(dvl * vhs[b], axis=0, keepdims=True)
            dlnb_ref[b : b + 1, :] += jnp.sum(dvl, axis=0, keepdims=True)
            dvhs.append(dvl * lng_ref[b : b + 1, :])
        width = 2.0 * q
        m1 = (jnp.sum(dvhs[0], axis=-1, keepdims=True) + jnp.sum(dvhs[1], axis=-1, keepdims=True)) / width
        m2 = (jnp.sum(dvhs[0] * vhs[0], axis=-1, keepdims=True)
              + jnp.sum(dvhs[1] * vhs[1], axis=-1, keepdims=True)) / width
        for b in range(2):
            dv = rstd * (dvhs[b] - m1 - vhs[b] * m2)
            dz_ref[2 + b] = (dv * slope_ref[2 + b].astype(F32)).astype(BF16)

        @pl.when(pl.program_id(0) == t // tm - 1)
        def _():
            row = lax.broadcasted_iota(jnp.int32, (CHUNK, CHUNK), 0)
            col = lax.broadcasted_iota(jnp.int32, (CHUNK, CHUNK), 1)
            for g in range(A_GROUPS):
                dws_ref[g] = jnp.where(row >= col, dws_ref[g], 0.0)

    return _call(
        name,
        body,
        grid=(t // tm,),
        in_specs=[_rows(tm, d), _rows(tm, d), _VM, _blocks(4, tm, q), _blocks(4, tm, q)] + [_VM] * 5,
        out_specs=[_blocks(4, tm, q)] + [_VM] * 6,
        out_shape=[
            jax.ShapeDtypeStruct((4, t, q), BF16),
            jax.ShapeDtypeStruct((2, q, d), F32),
            jax.ShapeDtypeStruct((A_GROUPS, CHUNK, CHUNK), F32),
            jax.ShapeDtypeStruct((CHUNK, A_GROUPS), F32),
            jax.ShapeDtypeStruct((2, q), F32),
            jax.ShapeDtypeStruct((2, q), F32),
            jax.ShapeDtypeStruct((1, d), F32),
        ],
        scratch_shapes=[pltpu.VMEM((tm, q), F32), pltpu.VMEM((tm, q), F32)],
        args=[dy, m, gpost, act, slope, ln_g, ln_b, w_s, b_t, w_out],
        exchanges=exchanges,
    )


def _window_counts(tm, win):
    pos = pl.program_id(0) * tm + lax.broadcasted_iota(jnp.int32, (tm, 1), 0)
    return jnp.minimum(pos + 1, win).astype(F32)


def _pooled(p, halo, tm, gd, inside=False):
    prev = halo if inside else jnp.where(pl.program_id(0) == 0, 0.0, halo)
    ext = jnp.concatenate([prev, p], axis=0)
    out = []
    for g, win in enumerate(B_WINDOWS):
        s = ext[:, g * gd : (g + 1) * gd]
        step = 1
        while step < win:
            s = s + pltpu.roll(s, step, 0)
            step *= 2
        total = s[HALO:, :]
        count = float(win) if inside else _window_counts(tm, win)
        out.append(total / count - p[:, g * gd : (g + 1) * gd])
    return out


def _halo_spec(t, tm, d, ahead):
    per = tm // HALO
    if ahead:
        return pl.BlockSpec((HALO, d), lambda i: (jnp.minimum((i + 1) * per, t // HALO - 1), 0))
    return pl.BlockSpec((HALO, d), lambda i: (jnp.maximum(i * per - 1, 0), 0))


def _mix_b_fwd(name, x, gpre, gpost, w_in, w_grp, scale, w_out, exchanges=()):
    t, d = x.shape
    gd = d // len(B_WINDOWS)
    tm = _step_rows(t)

    def body(x_ref, xh_ref, gpre_ref, gpost_ref, win_ref, wgrp_ref, scale_ref, wout_ref, o_ref, p_ref, m_ref):
        xv = x_ref[...]
        gain = gpre_ref[...]
        p = _dot(_rms(xv, gain).astype(BF16), win_ref[...])
        p_ref[...] = p
        halo = _dot(_rms(xh_ref[...], gain).astype(BF16), win_ref[...])
        pooled = _pooled(p, halo, tm, gd)
        mixed = jnp.concatenate([_dot(pg.astype(BF16), wgrp_ref[g]) for g, pg in enumerate(pooled)], axis=1)
        m = _dot((mixed * scale_ref[...]).astype(BF16), wout_ref[...])
        m_ref[...] = m
        o_ref[...] = xv + _rms(m, gpost_ref[...])

    return _call(
        name,
        body,
        grid=(t // tm,),
        in_specs=[_rows(tm, d), _halo_spec(t, tm, d, False), _VM, _VM, _VM, _VM, _VM, _VM],
        out_specs=[_rows(tm, d)] * 3,
        out_shape=[jax.ShapeDtypeStruct((t, d), F32)] * 3,
        args=[x, x, gpre, gpost, w_in, w_grp, scale, w_out],
        exchanges=exchanges,
    )


def _mix_b_bwd(name, dy, x, m, p, gpre, gpost, w_in, w_grp, scale, w_out, exchanges=()):
    t, d = dy.shape
    gd = d // len(B_WINDOWS)
    tm = _step_rows(t)
    n_steps = t // tm

    def body(dy_ref, x_ref, m_ref, p_ref, pprev_ref, dynext_ref, mnext_ref, pnext_ref,
             gpre_ref, gpost_ref, win_ref, wgrp_ref, scale_ref, wout_ref,
             dx_ref, dwout_ref, dwgrp_ref, dscale_ref, dpost_ref, dwin_ref, dpre_ref):
        @pl.when(pl.program_id(0) == 0)
        def _():
            for ref in (dwout_ref, dwgrp_ref, dscale_ref, dpost_ref, dwin_ref, dpre_ref):
                ref[...] = jnp.zeros_like(ref)

        scale = scale_ref[...]

        def tail_bwd(dy_rows, m_rows, pooled, counts, accumulate):
            dm, dgain = _rms_bwd(dy_rows, m_rows, gpost_ref[...])
            dmb = dm.astype(BF16)
            pooled = [pg.astype(BF16) for pg in pooled]
            mixed = jnp.concatenate([_dot(pg, wgrp_ref[g]) for g, pg in enumerate(pooled)], axis=1)
            dms = _dot_nt(dmb, wout_ref[...])
            dmixed = (dms * scale).astype(BF16)
            if accumulate:
                dpost_ref[...] += dgain
                dwout_ref[...] += _dot_tn((mixed * scale).astype(BF16), dmb)
                dscale_ref[...] += jnp.sum(dms * mixed, axis=0, keepdims=True)
            parts = []
            for g, win in enumerate(B_WINDOWS):
                dmg = dmixed[:, g * gd : (g + 1) * gd]
                if accumulate:
                    dwgrp_ref[g] += _dot_tn(pooled[g], dmg)
                parts.append(_dot_nt(dmg, wgrp_ref[g]) / counts(win))
            return jnp.concatenate(parts, axis=1)

        dyv = dy_ref[...]
        pv = p_ref[...]
        dq_blk = tail_bwd(dyv, m_ref[...], _pooled(pv, pprev_ref[...], tm, gd),
                          lambda win: _window_counts(tm, win), True)
        dq_next = tail_bwd(dynext_ref[...], mnext_ref[...],
                           _pooled(pnext_ref[...], pv[tm - HALO :, :], HALO, gd, inside=True),
                           float, False)
        dq_next = jnp.where(pl.program_id(0) == n_steps - 1, 0.0, dq_next)
        ext = jnp.concatenate([dq_blk, dq_next], axis=0)
        parts = []
        for g, win in enumerate(B_WINDOWS):
            cols = slice(g * gd, (g + 1) * gd)
            s = ext[:, cols]
            step = 1
            while step < win:
                s = s + pltpu.roll(s, tm + HALO - step, 0)
                step *= 2
            parts.append((s[:tm, :] - dq_blk[:, cols] * _window_counts(tm, win)).astype(BF16))
        dp = jnp.concatenate(parts, axis=1)
        xv = x_ref[...]
        gain = gpre_ref[...]
        hb = _rms(xv, gain).astype(BF16)
        dwin_ref[...] += _dot_tn(hb, dp)
        dx, dgain = _rms_bwd(_dot_nt(dp, win_ref[...]), xv, gain)
        dx_ref[...] = dyv + dx
        dpre_ref[...] += dgain

    before, after = _halo_spec(t, tm, d, False), _halo_spec(t, tm, d, True)
    return _call(
        name,
        body,
        grid=(n_steps,),
        in_specs=[_rows(tm, d)] * 4 + [before, after, after, after] + [_VM] * 6,
        out_specs=[_rows(tm, d)] + [_VM] * 6,
        out_shape=[
            jax.ShapeDtypeStruct((t, d), F32),
            jax.ShapeDtypeStruct((d, d), F32),
            jax.ShapeDtypeStruct((len(B_WINDOWS), gd, gd), F32),
            jax.ShapeDtypeStruct((1, d), F32),
            jax.ShapeDtypeStruct((1, d), F32),
            jax.ShapeDtypeStruct((d, d), F32),
            jax.ShapeDtypeStruct((1, d), F32),
        ],
        args=[dy, x, m, p, p, dy, m, p, gpre, gpost, w_in, w_grp, scale, w_out],
        exchanges=exchanges,
    )


def _cast_into_slots(name, place, w, dtype):
    n_layers, r, c = w.shape

    def body(place_ref, w_ref, *o_refs):
        del place_ref
        for j, o_ref in enumerate(o_refs):
            @pl.when(pl.program_id(0) == j)
            def _():
                o_ref[...] = w_ref[...].astype(dtype)

    return pl.pallas_call(
        body,
        name=name,
        grid_spec=pltpu.PrefetchScalarGridSpec(
            num_scalar_prefetch=1,
            grid=(n_layers,),
            in_specs=[pl.BlockSpec((1, r, c), lambda i, place_ref: (i, 0, 0))],
            out_specs=[pl.BlockSpec((1, r, c), lambda i, place_ref: (place_ref[0], 0, 0))] * n_layers,
        ),
        out_shape=[jax.ShapeDtypeStruct((N_CHIPS, r, c), dtype)] * n_layers,
        compiler_params=_params(),
    )(place, w)


def _row_tile(r):
    return 256 if r % 256 == 0 else r


def _pair_sum(name, place, dw, recv):
    _, r, c = dw.shape
    half = r // 2
    tr = _row_tile(half)
    per = half // tr

    def body(place_ref, a_ref, b_ref, o_ref):
        del place_ref
        o_ref[...] = (a_ref[...] + b_ref[...]).astype(BF16)

    return pl.pallas_call(
        body,
        name=name,
        grid_spec=pltpu.PrefetchScalarGridSpec(
            num_scalar_prefetch=1,
            grid=(N_CHIPS, per),
            in_specs=[
                pl.BlockSpec((1, tr, c), lambda k, i, place_ref: (k, place_ref[1] * per + i, 0)),
                pl.BlockSpec((1, tr, c), lambda k, i, place_ref: (k, i, 0)),
            ],
            out_specs=pl.BlockSpec((1, tr, c), lambda k, i, place_ref: (k, i, 0)),
        ),
        out_shape=jax.ShapeDtypeStruct(recv.shape, BF16),
        compiler_params=pltpu.CompilerParams(
            dimension_semantics=("arbitrary",) * 2, vmem_limit_bytes=VMEM_LIMIT_BYTES),
    )(place, dw, recv)


def _chip_sum(name, place, mine, others, exchanges=()):
    n_layers = len(mine)
    _, half, c = mine[0].shape
    tr = _row_tile(half)
    per = half // tr

    def body(place_ref, *refs):
        del place_ref
        o_ref = refs[-1]
        for j in range(n_layers):
            @pl.when(pl.program_id(0) == j)
            def _():
                parts = refs[4 * j : 4 * j + 4]
                acc = parts[0][...].astype(F32) + parts[1][...].astype(F32)
                acc = acc + parts[2][...].astype(F32)
                o_ref[...] = acc + parts[3][...].astype(F32)

    def part(j, flip):
        return pl.BlockSpec((1, tr, c), lambda l, i, place_ref: (
            jnp.bitwise_xor(place_ref[0], flip), jnp.where(l == j, i, 0), 0))

    args = []
    for j in range(n_layers):
        args += [mine[j], others[j], others[j], others[j]]
    (total,), ex_outs = _call(
        name,
        body,
        grid=(n_layers, per),
        in_specs=[part(j, flip) for j in range(n_layers) for flip in range(N_CHIPS)],
        out_specs=[pl.BlockSpec((1, tr, c), lambda l, i, place_ref: (l, place_ref[1] * per + i, 0))],
        out_shape=[jax.ShapeDtypeStruct((n_layers, 2 * half, c), F32)],
        args=args,
        prefetch=[place],
        exchanges=exchanges,
    )
    return total, ex_outs


def _adamw(name, w, g, m, v, exchanges=()):
    n_layers, r, c = w.shape
    tr = _row_tile(r)

    def body(w_ref, g_ref, m_ref, v_ref, go_ref, d_ref, nm_ref, nv_ref):
        gv = g_ref[...]
        go_ref[...] = gv
        nm = ADAM_B1 * m_ref[...] + (1.0 - ADAM_B1) * gv
        nv = ADAM_B2 * v_ref[...] + (1.0 - ADAM_B2) * jnp.square(gv)
        m_hat = nm / (1.0 - ADAM_B1 ** ADAM_STEP)
        v_hat = nv / (1.0 - ADAM_B2 ** ADAM_STEP)
        d_ref[...] = -ADAM_LR * (m_hat / (jnp.sqrt(v_hat) + ADAM_EPS) + ADAM_WD * w_ref[...])
        nm_ref[...] = nm
        nv_ref[...] = nv

    spec = pl.BlockSpec((1, tr, c), lambda l, i: (l, i, 0))
    return _call(
        name,
        body,
        grid=(n_layers, r // tr),
        in_specs=[spec] * 4,
        out_specs=[spec] * 4,
        out_shape=[jax.ShapeDtypeStruct(w.shape, F32)] * 4,
        args=[w, g, m, v],
        exchanges=exchanges,
    )


class _GatherSmall:
    def __init__(self, block):
        self.inputs = [block]
        self.out_shapes = [jax.ShapeDtypeStruct((N_DEV,) + block.shape, block.dtype)]
        self.aliases = {}
        self.n_sems = 5

    def _copies(self, ins, outs, send, recv, outbound):
        x, y, c = _position()
        peers = [(x, y, 1 - c)] + [(px, py, c) for px, py in _other_chips(x, y)]
        copies = []
        for k, (px, py, pc) in enumerate(peers):
            slot = 4 * x + 2 * y + c if outbound else 4 * px + 2 * py + pc
            copies.append(pltpu.make_async_remote_copy(
                src_ref=ins[0], dst_ref=outs[0].at[slot], send_sem=send.at[k], recv_sem=recv.at[k],
                device_id=(px, py, pc), device_id_type=MESH))
        return copies

    def _own(self, ins, outs, send):
        x, y, c = _position()
        return pltpu.make_async_copy(ins[0], outs[0].at[4 * x + 2 * y + c], send.at[4])

    def start(self, ins, outs, send, recv):
        self._own(ins, outs, send).start()
        for cp in self._copies(ins, outs, send, recv, True):
            cp.start()

    def finish(self, ins, outs, send, recv):
        for cp in self._copies(ins, outs, send, recv, False):
            cp.wait_recv()
        for cp in self._copies(ins, outs, send, recv, True):
            cp.wait_send()
        self._own(ins, outs, send).wait()


class _GatherSmallForward:
    def __init__(self, gathered):
        self.inputs = [gathered]
        self.out_shapes = [jax.ShapeDtypeStruct(gathered.shape, gathered.dtype)]
        self.aliases = {0: 0}
        self.n_sems = 3

    def _copies(self, outs, send, recv, core):
        x, y, c = _position()
        copies = []
        for k, (px, py) in enumerate(_other_chips(x, y)):
            rows = outs[0].at[4 * px + 2 * py + core]
            copies.append(pltpu.make_async_remote_copy(
                src_ref=rows, dst_ref=rows, send_sem=send.at[k], recv_sem=recv.at[k],
                device_id=(x, y, 1 - c), device_id_type=MESH))
        return copies

    def start(self, ins, outs, send, recv):
        for cp in self._copies(outs, send, recv, lax.axis_index("c")):
            cp.start()

    def finish(self, ins, outs, send, recv):
        c = lax.axis_index("c")
        for cp in self._copies(outs, send, recv, 1 - c):
            cp.wait_recv()
        for cp in self._copies(outs, send, recv, c):
            cp.wait_send()


def _sum_devices(gathered):
    _, m_per, n = gathered.shape

    def body(all_ref, sum_ref):
        acc = all_ref[0]
        for k in range(1, N_DEV):
            acc = acc + all_ref[k]
        sum_ref[...] = acc

    return pl.pallas_call(
        body,
        name="sum_devices",
        in_specs=[_VM],
        out_specs=_VM,
        out_shape=jax.ShapeDtypeStruct((m_per, n), F32),
        compiler_params=pltpu.CompilerParams(vmem_limit_bytes=VMEM_LIMIT_BYTES),
    )(gathered)


SHARDED = ("a_w_in", "a_w_out", "b_w_in", "b_w_grp", "b_scale", "b_w_out", "ffn_w_gate", "ffn_w_up", "ffn_w_down")
SMALL = ("a_ln_g", "a_ln_b", "a_w_s", "a_b_s", "mix_pre_g", "mix_post_g", "ffn_pre_g", "ffn_post_g")
WEIGHTS = ("a_w_in", "a_ln_g", "a_ln_b", "a_w_s", "a_b_s", "a_w_out", "b_w_in", "b_w_grp", "b_scale", "b_w_out",
           "mix_pre_g", "mix_post_g", "ffn_pre_g", "ffn_post_g", "ffn_w_gate", "ffn_w_up", "ffn_w_down")


TRANSPOSED = ("ffn_w_gate", "ffn_w_up")


def _as_layers(name, a):
    if name in TRANSPOSED:
        return jnp.swapaxes(a, 1, 2)
    if a.ndim == 2:
        return a.reshape(a.shape[0], 1, a.shape[1])
    return a.reshape(a.shape[0], -1, a.shape[-1])


def _from_layers(name, a, shape):
    if name in TRANSPOSED:
        return jnp.swapaxes(a, 1, 2)
    return a.reshape(shape)


def _pack_small(parts):
    return jnp.concatenate([p.reshape(-1, 128) for p in parts], axis=0)


def _unpack_small(packed, like):
    out, row = [], 0
    for ref in like:
        rows = ref.size // 128
        out.append(packed[row : row + rows].reshape(ref.shape))
        row += rows
    return out


def kernel(x, a_w_in, a_ln_g, a_ln_b, a_w_s, a_b_s, a_w_out, b_w_in, b_w_grp, b_scale, b_w_out, mix_pre_g, mix_post_g, ffn_pre_g, ffn_post_g, ffn_w_gate, ffn_w_up, ffn_w_down, loss_target, m_a_w_in, m_a_ln_g, m_a_ln_b, m_a_w_s, m_a_b_s, m_a_w_out, m_b_w_in, m_b_w_grp, m_b_scale, m_b_w_out, m_mix_pre_g, m_mix_post_g, m_ffn_pre_g, m_ffn_post_g, m_ffn_w_gate, m_ffn_w_up, m_ffn_w_down, v_a_w_in, v_a_ln_g, v_a_ln_b, v_a_w_s, v_a_b_s, v_a_w_out, v_b_w_in, v_b_w_grp, v_b_scale, v_b_w_out, v_mix_pre_g, v_mix_post_g, v_ffn_pre_g, v_ffn_post_g, v_ffn_w_gate, v_ffn_w_up, v_ffn_w_down):
    weights = dict(a_w_in=a_w_in, a_ln_g=a_ln_g, a_ln_b=a_ln_b, a_w_s=a_w_s, a_b_s=a_b_s, a_w_out=a_w_out,
                   b_w_in=b_w_in, b_w_grp=b_w_grp, b_scale=b_scale, b_w_out=b_w_out, mix_pre_g=mix_pre_g,
                   mix_post_g=mix_post_g, ffn_pre_g=ffn_pre_g, ffn_post_g=ffn_post_g, ffn_w_gate=ffn_w_gate,
                   ffn_w_up=ffn_w_up, ffn_w_down=ffn_w_down)
    mom1 = dict(a_w_in=m_a_w_in, a_ln_g=m_a_ln_g, a_ln_b=m_a_ln_b, a_w_s=m_a_w_s, a_b_s=m_a_b_s, a_w_out=m_a_w_out,
                b_w_in=m_b_w_in, b_w_grp=m_b_w_grp, b_scale=m_b_scale, b_w_out=m_b_w_out, mix_pre_g=m_mix_pre_g,
                mix_post_g=m_mix_post_g, ffn_pre_g=m_ffn_pre_g, ffn_post_g=m_ffn_post_g, ffn_w_gate=m_ffn_w_gate,
                ffn_w_up=m_ffn_w_up, ffn_w_down=m_ffn_w_down)
    mom2 = dict(a_w_in=v_a_w_in, a_ln_g=v_a_ln_g, a_ln_b=v_a_ln_b, a_w_s=v_a_w_s, a_b_s=v_a_b_s, a_w_out=v_a_w_out,
                b_w_in=v_b_w_in, b_w_grp=v_b_w_grp, b_scale=v_b_scale, b_w_out=v_b_w_out, mix_pre_g=v_mix_pre_g,
                mix_post_g=v_mix_post_g, ffn_pre_g=v_ffn_pre_g, ffn_post_g=v_ffn_post_g, ffn_w_gate=v_ffn_w_gate,
                ffn_w_up=v_ffn_w_up, ffn_w_down=v_ffn_w_down)

    t, d = x.shape[1], x.shape[2]
    depth = mix_pre_g.shape[0]
    gd_b = d // len(B_WINDOWS)
    xs = x.reshape(t, d)
    target = loss_target.reshape(t, d)

    chip = 2 * lax.axis_index("x") + lax.axis_index("y")
    place = jnp.stack([chip, lax.axis_index("c")]).astype(jnp.int32)
    bufs = {name: list(_cast_into_slots("cast_" + name, place, _as_layers(name, weights[name]),
                                        F32 if name == "b_scale" else BF16)) for name in SHARDED}

    def gain(name, i):
        return weights[name][i].reshape(1, d)

    def weight_keys(i):
        j = i // 2
        mixer = [("a_w_in", j), ("a_w_out", j)] if i % 2 == 0 else [("b_w_in", j), ("b_w_grp", j), ("b_w_out", j)]
        return mixer, [("ffn_w_gate", i), ("ffn_w_up", i), ("ffn_w_down", i)]

    def gather(keys):
        return _GatherWeights([bufs[n][j] for n, j in keys],
                              whole=[k for k, (n, _) in enumerate(keys) if n == "b_scale"])

    def gathered(keys, outs):
        for (n, j), buf in zip(keys, outs):
            bufs[n][j] = buf

    first = weight_keys(0)[0] + [("b_scale", j) for j in range(b_scale.shape[0])]
    gathered(first, _exchange("gather_first", [gather(first)])[0])
    saved = []
    cur = xs
    for i in range(depth):
        j = i // 2
        mixer_next, ffn_next = weight_keys(i + 1) if i + 1 < depth else ([], [])
        ffn_keys = weight_keys(i)[1]
        after_ffn = mixer_next if (i + 1) % 2 == 0 else mixer_next + ffn_next
        if i % 2 == 0:
            w_in = bufs["a_w_in"][j]
            q = w_in.shape[2]
            w_out = bufs["a_w_out"][j].reshape(2, q, d)
            ln_g = a_ln_g[j].reshape(2, q)
            ln_b = a_ln_b[j].reshape(2, q)
            b_t = jnp.transpose(a_b_s[j])
            (nxt, act, slope, m), (got,) = _mix_a_fwd(
                f"mix_a_fwd{j}", cur, gain("mix_pre_g", i), gain("mix_post_g", i), w_in, ln_g, ln_b, a_w_s[j], b_t,
                w_out, exchanges=[gather(ffn_keys)])
            gathered(ffn_keys, got)
            mix_saved = dict(x=cur, act=act, slope=slope, m=m, w_in=w_in, w_out=w_out, ln_g=ln_g, ln_b=ln_b, b_t=b_t)
        else:
            w_in = bufs["b_w_in"][j].reshape(d, d)
            w_out = bufs["b_w_out"][j].reshape(d, d)
            w_grp = jnp.transpose(bufs["b_w_grp"][j].reshape(N_CHIPS, len(B_WINDOWS), gd_b // N_CHIPS, gd_b),
                                  (1, 0, 2, 3)).reshape(len(B_WINDOWS), gd_b, gd_b)
            scale = bufs["b_scale"][j].reshape(1, d)
            (nxt, p, m), _ = _mix_b_fwd(f"mix_b_fwd{j}", cur, gain("mix_pre_g", i), gain("mix_post_g", i),
                                        w_in, w_grp, scale, w_out)
            mix_saved = dict(x=cur, p=p, m=m, w_in=w_in, w_out=w_out, w_grp=w_grp, scale=scale)
        cur = nxt
        wg, wu, wd = (bufs[n][k].reshape(1, -1, d) for n, k in ffn_keys)
        (nxt, a, dup, dgate, f, *sq), (got,) = _ffn_fwd(
            f"ffn_fwd{i}", cur, gain("ffn_pre_g", i), gain("ffn_post_g", i), wg, wu, wd,
            target=target if i == depth - 1 else None, exchanges=[gather(after_ffn)])
        gathered(after_ffn, got)
        saved.append((mix_saved, dict(x=cur, a=a, dup=dup, dgate=dgate, f=f, wg=wg, wu=wu, wd=wd)))
        cur = nxt

    dcur = cur
    loss = lax.psum(0.5 * sq[0][0, 0] / d, ("x", "y", "c"))

    grads = {name: [None] * weights[name].shape[0] for name in WEIGHTS}
    state = dict(to_sibling=[], to_chips=[])
    pair, from_chips = {}, {}

    def exchanges_due(carry=True):
        if not carry:
            return []
        return [_ToSibling([a for _, _, a in state["to_sibling"]]), _ToChips([a for _, _, a in state["to_chips"]])]

    def exchanged(outs):
        if not outs:
            return
        from_sibling, arrived = outs
        for (n, k, _), got in zip(state["to_chips"], arrived):
            from_chips[n, k] = got
        state["to_chips"] = []
        for (n, k, dw), got in zip(state["to_sibling"], from_sibling):
            pair[n, k] = _pair_sum(f"pair_sum_{n}{k}", place, dw, got)
            state["to_chips"].append((n, k, pair[n, k]))
        state["to_sibling"] = []

    def made(name, k, dw):
        grads[name][k] = dw
        state["to_sibling"].append((name, k, dw))

    for i in reversed(range(depth)):
        j = i // 2
        mix_saved, ffn_saved = saved[i]
        s = ffn_saved
        both = i == 0
        (dg, du, dwd, dgain), outs = _ffn_bwd_hidden(
            f"ffn_bwd_hidden{i}", dcur, s["f"], gain("ffn_post_g", i), s["a"], s["dup"], s["dgate"], s["wd"],
            exchanges=exchanges_due(both))
        exchanged(outs)
        grads["ffn_post_g"][i] = dgain
        made("ffn_w_down", i, dwd.reshape(N_CHIPS, -1, d))
        (dcur, dwg, dwu, dgain), outs = _bwd_in(
            f"ffn_bwd_in{i}", dcur, s["x"], gain("ffn_pre_g", i), [dg, du], [s["wg"], s["wu"]], transposed=True,
            exchanges=exchanges_due())
        exchanged(outs)
        grads["ffn_pre_g"][i] = dgain
        made("ffn_w_gate", i, dwg.reshape(N_CHIPS, -1, d))
        made("ffn_w_up", i, dwu.reshape(N_CHIPS, -1, d))
        s = mix_saved
        if i % 2 == 0:
            (dz, dwout, dws, dbacc, dlng, dlnb, dgain), outs = _mix_a_bwd_hidden(
                f"mix_a_bwd_hidden{j}", dcur, s["m"], gain("mix_post_g", i), s["act"], s["slope"], s["ln_g"], s["ln_b"],
                a_w_s[j], s["b_t"], s["w_out"], exchanges=exchanges_due(both))
            exchanged(outs)
            grads["a_w_s"][j] = dws
            grads["a_b_s"][j] = jnp.transpose(dbacc)
            grads["a_ln_g"][j] = dlng.reshape(-1)
            grads["a_ln_b"][j] = dlnb.reshape(-1)
            grads["mix_post_g"][i] = dgain
            made("a_w_out", j, dwout.reshape(N_CHIPS, -1, d))
            (dcur, dwin, dgain), outs = _bwd_in(
                f"mix_a_bwd_in{j}", dcur, s["x"], gain("mix_pre_g", i), [dz], [s["w_in"]], exchanges=exchanges_due())
            exchanged(outs)
            grads["mix_pre_g"][i] = dgain
            made("a_w_in", j, dwin)
        else:
            (dcur, dwout, dwgrp, dscale, dpost, dwin, dpre), outs = _mix_b_bwd(
                f"mix_b_bwd{j}", dcur, s["x"], s["m"], s["p"], gain("mix_pre_g", i), gain("mix_post_g", i),
                s["w_in"], s["w_grp"], s["scale"], s["w_out"], exchanges=exchanges_due())
            exchanged(outs)
            grads["b_scale"][j] = dscale
            grads["mix_post_g"][i] = dpost
            grads["mix_pre_g"][i] = dpre
            made("b_w_out", j, dwout.reshape(N_CHIPS, -1, d))
            made("b_w_grp", j, jnp.transpose(
                dwgrp.reshape(len(B_WINDOWS), N_CHIPS, gd_b // N_CHIPS, gd_b), (1, 0, 2, 3)).reshape(N_CHIPS, -1, gd_b))
            made("b_w_in", j, dwin.reshape(N_CHIPS, -1, d))
    grad_x = dcur.reshape(x.shape)

    small_grads = [jnp.stack([g.reshape(weights[name].shape[1:]) for g in grads[name]], axis=0) for name in SMALL]
    scale_grad = jnp.concatenate(grads["b_scale"], axis=0)
    outs = _exchange("grads_last", exchanges_due() + [_GatherSmall(_pack_small(small_grads + [scale_grad]))])
    exchanged(outs[:2])
    outs = _exchange("grads_last_to_chips", exchanges_due() + [_GatherSmallForward(outs[2][0])])
    exchanged(outs[:2])
    gathered_small = outs[2][0]

    reduced_names = [name for name in SHARDED if name != "b_scale"]
    sums = []
    for name in reduced_names:
        layers = range(weights[name].shape[0])
        sums.append(_chip_sum("chip_sum_" + name, place, [pair[name, k] for k in layers],
                              [from_chips[name, k] for k in layers])[0])
    reduced = dict(zip(reduced_names, _exchange("swap_halves", [_SwapHalves(sums)])[0]))

    out_g, out_d, out_m, out_v = {}, {}, {}, {}
    for name in reduced_names:
        shape = weights[name].shape
        results, _ = _adamw("adamw_" + name, _as_layers(name, weights[name]), reduced[name],
                            _as_layers(name, mom1[name]), _as_layers(name, mom2[name]))
        out_g[name], out_d[name], out_m[name], out_v[name] = (_from_layers(name, a, shape) for a in results)

    summed = _sum_devices(gathered_small)
    small_rows = summed.shape[0] - scale_grad.size // 128
    scale_sum = summed[small_rows:].reshape(scale_grad.shape)
    scale_mine = lax.dynamic_slice_in_dim(scale_sum, chip * b_scale.shape[1], b_scale.shape[1], axis=1)[:, None, :]
    summed = summed[:small_rows]
    results, _ = _adamw("adamw_b_scale", _as_layers("b_scale", b_scale), scale_mine,
                        _as_layers("b_scale", m_b_scale), _as_layers("b_scale", v_b_scale))
    out_g["b_scale"], out_d["b_scale"], out_m["b_scale"], out_v["b_scale"] = (a.reshape(b_scale.shape) for a in results)
    small_like = [weights[name] for name in SMALL]
    packs = [_pack_small([src[name] for name in SMALL]).reshape(1, -1, 128) for src in (weights, mom1, mom2)]
    results, _ = _adamw("adamw_small", packs[0], summed.reshape(1, -1, 128), packs[1], packs[2])
    for dst, packed in zip((out_g, out_d, out_m, out_v), (a[0] for a in results)):
        for name, val in zip(SMALL, _unpack_small(packed, small_like)):
            dst[name] = val

    return (loss, grad_x, *[out_g[n] for n in WEIGHTS], *[out_d[n] for n in WEIGHTS],
            *[out_m[n] for n in WEIGHTS], *[out_v[n] for n in WEIGHTS])
```

```python
import functools
import math

import jax
import jax.numpy as jnp
from jax import lax
from jax.experimental import pallas as pl
from jax.experimental.pallas import tpu as pltpu

F32 = jnp.float32
BF16 = jnp.bfloat16
MESH = pl.DeviceIdType.MESH

EPS = 1e-6
CHUNK = 128
A_GROUPS = 8
B_WINDOWS = (2, 4, 8, 16)
HALO = 16
N_CHIPS = 4
N_DEV = 8

ADAM_LR = 0.001
ADAM_B1 = 0.9
ADAM_B2 = 0.999
ADAM_EPS = 1e-08
ADAM_WD = 0.01
ADAM_STEP = 10

VMEM_LIMIT_BYTES = 60 * 1024 * 1024
INV_SQRT2 = 1.0 / math.sqrt(2.0)
INV_SQRT_2PI = 1.0 / math.sqrt(2.0 * math.pi)

_ANY = pl.BlockSpec(memory_space=pl.ANY)
_VM = pl.BlockSpec(memory_space=pltpu.VMEM)


def _params():
    return pltpu.CompilerParams(dimension_semantics=("arbitrary",), vmem_limit_bytes=VMEM_LIMIT_BYTES)


def _token_block(t):
    return 256 if t >= 1024 else 128


def _step_rows(t):
    return 2 * _token_block(t)


def _rows(tm, d):
    return pl.BlockSpec((tm, d), lambda i: (i, 0))


def _blocks(nb, tm, bw):
    return pl.BlockSpec((nb, tm, bw), lambda i: (0, i, 0))


def _dot(a, b):
    return lax.dot_general(a, b, (((1,), (0,)), ((), ())), preferred_element_type=F32)


def _dot_nt(a, b):
    return lax.dot_general(a, b, (((1,), (1,)), ((), ())), preferred_element_type=F32)


def _dot_tn(a, b):
    return lax.dot_general(a, b, (((0,), (0,)), ((), ())), preferred_element_type=F32)


def _rms(x, g):
    return x * lax.rsqrt(jnp.mean(x * x, axis=-1, keepdims=True) + EPS) * g


def _rms_bwd(dy, x, g):
    r = lax.rsqrt(jnp.mean(x * x, axis=-1, keepdims=True) + EPS)
    n = x * r
    dn = dy * g
    dx = r * (dn - n * jnp.mean(dn * n, axis=-1, keepdims=True))
    return dx, jnp.sum(dy * n, axis=0, keepdims=True)


def _gelu_and_grad(x):
    cdf = 0.5 * (1.0 + lax.erf(x * INV_SQRT2))
    return x * cdf, cdf + x * (jnp.exp(-0.5 * x * x) * INV_SQRT_2PI)


def _position():
    return lax.axis_index("x"), lax.axis_index("y"), lax.axis_index("c")


def _other_chips(x, y):
    return [(1 - x, y), (x, 1 - y), (1 - x, 1 - y)]


class _GatherWeights:
    def __init__(self, bufs, whole=()):
        self.inputs = list(bufs)
        self.out_shapes = [jax.ShapeDtypeStruct(b.shape, b.dtype) for b in bufs]
        self.aliases = {w: w for w in range(len(bufs))}
        self.n_sems = 6 * len(bufs)
        self.whole = frozenset(whole)

    def _part(self, outs, w, slot, core):
        if w in self.whole:
            return outs[w].at[slot]
        half = outs[w].shape[1] // 2
        return outs[w].at[slot, pl.ds(core * half, half)]

    def _ici(self, outs, send, recv, w, j, slot):
        x, y, c = _position()
        px, py = _other_chips(x, y)[j]
        part = self._part(outs, w, slot, c)
        return pltpu.make_async_remote_copy(
            src_ref=part, dst_ref=part, send_sem=send.at[6 * w + j], recv_sem=recv.at[6 * w + j],
            device_id=(px, py, c), device_id_type=MESH)

    def _d2d(self, outs, send, recv, w, j, slot, core):
        x, y, c = _position()
        part = self._part(outs, w, slot, core)
        return pltpu.make_async_remote_copy(
            src_ref=part, dst_ref=part, send_sem=send.at[6 * w + 3 + j], recv_sem=recv.at[6 * w + 3 + j],
            device_id=(x, y, 1 - c), device_id_type=MESH)

    def start(self, ins, outs, send, recv):
        x, y, _ = _position()
        for w in range(len(outs)):
            for j in range(3):
                self._ici(outs, send, recv, w, j, 2 * x + y).start()

    def finish(self, ins, outs, send, recv):
        x, y, c = _position()
        slots = [2 * px + py for px, py in _other_chips(x, y)]
        for w in range(len(outs)):
            for j, slot in enumerate(slots):
                self._ici(outs, send, recv, w, j, slot).wait_recv()
                if w not in self.whole:
                    self._d2d(outs, send, recv, w, j, slot, c).start()
        for w in range(len(outs)):
            for j, slot in enumerate(slots):
                if w not in self.whole:
                    self._d2d(outs, send, recv, w, j, slot, 1 - c).wait_recv()
        for w in range(len(outs)):
            for j, slot in enumerate(slots):
                self._ici(outs, send, recv, w, j, 2 * x + y).wait_send()
                if w not in self.whole:
                    self._d2d(outs, send, recv, w, j, slot, c).wait_send()


class _ToSibling:
    def __init__(self, grads):
        self.inputs = list(grads)
        self.out_shapes = [jax.ShapeDtypeStruct((g.shape[0], g.shape[1] // 2, g.shape[2]), g.dtype) for g in grads]
        self.aliases = {}
        self.n_sems = len(grads)

    def _copy(self, ins, outs, send, recv, w):
        x, y, c = _position()
        half = ins[w].shape[1] // 2
        return pltpu.make_async_remote_copy(
            src_ref=ins[w].at[:, pl.ds((1 - c) * half, half)], dst_ref=outs[w],
            send_sem=send.at[w], recv_sem=recv.at[w], device_id=(x, y, 1 - c), device_id_type=MESH)

    def start(self, ins, outs, send, recv):
        for w in range(len(ins)):
            self._copy(ins, outs, send, recv, w).start()

    def finish(self, ins, outs, send, recv):
        for w in range(len(ins)):
            self._copy(ins, outs, send, recv, w).wait_recv()
        for w in range(len(ins)):
            self._copy(ins, outs, send, recv, w).wait_send()


class _ToChips:
    def __init__(self, parts):
        self.inputs = list(parts)
        self.out_shapes = [jax.ShapeDtypeStruct(p.shape, p.dtype) for p in parts]
        self.aliases = {}
        self.n_sems = 3 * len(parts)

    def _copy(self, ins, outs, send, recv, w, j, outbound):
        x, y, c = _position()
        px, py = _other_chips(x, y)[j]
        me, peer = 2 * x + y, 2 * px + py
        src_slot, dst_slot = (peer, me) if outbound else (me, peer)
        return pltpu.make_async_remote_copy(
            src_ref=ins[w].at[src_slot], dst_ref=outs[w].at[dst_slot],
            send_sem=send.at[3 * w + j], recv_sem=recv.at[3 * w + j], device_id=(px, py, c), device_id_type=MESH)

    def start(self, ins, outs, send, recv):
        for w in range(len(ins)):
            for j in range(3):
                self._copy(ins, outs, send, recv, w, j, True).start()

    def finish(self, ins, outs, send, recv):
        for w in range(len(ins)):
            for j in range(3):
                self._copy(ins, outs, send, recv, w, j, False).wait_recv()
        for w in range(len(ins)):
            for j in range(3):
                self._copy(ins, outs, send, recv, w, j, True).wait_send()


class _SwapHalves:
    def __init__(self, bufs):
        self.inputs = list(bufs)
        self.out_shapes = [jax.ShapeDtypeStruct(b.shape, b.dtype) for b in bufs]
        self.aliases = {w: w for w in range(len(bufs))}
        self.n_sems = len(bufs)

    def _copy(self, outs, send, recv, w, core):
        x, y, c = _position()
        half = outs[w].shape[1] // 2
        rows = outs[w].at[:, pl.ds(core * half, half)]
        return pltpu.make_async_remote_copy(
            src_ref=rows, dst_ref=rows, send_sem=send.at[w], recv_sem=recv.at[w],
            device_id=(x, y, 1 - c), device_id_type=MESH)

    def start(self, ins, outs, send, recv):
        c = lax.axis_index("c")
        for w in range(len(outs)):
            self._copy(outs, send, recv, w, c).start()

    def finish(self, ins, outs, send, recv):
        c = lax.axis_index("c")
        for w in range(len(outs)):
            self._copy(outs, send, recv, w, 1 - c).wait_recv()
        for w in range(len(outs)):
            self._copy(outs, send, recv, w, c).wait_send()


def _call(name, body, *, grid, in_specs, out_specs, out_shape, args, scratch_shapes=(), prefetch=(), exchanges=()):
    given = list(exchanges)
    exchanges = [e for e in given if e.inputs]
    n_pre, n_in, n_out, n_scr = len(prefetch), len(args), len(out_shape), len(scratch_shapes)
    ex_in = [a for e in exchanges for a in e.inputs]
    ex_out = [s for e in exchanges for s in e.out_shapes]
    aliases = {}
    at_in, at_out = n_pre + n_in, n_out
    for e in exchanges:
        for i, o in e.aliases.items():
            aliases[at_in + i] = at_out + o
        at_in += len(e.inputs)
        at_out += len(e.out_shapes)

    def at_step(last):
        hit = None
        for axis, n in enumerate(grid):
            here = pl.program_id(axis) == (n - 1 if last else 0)
            hit = here if hit is None else jnp.logical_and(hit, here)
        return hit

    def fused(*refs):
        pre, refs = refs[:n_pre], refs[n_pre:]
        body_in, refs = refs[:n_in], refs[n_in:]
        ex_in_refs, refs = refs[: len(ex_in)], refs[len(ex_in) :]
        body_out, refs = refs[:n_out], refs[n_out:]
        ex_out_refs, refs = refs[: len(ex_out)], refs[len(ex_out) :]
        body_scr, sems = refs[:n_scr], refs[n_scr:]

        def each(stage):
            a = b = 0
            for n, e in enumerate(exchanges):
                ins, outs = ex_in_refs[a : a + len(e.inputs)], ex_out_refs[b : b + len(e.out_shapes)]
                getattr(e, stage)(ins, outs, sems[2 * n], sems[2 * n + 1])
                a += len(e.inputs)
                b += len(e.out_shapes)

        if exchanges:
            @pl.when(at_step(False))
            def _():
                each("start")

        if body is not None:
            body(*pre, *body_in, *body_out, *body_scr)

        if exchanges:
            @pl.when(at_step(True))
            def _():
                each("finish")

    outs = pl.pallas_call(
        fused,
        name=name,
        grid_spec=pltpu.PrefetchScalarGridSpec(
            num_scalar_prefetch=n_pre,
            grid=grid,
            in_specs=list(in_specs) + [_ANY] * len(ex_in),
            out_specs=list(out_specs) + [_ANY] * len(ex_out),
            scratch_shapes=list(scratch_shapes)
            + [pltpu.SemaphoreType.DMA((e.n_sems,)) for e in exchanges for _ in range(2)],
        ),
        out_shape=list(out_shape) + ex_out,
        input_output_aliases=aliases,
        compiler_params=pltpu.CompilerParams(
            dimension_semantics=("arbitrary",) * len(grid), vmem_limit_bytes=VMEM_LIMIT_BYTES),
    )(*prefetch, *args, *ex_in)
    body_outs, rest = list(outs[:n_out]), list(outs[n_out:])
    ex_outs = []
    for e in given:
        n_e = len(e.out_shapes) if e.inputs else 0
        ex_outs.append(rest[:n_e])
        rest = rest[n_e:]
    return body_outs, ex_outs


def _exchange(name, exchanges):
    return _call(name, None, grid=(1,), in_specs=[], out_specs=[], out_shape=[], args=[], exchanges=exchanges)[1]


def _ffn_fwd(name, x, gpre, gpost, wg, wu, wd, target=None, exchanges=()):
    t, d = x.shape
    nb, fs, _ = wg.shape
    sub = _token_block(t)
    tm = _step_rows(t)
    with_loss = target is not None

    def body(x_ref, gpre_ref, gpost_ref, wg_ref, wu_ref, wd_ref, *refs):
        if with_loss:
            t_ref, o_ref, a_ref, dup_ref, dgate_ref, f_ref, sq_ref = refs

            @pl.when(pl.program_id(0) == 0)
            def _():
                sq_ref[...] = jnp.zeros_like(sq_ref)
        else:
            o_ref, a_ref, dup_ref, dgate_ref, f_ref = refs
        for h in range(tm // sub):
            rows = slice(h * sub, (h + 1) * sub)
            xv = x_ref[rows, :]
            hb = _rms(xv, gpre_ref[...]).astype(BF16)
            f = jnp.zeros((sub, d), F32)
            for k in range(nb):
                g = _dot_nt(hb, wg_ref[k])
                u = _dot_nt(hb, wu_ref[k])
                s = jax.nn.sigmoid(g)
                sg = g * s
                a = (sg * u).astype(BF16)
                a_ref[k, rows, :] = a
                dup_ref[k, rows, :] = sg.astype(BF16)
                dgate_ref[k, rows, :] = (u * (s * (1.0 + g * (1.0 - s)))).astype(BF16)
                f = f + _dot(a, wd_ref[k])
            f_ref[rows, :] = f
            y = xv + _rms(f, gpost_ref[...])
            if with_loss:
                err = y - t_ref[rows, :]
                o_ref[rows, :] = err / d
                sq_ref[...] += jnp.sum(err * err)
            else:
                o_ref[rows, :] = y

    return _call(
        name,
        body,
        grid=(t // tm,),
        in_specs=[_rows(tm, d), _VM, _VM, _VM, _VM, _VM] + [_rows(tm, d)] * with_loss,
        out_specs=[_rows(tm, d)] + [_blocks(nb, tm, fs)] * 3 + [_rows(tm, d)] + [_VM] * with_loss,
        out_shape=[jax.ShapeDtypeStruct((t, d), F32)]
        + [jax.ShapeDtypeStruct((nb, t, fs), BF16)] * 3
        + [jax.ShapeDtypeStruct((t, d), F32)]
        + [jax.ShapeDtypeStruct((8, 128), F32)] * with_loss,
        args=[x, gpre, gpost, wg, wu, wd] + [target] * with_loss,
        exchanges=exchanges,
    )


def _ffn_bwd_hidden(name, dy, f, gpost, a, dup, dgate, wd, exchanges=()):
    t, d = dy.shape
    nb, fs, _ = wd.shape
    tm = _step_rows(t)

    def body(dy_ref, f_ref, gpost_ref, a_ref, dup_ref, dgate_ref, wd_ref, dg_ref, du_ref, dwd_ref, dgain_ref):
        @pl.when(pl.program_id(0) == 0)
        def _():
            dwd_ref[...] = jnp.zeros_like(dwd_ref)
            dgain_ref[...] = jnp.zeros_like(dgain_ref)

        df, dgain = _rms_bwd(dy_ref[...], f_ref[...], gpost_ref[...])
        dgain_ref[...] += dgain
        dfb = df.astype(BF16)
        for k in range(nb):
            da = _dot_nt(dfb, wd_ref[k])
            dwd_ref[k] += _dot_tn(a_ref[k], dfb)
            du_ref[k] = (da * dup_ref[k].astype(F32)).astype(BF16)
            dg_ref[k] = (da * dgate_ref[k].astype(F32)).astype(BF16)

    return _call(
        name,
        body,
        grid=(t // tm,),
        in_specs=[_rows(tm, d), _rows(tm, d), _VM] + [_blocks(nb, tm, fs)] * 3 + [_VM],
        out_specs=[_blocks(nb, tm, fs), _blocks(nb, tm, fs), _VM, _VM],
        out_shape=[
            jax.ShapeDtypeStruct((nb, t, fs), BF16),
            jax.ShapeDtypeStruct((nb, t, fs), BF16),
            jax.ShapeDtypeStruct((nb, fs, d), F32),
            jax.ShapeDtypeStruct((1, d), F32),
        ],
        args=[dy, f, gpost, a, dup, dgate, wd],
        exchanges=exchanges,
    )


def _bwd_in(name, dres, x, gpre, dzs, ws, transposed=False, exchanges=()):
    t, d = x.shape
    n = len(ws)
    resident = sum(6 * w.size for w in ws)
    tm = _step_rows(t) if resident <= VMEM_LIMIT_BYTES // 2 else _token_block(t)
    widths = [w.shape[1] if transposed else w.shape[2] for w in ws]

    def body(*refs):
        dres_ref, x_ref, gpre_ref = refs[:3]
        dz_refs = refs[3 : 3 + n]
        w_refs = refs[3 + n : 3 + 2 * n]
        dx_ref = refs[3 + 2 * n]
        dw_refs = refs[4 + 2 * n : 4 + 3 * n]
        dgain_ref = refs[4 + 3 * n]

        @pl.when(pl.program_id(0) == 0)
        def _():
            for dw_ref in dw_refs:
                dw_ref[...] = jnp.zeros_like(dw_ref)
            dgain_ref[...] = jnp.zeros_like(dgain_ref)

        xv = x_ref[...]
        gain = gpre_ref[...]
        hb = _rms(xv, gain).astype(BF16)
        dh = jnp.zeros((tm, d), F32)
        for dz_ref, w_ref, dw_ref in zip(dz_refs, w_refs, dw_refs):
            for k in range(w_ref.shape[0]):
                dz = dz_ref[k]
                if transposed:
                    dh = dh + _dot(dz, w_ref[k])
                    dw_ref[k] += _dot_tn(dz, hb)
                else:
                    dh = dh + _dot_nt(dz, w_ref[k])
                    dw_ref[k] += _dot_tn(hb, dz)
        dx, dgain = _rms_bwd(dh, xv, gain)
        dx_ref[...] = dres_ref[...] + dx
        dgain_ref[...] += dgain

    return _call(
        name,
        body,
        grid=(t // tm,),
        in_specs=[_rows(tm, d), _rows(tm, d), _VM]
        + [_blocks(w.shape[0], tm, bw) for w, bw in zip(ws, widths)]
        + [_VM] * n,
        out_specs=[_rows(tm, d)] + [_VM] * n + [_VM],
        out_shape=[jax.ShapeDtypeStruct((t, d), F32)]
        + [jax.ShapeDtypeStruct(w.shape, F32) for w in ws]
        + [jax.ShapeDtypeStruct((1, d), F32)],
        args=[dres, x, gpre, *dzs, *ws],
        exchanges=exchanges,
    )


def _causal_weights(ws_ref):
    row = lax.broadcasted_iota(jnp.int32, (CHUNK, CHUNK), 0)
    col = lax.broadcasted_iota(jnp.int32, (CHUNK, CHUNK), 1)
    return [jnp.where(row >= col, ws_ref[g], 0.0).astype(BF16) for g in range(A_GROUPS)]


def _layernorm_halves(v0, v1):
    width = v0.shape[-1] + v1.shape[-1]
    mu = (jnp.sum(v0, axis=-1, keepdims=True) + jnp.sum(v1, axis=-1, keepdims=True)) / width
    c0 = v0 - mu
    c1 = v1 - mu
    var = (jnp.sum(c0 * c0, axis=-1, keepdims=True) + jnp.sum(c1 * c1, axis=-1, keepdims=True)) / width
    rstd = lax.rsqrt(var + EPS)
    return c0 * rstd, c1 * rstd, rstd


def _spatial_gate(sv_ref, wtril, vl, bt_ref, half, tm, gd):
    for gg in range(A_GROUPS // 2):
        g = half * (A_GROUPS // 2) + gg
        bias = bt_ref[:, g : g + 1]
        for n in range(tm // CHUNK):
            blk = vl[n * CHUNK : (n + 1) * CHUNK, gg * gd : (gg + 1) * gd]
            sv_ref[n * CHUNK : (n + 1) * CHUNK, gg * gd : (gg + 1) * gd] = _dot(wtril[g], blk) + bias


def _mix_a_fwd(name, x, gpre, gpost, w_in, ln_g, ln_b, w_s, b_t, w_out, exchanges=()):
    t, d = x.shape
    _, _, q = w_in.shape
    gd = 2 * q // A_GROUPS
    tm = _step_rows(t)

    def body(x_ref, gpre_ref, gpost_ref, win_ref, lng_ref, lnb_ref, ws_ref, bt_ref, wout_ref,
             o_ref, z_ref, dz_ref, m_ref, sv_ref):
        xv = x_ref[...]
        hb = _rms(xv, gpre_ref[...]).astype(BF16)
        z = [None] * 4
        vls = []
        for k in (2, 3, 0, 1):
            act, slope = _gelu_and_grad(_dot(hb, win_ref[k]))
            z_ref[k] = act.astype(BF16)
            dz_ref[k] = slope.astype(BF16)
            z[k] = act
            if k == 3:
                vh0, vh1, _ = _layernorm_halves(z[2], z[3])
                vls = [(vh * lng_ref[b : b + 1, :] + lnb_ref[b : b + 1, :]).astype(BF16)
                       for b, vh in enumerate((vh0, vh1))]
        wtril = _causal_weights(ws_ref)
        m = jnp.zeros((tm, d), F32)
        for b in range(2):
            _spatial_gate(sv_ref, wtril, vls[b], bt_ref, b, tm, gd)
            gated = (z[b] * sv_ref[...]).astype(BF16)
            m = m + _dot(gated, wout_ref[b])
        m_ref[...] = m
        o_ref[...] = xv + _rms(m, gpost_ref[...])

    return _call(
        name,
        body,
        grid=(t // tm,),
        in_specs=[_rows(tm, d)] + [_VM] * 8,
        out_specs=[_rows(tm, d), _blocks(4, tm, q), _blocks(4, tm, q), _rows(tm, d)],
        out_shape=[
            jax.ShapeDtypeStruct((t, d), F32),
            jax.ShapeDtypeStruct((4, t, q), BF16),
            jax.ShapeDtypeStruct((4, t, q), BF16),
            jax.ShapeDtypeStruct((t, d), F32),
        ],
        scratch_shapes=[pltpu.VMEM((tm, q), F32)],
        args=[x, gpre, gpost, w_in, ln_g, ln_b, w_s, b_t, w_out],
        exchanges=exchanges,
    )


def _mix_a_bwd_hidden(name, dy, m, gpost, act, slope, ln_g, ln_b, w_s, b_t, w_out, exchanges=()):
    t, d = dy.shape
    _, _, q = act.shape
    gd = 2 * q // A_GROUPS
    tm = _step_rows(t)
    n_chunks = tm // CHUNK

    def body(dy_ref, m_ref, gpost_ref, z_ref, slope_ref, lng_ref, lnb_ref, ws_ref, bt_ref, wout_ref,
             dz_ref, dwout_ref, dws_ref, dbacc_ref, dlng_ref, dlnb_ref, dgain_ref, sv_ref, dvl_ref):
        first = pl.program_id(0) == 0

        @pl.when(first)
        def _():
            for ref in (dwout_ref, dws_ref, dbacc_ref, dlng_ref, dlnb_ref, dgain_ref):
                ref[...] = jnp.zeros_like(ref)

        dm, dgain = _rms_bwd(dy_ref[...], m_ref[...], gpost_ref[...])
        dgain_ref[...] += dgain
        dmb = dm.astype(BF16)
        vhs = list(_layernorm_halves(z_ref[2].astype(F32), z_ref[3].astype(F32)))
        rstd = vhs.pop()
        vls = [(vh * lng_ref[b : b + 1, :] + lnb_ref[b : b + 1, :]).astype(BF16) for b, vh in enumerate(vhs)]
        wtril = _causal_weights(ws_ref)
        dvhs = []
        for b in range(2):
            u = z_ref[b].astype(F32)
            _spatial_gate(sv_ref, wtril, vls[b], bt_ref, b, tm, gd)
            sv = sv_ref[...]
            gated = (u * sv).astype(BF16)
            dgated = _dot_nt(dmb, wout_ref[b])
            dwout_ref[b] += _dot_tn(gated, dmb)
            dz_ref[b] = (dgated * sv * slope_ref[b].astype(F32)).astype(BF16)
            dsv = dgated * u
            folded = dsv[0:CHUNK, :]
            for c in range(1, n_chunks):
                folded = folded + dsv[c * CHUNK : (c + 1) * CHUNK, :]
            for gg in range(A_GROUPS // 2):
                g = b * (A_GROUPS // 2) + gg
                dbacc_ref[:, g : g + 1] += jnp.sum(folded[:, gg * gd : (gg + 1) * gd], axis=1, keepdims=True)
            dsvb = dsv.astype(BF16)
            for gg in range(A_GROUPS // 2):
                g = b * (A_GROUPS // 2) + gg
                for c in range(n_chunks):
                    rows = slice(c * CHUNK, (c + 1) * CHUNK)
                    cols = slice(gg * gd, (gg + 1) * gd)
                    blk = dsvb[rows, cols]
                    dvl_ref[rows, cols] = _dot_tn(wtril[g], blk)
                    dws_ref[g] += _dot_nt(blk, vls[b][rows, cols])
            dvl = dvl_ref[...]
            dlng_ref[b : b + 1, :] += jnp.sum(dvl * vhs[b], axis=0, keepdims=True)
            dlnb_ref[b : b + 1, :] += jnp.sum(dvl, axis=0, keepdims=True)
            dvhs.append(dvl * lng_ref[b : b + 1, :])
        width = 2.0 * q
        m1 = (jnp.sum(dvhs[0], axis=-1, keepdims=True) + jnp.sum(dvhs[1], axis=-1, keepdims=True)) / width
        m2 = (jnp.sum(dvhs[0] * vhs[0], axis=-1, keepdims=True)
              + jnp.sum(dvhs[1] * vhs[1], axis=-1, keepdims=True)) / width
        for b in range(2):
            dv = rstd * (dvhs[b] - m1 - vhs[b] * m2)
            dz_ref[2 + b] = (dv * slope_ref[2 + b].astype(F32)).astype(BF16)

        @pl.when(pl.program_id(0) == t // tm - 1)
        def _():
            row = lax.broadcasted_iota(jnp.int32, (CHUNK, CHUNK), 0)
            col = lax.broadcasted_iota(jnp.int32, (CHUNK, CHUNK), 1)
            for g in range(A_GROUPS):
                dws_ref[g] = jnp.where(row >= col, dws_ref[g], 0.0)

    return _call(
        name,
        body,
        grid=(t // tm,),
        in_specs=[_rows(tm, d), _rows(tm, d), _VM, _blocks(4, tm, q), _blocks(4, tm, q)] + [_VM] * 5,
        out_specs=[_blocks(4, tm, q)] + [_VM] * 6,
        out_shape=[
            jax.ShapeDtypeStruct((4, t, q), BF16),
            jax.ShapeDtypeStruct((2, q, d), F32),
            jax.ShapeDtypeStruct((A_GROUPS, CHUNK, CHUNK), F32),
            jax.ShapeDtypeStruct((CHUNK, A_GROUPS), F32),
            jax.ShapeDtypeStruct((2, q), F32),
            jax.ShapeDtypeStruct((2, q), F32),
            jax.ShapeDtypeStruct((1, d), F32),
        ],
        scratch_shapes=[pltpu.VMEM((tm, q), F32), pltpu.VMEM((tm, q), F32)],
        args=[dy, m, gpost, act, slope, ln_g, ln_b, w_s, b_t, w_out],
        exchanges=exchanges,
    )


def _window_counts(tm, win):
    pos = pl.program_id(0) * tm + lax.broadcasted_iota(jnp.int32, (tm, 1), 0)
    return jnp.minimum(pos + 1, win).astype(F32)


def _pooled(p, halo, tm, gd, inside=False):
    prev = halo if inside else jnp.where(pl.program_id(0) == 0, 0.0, halo)
    ext = jnp.concatenate([prev, p], axis=0)
    out = []
    for g, win in enumerate(B_WINDOWS):
        s = ext[:, g * gd : (g + 1) * gd]
        step = 1
        while step < win:
            s = s + pltpu.roll(s, step, 0)
            step *= 2
        total = s[HALO:, :]
        count = float(win) if inside else _window_counts(tm, win)
        out.append(total / count - p[:, g * gd : (g + 1) * gd])
    return out


def _halo_spec(t, tm, d, ahead):
    per = tm // HALO
    if ahead:
        return pl.BlockSpec((HALO, d), lambda i: (jnp.minimum((i + 1) * per, t // HALO - 1), 0))
    return pl.BlockSpec((HALO, d), lambda i: (jnp.maximum(i * per - 1, 0), 0))


def _mix_b_fwd(name, x, gpre, gpost, w_in, w_grp, scale, w_out, exchanges=()):
    t, d = x.shape
    gd = d // len(B_WINDOWS)
    tm = _step_rows(t)

    def body(x_ref, xh_ref, gpre_ref, gpost_ref, win_ref, wgrp_ref, scale_ref, wout_ref, o_ref, p_ref, m_ref):
        xv = x_ref[...]
        gain = gpre_ref[...]
        p = _dot(_rms(xv, gain).astype(BF16), win_ref[...])
        p_ref[...] = p
        halo = _dot(_rms(xh_ref[...], gain).astype(BF16), win_ref[...])
        pooled = _pooled(p, halo, tm, gd)
        mixed = jnp.concatenate([_dot(pg.astype(BF16), wgrp_ref[g]) for g, pg in enumerate(pooled)], axis=1)
        m = _dot((mixed * scale_ref[...]).astype(BF16), wout_ref[...])
        m_ref[...] = m
        o_ref[...] = xv + _rms(m, gpost_ref[...])

    return _call(
        name,
        body,
        grid=(t // tm,),
        in_specs=[_rows(tm, d), _halo_spec(t, tm, d, False), _VM, _VM, _VM, _VM, _VM, _VM],
        out_specs=[_rows(tm, d)] * 3,
        out_shape=[jax.ShapeDtypeStruct((t, d), F32)] * 3,
        args=[x, x, gpre, gpost, w_in, w_grp, scale, w_out],
        exchanges=exchanges,
    )


def _mix_b_bwd(name, dy, x, m, p, gpre, gpost, w_in, w_grp, scale, w_out, exchanges=()):
    t, d = dy.shape
    gd = d // len(B_WINDOWS)
    tm = _step_rows(t)
    n_steps = t // tm

    def body(dy_ref, x_ref, m_ref, p_ref, pprev_ref, dynext_ref, mnext_ref, pnext_ref,
             gpre_ref, gpost_ref, win_ref, wgrp_ref, scale_ref, wout_ref,
             dx_ref, dwout_ref, dwgrp_ref, dscale_ref, dpost_ref, dwin_ref, dpre_ref):
        @pl.when(pl.program_id(0) == 0)
        def _():
            for ref in (dwout_ref, dwgrp_ref, dscale_ref, dpost_ref, dwin_ref, dpre_ref):
                ref[...] = jnp.zeros_like(ref)

        scale = scale_ref[...]

        def tail_bwd(dy_rows, m_rows, pooled, counts, accumulate):
            dm, dgain = _rms_bwd(dy_rows, m_rows, gpost_ref[...])
            dmb = dm.astype(BF16)
            pooled = [pg.astype(BF16) for pg in pooled]
            mixed = jnp.concatenate([_dot(pg, wgrp_ref[g]) for g, pg in enumerate(pooled)], axis=1)
            dms = _dot_nt(dmb, wout_ref[...])
            dmixed = (dms * scale).astype(BF16)
            if accumulate:
                dpost_ref[...] += dgain
                dwout_ref[...] += _dot_tn((mixed * scale).astype(BF16), dmb)
                dscale_ref[...] += jnp.sum(dms * mixed, axis=0, keepdims=True)
            parts = []
            for g, win in enumerate(B_WINDOWS):
                dmg = dmixed[:, g * gd : (g + 1) * gd]
                if accumulate:
                    dwgrp_ref[g] += _dot_tn(pooled[g], dmg)
                parts.append(_dot_nt(dmg, wgrp_ref[g]) / counts(win))
            return jnp.concatenate(parts, axis=1)

        dyv = dy_ref[...]
        pv = p_ref[...]
        dq_blk = tail_bwd(dyv, m_ref[...], _pooled(pv, pprev_ref[...], tm, gd),
                          lambda win: _window_counts(tm, win), True)
        dq_next = tail_bwd(dynext_ref[...], mnext_ref[...],
                           _pooled(pnext_ref[...], pv[tm - HALO :, :], HALO, gd, inside=True),
                           float, False)
        dq_next = jnp.where(pl.program_id(0) == n_steps - 1, 0.0, dq_next)
        ext = jnp.concatenate([dq_blk, dq_next], axis=0)
        parts = []
        for g, win in enumerate(B_WINDOWS):
            cols = slice(g * gd, (g + 1) * gd)
            s = ext[:, cols]
            step = 1
            while step < win:
                s = s + pltpu.roll(s, tm + HALO - step, 0)
                step *= 2
            parts.append((s[:tm, :] - dq_blk[:, cols] * _window_counts(tm, win)).astype(BF16))
        dp = jnp.concatenate(parts, axis=1)
        xv = x_ref[...]
        gain = gpre_ref[...]
        hb = _rms(xv, gain).astype(BF16)
        dwin_ref[...] += _dot_tn(hb, dp)
        dx, dgain = _rms_bwd(_dot_nt(dp, win_ref[...]), xv, gain)
        dx_ref[...] = dyv + dx
        dpre_ref[...] += dgain

    before, after = _halo_spec(t, tm, d, False), _halo_spec(t, tm, d, True)
    return _call(
        name,
        body,
        grid=(n_steps,),
        in_specs=[_rows(tm, d)] * 4 + [before, after, after, after] + [_VM] * 6,
        out_specs=[_rows(tm, d)] + [_VM] * 6,
        out_shape=[
            jax.ShapeDtypeStruct((t, d), F32),
            jax.ShapeDtypeStruct((d, d), F32),
            jax.ShapeDtypeStruct((len(B_WINDOWS), gd, gd), F32),
            jax.ShapeDtypeStruct((1, d), F32),
            jax.ShapeDtypeStruct((1, d), F32),
            jax.ShapeDtypeStruct((d, d), F32),
            jax.ShapeDtypeStruct((1, d), F32),
        ],
        args=[dy, x, m, p, p, dy, m, p, gpre, gpost, w_in, w_grp, scale, w_out],
        exchanges=exchanges,
    )


def _cast_into_slots(name, place, w, dtype):
    n_layers, r, c = w.shape

    def body(place_ref, w_ref, *o_refs):
        del place_ref
        for j, o_ref in enumerate(o_refs):
            @pl.when(pl.program_id(0) == j)
            def _():
                o_ref[...] = w_ref[...].astype(dtype)

    return pl.pallas_call(
        body,
        name=name,
        grid_spec=pltpu.PrefetchScalarGridSpec(
            num_scalar_prefetch=1,
            grid=(n_layers,),
            in_specs=[pl.BlockSpec((1, r, c), lambda i, place_ref: (i, 0, 0))],
            out_specs=[pl.BlockSpec((1, r, c), lambda i, place_ref: (place_ref[0], 0, 0))] * n_layers,
        ),
        out_shape=[jax.ShapeDtypeStruct((N_CHIPS, r, c), dtype)] * n_layers,
        compiler_params=_params(),
    )(place, w)


def _row_tile(r):
    return 256 if r % 256 == 0 else r


def _pair_sum(name, place, dw, recv):
    _, r, c = dw.shape
    half = r // 2
    tr = _row_tile(half)
    per = half // tr

    def body(place_ref, a_ref, b_ref, o_ref):
        del place_ref
        o_ref[...] = (a_ref[...] + b_ref[...]).astype(BF16)

    return pl.pallas_call(
        body,
        name=name,
        grid_spec=pltpu.PrefetchScalarGridSpec(
            num_scalar_prefetch=1,
            grid=(N_CHIPS, per),
            in_specs=[
                pl.BlockSpec((1, tr, c), lambda k, i, place_ref: (k, place_ref[1] * per + i, 0)),
                pl.BlockSpec((1, tr, c), lambda k, i, place_ref: (k, i, 0)),
            ],
            out_specs=pl.BlockSpec((1, tr, c), lambda k, i, place_ref: (k, i, 0)),
        ),
        out_shape=jax.ShapeDtypeStruct(recv.shape, BF16),
        compiler_params=pltpu.CompilerParams(
            dimension_semantics=("arbitrary",) * 2, vmem_limit_bytes=VMEM_LIMIT_BYTES),
    )(place, dw, recv)


def _chip_sum(name, place, mine, others, exchanges=()):
    n_layers = len(mine)
    _, half, c = mine[0].shape
    tr = _row_tile(half)
    per = half // tr

    def body(place_ref, *refs):
        del place_ref
        o_ref = refs[-1]
        for j in range(n_layers):
            @pl.when(pl.program_id(0) == j)
            def _():
                parts = refs[4 * j : 4 * j + 4]
                acc = parts[0][...].astype(F32) + parts[1][...].astype(F32)
                acc = acc + parts[2][...].astype(F32)
                o_ref[...] = acc + parts[3][...].astype(F32)

    def part(j, flip):
        return pl.BlockSpec((1, tr, c), lambda l, i, place_ref: (
            jnp.bitwise_xor(place_ref[0], flip), jnp.where(l == j, i, 0), 0))

    args = []
    for j in range(n_layers):
        args += [mine[j], others[j], others[j], others[j]]
    (total,), ex_outs = _call(
        name,
        body,
        grid=(n_layers, per),
        in_specs=[part(j, flip) for j in range(n_layers) for flip in range(N_CHIPS)],
        out_specs=[pl.BlockSpec((1, tr, c), lambda l, i, place_ref: (l, place_ref[1] * per + i, 0))],
        out_shape=[jax.ShapeDtypeStruct((n_layers, 2 * half, c), F32)],
        args=args,
        prefetch=[place],
        exchanges=exchanges,
    )
    return total, ex_outs


def _adamw(name, w, g, m, v, exchanges=()):
    n_layers, r, c = w.shape
    tr = _row_tile(r)

    def body(w_ref, g_ref, m_ref, v_ref, go_ref, d_ref, nm_ref, nv_ref):
        gv = g_ref[...]
        go_ref[...] = gv
        nm = ADAM_B1 * m_ref[...] + (1.0 - ADAM_B1) * gv
        nv = ADAM_B2 * v_ref[...] + (1.0 - ADAM_B2) * jnp.square(gv)
        m_hat = nm / (1.0 - ADAM_B1 ** ADAM_STEP)
        v_hat = nv / (1.0 - ADAM_B2 ** ADAM_STEP)
        d_ref[...] = -ADAM_LR * (m_hat / (jnp.sqrt(v_hat) + ADAM_EPS) + ADAM_WD * w_ref[...])
        nm_ref[...] = nm
        nv_ref[...] = nv

    spec = pl.BlockSpec((1, tr, c), lambda l, i: (l, i, 0))
    return _call(
        name,
        body,
        grid=(n_layers, r // tr),
        in_specs=[spec] * 4,
        out_specs=[spec] * 4,
        out_shape=[jax.ShapeDtypeStruct(w.shape, F32)] * 4,
        args=[w, g, m, v],
        exchanges=exchanges,
    )


class _GatherSmall:
    def __init__(self, gathered):
        self.inputs = [gathered]
        self.out_shapes = [jax.ShapeDtypeStruct(gathered.shape, gathered.dtype)]
        self.aliases = {0: 0}
        self.n_sems = 4

    def _copies(self, outs, send, recv, outbound):
        x, y, c = _position()
        peers = [(x, y, 1 - c)] + [(px, py, c) for px, py in _other_chips(x, y)]
        copies = []
        for k, (px, py, pc) in enumerate(peers):
            rows = outs[0].at[4 * x + 2 * y + c if outbound else 4 * px + 2 * py + pc]
            copies.append(pltpu.make_async_remote_copy(
                src_ref=rows, dst_ref=rows, send_sem=send.at[k], recv_sem=recv.at[k],
                device_id=(px, py, pc), device_id_type=MESH))
        return copies

    def start(self, ins, outs, send, recv):
        for cp in self._copies(outs, send, recv, True):
            cp.start()

    def finish(self, ins, outs, send, recv):
        for cp in self._copies(outs, send, recv, False):
            cp.wait_recv()
        for cp in self._copies(outs, send, recv, True):
            cp.wait_send()


class _GatherSmallForward:
    def __init__(self, gathered):
        self.inputs = [gathered]
        self.out_shapes = [jax.ShapeDtypeStruct(gathered.shape, gathered.dtype)]
        self.aliases = {0: 0}
        self.n_sems = 3

    def _copies(self, outs, send, recv, core):
        x, y, c = _position()
        copies = []
        for k, (px, py) in enumerate(_other_chips(x, y)):
            rows = outs[0].at[4 * px + 2 * py + core]
            copies.append(pltpu.make_async_remote_copy(
                src_ref=rows, dst_ref=rows, send_sem=send.at[k], recv_sem=recv.at[k],
                device_id=(x, y, 1 - c), device_id_type=MESH))
        return copies

    def start(self, ins, outs, send, recv):
        for cp in self._copies(outs, send, recv, lax.axis_index("c")):
            cp.start()

    def finish(self, ins, outs, send, recv):
        c = lax.axis_index("c")
        for cp in self._copies(outs, send, recv, 1 - c):
            cp.wait_recv()
        for cp in self._copies(outs, send, recv, c):
            cp.wait_send()


def _sum_devices(gathered):
    _, m_per, n = gathered.shape

    def body(all_ref, sum_ref):
        acc = all_ref[0]
        for k in range(1, N_DEV):
            acc = acc + all_ref[k]
        sum_ref[...] = acc

    return pl.pallas_call(
        body,
        name="sum_devices",
        in_specs=[_VM],
        out_specs=_VM,
        out_shape=jax.ShapeDtypeStruct((m_per, n), F32),
        compiler_params=pltpu.CompilerParams(vmem_limit_bytes=VMEM_LIMIT_BYTES),
    )(gathered)


SHARDED = ("a_w_in", "a_w_out", "b_w_in", "b_w_grp", "b_scale", "b_w_out", "ffn_w_gate", "ffn_w_up", "ffn_w_down")
SMALL = ("a_ln_g", "a_ln_b", "a_w_s", "a_b_s", "mix_pre_g", "mix_post_g", "ffn_pre_g", "ffn_post_g")
WEIGHTS = ("a_w_in", "a_ln_g", "a_ln_b", "a_w_s", "a_b_s", "a_w_out", "b_w_in", "b_w_grp", "b_scale", "b_w_out",
           "mix_pre_g", "mix_post_g", "ffn_pre_g", "ffn_post_g", "ffn_w_gate", "ffn_w_up", "ffn_w_down")


TRANSPOSED = ("ffn_w_gate", "ffn_w_up")


def _as_layers(name, a):
    if name in TRANSPOSED:
        return jnp.swapaxes(a, 1, 2)
    if a.ndim == 2:
        return a.reshape(a.shape[0], 1, a.shape[1])
    return a.reshape(a.shape[0], -1, a.shape[-1])


def _from_layers(name, a, shape):
    if name in TRANSPOSED:
        return jnp.swapaxes(a, 1, 2)
    return a.reshape(shape)


def _pack_small(parts):
    return jnp.concatenate([p.reshape(-1, 128) for p in parts], axis=0)


def _unpack_small(packed, like):
    out, row = [], 0
    for ref in like:
        rows = ref.size // 128
        out.append(packed[row : row + rows].reshape(ref.shape))
        row += rows
    return out


def kernel(x, a_w_in, a_ln_g, a_ln_b, a_w_s, a_b_s, a_w_out, b_w_in, b_w_grp, b_scale, b_w_out, mix_pre_g, mix_post_g, ffn_pre_g, ffn_post_g, ffn_w_gate, ffn_w_up, ffn_w_down, loss_target, m_a_w_in, m_a_ln_g, m_a_ln_b, m_a_w_s, m_a_b_s, m_a_w_out, m_b_w_in, m_b_w_grp, m_b_scale, m_b_w_out, m_mix_pre_g, m_mix_post_g, m_ffn_pre_g, m_ffn_post_g, m_ffn_w_gate, m_ffn_w_up, m_ffn_w_down, v_a_w_in, v_a_ln_g, v_a_ln_b, v_a_w_s, v_a_b_s, v_a_w_out, v_b_w_in, v_b_w_grp, v_b_scale, v_b_w_out, v_mix_pre_g, v_mix_post_g, v_ffn_pre_g, v_ffn_post_g, v_ffn_w_gate, v_ffn_w_up, v_ffn_w_down):
    weights = dict(a_w_in=a_w_in, a_ln_g=a_ln_g, a_ln_b=a_ln_b, a_w_s=a_w_s, a_b_s=a_b_s, a_w_out=a_w_out,
                   b_w_in=b_w_in, b_w_grp=b_w_grp, b_scale=b_scale, b_w_out=b_w_out, mix_pre_g=mix_pre_g,
                   mix_post_g=mix_post_g, ffn_pre_g=ffn_pre_g, ffn_post_g=ffn_post_g, ffn_w_gate=ffn_w_gate,
                   ffn_w_up=ffn_w_up, ffn_w_down=ffn_w_down)
    mom1 = dict(a_w_in=m_a_w_in, a_ln_g=m_a_ln_g, a_ln_b=m_a_ln_b, a_w_s=m_a_w_s, a_b_s=m_a_b_s, a_w_out=m_a_w_out,
                b_w_in=m_b_w_in, b_w_grp=m_b_w_grp, b_scale=m_b_scale, b_w_out=m_b_w_out, mix_pre_g=m_mix_pre_g,
                mix_post_g=m_mix_post_g, ffn_pre_g=m_ffn_pre_g, ffn_post_g=m_ffn_post_g, ffn_w_gate=m_ffn_w_gate,
                ffn_w_up=m_ffn_w_up, ffn_w_down=m_ffn_w_down)
    mom2 = dict(a_w_in=v_a_w_in, a_ln_g=v_a_ln_g, a_ln_b=v_a_ln_b, a_w_s=v_a_w_s, a_b_s=v_a_b_s, a_w_out=v_a_w_out,
                b_w_in=v_b_w_in, b_w_grp=v_b_w_grp, b_scale=v_b_scale, b_w_out=v_b_w_out, mix_pre_g=v_mix_pre_g,
                mix_post_g=v_mix_post_g, ffn_pre_g=v_ffn_pre_g, ffn_post_g=v_ffn_post_g, ffn_w_gate=v_ffn_w_gate,
                ffn_w_up=v_ffn_w_up, ffn_w_down=v_ffn_w_down)

    t, d = x.shape[1], x.shape[2]
    depth = mix_pre_g.shape[0]
    gd_b = d // len(B_WINDOWS)
    xs = x.reshape(t, d)
    target = loss_target.reshape(t, d)

    chip = 2 * lax.axis_index("x") + lax.axis_index("y")
    place = jnp.stack([chip, lax.axis_index("c")]).astype(jnp.int32)
    bufs = {name: list(_cast_into_slots("cast_" + name, place, _as_layers(name, weights[name]),
                                        F32 if name == "b_scale" else BF16)) for name in SHARDED}

    def gain(name, i):
        return weights[name][i].reshape(1, d)

    def weight_keys(i):
        j = i // 2
        mixer = [("a_w_in", j), ("a_w_out", j)] if i % 2 == 0 else [("b_w_in", j), ("b_w_grp", j), ("b_w_out", j)]
        return mixer, [("ffn_w_gate", i), ("ffn_w_up", i), ("ffn_w_down", i)]

    def gather(keys):
        return _GatherWeights([bufs[n][j] for n, j in keys],
                              whole=[k for k, (n, _) in enumerate(keys) if n == "b_scale"])

    def gathered(keys, outs):
        for (n, j), buf in zip(keys, outs):
            bufs[n][j] = buf

    first = weight_keys(0)[0] + [("b_scale", j) for j in range(b_scale.shape[0])]
    gathered(first, _exchange("gather_first", [gather(first)])[0])
    saved = []
    cur = xs
    for i in range(depth):
        j = i // 2
        mixer_next, ffn_next = weight_keys(i + 1) if i + 1 < depth else ([], [])
        ffn_keys = weight_keys(i)[1]
        after_ffn = mixer_next if (i + 1) % 2 == 0 else mixer_next + ffn_next
        if i % 2 == 0:
            w_in = bufs["a_w_in"][j]
            q = w_in.shape[2]
            w_out = bufs["a_w_out"][j].reshape(2, q, d)
            ln_g = a_ln_g[j].reshape(2, q)
            ln_b = a_ln_b[j].reshape(2, q)
            b_t = jnp.transpose(a_b_s[j])
            (nxt, act, slope, m), (got,) = _mix_a_fwd(
                f"mix_a_fwd{j}", cur, gain("mix_pre_g", i), gain("mix_post_g", i), w_in, ln_g, ln_b, a_w_s[j], b_t,
                w_out, exchanges=[gather(ffn_keys)])
            gathered(ffn_keys, got)
            mix_saved = dict(x=cur, act=act, slope=slope, m=m, w_in=w_in, w_out=w_out, ln_g=ln_g, ln_b=ln_b, b_t=b_t)
        else:
            w_in = bufs["b_w_in"][j].reshape(d, d)
            w_out = bufs["b_w_out"][j].reshape(d, d)
            w_grp = jnp.transpose(bufs["b_w_grp"][j].reshape(N_CHIPS, len(B_WINDOWS), gd_b // N_CHIPS, gd_b),
                                  (1, 0, 2, 3)).reshape(len(B_WINDOWS), gd_b, gd_b)
            scale = bufs["b_scale"][j].reshape(1, d)
            (nxt, p, m), _ = _mix_b_fwd(f"mix_b_fwd{j}", cur, gain("mix_pre_g", i), gain("mix_post_g", i),
                                        w_in, w_grp, scale, w_out)
            mix_saved = dict(x=cur, p=p, m=m, w_in=w_in, w_out=w_out, w_grp=w_grp, scale=scale)
        cur = nxt
        wg, wu, wd = (bufs[n][k].reshape(1, -1, d) for n, k in ffn_keys)
        (nxt, a, dup, dgate, f, *sq), (got,) = _ffn_fwd(
            f"ffn_fwd{i}", cur, gain("ffn_pre_g", i), gain("ffn_post_g", i), wg, wu, wd,
            target=target if i == depth - 1 else None, exchanges=[gather(after_ffn)])
        gathered(after_ffn, got)
        saved.append((mix_saved, dict(x=cur, a=a, dup=dup, dgate=dgate, f=f, wg=wg, wu=wu, wd=wd)))
        cur = nxt

    dcur = cur
    loss = lax.psum(0.5 * sq[0][0, 0] / d, ("x", "y", "c"))

    grads = {name: [None] * weights[name].shape[0] for name in WEIGHTS}
    state = dict(to_sibling=[], to_chips=[])
    pair, from_chips = {}, {}

    def exchanges_due(carry=True):
        if not carry:
            return []
        return [_ToSibling([a for _, _, a in state["to_sibling"]]), _ToChips([a for _, _, a in state["to_chips"]])]

    def exchanged(outs):
        if not outs:
            return
        from_sibling, arrived = outs
        for (n, k, _), got in zip(state["to_chips"], arrived):
            from_chips[n, k] = got
        state["to_chips"] = []
        for (n, k, dw), got in zip(state["to_sibling"], from_sibling):
            pair[n, k] = _pair_sum(f"pair_sum_{n}{k}", place, dw, got)
            state["to_chips"].append((n, k, pair[n, k]))
        state["to_sibling"] = []

    def made(name, k, dw):
        grads[name][k] = dw
        state["to_sibling"].append((name, k, dw))

    for i in reversed(range(depth)):
        j = i // 2
        mix_saved, ffn_saved = saved[i]
        s = ffn_saved
        both = i == 0
        (dg, du, dwd, dgain), outs = _ffn_bwd_hidden(
            f"ffn_bwd_hidden{i}", dcur, s["f"], gain("ffn_post_g", i), s["a"], s["dup"], s["dgate"], s["wd"],
            exchanges=exchanges_due(both))
        exchanged(outs)
        grads["ffn_post_g"][i] = dgain
        made("ffn_w_down", i, dwd.reshape(N_CHIPS, -1, d))
        (dcur, dwg, dwu, dgain), outs = _bwd_in(
            f"ffn_bwd_in{i}", dcur, s["x"], gain("ffn_pre_g", i), [dg, du], [s["wg"], s["wu"]], transposed=True,
            exchanges=exchanges_due())
        exchanged(outs)
        grads["ffn_pre_g"][i] = dgain
        made("ffn_w_gate", i, dwg.reshape(N_CHIPS, -1, d))
        made("ffn_w_up", i, dwu.reshape(N_CHIPS, -1, d))
        s = mix_saved
        if i % 2 == 0:
            (dz, dwout, dws, dbacc, dlng, dlnb, dgain), outs = _mix_a_bwd_hidden(
                f"mix_a_bwd_hidden{j}", dcur, s["m"], gain("mix_post_g", i), s["act"], s["slope"], s["ln_g"], s["ln_b"],
                a_w_s[j], s["b_t"], s["w_out"], exchanges=exchanges_due(both))
            exchanged(outs)
            grads["a_w_s"][j] = dws
            grads["a_b_s"][j] = jnp.transpose(dbacc)
            grads["a_ln_g"][j] = dlng.reshape(-1)
            grads["a_ln_b"][j] = dlnb.reshape(-1)
            grads["mix_post_g"][i] = dgain
            made("a_w_out", j, dwout.reshape(N_CHIPS, -1, d))
            (dcur, dwin, dgain), outs = _bwd_in(
                f"mix_a_bwd_in{j}", dcur, s["x"], gain("mix_pre_g", i), [dz], [s["w_in"]], exchanges=exchanges_due())
            exchanged(outs)
            grads["mix_pre_g"][i] = dgain
            made("a_w_in", j, dwin)
        else:
            (dcur, dwout, dwgrp, dscale, dpost, dwin, dpre), outs = _mix_b_bwd(
                f"mix_b_bwd{j}", dcur, s["x"], s["m"], s["p"], gain("mix_pre_g", i), gain("mix_post_g", i),
                s["w_in"], s["w_grp"], s["scale"], s["w_out"], exchanges=exchanges_due())
            exchanged(outs)
            grads["b_scale"][j] = dscale
            grads["mix_post_g"][i] = dpost
            grads["mix_pre_g"][i] = dpre
            made("b_w_out", j, dwout.reshape(N_CHIPS, -1, d))
            made("b_w_grp", j, jnp.transpose(
                dwgrp.reshape(len(B_WINDOWS), N_CHIPS, gd_b // N_CHIPS, gd_b), (1, 0, 2, 3)).reshape(N_CHIPS, -1, gd_b))
            made("b_w_in", j, dwin.reshape(N_CHIPS, -1, d))
    grad_x = dcur.reshape(x.shape)

    small_grads = [jnp.stack([g.reshape(weights[name].shape[1:]) for g in grads[name]], axis=0) for name in SMALL]
    scale_grad = jnp.concatenate(grads["b_scale"], axis=0)
    packed = _pack_small(small_grads + [scale_grad])
    mine = lax.dynamic_update_slice(jnp.zeros((N_DEV,) + packed.shape, F32), packed[None], (2 * chip + place[1], 0, 0))
    outs = _exchange("grads_last", exchanges_due() + [_GatherSmall(mine)])
    exchanged(outs[:2])
    outs = _exchange("grads_last_to_chips", exchanges_due() + [_GatherSmallForward(outs[2][0])])
    exchanged(outs[:2])
    gathered_small = outs[2][0]

    reduced_names = [name for name in SHARDED if name != "b_scale"]
    sums = []
    for name in reduced_names:
        layers = range(weights[name].shape[0])
        sums.append(_chip_sum("chip_sum_" + name, place, [pair[name, k] for k in layers],
                              [from_chips[name, k] for k in layers])[0])
    reduced = dict(zip(reduced_names, _exchange("swap_halves", [_SwapHalves(sums)])[0]))

    out_g, out_d, out_m, out_v = {}, {}, {}, {}
    for name in reduced_names:
        shape = weights[name].shape
        results, _ = _adamw("adamw_" + name, _as_layers(name, weights[name]), reduced[name],
                            _as_layers(name, mom1[name]), _as_layers(name, mom2[name]))
        out_g[name], out_d[name], out_m[name], out_v[name] = (_from_layers(name, a, shape) for a in results)

    summed = _sum_devices(gathered_small)
    small_rows = summed.shape[0] - scale_grad.size // 128
    scale_sum = summed[small_rows:].reshape(scale_grad.shape)
    scale_mine = lax.dynamic_slice_in_dim(scale_sum, chip * b_scale.shape[1], b_scale.shape[1], axis=1)[:, None, :]
    summed = summed[:small_rows]
    results, _ = _adamw("adamw_b_scale", _as_layers("b_scale", b_scale), scale_mine,
                        _as_layers("b_scale", m_b_scale), _as_layers("b_scale", v_b_scale))
    out_g["b_scale"], out_d["b_scale"], out_m["b_scale"], out_v["b_scale"] = (a.reshape(b_scale.shape) for a in results)
    small_like = [weights[name] for name in SMALL]
    packs = [_pack_small([src[name] for name in SMALL]).reshape(1, -1, 128) for src in (weights, mom1, mom2)]
    results, _ = _adamw("adamw_small", packs[0], summed.reshape(1, -1, 128), packs[1], packs[2])
    for dst, packed in zip((out_g, out_d, out_m, out_v), (a[0] for a in results)):
        for name, val in zip(SMALL, _unpack_small(packed, small_like)):
            dst[name] = val

    return (loss, grad_x, *[out_g[n] for n in WEIGHTS], *[out_d[n] for n in WEIGHTS],
            *[out_m[n] for n in WEIGHTS], *[out_v[n] for n in WEIGHTS])
```

```python
import functools
import math

import jax
import jax.numpy as jnp
from jax import lax
from jax.experimental import pallas as pl
from jax.experimental.pallas import tpu as pltpu

F32 = jnp.float32
BF16 = jnp.bfloat16
MESH = pl.DeviceIdType.MESH

EPS = 1e-6
CHUNK = 128
A_GROUPS = 8
B_WINDOWS = (2, 4, 8, 16)
HALO = 16
N_CHIPS = 4
N_DEV = 8

ADAM_LR = 0.001
ADAM_B1 = 0.9
ADAM_B2 = 0.999
ADAM_EPS = 1e-08
ADAM_WD = 0.01
ADAM_STEP = 10

VMEM_LIMIT_BYTES = 60 * 1024 * 1024
INV_SQRT2 = 1.0 / math.sqrt(2.0)
INV_SQRT_2PI = 1.0 / math.sqrt(2.0 * math.pi)

_ANY = pl.BlockSpec(memory_space=pl.ANY)
_VM = pl.BlockSpec(memory_space=pltpu.VMEM)


def _params():
    return pltpu.CompilerParams(dimension_semantics=("arbitrary",), vmem_limit_bytes=VMEM_LIMIT_BYTES)


def _token_block(t):
    return 256 if t >= 1024 else 128


def _step_rows(t):
    return 2 * _token_block(t)


def _rows(tm, d):
    return pl.BlockSpec((tm, d), lambda i: (i, 0))


def _blocks(nb, tm, bw):
    return pl.BlockSpec((nb, tm, bw), lambda i: (0, i, 0))


def _dot(a, b):
    return lax.dot_general(a, b, (((1,), (0,)), ((), ())), preferred_element_type=F32)


def _dot_nt(a, b):
    return lax.dot_general(a, b, (((1,), (1,)), ((), ())), preferred_element_type=F32)


def _dot_tn(a, b):
    return lax.dot_general(a, b, (((0,), (0,)), ((), ())), preferred_element_type=F32)


def _rms(x, g):
    return x * lax.rsqrt(jnp.mean(x * x, axis=-1, keepdims=True) + EPS) * g


def _rms_bwd(dy, x, g):
    r = lax.rsqrt(jnp.mean(x * x, axis=-1, keepdims=True) + EPS)
    n = x * r
    dn = dy * g
    dx = r * (dn - n * jnp.mean(dn * n, axis=-1, keepdims=True))
    return dx, jnp.sum(dy * n, axis=0, keepdims=True)


def _gelu_and_grad(x):
    cdf = 0.5 * (1.0 + lax.erf(x * INV_SQRT2))
    return x * cdf, cdf + x * (jnp.exp(-0.5 * x * x) * INV_SQRT_2PI)


def _position():
    return lax.axis_index("x"), lax.axis_index("y"), lax.axis_index("c")


def _other_chips(x, y):
    return [(1 - x, y), (x, 1 - y), (1 - x, 1 - y)]


class _GatherWeights:
    def __init__(self, bufs, whole=()):
        self.inputs = list(bufs)
        self.out_shapes = [jax.ShapeDtypeStruct(b.shape, b.dtype) for b in bufs]
        self.aliases = {w: w for w in range(len(bufs))}
        self.n_sems = 6 * len(bufs)
        self.whole = frozenset(whole)

    def _part(self, outs, w, slot, core):
        if w in self.whole:
            return outs[w].at[slot]
        half = outs[w].shape[1] // 2
        return outs[w].at[slot, pl.ds(core * half, half)]

    def _ici(self, outs, send, recv, w, j, slot):
        x, y, c = _position()
        px, py = _other_chips(x, y)[j]
        part = self._part(outs, w, slot, c)
        return pltpu.make_async_remote_copy(
            src_ref=part, dst_ref=part, send_sem=send.at[6 * w + j], recv_sem=recv.at[6 * w + j],
            device_id=(px, py, c), device_id_type=MESH)

    def _d2d(self, outs, send, recv, w, j, slot, core):
        x, y, c = _position()
        part = self._part(outs, w, slot, core)
        return pltpu.make_async_remote_copy(
            src_ref=part, dst_ref=part, send_sem=send.at[6 * w + 3 + j], recv_sem=recv.at[6 * w + 3 + j],
            device_id=(x, y, 1 - c), device_id_type=MESH)

    def start(self, ins, outs, send, recv):
        x, y, _ = _position()
        for w in range(len(outs)):
            for j in range(3):
                self._ici(outs, send, recv, w, j, 2 * x + y).start()

    def advance(self, ins, outs, send, recv):
        x, y, c = _position()
        slots = [2 * px + py for px, py in _other_chips(x, y)]
        for w in range(len(outs)):
            for j, slot in enumerate(slots):
                self._ici(outs, send, recv, w, j, slot).wait_recv()
                if w not in self.whole:
                    self._d2d(outs, send, recv, w, j, slot, c).start()

    def finish(self, ins, outs, send, recv):
        x, y, c = _position()
        slots = [2 * px + py for px, py in _other_chips(x, y)]
        for w in range(len(outs)):
            for j, slot in enumerate(slots):
                if w not in self.whole:
                    self._d2d(outs, send, recv, w, j, slot, 1 - c).wait_recv()
        for w in range(len(outs)):
            for j, slot in enumerate(slots):
                self._ici(outs, send, recv, w, j, 2 * x + y).wait_send()
                if w not in self.whole:
                    self._d2d(outs, send, recv, w, j, slot, c).wait_send()


class _ToSibling:
    def __init__(self, grads):
        self.inputs = list(grads)
        self.out_shapes = [jax.ShapeDtypeStruct((g.shape[0], g.shape[1] // 2, g.shape[2]), g.dtype) for g in grads]
        self.aliases = {}
        self.n_sems = len(grads)

    def _copy(self, ins, outs, send, recv, w):
        x, y, c = _position()
        half = ins[w].shape[1] // 2
        return pltpu.make_async_remote_copy(
            src_ref=ins[w].at[:, pl.ds((1 - c) * half, half)], dst_ref=outs[w],
            send_sem=send.at[w], recv_sem=recv.at[w], device_id=(x, y, 1 - c), device_id_type=MESH)

    def start(self, ins, outs, send, recv):
        for w in range(len(ins)):
            self._copy(ins, outs, send, recv, w).start()

    def finish(self, ins, outs, send, recv):
        for w in range(len(ins)):
            self._copy(ins, outs, send, recv, w).wait_recv()
        for w in range(len(ins)):
            self._copy(ins, outs, send, recv, w).wait_send()


class _ToChips:
    def __init__(self, parts):
        self.inputs = list(parts)
        self.out_shapes = [jax.ShapeDtypeStruct(p.shape, p.dtype) for p in parts]
        self.aliases = {}
        self.n_sems = 3 * len(parts)

    def _copy(self, ins, outs, send, recv, w, j, outbound):
        x, y, c = _position()
        px, py = _other_chips(x, y)[j]
        me, peer = 2 * x + y, 2 * px + py
        src_slot, dst_slot = (peer, me) if outbound else (me, peer)
        return pltpu.make_async_remote_copy(
            src_ref=ins[w].at[src_slot], dst_ref=outs[w].at[dst_slot],
            send_sem=send.at[3 * w + j], recv_sem=recv.at[3 * w + j], device_id=(px, py, c), device_id_type=MESH)

    def start(self, ins, outs, send, recv):
        for w in range(len(ins)):
            for j in range(3):
                self._copy(ins, outs, send, recv, w, j, True).start()

    def finish(self, ins, outs, send, recv):
        for w in range(len(ins)):
            for j in range(3):
                self._copy(ins, outs, send, recv, w, j, False).wait_recv()
        for w in range(len(ins)):
            for j in range(3):
                self._copy(ins, outs, send, recv, w, j, True).wait_send()


class _SwapHalves:
    def __init__(self, bufs):
        self.inputs = list(bufs)
        self.out_shapes = [jax.ShapeDtypeStruct(b.shape, b.dtype) for b in bufs]
        self.aliases = {w: w for w in range(len(bufs))}
        self.n_sems = len(bufs)

    def _copy(self, outs, send, recv, w, core):
        x, y, c = _position()
        half = outs[w].shape[1] // 2
        rows = outs[w].at[:, pl.ds(core * half, half)]
        return pltpu.make_async_remote_copy(
            src_ref=rows, dst_ref=rows, send_sem=send.at[w], recv_sem=recv.at[w],
            device_id=(x, y, 1 - c), device_id_type=MESH)

    def start(self, ins, outs, send, recv):
        c = lax.axis_index("c")
        for w in range(len(outs)):
            self._copy(outs, send, recv, w, c).start()

    def finish(self, ins, outs, send, recv):
        c = lax.axis_index("c")
        for w in range(len(outs)):
            self._copy(outs, send, recv, w, 1 - c).wait_recv()
        for w in range(len(outs)):
            self._copy(outs, send, recv, w, c).wait_send()


def _call(name, body, *, grid, in_specs, out_specs, out_shape, args, scratch_shapes=(), prefetch=(), exchanges=()):
    given = list(exchanges)
    exchanges = [e for e in given if e.inputs]
    n_pre, n_in, n_out, n_scr = len(prefetch), len(args), len(out_shape), len(scratch_shapes)
    ex_in = [a for e in exchanges for a in e.inputs]
    ex_out = [s for e in exchanges for s in e.out_shapes]
    aliases = {}
    at_in, at_out = n_pre + n_in, n_out
    for e in exchanges:
        for i, o in e.aliases.items():
            aliases[at_in + i] = at_out + o
        at_in += len(e.inputs)
        at_out += len(e.out_shapes)

    def at_step(last):
        hit = None
        for axis, n in enumerate(grid):
            here = pl.program_id(axis) == (n - 1 if last else 0)
            hit = here if hit is None else jnp.logical_and(hit, here)
        return hit

    def fused(*refs):
        pre, refs = refs[:n_pre], refs[n_pre:]
        body_in, refs = refs[:n_in], refs[n_in:]
        ex_in_refs, refs = refs[: len(ex_in)], refs[len(ex_in) :]
        body_out, refs = refs[:n_out], refs[n_out:]
        ex_out_refs, refs = refs[: len(ex_out)], refs[len(ex_out) :]
        body_scr, sems = refs[:n_scr], refs[n_scr:]

        def each(stage):
            a = b = 0
            for n, e in enumerate(exchanges):
                ins, outs = ex_in_refs[a : a + len(e.inputs)], ex_out_refs[b : b + len(e.out_shapes)]
                if hasattr(e, stage):
                    getattr(e, stage)(ins, outs, sems[2 * n], sems[2 * n + 1])
                a += len(e.inputs)
                b += len(e.out_shapes)

        if exchanges:
            @pl.when(at_step(False))
            def _():
                each("start")

        if body is not None:
            body(*pre, *body_in, *body_out, *body_scr)

        if exchanges:
            late = pl.program_id(0) == (3 * grid[0]) // 4 if len(grid) == 1 else at_step(True)

            @pl.when(late)
            def _():
                each("advance")

            @pl.when(at_step(True))
            def _():
                each("finish")

    outs = pl.pallas_call(
        fused,
        name=name,
        grid_spec=pltpu.PrefetchScalarGridSpec(
            num_scalar_prefetch=n_pre,
            grid=grid,
            in_specs=list(in_specs) + [_ANY] * len(ex_in),
            out_specs=list(out_specs) + [_ANY] * len(ex_out),
            scratch_shapes=list(scratch_shapes)
            + [pltpu.SemaphoreType.DMA((e.n_sems,)) for e in exchanges for _ in range(2)],
        ),
        out_shape=list(out_shape) + ex_out,
        input_output_aliases=aliases,
        compiler_params=pltpu.CompilerParams(
            dimension_semantics=("arbitrary",) * len(grid), vmem_limit_bytes=VMEM_LIMIT_BYTES),
    )(*prefetch, *args, *ex_in)
    body_outs, rest = list(outs[:n_out]), list(outs[n_out:])
    ex_outs = []
    for e in given:
        n_e = len(e.out_shapes) if e.inputs else 0
        ex_outs.append(rest[:n_e])
        rest = rest[n_e:]
    return body_outs, ex_outs


def _exchange(name, exchanges):
    return _call(name, None, grid=(1,), in_specs=[], out_specs=[], out_shape=[], args=[], exchanges=exchanges)[1]


def _ffn_fwd(name, x, gpre, gpost, wg, wu, wd, target=None, exchanges=()):
    t, d = x.shape
    nb, fs, _ = wg.shape
    sub = _token_block(t)
    tm = _step_rows(t)
    with_loss = target is not None

    def body(x_ref, gpre_ref, gpost_ref, wg_ref, wu_ref, wd_ref, *refs):
        if with_loss:
            t_ref, o_ref, a_ref, dup_ref, dgate_ref, f_ref, sq_ref = refs

            @pl.when(pl.program_id(0) == 0)
            def _():
                sq_ref[...] = jnp.zeros_like(sq_ref)
        else:
            o_ref, a_ref, dup_ref, dgate_ref, f_ref = refs
        for h in range(tm // sub):
            rows = slice(h * sub, (h + 1) * sub)
            xv = x_ref[rows, :]
            hb = _rms(xv, gpre_ref[...]).astype(BF16)
            f = jnp.zeros((sub, d), F32)
            for k in range(nb):
                g = _dot_nt(hb, wg_ref[k])
                u = _dot_nt(hb, wu_ref[k])
                s = jax.nn.sigmoid(g)
                sg = g * s
                a = (sg * u).astype(BF16)
                a_ref[k, rows, :] = a
                dup_ref[k, rows, :] = sg.astype(BF16)
                dgate_ref[k, rows, :] = (u * (s * (1.0 + g * (1.0 - s)))).astype(BF16)
                f = f + _dot(a, wd_ref[k])
            f_ref[rows, :] = f
            y = xv + _rms(f, gpost_ref[...])
            if with_loss:
                err = y - t_ref[rows, :]
                o_ref[rows, :] = err / d
                sq_ref[...] += jnp.sum(err * err)
            else:
                o_ref[rows, :] = y

    return _call(
        name,
        body,
        grid=(t // tm,),
        in_specs=[_rows(tm, d), _VM, _VM, _VM, _VM, _VM] + [_rows(tm, d)] * with_loss,
        out_specs=[_rows(tm, d)] + [_blocks(nb, tm, fs)] * 3 + [_rows(tm, d)] + [_VM] * with_loss,
        out_shape=[jax.ShapeDtypeStruct((t, d), F32)]
        + [jax.ShapeDtypeStruct((nb, t, fs), BF16)] * 3
        + [jax.ShapeDtypeStruct((t, d), F32)]
        + [jax.ShapeDtypeStruct((8, 128), F32)] * with_loss,
        args=[x, gpre, gpost, wg, wu, wd] + [target] * with_loss,
        exchanges=exchanges,
    )


def _ffn_bwd_hidden(name, dy, f, gpost, a, dup, dgate, wd, exchanges=()):
    t, d = dy.shape
    nb, fs, _ = wd.shape
    tm = _step_rows(t)

    def body(dy_ref, f_ref, gpost_ref, a_ref, dup_ref, dgate_ref, wd_ref, dg_ref, du_ref, dwd_ref, dgain_ref):
        @pl.when(pl.program_id(0) == 0)
        def _():
            dwd_ref[...] = jnp.zeros_like(dwd_ref)
            dgain_ref[...] = jnp.zeros_like(dgain_ref)

        df, dgain = _rms_bwd(dy_ref[...], f_ref[...], gpost_ref[...])
        dgain_ref[...] += dgain
        dfb = df.astype(BF16)
        for k in range(nb):
            da = _dot_nt(dfb, wd_ref[k])
            dwd_ref[k] += _dot_tn(a_ref[k], dfb)
            du_ref[k] = (da * dup_ref[k].astype(F32)).astype(BF16)
            dg_ref[k] = (da * dgate_ref[k].astype(F32)).astype(BF16)

    return _call(
        name,
        body,
        grid=(t // tm,),
        in_specs=[_rows(tm, d), _rows(tm, d), _VM] + [_blocks(nb, tm, fs)] * 3 + [_VM],
        out_specs=[_blocks(nb, tm, fs), _blocks(nb, tm, fs), _VM, _VM],
        out_shape=[
            jax.ShapeDtypeStruct((nb, t, fs), BF16),
            jax.ShapeDtypeStruct((nb, t, fs), BF16),
            jax.ShapeDtypeStruct((nb, fs, d), F32),
            jax.ShapeDtypeStruct((1, d), F32),
        ],
        args=[dy, f, gpost, a, dup, dgate, wd],
        exchanges=exchanges,
    )


def _bwd_in(name, dres, x, gpre, dzs, ws, transposed=False, exchanges=()):
    t, d = x.shape
    n = len(ws)
    resident = sum(6 * w.size for w in ws)
    tm = _step_rows(t) if resident <= VMEM_LIMIT_BYTES // 2 else _token_block(t)
    widths = [w.shape[1] if transposed else w.shape[2] for w in ws]

    def body(*refs):
        dres_ref, x_ref, gpre_ref = refs[:3]
        dz_refs = refs[3 : 3 + n]
        w_refs = refs[3 + n : 3 + 2 * n]
        dx_ref = refs[3 + 2 * n]
        dw_refs = refs[4 + 2 * n : 4 + 3 * n]
        dgain_ref = refs[4 + 3 * n]

        @pl.when(pl.program_id(0) == 0)
        def _():
            for dw_ref in dw_refs:
                dw_ref[...] = jnp.zeros_like(dw_ref)
            dgain_ref[...] = jnp.zeros_like(dgain_ref)

        xv = x_ref[...]
        gain = gpre_ref[...]
        hb = _rms(xv, gain).astype(BF16)
        dh = jnp.zeros((tm, d), F32)
        for dz_ref, w_ref, dw_ref in zip(dz_refs, w_refs, dw_refs):
            for k in range(w_ref.shape[0]):
                dz = dz_ref[k]
                if transposed:
                    dh = dh + _dot(dz, w_ref[k])
                    dw_ref[k] += _dot_tn(dz, hb)
                else:
                    dh = dh + _dot_nt(dz, w_ref[k])
                    dw_ref[k] += _dot_tn(hb, dz)
        dx, dgain = _rms_bwd(dh, xv, gain)
        dx_ref[...] = dres_ref[...] + dx
        dgain_ref[...] += dgain

    return _call(
        name,
        body,
        grid=(t // tm,),
        in_specs=[_rows(tm, d), _rows(tm, d), _VM]
        + [_blocks(w.shape[0], tm, bw) for w, bw in zip(ws, widths)]
        + [_VM] * n,
        out_specs=[_rows(tm, d)] + [_VM] * n + [_VM],
        out_shape=[jax.ShapeDtypeStruct((t, d), F32)]
        + [jax.ShapeDtypeStruct(w.shape, F32) for w in ws]
        + [jax.ShapeDtypeStruct((1, d), F32)],
        args=[dres, x, gpre, *dzs, *ws],
        exchanges=exchanges,
    )


def _causal_weights(ws_ref):
    row = lax.broadcasted_iota(jnp.int32, (CHUNK, CHUNK), 0)
    col = lax.broadcasted_iota(jnp.int32, (CHUNK, CHUNK), 1)
    return [jnp.where(row >= col, ws_ref[g], 0.0).astype(BF16) for g in range(A_GROUPS)]


def _layernorm_halves(v0, v1):
    width = v0.shape[-1] + v1.shape[-1]
    mu = (jnp.sum(v0, axis=-1, keepdims=True) + jnp.sum(v1, axis=-1, keepdims=True)) / width
    c0 = v0 - mu
    c1 = v1 - mu
    var = (jnp.sum(c0 * c0, axis=-1, keepdims=True) + jnp.sum(c1 * c1, axis=-1, keepdims=True)) / width
    rstd = lax.rsqrt(var + EPS)
    return c0 * rstd, c1 * rstd, rstd


def _spatial_gate(sv_ref, wtril, vl, bt_ref, half, tm, gd):
    for gg in range(A_GROUPS // 2):
        g = half * (A_GROUPS // 2) + gg
        bias = bt_ref[:, g : g + 1]
        for n in range(tm // CHUNK):
            blk = vl[n * CHUNK : (n + 1) * CHUNK, gg * gd : (gg + 1) * gd]
            sv_ref[n * CHUNK : (n + 1) * CHUNK, gg * gd : (gg + 1) * gd] = _dot(wtril[g], blk) + bias


def _mix_a_fwd(name, x, gpre, gpost, w_in, ln_g, ln_b, w_s, b_t, w_out, exchanges=()):
    t, d = x.shape
    _, _, q = w_in.shape
    gd = 2 * q // A_GROUPS
    tm = _step_rows(t)

    def body(x_ref, gpre_ref, gpost_ref, win_ref, lng_ref, lnb_ref, ws_ref, bt_ref, wout_ref,
             o_ref, z_ref, dz_ref, m_ref, sv_ref):
        xv = x_ref[...]
        hb = _rms(xv, gpre_ref[...]).astype(BF16)
        z = [None] * 4
        vls = []
        for k in (2, 3, 0, 1):
            act, slope = _gelu_and_grad(_dot(hb, win_ref[k]))
            z_ref[k] = act.astype(BF16)
            dz_ref[k] = slope.astype(BF16)
            z[k] = act
            if k == 3:
                vh0, vh1, _ = _layernorm_halves(z[2], z[3])
                vls = [(vh * lng_ref[b : b + 1, :] + lnb_ref[b : b + 1, :]).astype(BF16)
                       for b, vh in enumerate((vh0, vh1))]
        wtril = _causal_weights(ws_ref)
        m = jnp.zeros((tm, d), F32)
        for b in range(2):
            _spatial_gate(sv_ref, wtril, vls[b], bt_ref, b, tm, gd)
            gated = (z[b] * sv_ref[...]).astype(BF16)
            m = m + _dot(gated, wout_ref[b])
        m_ref[...] = m
        o_ref[...] = xv + _rms(m, gpost_ref[...])

    return _call(
        name,
        body,
        grid=(t // tm,),
        in_specs=[_rows(tm, d)] + [_VM] * 8,
        out_specs=[_rows(tm, d), _blocks(4, tm, q), _blocks(4, tm, q), _rows(tm, d)],
        out_shape=[
            jax.ShapeDtypeStruct((t, d), F32),
            jax.ShapeDtypeStruct((4, t, q), BF16),
            jax.ShapeDtypeStruct((4, t, q), BF16),
            jax.ShapeDtypeStruct((t, d), F32),
        ],
        scratch_shapes=[pltpu.VMEM((tm, q), F32)],
        args=[x, gpre, gpost, w_in, ln_g, ln_b, w_s, b_t, w_out],
        exchanges=exchanges,
    )


def _mix_a_bwd_hidden(name, dy, m, gpost, act, slope, ln_g, ln_b, w_s, b_t, w_out, exchanges=()):
    t, d = dy.shape
    _, _, q = act.shape
    gd = 2 * q // A_GROUPS
    tm = _step_rows(t)
    n_chunks = tm // CHUNK

    def body(dy_ref, m_ref, gpost_ref, z_ref, slope_ref, lng_ref, lnb_ref, ws_ref, bt_ref, wout_ref,
             dz_ref, dwout_ref, dws_ref, dbacc_ref, dlng_ref, dlnb_ref, dgain_ref, sv_ref, dvl_ref):
        first = pl.program_id(0) == 0

        @pl.when(first)
        def _():
            for ref in (dwout_ref, dws_ref, dbacc_ref, dlng_ref, dlnb_ref, dgain_ref):
                ref[...] = jnp.zeros_like(ref)

        dm, dgain = _rms_bwd(dy_ref[...], m_ref[...], gpost_ref[...])
        dgain_ref[...] += dgain
        dmb = dm.astype(BF16)
        vhs = list(_layernorm_halves(z_ref[2].astype(F32), z_ref[3].astype(F32)))
        rstd = vhs.pop()
        vls = [(vh * lng_ref[b : b + 1, :] + lnb_ref[b : b + 1, :]).astype(BF16) for b, vh in enumerate(vhs)]
        wtril = _causal_weights(ws_ref)
        dvhs = []
        for b in range(2):
            u = z_ref[b].astype(F32)
            _spatial_gate(sv_ref, wtril, vls[b], bt_ref, b, tm, gd)
            sv = sv_ref[...]
            gated = (u * sv).astype(BF16)
            dgated = _dot_nt(dmb, wout_ref[b])
            dwout_ref[b] += _dot_tn(gated, dmb)
            dz_ref[b] = (dgated * sv * slope_ref[b].astype(F32)).astype(BF16)
            dsv = dgated * u
            folded = dsv[0:CHUNK, :]
            for c in range(1, n_chunks):
                folded = folded + dsv[c * CHUNK : (c + 1) * CHUNK, :]
            for gg in range(A_GROUPS // 2):
                g = b * (A_GROUPS // 2) + gg
                dbacc_ref[:, g : g + 1] += jnp.sum(folded[:, gg * gd : (gg + 1) * gd], axis=1, keepdims=True)
            dsvb = dsv.astype(BF16)
            for gg in range(A_GROUPS // 2):
                g = b * (A_GROUPS // 2) + gg
                for c in range(n_chunks):
                    rows = slice(c * CHUNK, (c + 1) * CHUNK)
                    cols = slice(gg * gd, (gg + 1) * gd)
                    blk = dsvb[rows, cols]
                    dvl_ref[rows, cols] = _dot_tn(wtril[g], blk)
                    dws_ref[g] += _dot_nt(blk, vls[b][rows, cols])
            dvl = dvl_ref[...]
            dlng_ref[b : b + 1, :] += jnp.sum(dvl * vhs[b], axis=0, keepdims=True)
            dlnb_ref[b : b + 1, :] += jnp.sum(dvl, axis=0, keepdims=True)
            dvhs.append(dvl * lng_ref[b : b + 1, :])
        width = 2.0 * q
        m1 = (jnp.sum(dvhs[0], axis=-1, keepdims=True) + jnp.sum(dvhs[1], axis=-1, keepdims=True)) / width
        m2 = (jnp.sum(dvhs[0] * vhs[0], axis=-1, keepdims=True)
              + jnp.sum(dvhs[1] * vhs[1], axis=-1, keepdims=True)) / width
        for b in range(2):
            dv = rstd * (dvhs[b] - m1 - vhs[b] * m2)
            dz_ref[2 + b] = (dv * slope_ref[2 + b].astype(F32)).astype(BF16)

        @pl.when(pl.program_id(0) == t // tm - 1)
        def _():
            row = lax.broadcasted_iota(jnp.int32, (CHUNK, CHUNK), 0)
            col = lax.broadcasted_iota(jnp.int32, (CHUNK, CHUNK), 1)
            for g in range(A_GROUPS):
                dws_ref[g] = jnp.where(row >= col, dws_ref[g], 0.0)

    return _call(
        name,
        body,
        grid=(t // tm,),
        in_specs=[_rows(tm, d), _rows(tm, d), _VM, _blocks(4, tm, q), _blocks(4, tm, q)] + [_VM] * 5,
        out_specs=[_blocks(4, tm, q)] + [_VM] * 6,
        out_shape=[
            jax.ShapeDtypeStruct((4, t, q), BF16),
            jax.ShapeDtypeStruct((2, q, d), F32),
            jax.ShapeDtypeStruct((A_GROUPS, CHUNK, CHUNK), F32),
            jax.ShapeDtypeStruct((CHUNK, A_GROUPS), F32),
            jax.ShapeDtypeStruct((2, q), F32),
            jax.ShapeDtypeStruct((2, q), F32),
            jax.ShapeDtypeStruct((1, d), F32),
        ],
        scratch_shapes=[pltpu.VMEM((tm, q), F32), pltpu.VMEM((tm, q), F32)],
        args=[dy, m, gpost, act, slope, ln_g, ln_b, w_s, b_t, w_out],
        exchanges=exchanges,
    )


def _window_counts(tm, win):
    pos = pl.program_id(0) * tm + lax.broadcasted_iota(jnp.int32, (tm, 1), 0)
    return jnp.minimum(pos + 1, win).astype(F32)


def _pooled(p, halo, tm, gd, inside=False):
    prev = halo if inside else jnp.where(pl.program_id(0) == 0, 0.0, halo)
    ext = jnp.concatenate([prev, p], axis=0)
    out = []
    for g, win in enumerate(B_WINDOWS):
        s = ext[:, g * gd : (g + 1) * gd]
        step = 1
        while step < win:
            s = s + pltpu.roll(s, step, 0)
            step *= 2
        total = s[HALO:, :]
        count = float(win) if inside else _window_counts(tm, win)
        out.append(total / count - p[:, g * gd : (g + 1) * gd])
    return out


def _halo_spec(t, tm, d, ahead):
    per = tm // HALO
    if ahead:
        return pl.BlockSpec((HALO, d), lambda i: (jnp.minimum((i + 1) * per, t // HALO - 1), 0))
    return pl.BlockSpec((HALO, d), lambda i: (jnp.maximum(i * per - 1, 0), 0))


def _mix_b_fwd(name, x, gpre, gpost, w_in, w_grp, scale, w_out, exchanges=()):
    t, d = x.shape
    gd = d // len(B_WINDOWS)
    tm = _step_rows(t)

    def body(x_ref, xh_ref, gpre_ref, gpost_ref, win_ref, wgrp_ref, scale_ref, wout_ref, o_ref, p_ref, m_ref):
        xv = x_ref[...]
        gain = gpre_ref[...]
        p = _dot(_rms(xv, gain).astype(BF16), win_ref[...])
        p_ref[...] = p
        halo = _dot(_rms(xh_ref[...], gain).astype(BF16), win_ref[...])
        pooled = _pooled(p, halo, tm, gd)
        mixed = jnp.concatenate([_dot(pg.astype(BF16), wgrp_ref[g]) for g, pg in enumerate(pooled)], axis=1)
        m = _dot((mixed * scale_ref[...]).astype(BF16), wout_ref[...])
        m_ref[...] = m
        o_ref[...] = xv + _rms(m, gpost_ref[...])

    return _call(
        name,
        body,
        grid=(t // tm,),
        in_specs=[_rows(tm, d), _halo_spec(t, tm, d, False), _VM, _VM, _VM, _VM, _VM, _VM],
        out_specs=[_rows(tm, d)] * 3,
        out_shape=[jax.ShapeDtypeStruct((t, d), F32)] * 3,
        args=[x, x, gpre, gpost, w_in, w_grp, scale, w_out],
        exchanges=exchanges,
    )


def _mix_b_bwd(name, dy, x, m, p, gpre, gpost, w_in, w_grp, scale, w_out, exchanges=()):
    t, d = dy.shape
    gd = d // len(B_WINDOWS)
    tm = _step_rows(t)
    n_steps = t // tm

    def body(dy_ref, x_ref, m_ref, p_ref, pprev_ref, dynext_ref, mnext_ref, pnext_ref,
             gpre_ref, gpost_ref, win_ref, wgrp_ref, scale_ref, wout_ref,
             dx_ref, dwout_ref, dwgrp_ref, dscale_ref, dpost_ref, dwin_ref, dpre_ref):
        @pl.when(pl.program_id(0) == 0)
        def _():
            for ref in (dwout_ref, dwgrp_ref, dscale_ref, dpost_ref, dwin_ref, dpre_ref):
                ref[...] = jnp.zeros_like(ref)

        scale = scale_ref[...]

        def tail_bwd(dy_rows, m_rows, pooled, counts, accumulate):
            dm, dgain = _rms_bwd(dy_rows, m_rows, gpost_ref[...])
            dmb = dm.astype(BF16)
            pooled = [pg.astype(BF16) for pg in pooled]
            mixed = jnp.concatenate([_dot(pg, wgrp_ref[g]) for g, pg in enumerate(pooled)], axis=1)
            dms = _dot_nt(dmb, wout_ref[...])
            dmixed = (dms * scale).astype(BF16)
            if accumulate:
                dpost_ref[...] += dgain
                dwout_ref[...] += _dot_tn((mixed * scale).astype(BF16), dmb)
                dscale_ref[...] += jnp.sum(dms * mixed, axis=0, keepdims=True)
            parts = []
            for g, win in enumerate(B_WINDOWS):
                dmg = dmixed[:, g * gd : (g + 1) * gd]
                if accumulate:
                    dwgrp_ref[g] += _dot_tn(pooled[g], dmg)
                parts.append(_dot_nt(dmg, wgrp_ref[g]) / counts(win))
            return jnp.concatenate(parts, axis=1)

        dyv = dy_ref[...]
        pv = p_ref[...]
        dq_blk = tail_bwd(dyv, m_ref[...], _pooled(pv, pprev_ref[...], tm, gd),
                          lambda win: _window_counts(tm, win), True)
        dq_next = tail_bwd(dynext_ref[...], mnext_ref[...],
                           _pooled(pnext_ref[...], pv[tm - HALO :, :], HALO, gd, inside=True),
                           float, False)
        dq_next = jnp.where(pl.program_id(0) == n_steps - 1, 0.0, dq_next)
        ext = jnp.concatenate([dq_blk, dq_next], axis=0)
        parts = []
        for g, win in enumerate(B_WINDOWS):
            cols = slice(g * gd, (g + 1) * gd)
            s = ext[:, cols]
            step = 1
            while step < win:
                s = s + pltpu.roll(s, tm + HALO - step, 0)
                step *= 2
            parts.append((s[:tm, :] - dq_blk[:, cols] * _window_counts(tm, win)).astype(BF16))
        dp = jnp.concatenate(parts, axis=1)
        xv = x_ref[...]
        gain = gpre_ref[...]
        hb = _rms(xv, gain).astype(BF16)
        dwin_ref[...] += _dot_tn(hb, dp)
        dx, dgain = _rms_bwd(_dot_nt(dp, win_ref[...]), xv, gain)
        dx_ref[...] = dyv + dx
        dpre_ref[...] += dgain

    before, after = _halo_spec(t, tm, d, False), _halo_spec(t, tm, d, True)
    return _call(
        name,
        body,
        grid=(n_steps,),
        in_specs=[_rows(tm, d)] * 4 + [before, after, after, after] + [_VM] * 6,
        out_specs=[_rows(tm, d)] + [_VM] * 6,
        out_shape=[
            jax.ShapeDtypeStruct((t, d), F32),
            jax.ShapeDtypeStruct((d, d), F32),
            jax.ShapeDtypeStruct((len(B_WINDOWS), gd, gd), F32),
            jax.ShapeDtypeStruct((1, d), F32),
            jax.ShapeDtypeStruct((1, d), F32),
            jax.ShapeDtypeStruct((d, d), F32),
            jax.ShapeDtypeStruct((1, d), F32),
        ],
        args=[dy, x, m, p, p, dy, m, p, gpre, gpost, w_in, w_grp, scale, w_out],
        exchanges=exchanges,
    )


def _cast_into_slots(name, place, w, dtype):
    n_layers, r, c = w.shape

    def body(place_ref, w_ref, *o_refs):
        del place_ref
        for j, o_ref in enumerate(o_refs):
            @pl.when(pl.program_id(0) == j)
            def _():
                o_ref[...] = w_ref[...].astype(dtype)

    return pl.pallas_call(
        body,
        name=name,
        grid_spec=pltpu.PrefetchScalarGridSpec(
            num_scalar_prefetch=1,
            grid=(n_layers,),
            in_specs=[pl.BlockSpec((1, r, c), lambda i, place_ref: (i, 0, 0))],
            out_specs=[pl.BlockSpec((1, r, c), lambda i, place_ref: (place_ref[0], 0, 0))] * n_layers,
        ),
        out_shape=[jax.ShapeDtypeStruct((N_CHIPS, r, c), dtype)] * n_layers,
        compiler_params=_params(),
    )(place, w)


def _row_tile(r):
    return 256 if r % 256 == 0 else r


def _pair_sum(name, place, dw, recv):
    _, r, c = dw.shape
    half = r // 2
    tr = _row_tile(half)
    per = half // tr

    def body(place_ref, a_ref, b_ref, o_ref):
        del place_ref
        o_ref[...] = (a_ref[...] + b_ref[...]).astype(BF16)

    return pl.pallas_call(
        body,
        name=name,
        grid_spec=pltpu.PrefetchScalarGridSpec(
            num_scalar_prefetch=1,
            grid=(N_CHIPS, per),
            in_specs=[
                pl.BlockSpec((1, tr, c), lambda k, i, place_ref: (k, place_ref[1] * per + i, 0)),
                pl.BlockSpec((1, tr, c), lambda k, i, place_ref: (k, i, 0)),
            ],
            out_specs=pl.BlockSpec((1, tr, c), lambda k, i, place_ref: (k, i, 0)),
        ),
        out_shape=jax.ShapeDtypeStruct(recv.shape, BF16),
        compiler_params=pltpu.CompilerParams(
            dimension_semantics=("arbitrary",) * 2, vmem_limit_bytes=VMEM_LIMIT_BYTES),
    )(place, dw, recv)


def _chip_sum(name, place, mine, others, exchanges=()):
    n_layers = len(mine)
    _, half, c = mine[0].shape
    tr = _row_tile(half)
    per = half // tr

    def body(place_ref, *refs):
        del place_ref
        o_ref = refs[-1]
        for j in range(n_layers):
            @pl.when(pl.program_id(0) == j)
            def _():
                parts = refs[4 * j : 4 * j + 4]
                acc = parts[0][...].astype(F32) + parts[1][...].astype(F32)
                acc = acc + parts[2][...].astype(F32)
                o_ref[...] = acc + parts[3][...].astype(F32)

    def part(j, flip):
        return pl.BlockSpec((1, tr, c), lambda l, i, place_ref: (
            jnp.bitwise_xor(place_ref[0], flip), jnp.where(l == j, i, 0), 0))

    args = []
    for j in range(n_layers):
        args += [mine[j], others[j], others[j], others[j]]
    (total,), ex_outs = _call(
        name,
        body,
        grid=(n_layers, per),
        in_specs=[part(j, flip) for j in range(n_layers) for flip in range(N_CHIPS)],
        out_specs=[pl.BlockSpec((1, tr, c), lambda l, i, place_ref: (l, place_ref[1] * per + i, 0))],
        out_shape=[jax.ShapeDtypeStruct((n_layers, 2 * half, c), F32)],
        args=args,
        prefetch=[place],
        exchanges=exchanges,
    )
    return total, ex_outs


def _adamw(name, w, g, m, v, exchanges=()):
    n_layers, r, c = w.shape
    tr = _row_tile(r)

    def body(w_ref, g_ref, m_ref, v_ref, go_ref, d_ref, nm_ref, nv_ref):
        gv = g_ref[...]
        go_ref[...] = gv
        nm = ADAM_B1 * m_ref[...] + (1.0 - ADAM_B1) * gv
        nv = ADAM_B2 * v_ref[...] + (1.0 - ADAM_B2) * jnp.square(gv)
        m_hat = nm / (1.0 - ADAM_B1 ** ADAM_STEP)
        v_hat = nv / (1.0 - ADAM_B2 ** ADAM_STEP)
        d_ref[...] = -ADAM_LR * (m_hat / (jnp.sqrt(v_hat) + ADAM_EPS) + ADAM_WD * w_ref[...])
        nm_ref[...] = nm
        nv_ref[...] = nv

    spec = pl.BlockSpec((1, tr, c), lambda l, i: (l, i, 0))
    return _call(
        name,
        body,
        grid=(n_layers, r // tr),
        in_specs=[spec] * 4,
        out_specs=[spec] * 4,
        out_shape=[jax.ShapeDtypeStruct(w.shape, F32)] * 4,
        args=[w, g, m, v],
        exchanges=exchanges,
    )


class _GatherSmall:
    def __init__(self, gathered):
        self.inputs = [gathered]
        self.out_shapes = [jax.ShapeDtypeStruct(gathered.shape, gathered.dtype)]
        self.aliases = {0: 0}
        self.n_sems = 4

    def _copies(self, outs, send, recv, outbound):
        x, y, c = _position()
        peers = [(x, y, 1 - c)] + [(px, py, c) for px, py in _other_chips(x, y)]
        copies = []
        for k, (px, py, pc) in enumerate(peers):
            rows = outs[0].at[4 * x + 2 * y + c if outbound else 4 * px + 2 * py + pc]
            copies.append(pltpu.make_async_remote_copy(
                src_ref=rows, dst_ref=rows, send_sem=send.at[k], recv_sem=recv.at[k],
                device_id=(px, py, pc), device_id_type=MESH))
        return copies

    def start(self, ins, outs, send, recv):
        for cp in self._copies(outs, send, recv, True):
            cp.start()

    def finish(self, ins, outs, send, recv):
        for cp in self._copies(outs, send, recv, False):
            cp.wait_recv()
        for cp in self._copies(outs, send, recv, True):
            cp.wait_send()


class _GatherSmallForward:
    def __init__(self, gathered):
        self.inputs = [gathered]
        self.out_shapes = [jax.ShapeDtypeStruct(gathered.shape, gathered.dtype)]
        self.aliases = {0: 0}
        self.n_sems = 3

    def _copies(self, outs, send, recv, core):
        x, y, c = _position()
        copies = []
        for k, (px, py) in enumerate(_other_chips(x, y)):
            rows = outs[0].at[4 * px + 2 * py + core]
            copies.append(pltpu.make_async_remote_copy(
                src_ref=rows, dst_ref=rows, send_sem=send.at[k], recv_sem=recv.at[k],
                device_id=(x, y, 1 - c), device_id_type=MESH))
        return copies

    def start(self, ins, outs, send, recv):
        for cp in self._copies(outs, send, recv, lax.axis_index("c")):
            cp.start()

    def finish(self, ins, outs, send, recv):
        c = lax.axis_index("c")
        for cp in self._copies(outs, send, recv, 1 - c):
            cp.wait_recv()
        for cp in self._copies(outs, send, recv, c):
            cp.wait_send()


def _sum_devices(gathered):
    _, m_per, n = gathered.shape

    def body(all_ref, sum_ref):
        acc = all_ref[0]
        for k in range(1, N_DEV):
            acc = acc + all_ref[k]
        sum_ref[...] = acc

    return pl.pallas_call(
        body,
        name="sum_devices",
        in_specs=[_VM],
        out_specs=_VM,
        out_shape=jax.ShapeDtypeStruct((m_per, n), F32),
        compiler_params=pltpu.CompilerParams(vmem_limit_bytes=VMEM_LIMIT_BYTES),
    )(gathered)


SHARDED = ("a_w_in", "a_w_out", "b_w_in", "b_w_grp", "b_scale", "b_w_out", "ffn_w_gate", "ffn_w_up", "ffn_w_down")
SMALL = ("a_ln_g", "a_ln_b", "a_w_s", "a_b_s", "mix_pre_g", "mix_post_g", "ffn_pre_g", "ffn_post_g")
WEIGHTS = ("a_w_in", "a_ln_g", "a_ln_b", "a_w_s", "a_b_s", "a_w_out", "b_w_in", "b_w_grp", "b_scale", "b_w_out",
           "mix_pre_g", "mix_post_g", "ffn_pre_g", "ffn_post_g", "ffn_w_gate", "ffn_w_up", "ffn_w_down")


TRANSPOSED = ("ffn_w_gate", "ffn_w_up")


def _as_layers(name, a):
    if name in TRANSPOSED:
        return jnp.swapaxes(a, 1, 2)
    if a.ndim == 2:
        return a.reshape(a.shape[0], 1, a.shape[1])
    return a.reshape(a.shape[0], -1, a.shape[-1])


def _from_layers(name, a, shape):
    if name in TRANSPOSED:
        return jnp.swapaxes(a, 1, 2)
    return a.reshape(shape)


def _pack_small(parts):
    return jnp.concatenate([p.reshape(-1, 128) for p in parts], axis=0)


def _unpack_small(packed, like):
    out, row = [], 0
    for ref in like:
        rows = ref.size // 128
        out.append(packed[row : row + rows].reshape(ref.shape))
        row += rows
    return out


def kernel(x, a_w_in, a_ln_g, a_ln_b, a_w_s, a_b_s, a_w_out, b_w_in, b_w_grp, b_scale, b_w_out, mix_pre_g, mix_post_g, ffn_pre_g, ffn_post_g, ffn_w_gate, ffn_w_up, ffn_w_down, loss_target, m_a_w_in, m_a_ln_g, m_a_ln_b, m_a_w_s, m_a_b_s, m_a_w_out, m_b_w_in, m_b_w_grp, m_b_scale, m_b_w_out, m_mix_pre_g, m_mix_post_g, m_ffn_pre_g, m_ffn_post_g, m_ffn_w_gate, m_ffn_w_up, m_ffn_w_down, v_a_w_in, v_a_ln_g, v_a_ln_b, v_a_w_s, v_a_b_s, v_a_w_out, v_b_w_in, v_b_w_grp, v_b_scale, v_b_w_out, v_mix_pre_g, v_mix_post_g, v_ffn_pre_g, v_ffn_post_g, v_ffn_w_gate, v_ffn_w_up, v_ffn_w_down):
    weights = dict(a_w_in=a_w_in, a_ln_g=a_ln_g, a_ln_b=a_ln_b, a_w_s=a_w_s, a_b_s=a_b_s, a_w_out=a_w_out,
                   b_w_in=b_w_in, b_w_grp=b_w_grp, b_scale=b_scale, b_w_out=b_w_out, mix_pre_g=mix_pre_g,
                   mix_post_g=mix_post_g, ffn_pre_g=ffn_pre_g, ffn_post_g=ffn_post_g, ffn_w_gate=ffn_w_gate,
                   ffn_w_up=ffn_w_up, ffn_w_down=ffn_w_down)
    mom1 = dict(a_w_in=m_a_w_in, a_ln_g=m_a_ln_g, a_ln_b=m_a_ln_b, a_w_s=m_a_w_s, a_b_s=m_a_b_s, a_w_out=m_a_w_out,
                b_w_in=m_b_w_in, b_w_grp=m_b_w_grp, b_scale=m_b_scale, b_w_out=m_b_w_out, mix_pre_g=m_mix_pre_g,
                mix_post_g=m_mix_post_g, ffn_pre_g=m_ffn_pre_g, ffn_post_g=m_ffn_post_g, ffn_w_gate=m_ffn_w_gate,
                ffn_w_up=m_ffn_w_up, ffn_w_down=m_ffn_w_down)
    mom2 = dict(a_w_in=v_a_w_in, a_ln_g=v_a_ln_g, a_ln_b=v_a_ln_b, a_w_s=v_a_w_s, a_b_s=v_a_b_s, a_w_out=v_a_w_out,
                b_w_in=v_b_w_in, b_w_grp=v_b_w_grp, b_scale=v_b_scale, b_w_out=v_b_w_out, mix_pre_g=v_mix_pre_g,
                mix_post_g=v_mix_post_g, ffn_pre_g=v_ffn_pre_g, ffn_post_g=v_ffn_post_g, ffn_w_gate=v_ffn_w_gate,
                ffn_w_up=v_ffn_w_up, ffn_w_down=v_ffn_w_down)

    t, d = x.shape[1], x.shape[2]
    depth = mix_pre_g.shape[0]
    gd_b = d // len(B_WINDOWS)
    xs = x.reshape(t, d)
    target = loss_target.reshape(t, d)

    chip = 2 * lax.axis_index("x") + lax.axis_index("y")
    place = jnp.stack([chip, lax.axis_index("c")]).astype(jnp.int32)
    bufs = {name: list(_cast_into_slots("cast_" + name, place, _as_layers(name, weights[name]),
                                        F32 if name == "b_scale" else BF16)) for name in SHARDED}

    def gain(name, i):
        return weights[name][i].reshape(1, d)

    def weight_keys(i):
        j = i // 2
        mixer = [("a_w_in", j), ("a_w_out", j)] if i % 2 == 0 else [("b_w_in", j), ("b_w_grp", j), ("b_w_out", j)]
        return mixer, [("ffn_w_gate", i), ("ffn_w_up", i), ("ffn_w_down", i)]

    def gather(keys):
        return _GatherWeights([bufs[n][j] for n, j in keys],
                              whole=[k for k, (n, _) in enumerate(keys) if n == "b_scale"])

    def gathered(keys, outs):
        for (n, j), buf in zip(keys, outs):
            bufs[n][j] = buf

    first = weight_keys(0)[0] + [("b_scale", j) for j in range(b_scale.shape[0])]
    gathered(first, _exchange("gather_first", [gather(first)])[0])
    saved = []
    cur = xs
    for i in range(depth):
        j = i // 2
        mixer_next, ffn_next = weight_keys(i + 1) if i + 1 < depth else ([], [])
        ffn_keys = weight_keys(i)[1]
        after_ffn = mixer_next if (i + 1) % 2 == 0 else mixer_next + ffn_next
        if i % 2 == 0:
            w_in = bufs["a_w_in"][j]
            q = w_in.shape[2]
            w_out = bufs["a_w_out"][j].reshape(2, q, d)
            ln_g = a_ln_g[j].reshape(2, q)
            ln_b = a_ln_b[j].reshape(2, q)
            b_t = jnp.transpose(a_b_s[j])
            (nxt, act, slope, m), (got,) = _mix_a_fwd(
                f"mix_a_fwd{j}", cur, gain("mix_pre_g", i), gain("mix_post_g", i), w_in, ln_g, ln_b, a_w_s[j], b_t,
                w_out, exchanges=[gather(ffn_keys)])
            gathered(ffn_keys, got)
            mix_saved = dict(x=cur, act=act, slope=slope, m=m, w_in=w_in, w_out=w_out, ln_g=ln_g, ln_b=ln_b, b_t=b_t)
        else:
            w_in = bufs["b_w_in"][j].reshape(d, d)
            w_out = bufs["b_w_out"][j].reshape(d, d)
            w_grp = jnp.transpose(bufs["b_w_grp"][j].reshape(N_CHIPS, len(B_WINDOWS), gd_b // N_CHIPS, gd_b),
                                  (1, 0, 2, 3)).reshape(len(B_WINDOWS), gd_b, gd_b)
            scale = bufs["b_scale"][j].reshape(1, d)
            (nxt, p, m), _ = _mix_b_fwd(f"mix_b_fwd{j}", cur, gain("mix_pre_g", i), gain("mix_post_g", i),
                                        w_in, w_grp, scale, w_out)
            mix_saved = dict(x=cur, p=p, m=m, w_in=w_in, w_out=w_out, w_grp=w_grp, scale=scale)
        cur = nxt
        wg, wu, wd = (bufs[n][k].reshape(1, -1, d) for n, k in ffn_keys)
        (nxt, a, dup, dgate, f, *sq), (got,) = _ffn_fwd(
            f"ffn_fwd{i}", cur, gain("ffn_pre_g", i), gain("ffn_post_g", i), wg, wu, wd,
            target=target if i == depth - 1 else None, exchanges=[gather(after_ffn)])
        gathered(after_ffn, got)
        saved.append((mix_saved, dict(x=cur, a=a, dup=dup, dgate=dgate, f=f, wg=wg, wu=wu, wd=wd)))
        cur = nxt

    dcur = cur
    loss = lax.psum(0.5 * sq[0][0, 0] / d, ("x", "y", "c"))

    grads = {name: [None] * weights[name].shape[0] for name in WEIGHTS}
    state = dict(to_sibling=[], to_chips=[])
    pair, from_chips = {}, {}

    def exchanges_due(carry=True):
        if not carry:
            return []
        return [_ToSibling([a for _, _, a in state["to_sibling"]]), _ToChips([a for _, _, a in state["to_chips"]])]

    def exchanged(outs):
        if not outs:
            return
        from_sibling, arrived = outs
        for (n, k, _), got in zip(state["to_chips"], arrived):
            from_chips[n, k] = got
        state["to_chips"] = []
        for (n, k, dw), got in zip(state["to_sibling"], from_sibling):
            pair[n, k] = _pair_sum(f"pair_sum_{n}{k}", place, dw, got)
            state["to_chips"].append((n, k, pair[n, k]))
        state["to_sibling"] = []

    def made(name, k, dw):
        grads[name][k] = dw
        state["to_sibling"].append((name, k, dw))

    for i in reversed(range(depth)):
        j = i // 2
        mix_saved, ffn_saved = saved[i]
        s = ffn_saved
        both = i == 0
        (dg, du, dwd, dgain), outs = _ffn_bwd_hidden(
            f"ffn_bwd_hidden{i}", dcur, s["f"], gain("ffn_post_g", i), s["a"], s["dup"], s["dgate"], s["wd"],
            exchanges=exchanges_due(both))
        exchanged(outs)
        grads["ffn_post_g"][i] = dgain
        made("ffn_w_down", i, dwd.reshape(N_CHIPS, -1, d))
        (dcur, dwg, dwu, dgain), outs = _bwd_in(
            f"ffn_bwd_in{i}", dcur, s["x"], gain("ffn_pre_g", i), [dg, du], [s["wg"], s["wu"]], transposed=True,
            exchanges=exchanges_due())
        exchanged(outs)
        grads["ffn_pre_g"][i] = dgain
        made("ffn_w_gate", i, dwg.reshape(N_CHIPS, -1, d))
        made("ffn_w_up", i, dwu.reshape(N_CHIPS, -1, d))
        s = mix_saved
        if i % 2 == 0:
            (dz, dwout, dws, dbacc, dlng, dlnb, dgain), outs = _mix_a_bwd_hidden(
                f"mix_a_bwd_hidden{j}", dcur, s["m"], gain("mix_post_g", i), s["act"], s["slope"], s["ln_g"], s["ln_b"],
                a_w_s[j], s["b_t"], s["w_out"], exchanges=exchanges_due(both))
            exchanged(outs)
            grads["a_w_s"][j] = dws
            grads["a_b_s"][j] = jnp.transpose(dbacc)
            grads["a_ln_g"][j] = dlng.reshape(-1)
            grads["a_ln_b"][j] = dlnb.reshape(-1)
            grads["mix_post_g"][i] = dgain
            made("a_w_out", j, dwout.reshape(N_CHIPS, -1, d))
            (dcur, dwin, dgain), outs = _bwd_in(
                f"mix_a_bwd_in{j}", dcur, s["x"], gain("mix_pre_g", i), [dz], [s["w_in"]], exchanges=exchanges_due())
            exchanged(outs)
            grads["mix_pre_g"][i] = dgain
            made("a_w_in", j, dwin)
        else:
            (dcur, dwout, dwgrp, dscale, dpost, dwin, dpre), outs = _mix_b_bwd(
                f"mix_b_bwd{j}", dcur, s["x"], s["m"], s["p"], gain("mix_pre_g", i), gain("mix_post_g", i),
                s["w_in"], s["w_grp"], s["scale"], s["w_out"], exchanges=exchanges_due())
            exchanged(outs)
            grads["b_scale"][j] = dscale
            grads["mix_post_g"][i] = dpost
            grads["mix_pre_g"][i] = dpre
            made("b_w_out", j, dwout.reshape(N_CHIPS, -1, d))
            made("b_w_grp", j, jnp.transpose(
                dwgrp.reshape(len(B_WINDOWS), N_CHIPS, gd_b // N_CHIPS, gd_b), (1, 0, 2, 3)).reshape(N_CHIPS, -1, gd_b))
            made("b_w_in", j, dwin.reshape(N_CHIPS, -1, d))
    grad_x = dcur.reshape(x.shape)

    small_grads = [jnp.stack([g.reshape(weights[name].shape[1:]) for g in grads[name]], axis=0) for name in SMALL]
    scale_grad = jnp.concatenate(grads["b_scale"], axis=0)
    packed = _pack_small(small_grads + [scale_grad])
    mine = lax.dynamic_update_slice(jnp.zeros((N_DEV,) + packed.shape, F32), packed[None], (2 * chip + place[1], 0, 0))
    outs = _exchange("grads_last", exchanges_due() + [_GatherSmall(mine)])
    exchanged(outs[:2])
    outs = _exchange("grads_last_to_chips", exchanges_due() + [_GatherSmallForward(outs[2][0])])
    exchanged(outs[:2])
    gathered_small = outs[2][0]

    reduced_names = [name for name in SHARDED if name != "b_scale"]
    sums = []
    for name in reduced_names:
        layers = range(weights[name].shape[0])
        sums.append(_chip_sum("chip_sum_" + name, place, [pair[name, k] for k in layers],
                              [from_chips[name, k] for k in layers])[0])
    reduced = dict(zip(reduced_names, _exchange("swap_halves", [_SwapHalves(sums)])[0]))

    out_g, out_d, out_m, out_v = {}, {}, {}, {}
    for name in reduced_names:
        shape = weights[name].shape
        results, _ = _adamw("adamw_" + name, _as_layers(name, weights[name]), reduced[name],
                            _as_layers(name, mom1[name]), _as_layers(name, mom2[name]))
        out_g[name], out_d[name], out_m[name], out_v[name] = (_from_layers(name, a, shape) for a in results)

    summed = _sum_devices(gathered_small)
    small_rows = summed.shape[0] - scale_grad.size // 128
    scale_sum = summed[small_rows:].reshape(scale_grad.shape)
    scale_mine = lax.dynamic_slice_in_dim(scale_sum, chip * b_scale.shape[1], b_scale.shape[1], axis=1)[:, None, :]
    summed = summed[:small_rows]
    results, _ = _adamw("adamw_b_scale", _as_layers("b_scale", b_scale), scale_mine,
                        _as_layers("b_scale", m_b_scale), _as_layers("b_scale", v_b_scale))
    out_g["b_scale"], out_d["b_scale"], out_m["b_scale"], out_v["b_scale"] = (a.reshape(b_scale.shape) for a in results)
    small_like = [weights[name] for name in SMALL]
    packs = [_pack_small([src[name] for name in SMALL]).reshape(1, -1, 128) for src in (weights, mom1, mom2)]
    results, _ = _adamw("adamw_small", packs[0], summed.reshape(1, -1, 128), packs[1], packs[2])
    for dst, packed in zip((out_g, out_d, out_m, out_v), (a[0] for a in results)):
        for name, val in zip(SMALL, _unpack_small(packed, small_like)):
            dst[name] = val

    return (loss, grad_x, *[out_g[n] for n in WEIGHTS], *[out_d[n] for n in WEIGHTS],
            *[out_m[n] for n in WEIGHTS], *[out_v[n] for n in WEIGHTS])
```

```python
import functools
import math

import jax
import jax.numpy as jnp
from jax import lax
from jax.experimental import pallas as pl
from jax.experimental.pallas import tpu as pltpu

F32 = jnp.float32
BF16 = jnp.bfloat16
MESH = pl.DeviceIdType.MESH

EPS = 1e-6
CHUNK = 128
A_GROUPS = 8
B_WINDOWS = (2, 4, 8, 16)
HALO = 16
N_CHIPS = 4
N_DEV = 8

ADAM_LR = 0.001
ADAM_B1 = 0.9
ADAM_B2 = 0.999
ADAM_EPS = 1e-08
ADAM_WD = 0.01
ADAM_STEP = 10

VMEM_LIMIT_BYTES = 60 * 1024 * 1024
INV_SQRT2 = 1.0 / math.sqrt(2.0)
INV_SQRT_2PI = 1.0 / math.sqrt(2.0 * math.pi)

_ANY = pl.BlockSpec(memory_space=pl.ANY)
_VM = pl.BlockSpec(memory_space=pltpu.VMEM)


def _params():
    return pltpu.CompilerParams(dimension_semantics=("arbitrary",), vmem_limit_bytes=VMEM_LIMIT_BYTES)


def _token_block(t):
    return 256 if t >= 1024 else 128


def _step_rows(t):
    return 2 * _token_block(t)


def _rows(tm, d):
    return pl.BlockSpec((tm, d), lambda i: (i, 0))


def _blocks(nb, tm, bw):
    return pl.BlockSpec((nb, tm, bw), lambda i: (0, i, 0))


def _dot(a, b):
    return lax.dot_general(a, b, (((1,), (0,)), ((), ())), preferred_element_type=F32)


def _dot_nt(a, b):
    return lax.dot_general(a, b, (((1,), (1,)), ((), ())), preferred_element_type=F32)


def _dot_tn(a, b):
    return lax.dot_general(a, b, (((0,), (0,)), ((), ())), preferred_element_type=F32)


def _rms(x, g):
    return x * lax.rsqrt(jnp.mean(x * x, axis=-1, keepdims=True) + EPS) * g


def _rms_bwd(dy, x, g):
    r = lax.rsqrt(jnp.mean(x * x, axis=-1, keepdims=True) + EPS)
    n = x * r
    dn = dy * g
    dx = r * (dn - n * jnp.mean(dn * n, axis=-1, keepdims=True))
    return dx, jnp.sum(dy * n, axis=0, keepdims=True)


def _gelu_and_grad(x):
    cdf = 0.5 * (1.0 + lax.erf(x * INV_SQRT2))
    return x * cdf, cdf + x * (jnp.exp(-0.5 * x * x) * INV_SQRT_2PI)


def _position():
    return lax.axis_index("x"), lax.axis_index("y"), lax.axis_index("c")


def _other_chips(x, y):
    return [(1 - x, y), (x, 1 - y), (1 - x, 1 - y)]


class _GatherWeights:
    def __init__(self, bufs, whole=()):
        self.inputs = list(bufs)
        self.out_shapes = [jax.ShapeDtypeStruct(b.shape, b.dtype) for b in bufs]
        self.aliases = {w: w for w in range(len(bufs))}
        self.n_sems = 6 * len(bufs)
        self.whole = frozenset(whole)

    def _part(self, outs, w, slot, core):
        if w in self.whole:
            return outs[w].at[slot]
        half = outs[w].shape[1] // 2
        return outs[w].at[slot, pl.ds(core * half, half)]

    def _ici(self, outs, send, recv, w, j, slot):
        x, y, c = _position()
        px, py = _other_chips(x, y)[j]
        part = self._part(outs, w, slot, c)
        return pltpu.make_async_remote_copy(
            src_ref=part, dst_ref=part, send_sem=send.at[6 * w + j], recv_sem=recv.at[6 * w + j],
            device_id=(px, py, c), device_id_type=MESH)

    def _d2d(self, outs, send, recv, w, j, slot, core):
        x, y, c = _position()
        part = self._part(outs, w, slot, core)
        return pltpu.make_async_remote_copy(
            src_ref=part, dst_ref=part, send_sem=send.at[6 * w + 3 + j], recv_sem=recv.at[6 * w + 3 + j],
            device_id=(x, y, 1 - c), device_id_type=MESH)

    def start(self, ins, outs, send, recv):
        x, y, _ = _position()
        for w in range(len(outs)):
            for j in range(3):
                self._ici(outs, send, recv, w, j, 2 * x + y).start()

    def advance(self, ins, outs, send, recv):
        x, y, c = _position()
        slots = [2 * px + py for px, py in _other_chips(x, y)]
        for w in range(len(outs)):
            for j, slot in enumerate(slots):
                self._ici(outs, send, recv, w, j, slot).wait_recv()
                if w not in self.whole:
                    self._d2d(outs, send, recv, w, j, slot, c).start()

    def finish(self, ins, outs, send, recv):
        x, y, c = _position()
        slots = [2 * px + py for px, py in _other_chips(x, y)]
        for w in range(len(outs)):
            for j, slot in enumerate(slots):
                if w not in self.whole:
                    self._d2d(outs, send, recv, w, j, slot, 1 - c).wait_recv()
        for w in range(len(outs)):
            for j, slot in enumerate(slots):
                self._ici(outs, send, recv, w, j, 2 * x + y).wait_send()
                if w not in self.whole:
                    self._d2d(outs, send, recv, w, j, slot, c).wait_send()


class _ToSibling:
    def __init__(self, grads):
        self.inputs = list(grads)
        self.out_shapes = [jax.ShapeDtypeStruct((g.shape[0], g.shape[1] // 2, g.shape[2]), g.dtype) for g in grads]
        self.aliases = {}
        self.n_sems = len(grads)

    def _copy(self, ins, outs, send, recv, w):
        x, y, c = _position()
        half = ins[w].shape[1] // 2
        return pltpu.make_async_remote_copy(
            src_ref=ins[w].at[:, pl.ds((1 - c) * half, half)], dst_ref=outs[w],
            send_sem=send.at[w], recv_sem=recv.at[w], device_id=(x, y, 1 - c), device_id_type=MESH)

    def start(self, ins, outs, send, recv):
        for w in range(len(ins)):
            self._copy(ins, outs, send, recv, w).start()

    def finish(self, ins, outs, send, recv):
        for w in range(len(ins)):
            self._copy(ins, outs, send, recv, w).wait_recv()
        for w in range(len(ins)):
            self._copy(ins, outs, send, recv, w).wait_send()


class _ToChips:
    def __init__(self, parts):
        self.inputs = list(parts)
        self.out_shapes = [jax.ShapeDtypeStruct(p.shape, p.dtype) for p in parts]
        self.aliases = {}
        self.n_sems = 3 * len(parts)

    def _copy(self, ins, outs, send, recv, w, j, outbound):
        x, y, c = _position()
        px, py = _other_chips(x, y)[j]
        me, peer = 2 * x + y, 2 * px + py
        src_slot, dst_slot = (peer, me) if outbound else (me, peer)
        return pltpu.make_async_remote_copy(
            src_ref=ins[w].at[src_slot], dst_ref=outs[w].at[dst_slot],
            send_sem=send.at[3 * w + j], recv_sem=recv.at[3 * w + j], device_id=(px, py, c), device_id_type=MESH)

    def start(self, ins, outs, send, recv):
        for w in range(len(ins)):
            for j in range(3):
                self._copy(ins, outs, send, recv, w, j, True).start()

    def finish(self, ins, outs, send, recv):
        for w in range(len(ins)):
            for j in range(3):
                self._copy(ins, outs, send, recv, w, j, False).wait_recv()
        for w in range(len(ins)):
            for j in range(3):
                self._copy(ins, outs, send, recv, w, j, True).wait_send()


class _SwapHalves:
    def __init__(self, bufs):
        self.inputs = list(bufs)
        self.out_shapes = [jax.ShapeDtypeStruct(b.shape, b.dtype) for b in bufs]
        self.aliases = {w: w for w in range(len(bufs))}
        self.n_sems = len(bufs)

    def _copy(self, outs, send, recv, w, core):
        x, y, c = _position()
        half = outs[w].shape[1] // 2
        rows = outs[w].at[:, pl.ds(core * half, half)]
        return pltpu.make_async_remote_copy(
            src_ref=rows, dst_ref=rows, send_sem=send.at[w], recv_sem=recv.at[w],
            device_id=(x, y, 1 - c), device_id_type=MESH)

    def start(self, ins, outs, send, recv):
        c = lax.axis_index("c")
        for w in range(len(outs)):
            self._copy(outs, send, recv, w, c).start()

    def finish(self, ins, outs, send, recv):
        c = lax.axis_index("c")
        for w in range(len(outs)):
            self._copy(outs, send, recv, w, 1 - c).wait_recv()
        for w in range(len(outs)):
            self._copy(outs, send, recv, w, c).wait_send()


def _call(name, body, *, grid, in_specs, out_specs, out_shape, args, scratch_shapes=(), prefetch=(), exchanges=()):
    given = list(exchanges)
    exchanges = [e for e in given if e.inputs]
    n_pre, n_in, n_out, n_scr = len(prefetch), len(args), len(out_shape), len(scratch_shapes)
    ex_in = [a for e in exchanges for a in e.inputs]
    ex_out = [s for e in exchanges for s in e.out_shapes]
    aliases = {}
    at_in, at_out = n_pre + n_in, n_out
    for e in exchanges:
        for i, o in e.aliases.items():
            aliases[at_in + i] = at_out + o
        at_in += len(e.inputs)
        at_out += len(e.out_shapes)

    def at_step(last):
        hit = None
        for axis, n in enumerate(grid):
            here = pl.program_id(axis) == (n - 1 if last else 0)
            hit = here if hit is None else jnp.logical_and(hit, here)
        return hit

    def fused(*refs):
        pre, refs = refs[:n_pre], refs[n_pre:]
        body_in, refs = refs[:n_in], refs[n_in:]
        ex_in_refs, refs = refs[: len(ex_in)], refs[len(ex_in) :]
        body_out, refs = refs[:n_out], refs[n_out:]
        ex_out_refs, refs = refs[: len(ex_out)], refs[len(ex_out) :]
        body_scr, sems = refs[:n_scr], refs[n_scr:]

        def each(stage):
            a = b = 0
            for n, e in enumerate(exchanges):
                ins, outs = ex_in_refs[a : a + len(e.inputs)], ex_out_refs[b : b + len(e.out_shapes)]
                if hasattr(e, stage):
                    getattr(e, stage)(ins, outs, sems[2 * n], sems[2 * n + 1])
                a += len(e.inputs)
                b += len(e.out_shapes)

        if exchanges:
            @pl.when(at_step(False))
            def _():
                each("start")

        if body is not None:
            body(*pre, *body_in, *body_out, *body_scr)

        if exchanges:
            late = pl.program_id(0) == (3 * grid[0]) // 4 if len(grid) == 1 else at_step(True)

            @pl.when(late)
            def _():
                each("advance")

            @pl.when(at_step(True))
            def _():
                each("finish")

    outs = pl.pallas_call(
        fused,
        name=name,
        grid_spec=pltpu.PrefetchScalarGridSpec(
            num_scalar_prefetch=n_pre,
            grid=grid,
            in_specs=list(in_specs) + [_ANY] * len(ex_in),
            out_specs=list(out_specs) + [_ANY] * len(ex_out),
            scratch_shapes=list(scratch_shapes)
            + [pltpu.SemaphoreType.DMA((e.n_sems,)) for e in exchanges for _ in range(2)],
        ),
        out_shape=list(out_shape) + ex_out,
        input_output_aliases=aliases,
        compiler_params=pltpu.CompilerParams(
            dimension_semantics=("arbitrary",) * len(grid), vmem_limit_bytes=VMEM_LIMIT_BYTES),
    )(*prefetch, *args, *ex_in)
    body_outs, rest = list(outs[:n_out]), list(outs[n_out:])
    ex_outs = []
    for e in given:
        n_e = len(e.out_shapes) if e.inputs else 0
        ex_outs.append(rest[:n_e])
        rest = rest[n_e:]
    return body_outs, ex_outs


def _exchange(name, exchanges):
    return _call(name, None, grid=(1,), in_specs=[], out_specs=[], out_shape=[], args=[], exchanges=exchanges)[1]


def _ffn_fwd(name, x, gpre, gpost, wg, wu, wd, target=None, exchanges=()):
    t, d = x.shape
    nb, fs, _ = wg.shape
    sub = _token_block(t)
    tm = _step_rows(t)
    with_loss = target is not None

    def body(x_ref, gpre_ref, gpost_ref, wg_ref, wu_ref, wd_ref, *refs):
        if with_loss:
            t_ref, o_ref, a_ref, dup_ref, dgate_ref, f_ref, sq_ref = refs

            @pl.when(pl.program_id(0) == 0)
            def _():
                sq_ref[...] = jnp.zeros_like(sq_ref)
        else:
            o_ref, a_ref, dup_ref, dgate_ref, f_ref = refs
        for h in range(tm // sub):
            rows = slice(h * sub, (h + 1) * sub)
            xv = x_ref[rows, :]
            hb = _rms(xv, gpre_ref[...]).astype(BF16)
            f = jnp.zeros((sub, d), F32)
            for k in range(nb):
                g = _dot_nt(hb, wg_ref[k])
                u = _dot_nt(hb, wu_ref[k])
                s = jax.nn.sigmoid(g)
                sg = g * s
                a = (sg * u).astype(BF16)
                a_ref[k, rows, :] = a
                dup_ref[k, rows, :] = sg.astype(BF16)
                dgate_ref[k, rows, :] = (u * (s * (1.0 + g * (1.0 - s)))).astype(BF16)
                f = f + _dot(a, wd_ref[k])
            f_ref[rows, :] = f
            y = xv + _rms(f, gpost_ref[...])
            if with_loss:
                err = y - t_ref[rows, :]
                o_ref[rows, :] = err / d
                sq_ref[...] += jnp.sum(err * err)
            else:
                o_ref[rows, :] = y

    return _call(
        name,
        body,
        grid=(t // tm,),
        in_specs=[_rows(tm, d), _VM, _VM, _VM, _VM, _VM] + [_rows(tm, d)] * with_loss,
        out_specs=[_rows(tm, d)] + [_blocks(nb, tm, fs)] * 3 + [_rows(tm, d)] + [_VM] * with_loss,
        out_shape=[jax.ShapeDtypeStruct((t, d), F32)]
        + [jax.ShapeDtypeStruct((nb, t, fs), BF16)] * 3
        + [jax.ShapeDtypeStruct((t, d), F32)]
        + [jax.ShapeDtypeStruct((8, 128), F32)] * with_loss,
        args=[x, gpre, gpost, wg, wu, wd] + [target] * with_loss,
        exchanges=exchanges,
    )


def _ffn_bwd_hidden(name, dy, f, gpost, a, dup, dgate, wd, exchanges=()):
    t, d = dy.shape
    nb, fs, _ = wd.shape
    tm = _step_rows(t)

    def body(dy_ref, f_ref, gpost_ref, a_ref, dup_ref, dgate_ref, wd_ref, dg_ref, du_ref, dwd_ref, dgain_ref):
        @pl.when(pl.program_id(0) == 0)
        def _():
            dwd_ref[...] = jnp.zeros_like(dwd_ref)
            dgain_ref[...] = jnp.zeros_like(dgain_ref)

        df, dgain = _rms_bwd(dy_ref[...], f_ref[...], gpost_ref[...])
        dgain_ref[...] += dgain
        dfb = df.astype(BF16)
        for k in range(nb):
            da = _dot_nt(dfb, wd_ref[k])
            dwd_ref[k] += _dot_tn(a_ref[k], dfb)
            du_ref[k] = (da * dup_ref[k].astype(F32)).astype(BF16)
            dg_ref[k] = (da * dgate_ref[k].astype(F32)).astype(BF16)

    return _call(
        name,
        body,
        grid=(t // tm,),
        in_specs=[_rows(tm, d), _rows(tm, d), _VM] + [_blocks(nb, tm, fs)] * 3 + [_VM],
        out_specs=[_blocks(nb, tm, fs), _blocks(nb, tm, fs), _VM, _VM],
        out_shape=[
            jax.ShapeDtypeStruct((nb, t, fs), BF16),
            jax.ShapeDtypeStruct((nb, t, fs), BF16),
            jax.ShapeDtypeStruct((nb, fs, d), F32),
            jax.ShapeDtypeStruct((1, d), F32),
        ],
        args=[dy, f, gpost, a, dup, dgate, wd],
        exchanges=exchanges,
    )


def _bwd_in(name, dres, x, gpre, dzs, ws, transposed=False, exchanges=()):
    t, d = x.shape
    n = len(ws)
    resident = sum(6 * w.size for w in ws)
    tm = _step_rows(t) if resident <= VMEM_LIMIT_BYTES // 2 else _token_block(t)
    widths = [w.shape[1] if transposed else w.shape[2] for w in ws]

    def body(*refs):
        dres_ref, x_ref, gpre_ref = refs[:3]
        dz_refs = refs[3 : 3 + n]
        w_refs = refs[3 + n : 3 + 2 * n]
        dx_ref = refs[3 + 2 * n]
        dw_refs = refs[4 + 2 * n : 4 + 3 * n]
        dgain_ref = refs[4 + 3 * n]

        @pl.when(pl.program_id(0) == 0)
        def _():
            for dw_ref in dw_refs:
                dw_ref[...] = jnp.zeros_like(dw_ref)
            dgain_ref[...] = jnp.zeros_like(dgain_ref)

        xv = x_ref[...]
        gain = gpre_ref[...]
        hb = _rms(xv, gain).astype(BF16)
        dh = jnp.zeros((tm, d), F32)
        for dz_ref, w_ref, dw_ref in zip(dz_refs, w_refs, dw_refs):
            for k in range(w_ref.shape[0]):
                dz = dz_ref[k]
                if transposed:
                    dh = dh + _dot(dz, w_ref[k])
                    dw_ref[k] += _dot_tn(dz, hb)
                else:
                    dh = dh + _dot_nt(dz, w_ref[k])
                    dw_ref[k] += _dot_tn(hb, dz)
        dx, dgain = _rms_bwd(dh, xv, gain)
        dx_ref[...] = dres_ref[...] + dx
        dgain_ref[...] += dgain

    return _call(
        name,
        body,
        grid=(t // tm,),
        in_specs=[_rows(tm, d), _rows(tm, d), _VM]
        + [_blocks(w.shape[0], tm, bw) for w, bw in zip(ws, widths)]
        + [_VM] * n,
        out_specs=[_rows(tm, d)] + [_VM] * n + [_VM],
        out_shape=[jax.ShapeDtypeStruct((t, d), F32)]
        + [jax.ShapeDtypeStruct(w.shape, F32) for w in ws]
        + [jax.ShapeDtypeStruct((1, d), F32)],
        args=[dres, x, gpre, *dzs, *ws],
        exchanges=exchanges,
    )


def _causal_weights(ws_ref):
    row = lax.broadcasted_iota(jnp.int32, (CHUNK, CHUNK), 0)
    col = lax.broadcasted_iota(jnp.int32, (CHUNK, CHUNK), 1)
    return [jnp.where(row >= col, ws_ref[g], 0.0).astype(BF16) for g in range(A_GROUPS)]


def _layernorm_halves(v0, v1):
    width = v0.shape[-1] + v1.shape[-1]
    mu = (jnp.sum(v0, axis=-1, keepdims=True) + jnp.sum(v1, axis=-1, keepdims=True)) / width
    c0 = v0 - mu
    c1 = v1 - mu
    var = (jnp.sum(c0 * c0, axis=-1, keepdims=True) + jnp.sum(c1 * c1, axis=-1, keepdims=True)) / width
    rstd = lax.rsqrt(var + EPS)
    return c0 * rstd, c1 * rstd, rstd


def _spatial_gate(sv_ref, wtril, vl, bt_ref, half, tm, gd):
    for gg in range(A_GROUPS // 2):
        g = half * (A_GROUPS // 2) + gg
        bias = bt_ref[:, g : g + 1]
        for n in range(tm // CHUNK):
            blk = vl[n * CHUNK : (n + 1) * CHUNK, gg * gd : (gg + 1) * gd]
            sv_ref[n * CHUNK : (n + 1) * CHUNK, gg * gd : (gg + 1) * gd] = _dot(wtril[g], blk) + bias


def _mix_a_fwd(name, x, gpre, gpost, w_in, ln_g, ln_b, w_s, b_t, w_out, exchanges=()):
    t, d = x.shape
    _, _, q = w_in.shape
    gd = 2 * q // A_GROUPS
    tm = _step_rows(t)

    def body(x_ref, gpre_ref, gpost_ref, win_ref, lng_ref, lnb_ref, ws_ref, bt_ref, wout_ref,
             o_ref, z_ref, dz_ref, m_ref, sv_ref):
        xv = x_ref[...]
        hb = _rms(xv, gpre_ref[...]).astype(BF16)
        z = [None] * 4
        vls = []
        for k in (2, 3, 0, 1):
            act, slope = _gelu_and_grad(_dot(hb, win_ref[k]))
            z_ref[k] = act.astype(BF16)
            dz_ref[k] = slope.astype(BF16)
            z[k] = act
            if k == 3:
                vh0, vh1, _ = _layernorm_halves(z[2], z[3])
                vls = [(vh * lng_ref[b : b + 1, :] + lnb_ref[b : b + 1, :]).astype(BF16)
                       for b, vh in enumerate((vh0, vh1))]
        wtril = _causal_weights(ws_ref)
        m = jnp.zeros((tm, d), F32)
        for b in range(2):
            _spatial_gate(sv_ref, wtril, vls[b], bt_ref, b, tm, gd)
            gated = (z[b] * sv_ref[...]).astype(BF16)
            m = m + _dot(gated, wout_ref[b])
        m_ref[...] = m
        o_ref[...] = xv + _rms(m, gpost_ref[...])

    return _call(
        name,
        body,
        grid=(t // tm,),
        in_specs=[_rows(tm, d)] + [_VM] * 8,
        out_specs=[_rows(tm, d), _blocks(4, tm, q), _blocks(4, tm, q), _rows(tm, d)],
        out_shape=[
            jax.ShapeDtypeStruct((t, d), F32),
            jax.ShapeDtypeStruct((4, t, q), BF16),
            jax.ShapeDtypeStruct((4, t, q), BF16),
            jax.ShapeDtypeStruct((t, d), F32),
        ],
        scratch_shapes=[pltpu.VMEM((tm, q), F32)],
        args=[x, gpre, gpost, w_in, ln_g, ln_b, w_s, b_t, w_out],
        exchanges=exchanges,
    )


def _mix_a_bwd_hidden(name, dy, m, gpost, act, slope, ln_g, ln_b, w_s, b_t, w_out, exchanges=()):
    t, d = dy.shape
    _, _, q = act.shape
    gd = 2 * q // A_GROUPS
    tm = _step_rows(t)
    n_chunks = tm // CHUNK

    def body(dy_ref, m_ref, gpost_ref, z_ref, slope_ref, lng_ref, lnb_ref, ws_ref, bt_ref, wout_ref,
             dz_ref, dwout_ref, dws_ref, dbacc_ref, dlng_ref, dlnb_ref, dgain_ref, sv_ref, dvl_ref):
        first = pl.program_id(0) == 0

        @pl.when(first)
        def _():
            for ref in (dwout_ref, dws_ref, dbacc_ref, dlng_ref, dlnb_ref, dgain_ref):
                ref[...] = jnp.zeros_like(ref)

        dm, dgain = _rms_bwd(dy_ref[...], m_ref[...], gpost_ref[...])
        dgain_ref[...] += dgain
        dmb = dm.astype(BF16)
        vhs = list(_layernorm_halves(z_ref[2].astype(F32), z_ref[3].astype(F32)))
        rstd = vhs.pop()
        vls = [(vh * lng_ref[b : b + 1, :] + lnb_ref[b : b + 1, :]).astype(BF16) for b, vh in enumerate(vhs)]
        wtril = _causal_weights(ws_ref)
        dvhs = []
        for b in range(2):
            u = z_ref[b].astype(F32)
            _spatial_gate(sv_ref, wtril, vls[b], bt_ref, b, tm, gd)
            sv = sv_ref[...]
            gated = (u * sv).astype(BF16)
            dgated = _dot_nt(dmb, wout_ref[b])
            dwout_ref[b] += _dot_tn(gated, dmb)
            dz_ref[b] = (dgated * sv * slope_ref[b].astype(F32)).astype(BF16)
            dsv = dgated * u
            folded = dsv[0:CHUNK, :]
            for c in range(1, n_chunks):
                folded = folded + dsv[c * CHUNK : (c + 1) * CHUNK, :]
            for gg in range(A_GROUPS // 2):
                g = b * (A_GROUPS // 2) + gg
                dbacc_ref[:, g : g + 1] += jnp.sum(folded[:, gg * gd : (gg + 1) * gd], axis=1, keepdims=True)
            dsvb = dsv.astype(BF16)
            for gg in range(A_GROUPS // 2):
                g = b * (A_GROUPS // 2) + gg
                for c in range(n_chunks):
                    rows = slice(c * CHUNK, (c + 1) * CHUNK)
                    cols = slice(gg * gd, (gg + 1) * gd)
                    blk = dsvb[rows, cols]
                    dvl_ref[rows, cols] = _dot_tn(wtril[g], blk)
                    dws_ref[g] += _dot_nt(blk, vls[b][rows, cols])
            dvl = dvl_ref[...]
            dlng_ref[b : b + 1, :] += jnp.sum(dvl * vhs[b], axis=0, keepdims=True)
            dlnb_ref[b : b + 1, :] += jnp.sum(dvl, axis=0, keepdims=True)
            dvhs.append(dvl * lng_ref[b : b + 1, :])
        width = 2.0 * q
        m1 = (jnp.sum(dvhs[0], axis=-1, keepdims=True) + jnp.sum(dvhs[1], axis=-1, keepdims=True)) / width
        m2 = (jnp.sum(dvhs[0] * vhs[0], axis=-1, keepdims=True)
              + jnp.sum(dvhs[1] * vhs[1], axis=-1, keepdims=True)) / width
        for b in range(2):
            dv = rstd * (dvhs[b] - m1 - vhs[b] * m2)
            dz_ref[2 + b] = (dv * slope_ref[2 + b].astype(F32)).astype(BF16)

        @pl.when(pl.program_id(0) == t // tm - 1)
        def _():
            row = lax.broadcasted_iota(jnp.int32, (CHUNK, CHUNK), 0)
            col = lax.broadcasted_iota(jnp.int32, (CHUNK, CHUNK), 1)
            for g in range(A_GROUPS):
                dws_ref[g] = jnp.where(row >= col, dws_ref[g], 0.0)

    return _call(
        name,
        body,
        grid=(t // tm,),
        in_specs=[_rows(tm, d), _rows(tm, d), _VM, _blocks(4, tm, q), _blocks(4, tm, q)] + [_VM] * 5,
        out_specs=[_blocks(4, tm, q)] + [_VM] * 6,
        out_shape=[
            jax.ShapeDtypeStruct((4, t, q), BF16),
            jax.ShapeDtypeStruct((2, q, d), F32),
            jax.ShapeDtypeStruct((A_GROUPS, CHUNK, CHUNK), F32),
            jax.ShapeDtypeStruct((CHUNK, A_GROUPS), F32),
            jax.ShapeDtypeStruct((2, q), F32),
            jax.ShapeDtypeStruct((2, q), F32),
            jax.ShapeDtypeStruct((1, d), F32),
        ],
        scratch_shapes=[pltpu.VMEM((tm, q), F32), pltpu.VMEM((tm, q), F32)],
        args=[dy, m, gpost, act, slope, ln_g, ln_b, w_s, b_t, w_out],
        exchanges=exchanges,
    )


def _window_counts(tm, win):
    pos = pl.program_id(0) * tm + lax.broadcasted_iota(jnp.int32, (tm, 1), 0)
    return jnp.minimum(pos + 1, win).astype(F32)


def _pooled(p, halo, tm, gd, inside=False):
    prev = halo if inside else jnp.where(pl.program_id(0) == 0, 0.0, halo)
    ext = jnp.concatenate([prev, p], axis=0)
    out = []
    for g, win in enumerate(B_WINDOWS):
        s = ext[:, g * gd : (g + 1) * gd]
        step = 1
        while step < win:
            s = s + pltpu.roll(s, step, 0)
            step *= 2
        total = s[HALO:, :]
        count = float(win) if inside else _window_counts(tm, win)
        out.append(total / count - p[:, g * gd : (g + 1) * gd])
    return out


def _halo_spec(t, tm, d, ahead):
    per = tm // HALO
    if ahead:
        return pl.BlockSpec((HALO, d), lambda i: (jnp.minimum((i + 1) * per, t // HALO - 1), 0))
    return pl.BlockSpec((HALO, d), lambda i: (jnp.maximum(i * per - 1, 0), 0))


def _mix_b_fwd(name, x, gpre, gpost, w_in, w_grp, scale, w_out, exchanges=()):
    t, d = x.shape
    gd = d // len(B_WINDOWS)
    tm = _step_rows(t)

    def body(x_ref, xh_ref, gpre_ref, gpost_ref, win_ref, wgrp_ref, scale_ref, wout_ref, o_ref, p_ref, m_ref):
        xv = x_ref[...]
        gain = gpre_ref[...]
        p = _dot(_rms(xv, gain).astype(BF16), win_ref[...])
        p_ref[...] = p
        halo = _dot(_rms(xh_ref[...], gain).astype(BF16), win_ref[...])
        pooled = _pooled(p, halo, tm, gd)
        mixed = jnp.concatenate([_dot(pg.astype(BF16), wgrp_ref[g]) for g, pg in enumerate(pooled)], axis=1)
        m = _dot((mixed * scale_ref[...]).astype(BF16), wout_ref[...])
        m_ref[...] = m
        o_ref[...] = xv + _rms(m, gpost_ref[...])

    return _call(
        name,
        body,
        grid=(t // tm,),
        in_specs=[_rows(tm, d), _halo_spec(t, tm, d, False), _VM, _VM, _VM, _VM, _VM, _VM],
        out_specs=[_rows(tm, d)] * 3,
        out_shape=[jax.ShapeDtypeStruct((t, d), F32)] * 3,
        args=[x, x, gpre, gpost, w_in, w_grp, scale, w_out],
        exchanges=exchanges,
    )


def _mix_b_bwd(name, dy, x, m, p, gpre, gpost, w_in, w_grp, scale, w_out, exchanges=()):
    t, d = dy.shape
    gd = d // len(B_WINDOWS)
    tm = _step_rows(t)
    n_steps = t // tm

    def body(dy_ref, x_ref, m_ref, p_ref, pprev_ref, dynext_ref, mnext_ref, pnext_ref,
             gpre_ref, gpost_ref, win_ref, wgrp_ref, scale_ref, wout_ref,
             dx_ref, dwout_ref, dwgrp_ref, dscale_ref, dpost_ref, dwin_ref, dpre_ref):
        @pl.when(pl.program_id(0) == 0)
        def _():
            for ref in (dwout_ref, dwgrp_ref, dscale_ref, dpost_ref, dwin_ref, dpre_ref):
                ref[...] = jnp.zeros_like(ref)

        scale = scale_ref[...]

        def tail_bwd(dy_rows, m_rows, pooled, counts, accumulate):
            dm, dgain = _rms_bwd(dy_rows, m_rows, gpost_ref[...])
            dmb = dm.astype(BF16)
            pooled = [pg.astype(BF16) for pg in pooled]
            mixed = jnp.concatenate([_dot(pg, wgrp_ref[g]) for g, pg in enumerate(pooled)], axis=1)
            dms = _dot_nt(dmb, wout_ref[...])
            dmixed = (dms * scale).astype(BF16)
            if accumulate:
                dpost_ref[...] += dgain
                dwout_ref[...] += _dot_tn((mixed * scale).astype(BF16), dmb)
                dscale_ref[...] += jnp.sum(dms * mixed, axis=0, keepdims=True)
            parts = []
            for g, win in enumerate(B_WINDOWS):
                dmg = dmixed[:, g * gd : (g + 1) * gd]
                if accumulate:
                    dwgrp_ref[g] += _dot_tn(pooled[g], dmg)
                parts.append(_dot_nt(dmg, wgrp_ref[g]) / counts(win))
            return jnp.concatenate(parts, axis=1)

        dyv = dy_ref[...]
        pv = p_ref[...]
        dq_blk = tail_bwd(dyv, m_ref[...], _pooled(pv, pprev_ref[...], tm, gd),
                          lambda win: _window_counts(tm, win), True)
        dq_next = tail_bwd(dynext_ref[...], mnext_ref[...],
                           _pooled(pnext_ref[...], pv[tm - HALO :, :], HALO, gd, inside=True),
                           float, False)
        dq_next = jnp.where(pl.program_id(0) == n_steps - 1, 0.0, dq_next)
        ext = jnp.concatenate([dq_blk, dq_next], axis=0)
        parts = []
        for g, win in enumerate(B_WINDOWS):
            cols = slice(g * gd, (g + 1) * gd)
            s = ext[:, cols]
            step = 1
            while step < win:
                s = s + pltpu.roll(s, tm + HALO - step, 0)
                step *= 2
            parts.append((s[:tm, :] - dq_blk[:, cols] * _window_counts(tm, win)).astype(BF16))
        dp = jnp.concatenate(parts, axis=1)
        xv = x_ref[...]
        gain = gpre_ref[...]
        hb = _rms(xv, gain).astype(BF16)
        dwin_ref[...] += _dot_tn(hb, dp)
        dx, dgain = _rms_bwd(_dot_nt(dp, win_ref[...]), xv, gain)
        dx_ref[...] = dyv + dx
        dpre_ref[...] += dgain

    before, after = _halo_spec(t, tm, d, False), _halo_spec(t, tm, d, True)
    return _call(
        name,
        body,
        grid=(n_steps,),
        in_specs=[_rows(tm, d)] * 4 + [before, after, after, after] + [_VM] * 6,
        out_specs=[_rows(tm, d)] + [_VM] * 6,
        out_shape=[
            jax.ShapeDtypeStruct((t, d), F32),
            jax.ShapeDtypeStruct((d, d), F32),
            jax.ShapeDtypeStruct((len(B_WINDOWS), gd, gd), F32),
            jax.ShapeDtypeStruct((1, d), F32),
            jax.ShapeDtypeStruct((1, d), F32),
            jax.ShapeDtypeStruct((d, d), F32),
            jax.ShapeDtypeStruct((1, d), F32),
        ],
        args=[dy, x, m, p, p, dy, m, p, gpre, gpost, w_in, w_grp, scale, w_out],
        exchanges=exchanges,
    )


def _cast_into_slots(name, place, w, dtype):
    n_layers, r, c = w.shape

    def body(place_ref, w_ref, *o_refs):
        del place_ref
        for j, o_ref in enumerate(o_refs):
            @pl.when(pl.program_id(0) == j)
            def _():
                o_ref[...] = w_ref[...].astype(dtype)

    return pl.pallas_call(
        body,
        name=name,
        grid_spec=pltpu.PrefetchScalarGridSpec(
            num_scalar_prefetch=1,
            grid=(n_layers,),
            in_specs=[pl.BlockSpec((1, r, c), lambda i, place_ref: (i, 0, 0))],
            out_specs=[pl.BlockSpec((1, r, c), lambda i, place_ref: (place_ref[0], 0, 0))] * n_layers,
        ),
        out_shape=[jax.ShapeDtypeStruct((N_CHIPS, r, c), dtype)] * n_layers,
        compiler_params=_params(),
    )(place, w)


def _row_tile(r):
    return 256 if r % 256 == 0 else r


def _pair_sum(name, place, dw, recv):
    _, r, c = dw.shape
    half = r // 2
    tr = _row_tile(half)
    per = half // tr

    def body(place_ref, a_ref, b_ref, o_ref):
        del place_ref
        o_ref[...] = (a_ref[...] + b_ref[...]).astype(BF16)

    return pl.pallas_call(
        body,
        name=name,
        grid_spec=pltpu.PrefetchScalarGridSpec(
            num_scalar_prefetch=1,
            grid=(N_CHIPS, per),
            in_specs=[
                pl.BlockSpec((1, tr, c), lambda k, i, place_ref: (k, place_ref[1] * per + i, 0)),
                pl.BlockSpec((1, tr, c), lambda k, i, place_ref: (k, i, 0)),
            ],
            out_specs=pl.BlockSpec((1, tr, c), lambda k, i, place_ref: (k, i, 0)),
        ),
        out_shape=jax.ShapeDtypeStruct(recv.shape, BF16),
        compiler_params=pltpu.CompilerParams(
            dimension_semantics=("arbitrary",) * 2, vmem_limit_bytes=VMEM_LIMIT_BYTES),
    )(place, dw, recv)


def _chip_sum(name, place, mine, others, exchanges=()):
    n_layers = len(mine)
    _, half, c = mine[0].shape
    tr = _row_tile(half)
    per = half // tr

    def body(place_ref, *refs):
        del place_ref
        o_ref = refs[-1]
        for j in range(n_layers):
            @pl.when(pl.program_id(0) == j)
            def _():
                parts = refs[4 * j : 4 * j + 4]
                acc = parts[0][...].astype(F32) + parts[1][...].astype(F32)
                acc = acc + parts[2][...].astype(F32)
                o_ref[...] = acc + parts[3][...].astype(F32)

    def part(j, flip):
        return pl.BlockSpec((1, tr, c), lambda l, i, place_ref: (
            jnp.bitwise_xor(place_ref[0], flip), jnp.where(l == j, i, 0), 0))

    args = []
    for j in range(n_layers):
        args += [mine[j], others[j], others[j], others[j]]
    (total,), ex_outs = _call(
        name,
        body,
        grid=(n_layers, per),
        in_specs=[part(j, flip) for j in range(n_layers) for flip in range(N_CHIPS)],
        out_specs=[pl.BlockSpec((1, tr, c), lambda l, i, place_ref: (l, place_ref[1] * per + i, 0))],
        out_shape=[jax.ShapeDtypeStruct((n_layers, 2 * half, c), F32)],
        args=args,
        prefetch=[place],
        exchanges=exchanges,
    )
    return total, ex_outs


def _adamw(name, w, g, m, v, exchanges=()):
    n_layers, r, c = w.shape
    tr = _row_tile(r)

    def body(w_ref, g_ref, m_ref, v_ref, go_ref, d_ref, nm_ref, nv_ref):
        gv = g_ref[...]
        go_ref[...] = gv
        nm = ADAM_B1 * m_ref[...] + (1.0 - ADAM_B1) * gv
        nv = ADAM_B2 * v_ref[...] + (1.0 - ADAM_B2) * jnp.square(gv)
        m_hat = nm / (1.0 - ADAM_B1 ** ADAM_STEP)
        v_hat = nv / (1.0 - ADAM_B2 ** ADAM_STEP)
        d_ref[...] = -ADAM_LR * (m_hat / (jnp.sqrt(v_hat) + ADAM_EPS) + ADAM_WD * w_ref[...])
        nm_ref[...] = nm
        nv_ref[...] = nv

    spec = pl.BlockSpec((1, tr, c), lambda l, i: (l, i, 0))
    return _call(
        name,
        body,
        grid=(n_layers, r // tr),
        in_specs=[spec] * 4,
        out_specs=[spec] * 4,
        out_shape=[jax.ShapeDtypeStruct(w.shape, F32)] * 4,
        args=[w, g, m, v],
        exchanges=exchanges,
    )


class _GatherSmall:
    def __init__(self, gathered):
        self.inputs = [gathered]
        self.out_shapes = [jax.ShapeDtypeStruct(gathered.shape, gathered.dtype)]
        self.aliases = {0: 0}
        self.n_sems = 4

    def _copies(self, outs, send, recv, outbound):
        x, y, c = _position()
        peers = [(x, y, 1 - c)] + [(px, py, c) for px, py in _other_chips(x, y)]
        copies = []
        for k, (px, py, pc) in enumerate(peers):
            rows = outs[0].at[4 * x + 2 * y + c if outbound else 4 * px + 2 * py + pc]
            copies.append(pltpu.make_async_remote_copy(
                src_ref=rows, dst_ref=rows, send_sem=send.at[k], recv_sem=recv.at[k],
                device_id=(px, py, pc), device_id_type=MESH))
        return copies

    def start(self, ins, outs, send, recv):
        for cp in self._copies(outs, send, recv, True):
            cp.start()

    def finish(self, ins, outs, send, recv):
        for cp in self._copies(outs, send, recv, False):
            cp.wait_recv()
        for cp in self._copies(outs, send, recv, True):
            cp.wait_send()


class _GatherSmallForward:
    def __init__(self, gathered):
        self.inputs = [gathered]
        self.out_shapes = [jax.ShapeDtypeStruct(gathered.shape, gathered.dtype)]
        self.aliases = {0: 0}
        self.n_sems = 3

    def _copies(self, outs, send, recv, core):
        x, y, c = _position()
        copies = []
        for k, (px, py) in enumerate(_other_chips(x, y)):
            rows = outs[0].at[4 * px + 2 * py + core]
            copies.append(pltpu.make_async_remote_copy(
                src_ref=rows, dst_ref=rows, send_sem=send.at[k], recv_sem=recv.at[k],
                device_id=(x, y, 1 - c), device_id_type=MESH))
        return copies

    def start(self, ins, outs, send, recv):
        for cp in self._copies(outs, send, recv, lax.axis_index("c")):
            cp.start()

    def finish(self, ins, outs, send, recv):
        c = lax.axis_index("c")
        for cp in self._copies(outs, send, recv, 1 - c):
            cp.wait_recv()
        for cp in self._copies(outs, send, recv, c):
            cp.wait_send()


def _sum_devices(name, gathered):
    _, m_per, n = gathered.shape

    def body(all_ref, sum_ref):
        acc = all_ref[0]
        for k in range(1, N_DEV):
            acc = acc + all_ref[k]
        sum_ref[...] = acc

    return pl.pallas_call(
        body,
        name=name,
        in_specs=[_VM],
        out_specs=_VM,
        out_shape=jax.ShapeDtypeStruct((m_per, n), F32),
        compiler_params=pltpu.CompilerParams(vmem_limit_bytes=VMEM_LIMIT_BYTES),
    )(gathered)


SHARDED = ("a_w_in", "a_w_out", "b_w_in", "b_w_grp", "b_scale", "b_w_out", "ffn_w_gate", "ffn_w_up", "ffn_w_down")
SMALL = ("a_ln_g", "a_ln_b", "a_w_s", "a_b_s", "mix_pre_g", "mix_post_g", "ffn_pre_g", "ffn_post_g")
WEIGHTS = ("a_w_in", "a_ln_g", "a_ln_b", "a_w_s", "a_b_s", "a_w_out", "b_w_in", "b_w_grp", "b_scale", "b_w_out",
           "mix_pre_g", "mix_post_g", "ffn_pre_g", "ffn_post_g", "ffn_w_gate", "ffn_w_up", "ffn_w_down")


TRANSPOSED = ("ffn_w_gate", "ffn_w_up")


def _as_layers(name, a):
    if name in TRANSPOSED:
        return jnp.swapaxes(a, 1, 2)
    if a.ndim == 2:
        return a.reshape(a.shape[0], 1, a.shape[1])
    return a.reshape(a.shape[0], -1, a.shape[-1])


def _from_layers(name, a, shape):
    if name in TRANSPOSED:
        return jnp.swapaxes(a, 1, 2)
    return a.reshape(shape)


def _pack_small(parts):
    return jnp.concatenate([p.reshape(-1, 128) for p in parts], axis=0)


def _unpack_small(packed, like):
    out, row = [], 0
    for ref in like:
        rows = ref.size // 128
        out.append(packed[row : row + rows].reshape(ref.shape))
        row += rows
    return out


def kernel(x, a_w_in, a_ln_g, a_ln_b, a_w_s, a_b_s, a_w_out, b_w_in, b_w_grp, b_scale, b_w_out, mix_pre_g, mix_post_g, ffn_pre_g, ffn_post_g, ffn_w_gate, ffn_w_up, ffn_w_down, loss_target, m_a_w_in, m_a_ln_g, m_a_ln_b, m_a_w_s, m_a_b_s, m_a_w_out, m_b_w_in, m_b_w_grp, m_b_scale, m_b_w_out, m_mix_pre_g, m_mix_post_g, m_ffn_pre_g, m_ffn_post_g, m_ffn_w_gate, m_ffn_w_up, m_ffn_w_down, v_a_w_in, v_a_ln_g, v_a_ln_b, v_a_w_s, v_a_b_s, v_a_w_out, v_b_w_in, v_b_w_grp, v_b_scale, v_b_w_out, v_mix_pre_g, v_mix_post_g, v_ffn_pre_g, v_ffn_post_g, v_ffn_w_gate, v_ffn_w_up, v_ffn_w_down):
    weights = dict(a_w_in=a_w_in, a_ln_g=a_ln_g, a_ln_b=a_ln_b, a_w_s=a_w_s, a_b_s=a_b_s, a_w_out=a_w_out,
                   b_w_in=b_w_in, b_w_grp=b_w_grp, b_scale=b_scale, b_w_out=b_w_out, mix_pre_g=mix_pre_g,
                   mix_post_g=mix_post_g, ffn_pre_g=ffn_pre_g, ffn_post_g=ffn_post_g, ffn_w_gate=ffn_w_gate,
                   ffn_w_up=ffn_w_up, ffn_w_down=ffn_w_down)
    mom1 = dict(a_w_in=m_a_w_in, a_ln_g=m_a_ln_g, a_ln_b=m_a_ln_b, a_w_s=m_a_w_s, a_b_s=m_a_b_s, a_w_out=m_a_w_out,
                b_w_in=m_b_w_in, b_w_grp=m_b_w_grp, b_scale=m_b_scale, b_w_out=m_b_w_out, mix_pre_g=m_mix_pre_g,
                mix_post_g=m_mix_post_g, ffn_pre_g=m_ffn_pre_g, ffn_post_g=m_ffn_post_g, ffn_w_gate=m_ffn_w_gate,
                ffn_w_up=m_ffn_w_up, ffn_w_down=m_ffn_w_down)
    mom2 = dict(a_w_in=v_a_w_in, a_ln_g=v_a_ln_g, a_ln_b=v_a_ln_b, a_w_s=v_a_w_s, a_b_s=v_a_b_s, a_w_out=v_a_w_out,
                b_w_in=v_b_w_in, b_w_grp=v_b_w_grp, b_scale=v_b_scale, b_w_out=v_b_w_out, mix_pre_g=v_mix_pre_g,
                mix_post_g=v_mix_post_g, ffn_pre_g=v_ffn_pre_g, ffn_post_g=v_ffn_post_g, ffn_w_gate=v_ffn_w_gate,
                ffn_w_up=v_ffn_w_up, ffn_w_down=v_ffn_w_down)

    t, d = x.shape[1], x.shape[2]
    depth = mix_pre_g.shape[0]
    gd_b = d // len(B_WINDOWS)
    xs = x.reshape(t, d)
    target = loss_target.reshape(t, d)

    chip = 2 * lax.axis_index("x") + lax.axis_index("y")
    place = jnp.stack([chip, lax.axis_index("c")]).astype(jnp.int32)
    bufs = {name: list(_cast_into_slots("cast_" + name, place, _as_layers(name, weights[name]),
                                        F32 if name == "b_scale" else BF16)) for name in SHARDED}

    def gain(name, i):
        return weights[name][i].reshape(1, d)

    def weight_keys(i):
        j = i // 2
        mixer = [("a_w_in", j), ("a_w_out", j)] if i % 2 == 0 else [("b_w_in", j), ("b_w_grp", j), ("b_w_out", j)]
        return mixer, [("ffn_w_gate", i), ("ffn_w_up", i), ("ffn_w_down", i)]

    def gather(keys):
        return _GatherWeights([bufs[n][j] for n, j in keys],
                              whole=[k for k, (n, _) in enumerate(keys) if n == "b_scale"])

    def gathered(keys, outs):
        for (n, j), buf in zip(keys, outs):
            bufs[n][j] = buf

    first = weight_keys(0)[0] + [("b_scale", j) for j in range(b_scale.shape[0])]
    gathered(first, _exchange("gather_first", [gather(first)])[0])
    saved = []
    cur = xs
    for i in range(depth):
        j = i // 2
        mixer_next, ffn_next = weight_keys(i + 1) if i + 1 < depth else ([], [])
        ffn_keys = weight_keys(i)[1]
        after_ffn = mixer_next if (i + 1) % 2 == 0 else mixer_next + ffn_next
        if i % 2 == 0:
            w_in = bufs["a_w_in"][j]
            q = w_in.shape[2]
            w_out = bufs["a_w_out"][j].reshape(2, q, d)
            ln_g = a_ln_g[j].reshape(2, q)
            ln_b = a_ln_b[j].reshape(2, q)
            b_t = jnp.transpose(a_b_s[j])
            (nxt, act, slope, m), (got,) = _mix_a_fwd(
                f"mix_a_fwd{j}", cur, gain("mix_pre_g", i), gain("mix_post_g", i), w_in, ln_g, ln_b, a_w_s[j], b_t,
                w_out, exchanges=[gather(ffn_keys)])
            gathered(ffn_keys, got)
            mix_saved = dict(x=cur, act=act, slope=slope, m=m, w_in=w_in, w_out=w_out, ln_g=ln_g, ln_b=ln_b, b_t=b_t)
        else:
            w_in = bufs["b_w_in"][j].reshape(d, d)
            w_out = bufs["b_w_out"][j].reshape(d, d)
            w_grp = jnp.transpose(bufs["b_w_grp"][j].reshape(N_CHIPS, len(B_WINDOWS), gd_b // N_CHIPS, gd_b),
                                  (1, 0, 2, 3)).reshape(len(B_WINDOWS), gd_b, gd_b)
            scale = bufs["b_scale"][j].reshape(1, d)
            (nxt, p, m), _ = _mix_b_fwd(f"mix_b_fwd{j}", cur, gain("mix_pre_g", i), gain("mix_post_g", i),
                                        w_in, w_grp, scale, w_out)
            mix_saved = dict(x=cur, p=p, m=m, w_in=w_in, w_out=w_out, w_grp=w_grp, scale=scale)
        cur = nxt
        wg, wu, wd = (bufs[n][k].reshape(1, -1, d) for n, k in ffn_keys)
        (nxt, a, dup, dgate, f, *sq), (got,) = _ffn_fwd(
            f"ffn_fwd{i}", cur, gain("ffn_pre_g", i), gain("ffn_post_g", i), wg, wu, wd,
            target=target if i == depth - 1 else None, exchanges=[gather(after_ffn)])
        gathered(after_ffn, got)
        saved.append((mix_saved, dict(x=cur, a=a, dup=dup, dgate=dgate, f=f, wg=wg, wu=wu, wd=wd)))
        cur = nxt

    dcur = cur
    loss = lax.psum(0.5 * sq[0][0, 0] / d, ("x", "y", "c"))

    grads = {name: [None] * weights[name].shape[0] for name in WEIGHTS}
    state = dict(to_sibling=[], to_chips=[])
    pair, from_chips = {}, {}

    def exchanges_due(carry=True):
        if not carry:
            return []
        return [_ToSibling([a for _, _, a in state["to_sibling"]]), _ToChips([a for _, _, a in state["to_chips"]])]

    def exchanged(outs):
        if not outs:
            return
        from_sibling, arrived = outs
        for (n, k, _), got in zip(state["to_chips"], arrived):
            from_chips[n, k] = got
        state["to_chips"] = []
        for (n, k, dw), got in zip(state["to_sibling"], from_sibling):
            pair[n, k] = _pair_sum(f"pair_sum_{n}{k}", place, dw, got)
            state["to_chips"].append((n, k, pair[n, k]))
        state["to_sibling"] = []

    def made(name, k, dw):
        grads[name][k] = dw
        state["to_sibling"].append((name, k, dw))

    def own_slot(packed):
        return lax.dynamic_update_slice(jnp.zeros((N_DEV,) + packed.shape, F32), packed[None],
                                        (2 * chip + place[1], 0, 0))

    for i in reversed(range(depth)):
        j = i // 2
        mix_saved, ffn_saved = saved[i]
        s = ffn_saved
        both = i == 0
        (dg, du, dwd, dgain), outs = _ffn_bwd_hidden(
            f"ffn_bwd_hidden{i}", dcur, s["f"], gain("ffn_post_g", i), s["a"], s["dup"], s["dgate"], s["wd"],
            exchanges=exchanges_due(both))
        exchanged(outs)
        grads["ffn_post_g"][i] = dgain
        made("ffn_w_down", i, dwd.reshape(N_CHIPS, -1, d))
        (dcur, dwg, dwu, dgain), outs = _bwd_in(
            f"ffn_bwd_in{i}", dcur, s["x"], gain("ffn_pre_g", i), [dg, du], [s["wg"], s["wu"]], transposed=True,
            exchanges=exchanges_due())
        exchanged(outs)
        grads["ffn_pre_g"][i] = dgain
        made("ffn_w_gate", i, dwg.reshape(N_CHIPS, -1, d))
        made("ffn_w_up", i, dwu.reshape(N_CHIPS, -1, d))
        s = mix_saved
        if i % 2 == 0:
            (dz, dwout, dws, dbacc, dlng, dlnb, dgain), outs = _mix_a_bwd_hidden(
                f"mix_a_bwd_hidden{j}", dcur, s["m"], gain("mix_post_g", i), s["act"], s["slope"], s["ln_g"], s["ln_b"],
                a_w_s[j], s["b_t"], s["w_out"], exchanges=exchanges_due(both))
            exchanged(outs)
            grads["a_w_s"][j] = dws
            grads["a_b_s"][j] = jnp.transpose(dbacc)
            grads["a_ln_g"][j] = dlng.reshape(-1)
            grads["a_ln_b"][j] = dlnb.reshape(-1)
            grads["mix_post_g"][i] = dgain
            made("a_w_out", j, dwout.reshape(N_CHIPS, -1, d))
            riders = []
            if i == 0:
                grads["mix_pre_g"][0] = jnp.zeros((1, d), F32)
                small_grads = [jnp.stack([g.reshape(weights[name].shape[1:]) for g in grads[name]], axis=0)
                               for name in SMALL]
                scale_grad = jnp.concatenate(grads["b_scale"], axis=0)
                riders = [_GatherSmall(own_slot(_pack_small(small_grads + [scale_grad])))]
            (dcur, dwin, dgain), outs = _bwd_in(
                f"mix_a_bwd_in{j}", dcur, s["x"], gain("mix_pre_g", i), [dz], [s["w_in"]],
                exchanges=exchanges_due() + riders)
            exchanged(outs[:2])
            gathered_small = outs[2][0] if riders else None
            grads["mix_pre_g"][i] = dgain
            made("a_w_in", j, dwin)
        else:
            (dcur, dwout, dwgrp, dscale, dpost, dwin, dpre), outs = _mix_b_bwd(
                f"mix_b_bwd{j}", dcur, s["x"], s["m"], s["p"], gain("mix_pre_g", i), gain("mix_post_g", i),
                s["w_in"], s["w_grp"], s["scale"], s["w_out"], exchanges=exchanges_due())
            exchanged(outs)
            grads["b_scale"][j] = dscale
            grads["mix_post_g"][i] = dpost
            grads["mix_pre_g"][i] = dpre
            made("b_w_out", j, dwout.reshape(N_CHIPS, -1, d))
            made("b_w_grp", j, jnp.transpose(
                dwgrp.reshape(len(B_WINDOWS), N_CHIPS, gd_b // N_CHIPS, gd_b), (1, 0, 2, 3)).reshape(N_CHIPS, -1, gd_b))
            made("b_w_in", j, dwin.reshape(N_CHIPS, -1, d))
    grad_x = dcur.reshape(x.shape)

    outs = _exchange("grads_last", exchanges_due() + [_GatherSmallForward(gathered_small),
                                                      _GatherSmall(own_slot(_pack_small([grads["mix_pre_g"][0]])))])
    exchanged(outs[:2])
    gathered_small = outs[2][0]
    outs = _exchange("grads_last_to_chips", exchanges_due() + [_GatherSmallForward(outs[3][0])])
    exchanged(outs[:2])
    gathered_last = outs[2][0]

    reduced_names = [name for name in SHARDED if name != "b_scale"]
    sums = []
    for name in reduced_names:
        layers = range(weights[name].shape[0])
        sums.append(_chip_sum("chip_sum_" + name, place, [pair[name, k] for k in layers],
                              [from_chips[name, k] for k in layers])[0])
    reduced = dict(zip(reduced_names, _exchange("swap_halves", [_SwapHalves(sums)])[0]))

    out_g, out_d, out_m, out_v = {}, {}, {}, {}
    for name in reduced_names:
        shape = weights[name].shape
        results, _ = _adamw("adamw_" + name, _as_layers(name, weights[name]), reduced[name],
                            _as_layers(name, mom1[name]), _as_layers(name, mom2[name]))
        out_g[name], out_d[name], out_m[name], out_v[name] = (_from_layers(name, a, shape) for a in results)

    summed = _sum_devices("sum_devices", gathered_small)
    last_rows = sum(weights[name].size // 128 for name in SMALL[: SMALL.index("mix_pre_g")])
    summed = lax.dynamic_update_slice(summed, _sum_devices("sum_devices_last", gathered_last), (last_rows, 0))
    small_rows = summed.shape[0] - scale_grad.size // 128
    scale_sum = summed[small_rows:].reshape(scale_grad.shape)
    scale_mine = lax.dynamic_slice_in_dim(scale_sum, chip * b_scale.shape[1], b_scale.shape[1], axis=1)[:, None, :]
    summed = summed[:small_rows]
    results, _ = _adamw("adamw_b_scale", _as_layers("b_scale", b_scale), scale_mine,
                        _as_layers("b_scale", m_b_scale), _as_layers("b_scale", v_b_scale))
    out_g["b_scale"], out_d["b_scale"], out_m["b_scale"], out_v["b_scale"] = (a.reshape(b_scale.shape) for a in results)
    small_like = [weights[name] for name in SMALL]
    packs = [_pack_small([src[name] for name in SMALL]).reshape(1, -1, 128) for src in (weights, mom1, mom2)]
    results, _ = _adamw("adamw_small", packs[0], summed.reshape(1, -1, 128), packs[1], packs[2])
    for dst, packed in zip((out_g, out_d, out_m, out_v), (a[0] for a in results)):
        for name, val in zip(SMALL, _unpack_small(packed, small_like)):
            dst[name] = val

    return (loss, grad_x, *[out_g[n] for n in WEIGHTS], *[out_d[n] for n in WEIGHTS],
            *[out_m[n] for n in WEIGHTS], *[out_v[n] for n in WEIGHTS])
```

```python
import functools
import math

import jax
import jax.numpy as jnp
from jax import lax
from jax.experimental import pallas as pl
from jax.experimental.pallas import tpu as pltpu

F32 = jnp.float32
BF16 = jnp.bfloat16
MESH = pl.DeviceIdType.MESH

EPS = 1e-6
CHUNK = 128
A_GROUPS = 8
B_WINDOWS = (2, 4, 8, 16)
HALO = 16
N_CHIPS = 4
N_DEV = 8

ADAM_LR = 0.001
ADAM_B1 = 0.9
ADAM_B2 = 0.999
ADAM_EPS = 1e-08
ADAM_WD = 0.01
ADAM_STEP = 10

VMEM_LIMIT_BYTES = 60 * 1024 * 1024
INV_SQRT2 = 1.0 / math.sqrt(2.0)
INV_SQRT_2PI = 1.0 / math.sqrt(2.0 * math.pi)

_ANY = pl.BlockSpec(memory_space=pl.ANY)
_VM = pl.BlockSpec(memory_space=pltpu.VMEM)


def _params():
    return pltpu.CompilerParams(dimension_semantics=("arbitrary",), vmem_limit_bytes=VMEM_LIMIT_BYTES)


def _token_block(t):
    return 256 if t >= 1024 else 128


def _step_rows(t):
    return 2 * _token_block(t)


def _rows(tm, d):
    return pl.BlockSpec((tm, d), lambda i: (i, 0))


def _blocks(nb, tm, bw):
    return pl.BlockSpec((nb, tm, bw), lambda i: (0, i, 0))


def _dot(a, b):
    return lax.dot_general(a, b, (((1,), (0,)), ((), ())), preferred_element_type=F32)


def _dot_nt(a, b):
    return lax.dot_general(a, b, (((1,), (1,)), ((), ())), preferred_element_type=F32)


def _dot_tn(a, b):
    return lax.dot_general(a, b, (((0,), (0,)), ((), ())), preferred_element_type=F32)


def _rms(x, g):
    return x * lax.rsqrt(jnp.mean(x * x, axis=-1, keepdims=True) + EPS) * g


def _rms_bwd(dy, x, g):
    r = lax.rsqrt(jnp.mean(x * x, axis=-1, keepdims=True) + EPS)
    n = x * r
    dn = dy * g
    dx = r * (dn - n * jnp.mean(dn * n, axis=-1, keepdims=True))
    return dx, jnp.sum(dy * n, axis=0, keepdims=True)


def _gelu_and_grad(x):
    cdf = 0.5 * (1.0 + lax.erf(x * INV_SQRT2))
    return x * cdf, cdf + x * (jnp.exp(-0.5 * x * x) * INV_SQRT_2PI)


def _position():
    return lax.axis_index("x"), lax.axis_index("y"), lax.axis_index("c")


def _other_chips(x, y):
    return [(1 - x, y), (x, 1 - y), (1 - x, 1 - y)]


class _GatherWeights:
    def __init__(self, bufs, whole=()):
        self.inputs = list(bufs)
        self.out_shapes = [jax.ShapeDtypeStruct(b.shape, b.dtype) for b in bufs]
        self.aliases = {w: w for w in range(len(bufs))}
        self.n_sems = 6 * len(bufs)
        self.whole = frozenset(whole)

    def _part(self, outs, w, slot, core):
        if w in self.whole:
            return outs[w].at[slot]
        half = outs[w].shape[1] // 2
        return outs[w].at[slot, pl.ds(core * half, half)]

    def _ici(self, outs, send, recv, w, j, slot):
        x, y, c = _position()
        px, py = _other_chips(x, y)[j]
        part = self._part(outs, w, slot, c)
        return pltpu.make_async_remote_copy(
            src_ref=part, dst_ref=part, send_sem=send.at[6 * w + j], recv_sem=recv.at[6 * w + j],
            device_id=(px, py, c), device_id_type=MESH)

    def _d2d(self, outs, send, recv, w, j, slot, core):
        x, y, c = _position()
        part = self._part(outs, w, slot, core)
        return pltpu.make_async_remote_copy(
            src_ref=part, dst_ref=part, send_sem=send.at[6 * w + 3 + j], recv_sem=recv.at[6 * w + 3 + j],
            device_id=(x, y, 1 - c), device_id_type=MESH)

    def start(self, ins, outs, send, recv):
        x, y, _ = _position()
        for w in range(len(outs)):
            for j in range(3):
                self._ici(outs, send, recv, w, j, 2 * x + y).start()

    def advance(self, ins, outs, send, recv):
        x, y, c = _position()
        slots = [2 * px + py for px, py in _other_chips(x, y)]
        for w in range(len(outs)):
            for j, slot in enumerate(slots):
                self._ici(outs, send, recv, w, j, slot).wait_recv()
                if w not in self.whole:
                    self._d2d(outs, send, recv, w, j, slot, c).start()

    def finish(self, ins, outs, send, recv):
        x, y, c = _position()
        slots = [2 * px + py for px, py in _other_chips(x, y)]
        for w in range(len(outs)):
            for j, slot in enumerate(slots):
                if w not in self.whole:
                    self._d2d(outs, send, recv, w, j, slot, 1 - c).wait_recv()
        for w in range(len(outs)):
            for j, slot in enumerate(slots):
                self._ici(outs, send, recv, w, j, 2 * x + y).wait_send()
                if w not in self.whole:
                    self._d2d(outs, send, recv, w, j, slot, c).wait_send()


class _ToSibling:
    def __init__(self, grads):
        self.inputs = list(grads)
        self.out_shapes = [jax.ShapeDtypeStruct((g.shape[0], g.shape[1] // 2, g.shape[2]), g.dtype) for g in grads]
        self.aliases = {}
        self.n_sems = len(grads)

    def _copy(self, ins, outs, send, recv, w):
        x, y, c = _position()
        half = ins[w].shape[1] // 2
        return pltpu.make_async_remote_copy(
            src_ref=ins[w].at[:, pl.ds((1 - c) * half, half)], dst_ref=outs[w],
            send_sem=send.at[w], recv_sem=recv.at[w], device_id=(x, y, 1 - c), device_id_type=MESH)

    def start(self, ins, outs, send, recv):
        for w in range(len(ins)):
            self._copy(ins, outs, send, recv, w).start()

    def finish(self, ins, outs, send, recv):
        for w in range(len(ins)):
            self._copy(ins, outs, send, recv, w).wait_recv()
        for w in range(len(ins)):
            self._copy(ins, outs, send, recv, w).wait_send()


class _ToChips:
    def __init__(self, parts):
        self.inputs = list(parts)
        self.out_shapes = [jax.ShapeDtypeStruct(p.shape, p.dtype) for p in parts]
        self.aliases = {}
        self.n_sems = 3 * len(parts)

    def _copy(self, ins, outs, send, recv, w, j, outbound):
        x, y, c = _position()
        px, py = _other_chips(x, y)[j]
        me, peer = 2 * x + y, 2 * px + py
        src_slot, dst_slot = (peer, me) if outbound else (me, peer)
        return pltpu.make_async_remote_copy(
            src_ref=ins[w].at[src_slot], dst_ref=outs[w].at[dst_slot],
            send_sem=send.at[3 * w + j], recv_sem=recv.at[3 * w + j], device_id=(px, py, c), device_id_type=MESH)

    def start(self, ins, outs, send, recv):
        for w in range(len(ins)):
            for j in range(3):
                self._copy(ins, outs, send, recv, w, j, True).start()

    def finish(self, ins, outs, send, recv):
        for w in range(len(ins)):
            for j in range(3):
                self._copy(ins, outs, send, recv, w, j, False).wait_recv()
        for w in range(len(ins)):
            for j in range(3):
                self._copy(ins, outs, send, recv, w, j, True).wait_send()


class _SwapHalves:
    def __init__(self, bufs):
        self.inputs = list(bufs)
        self.out_shapes = [jax.ShapeDtypeStruct(b.shape, b.dtype) for b in bufs]
        self.aliases = {w: w for w in range(len(bufs))}
        self.n_sems = len(bufs)

    def _copy(self, outs, send, recv, w, core):
        x, y, c = _position()
        half = outs[w].shape[1] // 2
        rows = outs[w].at[:, pl.ds(core * half, half)]
        return pltpu.make_async_remote_copy(
            src_ref=rows, dst_ref=rows, send_sem=send.at[w], recv_sem=recv.at[w],
            device_id=(x, y, 1 - c), device_id_type=MESH)

    def start(self, ins, outs, send, recv):
        c = lax.axis_index("c")
        for w in range(len(outs)):
            self._copy(outs, send, recv, w, c).start()

    def finish(self, ins, outs, send, recv):
        c = lax.axis_index("c")
        for w in range(len(outs)):
            self._copy(outs, send, recv, w, 1 - c).wait_recv()
        for w in range(len(outs)):
            self._copy(outs, send, recv, w, c).wait_send()


def _call(name, body, *, grid, in_specs, out_specs, out_shape, args, scratch_shapes=(), prefetch=(), exchanges=()):
    given = list(exchanges)
    exchanges = [e for e in given if e.inputs]
    n_pre, n_in, n_out, n_scr = len(prefetch), len(args), len(out_shape), len(scratch_shapes)
    ex_in = [a for e in exchanges for a in e.inputs]
    ex_out = [s for e in exchanges for s in e.out_shapes]
    aliases = {}
    at_in, at_out = n_pre + n_in, n_out
    for e in exchanges:
        for i, o in e.aliases.items():
            aliases[at_in + i] = at_out + o
        at_in += len(e.inputs)
        at_out += len(e.out_shapes)

    def at_step(last):
        hit = None
        for axis, n in enumerate(grid):
            here = pl.program_id(axis) == (n - 1 if last else 0)
            hit = here if hit is None else jnp.logical_and(hit, here)
        return hit

    def fused(*refs):
        pre, refs = refs[:n_pre], refs[n_pre:]
        body_in, refs = refs[:n_in], refs[n_in:]
        ex_in_refs, refs = refs[: len(ex_in)], refs[len(ex_in) :]
        body_out, refs = refs[:n_out], refs[n_out:]
        ex_out_refs, refs = refs[: len(ex_out)], refs[len(ex_out) :]
        body_scr, sems = refs[:n_scr], refs[n_scr:]

        def each(stage):
            a = b = 0
            for n, e in enumerate(exchanges):
                ins, outs = ex_in_refs[a : a + len(e.inputs)], ex_out_refs[b : b + len(e.out_shapes)]
                if hasattr(e, stage):
                    getattr(e, stage)(ins, outs, sems[2 * n], sems[2 * n + 1])
                a += len(e.inputs)
                b += len(e.out_shapes)

        if exchanges:
            @pl.when(at_step(False))
            def _():
                each("start")

        if body is not None:
            body(*pre, *body_in, *body_out, *body_scr)

        if exchanges:
            late = pl.program_id(0) == (3 * grid[0]) // 4 if len(grid) == 1 else at_step(True)

            @pl.when(late)
            def _():
                each("advance")

            @pl.when(at_step(True))
            def _():
                each("finish")

    outs = pl.pallas_call(
        fused,
        name=name,
        grid_spec=pltpu.PrefetchScalarGridSpec(
            num_scalar_prefetch=n_pre,
            grid=grid,
            in_specs=list(in_specs) + [_ANY] * len(ex_in),
            out_specs=list(out_specs) + [_ANY] * len(ex_out),
            scratch_shapes=list(scratch_shapes)
            + [pltpu.SemaphoreType.DMA((e.n_sems,)) for e in exchanges for _ in range(2)],
        ),
        out_shape=list(out_shape) + ex_out,
        input_output_aliases=aliases,
        compiler_params=pltpu.CompilerParams(
            dimension_semantics=("arbitrary",) * len(grid), vmem_limit_bytes=VMEM_LIMIT_BYTES),
    )(*prefetch, *args, *ex_in)
    body_outs, rest = list(outs[:n_out]), list(outs[n_out:])
    ex_outs = []
    for e in given:
        n_e = len(e.out_shapes) if e.inputs else 0
        ex_outs.append(rest[:n_e])
        rest = rest[n_e:]
    return body_outs, ex_outs


def _exchange(name, exchanges):
    return _call(name, None, grid=(1,), in_specs=[], out_specs=[], out_shape=[], args=[], exchanges=exchanges)[1]


def _ffn_fwd(name, x, gpre, gpost, wg, wu, wd, target=None, exchanges=()):
    t, d = x.shape
    nb, fs, _ = wg.shape
    sub = _token_block(t)
    tm = _step_rows(t)
    with_loss = target is not None

    def body(x_ref, gpre_ref, gpost_ref, wg_ref, wu_ref, wd_ref, *refs):
        if with_loss:
            t_ref, o_ref, a_ref, dup_ref, dgate_ref, f_ref, sq_ref = refs

            @pl.when(pl.program_id(0) == 0)
            def _():
                sq_ref[...] = jnp.zeros_like(sq_ref)
        else:
            o_ref, a_ref, dup_ref, dgate_ref, f_ref = refs
        for h in range(tm // sub):
            rows = slice(h * sub, (h + 1) * sub)
            xv = x_ref[rows, :]
            hb = _rms(xv, gpre_ref[...]).astype(BF16)
            f = jnp.zeros((sub, d), F32)
            for k in range(nb):
                g = _dot_nt(hb, wg_ref[k])
                u = _dot_nt(hb, wu_ref[k])
                s = jax.nn.sigmoid(g)
                sg = g * s
                a = (sg * u).astype(BF16)
                a_ref[k, rows, :] = a
                dup_ref[k, rows, :] = sg.astype(BF16)
                dgate_ref[k, rows, :] = (u * (s * (1.0 + g * (1.0 - s)))).astype(BF16)
                f = f + _dot(a, wd_ref[k])
            f_ref[rows, :] = f
            y = xv + _rms(f, gpost_ref[...])
            if with_loss:
                err = y - t_ref[rows, :]
                o_ref[rows, :] = err / d
                sq_ref[...] += jnp.sum(err * err)
            else:
                o_ref[rows, :] = y

    return _call(
        name,
        body,
        grid=(t // tm,),
        in_specs=[_rows(tm, d), _VM, _VM, _VM, _VM, _VM] + [_rows(tm, d)] * with_loss,
        out_specs=[_rows(tm, d)] + [_blocks(nb, tm, fs)] * 3 + [_rows(tm, d)] + [_VM] * with_loss,
        out_shape=[jax.ShapeDtypeStruct((t, d), F32)]
        + [jax.ShapeDtypeStruct((nb, t, fs), BF16)] * 3
        + [jax.ShapeDtypeStruct((t, d), F32)]
        + [jax.ShapeDtypeStruct((8, 128), F32)] * with_loss,
        args=[x, gpre, gpost, wg, wu, wd] + [target] * with_loss,
        exchanges=exchanges,
    )


def _ffn_bwd_hidden(name, dy, f, gpost, a, dup, dgate, wd, exchanges=()):
    t, d = dy.shape
    nb, fs, _ = wd.shape
    tm = _step_rows(t)

    def body(dy_ref, f_ref, gpost_ref, a_ref, dup_ref, dgate_ref, wd_ref, dg_ref, du_ref, dwd_ref, dgain_ref):
        @pl.when(pl.program_id(0) == 0)
        def _():
            dwd_ref[...] = jnp.zeros_like(dwd_ref)
            dgain_ref[...] = jnp.zeros_like(dgain_ref)

        df, dgain = _rms_bwd(dy_ref[...], f_ref[...], gpost_ref[...])
        dgain_ref[...] += dgain
        dfb = df.astype(BF16)
        for k in range(nb):
            da = _dot_nt(dfb, wd_ref[k])
            dwd_ref[k] += _dot_tn(a_ref[k], dfb)
            du_ref[k] = (da * dup_ref[k].astype(F32)).astype(BF16)
            dg_ref[k] = (da * dgate_ref[k].astype(F32)).astype(BF16)

    return _call(
        name,
        body,
        grid=(t // tm,),
        in_specs=[_rows(tm, d), _rows(tm, d), _VM] + [_blocks(nb, tm, fs)] * 3 + [_VM],
        out_specs=[_blocks(nb, tm, fs), _blocks(nb, tm, fs), _VM, _VM],
        out_shape=[
            jax.ShapeDtypeStruct((nb, t, fs), BF16),
            jax.ShapeDtypeStruct((nb, t, fs), BF16),
            jax.ShapeDtypeStruct((nb, fs, d), F32),
            jax.ShapeDtypeStruct((1, d), F32),
        ],
        args=[dy, f, gpost, a, dup, dgate, wd],
        exchanges=exchanges,
    )


def _bwd_in(name, dres, x, gpre, dzs, ws, transposed=False, exchanges=()):
    t, d = x.shape
    n = len(ws)
    resident = sum(6 * w.size for w in ws)
    tm = _step_rows(t) if resident <= VMEM_LIMIT_BYTES // 2 else _token_block(t)
    widths = [w.shape[1] if transposed else w.shape[2] for w in ws]

    def body(*refs):
        dres_ref, x_ref, gpre_ref = refs[:3]
        dz_refs = refs[3 : 3 + n]
        w_refs = refs[3 + n : 3 + 2 * n]
        dx_ref = refs[3 + 2 * n]
        dw_refs = refs[4 + 2 * n : 4 + 3 * n]
        dgain_ref = refs[4 + 3 * n]

        @pl.when(pl.program_id(0) == 0)
        def _():
            for dw_ref in dw_refs:
                dw_ref[...] = jnp.zeros_like(dw_ref)
            dgain_ref[...] = jnp.zeros_like(dgain_ref)

        xv = x_ref[...]
        gain = gpre_ref[...]
        hb = _rms(xv, gain).astype(BF16)
        dh = jnp.zeros((tm, d), F32)
        for dz_ref, w_ref, dw_ref in zip(dz_refs, w_refs, dw_refs):
            for k in range(w_ref.shape[0]):
                dz = dz_ref[k]
                if transposed:
                    dh = dh + _dot(dz, w_ref[k])
                    dw_ref[k] += _dot_tn(dz, hb)
                else:
                    dh = dh + _dot_nt(dz, w_ref[k])
                    dw_ref[k] += _dot_tn(hb, dz)
        dx, dgain = _rms_bwd(dh, xv, gain)
        dx_ref[...] = dres_ref[...] + dx
        dgain_ref[...] += dgain

    return _call(
        name,
        body,
        grid=(t // tm,),
        in_specs=[_rows(tm, d), _rows(tm, d), _VM]
        + [_blocks(w.shape[0], tm, bw) for w, bw in zip(ws, widths)]
        + [_VM] * n,
        out_specs=[_rows(tm, d)] + [_VM] * n + [_VM],
        out_shape=[jax.ShapeDtypeStruct((t, d), F32)]
        + [jax.ShapeDtypeStruct(w.shape, F32) for w in ws]
        + [jax.ShapeDtypeStruct((1, d), F32)],
        args=[dres, x, gpre, *dzs, *ws],
        exchanges=exchanges,
    )


def _causal_weights(ws_ref):
    row = lax.broadcasted_iota(jnp.int32, (CHUNK, CHUNK), 0)
    col = lax.broadcasted_iota(jnp.int32, (CHUNK, CHUNK), 1)
    return [jnp.where(row >= col, ws_ref[g], 0.0).astype(BF16) for g in range(A_GROUPS)]


def _layernorm_halves(v0, v1):
    width = v0.shape[-1] + v1.shape[-1]
    mu = (jnp.sum(v0, axis=-1, keepdims=True) + jnp.sum(v1, axis=-1, keepdims=True)) / width
    c0 = v0 - mu
    c1 = v1 - mu
    var = (jnp.sum(c0 * c0, axis=-1, keepdims=True) + jnp.sum(c1 * c1, axis=-1, keepdims=True)) / width
    rstd = lax.rsqrt(var + EPS)
    return c0 * rstd, c1 * rstd, rstd


def _spatial_gate(sv_ref, wtril, vl, bt_ref, half, tm, gd):
    for gg in range(A_GROUPS // 2):
        g = half * (A_GROUPS // 2) + gg
        bias = bt_ref[:, g : g + 1]
        for n in range(tm // CHUNK):
            blk = vl[n * CHUNK : (n + 1) * CHUNK, gg * gd : (gg + 1) * gd]
            sv_ref[n * CHUNK : (n + 1) * CHUNK, gg * gd : (gg + 1) * gd] = _dot(wtril[g], blk) + bias


def _mix_a_fwd(name, x, gpre, gpost, w_in, ln_g, ln_b, w_s, b_t, w_out, exchanges=()):
    t, d = x.shape
    _, _, q = w_in.shape
    gd = 2 * q // A_GROUPS
    tm = _step_rows(t)

    def body(x_ref, gpre_ref, gpost_ref, win_ref, lng_ref, lnb_ref, ws_ref, bt_ref, wout_ref,
             o_ref, z_ref, dz_ref, m_ref, sv_ref):
        xv = x_ref[...]
        hb = _rms(xv, gpre_ref[...]).astype(BF16)
        z = [None] * 4
        vls = []
        for k in (2, 3, 0, 1):
            act, slope = _gelu_and_grad(_dot(hb, win_ref[k]))
            z_ref[k] = act.astype(BF16)
            dz_ref[k] = slope.astype(BF16)
            z[k] = act
            if k == 3:
                vh0, vh1, _ = _layernorm_halves(z[2], z[3])
                vls = [(vh * lng_ref[b : b + 1, :] + lnb_ref[b : b + 1, :]).astype(BF16)
                       for b, vh in enumerate((vh0, vh1))]
        wtril = _causal_weights(ws_ref)
        m = jnp.zeros((tm, d), F32)
        for b in range(2):
            _spatial_gate(sv_ref, wtril, vls[b], bt_ref, b, tm, gd)
            gated = (z[b] * sv_ref[...]).astype(BF16)
            m = m + _dot(gated, wout_ref[b])
        m_ref[...] = m
        o_ref[...] = xv + _rms(m, gpost_ref[...])

    return _call(
        name,
        body,
        grid=(t // tm,),
        in_specs=[_rows(tm, d)] + [_VM] * 8,
        out_specs=[_rows(tm, d), _blocks(4, tm, q), _blocks(4, tm, q), _rows(tm, d)],
        out_shape=[
            jax.ShapeDtypeStruct((t, d), F32),
            jax.ShapeDtypeStruct((4, t, q), BF16),
            jax.ShapeDtypeStruct((4, t, q), BF16),
            jax.ShapeDtypeStruct((t, d), F32),
        ],
        scratch_shapes=[pltpu.VMEM((tm, q), F32)],
        args=[x, gpre, gpost, w_in, ln_g, ln_b, w_s, b_t, w_out],
        exchanges=exchanges,
    )


def _mix_a_bwd_hidden(name, dy, m, gpost, act, slope, ln_g, ln_b, w_s, b_t, w_out, exchanges=()):
    t, d = dy.shape
    _, _, q = act.shape
    gd = 2 * q // A_GROUPS
    tm = _step_rows(t)
    n_chunks = tm // CHUNK

    def body(dy_ref, m_ref, gpost_ref, z_ref, slope_ref, lng_ref, lnb_ref, ws_ref, bt_ref, wout_ref,
             dz_ref, dwout_ref, dws_ref, dbacc_ref, dlng_ref, dlnb_ref, dgain_ref, sv_ref, dvl_ref):
        first = pl.program_id(0) == 0

        @pl.when(first)
        def _():
            for ref in (dwout_ref, dws_ref, dbacc_ref, dlng_ref, dlnb_ref, dgain_ref):
                ref[...] = jnp.zeros_like(ref)

        dm, dgain = _rms_bwd(dy_ref[...], m_ref[...], gpost_ref[...])
        dgain_ref[...] += dgain
        dmb = dm.astype(BF16)
        vhs = list(_layernorm_halves(z_ref[2].astype(F32), z_ref[3].astype(F32)))
        rstd = vhs.pop()
        vls = [(vh * lng_ref[b : b + 1, :] + lnb_ref[b : b + 1, :]).astype(BF16) for b, vh in enumerate(vhs)]
        wtril = _causal_weights(ws_ref)
        dvhs = []
        for b in range(2):
            u = z_ref[b].astype(F32)
            _spatial_gate(sv_ref, wtril, vls[b], bt_ref, b, tm, gd)
            sv = sv_ref[...]
            gated = (u * sv).astype(BF16)
            dgated = _dot_nt(dmb, wout_ref[b])
            dwout_ref[b] += _dot_tn(gated, dmb)
            dz_ref[b] = (dgated * sv * slope_ref[b].astype(F32)).astype(BF16)
            dsv = dgated * u
            folded = dsv[0:CHUNK, :]
            for c in range(1, n_chunks):
                folded = folded + dsv[c * CHUNK : (c + 1) * CHUNK, :]
            for gg in range(A_GROUPS // 2):
                g = b * (A_GROUPS // 2) + gg
                dbacc_ref[:, g : g + 1] += jnp.sum(folded[:, gg * gd : (gg + 1) * gd], axis=1, keepdims=True)
            dsvb = dsv.astype(BF16)
            for gg in range(A_GROUPS // 2):
                g = b * (A_GROUPS // 2) + gg
                for c in range(n_chunks):
                    rows = slice(c * CHUNK, (c + 1) * CHUNK)
                    cols = slice(gg * gd, (gg + 1) * gd)
                    blk = dsvb[rows, cols]
                    dvl_ref[rows, cols] = _dot_tn(wtril[g], blk)
                    dws_ref[g] += _dot_nt(blk, vls[b][rows, cols])
            dvl = dvl_ref[...]
            dlng_ref[b : b + 1, :] += jnp.sum(dvl * vhs[b], axis=0, keepdims=True)
            dlnb_ref[b : b + 1, :] += jnp.sum(dvl, axis=0, keepdims=True)
            dvhs.append(dvl * lng_ref[b : b + 1, :])
        width = 2.0 * q
        m1 = (jnp.sum(dvhs[0], axis=-1, keepdims=True) + jnp.sum(dvhs[1], axis=-1, keepdims=True)) / width
        m2 = (jnp.sum(dvhs[0] * vhs[0], axis=-1, keepdims=True)
              + jnp.sum(dvhs[1] * vhs[1], axis=-1, keepdims=True)) / width
        for b in range(2):
            dv = rstd * (dvhs[b] - m1 - vhs[b] * m2)
            dz_ref[2 + b] = (dv * slope_ref[2 + b].astype(F32)).astype(BF16)

        @pl.when(pl.program_id(0) == t // tm - 1)
        def _():
            row = lax.broadcasted_iota(jnp.int32, (CHUNK, CHUNK), 0)
            col = lax.broadcasted_iota(jnp.int32, (CHUNK, CHUNK), 1)
            for g in range(A_GROUPS):
                dws_ref[g] = jnp.where(row >= col, dws_ref[g], 0.0)

    return _call(
        name,
        body,
        grid=(t // tm,),
        in_specs=[_rows(tm, d), _rows(tm, d), _VM, _blocks(4, tm, q), _blocks(4, tm, q)] + [_VM] * 5,
        out_specs=[_blocks(4, tm, q)] + [_VM] * 6,
        out_shape=[
            jax.ShapeDtypeStruct((4, t, q), BF16),
            jax.ShapeDtypeStruct((2, q, d), F32),
            jax.ShapeDtypeStruct((A_GROUPS, CHUNK, CHUNK), F32),
            jax.ShapeDtypeStruct((CHUNK, A_GROUPS), F32),
            jax.ShapeDtypeStruct((2, q), F32),
            jax.ShapeDtypeStruct((2, q), F32),
            jax.ShapeDtypeStruct((1, d), F32),
        ],
        scratch_shapes=[pltpu.VMEM((tm, q), F32), pltpu.VMEM((tm, q), F32)],
        args=[dy, m, gpost, act, slope, ln_g, ln_b, w_s, b_t, w_out],
        exchanges=exchanges,
    )


def _window_counts(tm, win):
    pos = pl.program_id(0) * tm + lax.broadcasted_iota(jnp.int32, (tm, 1), 0)
    return jnp.minimum(pos + 1, win).astype(F32)


def _pooled(p, halo, tm, gd, inside=False):
    prev = halo if inside else jnp.where(pl.program_id(0) == 0, 0.0, halo)
    ext = jnp.concatenate([prev, p], axis=0)
    out = []
    for g, win in enumerate(B_WINDOWS):
        s = ext[:, g * gd : (g + 1) * gd]
        step = 1
        while step < win:
            s = s + pltpu.roll(s, step, 0)
            step *= 2
        total = s[HALO:, :]
        count = float(win) if inside else _window_counts(tm, win)
        out.append(total / count - p[:, g * gd : (g + 1) * gd])
    return out


def _halo_spec(t, tm, d, ahead):
    per = tm // HALO
    if ahead:
        return pl.BlockSpec((HALO, d), lambda i: (jnp.minimum((i + 1) * per, t // HALO - 1), 0))
    return pl.BlockSpec((HALO, d), lambda i: (jnp.maximum(i * per - 1, 0), 0))


def _mix_b_fwd(name, x, gpre, gpost, w_in, w_grp, scale, w_out, exchanges=()):
    t, d = x.shape
    gd = d // len(B_WINDOWS)
    tm = _step_rows(t)

    def body(x_ref, xh_ref, gpre_ref, gpost_ref, win_ref, wgrp_ref, scale_ref, wout_ref, o_ref, p_ref, m_ref):
        xv = x_ref[...]
        gain = gpre_ref[...]
        p = _dot(_rms(xv, gain).astype(BF16), win_ref[...])
        p_ref[...] = p
        halo = _dot(_rms(xh_ref[...], gain).astype(BF16), win_ref[...])
        pooled = _pooled(p, halo, tm, gd)
        mixed = jnp.concatenate([_dot(pg.astype(BF16), wgrp_ref[g]) for g, pg in enumerate(pooled)], axis=1)
        m = _dot((mixed * scale_ref[...]).astype(BF16), wout_ref[...])
        m_ref[...] = m
        o_ref[...] = xv + _rms(m, gpost_ref[...])

    return _call(
        name,
        body,
        grid=(t // tm,),
        in_specs=[_rows(tm, d), _halo_spec(t, tm, d, False), _VM, _VM, _VM, _VM, _VM, _VM],
        out_specs=[_rows(tm, d)] * 3,
        out_shape=[jax.ShapeDtypeStruct((t, d), F32)] * 3,
        args=[x, x, gpre, gpost, w_in, w_grp, scale, w_out],
        exchanges=exchanges,
    )


def _mix_b_bwd(name, dy, x, m, p, gpre, gpost, w_in, w_grp, scale, w_out, exchanges=()):
    t, d = dy.shape
    gd = d // len(B_WINDOWS)
    tm = _step_rows(t)
    n_steps = t // tm

    def body(dy_ref, x_ref, m_ref, p_ref, pprev_ref, dynext_ref, mnext_ref, pnext_ref,
             gpre_ref, gpost_ref, win_ref, wgrp_ref, scale_ref, wout_ref,
             dx_ref, dwout_ref, dwgrp_ref, dscale_ref, dpost_ref, dwin_ref, dpre_ref):
        @pl.when(pl.program_id(0) == 0)
        def _():
            for ref in (dwout_ref, dwgrp_ref, dscale_ref, dpost_ref, dwin_ref, dpre_ref):
                ref[...] = jnp.zeros_like(ref)

        scale = scale_ref[...]

        def tail_bwd(dy_rows, m_rows, pooled, counts, accumulate):
            dm, dgain = _rms_bwd(dy_rows, m_rows, gpost_ref[...])
            dmb = dm.astype(BF16)
            pooled = [pg.astype(BF16) for pg in pooled]
            mixed = jnp.concatenate([_dot(pg, wgrp_ref[g]) for g, pg in enumerate(pooled)], axis=1)
            dms = _dot_nt(dmb, wout_ref[...])
            dmixed = (dms * scale).astype(BF16)
            if accumulate:
                dpost_ref[...] += dgain
                dwout_ref[...] += _dot_tn((mixed * scale).astype(BF16), dmb)
                dscale_ref[...] += jnp.sum(dms * mixed, axis=0, keepdims=True)
            parts = []
            for g, win in enumerate(B_WINDOWS):
                dmg = dmixed[:, g * gd : (g + 1) * gd]
                if accumulate:
                    dwgrp_ref[g] += _dot_tn(pooled[g], dmg)
                parts.append(_dot_nt(dmg, wgrp_ref[g]) / counts(win))
            return jnp.concatenate(parts, axis=1)

        dyv = dy_ref[...]
        pv = p_ref[...]
        dq_blk = tail_bwd(dyv, m_ref[...], _pooled(pv, pprev_ref[...], tm, gd),
                          lambda win: _window_counts(tm, win), True)
        dq_next = tail_bwd(dynext_ref[...], mnext_ref[...],
                           _pooled(pnext_ref[...], pv[tm - HALO :, :], HALO, gd, inside=True),
                           float, False)
        dq_next = jnp.where(pl.program_id(0) == n_steps - 1, 0.0, dq_next)
        ext = jnp.concatenate([dq_blk, dq_next], axis=0)
        parts = []
        for g, win in enumerate(B_WINDOWS):
            cols = slice(g * gd, (g + 1) * gd)
            s = ext[:, cols]
            step = 1
            while step < win:
                s = s + pltpu.roll(s, tm + HALO - step, 0)
                step *= 2
            parts.append((s[:tm, :] - dq_blk[:, cols] * _window_counts(tm, win)).astype(BF16))
        dp = jnp.concatenate(parts, axis=1)
        xv = x_ref[...]
        gain = gpre_ref[...]
        hb = _rms(xv, gain).astype(BF16)
        dwin_ref[...] += _dot_tn(hb, dp)
        dx, dgain = _rms_bwd(_dot_nt(dp, win_ref[...]), xv, gain)
        dx_ref[...] = dyv + dx
        dpre_ref[...] += dgain

    before, after = _halo_spec(t, tm, d, False), _halo_spec(t, tm, d, True)
    return _call(
        name,
        body,
        grid=(n_steps,),
        in_specs=[_rows(tm, d)] * 4 + [before, after, after, after] + [_VM] * 6,
        out_specs=[_rows(tm, d)] + [_VM] * 6,
        out_shape=[
            jax.ShapeDtypeStruct((t, d), F32),
            jax.ShapeDtypeStruct((d, d), F32),
            jax.ShapeDtypeStruct((len(B_WINDOWS), gd, gd), F32),
            jax.ShapeDtypeStruct((1, d), F32),
            jax.ShapeDtypeStruct((1, d), F32),
            jax.ShapeDtypeStruct((d, d), F32),
            jax.ShapeDtypeStruct((1, d), F32),
        ],
        args=[dy, x, m, p, p, dy, m, p, gpre, gpost, w_in, w_grp, scale, w_out],
        exchanges=exchanges,
    )


def _cast_into_slots(name, place, w, dtype):
    n_layers, r, c = w.shape

    def body(place_ref, w_ref, *o_refs):
        del place_ref
        for j, o_ref in enumerate(o_refs):
            @pl.when(pl.program_id(0) == j)
            def _():
                o_ref[...] = w_ref[...].astype(dtype)

    return pl.pallas_call(
        body,
        name=name,
        grid_spec=pltpu.PrefetchScalarGridSpec(
            num_scalar_prefetch=1,
            grid=(n_layers,),
            in_specs=[pl.BlockSpec((1, r, c), lambda i, place_ref: (i, 0, 0))],
            out_specs=[pl.BlockSpec((1, r, c), lambda i, place_ref: (place_ref[0], 0, 0))] * n_layers,
        ),
        out_shape=[jax.ShapeDtypeStruct((N_CHIPS, r, c), dtype)] * n_layers,
        compiler_params=_params(),
    )(place, w)


def _row_tile(r):
    if r % 256 == 0:
        return 256
    return r // 2 if r % 32 == 0 else r


def _pair_sum(name, place, dw, recv):
    _, r, c = dw.shape
    half = r // 2
    tr = _row_tile(half)
    per = half // tr

    def body(place_ref, a_ref, b_ref, o_ref):
        del place_ref
        o_ref[...] = (a_ref[...] + b_ref[...]).astype(BF16)

    return pl.pallas_call(
        body,
        name=name,
        grid_spec=pltpu.PrefetchScalarGridSpec(
            num_scalar_prefetch=1,
            grid=(N_CHIPS, per),
            in_specs=[
                pl.BlockSpec((1, tr, c), lambda k, i, place_ref: (k, place_ref[1] * per + i, 0)),
                pl.BlockSpec((1, tr, c), lambda k, i, place_ref: (k, i, 0)),
            ],
            out_specs=pl.BlockSpec((1, tr, c), lambda k, i, place_ref: (k, i, 0)),
        ),
        out_shape=jax.ShapeDtypeStruct(recv.shape, BF16),
        compiler_params=pltpu.CompilerParams(
            dimension_semantics=("arbitrary",) * 2, vmem_limit_bytes=VMEM_LIMIT_BYTES),
    )(place, dw, recv)


def _chip_sum(name, place, mine, others, exchanges=()):
    n_layers = len(mine)
    _, half, c = mine[0].shape
    tr = _row_tile(half)
    per = half // tr

    def body(place_ref, *refs):
        del place_ref
        o_ref = refs[-1]
        for j in range(n_layers):
            @pl.when(pl.program_id(0) == j)
            def _():
                parts = refs[4 * j : 4 * j + 4]
                acc = parts[0][...].astype(F32) + parts[1][...].astype(F32)
                acc = acc + parts[2][...].astype(F32)
                o_ref[...] = acc + parts[3][...].astype(F32)

    def part(j, flip):
        return pl.BlockSpec((1, tr, c), lambda l, i, place_ref: (
            jnp.bitwise_xor(place_ref[0], flip), jnp.where(l == j, i, 0), 0))

    args = []
    for j in range(n_layers):
        args += [mine[j], others[j], others[j], others[j]]
    (total,), ex_outs = _call(
        name,
        body,
        grid=(n_layers, per),
        in_specs=[part(j, flip) for j in range(n_layers) for flip in range(N_CHIPS)],
        out_specs=[pl.BlockSpec((1, tr, c), lambda l, i, place_ref: (l, place_ref[1] * per + i, 0))],
        out_shape=[jax.ShapeDtypeStruct((n_layers, 2 * half, c), F32)],
        args=args,
        prefetch=[place],
        exchanges=exchanges,
    )
    return total, ex_outs


def _adamw(name, w, g, m, v, exchanges=()):
    n_layers, r, c = w.shape
    tr = _row_tile(r)

    def body(w_ref, g_ref, m_ref, v_ref, go_ref, d_ref, nm_ref, nv_ref):
        gv = g_ref[...]
        go_ref[...] = gv
        nm = ADAM_B1 * m_ref[...] + (1.0 - ADAM_B1) * gv
        nv = ADAM_B2 * v_ref[...] + (1.0 - ADAM_B2) * jnp.square(gv)
        m_hat = nm / (1.0 - ADAM_B1 ** ADAM_STEP)
        v_hat = nv / (1.0 - ADAM_B2 ** ADAM_STEP)
        d_ref[...] = -ADAM_LR * (m_hat / (jnp.sqrt(v_hat) + ADAM_EPS) + ADAM_WD * w_ref[...])
        nm_ref[...] = nm
        nv_ref[...] = nv

    spec = pl.BlockSpec((1, tr, c), lambda l, i: (l, i, 0))
    return _call(
        name,
        body,
        grid=(n_layers, r // tr),
        in_specs=[spec] * 4,
        out_specs=[spec] * 4,
        out_shape=[jax.ShapeDtypeStruct(w.shape, F32)] * 4,
        args=[w, g, m, v],
        exchanges=exchanges,
    )


class _GatherSmall:
    def __init__(self, gathered):
        self.inputs = [gathered]
        self.out_shapes = [jax.ShapeDtypeStruct(gathered.shape, gathered.dtype)]
        self.aliases = {0: 0}
        self.n_sems = 4

    def _copies(self, outs, send, recv, outbound):
        x, y, c = _position()
        peers = [(x, y, 1 - c)] + [(px, py, c) for px, py in _other_chips(x, y)]
        copies = []
        for k, (px, py, pc) in enumerate(peers):
            rows = outs[0].at[4 * x + 2 * y + c if outbound else 4 * px + 2 * py + pc]
            copies.append(pltpu.make_async_remote_copy(
                src_ref=rows, dst_ref=rows, send_sem=send.at[k], recv_sem=recv.at[k],
                device_id=(px, py, pc), device_id_type=MESH))
        return copies

    def start(self, ins, outs, send, recv):
        for cp in self._copies(outs, send, recv, True):
            cp.start()

    def finish(self, ins, outs, send, recv):
        for cp in self._copies(outs, send, recv, False):
            cp.wait_recv()
        for cp in self._copies(outs, send, recv, True):
            cp.wait_send()


class _GatherSmallForward:
    def __init__(self, gathered):
        self.inputs = [gathered]
        self.out_shapes = [jax.ShapeDtypeStruct(gathered.shape, gathered.dtype)]
        self.aliases = {0: 0}
        self.n_sems = 3

    def _copies(self, outs, send, recv, core):
        x, y, c = _position()
        copies = []
        for k, (px, py) in enumerate(_other_chips(x, y)):
            rows = outs[0].at[4 * px + 2 * py + core]
            copies.append(pltpu.make_async_remote_copy(
                src_ref=rows, dst_ref=rows, send_sem=send.at[k], recv_sem=recv.at[k],
                device_id=(x, y, 1 - c), device_id_type=MESH))
        return copies

    def start(self, ins, outs, send, recv):
        for cp in self._copies(outs, send, recv, lax.axis_index("c")):
            cp.start()

    def finish(self, ins, outs, send, recv):
        c = lax.axis_index("c")
        for cp in self._copies(outs, send, recv, 1 - c):
            cp.wait_recv()
        for cp in self._copies(outs, send, recv, c):
            cp.wait_send()


def _sum_devices(name, gathered):
    _, m_per, n = gathered.shape

    def body(all_ref, sum_ref):
        acc = all_ref[0]
        for k in range(1, N_DEV):
            acc = acc + all_ref[k]
        sum_ref[...] = acc

    return pl.pallas_call(
        body,
        name=name,
        in_specs=[_VM],
        out_specs=_VM,
        out_shape=jax.ShapeDtypeStruct((m_per, n), F32),
        compiler_params=pltpu.CompilerParams(vmem_limit_bytes=VMEM_LIMIT_BYTES),
    )(gathered)


SHARDED = ("a_w_in", "a_w_out", "b_w_in", "b_w_grp", "b_scale", "b_w_out", "ffn_w_gate", "ffn_w_up", "ffn_w_down")
SMALL = ("a_ln_g", "a_ln_b", "a_w_s", "a_b_s", "mix_pre_g", "mix_post_g", "ffn_pre_g", "ffn_post_g")
WEIGHTS = ("a_w_in", "a_ln_g", "a_ln_b", "a_w_s", "a_b_s", "a_w_out", "b_w_in", "b_w_grp", "b_scale", "b_w_out",
           "mix_pre_g", "mix_post_g", "ffn_pre_g", "ffn_post_g", "ffn_w_gate", "ffn_w_up", "ffn_w_down")


TRANSPOSED = ("ffn_w_gate", "ffn_w_up")


def _as_layers(name, a):
    if name in TRANSPOSED:
        return jnp.swapaxes(a, 1, 2)
    if a.ndim == 2:
        return a.reshape(a.shape[0], 1, a.shape[1])
    return a.reshape(a.shape[0], -1, a.shape[-1])


def _from_layers(name, a, shape):
    if name in TRANSPOSED:
        return jnp.swapaxes(a, 1, 2)
    return a.reshape(shape)


def _pack_small(parts):
    return jnp.concatenate([p.reshape(-1, 128) for p in parts], axis=0)


def _unpack_small(packed, like):
    out, row = [], 0
    for ref in like:
        rows = ref.size // 128
        out.append(packed[row : row + rows].reshape(ref.shape))
        row += rows
    return out


def kernel(x, a_w_in, a_ln_g, a_ln_b, a_w_s, a_b_s, a_w_out, b_w_in, b_w_grp, b_scale, b_w_out, mix_pre_g, mix_post_g, ffn_pre_g, ffn_post_g, ffn_w_gate, ffn_w_up, ffn_w_down, loss_target, m_a_w_in, m_a_ln_g, m_a_ln_b, m_a_w_s, m_a_b_s, m_a_w_out, m_b_w_in, m_b_w_grp, m_b_scale, m_b_w_out, m_mix_pre_g, m_mix_post_g, m_ffn_pre_g, m_ffn_post_g, m_ffn_w_gate, m_ffn_w_up, m_ffn_w_down, v_a_w_in, v_a_ln_g, v_a_ln_b, v_a_w_s, v_a_b_s, v_a_w_out, v_b_w_in, v_b_w_grp, v_b_scale, v_b_w_out, v_mix_pre_g, v_mix_post_g, v_ffn_pre_g, v_ffn_post_g, v_ffn_w_gate, v_ffn_w_up, v_ffn_w_down):
    weights = dict(a_w_in=a_w_in, a_ln_g=a_ln_g, a_ln_b=a_ln_b, a_w_s=a_w_s, a_b_s=a_b_s, a_w_out=a_w_out,
                   b_w_in=b_w_in, b_w_grp=b_w_grp, b_scale=b_scale, b_w_out=b_w_out, mix_pre_g=mix_pre_g,
                   mix_post_g=mix_post_g, ffn_pre_g=ffn_pre_g, ffn_post_g=ffn_post_g, ffn_w_gate=ffn_w_gate,
                   ffn_w_up=ffn_w_up, ffn_w_down=ffn_w_down)
    mom1 = dict(a_w_in=m_a_w_in, a_ln_g=m_a_ln_g, a_ln_b=m_a_ln_b, a_w_s=m_a_w_s, a_b_s=m_a_b_s, a_w_out=m_a_w_out,
                b_w_in=m_b_w_in, b_w_grp=m_b_w_grp, b_scale=m_b_scale, b_w_out=m_b_w_out, mix_pre_g=m_mix_pre_g,
                mix_post_g=m_mix_post_g, ffn_pre_g=m_ffn_pre_g, ffn_post_g=m_ffn_post_g, ffn_w_gate=m_ffn_w_gate,
                ffn_w_up=m_ffn_w_up, ffn_w_down=m_ffn_w_down)
    mom2 = dict(a_w_in=v_a_w_in, a_ln_g=v_a_ln_g, a_ln_b=v_a_ln_b, a_w_s=v_a_w_s, a_b_s=v_a_b_s, a_w_out=v_a_w_out,
                b_w_in=v_b_w_in, b_w_grp=v_b_w_grp, b_scale=v_b_scale, b_w_out=v_b_w_out, mix_pre_g=v_mix_pre_g,
                mix_post_g=v_mix_post_g, ffn_pre_g=v_ffn_pre_g, ffn_post_g=v_ffn_post_g, ffn_w_gate=v_ffn_w_gate,
                ffn_w_up=v_ffn_w_up, ffn_w_down=v_ffn_w_down)

    t, d = x.shape[1], x.shape[2]
    depth = mix_pre_g.shape[0]
    gd_b = d // len(B_WINDOWS)
    xs = x.reshape(t, d)
    target = loss_target.reshape(t, d)

    chip = 2 * lax.axis_index("x") + lax.axis_index("y")
    place = jnp.stack([chip, lax.axis_index("c")]).astype(jnp.int32)
    bufs = {name: list(_cast_into_slots("cast_" + name, place, _as_layers(name, weights[name]),
                                        F32 if name == "b_scale" else BF16)) for name in SHARDED}

    def gain(name, i):
        return weights[name][i].reshape(1, d)

    def weight_keys(i):
        j = i // 2
        mixer = [("a_w_in", j), ("a_w_out", j)] if i % 2 == 0 else [("b_w_in", j), ("b_w_grp", j), ("b_w_out", j)]
        return mixer, [("ffn_w_gate", i), ("ffn_w_up", i), ("ffn_w_down", i)]

    def gather(keys):
        return _GatherWeights([bufs[n][j] for n, j in keys],
                              whole=[k for k, (n, _) in enumerate(keys) if n == "b_scale"])

    def gathered(keys, outs):
        for (n, j), buf in zip(keys, outs):
            bufs[n][j] = buf

    first = weight_keys(0)[0] + [("b_scale", j) for j in range(b_scale.shape[0])]
    gathered(first, _exchange("gather_first", [gather(first)])[0])
    saved = []
    cur = xs
    for i in range(depth):
        j = i // 2
        mixer_next, ffn_next = weight_keys(i + 1) if i + 1 < depth else ([], [])
        ffn_keys = weight_keys(i)[1]
        after_ffn = mixer_next if (i + 1) % 2 == 0 else mixer_next + ffn_next
        if i % 2 == 0:
            w_in = bufs["a_w_in"][j]
            q = w_in.shape[2]
            w_out = bufs["a_w_out"][j].reshape(2, q, d)
            ln_g = a_ln_g[j].reshape(2, q)
            ln_b = a_ln_b[j].reshape(2, q)
            b_t = jnp.transpose(a_b_s[j])
            (nxt, act, slope, m), (got,) = _mix_a_fwd(
                f"mix_a_fwd{j}", cur, gain("mix_pre_g", i), gain("mix_post_g", i), w_in, ln_g, ln_b, a_w_s[j], b_t,
                w_out, exchanges=[gather(ffn_keys)])
            gathered(ffn_keys, got)
            mix_saved = dict(x=cur, act=act, slope=slope, m=m, w_in=w_in, w_out=w_out, ln_g=ln_g, ln_b=ln_b, b_t=b_t)
        else:
            w_in = bufs["b_w_in"][j].reshape(d, d)
            w_out = bufs["b_w_out"][j].reshape(d, d)
            w_grp = jnp.transpose(bufs["b_w_grp"][j].reshape(N_CHIPS, len(B_WINDOWS), gd_b // N_CHIPS, gd_b),
                                  (1, 0, 2, 3)).reshape(len(B_WINDOWS), gd_b, gd_b)
            scale = bufs["b_scale"][j].reshape(1, d)
            (nxt, p, m), _ = _mix_b_fwd(f"mix_b_fwd{j}", cur, gain("mix_pre_g", i), gain("mix_post_g", i),
                                        w_in, w_grp, scale, w_out)
            mix_saved = dict(x=cur, p=p, m=m, w_in=w_in, w_out=w_out, w_grp=w_grp, scale=scale)
        cur = nxt
        wg, wu, wd = (bufs[n][k].reshape(1, -1, d) for n, k in ffn_keys)
        (nxt, a, dup, dgate, f, *sq), (got,) = _ffn_fwd(
            f"ffn_fwd{i}", cur, gain("ffn_pre_g", i), gain("ffn_post_g", i), wg, wu, wd,
            target=target if i == depth - 1 else None, exchanges=[gather(after_ffn)])
        gathered(after_ffn, got)
        saved.append((mix_saved, dict(x=cur, a=a, dup=dup, dgate=dgate, f=f, wg=wg, wu=wu, wd=wd)))
        cur = nxt

    dcur = cur
    loss = lax.psum(0.5 * sq[0][0, 0] / d, ("x", "y", "c"))

    grads = {name: [None] * weights[name].shape[0] for name in WEIGHTS}
    state = dict(to_sibling=[], to_chips=[])
    pair, from_chips = {}, {}

    def exchanges_due(carry=True):
        if not carry:
            return []
        return [_ToSibling([a for _, _, a in state["to_sibling"]]), _ToChips([a for _, _, a in state["to_chips"]])]

    def exchanged(outs):
        if not outs:
            return
        from_sibling, arrived = outs
        for (n, k, _), got in zip(state["to_chips"], arrived):
            from_chips[n, k] = got
        state["to_chips"] = []
        for (n, k, dw), got in zip(state["to_sibling"], from_sibling):
            pair[n, k] = _pair_sum(f"pair_sum_{n}{k}", place, dw, got)
            state["to_chips"].append((n, k, pair[n, k]))
        state["to_sibling"] = []

    def made(name, k, dw):
        grads[name][k] = dw
        state["to_sibling"].append((name, k, dw))

    def own_slot(packed):
        return lax.dynamic_update_slice(jnp.zeros((N_DEV,) + packed.shape, F32), packed[None],
                                        (2 * chip + place[1], 0, 0))

    for i in reversed(range(depth)):
        j = i // 2
        mix_saved, ffn_saved = saved[i]
        s = ffn_saved
        both = i == 0
        (dg, du, dwd, dgain), outs = _ffn_bwd_hidden(
            f"ffn_bwd_hidden{i}", dcur, s["f"], gain("ffn_post_g", i), s["a"], s["dup"], s["dgate"], s["wd"],
            exchanges=exchanges_due(both))
        exchanged(outs)
        grads["ffn_post_g"][i] = dgain
        made("ffn_w_down", i, dwd.reshape(N_CHIPS, -1, d))
        (dcur, dwg, dwu, dgain), outs = _bwd_in(
            f"ffn_bwd_in{i}", dcur, s["x"], gain("ffn_pre_g", i), [dg, du], [s["wg"], s["wu"]], transposed=True,
            exchanges=exchanges_due())
        exchanged(outs)
        grads["ffn_pre_g"][i] = dgain
        made("ffn_w_gate", i, dwg.reshape(N_CHIPS, -1, d))
        made("ffn_w_up", i, dwu.reshape(N_CHIPS, -1, d))
        s = mix_saved
        if i % 2 == 0:
            (dz, dwout, dws, dbacc, dlng, dlnb, dgain), outs = _mix_a_bwd_hidden(
                f"mix_a_bwd_hidden{j}", dcur, s["m"], gain("mix_post_g", i), s["act"], s["slope"], s["ln_g"], s["ln_b"],
                a_w_s[j], s["b_t"], s["w_out"], exchanges=exchanges_due(both))
            exchanged(outs)
            grads["a_w_s"][j] = dws
            grads["a_b_s"][j] = jnp.transpose(dbacc)
            grads["a_ln_g"][j] = dlng.reshape(-1)
            grads["a_ln_b"][j] = dlnb.reshape(-1)
            grads["mix_post_g"][i] = dgain
            made("a_w_out", j, dwout.reshape(N_CHIPS, -1, d))
            riders = []
            if i == 0:
                grads["mix_pre_g"][0] = jnp.zeros((1, d), F32)
                small_grads = [jnp.stack([g.reshape(weights[name].shape[1:]) for g in grads[name]], axis=0)
                               for name in SMALL]
                scale_grad = jnp.concatenate(grads["b_scale"], axis=0)
                riders = [_GatherSmall(own_slot(_pack_small(small_grads + [scale_grad])))]
            (dcur, dwin, dgain), outs = _bwd_in(
                f"mix_a_bwd_in{j}", dcur, s["x"], gain("mix_pre_g", i), [dz], [s["w_in"]],
                exchanges=exchanges_due() + riders)
            exchanged(outs[:2])
            gathered_small = outs[2][0] if riders else None
            grads["mix_pre_g"][i] = dgain
            made("a_w_in", j, dwin)
        else:
            (dcur, dwout, dwgrp, dscale, dpost, dwin, dpre), outs = _mix_b_bwd(
                f"mix_b_bwd{j}", dcur, s["x"], s["m"], s["p"], gain("mix_pre_g", i), gain("mix_post_g", i),
                s["w_in"], s["w_grp"], s["scale"], s["w_out"], exchanges=exchanges_due())
            exchanged(outs)
            grads["b_scale"][j] = dscale
            grads["mix_post_g"][i] = dpost
            grads["mix_pre_g"][i] = dpre
            made("b_w_out", j, dwout.reshape(N_CHIPS, -1, d))
            made("b_w_grp", j, jnp.transpose(
                dwgrp.reshape(len(B_WINDOWS), N_CHIPS, gd_b // N_CHIPS, gd_b), (1, 0, 2, 3)).reshape(N_CHIPS, -1, gd_b))
            made("b_w_in", j, dwin.reshape(N_CHIPS, -1, d))
    grad_x = dcur.reshape(x.shape)

    outs = _exchange("grads_last", exchanges_due() + [_GatherSmallForward(gathered_small),
                                                      _GatherSmall(own_slot(_pack_small([grads["mix_pre_g"][0]])))])
    exchanged(outs[:2])
    gathered_small = outs[2][0]
    outs = _exchange("grads_last_to_chips", exchanges_due() + [_GatherSmallForward(outs[3][0])])
    exchanged(outs[:2])
    gathered_last = outs[2][0]

    reduced_names = [name for name in SHARDED if name != "b_scale"]
    sums = []
    for name in reduced_names:
        layers = range(weights[name].shape[0])
        sums.append(_chip_sum("chip_sum_" + name, place, [pair[name, k] for k in layers],
                              [from_chips[name, k] for k in layers])[0])
    reduced = dict(zip(reduced_names, _exchange("swap_halves", [_SwapHalves(sums)])[0]))

    out_g, out_d, out_m, out_v = {}, {}, {}, {}
    for name in reduced_names:
        shape = weights[name].shape
        results, _ = _adamw("adamw_" + name, _as_layers(name, weights[name]), reduced[name],
                            _as_layers(name, mom1[name]), _as_layers(name, mom2[name]))
        out_g[name], out_d[name], out_m[name], out_v[name] = (_from_layers(name, a, shape) for a in results)

    summed = _sum_devices("sum_devices", gathered_small)
    last_rows = sum(weights[name].size // 128 for name in SMALL[: SMALL.index("mix_pre_g")])
    summed = lax.dynamic_update_slice(summed, _sum_devices("sum_devices_last", gathered_last), (last_rows, 0))
    small_rows = summed.shape[0] - scale_grad.size // 128
    scale_sum = summed[small_rows:].reshape(scale_grad.shape)
    scale_mine = lax.dynamic_slice_in_dim(scale_sum, chip * b_scale.shape[1], b_scale.shape[1], axis=1)[:, None, :]
    summed = summed[:small_rows]
    results, _ = _adamw("adamw_b_scale", _as_layers("b_scale", b_scale), scale_mine,
                        _as_layers("b_scale", m_b_scale), _as_layers("b_scale", v_b_scale))
    out_g["b_scale"], out_d["b_scale"], out_m["b_scale"], out_v["b_scale"] = (a.reshape(b_scale.shape) for a in results)
    small_like = [weights[name] for name in SMALL]
    packs = [_pack_small([src[name] for name in SMALL]).reshape(1, -1, 128) for src in (weights, mom1, mom2)]
    results, _ = _adamw("adamw_small", packs[0], summed.reshape(1, -1, 128), packs[1], packs[2])
    for dst, packed in zip((out_g, out_d, out_m, out_v), (a[0] for a in results)):
        for name, val in zip(SMALL, _unpack_small(packed, small_like)):
            dst[name] = val

    return (loss, grad_x, *[out_g[n] for n in WEIGHTS], *[out_d[n] for n in WEIGHTS],
            *[out_m[n] for n in WEIGHTS], *[out_v[n] for n in WEIGHTS])
```

```python
import functools
import math

import jax
import jax.numpy as jnp
from jax import lax
from jax.experimental import pallas as pl
from jax.experimental.pallas import tpu as pltpu

F32 = jnp.float32
BF16 = jnp.bfloat16
MESH = pl.DeviceIdType.MESH

EPS = 1e-6
CHUNK = 128
A_GROUPS = 8
B_WINDOWS = (2, 4, 8, 16)
HALO = 16
N_CHIPS = 4
N_DEV = 8

ADAM_LR = 0.001
ADAM_B1 = 0.9
ADAM_B2 = 0.999
ADAM_EPS = 1e-08
ADAM_WD = 0.01
ADAM_STEP = 10

VMEM_LIMIT_BYTES = 60 * 1024 * 1024
INV_SQRT2 = 1.0 / math.sqrt(2.0)
INV_SQRT_2PI = 1.0 / math.sqrt(2.0 * math.pi)

_ANY = pl.BlockSpec(memory_space=pl.ANY)
_VM = pl.BlockSpec(memory_space=pltpu.VMEM)


def _params():
    return pltpu.CompilerParams(dimension_semantics=("arbitrary",), vmem_limit_bytes=VMEM_LIMIT_BYTES)


def _token_block(t):
    return 256 if t >= 1024 else 128


def _step_rows(t):
    return 2 * _token_block(t)


def _rows(tm, d):
    return pl.BlockSpec((tm, d), lambda i: (i, 0))


def _blocks(nb, tm, bw):
    return pl.BlockSpec((nb, tm, bw), lambda i: (0, i, 0))


def _dot(a, b):
    return lax.dot_general(a, b, (((1,), (0,)), ((), ())), preferred_element_type=F32)


def _dot_nt(a, b):
    return lax.dot_general(a, b, (((1,), (1,)), ((), ())), preferred_element_type=F32)


def _dot_tn(a, b):
    return lax.dot_general(a, b, (((0,), (0,)), ((), ())), preferred_element_type=F32)


def _rms(x, g):
    return x * lax.rsqrt(jnp.mean(x * x, axis=-1, keepdims=True) + EPS) * g


def _rms_bwd(dy, x, g):
    r = lax.rsqrt(jnp.mean(x * x, axis=-1, keepdims=True) + EPS)
    n = x * r
    dn = dy * g
    dx = r * (dn - n * jnp.mean(dn * n, axis=-1, keepdims=True))
    return dx, jnp.sum(dy * n, axis=0, keepdims=True)


def _gelu_and_grad(x):
    cdf = 0.5 * (1.0 + lax.erf(x * INV_SQRT2))
    return x * cdf, cdf + x * (jnp.exp(-0.5 * x * x) * INV_SQRT_2PI)


def _position():
    return lax.axis_index("x"), lax.axis_index("y"), lax.axis_index("c")


def _other_chips(x, y):
    return [(1 - x, y), (x, 1 - y), (1 - x, 1 - y)]


class _GatherWeights:
    def __init__(self, bufs, whole=()):
        self.inputs = list(bufs)
        self.out_shapes = [jax.ShapeDtypeStruct(b.shape, b.dtype) for b in bufs]
        self.aliases = {w: w for w in range(len(bufs))}
        self.n_sems = 6 * len(bufs)
        self.whole = frozenset(whole)

    def _part(self, outs, w, slot, core):
        if w in self.whole:
            return outs[w].at[slot]
        half = outs[w].shape[1] // 2
        return outs[w].at[slot, pl.ds(core * half, half)]

    def _ici(self, outs, send, recv, w, j, slot):
        x, y, c = _position()
        px, py = _other_chips(x, y)[j]
        part = self._part(outs, w, slot, c)
        return pltpu.make_async_remote_copy(
            src_ref=part, dst_ref=part, send_sem=send.at[6 * w + j], recv_sem=recv.at[6 * w + j],
            device_id=(px, py, c), device_id_type=MESH)

    def _d2d(self, outs, send, recv, w, j, slot, core):
        x, y, c = _position()
        part = self._part(outs, w, slot, core)
        return pltpu.make_async_remote_copy(
            src_ref=part, dst_ref=part, send_sem=send.at[6 * w + 3 + j], recv_sem=recv.at[6 * w + 3 + j],
            device_id=(x, y, 1 - c), device_id_type=MESH)

    def start(self, ins, outs, send, recv):
        x, y, _ = _position()
        for w in range(len(outs)):
            for j in range(3):
                self._ici(outs, send, recv, w, j, 2 * x + y).start()

    def advance(self, ins, outs, send, recv):
        x, y, c = _position()
        slots = [2 * px + py for px, py in _other_chips(x, y)]
        for w in range(len(outs)):
            for j, slot in enumerate(slots):
                self._ici(outs, send, recv, w, j, slot).wait_recv()
                if w not in self.whole:
                    self._d2d(outs, send, recv, w, j, slot, c).start()

    def finish(self, ins, outs, send, recv):
        x, y, c = _position()
        slots = [2 * px + py for px, py in _other_chips(x, y)]
        for w in range(len(outs)):
            for j, slot in enumerate(slots):
                if w not in self.whole:
                    self._d2d(outs, send, recv, w, j, slot, 1 - c).wait_recv()
        for w in range(len(outs)):
            for j, slot in enumerate(slots):
                self._ici(outs, send, recv, w, j, 2 * x + y).wait_send()
                if w not in self.whole:
                    self._d2d(outs, send, recv, w, j, slot, c).wait_send()


class _ToSibling:
    def __init__(self, grads):
        self.inputs = list(grads)
        self.out_shapes = [jax.ShapeDtypeStruct((g.shape[0], g.shape[1] // 2, g.shape[2]), g.dtype) for g in grads]
        self.aliases = {}
        self.n_sems = len(grads)

    def _copy(self, ins, outs, send, recv, w):
        x, y, c = _position()
        half = ins[w].shape[1] // 2
        return pltpu.make_async_remote_copy(
            src_ref=ins[w].at[:, pl.ds((1 - c) * half, half)], dst_ref=outs[w],
            send_sem=send.at[w], recv_sem=recv.at[w], device_id=(x, y, 1 - c), device_id_type=MESH)

    def start(self, ins, outs, send, recv):
        for w in range(len(ins)):
            self._copy(ins, outs, send, recv, w).start()

    def finish(self, ins, outs, send, recv):
        for w in range(len(ins)):
            self._copy(ins, outs, send, recv, w).wait_recv()
        for w in range(len(ins)):
            self._copy(ins, outs, send, recv, w).wait_send()


class _ToChips:
    def __init__(self, parts):
        self.inputs = list(parts)
        self.out_shapes = [jax.ShapeDtypeStruct(p.shape, p.dtype) for p in parts]
        self.aliases = {}
        self.n_sems = 3 * len(parts)

    def _copy(self, ins, outs, send, recv, w, j, outbound):
        x, y, c = _position()
        px, py = _other_chips(x, y)[j]
        me, peer = 2 * x + y, 2 * px + py
        src_slot, dst_slot = (peer, me) if outbound else (me, peer)
        return pltpu.make_async_remote_copy(
            src_ref=ins[w].at[src_slot], dst_ref=outs[w].at[dst_slot],
            send_sem=send.at[3 * w + j], recv_sem=recv.at[3 * w + j], device_id=(px, py, c), device_id_type=MESH)

    def start(self, ins, outs, send, recv):
        for w in range(len(ins)):
            for j in range(3):
                self._copy(ins, outs, send, recv, w, j, True).start()

    def finish(self, ins, outs, send, recv):
        for w in range(len(ins)):
            for j in range(3):
                self._copy(ins, outs, send, recv, w, j, False).wait_recv()
        for w in range(len(ins)):
            for j in range(3):
                self._copy(ins, outs, send, recv, w, j, True).wait_send()


class _SwapHalves:
    def __init__(self, bufs):
        self.inputs = list(bufs)
        self.out_shapes = [jax.ShapeDtypeStruct(b.shape, b.dtype) for b in bufs]
        self.aliases = {w: w for w in range(len(bufs))}
        self.n_sems = len(bufs)

    def _copy(self, outs, send, recv, w, core):
        x, y, c = _position()
        half = outs[w].shape[1] // 2
        rows = outs[w].at[:, pl.ds(core * half, half)]
        return pltpu.make_async_remote_copy(
            src_ref=rows, dst_ref=rows, send_sem=send.at[w], recv_sem=recv.at[w],
            device_id=(x, y, 1 - c), device_id_type=MESH)

    def start(self, ins, outs, send, recv):
        c = lax.axis_index("c")
        for w in range(len(outs)):
            self._copy(outs, send, recv, w, c).start()

    def finish(self, ins, outs, send, recv):
        c = lax.axis_index("c")
        for w in range(len(outs)):
            self._copy(outs, send, recv, w, 1 - c).wait_recv()
        for w in range(len(outs)):
            self._copy(outs, send, recv, w, c).wait_send()


def _call(name, body, *, grid, in_specs, out_specs, out_shape, args, scratch_shapes=(), prefetch=(), exchanges=()):
    given = list(exchanges)
    exchanges = [e for e in given if e.inputs]
    n_pre, n_in, n_out, n_scr = len(prefetch), len(args), len(out_shape), len(scratch_shapes)
    ex_in = [a for e in exchanges for a in e.inputs]
    ex_out = [s for e in exchanges for s in e.out_shapes]
    aliases = {}
    at_in, at_out = n_pre + n_in, n_out
    for e in exchanges:
        for i, o in e.aliases.items():
            aliases[at_in + i] = at_out + o
        at_in += len(e.inputs)
        at_out += len(e.out_shapes)

    def at_step(last):
        hit = None
        for axis, n in enumerate(grid):
            here = pl.program_id(axis) == (n - 1 if last else 0)
            hit = here if hit is None else jnp.logical_and(hit, here)
        return hit

    def fused(*refs):
        pre, refs = refs[:n_pre], refs[n_pre:]
        body_in, refs = refs[:n_in], refs[n_in:]
        ex_in_refs, refs = refs[: len(ex_in)], refs[len(ex_in) :]
        body_out, refs = refs[:n_out], refs[n_out:]
        ex_out_refs, refs = refs[: len(ex_out)], refs[len(ex_out) :]
        body_scr, sems = refs[:n_scr], refs[n_scr:]

        def each(stage):
            a = b = 0
            for n, e in enumerate(exchanges):
                ins, outs = ex_in_refs[a : a + len(e.inputs)], ex_out_refs[b : b + len(e.out_shapes)]
                if hasattr(e, stage):
                    getattr(e, stage)(ins, outs, sems[2 * n], sems[2 * n + 1])
                a += len(e.inputs)
                b += len(e.out_shapes)

        if exchanges:
            @pl.when(at_step(False))
            def _():
                each("start")

        if body is not None:
            body(*pre, *body_in, *body_out, *body_scr)

        if exchanges:
            late = pl.program_id(0) == (3 * grid[0]) // 4 if len(grid) == 1 else at_step(True)

            @pl.when(late)
            def _():
                each("advance")

            @pl.when(at_step(True))
            def _():
                each("finish")

    outs = pl.pallas_call(
        fused,
        name=name,
        grid_spec=pltpu.PrefetchScalarGridSpec(
            num_scalar_prefetch=n_pre,
            grid=grid,
            in_specs=list(in_specs) + [_ANY] * len(ex_in),
            out_specs=list(out_specs) + [_ANY] * len(ex_out),
            scratch_shapes=list(scratch_shapes)
            + [pltpu.SemaphoreType.DMA((e.n_sems,)) for e in exchanges for _ in range(2)],
        ),
        out_shape=list(out_shape) + ex_out,
        input_output_aliases=aliases,
        compiler_params=pltpu.CompilerParams(
            dimension_semantics=("arbitrary",) * len(grid), vmem_limit_bytes=VMEM_LIMIT_BYTES),
    )(*prefetch, *args, *ex_in)
    body_outs, rest = list(outs[:n_out]), list(outs[n_out:])
    ex_outs = []
    for e in given:
        n_e = len(e.out_shapes) if e.inputs else 0
        ex_outs.append(rest[:n_e])
        rest = rest[n_e:]
    return body_outs, ex_outs


def _exchange(name, exchanges):
    return _call(name, None, grid=(1,), in_specs=[], out_specs=[], out_shape=[], args=[], exchanges=exchanges)[1]


def _ffn_fwd(name, x, gpre, gpost, wg, wu, wd, target=None, exchanges=()):
    t, d = x.shape
    nb, fs, _ = wg.shape
    sub = _token_block(t)
    tm = _step_rows(t)
    with_loss = target is not None

    def body(x_ref, gpre_ref, gpost_ref, wg_ref, wu_ref, wd_ref, *refs):
        if with_loss:
            t_ref, o_ref, a_ref, dup_ref, dgate_ref, f_ref, sq_ref = refs

            @pl.when(pl.program_id(0) == 0)
            def _():
                sq_ref[...] = jnp.zeros_like(sq_ref)
        else:
            o_ref, a_ref, dup_ref, dgate_ref, f_ref = refs
        for h in range(tm // sub):
            rows = slice(h * sub, (h + 1) * sub)
            xv = x_ref[rows, :]
            hb = _rms(xv, gpre_ref[...]).astype(BF16)
            f = jnp.zeros((sub, d), F32)
            for k in range(nb):
                g = _dot_nt(hb, wg_ref[k])
                u = _dot_nt(hb, wu_ref[k])
                s = jax.nn.sigmoid(g)
                sg = g * s
                a = (sg * u).astype(BF16)
                a_ref[k, rows, :] = a
                dup_ref[k, rows, :] = sg.astype(BF16)
                dgate_ref[k, rows, :] = (u * (s * (1.0 + g * (1.0 - s)))).astype(BF16)
                f = f + _dot(a, wd_ref[k])
            f_ref[rows, :] = f
            y = xv + _rms(f, gpost_ref[...])
            if with_loss:
                err = y - t_ref[rows, :]
                o_ref[rows, :] = err / d
                sq_ref[...] += jnp.sum(err * err)
            else:
                o_ref[rows, :] = y

    return _call(
        name,
        body,
        grid=(t // tm,),
        in_specs=[_rows(tm, d), _VM, _VM, _VM, _VM, _VM] + [_rows(tm, d)] * with_loss,
        out_specs=[_rows(tm, d)] + [_blocks(nb, tm, fs)] * 3 + [_rows(tm, d)] + [_VM] * with_loss,
        out_shape=[jax.ShapeDtypeStruct((t, d), F32)]
        + [jax.ShapeDtypeStruct((nb, t, fs), BF16)] * 3
        + [jax.ShapeDtypeStruct((t, d), F32)]
        + [jax.ShapeDtypeStruct((8, 128), F32)] * with_loss,
        args=[x, gpre, gpost, wg, wu, wd] + [target] * with_loss,
        exchanges=exchanges,
    )


def _ffn_bwd_hidden(name, dy, f, gpost, a, dup, dgate, wd, exchanges=()):
    t, d = dy.shape
    nb, fs, _ = wd.shape
    tm = _step_rows(t)

    def body(dy_ref, f_ref, gpost_ref, a_ref, dup_ref, dgate_ref, wd_ref, dg_ref, du_ref, dwd_ref, dgain_ref):
        @pl.when(pl.program_id(0) == 0)
        def _():
            dwd_ref[...] = jnp.zeros_like(dwd_ref)
            dgain_ref[...] = jnp.zeros_like(dgain_ref)

        df, dgain = _rms_bwd(dy_ref[...], f_ref[...], gpost_ref[...])
        dgain_ref[...] += dgain
        dfb = df.astype(BF16)
        for k in range(nb):
            da = _dot_nt(dfb, wd_ref[k])
            dwd_ref[k] += _dot_tn(a_ref[k], dfb)
            du_ref[k] = (da * dup_ref[k].astype(F32)).astype(BF16)
            dg_ref[k] = (da * dgate_ref[k].astype(F32)).astype(BF16)

    return _call(
        name,
        body,
        grid=(t // tm,),
        in_specs=[_rows(tm, d), _rows(tm, d), _VM] + [_blocks(nb, tm, fs)] * 3 + [_VM],
        out_specs=[_blocks(nb, tm, fs), _blocks(nb, tm, fs), _VM, _VM],
        out_shape=[
            jax.ShapeDtypeStruct((nb, t, fs), BF16),
            jax.ShapeDtypeStruct((nb, t, fs), BF16),
            jax.ShapeDtypeStruct((nb, fs, d), F32),
            jax.ShapeDtypeStruct((1, d), F32),
        ],
        args=[dy, f, gpost, a, dup, dgate, wd],
        exchanges=exchanges,
    )


def _bwd_in(name, dres, x, gpre, dzs, ws, transposed=False, exchanges=()):
    t, d = x.shape
    n = len(ws)
    resident = sum(6 * w.size for w in ws)
    tm = _step_rows(t) if resident <= VMEM_LIMIT_BYTES // 2 else _token_block(t)
    widths = [w.shape[1] if transposed else w.shape[2] for w in ws]

    def body(*refs):
        dres_ref, x_ref, gpre_ref = refs[:3]
        dz_refs = refs[3 : 3 + n]
        w_refs = refs[3 + n : 3 + 2 * n]
        dx_ref = refs[3 + 2 * n]
        dw_refs = refs[4 + 2 * n : 4 + 3 * n]
        dgain_ref = refs[4 + 3 * n]

        @pl.when(pl.program_id(0) == 0)
        def _():
            for dw_ref in dw_refs:
                dw_ref[...] = jnp.zeros_like(dw_ref)
            dgain_ref[...] = jnp.zeros_like(dgain_ref)

        xv = x_ref[...]
        gain = gpre_ref[...]
        hb = _rms(xv, gain).astype(BF16)
        dh = jnp.zeros((tm, d), F32)
        for dz_ref, w_ref, dw_ref in zip(dz_refs, w_refs, dw_refs):
            for k in range(w_ref.shape[0]):
                dz = dz_ref[k]
                if transposed:
                    dh = dh + _dot(dz, w_ref[k])
                    dw_ref[k] += _dot_tn(dz, hb)
                else:
                    dh = dh + _dot_nt(dz, w_ref[k])
                    dw_ref[k] += _dot_tn(hb, dz)
        dx, dgain = _rms_bwd(dh, xv, gain)
        dx_ref[...] = dres_ref[...] + dx
        dgain_ref[...] += dgain

    return _call(
        name,
        body,
        grid=(t // tm,),
        in_specs=[_rows(tm, d), _rows(tm, d), _VM]
        + [_blocks(w.shape[0], tm, bw) for w, bw in zip(ws, widths)]
        + [_VM] * n,
        out_specs=[_rows(tm, d)] + [_VM] * n + [_VM],
        out_shape=[jax.ShapeDtypeStruct((t, d), F32)]
        + [jax.ShapeDtypeStruct(w.shape, F32) for w in ws]
        + [jax.ShapeDtypeStruct((1, d), F32)],
        args=[dres, x, gpre, *dzs, *ws],
        exchanges=exchanges,
    )


def _causal_weights(ws_ref):
    row = lax.broadcasted_iota(jnp.int32, (CHUNK, CHUNK), 0)
    col = lax.broadcasted_iota(jnp.int32, (CHUNK, CHUNK), 1)
    return [jnp.where(row >= col, ws_ref[g], 0.0).astype(BF16) for g in range(A_GROUPS)]


def _layernorm_halves(v0, v1):
    width = v0.shape[-1] + v1.shape[-1]
    mu = (jnp.sum(v0, axis=-1, keepdims=True) + jnp.sum(v1, axis=-1, keepdims=True)) / width
    c0 = v0 - mu
    c1 = v1 - mu
    var = (jnp.sum(c0 * c0, axis=-1, keepdims=True) + jnp.sum(c1 * c1, axis=-1, keepdims=True)) / width
    rstd = lax.rsqrt(var + EPS)
    return c0 * rstd, c1 * rstd, rstd


def _spatial_gate(sv_ref, wtril, vl, bt_ref, half, tm, gd):
    for gg in range(A_GROUPS // 2):
        g = half * (A_GROUPS // 2) + gg
        bias = bt_ref[:, g : g + 1]
        for n in range(tm // CHUNK):
            blk = vl[n * CHUNK : (n + 1) * CHUNK, gg * gd : (gg + 1) * gd]
            sv_ref[n * CHUNK : (n + 1) * CHUNK, gg * gd : (gg + 1) * gd] = _dot(wtril[g], blk) + bias


def _mix_a_fwd(name, x, gpre, gpost, w_in, ln_g, ln_b, w_s, b_t, w_out, exchanges=()):
    t, d = x.shape
    _, _, q = w_in.shape
    gd = 2 * q // A_GROUPS
    tm = _step_rows(t)

    def body(x_ref, gpre_ref, gpost_ref, win_ref, lng_ref, lnb_ref, ws_ref, bt_ref, wout_ref,
             o_ref, z_ref, dz_ref, m_ref, sv_ref):
        xv = x_ref[...]
        hb = _rms(xv, gpre_ref[...]).astype(BF16)
        z = [None] * 4
        vls = []
        for k in (2, 3, 0, 1):
            act, slope = _gelu_and_grad(_dot(hb, win_ref[k]))
            z_ref[k] = act.astype(BF16)
            dz_ref[k] = slope.astype(BF16)
            z[k] = act
            if k == 3:
                vh0, vh1, _ = _layernorm_halves(z[2], z[3])
                vls = [(vh * lng_ref[b : b + 1, :] + lnb_ref[b : b + 1, :]).astype(BF16)
                       for b, vh in enumerate((vh0, vh1))]
        wtril = _causal_weights(ws_ref)
        m = jnp.zeros((tm, d), F32)
        for b in range(2):
            _spatial_gate(sv_ref, wtril, vls[b], bt_ref, b, tm, gd)
            gated = (z[b] * sv_ref[...]).astype(BF16)
            m = m + _dot(gated, wout_ref[b])
        m_ref[...] = m
        o_ref[...] = xv + _rms(m, gpost_ref[...])

    return _call(
        name,
        body,
        grid=(t // tm,),
        in_specs=[_rows(tm, d)] + [_VM] * 8,
        out_specs=[_rows(tm, d), _blocks(4, tm, q), _blocks(4, tm, q), _rows(tm, d)],
        out_shape=[
            jax.ShapeDtypeStruct((t, d), F32),
            jax.ShapeDtypeStruct((4, t, q), BF16),
            jax.ShapeDtypeStruct((4, t, q), BF16),
            jax.ShapeDtypeStruct((t, d), F32),
        ],
        scratch_shapes=[pltpu.VMEM((tm, q), F32)],
        args=[x, gpre, gpost, w_in, ln_g, ln_b, w_s, b_t, w_out],
        exchanges=exchanges,
    )


def _mix_a_bwd_hidden(name, dy, m, gpost, act, slope, ln_g, ln_b, w_s, b_t, w_out, exchanges=()):
    t, d = dy.shape
    _, _, q = act.shape
    gd = 2 * q // A_GROUPS
    tm = _step_rows(t)
    n_chunks = tm // CHUNK

    def body(dy_ref, m_ref, gpost_ref, z_ref, slope_ref, lng_ref, lnb_ref, ws_ref, bt_ref, wout_ref,
             dz_ref, dwout_ref, dws_ref, dbacc_ref, dlng_ref, dlnb_ref, dgain_ref, sv_ref, dvl_ref):
        first = pl.program_id(0) == 0

        @pl.when(first)
        def _():
            for ref in (dwout_ref, dws_ref, dbacc_ref, dlng_ref, dlnb_ref, dgain_ref):
                ref[...] = jnp.zeros_like(ref)

        dm, dgain = _rms_bwd(dy_ref[...], m_ref[...], gpost_ref[...])
        dgain_ref[...] += dgain
        dmb = dm.astype(BF16)
        vhs = list(_layernorm_halves(z_ref[2].astype(F32), z_ref[3].astype(F32)))
        rstd = vhs.pop()
        vls = [(vh * lng_ref[b : b + 1, :] + lnb_ref[b : b + 1, :]).astype(BF16) for b, vh in enumerate(vhs)]
        wtril = _causal_weights(ws_ref)
        dvhs = []
        for b in range(2):
            u = z_ref[b].astype(F32)
            _spatial_gate(sv_ref, wtril, vls[b], bt_ref, b, tm, gd)
            sv = sv_ref[...]
            gated = (u * sv).astype(BF16)
            dgated = _dot_nt(dmb, wout_ref[b])
            dwout_ref[b] += _dot_tn(gated, dmb)
            dz_ref[b] = (dgated * sv * slope_ref[b].astype(F32)).astype(BF16)
            dsv = dgated * u
            folded = dsv[0:CHUNK, :]
            for c in range(1, n_chunks):
                folded = folded + dsv[c * CHUNK : (c + 1) * CHUNK, :]
            for gg in range(A_GROUPS // 2):
                g = b * (A_GROUPS // 2) + gg
                dbacc_ref[:, g : g + 1] += jnp.sum(folded[:, gg * gd : (gg + 1) * gd], axis=1, keepdims=True)
            dsvb = dsv.astype(BF16)
            for gg in range(A_GROUPS // 2):
                g = b * (A_GROUPS // 2) + gg
                for c in range(n_chunks):
                    rows = slice(c * CHUNK, (c + 1) * CHUNK)
                    cols = slice(gg * gd, (gg + 1) * gd)
                    blk = dsvb[rows, cols]
                    dvl_ref[rows, cols] = _dot_tn(wtril[g], blk)
                    dws_ref[g] += _dot_nt(blk, vls[b][rows, cols])
            dvl = dvl_ref[...]
            dlng_ref[b : b + 1, :] += jnp.sum(dvl * vhs[b], axis=0, keepdims=True)
            dlnb_ref[b : b + 1, :] += jnp.sum(dvl, axis=0, keepdims=True)
            dvhs.append(dvl * lng_ref[b : b + 1, :])
        width = 2.0 * q
        m1 = (jnp.sum(dvhs[0], axis=-1, keepdims=True) + jnp.sum(dvhs[1], axis=-1, keepdims=True)) / width
        m2 = (jnp.sum(dvhs[0] * vhs[0], axis=-1, keepdims=True)
              + jnp.sum(dvhs[1] * vhs[1], axis=-1, keepdims=True)) / width
        for b in range(2):
            dv = rstd * (dvhs[b] - m1 - vhs[b] * m2)
            dz_ref[2 + b] = (dv * slope_ref[2 + b].astype(F32)).astype(BF16)

        @pl.when(pl.program_id(0) == t // tm - 1)
        def _():
            row = lax.broadcasted_iota(jnp.int32, (CHUNK, CHUNK), 0)
            col = lax.broadcasted_iota(jnp.int32, (CHUNK, CHUNK), 1)
            for g in range(A_GROUPS):
                dws_ref[g] = jnp.where(row >= col, dws_ref[g], 0.0)

    return _call(
        name,
        body,
        grid=(t // tm,),
        in_specs=[_rows(tm, d), _rows(tm, d), _VM, _blocks(4, tm, q), _blocks(4, tm, q)] + [_VM] * 5,
        out_specs=[_blocks(4, tm, q)] + [_VM] * 6,
        out_shape=[
            jax.ShapeDtypeStruct((4, t, q), BF16),
            jax.ShapeDtypeStruct((2, q, d), F32),
            jax.ShapeDtypeStruct((A_GROUPS, CHUNK, CHUNK), F32),
            jax.ShapeDtypeStruct((CHUNK, A_GROUPS), F32),
            jax.ShapeDtypeStruct((2, q), F32),
            jax.ShapeDtypeStruct((2, q), F32),
            jax.ShapeDtypeStruct((1, d), F32),
        ],
        scratch_shapes=[pltpu.VMEM((tm, q), F32), pltpu.VMEM((tm, q), F32)],
        args=[dy, m, gpost, act, slope, ln_g, ln_b, w_s, b_t, w_out],
        exchanges=exchanges,
    )


def _window_counts(tm, win):
    pos = pl.program_id(0) * tm + lax.broadcasted_iota(jnp.int32, (tm, 1), 0)
    return jnp.minimum(pos + 1, win).astype(F32)


def _pooled(p, halo, tm, gd, inside=False):
    prev = halo if inside else jnp.where(pl.program_id(0) == 0, 0.0, halo)
    ext = jnp.concatenate([prev, p], axis=0)
    out = []
    for g, win in enumerate(B_WINDOWS):
        s = ext[:, g * gd : (g + 1) * gd]
        step = 1
        while step < win:
            s = s + pltpu.roll(s, step, 0)
            step *= 2
        total = s[HALO:, :]
        count = float(win) if inside else _window_counts(tm, win)
        out.append(total / count - p[:, g * gd : (g + 1) * gd])
    return out


def _halo_spec(t, tm, d, ahead):
    per = tm // HALO
    if ahead:
        return pl.BlockSpec((HALO, d), lambda i: (jnp.minimum((i + 1) * per, t // HALO - 1), 0))
    return pl.BlockSpec((HALO, d), lambda i: (jnp.maximum(i * per - 1, 0), 0))


def _mix_b_fwd(name, x, gpre, gpost, w_in, w_grp, scale, w_out, exchanges=()):
    t, d = x.shape
    gd = d // len(B_WINDOWS)
    tm = _step_rows(t)

    def body(x_ref, xh_ref, gpre_ref, gpost_ref, win_ref, wgrp_ref, scale_ref, wout_ref, o_ref, p_ref, m_ref):
        xv = x_ref[...]
        gain = gpre_ref[...]
        p = _dot(_rms(xv, gain).astype(BF16), win_ref[...])
        p_ref[...] = p
        halo = _dot(_rms(xh_ref[...], gain).astype(BF16), win_ref[...])
        pooled = _pooled(p, halo, tm, gd)
        mixed = jnp.concatenate([_dot(pg.astype(BF16), wgrp_ref[g]) for g, pg in enumerate(pooled)], axis=1)
        m = _dot((mixed * scale_ref[...]).astype(BF16), wout_ref[...])
        m_ref[...] = m
        o_ref[...] = xv + _rms(m, gpost_ref[...])

    return _call(
        name,
        body,
        grid=(t // tm,),
        in_specs=[_rows(tm, d), _halo_spec(t, tm, d, False), _VM, _VM, _VM, _VM, _VM, _VM],
        out_specs=[_rows(tm, d)] * 3,
        out_shape=[jax.ShapeDtypeStruct((t, d), F32)] * 3,
        args=[x, x, gpre, gpost, w_in, w_grp, scale, w_out],
        exchanges=exchanges,
    )


def _mix_b_bwd(name, dy, x, m, p, gpre, gpost, w_in, w_grp, scale, w_out, exchanges=()):
    t, d = dy.shape
    gd = d // len(B_WINDOWS)
    tm = _step_rows(t)
    n_steps = t // tm

    def body(dy_ref, x_ref, m_ref, p_ref, pprev_ref, dynext_ref, mnext_ref, pnext_ref,
             gpre_ref, gpost_ref, win_ref, wgrp_ref, scale_ref, wout_ref,
             dx_ref, dwout_ref, dwgrp_ref, dscale_ref, dpost_ref, dwin_ref, dpre_ref):
        @pl.when(pl.program_id(0) == 0)
        def _():
            for ref in (dwout_ref, dwgrp_ref, dscale_ref, dpost_ref, dwin_ref, dpre_ref):
                ref[...] = jnp.zeros_like(ref)

        scale = scale_ref[...]

        def tail_bwd(dy_rows, m_rows, pooled, counts, accumulate):
            dm, dgain = _rms_bwd(dy_rows, m_rows, gpost_ref[...])
            dmb = dm.astype(BF16)
            pooled = [pg.astype(BF16) for pg in pooled]
            mixed = jnp.concatenate([_dot(pg, wgrp_ref[g]) for g, pg in enumerate(pooled)], axis=1)
            dms = _dot_nt(dmb, wout_ref[...])
            dmixed = (dms * scale).astype(BF16)
            if accumulate:
                dpost_ref[...] += dgain
                dwout_ref[...] += _dot_tn((mixed * scale).astype(BF16), dmb)
                dscale_ref[...] += jnp.sum(dms * mixed, axis=0, keepdims=True)
            parts = []
            for g, win in enumerate(B_WINDOWS):
                dmg = dmixed[:, g * gd : (g + 1) * gd]
                if accumulate:
                    dwgrp_ref[g] += _dot_tn(pooled[g], dmg)
                parts.append(_dot_nt(dmg, wgrp_ref[g]) / counts(win))
            return jnp.concatenate(parts, axis=1)

        dyv = dy_ref[...]
        pv = p_ref[...]
        dq_blk = tail_bwd(dyv, m_ref[...], _pooled(pv, pprev_ref[...], tm, gd),
                          lambda win: _window_counts(tm, win), True)
        dq_next = tail_bwd(dynext_ref[...], mnext_ref[...],
                           _pooled(pnext_ref[...], pv[tm - HALO :, :], HALO, gd, inside=True),
                           float, False)
        dq_next = jnp.where(pl.program_id(0) == n_steps - 1, 0.0, dq_next)
        ext = jnp.concatenate([dq_blk, dq_next], axis=0)
        parts = []
        for g, win in enumerate(B_WINDOWS):
            cols = slice(g * gd, (g + 1) * gd)
            s = ext[:, cols]
            step = 1
            while step < win:
                s = s + pltpu.roll(s, tm + HALO - step, 0)
                step *= 2
            parts.append((s[:tm, :] - dq_blk[:, cols] * _window_counts(tm, win)).astype(BF16))
        dp = jnp.concatenate(parts, axis=1)
        xv = x_ref[...]
        gain = gpre_ref[...]
        hb = _rms(xv, gain).astype(BF16)
        dwin_ref[...] += _dot_tn(hb, dp)
        dx, dgain = _rms_bwd(_dot_nt(dp, win_ref[...]), xv, gain)
        dx_ref[...] = dyv + dx
        dpre_ref[...] += dgain

    before, after = _halo_spec(t, tm, d, False), _halo_spec(t, tm, d, True)
    return _call(
        name,
        body,
        grid=(n_steps,),
        in_specs=[_rows(tm, d)] * 4 + [before, after, after, after] + [_VM] * 6,
        out_specs=[_rows(tm, d)] + [_VM] * 6,
        out_shape=[
            jax.ShapeDtypeStruct((t, d), F32),
            jax.ShapeDtypeStruct((d, d), F32),
            jax.ShapeDtypeStruct((len(B_WINDOWS), gd, gd), F32),
            jax.ShapeDtypeStruct((1, d), F32),
            jax.ShapeDtypeStruct((1, d), F32),
            jax.ShapeDtypeStruct((d, d), F32),
            jax.ShapeDtypeStruct((1, d), F32),
        ],
        args=[dy, x, m, p, p, dy, m, p, gpre, gpost, w_in, w_grp, scale, w_out],
        exchanges=exchanges,
    )


def _cast_into_slots(name, place, w, dtype):
    n_layers, r, c = w.shape

    def body(place_ref, w_ref, *o_refs):
        del place_ref
        for j, o_ref in enumerate(o_refs):
            @pl.when(pl.program_id(0) == j)
            def _():
                o_ref[...] = w_ref[...].astype(dtype)

    return pl.pallas_call(
        body,
        name=name,
        grid_spec=pltpu.PrefetchScalarGridSpec(
            num_scalar_prefetch=1,
            grid=(n_layers,),
            in_specs=[pl.BlockSpec((1, r, c), lambda i, place_ref: (i, 0, 0))],
            out_specs=[pl.BlockSpec((1, r, c), lambda i, place_ref: (place_ref[0], 0, 0))] * n_layers,
        ),
        out_shape=[jax.ShapeDtypeStruct((N_CHIPS, r, c), dtype)] * n_layers,
        compiler_params=_params(),
    )(place, w)


def _row_tile(r):
    return 256 if r % 256 == 0 else r


def _pair_sum(name, place, dw, recv):
    _, r, c = dw.shape
    half = r // 2
    tr = _row_tile(half)
    per = half // tr

    def body(place_ref, a_ref, b_ref, o_ref):
        del place_ref
        o_ref[...] = (a_ref[...] + b_ref[...]).astype(BF16)

    return pl.pallas_call(
        body,
        name=name,
        grid_spec=pltpu.PrefetchScalarGridSpec(
            num_scalar_prefetch=1,
            grid=(N_CHIPS, per),
            in_specs=[
                pl.BlockSpec((1, tr, c), lambda k, i, place_ref: (k, place_ref[1] * per + i, 0)),
                pl.BlockSpec((1, tr, c), lambda k, i, place_ref: (k, i, 0)),
            ],
            out_specs=pl.BlockSpec((1, tr, c), lambda k, i, place_ref: (k, i, 0)),
        ),
        out_shape=jax.ShapeDtypeStruct(recv.shape, BF16),
        compiler_params=pltpu.CompilerParams(
            dimension_semantics=("arbitrary",) * 2, vmem_limit_bytes=VMEM_LIMIT_BYTES),
    )(place, dw, recv)


def _chip_sum(name, place, mine, others, exchanges=()):
    n_layers = len(mine)
    _, half, c = mine[0].shape
    tr = _row_tile(half)
    per = half // tr

    def body(place_ref, *refs):
        del place_ref
        o_ref = refs[-1]
        for j in range(n_layers):
            @pl.when(pl.program_id(0) == j)
            def _():
                parts = refs[4 * j : 4 * j + 4]
                acc = parts[0][...].astype(F32) + parts[1][...].astype(F32)
                acc = acc + parts[2][...].astype(F32)
                o_ref[...] = acc + parts[3][...].astype(F32)

    def part(j, flip):
        return pl.BlockSpec((1, tr, c), lambda l, i, place_ref: (
            jnp.bitwise_xor(place_ref[0], flip), jnp.where(l == j, i, 0), 0))

    args = []
    for j in range(n_layers):
        args += [mine[j], others[j], others[j], others[j]]
    (total,), ex_outs = _call(
        name,
        body,
        grid=(n_layers, per),
        in_specs=[part(j, flip) for j in range(n_layers) for flip in range(N_CHIPS)],
        out_specs=[pl.BlockSpec((1, tr, c), lambda l, i, place_ref: (l, place_ref[1] * per + i, 0))],
        out_shape=[jax.ShapeDtypeStruct((n_layers, 2 * half, c), F32)],
        args=args,
        prefetch=[place],
        exchanges=exchanges,
    )
    return total, ex_outs


def _adamw(name, w, g, m, v, exchanges=()):
    n_layers, r, c = w.shape
    tr = _row_tile(r)

    def body(w_ref, g_ref, m_ref, v_ref, go_ref, d_ref, nm_ref, nv_ref):
        gv = g_ref[...]
        go_ref[...] = gv
        nm = ADAM_B1 * m_ref[...] + (1.0 - ADAM_B1) * gv
        nv = ADAM_B2 * v_ref[...] + (1.0 - ADAM_B2) * jnp.square(gv)
        m_hat = nm / (1.0 - ADAM_B1 ** ADAM_STEP)
        v_hat = nv / (1.0 - ADAM_B2 ** ADAM_STEP)
        d_ref[...] = -ADAM_LR * (m_hat / (jnp.sqrt(v_hat) + ADAM_EPS) + ADAM_WD * w_ref[...])
        nm_ref[...] = nm
        nv_ref[...] = nv

    spec = pl.BlockSpec((1, tr, c), lambda l, i: (l, i, 0))
    return _call(
        name,
        body,
        grid=(n_layers, r // tr),
        in_specs=[spec] * 4,
        out_specs=[spec] * 4,
        out_shape=[jax.ShapeDtypeStruct(w.shape, F32)] * 4,
        args=[w, g, m, v],
        exchanges=exchanges,
    )


class _GatherSmall:
    def __init__(self, gathered):
        self.inputs = [gathered]
        self.out_shapes = [jax.ShapeDtypeStruct(gathered.shape, gathered.dtype)]
        self.aliases = {0: 0}
        self.n_sems = 4

    def _copies(self, outs, send, recv, outbound):
        x, y, c = _position()
        peers = [(x, y, 1 - c)] + [(px, py, c) for px, py in _other_chips(x, y)]
        copies = []
        for k, (px, py, pc) in enumerate(peers):
            rows = outs[0].at[4 * x + 2 * y + c if outbound else 4 * px + 2 * py + pc]
            copies.append(pltpu.make_async_remote_copy(
                src_ref=rows, dst_ref=rows, send_sem=send.at[k], recv_sem=recv.at[k],
                device_id=(px, py, pc), device_id_type=MESH))
        return copies

    def start(self, ins, outs, send, recv):
        for cp in self._copies(outs, send, recv, True):
            cp.start()

    def finish(self, ins, outs, send, recv):
        for cp in self._copies(outs, send, recv, False):
            cp.wait_recv()
        for cp in self._copies(outs, send, recv, True):
            cp.wait_send()


class _GatherSmallForward:
    def __init__(self, gathered):
        self.inputs = [gathered]
        self.out_shapes = [jax.ShapeDtypeStruct(gathered.shape, gathered.dtype)]
        self.aliases = {0: 0}
        self.n_sems = 3

    def _copies(self, outs, send, recv, core):
        x, y, c = _position()
        copies = []
        for k, (px, py) in enumerate(_other_chips(x, y)):
            rows = outs[0].at[4 * px + 2 * py + core]
            copies.append(pltpu.make_async_remote_copy(
                src_ref=rows, dst_ref=rows, send_sem=send.at[k], recv_sem=recv.at[k],
                device_id=(x, y, 1 - c), device_id_type=MESH))
        return copies

    def start(self, ins, outs, send, recv):
        for cp in self._copies(outs, send, recv, lax.axis_index("c")):
            cp.start()

    def finish(self, ins, outs, send, recv):
        c = lax.axis_index("c")
        for cp in self._copies(outs, send, recv, 1 - c):
            cp.wait_recv()
        for cp in self._copies(outs, send, recv, c):
            cp.wait_send()


def _sum_devices(name, gathered):
    _, m_per, n = gathered.shape

    def body(all_ref, sum_ref):
        acc = all_ref[0]
        for k in range(1, N_DEV):
            acc = acc + all_ref[k]
        sum_ref[...] = acc

    return pl.pallas_call(
        body,
        name=name,
        in_specs=[_VM],
        out_specs=_VM,
        out_shape=jax.ShapeDtypeStruct((m_per, n), F32),
        compiler_params=pltpu.CompilerParams(vmem_limit_bytes=VMEM_LIMIT_BYTES),
    )(gathered)


SHARDED = ("a_w_in", "a_w_out", "b_w_in", "b_w_grp", "b_scale", "b_w_out", "ffn_w_gate", "ffn_w_up", "ffn_w_down")
SMALL = ("a_ln_g", "a_ln_b", "a_w_s", "a_b_s", "mix_pre_g", "mix_post_g", "ffn_pre_g", "ffn_post_g")
WEIGHTS = ("a_w_in", "a_ln_g", "a_ln_b", "a_w_s", "a_b_s", "a_w_out", "b_w_in", "b_w_grp", "b_scale", "b_w_out",
           "mix_pre_g", "mix_post_g", "ffn_pre_g", "ffn_post_g", "ffn_w_gate", "ffn_w_up", "ffn_w_down")


TRANSPOSED = ("ffn_w_gate", "ffn_w_up")


def _as_layers(name, a):
    if name in TRANSPOSED:
        return jnp.swapaxes(a, 1, 2)
    if a.ndim == 2:
        return a.reshape(a.shape[0], 1, a.shape[1])
    return a.reshape(a.shape[0], -1, a.shape[-1])


def _from_layers(name, a, shape):
    if name in TRANSPOSED:
        return jnp.swapaxes(a, 1, 2)
    return a.reshape(shape)


def _pack_small(parts):
    return jnp.concatenate([p.reshape(-1, 128) for p in parts], axis=0)


def _unpack_small(packed, like):
    out, row = [], 0
    for ref in like:
        rows = ref.size // 128
        out.append(packed[row : row + rows].reshape(ref.shape))
        row += rows
    return out


def kernel(x, a_w_in, a_ln_g, a_ln_b, a_w_s, a_b_s, a_w_out, b_w_in, b_w_grp, b_scale, b_w_out, mix_pre_g, mix_post_g, ffn_pre_g, ffn_post_g, ffn_w_gate, ffn_w_up, ffn_w_down, loss_target, m_a_w_in, m_a_ln_g, m_a_ln_b, m_a_w_s, m_a_b_s, m_a_w_out, m_b_w_in, m_b_w_grp, m_b_scale, m_b_w_out, m_mix_pre_g, m_mix_post_g, m_ffn_pre_g, m_ffn_post_g, m_ffn_w_gate, m_ffn_w_up, m_ffn_w_down, v_a_w_in, v_a_ln_g, v_a_ln_b, v_a_w_s, v_a_b_s, v_a_w_out, v_b_w_in, v_b_w_grp, v_b_scale, v_b_w_out, v_mix_pre_g, v_mix_post_g, v_ffn_pre_g, v_ffn_post_g, v_ffn_w_gate, v_ffn_w_up, v_ffn_w_down):
    weights = dict(a_w_in=a_w_in, a_ln_g=a_ln_g, a_ln_b=a_ln_b, a_w_s=a_w_s, a_b_s=a_b_s, a_w_out=a_w_out,
                   b_w_in=b_w_in, b_w_grp=b_w_grp, b_scale=b_scale, b_w_out=b_w_out, mix_pre_g=mix_pre_g,
                   mix_post_g=mix_post_g, ffn_pre_g=ffn_pre_g, ffn_post_g=ffn_post_g, ffn_w_gate=ffn_w_gate,
                   ffn_w_up=ffn_w_up, ffn_w_down=ffn_w_down)
    mom1 = dict(a_w_in=m_a_w_in, a_ln_g=m_a_ln_g, a_ln_b=m_a_ln_b, a_w_s=m_a_w_s, a_b_s=m_a_b_s, a_w_out=m_a_w_out,
                b_w_in=m_b_w_in, b_w_grp=m_b_w_grp, b_scale=m_b_scale, b_w_out=m_b_w_out, mix_pre_g=m_mix_pre_g,
                mix_post_g=m_mix_post_g, ffn_pre_g=m_ffn_pre_g, ffn_post_g=m_ffn_post_g, ffn_w_gate=m_ffn_w_gate,
                ffn_w_up=m_ffn_w_up, ffn_w_down=m_ffn_w_down)
    mom2 = dict(a_w_in=v_a_w_in, a_ln_g=v_a_ln_g, a_ln_b=v_a_ln_b, a_w_s=v_a_w_s, a_b_s=v_a_b_s, a_w_out=v_a_w_out,
                b_w_in=v_b_w_in, b_w_grp=v_b_w_grp, b_scale=v_b_scale, b_w_out=v_b_w_out, mix_pre_g=v_mix_pre_g,
                mix_post_g=v_mix_post_g, ffn_pre_g=v_ffn_pre_g, ffn_post_g=v_ffn_post_g, ffn_w_gate=v_ffn_w_gate,
                ffn_w_up=v_ffn_w_up, ffn_w_down=v_ffn_w_down)

    t, d = x.shape[1], x.shape[2]
    depth = mix_pre_g.shape[0]
    gd_b = d // len(B_WINDOWS)
    xs = x.reshape(t, d)
    target = loss_target.reshape(t, d)

    chip = 2 * lax.axis_index("x") + lax.axis_index("y")
    place = jnp.stack([chip, lax.axis_index("c")]).astype(jnp.int32)
    bufs = {name: list(_cast_into_slots("cast_" + name, place, _as_layers(name, weights[name]),
                                        F32 if name == "b_scale" else BF16)) for name in SHARDED}

    def gain(name, i):
        return weights[name][i].reshape(1, d)

    def weight_keys(i):
        j = i // 2
        mixer = [("a_w_in", j), ("a_w_out", j)] if i % 2 == 0 else [("b_w_in", j), ("b_w_grp", j), ("b_w_out", j)]
        return mixer, [("ffn_w_gate", i), ("ffn_w_up", i), ("ffn_w_down", i)]

    def gather(keys):
        return _GatherWeights([bufs[n][j] for n, j in keys],
                              whole=[k for k, (n, _) in enumerate(keys) if n == "b_scale"])

    def gathered(keys, outs):
        for (n, j), buf in zip(keys, outs):
            bufs[n][j] = buf

    first = weight_keys(0)[0] + [("b_scale", j) for j in range(b_scale.shape[0])]
    gathered(first, _exchange("gather_first", [gather(first)])[0])
    saved = []
    cur = xs
    for i in range(depth):
        j = i // 2
        mixer_next, ffn_next = weight_keys(i + 1) if i + 1 < depth else ([], [])
        ffn_keys = weight_keys(i)[1]
        after_ffn = mixer_next if (i + 1) % 2 == 0 else mixer_next + ffn_next
        if i % 2 == 0:
            w_in = bufs["a_w_in"][j]
            q = w_in.shape[2]
            w_out = bufs["a_w_out"][j].reshape(2, q, d)
            ln_g = a_ln_g[j].reshape(2, q)
            ln_b = a_ln_b[j].reshape(2, q)
            b_t = jnp.transpose(a_b_s[j])
            (nxt, act, slope, m), (got,) = _mix_a_fwd(
                f"mix_a_fwd{j}", cur, gain("mix_pre_g", i), gain("mix_post_g", i), w_in, ln_g, ln_b, a_w_s[j], b_t,
                w_out, exchanges=[gather(ffn_keys)])
            gathered(ffn_keys, got)
            mix_saved = dict(x=cur, act=act, slope=slope, m=m, w_in=w_in, w_out=w_out, ln_g=ln_g, ln_b=ln_b, b_t=b_t)
        else:
            w_in = bufs["b_w_in"][j].reshape(d, d)
            w_out = bufs["b_w_out"][j].reshape(d, d)
            w_grp = jnp.transpose(bufs["b_w_grp"][j].reshape(N_CHIPS, len(B_WINDOWS), gd_b // N_CHIPS, gd_b),
                                  (1, 0, 2, 3)).reshape(len(B_WINDOWS), gd_b, gd_b)
            scale = bufs["b_scale"][j].reshape(1, d)
            (nxt, p, m), _ = _mix_b_fwd(f"mix_b_fwd{j}", cur, gain("mix_pre_g", i), gain("mix_post_g", i),
                                        w_in, w_grp, scale, w_out)
            mix_saved = dict(x=cur, p=p, m=m, w_in=w_in, w_out=w_out, w_grp=w_grp, scale=scale)
        cur = nxt
        wg, wu, wd = (bufs[n][k].reshape(1, -1, d) for n, k in ffn_keys)
        (nxt, a, dup, dgate, f, *sq), (got,) = _ffn_fwd(
            f"ffn_fwd{i}", cur, gain("ffn_pre_g", i), gain("ffn_post_g", i), wg, wu, wd,
            target=target if i == depth - 1 else None, exchanges=[gather(after_ffn)])
        gathered(after_ffn, got)
        saved.append((mix_saved, dict(x=cur, a=a, dup=dup, dgate=dgate, f=f, wg=wg, wu=wu, wd=wd)))
        cur = nxt

    dcur = cur
    loss_part = 0.5 * sq[0] / d

    grads = {name: [None] * weights[name].shape[0] for name in WEIGHTS}
    state = dict(to_sibling=[], to_chips=[])
    pair, from_chips = {}, {}

    def exchanges_due(carry=True):
        if not carry:
            return []
        return [_ToSibling([a for _, _, a in state["to_sibling"]]), _ToChips([a for _, _, a in state["to_chips"]])]

    def exchanged(outs):
        if not outs:
            return
        from_sibling, arrived = outs
        for (n, k, _), got in zip(state["to_chips"], arrived):
            from_chips[n, k] = got
        state["to_chips"] = []
        for (n, k, dw), got in zip(state["to_sibling"], from_sibling):
            pair[n, k] = _pair_sum(f"pair_sum_{n}{k}", place, dw, got)
            state["to_chips"].append((n, k, pair[n, k]))
        state["to_sibling"] = []

    def made(name, k, dw):
        grads[name][k] = dw
        state["to_sibling"].append((name, k, dw))

    def own_slot(packed):
        return lax.dynamic_update_slice(jnp.zeros((N_DEV,) + packed.shape, F32), packed[None],
                                        (2 * chip + place[1], 0, 0))

    for i in reversed(range(depth)):
        j = i // 2
        mix_saved, ffn_saved = saved[i]
        s = ffn_saved
        both = i == 0
        (dg, du, dwd, dgain), outs = _ffn_bwd_hidden(
            f"ffn_bwd_hidden{i}", dcur, s["f"], gain("ffn_post_g", i), s["a"], s["dup"], s["dgate"], s["wd"],
            exchanges=exchanges_due(both))
        exchanged(outs)
        grads["ffn_post_g"][i] = dgain
        made("ffn_w_down", i, dwd.reshape(N_CHIPS, -1, d))
        (dcur, dwg, dwu, dgain), outs = _bwd_in(
            f"ffn_bwd_in{i}", dcur, s["x"], gain("ffn_pre_g", i), [dg, du], [s["wg"], s["wu"]], transposed=True,
            exchanges=exchanges_due())
        exchanged(outs)
        grads["ffn_pre_g"][i] = dgain
        made("ffn_w_gate", i, dwg.reshape(N_CHIPS, -1, d))
        made("ffn_w_up", i, dwu.reshape(N_CHIPS, -1, d))
        s = mix_saved
        if i % 2 == 0:
            (dz, dwout, dws, dbacc, dlng, dlnb, dgain), outs = _mix_a_bwd_hidden(
                f"mix_a_bwd_hidden{j}", dcur, s["m"], gain("mix_post_g", i), s["act"], s["slope"], s["ln_g"], s["ln_b"],
                a_w_s[j], s["b_t"], s["w_out"], exchanges=exchanges_due(both))
            exchanged(outs)
            grads["a_w_s"][j] = dws
            grads["a_b_s"][j] = jnp.transpose(dbacc)
            grads["a_ln_g"][j] = dlng.reshape(-1)
            grads["a_ln_b"][j] = dlnb.reshape(-1)
            grads["mix_post_g"][i] = dgain
            made("a_w_out", j, dwout.reshape(N_CHIPS, -1, d))
            riders = []
            if i == 0:
                grads["mix_pre_g"][0] = jnp.zeros((1, d), F32)
                small_grads = [jnp.stack([g.reshape(weights[name].shape[1:]) for g in grads[name]], axis=0)
                               for name in SMALL]
                scale_grad = jnp.concatenate(grads["b_scale"], axis=0)
                riders = [_GatherSmall(own_slot(_pack_small(small_grads + [scale_grad, loss_part])))]
            (dcur, dwin, dgain), outs = _bwd_in(
                f"mix_a_bwd_in{j}", dcur, s["x"], gain("mix_pre_g", i), [dz], [s["w_in"]],
                exchanges=exchanges_due() + riders)
            exchanged(outs[:2])
            gathered_small = outs[2][0] if riders else None
            grads["mix_pre_g"][i] = dgain
            made("a_w_in", j, dwin)
        else:
            (dcur, dwout, dwgrp, dscale, dpost, dwin, dpre), outs = _mix_b_bwd(
                f"mix_b_bwd{j}", dcur, s["x"], s["m"], s["p"], gain("mix_pre_g", i), gain("mix_post_g", i),
                s["w_in"], s["w_grp"], s["scale"], s["w_out"], exchanges=exchanges_due())
            exchanged(outs)
            grads["b_scale"][j] = dscale
            grads["mix_post_g"][i] = dpost
            grads["mix_pre_g"][i] = dpre
            made("b_w_out", j, dwout.reshape(N_CHIPS, -1, d))
            made("b_w_grp", j, jnp.transpose(
                dwgrp.reshape(len(B_WINDOWS), N_CHIPS, gd_b // N_CHIPS, gd_b), (1, 0, 2, 3)).reshape(N_CHIPS, -1, gd_b))
            made("b_w_in", j, dwin.reshape(N_CHIPS, -1, d))
    grad_x = dcur.reshape(x.shape)

    outs = _exchange("grads_last", exchanges_due() + [_GatherSmallForward(gathered_small),
                                                      _GatherSmall(own_slot(_pack_small([grads["mix_pre_g"][0]])))])
    exchanged(outs[:2])
    gathered_small = outs[2][0]
    outs = _exchange("grads_last_to_chips", exchanges_due() + [_GatherSmallForward(outs[3][0])])
    exchanged(outs[:2])
    gathered_last = outs[2][0]

    reduced_names = [name for name in SHARDED if name != "b_scale"]
    sums = []
    for name in reduced_names:
        layers = range(weights[name].shape[0])
        sums.append(_chip_sum("chip_sum_" + name, place, [pair[name, k] for k in layers],
                              [from_chips[name, k] for k in layers])[0])
    reduced = dict(zip(reduced_names, _exchange("swap_halves", [_SwapHalves(sums)])[0]))

    out_g, out_d, out_m, out_v = {}, {}, {}, {}
    for name in reduced_names:
        shape = weights[name].shape
        results, _ = _adamw("adamw_" + name, _as_layers(name, weights[name]), reduced[name],
                            _as_layers(name, mom1[name]), _as_layers(name, mom2[name]))
        out_g[name], out_d[name], out_m[name], out_v[name] = (_from_layers(name, a, shape) for a in results)

    summed = _sum_devices("sum_devices", gathered_small)
    last_rows = sum(weights[name].size // 128 for name in SMALL[: SMALL.index("mix_pre_g")])
    summed = lax.dynamic_update_slice(summed, _sum_devices("sum_devices_last", gathered_last), (last_rows, 0))
    loss_row = summed.shape[0] - loss_part.shape[0]
    loss = summed[loss_row, 0]
    small_rows = loss_row - scale_grad.size // 128
    scale_sum = summed[small_rows:loss_row].reshape(scale_grad.shape)
    scale_mine = lax.dynamic_slice_in_dim(scale_sum, chip * b_scale.shape[1], b_scale.shape[1], axis=1)[:, None, :]
    summed = summed[:small_rows]
    results, _ = _adamw("adamw_b_scale", _as_layers("b_scale", b_scale), scale_mine,
                        _as_layers("b_scale", m_b_scale), _as_layers("b_scale", v_b_scale))
    out_g["b_scale"], out_d["b_scale"], out_m["b_scale"], out_v["b_scale"] = (a.reshape(b_scale.shape) for a in results)
    small_like = [weights[name] for name in SMALL]
    packs = [_pack_small([src[name] for name in SMALL]).reshape(1, -1, 128) for src in (weights, mom1, mom2)]
    results, _ = _adamw("adamw_small", packs[0], summed.reshape(1, -1, 128), packs[1], packs[2])
    for dst, packed in zip((out_g, out_d, out_m, out_v), (a[0] for a in results)):
        for name, val in zip(SMALL, _unpack_small(packed, small_like)):
            dst[name] = val

    return (loss, grad_x, *[out_g[n] for n in WEIGHTS], *[out_d[n] for n in WEIGHTS],
            *[out_m[n] for n in WEIGHTS], *[out_v[n] for n in WEIGHTS])
```

```python
import functools
import math

import jax
import jax.numpy as jnp
from jax import lax
from jax.experimental import pallas as pl
from jax.experimental.pallas import tpu as pltpu

F32 = jnp.float32
BF16 = jnp.bfloat16
MESH = pl.DeviceIdType.MESH

EPS = 1e-6
CHUNK = 128
A_GROUPS = 8
B_WINDOWS = (2, 4, 8, 16)
HALO = 16
N_CHIPS = 4
N_DEV = 8

ADAM_LR = 0.001
ADAM_B1 = 0.9
ADAM_B2 = 0.999
ADAM_EPS = 1e-08
ADAM_WD = 0.01
ADAM_STEP = 10

VMEM_LIMIT_BYTES = 60 * 1024 * 1024
INV_SQRT2 = 1.0 / math.sqrt(2.0)
INV_SQRT_2PI = 1.0 / math.sqrt(2.0 * math.pi)

_ANY = pl.BlockSpec(memory_space=pl.ANY)
_VM = pl.BlockSpec(memory_space=pltpu.VMEM)


def _params():
    return pltpu.CompilerParams(dimension_semantics=("arbitrary",), vmem_limit_bytes=VMEM_LIMIT_BYTES)


def _token_block(t):
    return 256 if t >= 1024 else 128


def _step_rows(t):
    return 2 * _token_block(t)


def _rows(tm, d):
    return pl.BlockSpec((tm, d), lambda i: (i, 0))


def _blocks(nb, tm, bw):
    return pl.BlockSpec((nb, tm, bw), lambda i: (0, i, 0))


def _dot(a, b):
    return lax.dot_general(a, b, (((1,), (0,)), ((), ())), preferred_element_type=F32)


def _dot_nt(a, b):
    return lax.dot_general(a, b, (((1,), (1,)), ((), ())), preferred_element_type=F32)


def _dot_tn(a, b):
    return lax.dot_general(a, b, (((0,), (0,)), ((), ())), preferred_element_type=F32)


def _rms(x, g):
    return x * lax.rsqrt(jnp.mean(x * x, axis=-1, keepdims=True) + EPS) * g


def _rms_bwd(dy, x, g):
    r = lax.rsqrt(jnp.mean(x * x, axis=-1, keepdims=True) + EPS)
    n = x * r
    dn = dy * g
    dx = r * (dn - n * jnp.mean(dn * n, axis=-1, keepdims=True))
    return dx, jnp.sum(dy * n, axis=0, keepdims=True)


def _gelu_and_grad(x):
    cdf = 0.5 * (1.0 + lax.erf(x * INV_SQRT2))
    return x * cdf, cdf + x * (jnp.exp(-0.5 * x * x) * INV_SQRT_2PI)


def _position():
    return lax.axis_index("x"), lax.axis_index("y"), lax.axis_index("c")


def _other_chips(x, y):
    return [(1 - x, y), (x, 1 - y), (1 - x, 1 - y)]


class _GatherWeights:
    def __init__(self, bufs, whole=()):
        self.inputs = list(bufs)
        self.out_shapes = [jax.ShapeDtypeStruct(b.shape, b.dtype) for b in bufs]
        self.aliases = {w: w for w in range(len(bufs))}
        self.n_sems = 6 * len(bufs)
        self.whole = frozenset(whole)

    def _part(self, outs, w, slot, core):
        if w in self.whole:
            return outs[w].at[slot]
        half = outs[w].shape[1] // 2
        return outs[w].at[slot, pl.ds(core * half, half)]

    def _ici(self, outs, send, recv, w, j, slot):
        x, y, c = _position()
        px, py = _other_chips(x, y)[j]
        part = self._part(outs, w, slot, c)
        return pltpu.make_async_remote_copy(
            src_ref=part, dst_ref=part, send_sem=send.at[6 * w + j], recv_sem=recv.at[6 * w + j],
            device_id=(px, py, c), device_id_type=MESH)

    def _d2d(self, outs, send, recv, w, j, slot, core):
        x, y, c = _position()
        part = self._part(outs, w, slot, core)
        return pltpu.make_async_remote_copy(
            src_ref=part, dst_ref=part, send_sem=send.at[6 * w + 3 + j], recv_sem=recv.at[6 * w + 3 + j],
            device_id=(x, y, 1 - c), device_id_type=MESH)

    def start(self, ins, outs, send, recv):
        x, y, _ = _position()
        for w in range(len(outs)):
            for j in range(3):
                self._ici(outs, send, recv, w, j, 2 * x + y).start()

    def advance(self, ins, outs, send, recv):
        x, y, c = _position()
        slots = [2 * px + py for px, py in _other_chips(x, y)]
        for w in range(len(outs)):
            for j, slot in enumerate(slots):
                self._ici(outs, send, recv, w, j, slot).wait_recv()
                if w not in self.whole:
                    self._d2d(outs, send, recv, w, j, slot, c).start()

    def finish(self, ins, outs, send, recv):
        x, y, c = _position()
        slots = [2 * px + py for px, py in _other_chips(x, y)]
        for w in range(len(outs)):
            for j, slot in enumerate(slots):
                if w not in self.whole:
                    self._d2d(outs, send, recv, w, j, slot, 1 - c).wait_recv()
        for w in range(len(outs)):
            for j, slot in enumerate(slots):
                self._ici(outs, send, recv, w, j, 2 * x + y).wait_send()
                if w not in self.whole:
                    self._d2d(outs, send, recv, w, j, slot, c).wait_send()


class _ToSibling:
    def __init__(self, grads):
        self.inputs = list(grads)
        self.out_shapes = [jax.ShapeDtypeStruct((g.shape[0], g.shape[1] // 2, g.shape[2]), g.dtype) for g in grads]
        self.aliases = {}
        self.n_sems = len(grads)

    def _copy(self, ins, outs, send, recv, w):
        x, y, c = _position()
        half = ins[w].shape[1] // 2
        return pltpu.make_async_remote_copy(
            src_ref=ins[w].at[:, pl.ds((1 - c) * half, half)], dst_ref=outs[w],
            send_sem=send.at[w], recv_sem=recv.at[w], device_id=(x, y, 1 - c), device_id_type=MESH)

    def start(self, ins, outs, send, recv):
        for w in range(len(ins)):
            self._copy(ins, outs, send, recv, w).start()

    def finish(self, ins, outs, send, recv):
        for w in range(len(ins)):
            self._copy(ins, outs, send, recv, w).wait_recv()
        for w in range(len(ins)):
            self._copy(ins, outs, send, recv, w).wait_send()


class _ToChips:
    def __init__(self, parts):
        self.inputs = list(parts)
        self.out_shapes = [jax.ShapeDtypeStruct(p.shape, p.dtype) for p in parts]
        self.aliases = {}
        self.n_sems = 3 * len(parts)

    def _copy(self, ins, outs, send, recv, w, j, outbound):
        x, y, c = _position()
        px, py = _other_chips(x, y)[j]
        me, peer = 2 * x + y, 2 * px + py
        src_slot, dst_slot = (peer, me) if outbound else (me, peer)
        return pltpu.make_async_remote_copy(
            src_ref=ins[w].at[src_slot], dst_ref=outs[w].at[dst_slot],
            send_sem=send.at[3 * w + j], recv_sem=recv.at[3 * w + j], device_id=(px, py, c), device_id_type=MESH)

    def start(self, ins, outs, send, recv):
        for w in range(len(ins)):
            for j in range(3):
                self._copy(ins, outs, send, recv, w, j, True).start()

    def finish(self, ins, outs, send, recv):
        for w in range(len(ins)):
            for j in range(3):
                self._copy(ins, outs, send, recv, w, j, False).wait_recv()
        for w in range(len(ins)):
            for j in range(3):
                self._copy(ins, outs, send, recv, w, j, True).wait_send()


class _SwapHalves:
    def __init__(self, bufs):
        self.inputs = list(bufs)
        self.out_shapes = [jax.ShapeDtypeStruct(b.shape, b.dtype) for b in bufs]
        self.aliases = {w: w for w in range(len(bufs))}
        self.n_sems = len(bufs)

    def _copy(self, outs, send, recv, w, core):
        x, y, c = _position()
        half = outs[w].shape[1] // 2
        rows = outs[w].at[:, pl.ds(core * half, half)]
        return pltpu.make_async_remote_copy(
            src_ref=rows, dst_ref=rows, send_sem=send.at[w], recv_sem=recv.at[w],
            device_id=(x, y, 1 - c), device_id_type=MESH)

    def start(self, ins, outs, send, recv):
        c = lax.axis_index("c")
        for w in range(len(outs)):
            self._copy(outs, send, recv, w, c).start()

    def finish(self, ins, outs, send, recv):
        c = lax.axis_index("c")
        for w in range(len(outs)):
            self._copy(outs, send, recv, w, 1 - c).wait_recv()
        for w in range(len(outs)):
            self._copy(outs, send, recv, w, c).wait_send()


def _call(name, body, *, grid, in_specs, out_specs, out_shape, args, scratch_shapes=(), prefetch=(), exchanges=()):
    given = list(exchanges)
    exchanges = [e for e in given if e.inputs]
    n_pre, n_in, n_out, n_scr = len(prefetch), len(args), len(out_shape), len(scratch_shapes)
    ex_in = [a for e in exchanges for a in e.inputs]
    ex_out = [s for e in exchanges for s in e.out_shapes]
    aliases = {}
    at_in, at_out = n_pre + n_in, n_out
    for e in exchanges:
        for i, o in e.aliases.items():
            aliases[at_in + i] = at_out + o
        at_in += len(e.inputs)
        at_out += len(e.out_shapes)

    def at_step(last):
        hit = None
        for axis, n in enumerate(grid):
            here = pl.program_id(axis) == (n - 1 if last else 0)
            hit = here if hit is None else jnp.logical_and(hit, here)
        return hit

    def fused(*refs):
        pre, refs = refs[:n_pre], refs[n_pre:]
        body_in, refs = refs[:n_in], refs[n_in:]
        ex_in_refs, refs = refs[: len(ex_in)], refs[len(ex_in) :]
        body_out, refs = refs[:n_out], refs[n_out:]
        ex_out_refs, refs = refs[: len(ex_out)], refs[len(ex_out) :]
        body_scr, sems = refs[:n_scr], refs[n_scr:]

        def each(stage):
            a = b = 0
            for n, e in enumerate(exchanges):
                ins, outs = ex_in_refs[a : a + len(e.inputs)], ex_out_refs[b : b + len(e.out_shapes)]
                if hasattr(e, stage):
                    getattr(e, stage)(ins, outs, sems[2 * n], sems[2 * n + 1])
                a += len(e.inputs)
                b += len(e.out_shapes)

        if exchanges:
            @pl.when(at_step(False))
            def _():
                each("start")

        if body is not None:
            body(*pre, *body_in, *body_out, *body_scr)

        if exchanges:
            late = pl.program_id(0) == (3 * grid[0]) // 4 if len(grid) == 1 else at_step(True)

            @pl.when(late)
            def _():
                each("advance")

            @pl.when(at_step(True))
            def _():
                each("finish")

    outs = pl.pallas_call(
        fused,
        name=name,
        grid_spec=pltpu.PrefetchScalarGridSpec(
            num_scalar_prefetch=n_pre,
            grid=grid,
            in_specs=list(in_specs) + [_ANY] * len(ex_in),
            out_specs=list(out_specs) + [_ANY] * len(ex_out),
            scratch_shapes=list(scratch_shapes)
            + [pltpu.SemaphoreType.DMA((e.n_sems,)) for e in exchanges for _ in range(2)],
        ),
        out_shape=list(out_shape) + ex_out,
        input_output_aliases=aliases,
        compiler_params=pltpu.CompilerParams(
            dimension_semantics=("arbitrary",) * len(grid), vmem_limit_bytes=VMEM_LIMIT_BYTES),
    )(*prefetch, *args, *ex_in)
    body_outs, rest = list(outs[:n_out]), list(outs[n_out:])
    ex_outs = []
    for e in given:
        n_e = len(e.out_shapes) if e.inputs else 0
        ex_outs.append(rest[:n_e])
        rest = rest[n_e:]
    return body_outs, ex_outs


def _exchange(name, exchanges):
    return _call(name, None, grid=(1,), in_specs=[], out_specs=[], out_shape=[], args=[], exchanges=exchanges)[1]


def _ffn_fwd(name, x, gpre, gpost, wg, wu, wd, target=None, exchanges=()):
    t, d = x.shape
    nb, fs, _ = wg.shape
    sub = _token_block(t)
    tm = _step_rows(t)
    with_loss = target is not None

    def body(x_ref, gpre_ref, gpost_ref, wg_ref, wu_ref, wd_ref, *refs):
        if with_loss:
            t_ref, o_ref, a_ref, dup_ref, dgate_ref, f_ref, sq_ref = refs

            @pl.when(pl.program_id(0) == 0)
            def _():
                sq_ref[...] = jnp.zeros_like(sq_ref)
        else:
            o_ref, a_ref, dup_ref, dgate_ref, f_ref = refs
        for h in range(tm // sub):
            rows = slice(h * sub, (h + 1) * sub)
            xv = x_ref[rows, :]
            hb = _rms(xv, gpre_ref[...]).astype(BF16)
            f = jnp.zeros((sub, d), F32)
            for k in range(nb):
                g = _dot_nt(hb, wg_ref[k])
                u = _dot_nt(hb, wu_ref[k])
                s = jax.nn.sigmoid(g)
                sg = g * s
                a = (sg * u).astype(BF16)
                a_ref[k, rows, :] = a
                dup_ref[k, rows, :] = sg.astype(BF16)
                dgate_ref[k, rows, :] = (u * (s * (1.0 + g * (1.0 - s)))).astype(BF16)
                f = f + _dot(a, wd_ref[k])
            f_ref[rows, :] = f
            y = xv + _rms(f, gpost_ref[...])
            if with_loss:
                err = y - t_ref[rows, :]
                o_ref[rows, :] = err / d
                sq_ref[...] += jnp.sum(err * err)
            else:
                o_ref[rows, :] = y

    return _call(
        name,
        body,
        grid=(t // tm,),
        in_specs=[_rows(tm, d), _VM, _VM, _VM, _VM, _VM] + [_rows(tm, d)] * with_loss,
        out_specs=[_rows(tm, d)] + [_blocks(nb, tm, fs)] * 3 + [_rows(tm, d)] + [_VM] * with_loss,
        out_shape=[jax.ShapeDtypeStruct((t, d), F32)]
        + [jax.ShapeDtypeStruct((nb, t, fs), BF16)] * 3
        + [jax.ShapeDtypeStruct((t, d), F32)]
        + [jax.ShapeDtypeStruct((8, 128), F32)] * with_loss,
        args=[x, gpre, gpost, wg, wu, wd] + [target] * with_loss,
        exchanges=exchanges,
    )


def _ffn_bwd_hidden(name, dy, f, gpost, a, dup, dgate, wd, exchanges=()):
    t, d = dy.shape
    nb, fs, _ = wd.shape
    tm = _step_rows(t)

    def body(dy_ref, f_ref, gpost_ref, a_ref, dup_ref, dgate_ref, wd_ref, dg_ref, du_ref, dwd_ref, dgain_ref):
        @pl.when(pl.program_id(0) == 0)
        def _():
            dwd_ref[...] = jnp.zeros_like(dwd_ref)
            dgain_ref[...] = jnp.zeros_like(dgain_ref)

        df, dgain = _rms_bwd(dy_ref[...], f_ref[...], gpost_ref[...])
        dgain_ref[...] += dgain
        dfb = df.astype(BF16)
        for k in range(nb):
            da = _dot_nt(dfb, wd_ref[k])
            dwd_ref[k] += _dot_tn(a_ref[k], dfb)
            du_ref[k] = (da * dup_ref[k].astype(F32)).astype(BF16)
            dg_ref[k] = (da * dgate_ref[k].astype(F32)).astype(BF16)

    return _call(
        name,
        body,
        grid=(t // tm,),
        in_specs=[_rows(tm, d), _rows(tm, d), _VM] + [_blocks(nb, tm, fs)] * 3 + [_VM],
        out_specs=[_blocks(nb, tm, fs), _blocks(nb, tm, fs), _VM, _VM],
        out_shape=[
            jax.ShapeDtypeStruct((nb, t, fs), BF16),
            jax.ShapeDtypeStruct((nb, t, fs), BF16),
            jax.ShapeDtypeStruct((nb, fs, d), F32),
            jax.ShapeDtypeStruct((1, d), F32),
        ],
        args=[dy, f, gpost, a, dup, dgate, wd],
        exchanges=exchanges,
    )


def _bwd_in(name, dres, x, gpre, dzs, ws, transposed=False, exchanges=()):
    t, d = x.shape
    n = len(ws)
    resident = sum(6 * w.size for w in ws)
    tm = _step_rows(t) if resident <= VMEM_LIMIT_BYTES // 2 else _token_block(t)
    widths = [w.shape[1] if transposed else w.shape[2] for w in ws]

    def body(*refs):
        dres_ref, x_ref, gpre_ref = refs[:3]
        dz_refs = refs[3 : 3 + n]
        w_refs = refs[3 + n : 3 + 2 * n]
        dx_ref = refs[3 + 2 * n]
        dw_refs = refs[4 + 2 * n : 4 + 3 * n]
        dgain_ref = refs[4 + 3 * n]

        @pl.when(pl.program_id(0) == 0)
        def _():
            for dw_ref in dw_refs:
                dw_ref[...] = jnp.zeros_like(dw_ref)
            dgain_ref[...] = jnp.zeros_like(dgain_ref)

        xv = x_ref[...]
        gain = gpre_ref[...]
        hb = _rms(xv, gain).astype(BF16)
        dh = jnp.zeros((tm, d), F32)
        for dz_ref, w_ref, dw_ref in zip(dz_refs, w_refs, dw_refs):
            for k in range(w_ref.shape[0]):
                dz = dz_ref[k]
                if transposed:
                    dh = dh + _dot(dz, w_ref[k])
                    dw_ref[k] += _dot_tn(dz, hb)
                else:
                    dh = dh + _dot_nt(dz, w_ref[k])
                    dw_ref[k] += _dot_tn(hb, dz)
        dx, dgain = _rms_bwd(dh, xv, gain)
        dx_ref[...] = dres_ref[...] + dx
        dgain_ref[...] += dgain

    return _call(
        name,
        body,
        grid=(t // tm,),
        in_specs=[_rows(tm, d), _rows(tm, d), _VM]
        + [_blocks(w.shape[0], tm, bw) for w, bw in zip(ws, widths)]
        + [_VM] * n,
        out_specs=[_rows(tm, d)] + [_VM] * n + [_VM],
        out_shape=[jax.ShapeDtypeStruct((t, d), F32)]
        + [jax.ShapeDtypeStruct(w.shape, F32) for w in ws]
        + [jax.ShapeDtypeStruct((1, d), F32)],
        args=[dres, x, gpre, *dzs, *ws],
        exchanges=exchanges,
    )


def _causal_weights(ws_ref):
    row = lax.broadcasted_iota(jnp.int32, (CHUNK, CHUNK), 0)
    col = lax.broadcasted_iota(jnp.int32, (CHUNK, CHUNK), 1)
    return [jnp.where(row >= col, ws_ref[g], 0.0).astype(BF16) for g in range(A_GROUPS)]


def _layernorm_halves(v0, v1):
    width = v0.shape[-1] + v1.shape[-1]
    mu = (jnp.sum(v0, axis=-1, keepdims=True) + jnp.sum(v1, axis=-1, keepdims=True)) / width
    c0 = v0 - mu
    c1 = v1 - mu
    var = (jnp.sum(c0 * c0, axis=-1, keepdims=True) + jnp.sum(c1 * c1, axis=-1, keepdims=True)) / width
    rstd = lax.rsqrt(var + EPS)
    return c0 * rstd, c1 * rstd, rstd


def _spatial_gate(sv_ref, wtril, vl, bt_ref, half, tm, gd):
    for gg in range(A_GROUPS // 2):
        g = half * (A_GROUPS // 2) + gg
        bias = bt_ref[:, g : g + 1]
        for n in range(tm // CHUNK):
            blk = vl[n * CHUNK : (n + 1) * CHUNK, gg * gd : (gg + 1) * gd]
            sv_ref[n * CHUNK : (n + 1) * CHUNK, gg * gd : (gg + 1) * gd] = _dot(wtril[g], blk) + bias


def _mix_a_fwd(name, x, gpre, gpost, w_in, ln_g, ln_b, w_s, b_t, w_out, exchanges=()):
    t, d = x.shape
    _, _, q = w_in.shape
    gd = 2 * q // A_GROUPS
    tm = _step_rows(t)

    def body(x_ref, gpre_ref, gpost_ref, win_ref, lng_ref, lnb_ref, ws_ref, bt_ref, wout_ref,
             o_ref, z_ref, dz_ref, m_ref, sv_ref):
        xv = x_ref[...]
        hb = _rms(xv, gpre_ref[...]).astype(BF16)
        z = [None] * 4
        vls = []
        for k in (2, 3, 0, 1):
            act, slope = _gelu_and_grad(_dot(hb, win_ref[k]))
            z_ref[k] = act.astype(BF16)
            dz_ref[k] = slope.astype(BF16)
            z[k] = act
            if k == 3:
                vh0, vh1, _ = _layernorm_halves(z[2], z[3])
                vls = [(vh * lng_ref[b : b + 1, :] + lnb_ref[b : b + 1, :]).astype(BF16)
                       for b, vh in enumerate((vh0, vh1))]
        wtril = _causal_weights(ws_ref)
        m = jnp.zeros((tm, d), F32)
        for b in range(2):
            _spatial_gate(sv_ref, wtril, vls[b], bt_ref, b, tm, gd)
            gated = (z[b] * sv_ref[...]).astype(BF16)
            m = m + _dot(gated, wout_ref[b])
        m_ref[...] = m
        o_ref[...] = xv + _rms(m, gpost_ref[...])

    return _call(
        name,
        body,
        grid=(t // tm,),
        in_specs=[_rows(tm, d)] + [_VM] * 8,
        out_specs=[_rows(tm, d), _blocks(4, tm, q), _blocks(4, tm, q), _rows(tm, d)],
        out_shape=[
            jax.ShapeDtypeStruct((t, d), F32),
            jax.ShapeDtypeStruct((4, t, q), BF16),
            jax.ShapeDtypeStruct((4, t, q), BF16),
            jax.ShapeDtypeStruct((t, d), F32),
        ],
        scratch_shapes=[pltpu.VMEM((tm, q), F32)],
        args=[x, gpre, gpost, w_in, ln_g, ln_b, w_s, b_t, w_out],
        exchanges=exchanges,
    )


def _mix_a_bwd_hidden(name, dy, m, gpost, act, slope, ln_g, ln_b, w_s, b_t, w_out, exchanges=()):
    t, d = dy.shape
    _, _, q = act.shape
    gd = 2 * q // A_GROUPS
    tm = _step_rows(t)
    n_chunks = tm // CHUNK

    def body(dy_ref, m_ref, gpost_ref, z_ref, slope_ref, lng_ref, lnb_ref, ws_ref, bt_ref, wout_ref,
             dz_ref, dwout_ref, dws_ref, dbacc_ref, dlng_ref, dlnb_ref, dgain_ref, sv_ref, dvl_ref):
        first = pl.program_id(0) == 0

        @pl.when(first)
        def _():
            for ref in (dwout_ref, dws_ref, dbacc_ref, dlng_ref, dlnb_ref, dgain_ref):
                ref[...] = jnp.zeros_like(ref)

        dm, dgain = _rms_bwd(dy_ref[...], m_ref[...], gpost_ref[...])
        dgain_ref[...] += dgain
        dmb = dm.astype(BF16)
        vhs = list(_layernorm_halves(z_ref[2].astype(F32), z_ref[3].astype(F32)))
        rstd = vhs.pop()
        vls = [(vh * lng_ref[b : b + 1, :] + lnb_ref[b : b + 1, :]).astype(BF16) for b, vh in enumerate(vhs)]
        wtril = _causal_weights(ws_ref)
        dvhs = []
        for b in range(2):
            u = z_ref[b].astype(F32)
            _spatial_gate(sv_ref, wtril, vls[b], bt_ref, b, tm, gd)
            sv = sv_ref[...]
            gated = (u * sv).astype(BF16)
            dgated = _dot_nt(dmb, wout_ref[b])
            dwout_ref[b] += _dot_tn(gated, dmb)
            dz_ref[b] = (dgated * sv * slope_ref[b].astype(F32)).astype(BF16)
            dsv = dgated * u
            folded = dsv[0:CHUNK, :]
            for c in range(1, n_chunks):
                folded = folded + dsv[c * CHUNK : (c + 1) * CHUNK, :]
            for gg in range(A_GROUPS // 2):
                g = b * (A_GROUPS // 2) + gg
                dbacc_ref[:, g : g + 1] += jnp.sum(folded[:, gg * gd : (gg + 1) * gd], axis=1, keepdims=True)
            dsvb = dsv.astype(BF16)
            for gg in range(A_GROUPS // 2):
                g = b * (A_GROUPS // 2) + gg
                for c in range(n_chunks):
                    rows = slice(c * CHUNK, (c + 1) * CHUNK)
                    cols = slice(gg * gd, (gg + 1) * gd)
                    blk = dsvb[rows, cols]
                    dvl_ref[rows, cols] = _dot_tn(wtril[g], blk)
                    dws_ref[g] += _dot_nt(blk, vls[b][rows, cols])
            dvl = dvl_ref[...]
            dlng_ref[b : b + 1, :] += jnp.sum(dvl * vhs[b], axis=0, keepdims=True)
            dlnb_ref[b : b + 1, :] += jnp.sum(dvl, axis=0, keepdims=True)
            dvhs.append(dvl * lng_ref[b : b + 1, :])
        width = 2.0 * q
        m1 = (jnp.sum(dvhs[0], axis=-1, keepdims=True) + jnp.sum(dvhs[1], axis=-1, keepdims=True)) / width
        m2 = (jnp.sum(dvhs[0] * vhs[0], axis=-1, keepdims=True)
              + jnp.sum(dvhs[1] * vhs[1], axis=-1, keepdims=True)) / width
        for b in range(2):
            dv = rstd * (dvhs[b] - m1 - vhs[b] * m2)
            dz_ref[2 + b] = (dv * slope_ref[2 + b].astype(F32)).astype(BF16)

        @pl.when(pl.program_id(0) == t // tm - 1)
        def _():
            row = lax.broadcasted_iota(jnp.int32, (CHUNK, CHUNK), 0)
            col = lax.broadcasted_iota(jnp.int32, (CHUNK, CHUNK), 1)
            for g in range(A_GROUPS):
                dws_ref[g] = jnp.where(row >= col, dws_ref[g], 0.0)

    return _call(
        name,
        body,
        grid=(t // tm,),
        in_specs=[_rows(tm, d), _rows(tm, d), _VM, _blocks(4, tm, q), _blocks(4, tm, q)] + [_VM] * 5,
        out_specs=[_blocks(4, tm, q)] + [_VM] * 6,
        out_shape=[
            jax.ShapeDtypeStruct((4, t, q), BF16),
            jax.ShapeDtypeStruct((2, q, d), F32),
            jax.ShapeDtypeStruct((A_GROUPS, CHUNK, CHUNK), F32),
            jax.ShapeDtypeStruct((CHUNK, A_GROUPS), F32),
            jax.ShapeDtypeStruct((2, q), F32),
            jax.ShapeDtypeStruct((2, q), F32),
            jax.ShapeDtypeStruct((1, d), F32),
        ],
        scratch_shapes=[pltpu.VMEM((tm, q), F32), pltpu.VMEM((tm, q), F32)],
        args=[dy, m, gpost, act, slope, ln_g, ln_b, w_s, b_t, w_out],
        exchanges=exchanges,
    )


def _window_counts(tm, win):
    pos = pl.program_id(0) * tm + lax.broadcasted_iota(jnp.int32, (tm, 1), 0)
    return jnp.minimum(pos + 1, win).astype(F32)


def _pooled(p, halo, tm, gd, inside=False):
    prev = halo if inside else jnp.where(pl.program_id(0) == 0, 0.0, halo)
    ext = jnp.concatenate([prev, p], axis=0)
    out = []
    for g, win in enumerate(B_WINDOWS):
        s = ext[:, g * gd : (g + 1) * gd]
        step = 1
        while step < win:
            s = s + pltpu.roll(s, step, 0)
            step *= 2
        total = s[HALO:, :]
        count = float(win) if inside else _window_counts(tm, win)
        out.append(total / count - p[:, g * gd : (g + 1) * gd])
    return out


def _halo_spec(t, tm, d, ahead):
    per = tm // HALO
    if ahead:
        return pl.BlockSpec((HALO, d), lambda i: (jnp.minimum((i + 1) * per, t // HALO - 1), 0))
    return pl.BlockSpec((HALO, d), lambda i: (jnp.maximum(i * per - 1, 0), 0))


def _mix_b_fwd(name, x, gpre, gpost, w_in, w_grp, scale, w_out, exchanges=()):
    t, d = x.shape
    gd = d // len(B_WINDOWS)
    tm = _step_rows(t)

    def body(x_ref, xh_ref, gpre_ref, gpost_ref, win_ref, wgrp_ref, scale_ref, wout_ref, o_ref, p_ref, m_ref):
        xv = x_ref[...]
        gain = gpre_ref[...]
        p = _dot(_rms(xv, gain).astype(BF16), win_ref[...])
        p_ref[...] = p
        halo = _dot(_rms(xh_ref[...], gain).astype(BF16), win_ref[...])
        pooled = _pooled(p, halo, tm, gd)
        mixed = jnp.concatenate([_dot(pg.astype(BF16), wgrp_ref[g]) for g, pg in enumerate(pooled)], axis=1)
        m = _dot((mixed * scale_ref[...]).astype(BF16), wout_ref[...])
        m_ref[...] = m
        o_ref[...] = xv + _rms(m, gpost_ref[...])

    return _call(
        name,
        body,
        grid=(t // tm,),
        in_specs=[_rows(tm, d), _halo_spec(t, tm, d, False), _VM, _VM, _VM, _VM, _VM, _VM],
        out_specs=[_rows(tm, d)] * 3,
        out_shape=[jax.ShapeDtypeStruct((t, d), F32)] * 3,
        args=[x, x, gpre, gpost, w_in, w_grp, scale, w_out],
        exchanges=exchanges,
    )


def _mix_b_bwd(name, dy, x, m, p, gpre, gpost, w_in, w_grp, scale, w_out, exchanges=()):
    t, d = dy.shape
    gd = d // len(B_WINDOWS)
    tm = _step_rows(t)
    n_steps = t // tm

    def body(dy_ref, x_ref, m_ref, p_ref, pprev_ref, dynext_ref, mnext_ref, pnext_ref,
             gpre_ref, gpost_ref, win_ref, wgrp_ref, scale_ref, wout_ref,
             dx_ref, dwout_ref, dwgrp_ref, dscale_ref, dpost_ref, dwin_ref, dpre_ref):
        @pl.when(pl.program_id(0) == 0)
        def _():
            for ref in (dwout_ref, dwgrp_ref, dscale_ref, dpost_ref, dwin_ref, dpre_ref):
                ref[...] = jnp.zeros_like(ref)

        scale = scale_ref[...]

        def tail_bwd(dy_rows, m_rows, pooled, counts, accumulate):
            dm, dgain = _rms_bwd(dy_rows, m_rows, gpost_ref[...])
            dmb = dm.astype(BF16)
            pooled = [pg.astype(BF16) for pg in pooled]
            mixed = jnp.concatenate([_dot(pg, wgrp_ref[g]) for g, pg in enumerate(pooled)], axis=1)
            dms = _dot_nt(dmb, wout_ref[...])
            dmixed = (dms * scale).astype(BF16)
            if accumulate:
                dpost_ref[...] += dgain
                dwout_ref[...] += _dot_tn((mixed * scale).astype(BF16), dmb)
                dscale_ref[...] += jnp.sum(dms * mixed, axis=0, keepdims=True)
            parts = []
            for g, win in enumerate(B_WINDOWS):
                dmg = dmixed[:, g * gd : (g + 1) * gd]
                if accumulate:
                    dwgrp_ref[g] += _dot_tn(pooled[g], dmg)
                parts.append(_dot_nt(dmg, wgrp_ref[g]) / counts(win))
            return jnp.concatenate(parts, axis=1)

        dyv = dy_ref[...]
        pv = p_ref[...]
        dq_blk = tail_bwd(dyv, m_ref[...], _pooled(pv, pprev_ref[...], tm, gd),
                          lambda win: _window_counts(tm, win), True)
        dq_next = tail_bwd(dynext_ref[...], mnext_ref[...],
                           _pooled(pnext_ref[...], pv[tm - HALO :, :], HALO, gd, inside=True),
                           float, False)
        dq_next = jnp.where(pl.program_id(0) == n_steps - 1, 0.0, dq_next)
        ext = jnp.concatenate([dq_blk, dq_next], axis=0)
        parts = []
        for g, win in enumerate(B_WINDOWS):
            cols = slice(g * gd, (g + 1) * gd)
            s = ext[:, cols]
            step = 1
            while step < win:
                s = s + pltpu.roll(s, tm + HALO - step, 0)
                step *= 2
            parts.append((s[:tm, :] - dq_blk[:, cols] * _window_counts(tm, win)).astype(BF16))
        dp = jnp.concatenate(parts, axis=1)
        xv = x_ref[...]
        gain = gpre_ref[...]
        hb = _rms(xv, gain).astype(BF16)
        dwin_ref[...] += _dot_tn(hb, dp)
        dx, dgain = _rms_bwd(_dot_nt(dp, win_ref[...]), xv, gain)
        dx_ref[...] = dyv + dx
        dpre_ref[...] += dgain

    before, after = _halo_spec(t, tm, d, False), _halo_spec(t, tm, d, True)
    return _call(
        name,
        body,
        grid=(n_steps,),
        in_specs=[_rows(tm, d)] * 4 + [before, after, after, after] + [_VM] * 6,
        out_specs=[_rows(tm, d)] + [_VM] * 6,
        out_shape=[
            jax.ShapeDtypeStruct((t, d), F32),
            jax.ShapeDtypeStruct((d, d), F32),
            jax.ShapeDtypeStruct((len(B_WINDOWS), gd, gd), F32),
            jax.ShapeDtypeStruct((1, d), F32),
            jax.ShapeDtypeStruct((1, d), F32),
            jax.ShapeDtypeStruct((d, d), F32),
            jax.ShapeDtypeStruct((1, d), F32),
        ],
        args=[dy, x, m, p, p, dy, m, p, gpre, gpost, w_in, w_grp, scale, w_out],
        exchanges=exchanges,
    )


def _cast_into_slots(name, place, w, dtype):
    n_layers, r, c = w.shape

    def body(place_ref, w_ref, *o_refs):
        del place_ref
        for j, o_ref in enumerate(o_refs):
            @pl.when(pl.program_id(0) == j)
            def _():
                o_ref[...] = w_ref[...].astype(dtype)

    return pl.pallas_call(
        body,
        name=name,
        grid_spec=pltpu.PrefetchScalarGridSpec(
            num_scalar_prefetch=1,
            grid=(n_layers,),
            in_specs=[pl.BlockSpec((1, r, c), lambda i, place_ref: (i, 0, 0))],
            out_specs=[pl.BlockSpec((1, r, c), lambda i, place_ref: (place_ref[0], 0, 0))] * n_layers,
        ),
        out_shape=[jax.ShapeDtypeStruct((N_CHIPS, r, c), dtype)] * n_layers,
        compiler_params=_params(),
    )(place, w)


def _row_tile(r):
    return 256 if r % 256 == 0 else r


def _pair_sum(name, place, dw, recv):
    _, r, c = dw.shape
    half = r // 2
    tr = _row_tile(half)
    per = half // tr

    def body(place_ref, a_ref, b_ref, o_ref):
        del place_ref
        o_ref[...] = (a_ref[...] + b_ref[...]).astype(BF16)

    return pl.pallas_call(
        body,
        name=name,
        grid_spec=pltpu.PrefetchScalarGridSpec(
            num_scalar_prefetch=1,
            grid=(N_CHIPS, per),
            in_specs=[
                pl.BlockSpec((1, tr, c), lambda k, i, place_ref: (k, place_ref[1] * per + i, 0)),
                pl.BlockSpec((1, tr, c), lambda k, i, place_ref: (k, i, 0)),
            ],
            out_specs=pl.BlockSpec((1, tr, c), lambda k, i, place_ref: (k, i, 0)),
        ),
        out_shape=jax.ShapeDtypeStruct(recv.shape, BF16),
        compiler_params=pltpu.CompilerParams(
            dimension_semantics=("arbitrary",) * 2, vmem_limit_bytes=VMEM_LIMIT_BYTES),
    )(place, dw, recv)


def _chip_sum(name, place, mine, others, exchanges=()):
    n_layers = len(mine)
    _, half, c = mine[0].shape
    tr = _row_tile(half)
    per = half // tr

    def body(place_ref, *refs):
        del place_ref
        o_ref = refs[-1]
        for j in range(n_layers):
            @pl.when(pl.program_id(0) == j)
            def _():
                parts = refs[4 * j : 4 * j + 4]
                acc = parts[0][...].astype(F32) + parts[1][...].astype(F32)
                acc = acc + parts[2][...].astype(F32)
                o_ref[...] = acc + parts[3][...].astype(F32)

    def part(j, flip):
        return pl.BlockSpec((1, tr, c), lambda l, i, place_ref: (
            jnp.bitwise_xor(place_ref[0], flip), jnp.where(l == j, i, 0), 0))

    args = []
    for j in range(n_layers):
        args += [mine[j], others[j], others[j], others[j]]
    (total,), ex_outs = _call(
        name,
        body,
        grid=(n_layers, per),
        in_specs=[part(j, flip) for j in range(n_layers) for flip in range(N_CHIPS)],
        out_specs=[pl.BlockSpec((1, tr, c), lambda l, i, place_ref: (l, place_ref[1] * per + i, 0))],
        out_shape=[jax.ShapeDtypeStruct((n_layers, 2 * half, c), F32)],
        args=args,
        prefetch=[place],
        exchanges=exchanges,
    )
    return total, ex_outs


def _adamw(name, w, g, m, v, exchanges=()):
    n_layers, r, c = w.shape
    tr = _row_tile(r)

    def body(w_ref, g_ref, m_ref, v_ref, go_ref, d_ref, nm_ref, nv_ref):
        gv = g_ref[...]
        go_ref[...] = gv
        nm = ADAM_B1 * m_ref[...] + (1.0 - ADAM_B1) * gv
        nv = ADAM_B2 * v_ref[...] + (1.0 - ADAM_B2) * jnp.square(gv)
        m_hat = nm / (1.0 - ADAM_B1 ** ADAM_STEP)
        v_hat = nv / (1.0 - ADAM_B2 ** ADAM_STEP)
        d_ref[...] = -ADAM_LR * (m_hat / (jnp.sqrt(v_hat) + ADAM_EPS) + ADAM_WD * w_ref[...])
        nm_ref[...] = nm
        nv_ref[...] = nv

    spec = pl.BlockSpec((1, tr, c), lambda l, i: (l, i, 0))
    return _call(
        name,
        body,
        grid=(n_layers, r // tr),
        in_specs=[spec] * 4,
        out_specs=[spec] * 4,
        out_shape=[jax.ShapeDtypeStruct(w.shape, F32)] * 4,
        args=[w, g, m, v],
        exchanges=exchanges,
    )


class _GatherSmall:
    def __init__(self, gathered):
        self.inputs = [gathered]
        self.out_shapes = [jax.ShapeDtypeStruct(gathered.shape, gathered.dtype)]
        self.aliases = {0: 0}
        self.n_sems = 4

    def _copies(self, outs, send, recv, outbound):
        x, y, c = _position()
        peers = [(x, y, 1 - c)] + [(px, py, c) for px, py in _other_chips(x, y)]
        copies = []
        for k, (px, py, pc) in enumerate(peers):
            rows = outs[0].at[4 * x + 2 * y + c if outbound else 4 * px + 2 * py + pc]
            copies.append(pltpu.make_async_remote_copy(
                src_ref=rows, dst_ref=rows, send_sem=send.at[k], recv_sem=recv.at[k],
                device_id=(px, py, pc), device_id_type=MESH))
        return copies

    def start(self, ins, outs, send, recv):
        for cp in self._copies(outs, send, recv, True):
            cp.start()

    def finish(self, ins, outs, send, recv):
        for cp in self._copies(outs, send, recv, False):
            cp.wait_recv()
        for cp in self._copies(outs, send, recv, True):
            cp.wait_send()


class _GatherSmallForward:
    def __init__(self, gathered):
        self.inputs = [gathered]
        self.out_shapes = [jax.ShapeDtypeStruct(gathered.shape, gathered.dtype)]
        self.aliases = {0: 0}
        self.n_sems = 3

    def _copies(self, outs, send, recv, core):
        x, y, c = _position()
        copies = []
        for k, (px, py) in enumerate(_other_chips(x, y)):
            rows = outs[0].at[4 * px + 2 * py + core]
            copies.append(pltpu.make_async_remote_copy(
                src_ref=rows, dst_ref=rows, send_sem=send.at[k], recv_sem=recv.at[k],
                device_id=(x, y, 1 - c), device_id_type=MESH))
        return copies

    def start(self, ins, outs, send, recv):
        for cp in self._copies(outs, send, recv, lax.axis_index("c")):
            cp.start()

    def finish(self, ins, outs, send, recv):
        c = lax.axis_index("c")
        for cp in self._copies(outs, send, recv, 1 - c):
            cp.wait_recv()
        for cp in self._copies(outs, send, recv, c):
            cp.wait_send()


def _sum_devices(name, gathered):
    _, m_per, n = gathered.shape

    def body(all_ref, sum_ref):
        acc = all_ref[0]
        for k in range(1, N_DEV):
            acc = acc + all_ref[k]
        sum_ref[...] = acc

    return pl.pallas_call(
        body,
        name=name,
        in_specs=[_VM],
        out_specs=_VM,
        out_shape=jax.ShapeDtypeStruct((m_per, n), F32),
        compiler_params=pltpu.CompilerParams(vmem_limit_bytes=VMEM_LIMIT_BYTES),
    )(gathered)


SHARDED = ("a_w_in", "a_w_out", "b_w_in", "b_w_grp", "b_scale", "b_w_out", "ffn_w_gate", "ffn_w_up", "ffn_w_down")
SMALL = ("a_ln_g", "a_ln_b", "a_w_s", "a_b_s", "mix_pre_g", "mix_post_g", "ffn_pre_g", "ffn_post_g")
WEIGHTS = ("a_w_in", "a_ln_g", "a_ln_b", "a_w_s", "a_b_s", "a_w_out", "b_w_in", "b_w_grp", "b_scale", "b_w_out",
           "mix_pre_g", "mix_post_g", "ffn_pre_g", "ffn_post_g", "ffn_w_gate", "ffn_w_up", "ffn_w_down")


TRANSPOSED = ("ffn_w_gate", "ffn_w_up")


def _as_layers(name, a):
    if name in TRANSPOSED:
        return jnp.swapaxes(a, 1, 2)
    if a.ndim == 2:
        return a.reshape(a.shape[0], 1, a.shape[1])
    return a.reshape(a.shape[0], -1, a.shape[-1])


def _from_layers(name, a, shape):
    if name in TRANSPOSED:
        return jnp.swapaxes(a, 1, 2)
    return a.reshape(shape)


def _pack_small(parts):
    return jnp.concatenate([p.reshape(-1, 128) for p in parts], axis=0)


def _unpack_small(packed, like):
    out, row = [], 0
    for ref in like:
        rows = ref.size // 128
        out.append(packed[row : row + rows].reshape(ref.shape))
        row += rows
    return out


def kernel(x, a_w_in, a_ln_g, a_ln_b, a_w_s, a_b_s, a_w_out, b_w_in, b_w_grp, b_scale, b_w_out, mix_pre_g, mix_post_g, ffn_pre_g, ffn_post_g, ffn_w_gate, ffn_w_up, ffn_w_down, loss_target, m_a_w_in, m_a_ln_g, m_a_ln_b, m_a_w_s, m_a_b_s, m_a_w_out, m_b_w_in, m_b_w_grp, m_b_scale, m_b_w_out, m_mix_pre_g, m_mix_post_g, m_ffn_pre_g, m_ffn_post_g, m_ffn_w_gate, m_ffn_w_up, m_ffn_w_down, v_a_w_in, v_a_ln_g, v_a_ln_b, v_a_w_s, v_a_b_s, v_a_w_out, v_b_w_in, v_b_w_grp, v_b_scale, v_b_w_out, v_mix_pre_g, v_mix_post_g, v_ffn_pre_g, v_ffn_post_g, v_ffn_w_gate, v_ffn_w_up, v_ffn_w_down):
    weights = dict(a_w_in=a_w_in, a_ln_g=a_ln_g, a_ln_b=a_ln_b, a_w_s=a_w_s, a_b_s=a_b_s, a_w_out=a_w_out,
                   b_w_in=b_w_in, b_w_grp=b_w_grp, b_scale=b_scale, b_w_out=b_w_out, mix_pre_g=mix_pre_g,
                   mix_post_g=mix_post_g, ffn_pre_g=ffn_pre_g, ffn_post_g=ffn_post_g, ffn_w_gate=ffn_w_gate,
                   ffn_w_up=ffn_w_up, ffn_w_down=ffn_w_down)
    mom1 = dict(a_w_in=m_a_w_in, a_ln_g=m_a_ln_g, a_ln_b=m_a_ln_b, a_w_s=m_a_w_s, a_b_s=m_a_b_s, a_w_out=m_a_w_out,
                b_w_in=m_b_w_in, b_w_grp=m_b_w_grp, b_scale=m_b_scale, b_w_out=m_b_w_out, mix_pre_g=m_mix_pre_g,
                mix_post_g=m_mix_post_g, ffn_pre_g=m_ffn_pre_g, ffn_post_g=m_ffn_post_g, ffn_w_gate=m_ffn_w_gate,
                ffn_w_up=m_ffn_w_up, ffn_w_down=m_ffn_w_down)
    mom2 = dict(a_w_in=v_a_w_in, a_ln_g=v_a_ln_g, a_ln_b=v_a_ln_b, a_w_s=v_a_w_s, a_b_s=v_a_b_s, a_w_out=v_a_w_out,
                b_w_in=v_b_w_in, b_w_grp=v_b_w_grp, b_scale=v_b_scale, b_w_out=v_b_w_out, mix_pre_g=v_mix_pre_g,
                mix_post_g=v_mix_post_g, ffn_pre_g=v_ffn_pre_g, ffn_post_g=v_ffn_post_g, ffn_w_gate=v_ffn_w_gate,
                ffn_w_up=v_ffn_w_up, ffn_w_down=v_ffn_w_down)

    t, d = x.shape[1], x.shape[2]
    depth = mix_pre_g.shape[0]
    gd_b = d // len(B_WINDOWS)
    xs = x.reshape(t, d)
    target = loss_target.reshape(t, d)

    chip = 2 * lax.axis_index("x") + lax.axis_index("y")
    place = jnp.stack([chip, lax.axis_index("c")]).astype(jnp.int32)
    bufs = {name: list(_cast_into_slots("cast_" + name, place, _as_layers(name, weights[name]),
                                        F32 if name == "b_scale" else BF16)) for name in SHARDED}

    def gain(name, i):
        return weights[name][i].reshape(1, d)

    def weight_keys(i):
        j = i // 2
        mixer = [("a_w_in", j), ("a_w_out", j)] if i % 2 == 0 else [("b_w_in", j), ("b_w_grp", j), ("b_w_out", j)]
        return mixer, [("ffn_w_gate", i), ("ffn_w_up", i), ("ffn_w_down", i)]

    def gather(keys):
        return _GatherWeights([bufs[n][j] for n, j in keys],
                              whole=[k for k, (n, _) in enumerate(keys) if n == "b_scale"])

    def gathered(keys, outs):
        for (n, j), buf in zip(keys, outs):
            bufs[n][j] = buf

    first = weight_keys(0)[0] + [("b_scale", j) for j in range(b_scale.shape[0])]
    gathered(first, _exchange("gather_first", [gather(first)])[0])
    saved = []
    cur = xs
    for i in range(depth):
        j = i // 2
        mixer_next, ffn_next = weight_keys(i + 1) if i + 1 < depth else ([], [])
        ffn_keys = weight_keys(i)[1]
        after_ffn = mixer_next if (i + 1) % 2 == 0 else mixer_next + ffn_next
        if i % 2 == 0:
            w_in = bufs["a_w_in"][j]
            q = w_in.shape[2]
            w_out = bufs["a_w_out"][j].reshape(2, q, d)
            ln_g = a_ln_g[j].reshape(2, q)
            ln_b = a_ln_b[j].reshape(2, q)
            b_t = jnp.transpose(a_b_s[j])
            (nxt, act, slope, m), (got,) = _mix_a_fwd(
                f"mix_a_fwd{j}", cur, gain("mix_pre_g", i), gain("mix_post_g", i), w_in, ln_g, ln_b, a_w_s[j], b_t,
                w_out, exchanges=[gather(ffn_keys)])
            gathered(ffn_keys, got)
            mix_saved = dict(x=cur, act=act, slope=slope, m=m, w_in=w_in, w_out=w_out, ln_g=ln_g, ln_b=ln_b, b_t=b_t)
        else:
            w_in = bufs["b_w_in"][j].reshape(d, d)
            w_out = bufs["b_w_out"][j].reshape(d, d)
            w_grp = jnp.transpose(bufs["b_w_grp"][j].reshape(N_CHIPS, len(B_WINDOWS), gd_b // N_CHIPS, gd_b),
                                  (1, 0, 2, 3)).reshape(len(B_WINDOWS), gd_b, gd_b)
            scale = bufs["b_scale"][j].reshape(1, d)
            (nxt, p, m), _ = _mix_b_fwd(f"mix_b_fwd{j}", cur, gain("mix_pre_g", i), gain("mix_post_g", i),
                                        w_in, w_grp, scale, w_out)
            mix_saved = dict(x=cur, p=p, m=m, w_in=w_in, w_out=w_out, w_grp=w_grp, scale=scale)
        cur = nxt
        wg, wu, wd = (bufs[n][k].reshape(1, -1, d) for n, k in ffn_keys)
        (nxt, a, dup, dgate, f, *sq), (got,) = _ffn_fwd(
            f"ffn_fwd{i}", cur, gain("ffn_pre_g", i), gain("ffn_post_g", i), wg, wu, wd,
            target=target if i == depth - 1 else None, exchanges=[gather(after_ffn)])
        gathered(after_ffn, got)
        saved.append((mix_saved, dict(x=cur, a=a, dup=dup, dgate=dgate, f=f, wg=wg, wu=wu, wd=wd)))
        cur = nxt

    dcur = cur
    loss = lax.psum(0.5 * sq[0][0, 0] / d, ("x", "y", "c"))

    grads = {name: [None] * weights[name].shape[0] for name in WEIGHTS}
    state = dict(to_sibling=[], to_chips=[])
    pair, from_chips = {}, {}

    def exchanges_due(carry=True):
        if not carry:
            return []
        return [_ToSibling([a for _, _, a in state["to_sibling"]]), _ToChips([a for _, _, a in state["to_chips"]])]

    def exchanged(outs):
        if not outs:
            return
        from_sibling, arrived = outs
        for (n, k, _), got in zip(state["to_chips"], arrived):
            from_chips[n, k] = got
        state["to_chips"] = []
        for (n, k, dw), got in zip(state["to_sibling"], from_sibling):
            pair[n, k] = _pair_sum(f"pair_sum_{n}{k}", place, dw, got)
            state["to_chips"].append((n, k, pair[n, k]))
        state["to_sibling"] = []

    def made(name, k, dw):
        grads[name][k] = dw
        state["to_sibling"].append((name, k, dw))

    def own_slot(packed):
        return lax.dynamic_update_slice(jnp.zeros((N_DEV,) + packed.shape, F32), packed[None],
                                        (2 * chip + place[1], 0, 0))

    for i in reversed(range(depth)):
        j = i // 2
        mix_saved, ffn_saved = saved[i]
        s = ffn_saved
        both = i == 0
        (dg, du, dwd, dgain), outs = _ffn_bwd_hidden(
            f"ffn_bwd_hidden{i}", dcur, s["f"], gain("ffn_post_g", i), s["a"], s["dup"], s["dgate"], s["wd"],
            exchanges=exchanges_due(both))
        exchanged(outs)
        grads["ffn_post_g"][i] = dgain
        made("ffn_w_down", i, dwd.reshape(N_CHIPS, -1, d))
        (dcur, dwg, dgain_gate), outs = _bwd_in(
            f"ffn_bwd_gate{i}", dcur, s["x"], gain("ffn_pre_g", i), [dg], [s["wg"]], transposed=True,
            exchanges=exchanges_due())
        exchanged(outs)
        (dcur, dwu, dgain_up), _ = _bwd_in(
            f"ffn_bwd_up{i}", dcur, s["x"], gain("ffn_pre_g", i), [du], [s["wu"]], transposed=True)
        grads["ffn_pre_g"][i] = dgain_gate + dgain_up
        made("ffn_w_gate", i, dwg.reshape(N_CHIPS, -1, d))
        made("ffn_w_up", i, dwu.reshape(N_CHIPS, -1, d))
        s = mix_saved
        if i % 2 == 0:
            (dz, dwout, dws, dbacc, dlng, dlnb, dgain), outs = _mix_a_bwd_hidden(
                f"mix_a_bwd_hidden{j}", dcur, s["m"], gain("mix_post_g", i), s["act"], s["slope"], s["ln_g"], s["ln_b"],
                a_w_s[j], s["b_t"], s["w_out"], exchanges=exchanges_due(both))
            exchanged(outs)
            grads["a_w_s"][j] = dws
            grads["a_b_s"][j] = jnp.transpose(dbacc)
            grads["a_ln_g"][j] = dlng.reshape(-1)
            grads["a_ln_b"][j] = dlnb.reshape(-1)
            grads["mix_post_g"][i] = dgain
            made("a_w_out", j, dwout.reshape(N_CHIPS, -1, d))
            riders = []
            if i == 0:
                grads["mix_pre_g"][0] = jnp.zeros((1, d), F32)
                small_grads = [jnp.stack([g.reshape(weights[name].shape[1:]) for g in grads[name]], axis=0)
                               for name in SMALL]
                scale_grad = jnp.concatenate(grads["b_scale"], axis=0)
                riders = [_GatherSmall(own_slot(_pack_small(small_grads + [scale_grad])))]
            (dcur, dwin, dgain), outs = _bwd_in(
                f"mix_a_bwd_in{j}", dcur, s["x"], gain("mix_pre_g", i), [dz], [s["w_in"]],
                exchanges=exchanges_due() + riders)
            exchanged(outs[:2])
            gathered_small = outs[2][0] if riders else None
            grads["mix_pre_g"][i] = dgain
            made("a_w_in", j, dwin)
        else:
            (dcur, dwout, dwgrp, dscale, dpost, dwin, dpre), outs = _mix_b_bwd(
                f"mix_b_bwd{j}", dcur, s["x"], s["m"], s["p"], gain("mix_pre_g", i), gain("mix_post_g", i),
                s["w_in"], s["w_grp"], s["scale"], s["w_out"], exchanges=exchanges_due())
            exchanged(outs)
            grads["b_scale"][j] = dscale
            grads["mix_post_g"][i] = dpost
            grads["mix_pre_g"][i] = dpre
            made("b_w_out", j, dwout.reshape(N_CHIPS, -1, d))
            made("b_w_grp", j, jnp.transpose(
                dwgrp.reshape(len(B_WINDOWS), N_CHIPS, gd_b // N_CHIPS, gd_b), (1, 0, 2, 3)).reshape(N_CHIPS, -1, gd_b))
            made("b_w_in", j, dwin.reshape(N_CHIPS, -1, d))
    grad_x = dcur.reshape(x.shape)

    outs = _exchange("grads_last", exchanges_due() + [_GatherSmallForward(gathered_small),
                                                      _GatherSmall(own_slot(_pack_small([grads["mix_pre_g"][0]])))])
    exchanged(outs[:2])
    gathered_small = outs[2][0]
    outs = _exchange("grads_last_to_chips", exchanges_due() + [_GatherSmallForward(outs[3][0])])
    exchanged(outs[:2])
    gathered_last = outs[2][0]

    reduced_names = [name for name in SHARDED if name != "b_scale"]
    sums = []
    for name in reduced_names:
        layers = range(weights[name].shape[0])
        sums.append(_chip_sum("chip_sum_" + name, place, [pair[name, k] for k in layers],
                              [from_chips[name, k] for k in layers])[0])
    reduced = dict(zip(reduced_names, _exchange("swap_halves", [_SwapHalves(sums)])[0]))

    out_g, out_d, out_m, out_v = {}, {}, {}, {}
    for name in reduced_names:
        shape = weights[name].shape
        results, _ = _adamw("adamw_" + name, _as_layers(name, weights[name]), reduced[name],
                            _as_layers(name, mom1[name]), _as_layers(name, mom2[name]))
        out_g[name], out_d[name], out_m[name], out_v[name] = (_from_layers(name, a, shape) for a in results)

    summed = _sum_devices("sum_devices", gathered_small)
    last_rows = sum(weights[name].size // 128 for name in SMALL[: SMALL.index("mix_pre_g")])
    summed = lax.dynamic_update_slice(summed, _sum_devices("sum_devices_last", gathered_last), (last_rows, 0))
    small_rows = summed.shape[0] - scale_grad.size // 128
    scale_sum = summed[small_rows:].reshape(scale_grad.shape)
    scale_mine = lax.dynamic_slice_in_dim(scale_sum, chip * b_scale.shape[1], b_scale.shape[1], axis=1)[:, None, :]
    summed = summed[:small_rows]
    results, _ = _adamw("adamw_b_scale", _as_layers("b_scale", b_scale), scale_mine,
                        _as_layers("b_scale", m_b_scale), _as_layers("b_scale", v_b_scale))
    out_g["b_scale"], out_d["b_scale"], out_m["b_scale"], out_v["b_scale"] = (a.reshape(b_scale.shape) for a in results)
    small_like = [weights[name] for name in SMALL]
    packs = [_pack_small([src[name] for name in SMALL]).reshape(1, -1, 128) for src in (weights, mom1, mom2)]
    results, _ = _adamw("adamw_small", packs[0], summed.reshape(1, -1, 128), packs[1], packs[2])
    for dst, packed in zip((out_g, out_d, out_m, out_v), (a[0] for a in results)):
        for name, val in zip(SMALL, _unpack_small(packed, small_like)):
            dst[name] = val

    return (loss, grad_x, *[out_g[n] for n in WEIGHTS], *[out_d[n] for n in WEIGHTS],
            *[out_m[n] for n in WEIGHTS], *[out_v[n] for n in WEIGHTS])
```
